```python
import math
import jax, jax.numpy as jnp
from jax import lax
import numpy as np

D_MODEL = 2048
BATCH = 8
SEQ = 4096
DEPTH = 1

GRID_W = 64
CTX_LEN = 256
N_ADA = 9
D_FF = 5632
SSM_WIDTH = 1024
SSM_GROUP = 16
SSM_GROUPS = SSM_WIDTH // SSM_GROUP
SSM_STATE = 64
RET_HEADS = 8
RET_DK = 128
RET_DV = 256
RET_QK_WIDTH = RET_HEADS * RET_DK
RET_V_WIDTH = RET_HEADS * RET_DV
RET_CHUNK = 128
ROPE_BASE = 10000.0
NORM_EPS = 1e-6
MIX_SPLITS = (SSM_WIDTH, RET_QK_WIDTH, RET_QK_WIDTH, RET_V_WIDTH, RET_V_WIDTH, D_MODEL, D_MODEL)
MIX_IN_WIDTH = sum(MIX_SPLITS)

kernel_name = "hybrid_s5_retention_macaron_dit_layer"


def rmsnorm(h, g):
    hf = h.astype(jnp.float32)
    hf = hf * lax.rsqrt(jnp.mean(hf * hf, axis=-1, keepdims=True) + NORM_EPS)
    return (hf * g.astype(jnp.float32)).astype(h.dtype)


def head_rmsnorm(o):
    of = o.astype(jnp.float32)
    return of * lax.rsqrt(jnp.mean(of * of, axis=-1, keepdims=True) + NORM_EPS)


def ada_pre(h, ada, i, g):
    return rmsnorm(h, g) * (1.0 + ada[:, :, 3 * i + 1]) + ada[:, :, 3 * i]


def ada_post(h, out, ada, i, g, res_w):
    return (h + res_w * ada[:, :, 3 * i + 2] * rmsnorm(out, g)).astype(h.dtype)


def swiglu(h, w_in, w_out):
    gt, up = jnp.split(h @ w_in, 2, axis=-1)
    return (jax.nn.silu(gt) * up) @ w_out


def _flip(t, axis, rev):
    return jnp.flip(t, axis=axis) if rev else t


def rope_1d(t, pos):
    half = t.shape[-1] // 2
    inv = ROPE_BASE ** (-jnp.arange(half, dtype=jnp.float32) / half)
    ang = pos.astype(jnp.float32)[:, None] * inv
    cos = jnp.cos(ang).astype(t.dtype)
    sin = jnp.sin(ang).astype(t.dtype)
    t1, t2 = t[..., :half], t[..., half:]
    return jnp.concatenate([t1 * cos - t2 * sin, t1 * sin + t2 * cos], axis=-1)


def rope_2d(t, row, col):
    half = t.shape[-1] // 2
    return jnp.concatenate([rope_1d(t[..., :half], row), rope_1d(t[..., half:], col)], axis=-1)


def split_heads(t, dh):
    b, l, _ = t.shape
    return t.reshape(b, l, -1, dh).transpose(0, 2, 1, 3)


def merge_heads(o):
    b, h, l, dh = o.shape
    return o.transpose(0, 2, 1, 3).reshape(b, l, h * dh)


def complex_affine_combine(e1, e2):
    a1r, a1i, b1r, b1i = e1
    a2r, a2i, b2r, b2i = e2
    ar = a2r * a1r - a2i * a1i
    ai = a2r * a1i + a2i * a1r
    br = a2r * b1r - a2i * b1i + b2r
    bi = a2r * b1i + a2i * b1r + b2i
    return ar, ai, br, bi


def s5_scan(u, lam_re, lam_im, log_step, b_re, b_im, c_re, c_im, h0_re, h0_im, with_output):
    f32 = jnp.float32
    lam_re, lam_im = lam_re.astype(f32), lam_im.astype(f32)
    b_re, b_im = b_re.astype(f32), b_im.astype(f32)
    step = jnp.exp(log_step.astype(f32))[:, None]
    mag = jnp.exp(lam_re * step)
    a_re, a_im = mag * jnp.cos(lam_im * step), mag * jnp.sin(lam_im * step)
    den = lam_re * lam_re + lam_im * lam_im
    num_re, num_im = a_re - 1.0, a_im
    k_re = (num_re * lam_re + num_im * lam_im) / den
    k_im = (num_im * lam_re - num_re * lam_im) / den
    bb_re = k_re[..., None] * b_re - k_im[..., None] * b_im
    bb_im = k_re[..., None] * b_im + k_im[..., None] * b_re
    x_re = jnp.einsum('blgc,gpc->lbgp', u, bb_re)
    x_im = jnp.einsum('blgc,gpc->lbgp', u, bb_im)
    x_re = x_re.at[0].add(a_re * h0_re - a_im * h0_im)
    x_im = x_im.at[0].add(a_re * h0_im + a_im * h0_re)
    seq_len = u.shape[1]
    ar = jnp.broadcast_to(a_re, (seq_len, 1) + a_re.shape)
    ai = jnp.broadcast_to(a_im, (seq_len, 1) + a_im.shape)
    _, _, h_re, h_im = lax.associative_scan(complex_affine_combine, (ar, ai, x_re, x_im), axis=0)
    final = (h_re[-1], h_im[-1])
    if not with_output:
        return None, final
    y = (jnp.einsum('lbgp,gcp->blgc', h_re, c_re.astype(f32))
         - jnp.einsum('lbgp,gcp->blgc', h_im, c_im.astype(f32)))
    return y, final


def retention_chunked(q, k, v, log_gamma, s0, strict, with_output):
    b, h, seq_len, dk = k.shape
    dv = v.shape[-1]
    n = seq_len // RET_CHUNK
    kc = k.reshape(b, h, n, RET_CHUNK, dk)
    vc = v.reshape(b, h, n, RET_CHUNK, dv)
    pos = jnp.arange(RET_CHUNK, dtype=jnp.float32)
    lg = log_gamma[:, None]
    w_end = jnp.exp(lg * (RET_CHUNK - 1.0 - pos))
    kv = jnp.einsum('bhncd,bhnce->nbhde', kc * w_end[None, :, None, :, None], vc)
    g_chunk = jnp.exp(log_gamma * RET_CHUNK)[None, :, None, None]

    def step(s, inc):
        return g_chunk * s + inc, s

    s_final, s_in = lax.scan(step, s0, kv)
    if not with_output:
        return None, s_final
    qc = q.reshape(b, h, n, RET_CHUNK, dk)
    diff = pos[:, None] - pos[None, :]
    mask = diff > 0 if strict else diff >= 0
    decay = jnp.where(mask, jnp.exp(lg[:, :, None] * jnp.where(mask, diff, 0.0)), 0.0)
    scores = jnp.einsum('bhnid,bhnjd->bhnij', qc, kc) * decay[None, :, None]
    o = jnp.einsum('bhnij,bhnje->bhnie', scores, vc)
    w_in = jnp.exp(lg * (pos + 1.0))
    o = o + jnp.einsum('bhnid,nbhde->bhnie', qc * w_in[None, :, None, :, None], s_in)
    return o.reshape(b, h, seq_len, dv), s_final


def merge_branches(y_ssm, o_ret, g, gs, gr, ssm_glu_w, ret_w_proj, mix_w_out):
    a = jax.nn.gelu(y_ssm)
    ga, gb = jnp.split(a @ ssm_glu_w, 2, axis=-1)
    ssm_branch = ga * jax.nn.sigmoid(gb)
    ret_branch = (jax.nn.silu(g) * o_ret) @ ret_w_proj
    merged = jax.nn.sigmoid(gs) * ssm_branch + jax.nn.sigmoid(gr) * ret_branch
    return merged @ mix_w_out


def token_mixer(u_x, u_c, w_in, lam_re, lam_im, log_step, b_re, b_im, c_re, c_im, d_skip,
                glu_w, decay_logit, ret_w_proj, w_out, with_ctx_out):
    f32 = jnp.float32
    b, seq_len, _ = u_x.shape
    ctx_len = u_c.shape[1]
    dt = u_x.dtype
    rows = seq_len // GRID_W
    row = jnp.repeat(jnp.arange(rows, dtype=jnp.int32), GRID_W)
    col = jnp.tile(jnp.arange(GRID_W, dtype=jnp.int32), rows)
    cuts = [int(v) for v in np.cumsum(MIX_SPLITS)[:-1]]
    s_x, q_x, k_x, v_x, g_x, gs_x, gr_x = jnp.split(u_x @ w_in, cuts, axis=-1)
    s_c, q_c, k_c, v_c, g_c, gs_c, gr_c = jnp.split(u_c @ w_in, cuts, axis=-1)

    us_x = s_x.astype(f32).reshape(b, seq_len, SSM_GROUPS, SSM_GROUP)
    us_c = s_c.astype(f32).reshape(b, ctx_len, SSM_GROUPS, SSM_GROUP)
    dg = d_skip.astype(f32).reshape(SSM_GROUPS, SSM_GROUP)
    y_x = dg * us_x
    y_c = dg * us_c
    h_zero = jnp.zeros((b, SSM_GROUPS, SSM_STATE), f32)
    for d in range(2):
        rev = d == 1
        prm = (lam_re[d], lam_im[d], log_step[d], b_re[d], b_im[d], c_re[d], c_im[d])
        yc_d, (hc_re, hc_im) = s5_scan(_flip(us_c, 1, rev), *prm, h_zero, h_zero, with_ctx_out)
        yx_d, _ = s5_scan(_flip(us_x, 1, rev), *prm, hc_re, hc_im, True)
        y_x = y_x + _flip(yx_d, 1, rev)
        if with_ctx_out:
            y_c = y_c + _flip(yc_d, 1, rev)

    q_scale = RET_DK ** -0.5
    qx = rope_2d(split_heads(q_x, RET_DK), row, col) * q_scale
    kx = rope_2d(split_heads(k_x, RET_DK), row, col)
    vx = split_heads(v_x, RET_DV)
    qc = split_heads(q_c, RET_DK) * q_scale
    kc = split_heads(k_c, RET_DK)
    vc = split_heads(v_c, RET_DV)
    log_gamma = jax.nn.log_sigmoid(decay_logit.astype(f32))
    s_zero = jnp.zeros((b, RET_HEADS, RET_DK, RET_DV), f32)
    ox_dirs = []
    oc_dirs = []
    for d in range(2):
        rev = d == 1
        oc_d, s_ctx = retention_chunked(_flip(qc, 2, rev), _flip(kc, 2, rev), _flip(vc, 2, rev),
                                        log_gamma[d], s_zero, rev, with_ctx_out)
        ox_d, _ = retention_chunked(_flip(qx, 2, rev), _flip(kx, 2, rev), _flip(vx, 2, rev),
                                    log_gamma[d], s_ctx, rev, True)
        ox_dirs.append(_flip(ox_d, 2, rev))
        if with_ctx_out:
            oc_dirs.append(_flip(oc_d, 2, rev))
    o_x = merge_heads(head_rmsnorm(ox_dirs[0] + ox_dirs[1])).astype(dt)

    out_x = merge_branches(y_x.reshape(b, seq_len, SSM_WIDTH).astype(dt), o_x, g_x, gs_x, gr_x,
                           glu_w, ret_w_proj, w_out)
    if not with_ctx_out:
        return out_x, None
    o_c = merge_heads(head_rmsnorm(oc_dirs[0] + oc_dirs[1])).astype(dt)
    out_c = merge_branches(y_c.reshape(b, ctx_len, SSM_WIDTH).astype(dt), o_c, g_c, gs_c, gr_c,
                           glu_w, ret_w_proj, w_out)
    return out_x, out_c


def _fwd_setup_inputs(seed: int = 0) -> dict:
    key = jax.random.key(seed)
    ks = jax.random.split(key, 24)
    f32 = jnp.float32
    G, P, H = SSM_GROUPS, SSM_STATE, RET_HEADS

    def normal(k, shape, scale):
        return jax.random.normal(k, shape, f32) * scale

    n_idx = jnp.arange(P, dtype=f32)
    heads = jnp.arange(H, dtype=f32)
    decay_base = jnp.log(2.0 ** (5.0 + heads) - 1.0)
    return {
        "x": normal(ks[0], (BATCH, SEQ, D_MODEL), 1.0),
        "c": normal(ks[1], (BATCH, D_MODEL), 1.0),
        "ctx": normal(ks[2], (BATCH, CTX_LEN, D_MODEL), 1.0),
        "c_ctx": normal(ks[3], (D_MODEL,), 1.0),
        "ada_w": normal(ks[4], (DEPTH, D_MODEL, N_ADA * D_MODEL), 0.5 * D_MODEL ** -0.5),
        "ada_b": normal(ks[5], (DEPTH, N_ADA * D_MODEL), 0.02),
        "norm_g": 1.0 + normal(ks[6], (DEPTH, 6, D_MODEL), 0.02),
        "ffn_w_in": normal(ks[7], (DEPTH, 2, D_MODEL, 2 * D_FF), D_MODEL ** -0.5),
        "ffn_w_out": normal(ks[8], (DEPTH, 2, D_FF, D_MODEL), D_FF ** -0.5),
        "mix_w_in": normal(ks[9], (DEPTH, D_MODEL, MIX_IN_WIDTH), D_MODEL ** -0.5),
        "ssm_lam_re": -0.5 + normal(ks[10], (DEPTH, 2, G, P), 0.01),
        "ssm_lam_im": math.pi * n_idx + normal(ks[11], (DEPTH, 2, G, P), 0.01),
        "ssm_log_step": jax.random.uniform(ks[12], (DEPTH, 2, G), f32, math.log(1e-3), math.log(1e-1)),
        "ssm_b_re": normal(ks[13], (DEPTH, 2, G, P, SSM_GROUP), (2 * SSM_GROUP) ** -0.5),
        "ssm_b_im": normal(ks[14], (DEPTH, 2, G, P, SSM_GROUP), (2 * SSM_GROUP) ** -0.5),
        "ssm_c_re": normal(ks[15], (DEPTH, 2, G, SSM_GROUP, P), P ** -0.5),
        "ssm_c_im": normal(ks[16], (DEPTH, 2, G, SSM_GROUP, P), P ** -0.5),
        "ssm_d": normal(ks[17], (DEPTH, SSM_WIDTH), 1.0),
        "ssm_glu_w": normal(ks[18], (DEPTH, SSM_WIDTH, 2 * D_MODEL), SSM_WIDTH ** -0.5),
        "ret_decay_logit": decay_base + normal(ks[19], (DEPTH, 2, H), 0.05),
        "ret_w_proj": normal(ks[20], (DEPTH, RET_V_WIDTH, D_MODEL), RET_V_WIDTH ** -0.5),
        "mix_w_out": normal(ks[21], (DEPTH, D_MODEL, D_MODEL), D_MODEL ** -0.5),
    }


def _fwd_reference(x, c, ctx, c_ctx, ada_w, ada_b, norm_g, ffn_w_in, ffn_w_out, mix_w_in,
              ssm_lam_re, ssm_lam_im, ssm_log_step, ssm_b_re, ssm_b_im, ssm_c_re, ssm_c_im,
              ssm_d, ssm_glu_w, ret_decay_logit, ret_w_proj, mix_w_out):
    b = x.shape[0]
    for l in range(DEPTH):
        last = l == DEPTH - 1
        g = norm_g[l]
        ada_x = (jax.nn.silu(c) @ ada_w[l] + ada_b[l]).reshape(b, 1, N_ADA, D_MODEL)
        ada_c = (jax.nn.silu(c_ctx) @ ada_w[l] + ada_b[l]).reshape(1, 1, N_ADA, D_MODEL)

        x = ada_post(x, swiglu(ada_pre(x, ada_x, 0, g[0]), ffn_w_in[l, 0], ffn_w_out[l, 0]), ada_x, 0, g[1], 0.5)
        ctx = ada_post(ctx, swiglu(ada_pre(ctx, ada_c, 0, g[0]), ffn_w_in[l, 0], ffn_w_out[l, 0]), ada_c, 0, g[1], 0.5)

        u_x = ada_pre(x, ada_x, 1, g[2])
        u_c = ada_pre(ctx, ada_c, 1, g[2])
        mix_x, mix_c = token_mixer(u_x, u_c, mix_w_in[l], ssm_lam_re[l], ssm_lam_im[l], ssm_log_step[l],
                                   ssm_b_re[l], ssm_b_im[l], ssm_c_re[l], ssm_c_im[l], ssm_d[l],
                                   ssm_glu_w[l], ret_decay_logit[l], ret_w_proj[l], mix_w_out[l],
                                   not last)
        x = ada_post(x, mix_x, ada_x, 1, g[3], 1.0)

        x = ada_post(x, swiglu(ada_pre(x, ada_x, 2, g[4]), ffn_w_in[l, 1], ffn_w_out[l, 1]), ada_x, 2, g[5], 0.5)
        if not last:
            ctx = ada_post(ctx, mix_c, ada_c, 1, g[3], 1.0)
            ctx = ada_post(ctx, swiglu(ada_pre(ctx, ada_c, 2, g[4]), ffn_w_in[l, 1], ffn_w_out[l, 1]), ada_c, 2, g[5], 0.5)
    return x


import jax as _jax
import jax.numpy as _jnp

TWIN_FORMAT = 'train_step'
FWD_PARAMS = ['x', 'c', 'ctx', 'c_ctx', 'ada_w', 'ada_b', 'norm_g', 'ffn_w_in', 'ffn_w_out', 'mix_w_in', 'ssm_lam_re', 'ssm_lam_im', 'ssm_log_step', 'ssm_b_re', 'ssm_b_im', 'ssm_c_re', 'ssm_c_im', 'ssm_d', 'ssm_glu_w', 'ret_decay_logit', 'ret_w_proj', 'mix_w_out']
TWIN_WEIGHTS = ['c_ctx', 'ada_w', 'ada_b', 'norm_g', 'ffn_w_in', 'ffn_w_out', 'mix_w_in', 'ssm_lam_re', 'ssm_lam_im', 'ssm_log_step', 'ssm_b_re', 'ssm_b_im', 'ssm_c_re', 'ssm_c_im', 'ssm_d', 'ssm_glu_w', 'ret_decay_logit', 'ret_w_proj', 'mix_w_out']
TWIN_DIFF_INPUT = 'x'
TWIN_INPUTS = ['x', 'c', 'ctx', 'c_ctx', 'ada_w', 'ada_b', 'norm_g', 'ffn_w_in', 'ffn_w_out', 'mix_w_in', 'ssm_lam_re', 'ssm_lam_im', 'ssm_log_step', 'ssm_b_re', 'ssm_b_im', 'ssm_c_re', 'ssm_c_im', 'ssm_d', 'ssm_glu_w', 'ret_decay_logit', 'ret_w_proj', 'mix_w_out', 'loss_target', 'm_c_ctx', 'm_ada_w', 'm_ada_b', 'm_norm_g', 'm_ffn_w_in', 'm_ffn_w_out', 'm_mix_w_in', 'm_ssm_lam_re', 'm_ssm_lam_im', 'm_ssm_log_step', 'm_ssm_b_re', 'm_ssm_b_im', 'm_ssm_c_re', 'm_ssm_c_im', 'm_ssm_d', 'm_ssm_glu_w', 'm_ret_decay_logit', 'm_ret_w_proj', 'm_mix_w_out', 'v_c_ctx', 'v_ada_w', 'v_ada_b', 'v_norm_g', 'v_ffn_w_in', 'v_ffn_w_out', 'v_mix_w_in', 'v_ssm_lam_re', 'v_ssm_lam_im', 'v_ssm_log_step', 'v_ssm_b_re', 'v_ssm_b_im', 'v_ssm_c_re', 'v_ssm_c_im', 'v_ssm_d', 'v_ssm_glu_w', 'v_ret_decay_logit', 'v_ret_w_proj', 'v_mix_w_out']
TWIN_OUTPUTS = ['loss', 'grad_x', 'grad_c_ctx', 'grad_ada_w', 'grad_ada_b', 'grad_norm_g', 'grad_ffn_w_in', 'grad_ffn_w_out', 'grad_mix_w_in', 'grad_ssm_lam_re', 'grad_ssm_lam_im', 'grad_ssm_log_step', 'grad_ssm_b_re', 'grad_ssm_b_im', 'grad_ssm_c_re', 'grad_ssm_c_im', 'grad_ssm_d', 'grad_ssm_glu_w', 'grad_ret_decay_logit', 'grad_ret_w_proj', 'grad_mix_w_out', 'delta_c_ctx', 'delta_ada_w', 'delta_ada_b', 'delta_norm_g', 'delta_ffn_w_in', 'delta_ffn_w_out', 'delta_mix_w_in', 'delta_ssm_lam_re', 'delta_ssm_lam_im', 'delta_ssm_log_step', 'delta_ssm_b_re', 'delta_ssm_b_im', 'delta_ssm_c_re', 'delta_ssm_c_im', 'delta_ssm_d', 'delta_ssm_glu_w', 'delta_ret_decay_logit', 'delta_ret_w_proj', 'delta_mix_w_out', 'new_m_c_ctx', 'new_m_ada_w', 'new_m_ada_b', 'new_m_norm_g', 'new_m_ffn_w_in', 'new_m_ffn_w_out', 'new_m_mix_w_in', 'new_m_ssm_lam_re', 'new_m_ssm_lam_im', 'new_m_ssm_log_step', 'new_m_ssm_b_re', 'new_m_ssm_b_im', 'new_m_ssm_c_re', 'new_m_ssm_c_im', 'new_m_ssm_d', 'new_m_ssm_glu_w', 'new_m_ret_decay_logit', 'new_m_ret_w_proj', 'new_m_mix_w_out', 'new_v_c_ctx', 'new_v_ada_w', 'new_v_ada_b', 'new_v_norm_g', 'new_v_ffn_w_in', 'new_v_ffn_w_out', 'new_v_mix_w_in', 'new_v_ssm_lam_re', 'new_v_ssm_lam_im', 'new_v_ssm_log_step', 'new_v_ssm_b_re', 'new_v_ssm_b_im', 'new_v_ssm_c_re', 'new_v_ssm_c_im', 'new_v_ssm_d', 'new_v_ssm_glu_w', 'new_v_ret_decay_logit', 'new_v_ret_w_proj', 'new_v_mix_w_out']
TWIN_LEAF_KINDS = {'loss': 'loss', 'grad_x': 'grad_x', 'grad_c_ctx': 'grad_w', 'grad_ada_w': 'grad_w', 'grad_ada_b': 'grad_w', 'grad_norm_g': 'grad_w', 'grad_ffn_w_in': 'grad_w', 'grad_ffn_w_out': 'grad_w', 'grad_mix_w_in': 'grad_w', 'grad_ssm_lam_re': 'grad_w', 'grad_ssm_lam_im': 'grad_w', 'grad_ssm_log_step': 'grad_w', 'grad_ssm_b_re': 'grad_w', 'grad_ssm_b_im': 'grad_w', 'grad_ssm_c_re': 'grad_w', 'grad_ssm_c_im': 'grad_w', 'grad_ssm_d': 'grad_w', 'grad_ssm_glu_w': 'grad_w', 'grad_ret_decay_logit': 'grad_w', 'grad_ret_w_proj': 'grad_w', 'grad_mix_w_out': 'grad_w', 'delta_c_ctx': 'delta_w', 'delta_ada_w': 'delta_w', 'delta_ada_b': 'delta_w', 'delta_norm_g': 'delta_w', 'delta_ffn_w_in': 'delta_w', 'delta_ffn_w_out': 'delta_w', 'delta_mix_w_in': 'delta_w', 'delta_ssm_lam_re': 'delta_w', 'delta_ssm_lam_im': 'delta_w', 'delta_ssm_log_step': 'delta_w', 'delta_ssm_b_re': 'delta_w', 'delta_ssm_b_im': 'delta_w', 'delta_ssm_c_re': 'delta_w', 'delta_ssm_c_im': 'delta_w', 'delta_ssm_d': 'delta_w', 'delta_ssm_glu_w': 'delta_w', 'delta_ret_decay_logit': 'delta_w', 'delta_ret_w_proj': 'delta_w', 'delta_mix_w_out': 'delta_w', 'new_m_c_ctx': 'new_m', 'new_m_ada_w': 'new_m', 'new_m_ada_b': 'new_m', 'new_m_norm_g': 'new_m', 'new_m_ffn_w_in': 'new_m', 'new_m_ffn_w_out': 'new_m', 'new_m_mix_w_in': 'new_m', 'new_m_ssm_lam_re': 'new_m', 'new_m_ssm_lam_im': 'new_m', 'new_m_ssm_log_step': 'new_m', 'new_m_ssm_b_re': 'new_m', 'new_m_ssm_b_im': 'new_m', 'new_m_ssm_c_re': 'new_m', 'new_m_ssm_c_im': 'new_m', 'new_m_ssm_d': 'new_m', 'new_m_ssm_glu_w': 'new_m', 'new_m_ret_decay_logit': 'new_m', 'new_m_ret_w_proj': 'new_m', 'new_m_mix_w_out': 'new_m', 'new_v_c_ctx': 'new_v', 'new_v_ada_w': 'new_v', 'new_v_ada_b': 'new_v', 'new_v_norm_g': 'new_v', 'new_v_ffn_w_in': 'new_v', 'new_v_ffn_w_out': 'new_v', 'new_v_mix_w_in': 'new_v', 'new_v_ssm_lam_re': 'new_v', 'new_v_ssm_lam_im': 'new_v', 'new_v_ssm_log_step': 'new_v', 'new_v_ssm_b_re': 'new_v', 'new_v_ssm_b_im': 'new_v', 'new_v_ssm_c_re': 'new_v', 'new_v_ssm_c_im': 'new_v', 'new_v_ssm_d': 'new_v', 'new_v_ssm_glu_w': 'new_v', 'new_v_ret_decay_logit': 'new_v', 'new_v_ret_w_proj': 'new_v', 'new_v_mix_w_out': 'new_v'}


def _forward(args):
    return _fwd_reference(*[args[k] for k in FWD_PARAMS])


def _output_shape():
    def fwd():
        inp = _fwd_setup_inputs(0)
        return _fwd_reference(*[inp[k] for k in FWD_PARAMS])
    out = _jax.eval_shape(fwd)
    return out.shape, out.dtype

N_MICROBATCH = 1
ADAM_LR = 0.001
ADAM_B1 = 0.9
ADAM_B2 = 0.999
ADAM_EPS = 1e-08
ADAM_WD = 0.01
ADAM_STEP = 10
PER_EXAMPLE_BATCH_AXIS = {'x': 0, 'c': 0, 'ctx': 0, 'loss_target': 0}
SHARED_INPUTS = []
_WEIGHT_DTYPES = {'c_ctx': _jnp.float32, 'ada_w': _jnp.float32, 'ada_b': _jnp.float32, 'norm_g': _jnp.float32, 'ffn_w_in': _jnp.float32, 'ffn_w_out': _jnp.float32, 'mix_w_in': _jnp.float32, 'ssm_lam_re': _jnp.float32, 'ssm_lam_im': _jnp.float32, 'ssm_log_step': _jnp.float32, 'ssm_b_re': _jnp.float32, 'ssm_b_im': _jnp.float32, 'ssm_c_re': _jnp.float32, 'ssm_c_im': _jnp.float32, 'ssm_d': _jnp.float32, 'ssm_glu_w': _jnp.float32, 'ret_decay_logit': _jnp.float32, 'ret_w_proj': _jnp.float32, 'mix_w_out': _jnp.float32}
MOMENT_SCALE = {'c_ctx': 2.755159e-02, 'ada_w': 3.630178e-01, 'ada_b': 7.891912e-01, 'norm_g': 6.991134e-01, 'ffn_w_in': 1.366695e-02, 'ffn_w_out': 2.306295e-02, 'mix_w_in': 3.720922e-02, 'ssm_lam_re': 3.761023e-03, 'ssm_lam_im': 4.486347e-03, 'ssm_log_step': 1.438538e+00, 'ssm_b_re': 2.690636e-03, 'ssm_b_im': 2.800659e-03, 'ssm_c_re': 3.609965e-03, 'ssm_c_im': 3.507234e-03, 'ssm_d': 5.957358e-02, 'ssm_glu_w': 2.942845e-02, 'ret_decay_logit': 2.244838e-01, 'ret_w_proj': 3.890548e-02, 'mix_w_out': 5.695240e-02}


def _to_microbatches(a, axis):
    t = _jnp.moveaxis(a, axis, 0)
    t = t.reshape((N_MICROBATCH, t.shape[0] // N_MICROBATCH) + t.shape[1:])
    return _jnp.moveaxis(t, 1, axis + 1)


def setup_inputs(seed: int = 0) -> dict:
    inp = _fwd_setup_inputs(seed)
    key = _jax.random.fold_in(_jax.random.key(seed), 7919)
    shape, _ = _output_shape()
    out = dict(inp)
    out["loss_target"] = _jax.random.normal(_jax.random.fold_in(key, 0), shape, _jnp.float32)
    for i, name in enumerate(TWIN_WEIGHTS):
        w = inp[name].astype(_jnp.float32)
        if MOMENT_SCALE is None:
            s = _jnp.sqrt(_jnp.mean(_jnp.square(w)) + 1e-30)
        else:
            s = MOMENT_SCALE[name]
        km, kv = _jax.random.split(_jax.random.fold_in(key, i + 1))
        out[name] = w
        out["m_" + name] = s * _jax.random.normal(km, w.shape, _jnp.float32)
        out["v_" + name] = (s * s) * _jax.random.uniform(kv, w.shape, _jnp.float32, 0.5, 1.5)
    if N_MICROBATCH > 1:
        for name, axis in PER_EXAMPLE_BATCH_AXIS.items():
            out[name] = _to_microbatches(out[name], axis)
    return {'x': out['x'], 'c': out['c'], 'ctx': out['ctx'], 'c_ctx': out['c_ctx'], 'ada_w': out['ada_w'], 'ada_b': out['ada_b'], 'norm_g': out['norm_g'], 'ffn_w_in': out['ffn_w_in'], 'ffn_w_out': out['ffn_w_out'], 'mix_w_in': out['mix_w_in'], 'ssm_lam_re': out['ssm_lam_re'], 'ssm_lam_im': out['ssm_lam_im'], 'ssm_log_step': out['ssm_log_step'], 'ssm_b_re': out['ssm_b_re'], 'ssm_b_im': out['ssm_b_im'], 'ssm_c_re': out['ssm_c_re'], 'ssm_c_im': out['ssm_c_im'], 'ssm_d': out['ssm_d'], 'ssm_glu_w': out['ssm_glu_w'], 'ret_decay_logit': out['ret_decay_logit'], 'ret_w_proj': out['ret_w_proj'], 'mix_w_out': out['mix_w_out'], 'loss_target': out['loss_target'], 'm_c_ctx': out['m_c_ctx'], 'm_ada_w': out['m_ada_w'], 'm_ada_b': out['m_ada_b'], 'm_norm_g': out['m_norm_g'], 'm_ffn_w_in': out['m_ffn_w_in'], 'm_ffn_w_out': out['m_ffn_w_out'], 'm_mix_w_in': out['m_mix_w_in'], 'm_ssm_lam_re': out['m_ssm_lam_re'], 'm_ssm_lam_im': out['m_ssm_lam_im'], 'm_ssm_log_step': out['m_ssm_log_step'], 'm_ssm_b_re': out['m_ssm_b_re'], 'm_ssm_b_im': out['m_ssm_b_im'], 'm_ssm_c_re': out['m_ssm_c_re'], 'm_ssm_c_im': out['m_ssm_c_im'], 'm_ssm_d': out['m_ssm_d'], 'm_ssm_glu_w': out['m_ssm_glu_w'], 'm_ret_decay_logit': out['m_ret_decay_logit'], 'm_ret_w_proj': out['m_ret_w_proj'], 'm_mix_w_out': out['m_mix_w_out'], 'v_c_ctx': out['v_c_ctx'], 'v_ada_w': out['v_ada_w'], 'v_ada_b': out['v_ada_b'], 'v_norm_g': out['v_norm_g'], 'v_ffn_w_in': out['v_ffn_w_in'], 'v_ffn_w_out': out['v_ffn_w_out'], 'v_mix_w_in': out['v_mix_w_in'], 'v_ssm_lam_re': out['v_ssm_lam_re'], 'v_ssm_lam_im': out['v_ssm_lam_im'], 'v_ssm_log_step': out['v_ssm_log_step'], 'v_ssm_b_re': out['v_ssm_b_re'], 'v_ssm_b_im': out['v_ssm_b_im'], 'v_ssm_c_re': out['v_ssm_c_re'], 'v_ssm_c_im': out['v_ssm_c_im'], 'v_ssm_d': out['v_ssm_d'], 'v_ssm_glu_w': out['v_ssm_glu_w'], 'v_ret_decay_logit': out['v_ret_decay_logit'], 'v_ret_w_proj': out['v_ret_w_proj'], 'v_mix_w_out': out['v_mix_w_out']}


def _loss(weights, diff, rest, loss_target):
    with _jax.named_scope("forward"):
        args = {**rest, TWIN_DIFF_INPUT: diff, **{k: w.astype(_WEIGHT_DTYPES[k]) for k, w in weights.items()}}
        y = _forward(args)
    with _jax.named_scope("loss_head"):
        err = _jnp.square(y.astype(_jnp.float32) - loss_target)
        return 0.5 * _jnp.sum(_jnp.mean(err, axis=-1)) if err.ndim else 0.5 * err


def _adamw(w, g, m, v):
    m = ADAM_B1 * m + (1.0 - ADAM_B1) * g
    v = ADAM_B2 * v + (1.0 - ADAM_B2) * _jnp.square(g)
    m_hat = m / (1.0 - ADAM_B1 ** ADAM_STEP)
    v_hat = v / (1.0 - ADAM_B2 ** ADAM_STEP)
    delta = -ADAM_LR * (m_hat / (_jnp.sqrt(v_hat) + ADAM_EPS) + ADAM_WD * w)
    return delta, m, v


def reference(x, c, ctx, c_ctx, ada_w, ada_b, norm_g, ffn_w_in, ffn_w_out, mix_w_in, ssm_lam_re, ssm_lam_im, ssm_log_step, ssm_b_re, ssm_b_im, ssm_c_re, ssm_c_im, ssm_d, ssm_glu_w, ret_decay_logit, ret_w_proj, mix_w_out, loss_target, m_c_ctx, m_ada_w, m_ada_b, m_norm_g, m_ffn_w_in, m_ffn_w_out, m_mix_w_in, m_ssm_lam_re, m_ssm_lam_im, m_ssm_log_step, m_ssm_b_re, m_ssm_b_im, m_ssm_c_re, m_ssm_c_im, m_ssm_d, m_ssm_glu_w, m_ret_decay_logit, m_ret_w_proj, m_mix_w_out, v_c_ctx, v_ada_w, v_ada_b, v_norm_g, v_ffn_w_in, v_ffn_w_out, v_mix_w_in, v_ssm_lam_re, v_ssm_lam_im, v_ssm_log_step, v_ssm_b_re, v_ssm_b_im, v_ssm_c_re, v_ssm_c_im, v_ssm_d, v_ssm_glu_w, v_ret_decay_logit, v_ret_w_proj, v_mix_w_out):
    given = dict(x=x, c=c, ctx=ctx, c_ctx=c_ctx, ada_w=ada_w, ada_b=ada_b, norm_g=norm_g, ffn_w_in=ffn_w_in, ffn_w_out=ffn_w_out, mix_w_in=mix_w_in, ssm_lam_re=ssm_lam_re, ssm_lam_im=ssm_lam_im, ssm_log_step=ssm_log_step, ssm_b_re=ssm_b_re, ssm_b_im=ssm_b_im, ssm_c_re=ssm_c_re, ssm_c_im=ssm_c_im, ssm_d=ssm_d, ssm_glu_w=ssm_glu_w, ret_decay_logit=ret_decay_logit, ret_w_proj=ret_w_proj, mix_w_out=mix_w_out, loss_target=loss_target, m_c_ctx=m_c_ctx, m_ada_w=m_ada_w, m_ada_b=m_ada_b, m_norm_g=m_norm_g, m_ffn_w_in=m_ffn_w_in, m_ffn_w_out=m_ffn_w_out, m_mix_w_in=m_mix_w_in, m_ssm_lam_re=m_ssm_lam_re, m_ssm_lam_im=m_ssm_lam_im, m_ssm_log_step=m_ssm_log_step, m_ssm_b_re=m_ssm_b_re, m_ssm_b_im=m_ssm_b_im, m_ssm_c_re=m_ssm_c_re, m_ssm_c_im=m_ssm_c_im, m_ssm_d=m_ssm_d, m_ssm_glu_w=m_ssm_glu_w, m_ret_decay_logit=m_ret_decay_logit, m_ret_w_proj=m_ret_w_proj, m_mix_w_out=m_mix_w_out, v_c_ctx=v_c_ctx, v_ada_w=v_ada_w, v_ada_b=v_ada_b, v_norm_g=v_norm_g, v_ffn_w_in=v_ffn_w_in, v_ffn_w_out=v_ffn_w_out, v_mix_w_in=v_mix_w_in, v_ssm_lam_re=v_ssm_lam_re, v_ssm_lam_im=v_ssm_lam_im, v_ssm_log_step=v_ssm_log_step, v_ssm_b_re=v_ssm_b_re, v_ssm_b_im=v_ssm_b_im, v_ssm_c_re=v_ssm_c_re, v_ssm_c_im=v_ssm_c_im, v_ssm_d=v_ssm_d, v_ssm_glu_w=v_ssm_glu_w, v_ret_decay_logit=v_ret_decay_logit, v_ret_w_proj=v_ret_w_proj, v_mix_w_out=v_mix_w_out)
    weights = {n: given[n] for n in TWIN_WEIGHTS}
    shared = {n: given[n] for n in SHARED_INPUTS}
    per_example = {n: given[n] for n in ['x', 'c', 'ctx']}
    grad_fn = _jax.value_and_grad(_loss, argnums=(0, 1))

    def one_microbatch(ex, loss_target):
        ex = dict(ex)
        diff = ex.pop(TWIN_DIFF_INPUT)
        return grad_fn(weights, diff, {**shared, **ex}, loss_target)

    if N_MICROBATCH == 1:
        loss, (grad_w, grad_x) = one_microbatch(per_example, given["loss_target"])
    else:
        def body(carry, xs):
            loss_sum, grad_sum = carry
            l_k, (gw_k, gx_k) = one_microbatch(xs[0], xs[1])
            with _jax.named_scope("update"):
                return (loss_sum + l_k, _jax.tree.map(_jnp.add, grad_sum, gw_k)), gx_k

        init = (_jnp.zeros((), _jnp.float32), _jax.tree.map(_jnp.zeros_like, weights))
        (loss, grad_w), grad_x = _jax.lax.scan(body, init, (per_example, given["loss_target"]))
    with _jax.named_scope("update"):
        delta_w, new_m, new_v = {}, {}, {}
        for n in TWIN_WEIGHTS:
            delta_w[n], new_m[n], new_v[n] = _adamw(weights[n], grad_w[n], given["m_" + n], given["v_" + n])
    return (loss, grad_x, *[grad_w[n] for n in TWIN_WEIGHTS], *[delta_w[n] for n in TWIN_WEIGHTS],
            *[new_m[n] for n in TWIN_WEIGHTS], *[new_v[n] for n in TWIN_WEIGHTS])
```

```python
import functools
import math

import jax
import jax.numpy as jnp
import numpy as np
from jax import lax
from jax.experimental import pallas as pl
from jax.experimental.pallas import tpu as pltpu

F32 = jnp.float32
BF16 = jnp.bfloat16
MXU_DTYPE = jnp.bfloat16
MESH_AXES = ("x", "y", "c")
N_DEV = 8
V7X_VMEM_LIMIT_BYTES = 56 * 1024 * 1024
LANE = 128
SUBLANE = 8

GRID_W = 64
RET_CHUNK = 128
ROPE_BASE = 10000.0
NORM_EPS = 1e-6
ADAM_LR = 0.001
ADAM_B1 = 0.9
ADAM_B2 = 0.999
ADAM_EPS = 1e-08
ADAM_WD = 0.01
ADAM_STEP = 10
SSM_TILE_GROUPS = 8
SSM_HALF_GROUPS = 4


def _params(sem=None):
    return pltpu.CompilerParams(dimension_semantics=sem, vmem_limit_bytes=V7X_VMEM_LIMIT_BYTES)


def _tile(n, target, mult):
    best = None
    t = mult
    while t <= min(n, target):
        if n % t == 0:
            best = t
        t += mult
    return n if best is None else best


def _sds(shape, dtype):
    return jax.ShapeDtypeStruct(tuple(shape), dtype)


def _mm(a, b, dims, out_dtype, name, tm=512, tn=1408, tk=2048):
    if dims == "nn":
        (m, k), (k2, n) = a.shape, b.shape
    elif dims == "nt":
        (m, k), (n, k2) = a.shape, b.shape
    else:
        (k, m), (k2, n) = a.shape, b.shape
    assert k == k2, (a.shape, b.shape, dims)
    tm = _tile(m, tm, 16)
    tn = _tile(n, tn, LANE)
    tk = _tile(k, tk, LANE if dims != "tn" else 16)
    nk = k // tk
    dn = {"nn": (((1,), (0,)), ((), ())), "nt": (((1,), (1,)), ((), ())), "tn": (((0,), (0,)), ((), ()))}[dims]

    def body(a_ref, b_ref, o_ref, acc_ref):
        kk = pl.program_id(2)

        @pl.when(kk == 0)
        def _():
            acc_ref[...] = jnp.zeros_like(acc_ref)

        acc_ref[...] += lax.dot_general(a_ref[...].astype(MXU_DTYPE), b_ref[...].astype(MXU_DTYPE), dn,
                                        preferred_element_type=F32)

        @pl.when(kk == nk - 1)
        def _():
            o_ref[...] = acc_ref[...].astype(o_ref.dtype)

    if dims == "nn":
        a_spec = pl.BlockSpec((tm, tk), lambda j, i, kk: (i, kk))
        b_spec = pl.BlockSpec((tk, tn), lambda j, i, kk: (kk, j))
    elif dims == "nt":
        a_spec = pl.BlockSpec((tm, tk), lambda j, i, kk: (i, kk))
        b_spec = pl.BlockSpec((tn, tk), lambda j, i, kk: (j, kk))
    else:
        a_spec = pl.BlockSpec((tk, tm), lambda j, i, kk: (kk, i))
        b_spec = pl.BlockSpec((tk, tn), lambda j, i, kk: (kk, j))
    return pl.pallas_call(
        body, name=name, grid=(n // tn, m // tm, nk), in_specs=[a_spec, b_spec],
        out_specs=pl.BlockSpec((tm, tn), lambda j, i, kk: (i, j)), out_shape=_sds((m, n), out_dtype),
        scratch_shapes=[pltpu.VMEM((tm, tn), F32)],
        compiler_params=_params(("parallel", "parallel", "arbitrary")))(a, b)


def _rows(name, body, n_tiles, ins, outs):
    in_specs = [pl.BlockSpec(blk, imap) for (_, blk, imap) in ins]
    out_specs = [pl.BlockSpec(blk, imap) for (_, _, blk, imap) in outs]
    out_shape = [_sds(shape, dt) for (shape, dt, _, _) in outs]
    res = pl.pallas_call(body, name=name, grid=(n_tiles,), in_specs=in_specs, out_specs=out_specs,
                         out_shape=out_shape, compiler_params=_params(("arbitrary",)))(*[a for (a, _, _) in ins])
    return res


def _row_in(arr, tile, width=None, col=0, x_only_offset=None):
    width = arr.shape[1] if width is None else width
    if x_only_offset is None:
        return (arr, (tile, width), lambda i: (i, col))
    return (arr, (tile, width), lambda i: (jnp.maximum(i - x_only_offset, 0), col))


def _vec_in(arr, idx_fn):
    return (arr, (1, 1, arr.shape[2]), lambda i: (idx_fn(i), 0, 0))


def _rms(h):
    return lax.rsqrt(jnp.mean(h * h, axis=-1, keepdims=True) + NORM_EPS)


def _sigmoid(z):
    return 1.0 / (1.0 + jnp.exp(-z))


def _ada_pre_fwd(h, g6, mods, gi, mi, nct, tile, name):
    r, d = h.shape
    sel = lambda i: jnp.where(i >= nct, 1, 0)

    def body(h_ref, g_ref, sh_ref, sc_ref, u_ref):
        hh = h_ref[...]
        n = hh * _rms(hh) * g_ref[0]
        u_ref[...] = (n * (1.0 + sc_ref[0]) + sh_ref[0]).astype(u_ref.dtype)

    (u,) = _rows(name, body, r // tile,
                 [_row_in(h, tile), _vec_in(g6, lambda i: gi), _vec_in(mods, lambda i: sel(i) * 9 + 3 * mi),
                  _vec_in(mods, lambda i: sel(i) * 9 + 3 * mi + 1)],
                 [((r, d), BF16, (tile, d), lambda i: (i, 0))])
    return u


def _ada_pre_bwd(h, du, dres, g6, mods, gi, mi, nct, nsel, tile, name, dres_x_only=False):
    r, d = h.shape
    sel = lambda i: jnp.where(i >= nct, 1, 0) if nsel == 2 else 0
    msel = lambda i: jnp.where(i >= nct, 1, 0)
    off = nct if dres_x_only else None

    def body(h_ref, du_ref, dr_ref, g_ref, sc_ref, dh_ref, dg_ref, dsh_ref, dsc_ref):
        i = pl.program_id(0)
        hh = h_ref[...]
        rr = _rms(hh)
        g = g_ref[0]
        hn = hh * rr
        n = hn * g
        du_ = du_ref[...].astype(F32)
        dn = du_ * (1.0 + sc_ref[0])

        @pl.when(i == 0)
        def _():
            dg_ref[...] = jnp.zeros_like(dg_ref)

        @pl.when((i == 0) | (i == nct))
        def _():
            dsh_ref[...] = jnp.zeros_like(dsh_ref)
            dsc_ref[...] = jnp.zeros_like(dsc_ref)

        dg_ref[0] += jnp.sum(dn * hn, axis=0, keepdims=True)
        dsh_ref[0] += jnp.sum(du_, axis=0, keepdims=True)
        dsc_ref[0] += jnp.sum(du_ * n, axis=0, keepdims=True)
        t = dn * g
        dh = rr * t - hn * (rr * jnp.mean(t * hn, axis=-1, keepdims=True))
        if dres_x_only:
            dh_ref[...] = dh + jnp.where(i >= nct, dr_ref[...], 0.0)
        else:
            dh_ref[...] = dh + dr_ref[...]

    dh, dg, dsh, dsc = _rows(
        name, body, r // tile,
        [_row_in(h, tile), _row_in(du, tile), _row_in(dres, tile, x_only_offset=off), _vec_in(g6, lambda i: gi),
         _vec_in(mods, lambda i: msel(i) * 9 + 3 * mi + 1)],
        [((r, d), F32, (tile, d), lambda i: (i, 0)), ((1, 1, d), F32, (1, 1, d), lambda i: (0, 0, 0)),
         ((nsel, 1, d), F32, (1, 1, d), lambda i: (sel(i), 0, 0)),
         ((nsel, 1, d), F32, (1, 1, d), lambda i: (sel(i), 0, 0))])
    return dh, dg, dsh, dsc


def _ada_post_fwd(h, o, g6, mods, gi, mi, res_w, nct, tile, name, h_x_only=False):
    r, d = o.shape
    sel = lambda i: jnp.where(i >= nct, 1, 0)

    def body(h_ref, o_ref, g_ref, gt_ref, y_ref):
        oo = o_ref[...]
        n = oo * _rms(oo) * g_ref[0]
        y_ref[...] = h_ref[...] + res_w * gt_ref[0] * n

    (y,) = _rows(name, body, r // tile,
                 [_row_in(h, tile), _row_in(o, tile), _vec_in(g6, lambda i: gi),
                  _vec_in(mods, lambda i: sel(i) * 9 + 3 * mi + 2)],
                 [((r, d), F32, (tile, d), lambda i: (i, 0))])
    return y


def _ada_post_bwd(dy, o, g6, mods, gi, mi, res_w, nct, nsel, tile, name):
    r, d = o.shape
    sel = lambda i: jnp.where(i >= nct, 1, 0) if nsel == 2 else 0
    msel = lambda i: jnp.where(i >= nct, 1, 0)

    def body(dy_ref, o_ref, g_ref, gt_ref, do_ref, dg_ref, dgt_ref):
        i = pl.program_id(0)
        oo = o_ref[...]
        rr = _rms(oo)
        g = g_ref[0]
        on = oo * rr
        dy_ = dy_ref[...] * res_w

        @pl.when(i == 0)
        def _():
            dg_ref[...] = jnp.zeros_like(dg_ref)

        @pl.when((i == 0) | (i == nct))
        def _():
            dgt_ref[...] = jnp.zeros_like(dgt_ref)

        dgt_ref[0] += jnp.sum(dy_ * (on * g), axis=0, keepdims=True)
        dn = dy_ * gt_ref[0]
        dg_ref[0] += jnp.sum(dn * on, axis=0, keepdims=True)
        t = dn * g
        do_ref[...] = rr * t - on * (rr * jnp.mean(t * on, axis=-1, keepdims=True))

    do, dg, dgt = _rows(
        name, body, r // tile,
        [_row_in(dy, tile), _row_in(o, tile), _vec_in(g6, lambda i: gi),
         _vec_in(mods, lambda i: msel(i) * 9 + 3 * mi + 2)],
        [((r, d), F32, (tile, d), lambda i: (i, 0)), ((1, 1, d), F32, (1, 1, d), lambda i: (0, 0, 0)),
         ((nsel, 1, d), F32, (1, 1, d), lambda i: (sel(i), 0, 0))])
    return do, dg, dgt


def _swiglu_fwd(h, tile, name):
    r, w2 = h.shape
    f = w2 // 2

    def body(h_ref, a_ref):
        gt = h_ref[:, :f]
        up = h_ref[:, f:]
        a_ref[...] = (gt * _sigmoid(gt) * up).astype(a_ref.dtype)

    (a,) = _rows(name, body, r // tile, [_row_in(h, tile)], [((r, f), BF16, (tile, f), lambda i: (i, 0))])
    return a


def _swiglu_bwd(h, da, tile, name):
    r, w2 = h.shape
    f = w2 // 2

    def body(h_ref, da_ref, dh_ref):
        gt = h_ref[:, :f]
        up = h_ref[:, f:]
        d = da_ref[...]
        sg = _sigmoid(gt)
        dh_ref[:, :f] = (d * up * (sg * (1.0 + gt * (1.0 - sg)))).astype(dh_ref.dtype)
        dh_ref[:, f:] = (d * gt * sg).astype(dh_ref.dtype)

    (dh,) = _rows(name, body, r // tile, [_row_in(h, tile), _row_in(da, tile)],
                  [((r, w2), BF16, (tile, w2), lambda i: (i, 0))])
    return dh


def _gelu_parts(y):
    c0 = math.sqrt(2.0 / math.pi)
    inner = c0 * (y + 0.044715 * y * y * y)
    th = jnp.tanh(inner)
    return th, c0 * (1.0 + 3 * 0.044715 * y * y)


def _ssm_out_fwd(y0, y1, hm, dskip, nct, tile, name):
    t_rows, s = y0.shape

    def body(y0_ref, y1_ref, u_ref, d_ref, a_ref):
        y = y0_ref[...] + y1_ref[...] + d_ref[0] * u_ref[...]
        th, _ = _gelu_parts(y)
        a_ref[...] = (0.5 * y * (1.0 + th)).astype(a_ref.dtype)

    (a,) = _rows(name, body, t_rows // tile,
                 [_row_in(y0, tile), _row_in(y1, tile), (hm, (tile, s), lambda i: (i + nct, 0)),
                  _vec_in(dskip, lambda i: 0)],
                 [((t_rows, s), BF16, (tile, s), lambda i: (i, 0))])
    return a


def _ssm_out_bwd(y0, y1, hm, dskip, da, nct, tile, name):
    t_rows, s = y0.shape

    def body(y0_ref, y1_ref, u_ref, d_ref, da_ref, dy_ref, du_ref, dd_ref):
        i = pl.program_id(0)
        u = u_ref[...]
        y = y0_ref[...] + y1_ref[...] + d_ref[0] * u
        th, dinner = _gelu_parts(y)
        dy = da_ref[...] * (0.5 * (1.0 + th) + 0.5 * y * (1.0 - th * th) * dinner)
        dy_ref[...] = dy
        du_ref[...] = dy * d_ref[0]

        @pl.when(i == 0)
        def _():
            dd_ref[...] = jnp.zeros_like(dd_ref)

        dd_ref[0] += jnp.sum(dy * u, axis=0, keepdims=True)

    dy, du, dd = _rows(name, body, t_rows // tile,
                       [_row_in(y0, tile), _row_in(y1, tile), (hm, (tile, s), lambda i: (i + nct, 0)),
                        _vec_in(dskip, lambda i: 0), _row_in(da, tile)],
                       [((t_rows, s), F32, (tile, s), lambda i: (i, 0)), ((t_rows, s), F32, (tile, s), lambda i: (i, 0)),
                        ((1, 1, s), F32, (1, 1, s), lambda i: (0, 0, 0))])
    return dy, du, dd


def _col_pieces(arr, off, width, tile, nct, unit=None):
    pw = math.gcd(off, width if unit is None else unit)
    specs = [(arr, (tile, pw), functools.partial(lambda i, cb: (i + nct, cb), cb=off // pw + p))
             for p in range(width // pw)]
    return specs, pw


def _ret_gate_fwd(o0, o1, hm, g_off, heads, dv, nct, tile, name):
    t_rows, w = o0.shape
    g_specs, pw = _col_pieces(hm, g_off, w, tile, nct)
    ng = len(g_specs)

    def body(o0_ref, o1_ref, *refs):
        g_refs, r_ref = refs[:ng], refs[ng]
        for hd in range(heads):
            cs = slice(hd * dv, (hd + 1) * dv)
            o = o0_ref[:, cs] + o1_ref[:, cs]
            lo = (hd * dv) % pw
            g = g_refs[(hd * dv) // pw][:, lo:lo + dv]
            r_ref[:, cs] = (g * _sigmoid(g) * (o * _rms(o))).astype(r_ref.dtype)

    (ri,) = _rows(name, body, t_rows // tile, [_row_in(o0, tile), _row_in(o1, tile)] + g_specs,
                  [((t_rows, w), BF16, (tile, w), lambda i: (i, 0))])
    return ri


def _ret_gate_bwd(o0, o1, hm, g_off, dri, heads, dv, nct, tile, name):
    t_rows, w = o0.shape
    g_specs, pw = _col_pieces(hm, g_off, w, tile, nct)
    ng = len(g_specs)

    def body(o0_ref, o1_ref, d_ref, *refs):
        g_refs, do_ref, dg_ref = refs[:ng], refs[ng], refs[ng + 1]
        for hd in range(heads):
            cs = slice(hd * dv, (hd + 1) * dv)
            o = o0_ref[:, cs] + o1_ref[:, cs]
            lo = (hd * dv) % pw
            g = g_refs[(hd * dv) // pw][:, lo:lo + dv]
            d = d_ref[:, cs]
            rr = _rms(o)
            on = o * rr
            sg = _sigmoid(g)
            dg_ref[:, cs] = d * on * (sg * (1.0 + g * (1.0 - sg)))
            t = d * (g * sg)
            do_ref[:, cs] = rr * t - on * (rr * jnp.mean(t * on, axis=-1, keepdims=True))

    do, dg = _rows(name, body, t_rows // tile, [_row_in(o0, tile), _row_in(o1, tile), _row_in(dri, tile)] + g_specs,
                   [((t_rows, w), F32, (tile, w), lambda i: (i, 0)), ((t_rows, w), F32, (tile, w), lambda i: (i, 0))])
    return do, dg


def _merge_fwd(gab, rb, hm, gs_off, nct, tile, name):
    t_rows, d = rb.shape
    specs, pw = _col_pieces(hm, gs_off, 2 * d, tile, nct, unit=d)
    npc = d // pw

    def body(gab_ref, rb_ref, *refs):
        gs_refs, gr_refs, m_ref = refs[:npc], refs[npc:2 * npc], refs[2 * npc]
        for p in range(npc):
            cs = slice(p * pw, (p + 1) * pw)
            ga = gab_ref[:, cs]
            gb = gab_ref[:, d + p * pw:d + (p + 1) * pw]
            m_ref[:, cs] = (_sigmoid(gs_refs[p][...]) * (ga * _sigmoid(gb))
                            + _sigmoid(gr_refs[p][...]) * rb_ref[:, cs]).astype(m_ref.dtype)

    (mg,) = _rows(name, body, t_rows // tile, [_row_in(gab, tile), _row_in(rb, tile)] + specs,
                  [((t_rows, d), BF16, (tile, d), lambda i: (i, 0))])
    return mg


def _merge_bwd(gab, rb, hm, gs_off, dm, nct, tile, name):
    t_rows, d = rb.shape
    specs, pw = _col_pieces(hm, gs_off, 2 * d, tile, nct, unit=d)
    npc = d // pw

    def body(gab_ref, rb_ref, dm_ref, *refs):
        gs_refs, gr_refs = refs[:npc], refs[npc:2 * npc]
        dgab_ref, drb_ref, dgs_ref, dgr_ref = refs[2 * npc:]
        for p in range(npc):
            cs = slice(p * pw, (p + 1) * pw)
            cs2 = slice(d + p * pw, d + (p + 1) * pw)
            ga = gab_ref[:, cs]
            gb = gab_ref[:, cs2]
            dmm = dm_ref[:, cs]
            ss = _sigmoid(gs_refs[p][...])
            sr = _sigmoid(gr_refs[p][...])
            sb = _sigmoid(gb)
            dbr = dmm * ss
            dgab_ref[:, cs] = (dbr * sb).astype(dgab_ref.dtype)
            dgab_ref[:, cs2] = (dbr * ga * sb * (1.0 - sb)).astype(dgab_ref.dtype)
            drb_ref[:, cs] = (dmm * sr).astype(drb_ref.dtype)
            dgs_ref[:, cs] = dmm * (ga * sb) * ss * (1.0 - ss)
            dgr_ref[:, cs] = dmm * rb_ref[:, cs] * sr * (1.0 - sr)

    return _rows(name, body, t_rows // tile, [_row_in(gab, tile), _row_in(rb, tile), _row_in(dm, tile)] + specs,
                 [((t_rows, 2 * d), BF16, (tile, 2 * d), lambda i: (i, 0)), ((t_rows, d), BF16, (tile, d), lambda i: (i, 0)),
                  ((t_rows, d), F32, (tile, d), lambda i: (i, 0)), ((t_rows, d), F32, (tile, d), lambda i: (i, 0))])


def _assemble_dhm(dus, dq0, dq1, dk0, dk1, dv0, dv1, dg, dgs, dgr, nct, tile, name):
    r, s = dus.shape
    qk = dq0.shape[1]
    vw = dv0.shape[1]
    d = dgs.shape[1]
    mi = s + 2 * qk + 2 * vw + 2 * d
    c_q, c_k, c_v, c_g, c_gs, c_gr = s, s + qk, s + 2 * qk, s + 2 * qk + vw, s + 2 * qk + 2 * vw, s + 2 * qk + 2 * vw + d

    def body(dus_ref, dq0_ref, dq1_ref, dk0_ref, dk1_ref, dv0_ref, dv1_ref, dg_ref, dgs_ref, dgr_ref, o_ref):
        i = pl.program_id(0)
        lat = i >= nct
        o_ref[:, :s] = dus_ref[...].astype(o_ref.dtype)
        o_ref[:, c_q:c_k] = (dq0_ref[...] + dq1_ref[...]).astype(o_ref.dtype)
        o_ref[:, c_k:c_v] = (dk0_ref[...] + dk1_ref[...]).astype(o_ref.dtype)
        o_ref[:, c_v:c_g] = (dv0_ref[...] + dv1_ref[...]).astype(o_ref.dtype)
        o_ref[:, c_g:c_gs] = jnp.where(lat, dg_ref[...], 0.0).astype(o_ref.dtype)
        o_ref[:, c_gs:c_gr] = jnp.where(lat, dgs_ref[...], 0.0).astype(o_ref.dtype)
        o_ref[:, c_gr:] = jnp.where(lat, dgr_ref[...], 0.0).astype(o_ref.dtype)

    (out,) = _rows(name, body, r // tile,
                   [_row_in(dus, tile), _row_in(dq0, tile), _row_in(dq1, tile), _row_in(dk0, tile), _row_in(dk1, tile),
                    _row_in(dv0, tile), _row_in(dv1, tile), _row_in(dg, tile, x_only_offset=nct),
                    _row_in(dgs, tile, x_only_offset=nct), _row_in(dgr, tile, x_only_offset=nct)],
                   [((r, mi), BF16, (tile, mi), lambda i: (i, 0))])
    return out


def _loss_grad(y, target, tile, name):
    t_rows, d = y.shape

    def body(y_ref, t_ref, dy_ref, l_ref):
        i = pl.program_id(0)
        e = y_ref[...] - t_ref[...]
        dy_ref[...] = e * (1.0 / d)

        @pl.when(i == 0)
        def _():
            l_ref[...] = jnp.zeros_like(l_ref)

        l_ref[0] += jnp.sum(e * e, axis=0, keepdims=True)

    return _rows(name, body, t_rows // tile, [_row_in(y, tile), _row_in(target, tile)],
                 [((t_rows, d), F32, (tile, d), lambda i: (i, 0)), ((1, 1, d), F32, (1, 1, d), lambda i: (0, 0, 0))])


def _silu_rows(v, name):
    def body(v_ref, o_ref):
        z = v_ref[...]
        o_ref[...] = z * _sigmoid(z)

    (o,) = _rows(name, body, 1, [_row_in(v, v.shape[0])], [(v.shape, F32, v.shape, lambda i: (0, 0))])
    return o


def _silu_grad_rows(v, dv, name):
    def body(v_ref, d_ref, o_ref):
        z = v_ref[...]
        sg = _sigmoid(z)
        o_ref[...] = d_ref[...] * (sg * (1.0 + z * (1.0 - sg)))

    (o,) = _rows(name, body, 1, [_row_in(v, v.shape[0]), _row_in(dv, v.shape[0])],
                 [(v.shape, F32, v.shape, lambda i: (0, 0))])
    return o


def _sum_leading(g8, name):
    n, r, c = g8.shape
    tile = _tile(r, 256, SUBLANE)

    def body(g_ref, o_ref):
        acc = g_ref[0]
        for j in range(1, n):
            acc = acc + g_ref[j]
        o_ref[...] = acc

    (o,) = _rows(name, body, r // tile, [(g8, (n, tile, c), lambda i: (0, i, 0))],
                 [((r, c), F32, (tile, c), lambda i: (i, 0))])
    return o


def _pair_sum(a, b, name):
    n, r, c = a.shape
    tile = _tile(r, 256, 16)

    def body(a_ref, b_ref, o_ref):
        o_ref[...] = (a_ref[...].astype(F32) + b_ref[...].astype(F32)).astype(o_ref.dtype)

    spec = lambda arr: (arr, (1, tile, c), lambda i: (i // (r // tile), i % (r // tile), 0))
    (o,) = _rows(name, body, n * (r // tile), [spec(a), spec(b)],
                 [((n, r, c), a.dtype, (1, tile, c), lambda i: (i // (r // tile), i % (r // tile), 0))])
    return o


def _adamw(w, m, v, gparts, name):
    r, c = w.shape
    n = gparts.shape[0]
    tile = _tile(r, 256, 16)
    c1 = 1.0 / (1.0 - ADAM_B1 ** ADAM_STEP)
    c2 = 1.0 / (1.0 - ADAM_B2 ** ADAM_STEP)

    def body(w_ref, m_ref, v_ref, g_ref, go_ref, d_ref, mo_ref, vo_ref):
        g = g_ref[0].astype(F32)
        for j in range(1, n):
            g = g + g_ref[j].astype(F32)
        mm = ADAM_B1 * m_ref[...] + (1.0 - ADAM_B1) * g
        vv = ADAM_B2 * v_ref[...] + (1.0 - ADAM_B2) * (g * g)
        go_ref[...] = g
        mo_ref[...] = mm
        vo_ref[...] = vv
        d_ref[...] = -ADAM_LR * ((mm * c1) / (jnp.sqrt(vv * c2) + ADAM_EPS) + ADAM_WD * w_ref[...])

    rs = lambda arr: _row_in(arr, tile)
    out = ((r, c), F32, (tile, c), lambda i: (i, 0))
    return _rows(name, body, r // tile, [rs(w), rs(m), rs(v), (gparts, (n, tile, c), lambda i: (0, i, 0))],
                 [out, out, out, out])


def _cmul(ar, ai, br, bi):
    return ar * br - ai * bi, ar * bi + ai * br


def _cpow(ar, ai, n):
    pr, pi = jnp.ones_like(ar), jnp.zeros_like(ar)
    br, bi = ar, ai
    while n:
        if n & 1:
            pr, pi = _cmul(pr, pi, br, bi)
        n >>= 1
        if n:
            br, bi = _cmul(br, bi, br, bi)
    return pr, pi


def _s5_scan_into(xr_ref, xi_ref, ar1, ai1, ns, fr_ref, fi_ref, hr_ref, hi_ref, reverse):
    st = ar1.shape[1]
    ar = jnp.broadcast_to(ar1, (SUBLANE, st))
    ai = jnp.broadcast_to(ai1, (SUBLANE, st))
    zero = jnp.zeros((SUBLANE, st), F32)
    zero1 = jnp.zeros((1, st), F32)

    def slab(k):
        return pl.ds(pl.multiple_of(k * SUBLANE, SUBLANE), SUBLANE)

    def pass1(j, carry):
        hr, hi = carry
        k = ns - 1 - j if reverse else j
        nr, ni = _cmul(ar, ai, hr, hi)
        return nr + xr_ref[slab(k), :], ni + xi_ref[slab(k), :]

    fr, fi = lax.fori_loop(0, ns, pass1, (zero, zero))
    fr_ref[...] = fr
    fi_ref[...] = fi
    pr, pi = _cpow(ar1, ai1, ns)
    order = list(range(N_DEV - 1, -1, -1)) if reverse else list(range(N_DEV))
    hr_ref[order[0]:order[0] + 1, :] = zero1
    hi_ref[order[0]:order[0] + 1, :] = zero1
    for a_, b_ in zip(order[:-1], order[1:]):
        cr, ci = _cmul(pr, pi, hr_ref[a_:a_ + 1, :], hi_ref[a_:a_ + 1, :])
        hr_ref[b_:b_ + 1, :] = cr + fr_ref[a_:a_ + 1, :]
        hi_ref[b_:b_ + 1, :] = ci + fi_ref[a_:a_ + 1, :]

    def pass2(j, carry):
        hr, hi = carry
        k = ns - 1 - j if reverse else j
        nr, ni = _cmul(ar, ai, hr, hi)
        nr = nr + xr_ref[slab(k), :]
        ni = ni + xi_ref[slab(k), :]
        xr_ref[slab(k), :] = nr
        xi_ref[slab(k), :] = ni
        return nr, ni

    lax.fori_loop(0, ns, pass2, (hr_ref[...], hi_ref[...]))


def _s5_specs(r, ch, st):
    u_spec = pl.BlockSpec((r, ch), lambda j: (0, j // 2))
    w_spec = pl.BlockSpec((1, ch, st), lambda j: (j, 0, 0))
    c_spec = pl.BlockSpec((1, st, ch), lambda j: (j, 0, 0))
    a_spec = pl.BlockSpec((1, 2, st), lambda j: (j, 0, 0))
    return u_spec, w_spec, c_spec, a_spec


def _s5_fwd(up, wre, wim, cre, cim, a, name):
    r, s = up.shape
    nh, ch, st = wre.shape
    ns = r // N_DEV
    u_spec, w_spec, c_spec, a_spec = _s5_specs(r, ch, st)

    def body(u_ref, wre_ref, wim_ref, cre_ref, cim_ref, a_ref, y_ref, xr, xi, fr, fi, hr, hi):
        j = pl.program_id(0)
        for rb in range(N_DEV):
            rows = slice(rb * ns, (rb + 1) * ns)
            ub = u_ref[rows, :].astype(MXU_DTYPE)
            xr[rows, :] = jnp.dot(ub, wre_ref[0].astype(MXU_DTYPE), preferred_element_type=F32)
            xi[rows, :] = jnp.dot(ub, wim_ref[0].astype(MXU_DTYPE), preferred_element_type=F32)
        _s5_scan_into(xr, xi, a_ref[0, 0:1, :], a_ref[0, 1:2, :], ns, fr, fi, hr, hi, False)
        for rb in range(N_DEV):
            rows = slice(rb * ns, (rb + 1) * ns)
            yb = (jnp.dot(xr[rows, :].astype(MXU_DTYPE), cre_ref[0].astype(MXU_DTYPE), preferred_element_type=F32)
                  - jnp.dot(xi[rows, :].astype(MXU_DTYPE), cim_ref[0].astype(MXU_DTYPE), preferred_element_type=F32))

            @pl.when(j % 2 == 0)
            def _():
                y_ref[rows, :] = yb

            @pl.when(j % 2 == 1)
            def _():
                y_ref[rows, :] += yb

    small = pltpu.VMEM((SUBLANE, st), F32)
    return pl.pallas_call(
        body, name=name, grid=(nh,), in_specs=[u_spec, w_spec, w_spec, c_spec, c_spec, a_spec],
        out_specs=pl.BlockSpec((r, ch), lambda j: (0, j // 2)), out_shape=_sds((r, s), F32),
        scratch_shapes=[pltpu.VMEM((r, st), F32), pltpu.VMEM((r, st), F32), small, small, small, small],
        compiler_params=_params(("arbitrary",)))(up, wre, wim, cre, cim, a)


def _s5_bwd(up, dyp, wre, wim, cre, cim, a, name):
    r, s = up.shape
    nh, ch, st = wre.shape
    ns = r // N_DEV
    u_spec, w_spec, c_spec, a_spec = _s5_specs(r, ch, st)
    nt = (((1,), (1,)), ((), ()))
    tn = (((0,), (0,)), ((), ()))

    def body(u_ref, dy_ref, wre_ref, wim_ref, cre_ref, cim_ref, a_ref,
             du_ref, dwre_ref, dwim_ref, dcre_ref, dcim_ref, da_ref,
             hr, hi, gr, gi, fr, fi, sr, si, er, ei):
        j = pl.program_id(0)
        wre_b = wre_ref[0].astype(MXU_DTYPE)
        wim_b = wim_ref[0].astype(MXU_DTYPE)
        cre_b = cre_ref[0].astype(MXU_DTYPE)
        cim_b = cim_ref[0].astype(MXU_DTYPE)
        for rb in range(N_DEV):
            rows = slice(rb * ns, (rb + 1) * ns)
            ub = u_ref[rows, :].astype(MXU_DTYPE)
            hr[rows, :] = jnp.dot(ub, wre_b, preferred_element_type=F32)
            hi[rows, :] = jnp.dot(ub, wim_b, preferred_element_type=F32)
        ar1, ai1 = a_ref[0, 0:1, :], a_ref[0, 1:2, :]
        _s5_scan_into(hr, hi, ar1, ai1, ns, fr, fi, sr, si, False)
        dcre = jnp.zeros((st, ch), F32)
        dcim = jnp.zeros((st, ch), F32)
        for rb in range(N_DEV):
            rows = slice(rb * ns, (rb + 1) * ns)
            dyb = dy_ref[rows, :].astype(MXU_DTYPE)
            gr[rows, :] = lax.dot_general(dyb, cre_b, nt, preferred_element_type=F32)
            gi[rows, :] = -lax.dot_general(dyb, cim_b, nt, preferred_element_type=F32)
            dcre += lax.dot_general(hr[rows, :].astype(MXU_DTYPE), dyb, tn, preferred_element_type=F32)
            dcim -= lax.dot_general(hi[rows, :].astype(MXU_DTYPE), dyb, tn, preferred_element_type=F32)
        dcre_ref[0] = dcre
        dcim_ref[0] = dcim
        _s5_scan_into(gr, gi, ar1, -ai1, ns, fr, fi, er, ei, True)

        def slab(k):
            return pl.ds(pl.multiple_of(k * SUBLANE, SUBLANE), SUBLANE)

        def acc_step(k, carry):
            acr, aci = carry
            g_r, g_i = gr[slab(k), :], gi[slab(k), :]
            p_r, p_i = hr[slab(k - 1), :], hi[slab(k - 1), :]
            return acr + g_r * p_r + g_i * p_i, aci + g_i * p_r - g_r * p_i

        g_r, g_i = gr[0:SUBLANE, :], gi[0:SUBLANE, :]
        p_r, p_i = sr[...], si[...]
        acr, aci = lax.fori_loop(1, ns, acc_step, (g_r * p_r + g_i * p_i, g_i * p_r - g_r * p_i))
        da_ref[0, 0:1, :] = jnp.sum(acr, axis=0, keepdims=True)
        da_ref[0, 1:2, :] = jnp.sum(aci, axis=0, keepdims=True)
        dwre = jnp.zeros((ch, st), F32)
        dwim = jnp.zeros((ch, st), F32)
        for rb in range(N_DEV):
            rows = slice(rb * ns, (rb + 1) * ns)
            grb = gr[rows, :].astype(MXU_DTYPE)
            gib = gi[rows, :].astype(MXU_DTYPE)
            ub = u_ref[rows, :].astype(MXU_DTYPE)
            dub = (lax.dot_general(grb, wre_b, nt, preferred_element_type=F32)
                   + lax.dot_general(gib, wim_b, nt, preferred_element_type=F32))
            dwre += lax.dot_general(ub, grb, tn, preferred_element_type=F32)
            dwim += lax.dot_general(ub, gib, tn, preferred_element_type=F32)

            @pl.when(j % 2 == 0)
            def _():
                du_ref[rows, :] = dub

            @pl.when(j % 2 == 1)
            def _():
                du_ref[rows, :] += dub

        dwre_ref[0] = dwre
        dwim_ref[0] = dwim

    small = pltpu.VMEM((SUBLANE, st), F32)
    big = pltpu.VMEM((r, st), F32)
    return pl.pallas_call(
        body, name=name, grid=(nh,), in_specs=[u_spec, u_spec, w_spec, w_spec, c_spec, c_spec, a_spec],
        out_specs=[pl.BlockSpec((r, ch), lambda j: (0, j // 2)), w_spec, w_spec, c_spec, c_spec, a_spec],
        out_shape=[_sds((r, s), F32), _sds(wre.shape, F32), _sds(wre.shape, F32), _sds(cre.shape, F32),
                   _sds(cre.shape, F32), _sds(a.shape, F32)],
        scratch_shapes=[big, big, big, big, small, small, small, small, small, small],
        compiler_params=_params(("arbitrary",)))(up, dyp, wre, wim, cre, cim, a)


def _rope(t, cos, sin):
    quarter = t.shape[1] // 4
    lane = lax.broadcasted_iota(jnp.int32, t.shape, 1)
    first = (lane // quarter) % 2 == 0
    partner = jnp.where(first, pltpu.roll(t, t.shape[1] - quarter, 1), pltpu.roll(t, quarter, 1))
    return t * cos + partner * sin


def _rope_t(d, cos, sin):
    quarter = d.shape[1] // 4
    ds_ = d * sin
    lane = lax.broadcasted_iota(jnp.int32, d.shape, 1)
    first = (lane // quarter) % 2 == 0
    partner = jnp.where(first, pltpu.roll(ds_, d.shape[1] - quarter, 1), pltpu.roll(ds_, quarter, 1))
    return d * cos + partner


def _chunk_of_step(s, nch, ncc, rev):
    if not rev:
        return s
    return jnp.where(s < ncc, ncc - 1 - s, nch + ncc - 1 - s)


def _ret_fwd(hm, cos, sin, decay, wend, win, gch, heads, dk, dv, q_off, ncc, rev, name):
    r = hm.shape[0]
    ch = RET_CHUNK
    nch = r // ch
    t_rows = r - ncc * ch
    qb, kb, vb = q_off // dk, (q_off + heads * dk) // dk, (q_off + 2 * heads * dk) // dv
    q_scale = dk ** -0.5
    nt = (((1,), (1,)), ((), ()))
    tn = (((0,), (0,)), ((), ()))
    cof = lambda s: _chunk_of_step(s, nch, ncc, rev)

    def body(q_ref, k_ref, v_ref, cos_ref, sin_ref, dec_ref, we_ref, wi_ref, g_ref, o_ref, sin_out, st):
        s = pl.program_id(1)

        @pl.when(s == 0)
        def _():
            st[...] = jnp.zeros_like(st)

        q = _rope(q_ref[...], cos_ref[...], sin_ref[...]) * q_scale
        k = _rope(k_ref[...], cos_ref[...], sin_ref[...])
        v = v_ref[...].astype(MXU_DTYPE)
        s_cur = st[...]
        sin_out[0, 0] = s_cur
        kw = (k * we_ref[0]).astype(MXU_DTYPE)
        qw = (q * wi_ref[0]).astype(MXU_DTYPE)
        scores = lax.dot_general(q.astype(MXU_DTYPE), k.astype(MXU_DTYPE), nt, preferred_element_type=F32) * dec_ref[0]
        o_ref[...] = (jnp.dot(scores.astype(MXU_DTYPE), v, preferred_element_type=F32)
                      + jnp.dot(qw, s_cur.astype(MXU_DTYPE), preferred_element_type=F32))
        st[...] = g_ref[0] * s_cur + lax.dot_general(kw, v, tn, preferred_element_type=F32)

    tab = lambda w: pl.BlockSpec((1, ch, w), lambda h, s: (h, 0, 0))
    return pl.pallas_call(
        body, name=name, grid=(heads, nch),
        in_specs=[pl.BlockSpec((ch, dk), lambda h, s: (cof(s), qb + h)),
                  pl.BlockSpec((ch, dk), lambda h, s: (cof(s), kb + h)),
                  pl.BlockSpec((ch, dv), lambda h, s: (cof(s), vb + h)),
                  pl.BlockSpec((ch, dk), lambda h, s: (cof(s), 0)),
                  pl.BlockSpec((ch, dk), lambda h, s: (cof(s), 0)),
                  tab(ch), tab(dk), tab(dk), tab(dv)],
        out_specs=[pl.BlockSpec((ch, dv), lambda h, s: (jnp.maximum(cof(s) - ncc, 0) if not rev
                                                         else jnp.where(s < ncc, nch - ncc - 1, cof(s) - ncc), h)),
                   pl.BlockSpec((1, 1, dk, dv), lambda h, s: (h, s, 0, 0))],
        out_shape=[_sds((t_rows, heads * dv), F32), _sds((heads, nch, dk, dv), F32)],
        scratch_shapes=[pltpu.VMEM((dk, dv), F32)],
        compiler_params=_params(("parallel", "arbitrary")))(hm, hm, hm, cos, sin, decay, wend, win, gch)


def _ret_bwd(hm, cos, sin, decay, wend, win, gch, s_in, do, heads, dk, dv, q_off, ncc, rev, name):
    r = hm.shape[0]
    ch = RET_CHUNK
    nch = r // ch
    qb, kb, vb = q_off // dk, (q_off + heads * dk) // dk, (q_off + 2 * heads * dk) // dv
    q_scale = dk ** -0.5
    nt = (((1,), (1,)), ((), ()))
    tn = (((0,), (0,)), ((), ()))
    cof = lambda rr: _chunk_of_step(nch - 1 - rr, nch, ncc, rev)

    def body(q_ref, k_ref, v_ref, cos_ref, sin_ref, dec_ref, we_ref, wi_ref, g_ref, sin_ref2, do_ref,
             dq_ref, dk_ref, dv_ref, ddec_ref, dwe_ref, dwi_ref, dg_ref, dst):
        rr = pl.program_id(1)
        n = cof(rr)

        @pl.when(rr == 0)
        def _():
            dst[...] = jnp.zeros_like(dst)
            ddec_ref[...] = jnp.zeros_like(ddec_ref)
            dwe_ref[...] = jnp.zeros_like(dwe_ref)
            dwi_ref[...] = jnp.zeros_like(dwi_ref)
            dg_ref[...] = jnp.zeros_like(dg_ref)

        cos_, sin_ = cos_ref[...], sin_ref[...]
        q = _rope(q_ref[...], cos_, sin_) * q_scale
        k = _rope(k_ref[...], cos_, sin_)
        v = v_ref[...].astype(MXU_DTYPE)
        qb_, kb_ = q.astype(MXU_DTYPE), k.astype(MXU_DTYPE)
        kw = (k * we_ref[0]).astype(MXU_DTYPE)
        qw = (q * wi_ref[0]).astype(MXU_DTYPE)
        sraw = lax.dot_general(qb_, kb_, nt, preferred_element_type=F32)
        scores = (sraw * dec_ref[0]).astype(MXU_DTYPE)
        d_o = jnp.where(n >= ncc, do_ref[...], 0.0).astype(MXU_DTYPE)
        s_n = sin_ref2[0, 0]
        s_nb = s_n.astype(MXU_DTYPE)
        ds1 = dst[...]
        ds1b = ds1.astype(MXU_DTYPE)
        dsc = lax.dot_general(d_o, v, nt, preferred_element_type=F32)
        dsr = (dsc * dec_ref[0]).astype(MXU_DTYPE)
        ddec_ref[0] += dsc * sraw
        t1 = lax.dot_general(d_o, s_nb, nt, preferred_element_type=F32)
        dq_r = jnp.dot(dsr, kb_, preferred_element_type=F32) + t1 * wi_ref[0]
        dwi_ref[0] += t1 * q
        t2 = lax.dot_general(v, ds1b, nt, preferred_element_type=F32)
        dk_r = lax.dot_general(dsr, qb_, tn, preferred_element_type=F32) + t2 * we_ref[0]
        dwe_ref[0] += t2 * k
        dv_ref[...] = (lax.dot_general(scores, d_o, tn, preferred_element_type=F32)
                       + jnp.dot(kw, ds1b, preferred_element_type=F32))
        dg_ref[0] += ds1 * s_n
        dst[...] = g_ref[0] * ds1 + lax.dot_general(qw, d_o, tn, preferred_element_type=F32)
        dq_ref[...] = _rope_t(dq_r, cos_, sin_) * q_scale
        dk_ref[...] = _rope_t(dk_r, cos_, sin_)

    tab = lambda w: pl.BlockSpec((1, ch, w), lambda h, rr: (h, 0, 0))
    return pl.pallas_call(
        body, name=name, grid=(heads, nch),
        in_specs=[pl.BlockSpec((ch, dk), lambda h, rr: (cof(rr), qb + h)),
                  pl.BlockSpec((ch, dk), lambda h, rr: (cof(rr), kb + h)),
                  pl.BlockSpec((ch, dv), lambda h, rr: (cof(rr), vb + h)),
                  pl.BlockSpec((ch, dk), lambda h, rr: (cof(rr), 0)),
                  pl.BlockSpec((ch, dk), lambda h, rr: (cof(rr), 0)),
                  tab(ch), tab(dk), tab(dk), tab(dv),
                  pl.BlockSpec((1, 1, dk, dv), lambda h, rr: (h, nch - 1 - rr, 0, 0)),
                  pl.BlockSpec((ch, dv), lambda h, rr: (jnp.maximum(cof(rr) - ncc, 0), h))],
        out_specs=[pl.BlockSpec((ch, dk), lambda h, rr: (cof(rr), h)),
                   pl.BlockSpec((ch, dk), lambda h, rr: (cof(rr), h)),
                   pl.BlockSpec((ch, dv), lambda h, rr: (cof(rr), h)),
                   tab(ch), tab(dk), tab(dk), tab(dv)],
        out_shape=[_sds((r, heads * dk), F32), _sds((r, heads * dk), F32), _sds((r, heads * dv), F32),
                   _sds(decay.shape, F32), _sds(wend.shape, F32), _sds(win.shape, F32), _sds(gch.shape, F32)],
        scratch_shapes=[pltpu.VMEM((dk, dv), F32)],
        compiler_params=_params(("parallel", "arbitrary")))(hm, hm, hm, cos, sin, decay, wend, win, gch, s_in, do)


_HBM = pl.BlockSpec(memory_space=pltpu.HBM)
_MESH = pl.DeviceIdType.MESH


def _axis_slice(ref, axis, start, size):
    idx = [slice(None)] * len(ref.shape)
    idx[axis] = pl.ds(start, size)
    return ref.at[tuple(idx)]


def _all_gather(shard, axis, name):
    m = shard.shape[axis]
    out_shape = list(shard.shape)
    out_shape[axis] = N_DEV * m

    def body(x_ref, out_ref, send_sems, recv_sems, local_sem):
        x, y, c = lax.axis_index("x"), lax.axis_index("y"), lax.axis_index("c")
        me, sibling = (x, y, c), (x, y, 1 - c)
        chips = [(1 - x, y), (x, 1 - y), (1 - x, 1 - y)]

        def block(px, py, pc):
            return _axis_slice(out_ref, axis, (4 * px + 2 * py + pc) * m, m)

        def copy(k, blk, to, src=None):
            return pltpu.make_async_remote_copy(
                src_ref=block(*blk) if src is None else src, dst_ref=block(*blk), send_sem=send_sems.at[k],
                recv_sem=recv_sems.at[k], device_id=to, device_id_type=_MESH)

        mine = pltpu.make_async_copy(x_ref, block(*me), local_sem)
        mine.start()
        first = [copy(0, me, sibling, src=x_ref)]
        first += [copy(1 + j, me, (*chip, c), src=x_ref) for j, chip in enumerate(chips)]
        for cp in first:
            cp.start()
        passed = [copy(4 + j, (*chip, c), sibling) for j, chip in enumerate(chips)]
        for j, chip in enumerate(chips):
            copy(1 + j, (*chip, c), me).wait_recv()
            passed[j].start()
        copy(0, sibling, me).wait_recv()
        for j, chip in enumerate(chips):
            copy(4 + j, (*chip, 1 - c), me).wait_recv()
        for cp in first + passed:
            cp.wait_send()
        mine.wait()

    return pl.pallas_call(
        body, name=name, out_shape=_sds(out_shape, shard.dtype), in_specs=[_HBM], out_specs=_HBM,
        scratch_shapes=[pltpu.SemaphoreType.DMA((7,)), pltpu.SemaphoreType.DMA((7,)), pltpu.SemaphoreType.DMA(())],
    )(shard)


def _rs_sibling(g, axis, name):
    m = g.shape[axis] // N_DEV
    blk_shape = list(g.shape)
    blk_shape[axis] = m
    n_chips = N_DEV // 2

    def body(g_ref, mine_ref, recv_ref, send_sems, recv_sems, local_sems):
        x, y, c = lax.axis_index("x"), lax.axis_index("y"), lax.axis_index("c")
        sibling = (x, y, 1 - c)
        keep = [pltpu.make_async_copy(_axis_slice(g_ref, axis, (2 * q + c) * m, m), mine_ref.at[q], local_sems.at[q])
                for q in range(n_chips)]
        send = [pltpu.make_async_remote_copy(
            src_ref=_axis_slice(g_ref, axis, (2 * q + 1 - c) * m, m), dst_ref=recv_ref.at[q],
            send_sem=send_sems.at[q], recv_sem=recv_sems.at[q], device_id=sibling, device_id_type=_MESH)
            for q in range(n_chips)]
        for cp in keep + send:
            cp.start()
        for cp in send:
            cp.wait_recv()
        for cp in send:
            cp.wait_send()
        for cp in keep:
            cp.wait()

    out = _sds([n_chips] + blk_shape, g.dtype)
    return pl.pallas_call(
        body, name=name, out_shape=[out, out], in_specs=[_HBM], out_specs=[_HBM, _HBM],
        scratch_shapes=[pltpu.SemaphoreType.DMA((n_chips,)), pltpu.SemaphoreType.DMA((n_chips,)),
                        pltpu.SemaphoreType.DMA((n_chips,))],
    )(g)


def _rs_chips(p, name):
    n_chips = p.shape[0]

    def body(p_ref, out_ref, send_sems, recv_sems, local_sem):
        x, y, c = lax.axis_index("x"), lax.axis_index("y"), lax.axis_index("c")
        my_q = 2 * x + y
        chips = [(1 - x, y), (x, 1 - y), (1 - x, 1 - y)]
        own = pltpu.make_async_copy(p_ref.at[my_q], out_ref.at[my_q], local_sem)
        own.start()
        send = [pltpu.make_async_remote_copy(
            src_ref=p_ref.at[2 * cx + cy], dst_ref=out_ref.at[my_q], send_sem=send_sems.at[j],
            recv_sem=recv_sems.at[j], device_id=(cx, cy, c), device_id_type=_MESH)
            for j, (cx, cy) in enumerate(chips)]
        for cp in send:
            cp.start()
        for j, (cx, cy) in enumerate(chips):
            pltpu.make_async_remote_copy(
                src_ref=p_ref.at[my_q], dst_ref=out_ref.at[2 * cx + cy], send_sem=send_sems.at[j],
                recv_sem=recv_sems.at[j], device_id=(cx, cy, c), device_id_type=_MESH).wait_recv()
        for cp in send:
            cp.wait_send()
        own.wait()

    return pl.pallas_call(
        body, name=name, out_shape=_sds(p.shape, p.dtype), in_specs=[_HBM], out_specs=_HBM,
        scratch_shapes=[pltpu.SemaphoreType.DMA((n_chips - 1,)), pltpu.SemaphoreType.DMA((n_chips - 1,)),
                        pltpu.SemaphoreType.DMA(())],
    )(p)


def _reduce_scatter(g, axis, name):
    mine, recv = _rs_sibling(g, axis, name + "_d2d")
    shp = mine.shape
    flat = lambda a: a.reshape(shp[0], -1, shp[-1])
    p = _pair_sum(flat(mine), flat(recv), name + "_pair").reshape(shp)
    return _rs_chips(p, name + "_ici")


def _s5_tables(lam_re, lam_im, log_step, b_re, b_im, c_re, c_im):
    g, p, cg = b_re.shape
    step = jnp.exp(log_step)[:, None]
    mag = jnp.exp(lam_re * step)
    a_re, a_im = mag * jnp.cos(lam_im * step), mag * jnp.sin(lam_im * step)
    den = lam_re * lam_re + lam_im * lam_im
    num_re, num_im = a_re - 1.0, a_im
    k_re = (num_re * lam_re + num_im * lam_im) / den
    k_im = (num_im * lam_re - num_re * lam_im) / den
    bb_re = k_re[..., None] * b_re - k_im[..., None] * b_im
    bb_im = k_re[..., None] * b_im + k_im[..., None] * b_re
    gt = g // SSM_TILE_GROUPS
    hg = SSM_HALF_GROUPS
    eye = jnp.eye(SSM_TILE_GROUPS, dtype=F32).reshape(SSM_TILE_GROUPS, 2, hg)

    def pack_b(bb):
        w = jnp.einsum("jhqpc,ghq->jhgcqp", bb.reshape(gt, 2, hg, p, cg), eye)
        return w.reshape(gt * 2, SSM_TILE_GROUPS * cg, hg * p)

    def pack_c(cc):
        w = jnp.einsum("jhqcp,ghq->jhqpgc", cc.reshape(gt, 2, hg, cg, p), eye)
        return w.reshape(gt * 2, hg * p, SSM_TILE_GROUPS * cg)

    a = jnp.stack([a_re.reshape(gt * 2, hg * p), a_im.reshape(gt * 2, hg * p)], axis=1)
    return pack_b(bb_re), pack_b(bb_im), pack_c(c_re), pack_c(c_im), a


def _ret_tables(decay_logit, rev, dk, dv):
    ch = RET_CHUNK
    h = decay_logit.shape[0]
    lg = jax.nn.log_sigmoid(decay_logit)[:, None]
    pos = jnp.arange(ch, dtype=F32)
    diff = pos[:, None] - pos[None, :]
    if rev:
        diff = -diff
        mask = diff > 0
        w_end = jnp.exp(lg * pos)
        w_in = jnp.exp(lg * (ch - pos))
    else:
        mask = diff >= 0
        w_end = jnp.exp(lg * (ch - 1.0 - pos))
        w_in = jnp.exp(lg * (pos + 1.0))
    decay = jnp.where(mask, jnp.exp(lg[:, :, None] * jnp.where(mask, diff, 0.0)), 0.0)
    g_chunk = jnp.exp(lg[:, 0] * ch)
    return (decay, jnp.broadcast_to(w_end[:, :, None], (h, ch, dk)), jnp.broadcast_to(w_in[:, :, None], (h, ch, dk)),
            jnp.broadcast_to(g_chunk[:, None, None], (h, dk, dv)))


def _rope_tables(t_rows, ncc, dk):
    quarter = dk // 4
    idx = np.arange(t_rows)
    row, col = idx // GRID_W, idx % GRID_W
    inv = ROPE_BASE ** (-np.arange(quarter, dtype=np.float32) / quarter)
    ang_r = row.astype(np.float32)[:, None] * inv
    ang_c = col.astype(np.float32)[:, None] * inv
    ang_r, ang_c = jnp.asarray(ang_r, F32), jnp.asarray(ang_c, F32)
    cos = jnp.concatenate([jnp.cos(ang_r), jnp.cos(ang_r), jnp.cos(ang_c), jnp.cos(ang_c)], axis=1)
    sin = jnp.concatenate([-jnp.sin(ang_r), jnp.sin(ang_r), -jnp.sin(ang_c), jnp.sin(ang_c)], axis=1)
    n_ctx = ncc * RET_CHUNK
    cos = jnp.concatenate([jnp.ones((n_ctx, dk), F32), cos], axis=0)
    sin = jnp.concatenate([jnp.zeros((n_ctx, dk), F32), sin], axis=0)
    return cos, sin


def _to_scan_layout(u, n_ctx, rev):
    r, w = u.shape
    if rev:
        u = jnp.concatenate([u[:n_ctx][::-1], u[n_ctx:][::-1]], axis=0)
    return u.reshape(N_DEV, r // N_DEV, w).transpose(1, 0, 2).reshape(r, w)


def _from_scan_layout(yp, n_ctx, rev):
    r, w = yp.shape
    y = yp.reshape(r // N_DEV, N_DEV, w).transpose(1, 0, 2).reshape(r, w)
    if rev:
        y = jnp.concatenate([y[:n_ctx][::-1], y[n_ctx:][::-1]], axis=0)
    return y


def _pack(parts, width):
    flat = jnp.concatenate([p.reshape(-1).astype(F32) for p in parts])
    n = flat.shape[0]
    quantum = width * SUBLANE
    padded = -(-n // quantum) * quantum
    return jnp.pad(flat, (0, padded - n)).reshape(padded // width, width)


def _unpack(flat2d, shapes):
    flat = flat2d.reshape(-1)
    out, off = [], 0
    for shp in shapes:
        n = int(np.prod(shp))
        out.append(flat[off:off + n].reshape(shp))
        off += n
    return out


def kernel(x, c, ctx, c_ctx, ada_w, ada_b, norm_g, ffn_w_in, ffn_w_out, mix_w_in, ssm_lam_re, ssm_lam_im, ssm_log_step, ssm_b_re, ssm_b_im, ssm_c_re, ssm_c_im, ssm_d, ssm_glu_w, ret_decay_logit, ret_w_proj, mix_w_out, loss_target, m_c_ctx, m_ada_w, m_ada_b, m_norm_g, m_ffn_w_in, m_ffn_w_out, m_mix_w_in, m_ssm_lam_re, m_ssm_lam_im, m_ssm_log_step, m_ssm_b_re, m_ssm_b_im, m_ssm_c_re, m_ssm_c_im, m_ssm_d, m_ssm_glu_w, m_ret_decay_logit, m_ret_w_proj, m_mix_w_out, v_c_ctx, v_ada_w, v_ada_b, v_norm_g, v_ffn_w_in, v_ffn_w_out, v_mix_w_in, v_ssm_lam_re, v_ssm_lam_im, v_ssm_log_step, v_ssm_b_re, v_ssm_b_im, v_ssm_c_re, v_ssm_c_im, v_ssm_d, v_ssm_glu_w, v_ret_decay_logit, v_ret_w_proj, v_mix_w_out):
    t_rows, d = x.shape[1], x.shape[2]
    n_ctx = ctx.shape[1]
    r = n_ctx + t_rows
    ssm_w = ssm_d.shape[1]
    heads = ret_decay_logit.shape[2]
    mi = mix_w_in.shape[2] * N_DEV
    dk = (mi - ssm_w - 2 * d) // (6 * heads)
    dv = 2 * dk
    qk_w, v_w = heads * dk, heads * dv
    q_off = ssm_w
    ncc = n_ctx // RET_CHUNK
    tile = n_ctx
    nct = 1
    wide_tile = _tile(n_ctx, 128, 16)
    assert r % (N_DEV * SUBLANE) == 0 and n_ctx % RET_CHUNK == 0 and t_rows % tile == 0
    me = 4 * lax.axis_index("x") + 2 * lax.axis_index("y") + lax.axis_index("c")
    g_off = ssm_w + 2 * qk_w + v_w
    gs_off = g_off + v_w

    bf = lambda w: w.astype(BF16)
    w_in1 = _all_gather(bf(ffn_w_in[0, 0]), 1, "ag_ffn1_in")
    w_out1 = _all_gather(bf(ffn_w_out[0, 0]), 0, "ag_ffn1_out")
    w_mix = _all_gather(bf(mix_w_in[0]), 1, "ag_mix_in")
    w_glu = _all_gather(bf(ssm_glu_w[0]), 1, "ag_glu")
    w_rp = _all_gather(bf(ret_w_proj[0]), 0, "ag_ret_proj")
    w_mo = _all_gather(bf(mix_w_out[0]), 0, "ag_mix_out")
    w_in2 = _all_gather(bf(ffn_w_in[0, 1]), 1, "ag_ffn2_in")
    w_out2 = _all_gather(bf(ffn_w_out[0, 1]), 0, "ag_ffn2_out")

    ng_cols = norm_g.shape[2]
    small0 = _pack([c[0], norm_g[0]], d)
    small0_all = _all_gather(small0, 0, "ag_cond").reshape(N_DEV, -1)
    c_all = small0_all[:, :d]
    g_full = small0_all[:, d:d + 6 * ng_cols].reshape(N_DEV, 6, ng_cols).transpose(1, 0, 2).reshape(6, d)
    g6 = g_full.reshape(6, 1, d)
    cc = jnp.concatenate([c_all, c_ctx[None, :], jnp.zeros((2 * SUBLANE - N_DEV - 1, d), F32)], axis=0)
    sc = _silu_rows(cc, "ada_silu")
    na = ada_w.shape[2]
    a_loc = _mm(sc, ada_w[0], "nn", F32, "ada_fwd", tm=16, tn=na, tk=512)
    a_all = _all_gather(a_loc, 0, "ag_ada").reshape(N_DEV, 2 * SUBLANE, na)
    ada_x = lax.dynamic_index_in_dim(a_all, me, axis=1, keepdims=False).reshape(9 * d) + ada_b[0]
    ada_c = a_all[:, N_DEV, :].reshape(9 * d) + ada_b[0]
    mods = jnp.stack([ada_c.reshape(9, d), ada_x.reshape(9, d)]).reshape(18, 1, d)

    xin = jnp.concatenate([ctx[0], x[0]], axis=0)
    u1 = _ada_pre_fwd(xin, g6, mods, 0, 0, nct, tile, "pre1")
    h1 = _mm(u1, w_in1, "nn", F32, "ffn1_in", tm=544)
    a1 = _swiglu_fwd(h1, wide_tile, "swiglu1")
    o1 = _mm(a1, w_out1, "nn", F32, "ffn1_out", tm=544, tn=d, tk=1408)
    x1 = _ada_post_fwd(xin, o1, g6, mods, 1, 0, 0.5, nct, tile, "post1")
    u2 = _ada_pre_fwd(x1, g6, mods, 2, 1, nct, tile, "pre2")
    hm = _mm(u2, w_mix, "nn", F32, "mix_in", tm=544)

    us = hm[:, :ssm_w]
    dskip = ssm_d.reshape(1, 1, ssm_w)
    s5_tabs, s5_vjps, ups, y_dirs = [], [], [], []
    for dr in range(2):
        prm = (ssm_lam_re[0, dr], ssm_lam_im[0, dr], ssm_log_step[0, dr], ssm_b_re[0, dr], ssm_b_im[0, dr],
               ssm_c_re[0, dr], ssm_c_im[0, dr])
        tabs, vjp_fn = jax.vjp(_s5_tables, *prm)
        up = _to_scan_layout(us, n_ctx, dr == 1)
        yp = _s5_fwd(up, *tabs, "s5_fwd%d" % dr)
        s5_tabs.append(tabs)
        s5_vjps.append(vjp_fn)
        ups.append(up)
        y_dirs.append(_from_scan_layout(yp, n_ctx, dr == 1)[n_ctx:])
    a_ssm = _ssm_out_fwd(y_dirs[0], y_dirs[1], hm, dskip, nct, tile, "ssm_out")
    gab = _mm(a_ssm, w_glu, "nn", F32, "glu", tm=512, tn=2048, tk=ssm_w)

    cos, sin = _rope_tables(t_rows, ncc, dk)
    ret_tabs, ret_vjps, o_dirs, s_ins = [], [], [], []
    for dr in range(2):
        tabs, vjp_fn = jax.vjp(functools.partial(_ret_tables, rev=dr == 1, dk=dk, dv=dv), ret_decay_logit[0, dr])
        o_d, s_in = _ret_fwd(hm, cos, sin, *tabs, heads, dk, dv, q_off, ncc, dr == 1, "ret_fwd%d" % dr)
        ret_tabs.append(tabs)
        ret_vjps.append(vjp_fn)
        o_dirs.append(o_d)
        s_ins.append(s_in)
    ret_in = _ret_gate_fwd(o_dirs[0], o_dirs[1], hm, g_off, heads, dv, nct, tile, "ret_gate")
    rb = _mm(ret_in, w_rp, "nn", F32, "ret_proj", tm=512, tn=d, tk=v_w)
    merged = _merge_fwd(gab, rb, hm, gs_off, nct, tile, "merge")
    mix = _mm(merged, w_mo, "nn", F32, "mix_out", tm=512, tn=d, tk=d)
    x1x = x1[n_ctx:]
    x2 = _ada_post_fwd(x1x, mix, g6, mods, 3, 1, 1.0, 0, tile, "post2")
    u3 = _ada_pre_fwd(x2, g6, mods, 4, 2, 0, tile, "pre3")
    h3 = _mm(u3, w_in2, "nn", F32, "ffn2_in", tm=512)
    a3 = _swiglu_fwd(h3, wide_tile, "swiglu2")
    o3 = _mm(a3, w_out2, "nn", F32, "ffn2_out", tm=512, tn=d, tk=1408)
    x3 = _ada_post_fwd(x2, o3, g6, mods, 5, 2, 0.5, 0, tile, "post3")
    dy, lcols = _loss_grad(x3, loss_target[0], tile, "loss")
    loss = lax.psum(0.5 * jnp.sum(lcols) / d, MESH_AXES)

    dg6 = [None] * 6
    dmod = {}

    def add_mod(sel_rows, k, val):
        for sel, row in sel_rows:
            dmod[(sel, k)] = dmod.get((sel, k), 0.0) + val[row, 0]

    both, lat = [(0, 0), (1, 1)], [(1, 0)]
    do3, dg6[5], dgt = _ada_post_bwd(dy, o3, g6, mods, 5, 2, 0.5, 0, 1, tile, "post3_bwd")
    add_mod(lat, 8, dgt)
    da3 = _mm(do3, w_out2, "nt", F32, "ffn2_out_dx", tm=512, tn=1408, tk=d)
    gw_out2 = _mm(a3, do3, "tn", BF16, "ffn2_out_dw", tm=1408, tn=d, tk=512)
    dh3 = _swiglu_bwd(h3, da3, wide_tile, "swiglu2_bwd")
    du3 = _mm(dh3, w_in2, "nt", F32, "ffn2_in_dx", tm=512, tn=d, tk=1408)
    gw_in2 = _mm(u3, dh3, "tn", BF16, "ffn2_in_dw", tm=d, tn=1408, tk=512)
    dx2, dg6[4], dsh, dsc = _ada_pre_bwd(x2, du3, dy, g6, mods, 4, 2, 0, 1, tile, "pre3_bwd")
    add_mod(lat, 6, dsh)
    add_mod(lat, 7, dsc)
    dmix, dg6[3], dgt = _ada_post_bwd(dx2, mix, g6, mods, 3, 1, 1.0, 0, 1, tile, "post2_bwd")
    add_mod(lat, 5, dgt)
    dmerged = _mm(dmix, w_mo, "nt", F32, "mix_out_dx", tm=512, tn=d, tk=d)
    gw_mo = _mm(merged, dmix, "tn", BF16, "mix_out_dw", tm=d, tn=d, tk=512)
    dgab, drb, dgs, dgr = _merge_bwd(gab, rb, hm, gs_off, dmerged, nct, tile, "merge_bwd")
    da_ssm = _mm(dgab, w_glu, "nt", F32, "glu_dx", tm=512, tn=ssm_w, tk=2 * d)
    gw_glu = _mm(a_ssm, dgab, "tn", BF16, "glu_dw", tm=ssm_w, tn=2 * d, tk=512)
    dret_in = _mm(drb, w_rp, "nt", F32, "ret_proj_dx", tm=512, tn=v_w, tk=d)
    gw_rp = _mm(ret_in, drb, "tn", BF16, "ret_proj_dw", tm=v_w, tn=d, tk=512)
    d_o, dg_gate = _ret_gate_bwd(o_dirs[0], o_dirs[1], hm, g_off, dret_in, heads, dv, nct, tile, "ret_gate_bwd")
    dy_ssm, dus_direct, d_dskip = _ssm_out_bwd(y_dirs[0], y_dirs[1], hm, dskip, da_ssm, nct, tile, "ssm_out_bwd")
    dqkv, g_decay = [], []
    for dr in range(2):
        outs = _ret_bwd(hm, cos, sin, *ret_tabs[dr], s_ins[dr], d_o, heads, dk, dv, q_off, ncc, dr == 1,
                        "ret_bwd%d" % dr)
        dqkv.append(outs[:3])
        (gl,) = ret_vjps[dr](tuple(outs[3:]))
        g_decay.append(gl)
    dus = jnp.concatenate([jnp.zeros((n_ctx, ssm_w), F32), dus_direct], axis=0)
    dy_full = jnp.concatenate([jnp.zeros((n_ctx, ssm_w), F32), dy_ssm], axis=0)
    g_s5 = []
    for dr in range(2):
        dyp = _to_scan_layout(dy_full, n_ctx, dr == 1)
        outs = _s5_bwd(ups[dr], dyp, *s5_tabs[dr], "s5_bwd%d" % dr)
        dus = dus + _from_scan_layout(outs[0], n_ctx, dr == 1)
        g_s5.append(s5_vjps[dr](tuple(outs[1:])))
    dhm = _assemble_dhm(dus, dqkv[0][0], dqkv[1][0], dqkv[0][1], dqkv[1][1], dqkv[0][2], dqkv[1][2],
                        dg_gate, dgs, dgr, n_ctx // wide_tile, wide_tile, "assemble_dhm")
    du2 = _mm(dhm, w_mix, "nt", F32, "mix_in_dx", tm=544, tn=d, tk=1408)
    gw_mix = _mm(u2, dhm, "tn", BF16, "mix_in_dw", tm=d, tn=1408, tk=544)
    dx1, dg6[2], dsh, dsc = _ada_pre_bwd(x1, du2, dx2, g6, mods, 2, 1, nct, 2, tile, "pre2_bwd", dres_x_only=True)
    add_mod(both, 3, dsh)
    add_mod(both, 4, dsc)
    do1, dg6[1], dgt = _ada_post_bwd(dx1, o1, g6, mods, 1, 0, 0.5, nct, 2, tile, "post1_bwd")
    add_mod(both, 2, dgt)
    da1 = _mm(do1, w_out1, "nt", F32, "ffn1_out_dx", tm=544, tn=1408, tk=d)
    gw_out1 = _mm(a1, do1, "tn", BF16, "ffn1_out_dw", tm=1408, tn=d, tk=544)
    dh1 = _swiglu_bwd(h1, da1, wide_tile, "swiglu1_bwd")
    du1 = _mm(dh1, w_in1, "nt", F32, "ffn1_in_dx", tm=544, tn=d, tk=1408)
    gw_in1 = _mm(u1, dh1, "tn", BF16, "ffn1_in_dw", tm=d, tn=1408, tk=544)
    dxin, dg6[0], dsh, dsc = _ada_pre_bwd(xin, du1, dx1, g6, mods, 0, 0, nct, 2, tile, "pre1_bwd")
    add_mod(both, 0, dsh)
    add_mod(both, 1, dsc)
    grad_x = dxin[n_ctx:][None]

    zero_d = jnp.zeros((d,), F32)
    d_ada_x = jnp.stack([dmod.get((1, k), zero_d) for k in range(9)]).reshape(9 * d)
    d_ada_c = jnp.stack([dmod.get((0, k), zero_d) for k in range(9)]).reshape(9 * d)
    dg_full = jnp.stack([g[0, 0] for g in dg6])
    s5_names = 7
    s5_stack = [jnp.stack([g_s5[0][i], g_s5[1][i]]) for i in range(s5_names)]
    small_parts = [d_ada_x, d_ada_c, dg_full] + s5_stack + [d_dskip, jnp.stack(g_decay)]
    small_shapes = [p.shape for p in small_parts]
    packed = _pack(small_parts, 1024)
    gathered = _all_gather(packed, 0, "ag_small_grads").reshape(N_DEV, -1, 1024)
    summed = _sum_leading(gathered, "sum_small_grads")
    sums = _unpack(summed, small_shapes)
    sum_dx, sum_dc, sum_dg = sums[0], sums[1], sums[2]
    grad_ada_b = (sum_dx + sum_dc)[None]
    dx_rows = gathered.reshape(N_DEV, -1)[:, :9 * d]
    col0 = me * na
    da_rows = jnp.concatenate([lax.dynamic_slice_in_dim(dx_rows, col0, na, axis=1),
                               lax.dynamic_slice_in_dim(sum_dc[None], col0, na, axis=1),
                               jnp.zeros((2 * SUBLANE - N_DEV - 1, na), F32)], axis=0)
    grad_ada_w = _mm(sc, da_rows, "tn", F32, "ada_dw", tm=512, tn=na, tk=16)
    d_sc = _mm(da_rows, ada_w[0], "nt", F32, "ada_dx", tm=16, tn=512, tk=na)
    d_sc_all = _all_gather(jnp.broadcast_to(d_sc[N_DEV:N_DEV + 1], (SUBLANE, d)), 0, "ag_dctx")
    d_sc_sum = _sum_leading(d_sc_all.reshape(N_DEV, SUBLANE, d), "sum_dctx")
    grad_c_ctx = _silu_grad_rows(jnp.broadcast_to(c_ctx[None], (SUBLANE, d)), d_sc_sum, "ctx_silu_bwd")[0]
    grad_norm_g = lax.dynamic_slice_in_dim(sum_dg, me * ng_cols, ng_cols, axis=1)[None]

    def big_update(w2d, m2d, v2d, gfull, axis, name):
        parts = _reduce_scatter(gfull, axis, "rs_" + name)
        return _adamw(w2d, m2d, v2d, parts.reshape(parts.shape[0], *w2d.shape), "adamw_" + name)

    upd = {}
    res_in = [big_update(ffn_w_in[0, l], m_ffn_w_in[0, l], v_ffn_w_in[0, l], gw, 1, "ffn%d_in" % (l + 1))
              for l, gw in enumerate([gw_in1, gw_in2])]
    upd["ffn_w_in"] = [jnp.stack([res_in[0][i], res_in[1][i]])[None] for i in range(4)]
    res_out = [big_update(ffn_w_out[0, l], m_ffn_w_out[0, l], v_ffn_w_out[0, l], gw, 0, "ffn%d_out" % (l + 1))
               for l, gw in enumerate([gw_out1, gw_out2])]
    upd["ffn_w_out"] = [jnp.stack([res_out[0][i], res_out[1][i]])[None] for i in range(4)]
    upd["mix_w_in"] = [o[None] for o in big_update(mix_w_in[0], m_mix_w_in[0], v_mix_w_in[0], gw_mix, 1, "mix_in")]
    upd["ssm_glu_w"] = [o[None] for o in big_update(ssm_glu_w[0], m_ssm_glu_w[0], v_ssm_glu_w[0], gw_glu, 1, "glu")]
    upd["ret_w_proj"] = [o[None] for o in big_update(ret_w_proj[0], m_ret_w_proj[0], v_ret_w_proj[0], gw_rp, 0,
                                                     "ret_proj")]
    upd["mix_w_out"] = [o[None] for o in big_update(mix_w_out[0], m_mix_w_out[0], v_mix_w_out[0], gw_mo, 0,
                                                    "mix_out")]
    upd["ada_w"] = [o[None] for o in _adamw(ada_w[0], m_ada_w[0], v_ada_w[0], grad_ada_w[None], "adamw_ada_w")]

    small_names = ["c_ctx", "ada_b", "norm_g", "ssm_lam_re", "ssm_lam_im", "ssm_log_step", "ssm_b_re", "ssm_b_im",
                   "ssm_c_re", "ssm_c_im", "ssm_d", "ret_decay_logit"]
    small_w = [c_ctx, ada_b, norm_g, ssm_lam_re, ssm_lam_im, ssm_log_step, ssm_b_re, ssm_b_im, ssm_c_re, ssm_c_im,
               ssm_d, ret_decay_logit]
    small_m = [m_c_ctx, m_ada_b, m_norm_g, m_ssm_lam_re, m_ssm_lam_im, m_ssm_log_step, m_ssm_b_re, m_ssm_b_im,
               m_ssm_c_re, m_ssm_c_im, m_ssm_d, m_ret_decay_logit]
    small_v = [v_c_ctx, v_ada_b, v_norm_g, v_ssm_lam_re, v_ssm_lam_im, v_ssm_log_step, v_ssm_b_re, v_ssm_b_im,
               v_ssm_c_re, v_ssm_c_im, v_ssm_d, v_ret_decay_logit]
    small_g = [grad_c_ctx, grad_ada_b, grad_norm_g] + [s[None] for s in sums[3:3 + s5_names]] + \
              [sums[3 + s5_names].reshape(ssm_d.shape), sums[4 + s5_names][None]]
    shapes = [w.shape for w in small_w]
    res = _adamw(_pack(small_w, 1024), _pack(small_m, 1024), _pack(small_v, 1024), _pack(small_g, 1024)[None],
                 "adamw_small")
    small_out = [_unpack(o, shapes) for o in res]
    for i, nm in enumerate(small_names):
        upd[nm] = [small_out[kind][i] for kind in range(4)]

    order = ["c_ctx", "ada_w", "ada_b", "norm_g", "ffn_w_in", "ffn_w_out", "mix_w_in", "ssm_lam_re", "ssm_lam_im",
             "ssm_log_step", "ssm_b_re", "ssm_b_im", "ssm_c_re", "ssm_c_im", "ssm_d", "ssm_glu_w", "ret_decay_logit",
             "ret_w_proj", "mix_w_out"]
    outs = [loss, grad_x]
    for kind in range(4):
        outs += [upd[nm][kind] for nm in order]
    return tuple(outs)
```

```python
import functools
import math

import jax
import jax.numpy as jnp
import numpy as np
from jax import lax
from jax.experimental import pallas as pl
from jax.experimental.pallas import tpu as pltpu

F32 = jnp.float32
BF16 = jnp.bfloat16
MXU_DTYPE = jnp.bfloat16
MESH_AXES = ("x", "y", "c")
N_DEV = 8
V7X_VMEM_LIMIT_BYTES = 56 * 1024 * 1024
LANE = 128
SUBLANE = 8

GRID_W = 64
RET_CHUNK = 128
ROPE_BASE = 10000.0
NORM_EPS = 1e-6
ADAM_LR = 0.001
ADAM_B1 = 0.9
ADAM_B2 = 0.999
ADAM_EPS = 1e-08
ADAM_WD = 0.01
ADAM_STEP = 10
SSM_TILE_GROUPS = 8
SSM_HALF_GROUPS = 4


def _params(sem=None):
    return pltpu.CompilerParams(dimension_semantics=sem, vmem_limit_bytes=V7X_VMEM_LIMIT_BYTES)


def _tile(n, target, mult):
    best = None
    t = mult
    while t <= min(n, target):
        if n % t == 0:
            best = t
        t += mult
    return n if best is None else best


def _sds(shape, dtype):
    return jax.ShapeDtypeStruct(tuple(shape), dtype)


def _mm(a, b, dims, out_dtype, name, tm=512, tn=1408, tk=2048):
    if dims == "nn":
        (m, k), (k2, n) = a.shape, b.shape
    elif dims == "nt":
        (m, k), (n, k2) = a.shape, b.shape
    else:
        (k, m), (k2, n) = a.shape, b.shape
    assert k == k2, (a.shape, b.shape, dims)
    tm = _tile(m, tm, 16)
    tn = _tile(n, tn, LANE)
    tk = _tile(k, tk, LANE if dims != "tn" else 16)
    nk = k // tk
    dn = {"nn": (((1,), (0,)), ((), ())), "nt": (((1,), (1,)), ((), ())), "tn": (((0,), (0,)), ((), ()))}[dims]

    def body(a_ref, b_ref, o_ref, acc_ref):
        kk = pl.program_id(2)

        @pl.when(kk == 0)
        def _():
            acc_ref[...] = jnp.zeros_like(acc_ref)

        acc_ref[...] += lax.dot_general(a_ref[...].astype(MXU_DTYPE), b_ref[...].astype(MXU_DTYPE), dn,
                                        preferred_element_type=F32)

        @pl.when(kk == nk - 1)
        def _():
            o_ref[...] = acc_ref[...].astype(o_ref.dtype)

    if dims == "nn":
        a_spec = pl.BlockSpec((tm, tk), lambda j, i, kk: (i, kk))
        b_spec = pl.BlockSpec((tk, tn), lambda j, i, kk: (kk, j))
    elif dims == "nt":
        a_spec = pl.BlockSpec((tm, tk), lambda j, i, kk: (i, kk))
        b_spec = pl.BlockSpec((tn, tk), lambda j, i, kk: (j, kk))
    else:
        a_spec = pl.BlockSpec((tk, tm), lambda j, i, kk: (kk, i))
        b_spec = pl.BlockSpec((tk, tn), lambda j, i, kk: (kk, j))
    return pl.pallas_call(
        body, name=name, grid=(n // tn, m // tm, nk), in_specs=[a_spec, b_spec],
        out_specs=pl.BlockSpec((tm, tn), lambda j, i, kk: (i, j)), out_shape=_sds((m, n), out_dtype),
        scratch_shapes=[pltpu.VMEM((tm, tn), F32)],
        compiler_params=_params(("parallel", "parallel", "arbitrary")))(a, b)


def _rows(name, body, n_tiles, ins, outs):
    in_specs = [pl.BlockSpec(blk, imap) for (_, blk, imap) in ins]
    out_specs = [pl.BlockSpec(blk, imap) for (_, _, blk, imap) in outs]
    out_shape = [_sds(shape, dt) for (shape, dt, _, _) in outs]
    res = pl.pallas_call(body, name=name, grid=(n_tiles,), in_specs=in_specs, out_specs=out_specs,
                         out_shape=out_shape, compiler_params=_params(("arbitrary",)))(*[a for (a, _, _) in ins])
    return res


def _row_in(arr, tile, width=None, col=0, x_only_offset=None):
    width = arr.shape[1] if width is None else width
    if x_only_offset is None:
        return (arr, (tile, width), lambda i: (i, col))
    return (arr, (tile, width), lambda i: (jnp.maximum(i - x_only_offset, 0), col))


def _vec_in(arr, idx_fn):
    return (arr, (1, 1, arr.shape[2]), lambda i: (idx_fn(i), 0, 0))


def _rms(h):
    return lax.rsqrt(jnp.mean(h * h, axis=-1, keepdims=True) + NORM_EPS)


def _sigmoid(z):
    return 1.0 / (1.0 + jnp.exp(-z))


def _ada_pre_fwd(h, g6, mods, gi, mi, nct, tile, name):
    r, d = h.shape
    sel = lambda i: jnp.where(i >= nct, 1, 0)

    def body(h_ref, g_ref, sh_ref, sc_ref, u_ref):
        hh = h_ref[...]
        n = hh * _rms(hh) * g_ref[0]
        u_ref[...] = (n * (1.0 + sc_ref[0]) + sh_ref[0]).astype(u_ref.dtype)

    (u,) = _rows(name, body, r // tile,
                 [_row_in(h, tile), _vec_in(g6, lambda i: gi), _vec_in(mods, lambda i: sel(i) * 9 + 3 * mi),
                  _vec_in(mods, lambda i: sel(i) * 9 + 3 * mi + 1)],
                 [((r, d), BF16, (tile, d), lambda i: (i, 0))])
    return u


def _ada_pre_bwd(h, du, dres, g6, mods, gi, mi, nct, nsel, tile, name, dres_x_only=False):
    r, d = h.shape
    sel = lambda i: jnp.where(i >= nct, 1, 0) if nsel == 2 else 0
    msel = lambda i: jnp.where(i >= nct, 1, 0)
    off = nct if dres_x_only else None

    def body(h_ref, du_ref, dr_ref, g_ref, sc_ref, dh_ref, dg_ref, dsh_ref, dsc_ref):
        i = pl.program_id(0)
        hh = h_ref[...]
        rr = _rms(hh)
        g = g_ref[0]
        hn = hh * rr
        n = hn * g
        du_ = du_ref[...].astype(F32)
        dn = du_ * (1.0 + sc_ref[0])

        @pl.when(i == 0)
        def _():
            dg_ref[...] = jnp.zeros_like(dg_ref)

        @pl.when((i == 0) | (i == nct))
        def _():
            dsh_ref[...] = jnp.zeros_like(dsh_ref)
            dsc_ref[...] = jnp.zeros_like(dsc_ref)

        dg_ref[0] += jnp.sum(dn * hn, axis=0, keepdims=True)
        dsh_ref[0] += jnp.sum(du_, axis=0, keepdims=True)
        dsc_ref[0] += jnp.sum(du_ * n, axis=0, keepdims=True)
        t = dn * g
        dh = rr * t - hn * (rr * jnp.mean(t * hn, axis=-1, keepdims=True))
        if dres_x_only:
            dh_ref[...] = dh + jnp.where(i >= nct, dr_ref[...], 0.0)
        else:
            dh_ref[...] = dh + dr_ref[...]

    dh, dg, dsh, dsc = _rows(
        name, body, r // tile,
        [_row_in(h, tile), _row_in(du, tile), _row_in(dres, tile, x_only_offset=off), _vec_in(g6, lambda i: gi),
         _vec_in(mods, lambda i: msel(i) * 9 + 3 * mi + 1)],
        [((r, d), F32, (tile, d), lambda i: (i, 0)), ((1, 1, d), F32, (1, 1, d), lambda i: (0, 0, 0)),
         ((nsel, 1, d), F32, (1, 1, d), lambda i: (sel(i), 0, 0)),
         ((nsel, 1, d), F32, (1, 1, d), lambda i: (sel(i), 0, 0))])
    return dh, dg, dsh, dsc


def _ada_post_fwd(h, o, g6, mods, gi, mi, res_w, nct, tile, name, h_x_only=False):
    r, d = o.shape
    sel = lambda i: jnp.where(i >= nct, 1, 0)

    def body(h_ref, o_ref, g_ref, gt_ref, y_ref):
        oo = o_ref[...]
        n = oo * _rms(oo) * g_ref[0]
        y_ref[...] = h_ref[...] + res_w * gt_ref[0] * n

    (y,) = _rows(name, body, r // tile,
                 [_row_in(h, tile), _row_in(o, tile), _vec_in(g6, lambda i: gi),
                  _vec_in(mods, lambda i: sel(i) * 9 + 3 * mi + 2)],
                 [((r, d), F32, (tile, d), lambda i: (i, 0))])
    return y


def _ada_post_bwd(dy, o, g6, mods, gi, mi, res_w, nct, nsel, tile, name):
    r, d = o.shape
    sel = lambda i: jnp.where(i >= nct, 1, 0) if nsel == 2 else 0
    msel = lambda i: jnp.where(i >= nct, 1, 0)

    def body(dy_ref, o_ref, g_ref, gt_ref, do_ref, dg_ref, dgt_ref):
        i = pl.program_id(0)
        oo = o_ref[...]
        rr = _rms(oo)
        g = g_ref[0]
        on = oo * rr
        dy_ = dy_ref[...] * res_w

        @pl.when(i == 0)
        def _():
            dg_ref[...] = jnp.zeros_like(dg_ref)

        @pl.when((i == 0) | (i == nct))
        def _():
            dgt_ref[...] = jnp.zeros_like(dgt_ref)

        dgt_ref[0] += jnp.sum(dy_ * (on * g), axis=0, keepdims=True)
        dn = dy_ * gt_ref[0]
        dg_ref[0] += jnp.sum(dn * on, axis=0, keepdims=True)
        t = dn * g
        do_ref[...] = rr * t - on * (rr * jnp.mean(t * on, axis=-1, keepdims=True))

    do, dg, dgt = _rows(
        name, body, r // tile,
        [_row_in(dy, tile), _row_in(o, tile), _vec_in(g6, lambda i: gi),
         _vec_in(mods, lambda i: msel(i) * 9 + 3 * mi + 2)],
        [((r, d), F32, (tile, d), lambda i: (i, 0)), ((1, 1, d), F32, (1, 1, d), lambda i: (0, 0, 0)),
         ((nsel, 1, d), F32, (1, 1, d), lambda i: (sel(i), 0, 0))])
    return do, dg, dgt


def _swiglu_fwd(h, tile, name):
    r, w2 = h.shape
    f = w2 // 2

    def body(h_ref, a_ref):
        gt = h_ref[:, :f]
        up = h_ref[:, f:]
        a_ref[...] = (gt * _sigmoid(gt) * up).astype(a_ref.dtype)

    (a,) = _rows(name, body, r // tile, [_row_in(h, tile)], [((r, f), BF16, (tile, f), lambda i: (i, 0))])
    return a


def _swiglu_bwd(h, da, tile, name):
    r, w2 = h.shape
    f = w2 // 2

    def body(h_ref, da_ref, dh_ref):
        gt = h_ref[:, :f]
        up = h_ref[:, f:]
        d = da_ref[...]
        sg = _sigmoid(gt)
        dh_ref[:, :f] = (d * up * (sg * (1.0 + gt * (1.0 - sg)))).astype(dh_ref.dtype)
        dh_ref[:, f:] = (d * gt * sg).astype(dh_ref.dtype)

    (dh,) = _rows(name, body, r // tile, [_row_in(h, tile), _row_in(da, tile)],
                  [((r, w2), BF16, (tile, w2), lambda i: (i, 0))])
    return dh


def _gelu_parts(y):
    c0 = math.sqrt(2.0 / math.pi)
    inner = c0 * (y + 0.044715 * y * y * y)
    th = jnp.tanh(inner)
    return th, c0 * (1.0 + 3 * 0.044715 * y * y)


def _ssm_out_fwd(y0, y1, hm, dskip, nct, tile, name):
    t_rows, s = y0.shape

    def body(y0_ref, y1_ref, u_ref, d_ref, a_ref):
        y = y0_ref[...] + y1_ref[...] + d_ref[0] * u_ref[...]
        th, _ = _gelu_parts(y)
        a_ref[...] = (0.5 * y * (1.0 + th)).astype(a_ref.dtype)

    (a,) = _rows(name, body, t_rows // tile,
                 [_row_in(y0, tile), _row_in(y1, tile), (hm, (tile, s), lambda i: (i + nct, 0)),
                  _vec_in(dskip, lambda i: 0)],
                 [((t_rows, s), BF16, (tile, s), lambda i: (i, 0))])
    return a


def _ssm_out_bwd(y0, y1, hm, dskip, da, nct, tile, name):
    t_rows, s = y0.shape

    def body(y0_ref, y1_ref, u_ref, d_ref, da_ref, dy_ref, du_ref, dd_ref):
        i = pl.program_id(0)
        u = u_ref[...]
        y = y0_ref[...] + y1_ref[...] + d_ref[0] * u
        th, dinner = _gelu_parts(y)
        dy = da_ref[...] * (0.5 * (1.0 + th) + 0.5 * y * (1.0 - th * th) * dinner)
        dy_ref[...] = dy
        du_ref[...] = dy * d_ref[0]

        @pl.when(i == 0)
        def _():
            dd_ref[...] = jnp.zeros_like(dd_ref)

        dd_ref[0] += jnp.sum(dy * u, axis=0, keepdims=True)

    dy, du, dd = _rows(name, body, t_rows // tile,
                       [_row_in(y0, tile), _row_in(y1, tile), (hm, (tile, s), lambda i: (i + nct, 0)),
                        _vec_in(dskip, lambda i: 0), _row_in(da, tile)],
                       [((t_rows, s), F32, (tile, s), lambda i: (i, 0)), ((t_rows, s), F32, (tile, s), lambda i: (i, 0)),
                        ((1, 1, s), F32, (1, 1, s), lambda i: (0, 0, 0))])
    return dy, du, dd


def _col_pieces(arr, off, width, tile, nct, unit=None):
    pw = math.gcd(off, width if unit is None else unit)
    specs = [(arr, (tile, pw), functools.partial(lambda i, cb: (i + nct, cb), cb=off // pw + p))
             for p in range(width // pw)]
    return specs, pw


def _ret_gate_fwd(o0, o1, hm, g_off, heads, dv, nct, tile, name):
    t_rows, w = o0.shape
    g_specs, pw = _col_pieces(hm, g_off, w, tile, nct)
    ng = len(g_specs)

    def body(o0_ref, o1_ref, *refs):
        g_refs, r_ref = refs[:ng], refs[ng]
        for hd in range(heads):
            cs = slice(hd * dv, (hd + 1) * dv)
            o = o0_ref[:, cs] + o1_ref[:, cs]
            lo = (hd * dv) % pw
            g = g_refs[(hd * dv) // pw][:, lo:lo + dv]
            r_ref[:, cs] = (g * _sigmoid(g) * (o * _rms(o))).astype(r_ref.dtype)

    (ri,) = _rows(name, body, t_rows // tile, [_row_in(o0, tile), _row_in(o1, tile)] + g_specs,
                  [((t_rows, w), BF16, (tile, w), lambda i: (i, 0))])
    return ri


def _ret_gate_bwd(o0, o1, hm, g_off, dri, heads, dv, nct, tile, name):
    t_rows, w = o0.shape
    g_specs, pw = _col_pieces(hm, g_off, w, tile, nct)
    ng = len(g_specs)

    def body(o0_ref, o1_ref, d_ref, *refs):
        g_refs, do_ref, dg_ref = refs[:ng], refs[ng], refs[ng + 1]
        for hd in range(heads):
            cs = slice(hd * dv, (hd + 1) * dv)
            o = o0_ref[:, cs] + o1_ref[:, cs]
            lo = (hd * dv) % pw
            g = g_refs[(hd * dv) // pw][:, lo:lo + dv]
            d = d_ref[:, cs]
            rr = _rms(o)
            on = o * rr
            sg = _sigmoid(g)
            dg_ref[:, cs] = d * on * (sg * (1.0 + g * (1.0 - sg)))
            t = d * (g * sg)
            do_ref[:, cs] = rr * t - on * (rr * jnp.mean(t * on, axis=-1, keepdims=True))

    do, dg = _rows(name, body, t_rows // tile, [_row_in(o0, tile), _row_in(o1, tile), _row_in(dri, tile)] + g_specs,
                   [((t_rows, w), F32, (tile, w), lambda i: (i, 0)), ((t_rows, w), F32, (tile, w), lambda i: (i, 0))])
    return do, dg


def _merge_fwd(gab, rb, hm, gs_off, nct, tile, name):
    t_rows, d = rb.shape
    specs, pw = _col_pieces(hm, gs_off, 2 * d, tile, nct, unit=d)
    npc = d // pw

    def body(gab_ref, rb_ref, *refs):
        gs_refs, gr_refs, m_ref = refs[:npc], refs[npc:2 * npc], refs[2 * npc]
        for p in range(npc):
            cs = slice(p * pw, (p + 1) * pw)
            ga = gab_ref[:, cs]
            gb = gab_ref[:, d + p * pw:d + (p + 1) * pw]
            m_ref[:, cs] = (_sigmoid(gs_refs[p][...]) * (ga * _sigmoid(gb))
                            + _sigmoid(gr_refs[p][...]) * rb_ref[:, cs]).astype(m_ref.dtype)

    (mg,) = _rows(name, body, t_rows // tile, [_row_in(gab, tile), _row_in(rb, tile)] + specs,
                  [((t_rows, d), BF16, (tile, d), lambda i: (i, 0))])
    return mg


def _merge_bwd(gab, rb, hm, gs_off, dm, nct, tile, name):
    t_rows, d = rb.shape
    specs, pw = _col_pieces(hm, gs_off, 2 * d, tile, nct, unit=d)
    npc = d // pw

    def body(gab_ref, rb_ref, dm_ref, *refs):
        gs_refs, gr_refs = refs[:npc], refs[npc:2 * npc]
        dgab_ref, drb_ref, dgs_ref, dgr_ref = refs[2 * npc:]
        for p in range(npc):
            cs = slice(p * pw, (p + 1) * pw)
            cs2 = slice(d + p * pw, d + (p + 1) * pw)
            ga = gab_ref[:, cs]
            gb = gab_ref[:, cs2]
            dmm = dm_ref[:, cs]
            ss = _sigmoid(gs_refs[p][...])
            sr = _sigmoid(gr_refs[p][...])
            sb = _sigmoid(gb)
            dbr = dmm * ss
            dgab_ref[:, cs] = (dbr * sb).astype(dgab_ref.dtype)
            dgab_ref[:, cs2] = (dbr * ga * sb * (1.0 - sb)).astype(dgab_ref.dtype)
            drb_ref[:, cs] = (dmm * sr).astype(drb_ref.dtype)
            dgs_ref[:, cs] = dmm * (ga * sb) * ss * (1.0 - ss)
            dgr_ref[:, cs] = dmm * rb_ref[:, cs] * sr * (1.0 - sr)

    return _rows(name, body, t_rows // tile, [_row_in(gab, tile), _row_in(rb, tile), _row_in(dm, tile)] + specs,
                 [((t_rows, 2 * d), BF16, (tile, 2 * d), lambda i: (i, 0)), ((t_rows, d), BF16, (tile, d), lambda i: (i, 0)),
                  ((t_rows, d), F32, (tile, d), lambda i: (i, 0)), ((t_rows, d), F32, (tile, d), lambda i: (i, 0))])


def _assemble_dhm(dus, dq0, dq1, dk0, dk1, dv0, dv1, dg, dgs, dgr, nct, tile, name):
    r, s = dus.shape
    qk = dq0.shape[1]
    vw = dv0.shape[1]
    d = dgs.shape[1]
    mi = s + 2 * qk + 2 * vw + 2 * d
    c_q, c_k, c_v, c_g, c_gs, c_gr = s, s + qk, s + 2 * qk, s + 2 * qk + vw, s + 2 * qk + 2 * vw, s + 2 * qk + 2 * vw + d

    def body(dus_ref, dq0_ref, dq1_ref, dk0_ref, dk1_ref, dv0_ref, dv1_ref, dg_ref, dgs_ref, dgr_ref, o_ref):
        i = pl.program_id(0)
        lat = i >= nct
        o_ref[:, :s] = dus_ref[...].astype(o_ref.dtype)
        o_ref[:, c_q:c_k] = (dq0_ref[...] + dq1_ref[...]).astype(o_ref.dtype)
        o_ref[:, c_k:c_v] = (dk0_ref[...] + dk1_ref[...]).astype(o_ref.dtype)
        o_ref[:, c_v:c_g] = (dv0_ref[...] + dv1_ref[...]).astype(o_ref.dtype)
        o_ref[:, c_g:c_gs] = jnp.where(lat, dg_ref[...], 0.0).astype(o_ref.dtype)
        o_ref[:, c_gs:c_gr] = jnp.where(lat, dgs_ref[...], 0.0).astype(o_ref.dtype)
        o_ref[:, c_gr:] = jnp.where(lat, dgr_ref[...], 0.0).astype(o_ref.dtype)

    (out,) = _rows(name, body, r // tile,
                   [_row_in(dus, tile), _row_in(dq0, tile), _row_in(dq1, tile), _row_in(dk0, tile), _row_in(dk1, tile),
                    _row_in(dv0, tile), _row_in(dv1, tile), _row_in(dg, tile, x_only_offset=nct),
                    _row_in(dgs, tile, x_only_offset=nct), _row_in(dgr, tile, x_only_offset=nct)],
                   [((r, mi), BF16, (tile, mi), lambda i: (i, 0))])
    return out


def _loss_grad(y, target, tile, name):
    t_rows, d = y.shape

    def body(y_ref, t_ref, dy_ref, l_ref):
        i = pl.program_id(0)
        e = y_ref[...] - t_ref[...]
        dy_ref[...] = e * (1.0 / d)

        @pl.when(i == 0)
        def _():
            l_ref[...] = jnp.zeros_like(l_ref)

        l_ref[0] += jnp.sum(e * e, axis=0, keepdims=True)

    return _rows(name, body, t_rows // tile, [_row_in(y, tile), _row_in(target, tile)],
                 [((t_rows, d), F32, (tile, d), lambda i: (i, 0)), ((1, 1, d), F32, (1, 1, d), lambda i: (0, 0, 0))])


def _silu_rows(v, name):
    def body(v_ref, o_ref):
        z = v_ref[...]
        o_ref[...] = z * _sigmoid(z)

    (o,) = _rows(name, body, 1, [_row_in(v, v.shape[0])], [(v.shape, F32, v.shape, lambda i: (0, 0))])
    return o


def _silu_grad_rows(v, dv, name):
    def body(v_ref, d_ref, o_ref):
        z = v_ref[...]
        sg = _sigmoid(z)
        o_ref[...] = d_ref[...] * (sg * (1.0 + z * (1.0 - sg)))

    (o,) = _rows(name, body, 1, [_row_in(v, v.shape[0]), _row_in(dv, v.shape[0])],
                 [(v.shape, F32, v.shape, lambda i: (0, 0))])
    return o


def _sum_leading(g8, name):
    n, r, c = g8.shape
    tile = _tile(r, 256, SUBLANE)

    def body(g_ref, o_ref):
        acc = g_ref[0]
        for j in range(1, n):
            acc = acc + g_ref[j]
        o_ref[...] = acc

    (o,) = _rows(name, body, r // tile, [(g8, (n, tile, c), lambda i: (0, i, 0))],
                 [((r, c), F32, (tile, c), lambda i: (i, 0))])
    return o


def _pair_sum(g, recv, axis, name):
    n, br, bc = recv.shape
    tile = _tile(br, 256, 16)
    nrt = br // tile
    core = lax.axis_index("c").astype(jnp.int32).reshape(1)

    def body(c_ref, g_ref, r_ref, o_ref):
        o_ref[0] = (g_ref[...].astype(F32) + r_ref[0].astype(F32)).astype(o_ref.dtype)

    if axis == 1:
        g_spec = pl.BlockSpec((tile, bc), lambda q, i, c_ref: (i, 2 * q + c_ref[0]))
    else:
        g_spec = pl.BlockSpec((tile, bc), lambda q, i, c_ref: ((2 * q + c_ref[0]) * nrt + i, 0))
    slot = pl.BlockSpec((1, tile, bc), lambda q, i, c_ref: (q, i, 0))
    return pl.pallas_call(
        body, name=name, out_shape=_sds((n, br, bc), recv.dtype),
        grid_spec=pltpu.PrefetchScalarGridSpec(num_scalar_prefetch=1, grid=(n, nrt), in_specs=[g_spec, slot],
                                               out_specs=slot),
        compiler_params=_params(("arbitrary", "arbitrary")))(core, g, recv)


def _adam_math(w, m, v, g):
    c1 = 1.0 / (1.0 - ADAM_B1 ** ADAM_STEP)
    c2 = 1.0 / (1.0 - ADAM_B2 ** ADAM_STEP)
    mm = ADAM_B1 * m + (1.0 - ADAM_B1) * g
    vv = ADAM_B2 * v + (1.0 - ADAM_B2) * (g * g)
    return -ADAM_LR * ((mm * c1) / (jnp.sqrt(vv * c2) + ADAM_EPS) + ADAM_WD * w), mm, vv


def _adamw(w, m, v, gparts, name):
    r, c = w.shape
    n = gparts.shape[0]
    tile = _tile(r, 256, 16)

    def body(w_ref, m_ref, v_ref, g_ref, go_ref, d_ref, mo_ref, vo_ref):
        g = g_ref[0].astype(F32)
        for j in range(1, n):
            g = g + g_ref[j].astype(F32)
        go_ref[...] = g
        d_ref[...], mo_ref[...], vo_ref[...] = _adam_math(w_ref[...], m_ref[...], v_ref[...], g)

    rs = lambda arr: _row_in(arr, tile)
    out = ((r, c), F32, (tile, c), lambda i: (i, 0))
    return _rows(name, body, r // tile, [rs(w), rs(m), rs(v), (gparts, (n, tile, c), lambda i: (0, i, 0))],
                 [out, out, out, out])


def _adamw_scattered(w, m, v, p, recv, name):
    r, c = w.shape
    n = recv.shape[0]
    tile = _tile(r, 256, 16)
    chip = (2 * lax.axis_index("x") + lax.axis_index("y")).astype(jnp.int32).reshape(1)

    def body(q_ref, w_ref, m_ref, v_ref, p_ref, g_ref, go_ref, d_ref, mo_ref, vo_ref):
        g = p_ref[0].astype(F32)
        for j in range(n):
            g = g + g_ref[j].astype(F32)
        go_ref[...] = g
        d_ref[...], mo_ref[...], vo_ref[...] = _adam_math(w_ref[...], m_ref[...], v_ref[...], g)

    row = pl.BlockSpec((tile, c), lambda i, q_ref: (i, 0))
    out = _sds((r, c), F32)
    return pl.pallas_call(
        body, name=name, out_shape=[out, out, out, out],
        grid_spec=pltpu.PrefetchScalarGridSpec(
            num_scalar_prefetch=1, grid=(r // tile,),
            in_specs=[row, row, row, pl.BlockSpec((1, tile, c), lambda i, q_ref: (q_ref[0], i, 0)),
                      pl.BlockSpec((n, tile, c), lambda i, q_ref: (0, i, 0))],
            out_specs=[row, row, row, row]),
        compiler_params=_params(("arbitrary",)))(chip, w, m, v, p, recv)


def _cmul(ar, ai, br, bi):
    return ar * br - ai * bi, ar * bi + ai * br


def _cpow(ar, ai, n):
    pr, pi = jnp.ones_like(ar), jnp.zeros_like(ar)
    br, bi = ar, ai
    while n:
        if n & 1:
            pr, pi = _cmul(pr, pi, br, bi)
        n >>= 1
        if n:
            br, bi = _cmul(br, bi, br, bi)
    return pr, pi


def _s5_scan_into(xr_ref, xi_ref, ar1, ai1, ns, fr_ref, fi_ref, hr_ref, hi_ref, reverse):
    st = ar1.shape[1]
    ar = jnp.broadcast_to(ar1, (SUBLANE, st))
    ai = jnp.broadcast_to(ai1, (SUBLANE, st))
    zero = jnp.zeros((SUBLANE, st), F32)
    zero1 = jnp.zeros((1, st), F32)

    def slab(k):
        return pl.ds(pl.multiple_of(k * SUBLANE, SUBLANE), SUBLANE)

    def pass1(j, carry):
        hr, hi = carry
        k = ns - 1 - j if reverse else j
        nr, ni = _cmul(ar, ai, hr, hi)
        return nr + xr_ref[slab(k), :], ni + xi_ref[slab(k), :]

    fr, fi = lax.fori_loop(0, ns, pass1, (zero, zero))
    fr_ref[...] = fr
    fi_ref[...] = fi
    pr, pi = _cpow(ar1, ai1, ns)
    order = list(range(N_DEV - 1, -1, -1)) if reverse else list(range(N_DEV))
    hr_ref[order[0]:order[0] + 1, :] = zero1
    hi_ref[order[0]:order[0] + 1, :] = zero1
    for a_, b_ in zip(order[:-1], order[1:]):
        cr, ci = _cmul(pr, pi, hr_ref[a_:a_ + 1, :], hi_ref[a_:a_ + 1, :])
        hr_ref[b_:b_ + 1, :] = cr + fr_ref[a_:a_ + 1, :]
        hi_ref[b_:b_ + 1, :] = ci + fi_ref[a_:a_ + 1, :]

    def pass2(j, carry):
        hr, hi = carry
        k = ns - 1 - j if reverse else j
        nr, ni = _cmul(ar, ai, hr, hi)
        nr = nr + xr_ref[slab(k), :]
        ni = ni + xi_ref[slab(k), :]
        xr_ref[slab(k), :] = nr
        xi_ref[slab(k), :] = ni
        return nr, ni

    lax.fori_loop(0, ns, pass2, (hr_ref[...], hi_ref[...]))


def _s5_specs(r, ch, st):
    u_spec = pl.BlockSpec((r, ch), lambda j: (0, j // 2))
    w_spec = pl.BlockSpec((1, ch, st), lambda j: (j, 0, 0))
    c_spec = pl.BlockSpec((1, st, ch), lambda j: (j, 0, 0))
    a_spec = pl.BlockSpec((1, 2, st), lambda j: (j, 0, 0))
    return u_spec, w_spec, c_spec, a_spec


def _s5_fwd(up, wre, wim, cre, cim, a, rev, name):
    r, s = up.shape
    nh, ch, st = wre.shape
    ns = r // N_DEV
    u_spec, w_spec, c_spec, a_spec = _s5_specs(r, ch, st)

    def body(u_ref, wre_ref, wim_ref, cre_ref, cim_ref, a_ref, y_ref, xr, xi, fr, fi, hr, hi):
        j = pl.program_id(0)
        for rb in range(N_DEV):
            rows = slice(rb * ns, (rb + 1) * ns)
            ub = u_ref[rows, :].astype(MXU_DTYPE)
            xr[rows, :] = jnp.dot(ub, wre_ref[0].astype(MXU_DTYPE), preferred_element_type=F32)
            xi[rows, :] = jnp.dot(ub, wim_ref[0].astype(MXU_DTYPE), preferred_element_type=F32)
        _s5_scan_into(xr, xi, a_ref[0, 0:1, :], a_ref[0, 1:2, :], ns, fr, fi, hr, hi, rev)
        for rb in range(N_DEV):
            rows = slice(rb * ns, (rb + 1) * ns)
            yb = (jnp.dot(xr[rows, :].astype(MXU_DTYPE), cre_ref[0].astype(MXU_DTYPE), preferred_element_type=F32)
                  - jnp.dot(xi[rows, :].astype(MXU_DTYPE), cim_ref[0].astype(MXU_DTYPE), preferred_element_type=F32))

            @pl.when(j % 2 == 0)
            def _():
                y_ref[rows, :] = yb

            @pl.when(j % 2 == 1)
            def _():
                y_ref[rows, :] += yb

    small = pltpu.VMEM((SUBLANE, st), F32)
    return pl.pallas_call(
        body, name=name, grid=(nh,), in_specs=[u_spec, w_spec, w_spec, c_spec, c_spec, a_spec],
        out_specs=pl.BlockSpec((r, ch), lambda j: (0, j // 2)), out_shape=_sds((r, s), F32),
        scratch_shapes=[pltpu.VMEM((r, st), F32), pltpu.VMEM((r, st), F32), small, small, small, small],
        compiler_params=_params(("arbitrary",)))(up, wre, wim, cre, cim, a)


def _s5_bwd(up, dyp, wre, wim, cre, cim, a, rev, name):
    r, s = up.shape
    nh, ch, st = wre.shape
    ns = r // N_DEV
    u_spec, w_spec, c_spec, a_spec = _s5_specs(r, ch, st)
    nt = (((1,), (1,)), ((), ()))
    tn = (((0,), (0,)), ((), ()))

    def body(u_ref, dy_ref, wre_ref, wim_ref, cre_ref, cim_ref, a_ref,
             du_ref, dwre_ref, dwim_ref, dcre_ref, dcim_ref, da_ref,
             hr, hi, gr, gi, fr, fi, sr, si, er, ei):
        j = pl.program_id(0)
        wre_b = wre_ref[0].astype(MXU_DTYPE)
        wim_b = wim_ref[0].astype(MXU_DTYPE)
        cre_b = cre_ref[0].astype(MXU_DTYPE)
        cim_b = cim_ref[0].astype(MXU_DTYPE)
        for rb in range(N_DEV):
            rows = slice(rb * ns, (rb + 1) * ns)
            ub = u_ref[rows, :].astype(MXU_DTYPE)
            hr[rows, :] = jnp.dot(ub, wre_b, preferred_element_type=F32)
            hi[rows, :] = jnp.dot(ub, wim_b, preferred_element_type=F32)
        ar1, ai1 = a_ref[0, 0:1, :], a_ref[0, 1:2, :]
        _s5_scan_into(hr, hi, ar1, ai1, ns, fr, fi, sr, si, rev)
        dcre = jnp.zeros((st, ch), F32)
        dcim = jnp.zeros((st, ch), F32)
        for rb in range(N_DEV):
            rows = slice(rb * ns, (rb + 1) * ns)
            dyb = dy_ref[rows, :].astype(MXU_DTYPE)
            gr[rows, :] = lax.dot_general(dyb, cre_b, nt, preferred_element_type=F32)
            gi[rows, :] = -lax.dot_general(dyb, cim_b, nt, preferred_element_type=F32)
            dcre += lax.dot_general(hr[rows, :].astype(MXU_DTYPE), dyb, tn, preferred_element_type=F32)
            dcim -= lax.dot_general(hi[rows, :].astype(MXU_DTYPE), dyb, tn, preferred_element_type=F32)
        dcre_ref[0] = dcre
        dcim_ref[0] = dcim
        _s5_scan_into(gr, gi, ar1, -ai1, ns, fr, fi, er, ei, not rev)

        def slab(k):
            return pl.ds(pl.multiple_of(k * SUBLANE, SUBLANE), SUBLANE)

        step_back = 1 if rev else -1

        def acc_step(k, carry):
            acr, aci = carry
            g_r, g_i = gr[slab(k), :], gi[slab(k), :]
            p_r, p_i = hr[slab(k + step_back), :], hi[slab(k + step_back), :]
            return acr + g_r * p_r + g_i * p_i, aci + g_i * p_r - g_r * p_i

        edge = (ns - 1) * SUBLANE if rev else 0
        g_r, g_i = gr[edge:edge + SUBLANE, :], gi[edge:edge + SUBLANE, :]
        p_r, p_i = sr[...], si[...]
        lo, hi_k = (0, ns - 1) if rev else (1, ns)
        acr, aci = lax.fori_loop(lo, hi_k, acc_step, (g_r * p_r + g_i * p_i, g_i * p_r - g_r * p_i))
        da_ref[0, 0:1, :] = jnp.sum(acr, axis=0, keepdims=True)
        da_ref[0, 1:2, :] = jnp.sum(aci, axis=0, keepdims=True)
        dwre = jnp.zeros((ch, st), F32)
        dwim = jnp.zeros((ch, st), F32)
        for rb in range(N_DEV):
            rows = slice(rb * ns, (rb + 1) * ns)
            grb = gr[rows, :].astype(MXU_DTYPE)
            gib = gi[rows, :].astype(MXU_DTYPE)
            ub = u_ref[rows, :].astype(MXU_DTYPE)
            dub = (lax.dot_general(grb, wre_b, nt, preferred_element_type=F32)
                   + lax.dot_general(gib, wim_b, nt, preferred_element_type=F32))
            dwre += lax.dot_general(ub, grb, tn, preferred_element_type=F32)
            dwim += lax.dot_general(ub, gib, tn, preferred_element_type=F32)

            @pl.when(j % 2 == 0)
            def _():
                du_ref[rows, :] = dub

            @pl.when(j % 2 == 1)
            def _():
                du_ref[rows, :] += dub

        dwre_ref[0] = dwre
        dwim_ref[0] = dwim

    small = pltpu.VMEM((SUBLANE, st), F32)
    big = pltpu.VMEM((r, st), F32)
    return pl.pallas_call(
        body, name=name, grid=(nh,), in_specs=[u_spec, u_spec, w_spec, w_spec, c_spec, c_spec, a_spec],
        out_specs=[pl.BlockSpec((r, ch), lambda j: (0, j // 2)), w_spec, w_spec, c_spec, c_spec, a_spec],
        out_shape=[_sds((r, s), F32), _sds(wre.shape, F32), _sds(wre.shape, F32), _sds(cre.shape, F32),
                   _sds(cre.shape, F32), _sds(a.shape, F32)],
        scratch_shapes=[big, big, big, big, small, small, small, small, small, small],
        compiler_params=_params(("arbitrary",)))(up, dyp, wre, wim, cre, cim, a)


def _rope(t, cos, sin):
    quarter = t.shape[1] // 4
    lane = lax.broadcasted_iota(jnp.int32, t.shape, 1)
    first = (lane // quarter) % 2 == 0
    partner = jnp.where(first, pltpu.roll(t, t.shape[1] - quarter, 1), pltpu.roll(t, quarter, 1))
    return t * cos + partner * sin


def _rope_t(d, cos, sin):
    quarter = d.shape[1] // 4
    ds_ = d * sin
    lane = lax.broadcasted_iota(jnp.int32, d.shape, 1)
    first = (lane // quarter) % 2 == 0
    partner = jnp.where(first, pltpu.roll(ds_, d.shape[1] - quarter, 1), pltpu.roll(ds_, quarter, 1))
    return d * cos + partner


def _chunk_of_step(s, nch, ncc, rev):
    if not rev:
        return s
    return jnp.where(s < ncc, ncc - 1 - s, nch + ncc - 1 - s)


def _ret_fwd(hm, cos, sin, decay, wend, win, gch, heads, dk, dv, q_off, ncc, rev, name):
    r = hm.shape[0]
    ch = RET_CHUNK
    nch = r // ch
    t_rows = r - ncc * ch
    qb, kb, vb = q_off // dk, (q_off + heads * dk) // dk, (q_off + 2 * heads * dk) // dv
    q_scale = dk ** -0.5
    nt = (((1,), (1,)), ((), ()))
    tn = (((0,), (0,)), ((), ()))
    cof = lambda s: _chunk_of_step(s, nch, ncc, rev)

    def body(q_ref, k_ref, v_ref, cos_ref, sin_ref, dec_ref, we_ref, wi_ref, g_ref, o_ref, sin_out, st):
        s = pl.program_id(1)

        @pl.when(s == 0)
        def _():
            st[...] = jnp.zeros_like(st)

        q = _rope(q_ref[...], cos_ref[...], sin_ref[...]) * q_scale
        k = _rope(k_ref[...], cos_ref[...], sin_ref[...])
        v = v_ref[...].astype(MXU_DTYPE)
        s_cur = st[...]
        sin_out[0, 0] = s_cur
        kw = (k * we_ref[0]).astype(MXU_DTYPE)
        qw = (q * wi_ref[0]).astype(MXU_DTYPE)
        scores = lax.dot_general(q.astype(MXU_DTYPE), k.astype(MXU_DTYPE), nt, preferred_element_type=F32) * dec_ref[0]
        o_ref[...] = (jnp.dot(scores.astype(MXU_DTYPE), v, preferred_element_type=F32)
                      + jnp.dot(qw, s_cur.astype(MXU_DTYPE), preferred_element_type=F32))
        st[...] = g_ref[0] * s_cur + lax.dot_general(kw, v, tn, preferred_element_type=F32)

    tab = lambda w: pl.BlockSpec((1, ch, w), lambda h, s: (h, 0, 0))
    return pl.pallas_call(
        body, name=name, grid=(heads, nch),
        in_specs=[pl.BlockSpec((ch, dk), lambda h, s: (cof(s), qb + h)),
                  pl.BlockSpec((ch, dk), lambda h, s: (cof(s), kb + h)),
                  pl.BlockSpec((ch, dv), lambda h, s: (cof(s), vb + h)),
                  pl.BlockSpec((ch, dk), lambda h, s: (cof(s), 0)),
                  pl.BlockSpec((ch, dk), lambda h, s: (cof(s), 0)),
                  tab(ch), tab(dk), tab(dk), tab(dv)],
        out_specs=[pl.BlockSpec((ch, dv), lambda h, s: (jnp.maximum(cof(s) - ncc, 0) if not rev
                                                         else jnp.where(s < ncc, nch - ncc - 1, cof(s) - ncc), h)),
                   pl.BlockSpec((1, 1, dk, dv), lambda h, s: (h, s, 0, 0))],
        out_shape=[_sds((t_rows, heads * dv), F32), _sds((heads, nch, dk, dv), F32)],
        scratch_shapes=[pltpu.VMEM((dk, dv), F32)],
        compiler_params=_params(("parallel", "arbitrary")))(hm, hm, hm, cos, sin, decay, wend, win, gch)


def _ret_bwd(hm, cos, sin, decay, wend, win, gch, s_in, do, heads, dk, dv, q_off, ncc, rev, name):
    r = hm.shape[0]
    ch = RET_CHUNK
    nch = r // ch
    qb, kb, vb = q_off // dk, (q_off + heads * dk) // dk, (q_off + 2 * heads * dk) // dv
    q_scale = dk ** -0.5
    nt = (((1,), (1,)), ((), ()))
    tn = (((0,), (0,)), ((), ()))
    cof = lambda rr: _chunk_of_step(nch - 1 - rr, nch, ncc, rev)

    def body(q_ref, k_ref, v_ref, cos_ref, sin_ref, dec_ref, we_ref, wi_ref, g_ref, sin_ref2, do_ref,
             dq_ref, dk_ref, dv_ref, ddec_ref, dwe_ref, dwi_ref, dg_ref, dst):
        rr = pl.program_id(1)
        n = cof(rr)

        @pl.when(rr == 0)
        def _():
            dst[...] = jnp.zeros_like(dst)
            ddec_ref[...] = jnp.zeros_like(ddec_ref)
            dwe_ref[...] = jnp.zeros_like(dwe_ref)
            dwi_ref[...] = jnp.zeros_like(dwi_ref)
            dg_ref[...] = jnp.zeros_like(dg_ref)

        cos_, sin_ = cos_ref[...], sin_ref[...]
        q = _rope(q_ref[...], cos_, sin_) * q_scale
        k = _rope(k_ref[...], cos_, sin_)
        v = v_ref[...].astype(MXU_DTYPE)
        qb_, kb_ = q.astype(MXU_DTYPE), k.astype(MXU_DTYPE)
        kw = (k * we_ref[0]).astype(MXU_DTYPE)
        qw = (q * wi_ref[0]).astype(MXU_DTYPE)
        sraw = lax.dot_general(qb_, kb_, nt, preferred_element_type=F32)
        scores = (sraw * dec_ref[0]).astype(MXU_DTYPE)
        d_o = jnp.where(n >= ncc, do_ref[...], 0.0).astype(MXU_DTYPE)
        s_n = sin_ref2[0, 0]
        s_nb = s_n.astype(MXU_DTYPE)
        ds1 = dst[...]
        ds1b = ds1.astype(MXU_DTYPE)
        dsc = lax.dot_general(d_o, v, nt, preferred_element_type=F32)
        dsr = (dsc * dec_ref[0]).astype(MXU_DTYPE)
        ddec_ref[0] += dsc * sraw
        t1 = lax.dot_general(d_o, s_nb, nt, preferred_element_type=F32)
        dq_r = jnp.dot(dsr, kb_, preferred_element_type=F32) + t1 * wi_ref[0]
        dwi_ref[0] += t1 * q
        t2 = lax.dot_general(v, ds1b, nt, preferred_element_type=F32)
        dk_r = lax.dot_general(dsr, qb_, tn, preferred_element_type=F32) + t2 * we_ref[0]
        dwe_ref[0] += t2 * k
        dv_ref[...] = (lax.dot_general(scores, d_o, tn, preferred_element_type=F32)
                       + jnp.dot(kw, ds1b, preferred_element_type=F32))
        dg_ref[0] += ds1 * s_n
        dst[...] = g_ref[0] * ds1 + lax.dot_general(qw, d_o, tn, preferred_element_type=F32)
        dq_ref[...] = _rope_t(dq_r, cos_, sin_) * q_scale
        dk_ref[...] = _rope_t(dk_r, cos_, sin_)

    tab = lambda w: pl.BlockSpec((1, ch, w), lambda h, rr: (h, 0, 0))
    return pl.pallas_call(
        body, name=name, grid=(heads, nch),
        in_specs=[pl.BlockSpec((ch, dk), lambda h, rr: (cof(rr), qb + h)),
                  pl.BlockSpec((ch, dk), lambda h, rr: (cof(rr), kb + h)),
                  pl.BlockSpec((ch, dv), lambda h, rr: (cof(rr), vb + h)),
                  pl.BlockSpec((ch, dk), lambda h, rr: (cof(rr), 0)),
                  pl.BlockSpec((ch, dk), lambda h, rr: (cof(rr), 0)),
                  tab(ch), tab(dk), tab(dk), tab(dv),
                  pl.BlockSpec((1, 1, dk, dv), lambda h, rr: (h, nch - 1 - rr, 0, 0)),
                  pl.BlockSpec((ch, dv), lambda h, rr: (jnp.maximum(cof(rr) - ncc, 0), h))],
        out_specs=[pl.BlockSpec((ch, dk), lambda h, rr: (cof(rr), h)),
                   pl.BlockSpec((ch, dk), lambda h, rr: (cof(rr), h)),
                   pl.BlockSpec((ch, dv), lambda h, rr: (cof(rr), h)),
                   tab(ch), tab(dk), tab(dk), tab(dv)],
        out_shape=[_sds((r, heads * dk), F32), _sds((r, heads * dk), F32), _sds((r, heads * dv), F32),
                   _sds(decay.shape, F32), _sds(wend.shape, F32), _sds(win.shape, F32), _sds(gch.shape, F32)],
        scratch_shapes=[pltpu.VMEM((dk, dv), F32)],
        compiler_params=_params(("parallel", "arbitrary")))(hm, hm, hm, cos, sin, decay, wend, win, gch, s_in, do)


_HBM = pl.BlockSpec(memory_space=pltpu.HBM)
_MESH = pl.DeviceIdType.MESH


def _axis_slice(ref, axis, start, size):
    idx = [slice(None)] * len(ref.shape)
    idx[axis] = pl.ds(start, size)
    return ref.at[tuple(idx)]


def _all_gather(shard, axis, name):
    m = shard.shape[axis]
    out_shape = list(shard.shape)
    out_shape[axis] = N_DEV * m

    def body(x_ref, out_ref, send_sems, recv_sems, local_sem):
        x, y, c = lax.axis_index("x"), lax.axis_index("y"), lax.axis_index("c")
        me, sibling = (x, y, c), (x, y, 1 - c)
        chips = [(1 - x, y), (x, 1 - y), (1 - x, 1 - y)]

        def block(px, py, pc):
            return _axis_slice(out_ref, axis, (4 * px + 2 * py + pc) * m, m)

        def copy(k, blk, to, src=None):
            return pltpu.make_async_remote_copy(
                src_ref=block(*blk) if src is None else src, dst_ref=block(*blk), send_sem=send_sems.at[k],
                recv_sem=recv_sems.at[k], device_id=to, device_id_type=_MESH)

        mine = pltpu.make_async_copy(x_ref, block(*me), local_sem)
        mine.start()
        first = [copy(0, me, sibling, src=x_ref)]
        first += [copy(1 + j, me, (*chip, c), src=x_ref) for j, chip in enumerate(chips)]
        for cp in first:
            cp.start()
        passed = [copy(4 + j, (*chip, c), sibling) for j, chip in enumerate(chips)]
        for j, chip in enumerate(chips):
            copy(1 + j, (*chip, c), me).wait_recv()
            passed[j].start()
        copy(0, sibling, me).wait_recv()
        for j, chip in enumerate(chips):
            copy(4 + j, (*chip, 1 - c), me).wait_recv()
        for cp in first + passed:
            cp.wait_send()
        mine.wait()

    return pl.pallas_call(
        body, name=name, out_shape=_sds(out_shape, shard.dtype), in_specs=[_HBM], out_specs=_HBM,
        scratch_shapes=[pltpu.SemaphoreType.DMA((7,)), pltpu.SemaphoreType.DMA((7,)), pltpu.SemaphoreType.DMA(())],
    )(shard)


def _rs_sibling(g, axis, name):
    m = g.shape[axis] // N_DEV
    blk_shape = list(g.shape)
    blk_shape[axis] = m
    n_chips = N_DEV // 2

    def body(g_ref, recv_ref, send_sems, recv_sems):
        x, y, c = lax.axis_index("x"), lax.axis_index("y"), lax.axis_index("c")
        sibling = (x, y, 1 - c)
        send = [pltpu.make_async_remote_copy(
            src_ref=_axis_slice(g_ref, axis, (2 * q + 1 - c) * m, m), dst_ref=recv_ref.at[q],
            send_sem=send_sems.at[q], recv_sem=recv_sems.at[q], device_id=sibling, device_id_type=_MESH)
            for q in range(n_chips)]
        for cp in send:
            cp.start()
        for cp in send:
            cp.wait_recv()
        for cp in send:
            cp.wait_send()

    return pl.pallas_call(
        body, name=name, out_shape=_sds([n_chips] + blk_shape, g.dtype), in_specs=[_HBM], out_specs=_HBM,
        scratch_shapes=[pltpu.SemaphoreType.DMA((n_chips,)), pltpu.SemaphoreType.DMA((n_chips,))],
    )(g)


def _rs_chips(p, name):
    n_peers = p.shape[0] - 1

    def body(p_ref, out_ref, send_sems, recv_sems):
        x, y, c = lax.axis_index("x"), lax.axis_index("y"), lax.axis_index("c")
        chips = [(1 - x, y), (x, 1 - y), (1 - x, 1 - y)]
        send = [pltpu.make_async_remote_copy(
            src_ref=p_ref.at[2 * cx + cy], dst_ref=out_ref.at[j], send_sem=send_sems.at[j],
            recv_sem=recv_sems.at[j], device_id=(cx, cy, c), device_id_type=_MESH)
            for j, (cx, cy) in enumerate(chips)]
        for cp in send:
            cp.start()
        for cp in send:
            cp.wait_recv()
        for cp in send:
            cp.wait_send()

    return pl.pallas_call(
        body, name=name, out_shape=_sds((n_peers,) + p.shape[1:], p.dtype), in_specs=[_HBM], out_specs=_HBM,
        scratch_shapes=[pltpu.SemaphoreType.DMA((n_peers,)), pltpu.SemaphoreType.DMA((n_peers,))],
    )(p)


def _reduce_scatter(g, axis, name):
    sib = _rs_sibling(g, axis, name + "_d2d")
    p = _pair_sum(g, sib, axis, name + "_pair")
    return p, _rs_chips(p, name + "_ici")


def _s5_tables(lam_re, lam_im, log_step, b_re, b_im, c_re, c_im):
    g, p, cg = b_re.shape
    step = jnp.exp(log_step)[:, None]
    mag = jnp.exp(lam_re * step)
    a_re, a_im = mag * jnp.cos(lam_im * step), mag * jnp.sin(lam_im * step)
    den = lam_re * lam_re + lam_im * lam_im
    num_re, num_im = a_re - 1.0, a_im
    k_re = (num_re * lam_re + num_im * lam_im) / den
    k_im = (num_im * lam_re - num_re * lam_im) / den
    bb_re = k_re[..., None] * b_re - k_im[..., None] * b_im
    bb_im = k_re[..., None] * b_im + k_im[..., None] * b_re
    gt = g // SSM_TILE_GROUPS
    hg = SSM_HALF_GROUPS
    eye = jnp.eye(SSM_TILE_GROUPS, dtype=F32).reshape(SSM_TILE_GROUPS, 2, hg)

    def pack_b(bb):
        w = jnp.einsum("jhqpc,ghq->jhgcqp", bb.reshape(gt, 2, hg, p, cg), eye)
        return w.reshape(gt * 2, SSM_TILE_GROUPS * cg, hg * p)

    def pack_c(cc):
        w = jnp.einsum("jhqcp,ghq->jhqpgc", cc.reshape(gt, 2, hg, cg, p), eye)
        return w.reshape(gt * 2, hg * p, SSM_TILE_GROUPS * cg)

    a = jnp.stack([a_re.reshape(gt * 2, hg * p), a_im.reshape(gt * 2, hg * p)], axis=1)
    return pack_b(bb_re), pack_b(bb_im), pack_c(c_re), pack_c(c_im), a


def _ret_tables(decay_logit, rev, dk, dv):
    ch = RET_CHUNK
    h = decay_logit.shape[0]
    lg = jax.nn.log_sigmoid(decay_logit)[:, None]
    pos = jnp.arange(ch, dtype=F32)
    diff = pos[:, None] - pos[None, :]
    if rev:
        diff = -diff
        mask = diff > 0
        w_end = jnp.exp(lg * pos)
        w_in = jnp.exp(lg * (ch - pos))
    else:
        mask = diff >= 0
        w_end = jnp.exp(lg * (ch - 1.0 - pos))
        w_in = jnp.exp(lg * (pos + 1.0))
    decay = jnp.where(mask, jnp.exp(lg[:, :, None] * jnp.where(mask, diff, 0.0)), 0.0)
    g_chunk = jnp.exp(lg[:, 0] * ch)
    return (decay, jnp.broadcast_to(w_end[:, :, None], (h, ch, dk)), jnp.broadcast_to(w_in[:, :, None], (h, ch, dk)),
            jnp.broadcast_to(g_chunk[:, None, None], (h, dk, dv)))


def _rope_tables(t_rows, ncc, dk):
    quarter = dk // 4
    idx = np.arange(t_rows)
    row, col = idx // GRID_W, idx % GRID_W
    inv = ROPE_BASE ** (-np.arange(quarter, dtype=np.float32) / quarter)
    ang_r = row.astype(np.float32)[:, None] * inv
    ang_c = col.astype(np.float32)[:, None] * inv
    ang_r, ang_c = jnp.asarray(ang_r, F32), jnp.asarray(ang_c, F32)
    cos = jnp.concatenate([jnp.cos(ang_r), jnp.cos(ang_r), jnp.cos(ang_c), jnp.cos(ang_c)], axis=1)
    sin = jnp.concatenate([-jnp.sin(ang_r), jnp.sin(ang_r), -jnp.sin(ang_c), jnp.sin(ang_c)], axis=1)
    n_ctx = ncc * RET_CHUNK
    cos = jnp.concatenate([jnp.ones((n_ctx, dk), F32), cos], axis=0)
    sin = jnp.concatenate([jnp.zeros((n_ctx, dk), F32), sin], axis=0)
    return cos, sin


def _to_scan_layout(ctx_rows, lat_rows, rev):
    u = jnp.concatenate([lat_rows, ctx_rows] if rev else [ctx_rows, lat_rows], axis=0)
    r, w = u.shape
    return u.reshape(N_DEV, r // N_DEV, w).transpose(1, 0, 2).reshape(r, w)


def _from_scan_layout(yp, n_ctx, rev):
    r, w = yp.shape
    y = yp.reshape(r // N_DEV, N_DEV, w).transpose(1, 0, 2).reshape(r, w)
    return (y[r - n_ctx:], y[:r - n_ctx]) if rev else (y[:n_ctx], y[n_ctx:])


def _pack(parts, width):
    flat = jnp.concatenate([p.reshape(-1).astype(F32) for p in parts])
    n = flat.shape[0]
    quantum = width * SUBLANE
    padded = -(-n // quantum) * quantum
    return jnp.pad(flat, (0, padded - n)).reshape(padded // width, width)


def _unpack(flat2d, shapes):
    flat = flat2d.reshape(-1)
    out, off = [], 0
    for shp in shapes:
        n = int(np.prod(shp))
        out.append(flat[off:off + n].reshape(shp))
        off += n
    return out


def kernel(x, c, ctx, c_ctx, ada_w, ada_b, norm_g, ffn_w_in, ffn_w_out, mix_w_in, ssm_lam_re, ssm_lam_im, ssm_log_step, ssm_b_re, ssm_b_im, ssm_c_re, ssm_c_im, ssm_d, ssm_glu_w, ret_decay_logit, ret_w_proj, mix_w_out, loss_target, m_c_ctx, m_ada_w, m_ada_b, m_norm_g, m_ffn_w_in, m_ffn_w_out, m_mix_w_in, m_ssm_lam_re, m_ssm_lam_im, m_ssm_log_step, m_ssm_b_re, m_ssm_b_im, m_ssm_c_re, m_ssm_c_im, m_ssm_d, m_ssm_glu_w, m_ret_decay_logit, m_ret_w_proj, m_mix_w_out, v_c_ctx, v_ada_w, v_ada_b, v_norm_g, v_ffn_w_in, v_ffn_w_out, v_mix_w_in, v_ssm_lam_re, v_ssm_lam_im, v_ssm_log_step, v_ssm_b_re, v_ssm_b_im, v_ssm_c_re, v_ssm_c_im, v_ssm_d, v_ssm_glu_w, v_ret_decay_logit, v_ret_w_proj, v_mix_w_out):
    t_rows, d = x.shape[1], x.shape[2]
    n_ctx = ctx.shape[1]
    r = n_ctx + t_rows
    ssm_w = ssm_d.shape[1]
    heads = ret_decay_logit.shape[2]
    mi = mix_w_in.shape[2] * N_DEV
    dk = (mi - ssm_w - 2 * d) // (6 * heads)
    dv = 2 * dk
    qk_w, v_w = heads * dk, heads * dv
    q_off = ssm_w
    ncc = n_ctx // RET_CHUNK
    tile = n_ctx
    nct = 1
    wide_tile = _tile(n_ctx, 128, 16)
    assert r % (N_DEV * SUBLANE) == 0 and n_ctx % RET_CHUNK == 0 and t_rows % tile == 0
    me = 4 * lax.axis_index("x") + 2 * lax.axis_index("y") + lax.axis_index("c")
    g_off = ssm_w + 2 * qk_w + v_w
    gs_off = g_off + v_w

    bf = lambda w: w.astype(BF16)
    w_in1 = _all_gather(bf(ffn_w_in[0, 0]), 1, "ag_ffn1_in")
    w_out1 = _all_gather(bf(ffn_w_out[0, 0]), 0, "ag_ffn1_out")
    w_mix = _all_gather(bf(mix_w_in[0]), 1, "ag_mix_in")
    w_glu = _all_gather(bf(ssm_glu_w[0]), 1, "ag_glu")
    w_rp = _all_gather(bf(ret_w_proj[0]), 0, "ag_ret_proj")
    w_mo = _all_gather(bf(mix_w_out[0]), 0, "ag_mix_out")
    w_in2 = _all_gather(bf(ffn_w_in[0, 1]), 1, "ag_ffn2_in")
    w_out2 = _all_gather(bf(ffn_w_out[0, 1]), 0, "ag_ffn2_out")

    ng_cols = norm_g.shape[2]
    small0 = _pack([c[0], norm_g[0]], d)
    small0_all = _all_gather(small0, 0, "ag_cond").reshape(N_DEV, -1)
    c_all = small0_all[:, :d]
    g_full = small0_all[:, d:d + 6 * ng_cols].reshape(N_DEV, 6, ng_cols).transpose(1, 0, 2).reshape(6, d)
    g6 = g_full.reshape(6, 1, d)
    cc = jnp.concatenate([c_all, c_ctx[None, :], jnp.zeros((2 * SUBLANE - N_DEV - 1, d), F32)], axis=0)
    sc = _silu_rows(cc, "ada_silu")
    na = ada_w.shape[2]
    a_loc = _mm(sc, ada_w[0], "nn", F32, "ada_fwd", tm=16, tn=na, tk=512)
    a_all = _all_gather(a_loc, 0, "ag_ada").reshape(N_DEV, 2 * SUBLANE, na)
    ada_x = lax.dynamic_index_in_dim(a_all, me, axis=1, keepdims=False).reshape(9 * d) + ada_b[0]
    ada_c = a_all[:, N_DEV, :].reshape(9 * d) + ada_b[0]
    mods = jnp.stack([ada_c.reshape(9, d), ada_x.reshape(9, d)]).reshape(18, 1, d)

    xin = jnp.concatenate([ctx[0], x[0]], axis=0)
    u1 = _ada_pre_fwd(xin, g6, mods, 0, 0, nct, tile, "pre1")
    h1 = _mm(u1, w_in1, "nn", F32, "ffn1_in", tm=544)
    a1 = _swiglu_fwd(h1, wide_tile, "swiglu1")
    o1 = _mm(a1, w_out1, "nn", F32, "ffn1_out", tm=544, tn=d, tk=1408)
    x1 = _ada_post_fwd(xin, o1, g6, mods, 1, 0, 0.5, nct, tile, "post1")
    u2 = _ada_pre_fwd(x1, g6, mods, 2, 1, nct, tile, "pre2")
    hm = _mm(u2, w_mix, "nn", F32, "mix_in", tm=544)

    us_ctx, us_lat = hm[:n_ctx, :ssm_w], hm[n_ctx:, :ssm_w]
    dskip = ssm_d.reshape(1, 1, ssm_w)
    s5_tabs, s5_vjps, ups, y_dirs = [], [], [], []
    for dr in range(2):
        prm = (ssm_lam_re[0, dr], ssm_lam_im[0, dr], ssm_log_step[0, dr], ssm_b_re[0, dr], ssm_b_im[0, dr],
               ssm_c_re[0, dr], ssm_c_im[0, dr])
        tabs, vjp_fn = jax.vjp(_s5_tables, *prm)
        up = _to_scan_layout(us_ctx, us_lat, dr == 1)
        yp = _s5_fwd(up, *tabs, dr == 1, "s5_fwd%d" % dr)
        s5_tabs.append(tabs)
        s5_vjps.append(vjp_fn)
        ups.append(up)
        y_dirs.append(_from_scan_layout(yp, n_ctx, dr == 1)[1])
    a_ssm = _ssm_out_fwd(y_dirs[0], y_dirs[1], hm, dskip, nct, tile, "ssm_out")
    gab = _mm(a_ssm, w_glu, "nn", F32, "glu", tm=512, tn=2048, tk=ssm_w)

    cos, sin = _rope_tables(t_rows, ncc, dk)
    ret_tabs, ret_vjps, o_dirs, s_ins = [], [], [], []
    for dr in range(2):
        tabs, vjp_fn = jax.vjp(functools.partial(_ret_tables, rev=dr == 1, dk=dk, dv=dv), ret_decay_logit[0, dr])
        o_d, s_in = _ret_fwd(hm, cos, sin, *tabs, heads, dk, dv, q_off, ncc, dr == 1, "ret_fwd%d" % dr)
        ret_tabs.append(tabs)
        ret_vjps.append(vjp_fn)
        o_dirs.append(o_d)
        s_ins.append(s_in)
    ret_in = _ret_gate_fwd(o_dirs[0], o_dirs[1], hm, g_off, heads, dv, nct, tile, "ret_gate")
    rb = _mm(ret_in, w_rp, "nn", F32, "ret_proj", tm=512, tn=d, tk=v_w)
    merged = _merge_fwd(gab, rb, hm, gs_off, nct, tile, "merge")
    mix = _mm(merged, w_mo, "nn", F32, "mix_out", tm=512, tn=d, tk=d)
    x1x = x1[n_ctx:]
    x2 = _ada_post_fwd(x1x, mix, g6, mods, 3, 1, 1.0, 0, tile, "post2")
    u3 = _ada_pre_fwd(x2, g6, mods, 4, 2, 0, tile, "pre3")
    h3 = _mm(u3, w_in2, "nn", F32, "ffn2_in", tm=512)
    a3 = _swiglu_fwd(h3, wide_tile, "swiglu2")
    o3 = _mm(a3, w_out2, "nn", F32, "ffn2_out", tm=512, tn=d, tk=1408)
    x3 = _ada_post_fwd(x2, o3, g6, mods, 5, 2, 0.5, 0, tile, "post3")
    dy, lcols = _loss_grad(x3, loss_target[0], tile, "loss")
    loss = lax.psum(0.5 * jnp.sum(lcols) / d, MESH_AXES)

    dg6 = [None] * 6
    dmod = {}

    def add_mod(sel_rows, k, val):
        for sel, row in sel_rows:
            dmod[(sel, k)] = dmod.get((sel, k), 0.0) + val[row, 0]

    both, lat = [(0, 0), (1, 1)], [(1, 0)]
    do3, dg6[5], dgt = _ada_post_bwd(dy, o3, g6, mods, 5, 2, 0.5, 0, 1, tile, "post3_bwd")
    add_mod(lat, 8, dgt)
    da3 = _mm(do3, w_out2, "nt", F32, "ffn2_out_dx", tm=512, tn=1408, tk=d)
    gw_out2 = _mm(a3, do3, "tn", BF16, "ffn2_out_dw", tm=1408, tn=d, tk=512)
    dh3 = _swiglu_bwd(h3, da3, wide_tile, "swiglu2_bwd")
    du3 = _mm(dh3, w_in2, "nt", F32, "ffn2_in_dx", tm=512, tn=d, tk=1408)
    gw_in2 = _mm(u3, dh3, "tn", BF16, "ffn2_in_dw", tm=d, tn=1408, tk=512)
    dx2, dg6[4], dsh, dsc = _ada_pre_bwd(x2, du3, dy, g6, mods, 4, 2, 0, 1, tile, "pre3_bwd")
    add_mod(lat, 6, dsh)
    add_mod(lat, 7, dsc)
    dmix, dg6[3], dgt = _ada_post_bwd(dx2, mix, g6, mods, 3, 1, 1.0, 0, 1, tile, "post2_bwd")
    add_mod(lat, 5, dgt)
    dmerged = _mm(dmix, w_mo, "nt", F32, "mix_out_dx", tm=512, tn=d, tk=d)
    gw_mo = _mm(merged, dmix, "tn", BF16, "mix_out_dw", tm=d, tn=d, tk=512)
    dgab, drb, dgs, dgr = _merge_bwd(gab, rb, hm, gs_off, dmerged, nct, tile, "merge_bwd")
    da_ssm = _mm(dgab, w_glu, "nt", F32, "glu_dx", tm=512, tn=ssm_w, tk=2 * d)
    gw_glu = _mm(a_ssm, dgab, "tn", BF16, "glu_dw", tm=ssm_w, tn=2 * d, tk=512)
    dret_in = _mm(drb, w_rp, "nt", F32, "ret_proj_dx", tm=512, tn=v_w, tk=d)
    gw_rp = _mm(ret_in, drb, "tn", BF16, "ret_proj_dw", tm=v_w, tn=d, tk=512)
    d_o, dg_gate = _ret_gate_bwd(o_dirs[0], o_dirs[1], hm, g_off, dret_in, heads, dv, nct, tile, "ret_gate_bwd")
    dy_ssm, dus_direct, d_dskip = _ssm_out_bwd(y_dirs[0], y_dirs[1], hm, dskip, da_ssm, nct, tile, "ssm_out_bwd")
    dqkv, g_decay = [], []
    for dr in range(2):
        outs = _ret_bwd(hm, cos, sin, *ret_tabs[dr], s_ins[dr], d_o, heads, dk, dv, q_off, ncc, dr == 1,
                        "ret_bwd%d" % dr)
        dqkv.append(outs[:3])
        (gl,) = ret_vjps[dr](tuple(outs[3:]))
        g_decay.append(gl)
    g_s5, du_ctx, du_lat = [], [], [dus_direct]
    for dr in range(2):
        dyp = _to_scan_layout(jnp.zeros((n_ctx, ssm_w), F32), dy_ssm, dr == 1)
        outs = _s5_bwd(ups[dr], dyp, *s5_tabs[dr], dr == 1, "s5_bwd%d" % dr)
        part_ctx, part_lat = _from_scan_layout(outs[0], n_ctx, dr == 1)
        du_ctx.append(part_ctx)
        du_lat.append(part_lat)
        g_s5.append(s5_vjps[dr](tuple(outs[1:])))
    dus = jnp.concatenate([du_ctx[0] + du_ctx[1], du_lat[0] + du_lat[1] + du_lat[2]], axis=0)
    dhm = _assemble_dhm(dus, dqkv[0][0], dqkv[1][0], dqkv[0][1], dqkv[1][1], dqkv[0][2], dqkv[1][2],
                        dg_gate, dgs, dgr, n_ctx // wide_tile, wide_tile, "assemble_dhm")
    du2 = _mm(dhm, w_mix, "nt", F32, "mix_in_dx", tm=544, tn=d, tk=1408)
    gw_mix = _mm(u2, dhm, "tn", BF16, "mix_in_dw", tm=d, tn=1408, tk=544)
    dx1, dg6[2], dsh, dsc = _ada_pre_bwd(x1, du2, dx2, g6, mods, 2, 1, nct, 2, tile, "pre2_bwd", dres_x_only=True)
    add_mod(both, 3, dsh)
    add_mod(both, 4, dsc)
    do1, dg6[1], dgt = _ada_post_bwd(dx1, o1, g6, mods, 1, 0, 0.5, nct, 2, tile, "post1_bwd")
    add_mod(both, 2, dgt)
    da1 = _mm(do1, w_out1, "nt", F32, "ffn1_out_dx", tm=544, tn=1408, tk=d)
    gw_out1 = _mm(a1, do1, "tn", BF16, "ffn1_out_dw", tm=1408, tn=d, tk=544)
    dh1 = _swiglu_bwd(h1, da1, wide_tile, "swiglu1_bwd")
    du1 = _mm(dh1, w_in1, "nt", F32, "ffn1_in_dx", tm=544, tn=d, tk=1408)
    gw_in1 = _mm(u1, dh1, "tn", BF16, "ffn1_in_dw", tm=d, tn=1408, tk=544)
    dxin, dg6[0], dsh, dsc = _ada_pre_bwd(xin, du1, dx1, g6, mods, 0, 0, nct, 2, tile, "pre1_bwd")
    add_mod(both, 0, dsh)
    add_mod(both, 1, dsc)
    grad_x = dxin[n_ctx:][None]

    zero_d = jnp.zeros((d,), F32)
    d_ada_x = jnp.stack([dmod.get((1, k), zero_d) for k in range(9)]).reshape(9 * d)
    d_ada_c = jnp.stack([dmod.get((0, k), zero_d) for k in range(9)]).reshape(9 * d)
    dg_full = jnp.stack([g[0, 0] for g in dg6])
    s5_names = 7
    s5_stack = [jnp.stack([g_s5[0][i], g_s5[1][i]]) for i in range(s5_names)]
    small_parts = [d_ada_x, d_ada_c, dg_full] + s5_stack + [d_dskip, jnp.stack(g_decay)]
    small_shapes = [p.shape for p in small_parts]
    packed = _pack(small_parts, 1024)
    gathered = _all_gather(packed, 0, "ag_small_grads").reshape(N_DEV, -1, 1024)
    summed = _sum_leading(gathered, "sum_small_grads")
    sums = _unpack(summed, small_shapes)
    sum_dx, sum_dc, sum_dg = sums[0], sums[1], sums[2]
    grad_ada_b = (sum_dx + sum_dc)[None]
    dx_rows = gathered.reshape(N_DEV, -1)[:, :9 * d]
    col0 = me * na
    da_rows = jnp.concatenate([lax.dynamic_slice_in_dim(dx_rows, col0, na, axis=1),
                               lax.dynamic_slice_in_dim(sum_dc[None], col0, na, axis=1),
                               jnp.zeros((2 * SUBLANE - N_DEV - 1, na), F32)], axis=0)
    grad_ada_w = _mm(sc, da_rows, "tn", F32, "ada_dw", tm=512, tn=na, tk=16)
    d_sc = _mm(da_rows, ada_w[0], "nt", F32, "ada_dx", tm=16, tn=512, tk=na)
    d_sc_all = _all_gather(jnp.broadcast_to(d_sc[N_DEV:N_DEV + 1], (SUBLANE, d)), 0, "ag_dctx")
    d_sc_sum = _sum_leading(d_sc_all.reshape(N_DEV, SUBLANE, d), "sum_dctx")
    grad_c_ctx = _silu_grad_rows(jnp.broadcast_to(c_ctx[None], (SUBLANE, d)), d_sc_sum, "ctx_silu_bwd")[0]
    grad_norm_g = lax.dynamic_slice_in_dim(sum_dg, me * ng_cols, ng_cols, axis=1)[None]

    def big_update(w2d, m2d, v2d, gfull, axis, name):
        p, recv = _reduce_scatter(gfull, axis, "rs_" + name)
        return _adamw_scattered(w2d, m2d, v2d, p, recv, "adamw_" + name)

    upd = {}
    res_in = [big_update(ffn_w_in[0, l], m_ffn_w_in[0, l], v_ffn_w_in[0, l], gw, 1, "ffn%d_in" % (l + 1))
              for l, gw in enumerate([gw_in1, gw_in2])]
    upd["ffn_w_in"] = [jnp.stack([res_in[0][i], res_in[1][i]])[None] for i in range(4)]
    res_out = [big_update(ffn_w_out[0, l], m_ffn_w_out[0, l], v_ffn_w_out[0, l], gw, 0, "ffn%d_out" % (l + 1))
               for l, gw in enumerate([gw_out1, gw_out2])]
    upd["ffn_w_out"] = [jnp.stack([res_out[0][i], res_out[1][i]])[None] for i in range(4)]
    upd["mix_w_in"] = [o[None] for o in big_update(mix_w_in[0], m_mix_w_in[0], v_mix_w_in[0], gw_mix, 1, "mix_in")]
    upd["ssm_glu_w"] = [o[None] for o in big_update(ssm_glu_w[0], m_ssm_glu_w[0], v_ssm_glu_w[0], gw_glu, 1, "glu")]
    upd["ret_w_proj"] = [o[None] for o in big_update(ret_w_proj[0], m_ret_w_proj[0], v_ret_w_proj[0], gw_rp, 0,
                                                     "ret_proj")]
    upd["mix_w_out"] = [o[None] for o in big_update(mix_w_out[0], m_mix_w_out[0], v_mix_w_out[0], gw_mo, 0,
                                                    "mix_out")]
    upd["ada_w"] = [o[None] for o in _adamw(ada_w[0], m_ada_w[0], v_ada_w[0], grad_ada_w[None], "adamw_ada_w")]

    small_names = ["c_ctx", "ada_b", "norm_g", "ssm_lam_re", "ssm_lam_im", "ssm_log_step", "ssm_b_re", "ssm_b_im",
                   "ssm_c_re", "ssm_c_im", "ssm_d", "ret_decay_logit"]
    small_w = [c_ctx, ada_b, norm_g, ssm_lam_re, ssm_lam_im, ssm_log_step, ssm_b_re, ssm_b_im, ssm_c_re, ssm_c_im,
               ssm_d, ret_decay_logit]
    small_m = [m_c_ctx, m_ada_b, m_norm_g, m_ssm_lam_re, m_ssm_lam_im, m_ssm_log_step, m_ssm_b_re, m_ssm_b_im,
               m_ssm_c_re, m_ssm_c_im, m_ssm_d, m_ret_decay_logit]
    small_v = [v_c_ctx, v_ada_b, v_norm_g, v_ssm_lam_re, v_ssm_lam_im, v_ssm_log_step, v_ssm_b_re, v_ssm_b_im,
               v_ssm_c_re, v_ssm_c_im, v_ssm_d, v_ret_decay_logit]
    small_g = [grad_c_ctx, grad_ada_b, grad_norm_g] + [s[None] for s in sums[3:3 + s5_names]] + \
              [sums[3 + s5_names].reshape(ssm_d.shape), sums[4 + s5_names][None]]
    shapes = [w.shape for w in small_w]
    res = _adamw(_pack(small_w, 1024), _pack(small_m, 1024), _pack(small_v, 1024), _pack(small_g, 1024)[None],
                 "adamw_small")
    small_out = [_unpack(o, shapes) for o in res]
    for i, nm in enumerate(small_names):
        upd[nm] = [small_out[kind][i] for kind in range(4)]

    order = ["c_ctx", "ada_w", "ada_b", "norm_g", "ffn_w_in", "ffn_w_out", "mix_w_in", "ssm_lam_re", "ssm_lam_im",
             "ssm_log_step", "ssm_b_re", "ssm_b_im", "ssm_c_re", "ssm_c_im", "ssm_d", "ssm_glu_w", "ret_decay_logit",
             "ret_w_proj", "mix_w_out"]
    outs = [loss, grad_x]
    for kind in range(4):
        outs += [upd[nm][kind] for nm in order]
    return tuple(outs)
```

```python
import functools
import math

import jax
import jax.numpy as jnp
import numpy as np
from jax import lax
from jax.experimental import pallas as pl
from jax.experimental.pallas import tpu as pltpu
from jax.experimental.pallas import tpu_sc as plsc

F32 = jnp.float32
BF16 = jnp.bfloat16
MXU_DTYPE = jnp.bfloat16
MESH_AXES = ("x", "y", "c")
N_DEV = 8
V7X_VMEM_LIMIT_BYTES = 56 * 1024 * 1024
LANE = 128
SUBLANE = 8

GRID_W = 64
RET_CHUNK = 128
ROPE_BASE = 10000.0
NORM_EPS = 1e-6
ADAM_LR = 0.001
ADAM_B1 = 0.9
ADAM_B2 = 0.999
ADAM_EPS = 1e-08
ADAM_WD = 0.01
ADAM_STEP = 10
SSM_TILE_GROUPS = 8
SSM_HALF_GROUPS = 4


def _params(sem=None):
    return pltpu.CompilerParams(dimension_semantics=sem, vmem_limit_bytes=V7X_VMEM_LIMIT_BYTES)


def _tile(n, target, mult):
    best = None
    t = mult
    while t <= min(n, target):
        if n % t == 0:
            best = t
        t += mult
    return n if best is None else best


def _sds(shape, dtype):
    return jax.ShapeDtypeStruct(tuple(shape), dtype)


def _mm(a, b, dims, out_dtype, name, tm=512, tn=1408, tk=2048):
    if dims == "nn":
        (m, k), (k2, n) = a.shape, b.shape
    elif dims == "nt":
        (m, k), (n, k2) = a.shape, b.shape
    else:
        (k, m), (k2, n) = a.shape, b.shape
    assert k == k2, (a.shape, b.shape, dims)
    tm = _tile(m, tm, 16)
    tn = _tile(n, tn, LANE)
    tk = _tile(k, tk, LANE if dims != "tn" else 16)
    nk = k // tk
    dn = {"nn": (((1,), (0,)), ((), ())), "nt": (((1,), (1,)), ((), ())), "tn": (((0,), (0,)), ((), ()))}[dims]

    def body(a_ref, b_ref, o_ref, acc_ref):
        kk = pl.program_id(2)

        @pl.when(kk == 0)
        def _():
            acc_ref[...] = jnp.zeros_like(acc_ref)

        acc_ref[...] += lax.dot_general(a_ref[...].astype(MXU_DTYPE), b_ref[...].astype(MXU_DTYPE), dn,
                                        preferred_element_type=F32)

        @pl.when(kk == nk - 1)
        def _():
            o_ref[...] = acc_ref[...].astype(o_ref.dtype)

    if dims == "nn":
        a_spec = pl.BlockSpec((tm, tk), lambda j, i, kk: (i, kk))
        b_spec = pl.BlockSpec((tk, tn), lambda j, i, kk: (kk, j))
    elif dims == "nt":
        a_spec = pl.BlockSpec((tm, tk), lambda j, i, kk: (i, kk))
        b_spec = pl.BlockSpec((tn, tk), lambda j, i, kk: (j, kk))
    else:
        a_spec = pl.BlockSpec((tk, tm), lambda j, i, kk: (kk, i))
        b_spec = pl.BlockSpec((tk, tn), lambda j, i, kk: (kk, j))
    return pl.pallas_call(
        body, name=name, grid=(n // tn, m // tm, nk), in_specs=[a_spec, b_spec],
        out_specs=pl.BlockSpec((tm, tn), lambda j, i, kk: (i, j)), out_shape=_sds((m, n), out_dtype),
        scratch_shapes=[pltpu.VMEM((tm, tn), F32)],
        compiler_params=_params(("parallel", "parallel", "arbitrary")))(a, b)


def _rows(name, body, n_tiles, ins, outs):
    in_specs = [pl.BlockSpec(blk, imap) for (_, blk, imap) in ins]
    out_specs = [pl.BlockSpec(blk, imap) for (_, _, blk, imap) in outs]
    out_shape = [_sds(shape, dt) for (shape, dt, _, _) in outs]
    res = pl.pallas_call(body, name=name, grid=(n_tiles,), in_specs=in_specs, out_specs=out_specs,
                         out_shape=out_shape, compiler_params=_params(("arbitrary",)))(*[a for (a, _, _) in ins])
    return res


def _row_in(arr, tile, width=None, col=0, x_only_offset=None):
    width = arr.shape[1] if width is None else width
    if x_only_offset is None:
        return (arr, (tile, width), lambda i: (i, col))
    return (arr, (tile, width), lambda i: (jnp.maximum(i - x_only_offset, 0), col))


def _vec_in(arr, idx_fn):
    return (arr, (1, 1, arr.shape[2]), lambda i: (idx_fn(i), 0, 0))


def _rms(h):
    return lax.rsqrt(jnp.mean(h * h, axis=-1, keepdims=True) + NORM_EPS)


def _sigmoid(z):
    return 1.0 / (1.0 + jnp.exp(-z))


def _ada_pre_fwd(h, g6, mods, gi, mi, nct, tile, name):
    r, d = h.shape
    sel = lambda i: jnp.where(i >= nct, 1, 0)

    def body(h_ref, g_ref, sh_ref, sc_ref, u_ref):
        hh = h_ref[...]
        n = hh * _rms(hh) * g_ref[0]
        u_ref[...] = (n * (1.0 + sc_ref[0]) + sh_ref[0]).astype(u_ref.dtype)

    (u,) = _rows(name, body, r // tile,
                 [_row_in(h, tile), _vec_in(g6, lambda i: gi), _vec_in(mods, lambda i: sel(i) * 9 + 3 * mi),
                  _vec_in(mods, lambda i: sel(i) * 9 + 3 * mi + 1)],
                 [((r, d), BF16, (tile, d), lambda i: (i, 0))])
    return u


def _ada_pre_bwd(h, du, dres, g6, mods, gi, mi, nct, nsel, tile, name, dres_x_only=False):
    r, d = h.shape
    sel = lambda i: jnp.where(i >= nct, 1, 0) if nsel == 2 else 0
    msel = lambda i: jnp.where(i >= nct, 1, 0)
    off = nct if dres_x_only else None

    def body(h_ref, du_ref, dr_ref, g_ref, sc_ref, dh_ref, dg_ref, dsh_ref, dsc_ref):
        i = pl.program_id(0)
        hh = h_ref[...]
        rr = _rms(hh)
        g = g_ref[0]
        hn = hh * rr
        n = hn * g
        du_ = du_ref[...].astype(F32)
        dn = du_ * (1.0 + sc_ref[0])

        @pl.when(i == 0)
        def _():
            dg_ref[...] = jnp.zeros_like(dg_ref)

        @pl.when((i == 0) | (i == nct))
        def _():
            dsh_ref[...] = jnp.zeros_like(dsh_ref)
            dsc_ref[...] = jnp.zeros_like(dsc_ref)

        dg_ref[0] += jnp.sum(dn * hn, axis=0, keepdims=True)
        dsh_ref[0] += jnp.sum(du_, axis=0, keepdims=True)
        dsc_ref[0] += jnp.sum(du_ * n, axis=0, keepdims=True)
        t = dn * g
        dh = rr * t - hn * (rr * jnp.mean(t * hn, axis=-1, keepdims=True))
        if dres_x_only:
            dh_ref[...] = dh + jnp.where(i >= nct, dr_ref[...], 0.0)
        else:
            dh_ref[...] = dh + dr_ref[...]

    dh, dg, dsh, dsc = _rows(
        name, body, r // tile,
        [_row_in(h, tile), _row_in(du, tile), _row_in(dres, tile, x_only_offset=off), _vec_in(g6, lambda i: gi),
         _vec_in(mods, lambda i: msel(i) * 9 + 3 * mi + 1)],
        [((r, d), F32, (tile, d), lambda i: (i, 0)), ((1, 1, d), F32, (1, 1, d), lambda i: (0, 0, 0)),
         ((nsel, 1, d), F32, (1, 1, d), lambda i: (sel(i), 0, 0)),
         ((nsel, 1, d), F32, (1, 1, d), lambda i: (sel(i), 0, 0))])
    return dh, dg, dsh, dsc


def _ada_post_fwd(h, o, g6, mods, gi, mi, res_w, nct, tile, name, h_x_only=False):
    r, d = o.shape
    sel = lambda i: jnp.where(i >= nct, 1, 0)

    def body(h_ref, o_ref, g_ref, gt_ref, y_ref):
        oo = o_ref[...]
        n = oo * _rms(oo) * g_ref[0]
        y_ref[...] = h_ref[...] + res_w * gt_ref[0] * n

    (y,) = _rows(name, body, r // tile,
                 [_row_in(h, tile), _row_in(o, tile), _vec_in(g6, lambda i: gi),
                  _vec_in(mods, lambda i: sel(i) * 9 + 3 * mi + 2)],
                 [((r, d), F32, (tile, d), lambda i: (i, 0))])
    return y


def _ada_post_bwd(dy, o, g6, mods, gi, mi, res_w, nct, nsel, tile, name):
    r, d = o.shape
    sel = lambda i: jnp.where(i >= nct, 1, 0) if nsel == 2 else 0
    msel = lambda i: jnp.where(i >= nct, 1, 0)

    def body(dy_ref, o_ref, g_ref, gt_ref, do_ref, dg_ref, dgt_ref):
        i = pl.program_id(0)
        oo = o_ref[...]
        rr = _rms(oo)
        g = g_ref[0]
        on = oo * rr
        dy_ = dy_ref[...] * res_w

        @pl.when(i == 0)
        def _():
            dg_ref[...] = jnp.zeros_like(dg_ref)

        @pl.when((i == 0) | (i == nct))
        def _():
            dgt_ref[...] = jnp.zeros_like(dgt_ref)

        dgt_ref[0] += jnp.sum(dy_ * (on * g), axis=0, keepdims=True)
        dn = dy_ * gt_ref[0]
        dg_ref[0] += jnp.sum(dn * on, axis=0, keepdims=True)
        t = dn * g
        do_ref[...] = rr * t - on * (rr * jnp.mean(t * on, axis=-1, keepdims=True))

    do, dg, dgt = _rows(
        name, body, r // tile,
        [_row_in(dy, tile), _row_in(o, tile), _vec_in(g6, lambda i: gi),
         _vec_in(mods, lambda i: msel(i) * 9 + 3 * mi + 2)],
        [((r, d), F32, (tile, d), lambda i: (i, 0)), ((1, 1, d), F32, (1, 1, d), lambda i: (0, 0, 0)),
         ((nsel, 1, d), F32, (1, 1, d), lambda i: (sel(i), 0, 0))])
    return do, dg, dgt


def _swiglu_fwd(h, tile, name):
    r, w2 = h.shape
    f = w2 // 2

    def body(h_ref, a_ref):
        gt = h_ref[:, :f]
        up = h_ref[:, f:]
        a_ref[...] = (gt * _sigmoid(gt) * up).astype(a_ref.dtype)

    (a,) = _rows(name, body, r // tile, [_row_in(h, tile)], [((r, f), BF16, (tile, f), lambda i: (i, 0))])
    return a


def _swiglu_bwd(h, da, tile, name):
    r, w2 = h.shape
    f = w2 // 2

    def body(h_ref, da_ref, dh_ref):
        gt = h_ref[:, :f]
        up = h_ref[:, f:]
        d = da_ref[...]
        sg = _sigmoid(gt)
        dh_ref[:, :f] = (d * up * (sg * (1.0 + gt * (1.0 - sg)))).astype(dh_ref.dtype)
        dh_ref[:, f:] = (d * gt * sg).astype(dh_ref.dtype)

    (dh,) = _rows(name, body, r // tile, [_row_in(h, tile), _row_in(da, tile)],
                  [((r, w2), BF16, (tile, w2), lambda i: (i, 0))])
    return dh


def _gelu_parts(y):
    c0 = math.sqrt(2.0 / math.pi)
    inner = c0 * (y + 0.044715 * y * y * y)
    th = jnp.tanh(inner)
    return th, c0 * (1.0 + 3 * 0.044715 * y * y)


def _ssm_out_fwd(y0, y1, hm, dskip, nct, tile, name):
    t_rows, s = y0.shape

    def body(y0_ref, y1_ref, u_ref, d_ref, a_ref):
        y = y0_ref[...] + y1_ref[...] + d_ref[0] * u_ref[...]
        th, _ = _gelu_parts(y)
        a_ref[...] = (0.5 * y * (1.0 + th)).astype(a_ref.dtype)

    (a,) = _rows(name, body, t_rows // tile,
                 [_row_in(y0, tile), _row_in(y1, tile), (hm, (tile, s), lambda i: (i + nct, 0)),
                  _vec_in(dskip, lambda i: 0)],
                 [((t_rows, s), BF16, (tile, s), lambda i: (i, 0))])
    return a


def _ssm_out_bwd(y0, y1, hm, dskip, da, nct, tile, name):
    t_rows, s = y0.shape

    def body(y0_ref, y1_ref, u_ref, d_ref, da_ref, dy_ref, du_ref, dd_ref):
        i = pl.program_id(0)
        u = u_ref[...]
        y = y0_ref[...] + y1_ref[...] + d_ref[0] * u
        th, dinner = _gelu_parts(y)
        dy = da_ref[...] * (0.5 * (1.0 + th) + 0.5 * y * (1.0 - th * th) * dinner)
        dy_ref[...] = dy
        du_ref[...] = dy * d_ref[0]

        @pl.when(i == 0)
        def _():
            dd_ref[...] = jnp.zeros_like(dd_ref)

        dd_ref[0] += jnp.sum(dy * u, axis=0, keepdims=True)

    dy, du, dd = _rows(name, body, t_rows // tile,
                       [_row_in(y0, tile), _row_in(y1, tile), (hm, (tile, s), lambda i: (i + nct, 0)),
                        _vec_in(dskip, lambda i: 0), _row_in(da, tile)],
                       [((t_rows, s), F32, (tile, s), lambda i: (i, 0)), ((t_rows, s), F32, (tile, s), lambda i: (i, 0)),
                        ((1, 1, s), F32, (1, 1, s), lambda i: (0, 0, 0))])
    return dy, du, dd


def _col_pieces(arr, off, width, tile, nct, unit=None):
    pw = math.gcd(off, width if unit is None else unit)
    specs = [(arr, (tile, pw), functools.partial(lambda i, cb: (i + nct, cb), cb=off // pw + p))
             for p in range(width // pw)]
    return specs, pw


def _ret_gate_fwd(o0, o1, hm, g_off, heads, dv, nct, tile, name):
    t_rows, w = o0.shape
    g_specs, pw = _col_pieces(hm, g_off, w, tile, nct)
    ng = len(g_specs)

    def body(o0_ref, o1_ref, *refs):
        g_refs, r_ref = refs[:ng], refs[ng]
        for hd in range(heads):
            cs = slice(hd * dv, (hd + 1) * dv)
            o = o0_ref[:, cs] + o1_ref[:, cs]
            lo = (hd * dv) % pw
            g = g_refs[(hd * dv) // pw][:, lo:lo + dv]
            r_ref[:, cs] = (g * _sigmoid(g) * (o * _rms(o))).astype(r_ref.dtype)

    (ri,) = _rows(name, body, t_rows // tile, [_row_in(o0, tile), _row_in(o1, tile)] + g_specs,
                  [((t_rows, w), BF16, (tile, w), lambda i: (i, 0))])
    return ri


def _ret_gate_bwd(o0, o1, hm, g_off, dri, heads, dv, nct, tile, name):
    t_rows, w = o0.shape
    g_specs, pw = _col_pieces(hm, g_off, w, tile, nct)
    ng = len(g_specs)

    def body(o0_ref, o1_ref, d_ref, *refs):
        g_refs, do_ref, dg_ref = refs[:ng], refs[ng], refs[ng + 1]
        for hd in range(heads):
            cs = slice(hd * dv, (hd + 1) * dv)
            o = o0_ref[:, cs] + o1_ref[:, cs]
            lo = (hd * dv) % pw
            g = g_refs[(hd * dv) // pw][:, lo:lo + dv]
            d = d_ref[:, cs]
            rr = _rms(o)
            on = o * rr
            sg = _sigmoid(g)
            dg_ref[:, cs] = d * on * (sg * (1.0 + g * (1.0 - sg)))
            t = d * (g * sg)
            do_ref[:, cs] = rr * t - on * (rr * jnp.mean(t * on, axis=-1, keepdims=True))

    do, dg = _rows(name, body, t_rows // tile, [_row_in(o0, tile), _row_in(o1, tile), _row_in(dri, tile)] + g_specs,
                   [((t_rows, w), F32, (tile, w), lambda i: (i, 0)), ((t_rows, w), F32, (tile, w), lambda i: (i, 0))])
    return do, dg


def _merge_fwd(gab, rb, hm, gs_off, nct, tile, name):
    t_rows, d = rb.shape
    specs, pw = _col_pieces(hm, gs_off, 2 * d, tile, nct, unit=d)
    npc = d // pw

    def body(gab_ref, rb_ref, *refs):
        gs_refs, gr_refs, m_ref = refs[:npc], refs[npc:2 * npc], refs[2 * npc]
        for p in range(npc):
            cs = slice(p * pw, (p + 1) * pw)
            ga = gab_ref[:, cs]
            gb = gab_ref[:, d + p * pw:d + (p + 1) * pw]
            m_ref[:, cs] = (_sigmoid(gs_refs[p][...]) * (ga * _sigmoid(gb))
                            + _sigmoid(gr_refs[p][...]) * rb_ref[:, cs]).astype(m_ref.dtype)

    (mg,) = _rows(name, body, t_rows // tile, [_row_in(gab, tile), _row_in(rb, tile)] + specs,
                  [((t_rows, d), BF16, (tile, d), lambda i: (i, 0))])
    return mg


def _merge_bwd(gab, rb, hm, gs_off, dm, nct, tile, name):
    t_rows, d = rb.shape
    specs, pw = _col_pieces(hm, gs_off, 2 * d, tile, nct, unit=d)
    npc = d // pw

    def body(gab_ref, rb_ref, dm_ref, *refs):
        gs_refs, gr_refs = refs[:npc], refs[npc:2 * npc]
        dgab_ref, drb_ref, dgs_ref, dgr_ref = refs[2 * npc:]
        for p in range(npc):
            cs = slice(p * pw, (p + 1) * pw)
            cs2 = slice(d + p * pw, d + (p + 1) * pw)
            ga = gab_ref[:, cs]
            gb = gab_ref[:, cs2]
            dmm = dm_ref[:, cs]
            ss = _sigmoid(gs_refs[p][...])
            sr = _sigmoid(gr_refs[p][...])
            sb = _sigmoid(gb)
            dbr = dmm * ss
            dgab_ref[:, cs] = (dbr * sb).astype(dgab_ref.dtype)
            dgab_ref[:, cs2] = (dbr * ga * sb * (1.0 - sb)).astype(dgab_ref.dtype)
            drb_ref[:, cs] = (dmm * sr).astype(drb_ref.dtype)
            dgs_ref[:, cs] = dmm * (ga * sb) * ss * (1.0 - ss)
            dgr_ref[:, cs] = dmm * rb_ref[:, cs] * sr * (1.0 - sr)

    return _rows(name, body, t_rows // tile, [_row_in(gab, tile), _row_in(rb, tile), _row_in(dm, tile)] + specs,
                 [((t_rows, 2 * d), BF16, (tile, 2 * d), lambda i: (i, 0)), ((t_rows, d), BF16, (tile, d), lambda i: (i, 0)),
                  ((t_rows, d), F32, (tile, d), lambda i: (i, 0)), ((t_rows, d), F32, (tile, d), lambda i: (i, 0))])


def _assemble_dhm(dus, dq0, dq1, dk0, dk1, dv0, dv1, dg, dgs, dgr, nct, tile, name):
    r, s = dus.shape
    qk = dq0.shape[1]
    vw = dv0.shape[1]
    d = dgs.shape[1]
    mi = s + 2 * qk + 2 * vw + 2 * d
    c_q, c_k, c_v, c_g, c_gs, c_gr = s, s + qk, s + 2 * qk, s + 2 * qk + vw, s + 2 * qk + 2 * vw, s + 2 * qk + 2 * vw + d

    def body(dus_ref, dq0_ref, dq1_ref, dk0_ref, dk1_ref, dv0_ref, dv1_ref, dg_ref, dgs_ref, dgr_ref, o_ref):
        i = pl.program_id(0)
        lat = i >= nct
        o_ref[:, :s] = dus_ref[...].astype(o_ref.dtype)
        o_ref[:, c_q:c_k] = (dq0_ref[...] + dq1_ref[...]).astype(o_ref.dtype)
        o_ref[:, c_k:c_v] = (dk0_ref[...] + dk1_ref[...]).astype(o_ref.dtype)
        o_ref[:, c_v:c_g] = (dv0_ref[...] + dv1_ref[...]).astype(o_ref.dtype)
        o_ref[:, c_g:c_gs] = jnp.where(lat, dg_ref[...], 0.0).astype(o_ref.dtype)
        o_ref[:, c_gs:c_gr] = jnp.where(lat, dgs_ref[...], 0.0).astype(o_ref.dtype)
        o_ref[:, c_gr:] = jnp.where(lat, dgr_ref[...], 0.0).astype(o_ref.dtype)

    (out,) = _rows(name, body, r // tile,
                   [_row_in(dus, tile), _row_in(dq0, tile), _row_in(dq1, tile), _row_in(dk0, tile), _row_in(dk1, tile),
                    _row_in(dv0, tile), _row_in(dv1, tile), _row_in(dg, tile, x_only_offset=nct),
                    _row_in(dgs, tile, x_only_offset=nct), _row_in(dgr, tile, x_only_offset=nct)],
                   [((r, mi), BF16, (tile, mi), lambda i: (i, 0))])
    return out


def _loss_grad(y, target, tile, name):
    t_rows, d = y.shape

    def body(y_ref, t_ref, dy_ref, l_ref):
        i = pl.program_id(0)
        e = y_ref[...] - t_ref[...]
        dy_ref[...] = e * (1.0 / d)

        @pl.when(i == 0)
        def _():
            l_ref[...] = jnp.zeros_like(l_ref)

        l_ref[0] += jnp.sum(e * e, axis=0, keepdims=True)

    return _rows(name, body, t_rows // tile, [_row_in(y, tile), _row_in(target, tile)],
                 [((t_rows, d), F32, (tile, d), lambda i: (i, 0)), ((1, 1, d), F32, (1, 1, d), lambda i: (0, 0, 0))])


def _silu_rows(v, name):
    def body(v_ref, o_ref):
        z = v_ref[...]
        o_ref[...] = z * _sigmoid(z)

    (o,) = _rows(name, body, 1, [_row_in(v, v.shape[0])], [(v.shape, F32, v.shape, lambda i: (0, 0))])
    return o


def _silu_grad_rows(v, dv, name):
    def body(v_ref, d_ref, o_ref):
        z = v_ref[...]
        sg = _sigmoid(z)
        o_ref[...] = d_ref[...] * (sg * (1.0 + z * (1.0 - sg)))

    (o,) = _rows(name, body, 1, [_row_in(v, v.shape[0]), _row_in(dv, v.shape[0])],
                 [(v.shape, F32, v.shape, lambda i: (0, 0))])
    return o


def _sum_leading(g8, name):
    n, r, c = g8.shape
    tile = _tile(r, 256, SUBLANE)

    def body(g_ref, o_ref):
        acc = g_ref[0]
        for j in range(1, n):
            acc = acc + g_ref[j]
        o_ref[...] = acc

    (o,) = _rows(name, body, r // tile, [(g8, (n, tile, c), lambda i: (0, i, 0))],
                 [((r, c), F32, (tile, c), lambda i: (i, 0))])
    return o


def _pair_sum(g, recv, axis, name):
    n, br, bc = recv.shape
    tile = _tile(br, 256, 16)
    nrt = br // tile
    core = lax.axis_index("c").astype(jnp.int32).reshape(1)

    def body(c_ref, g_ref, r_ref, o_ref):
        o_ref[0] = (g_ref[...].astype(F32) + r_ref[0].astype(F32)).astype(o_ref.dtype)

    if axis == 1:
        g_spec = pl.BlockSpec((tile, bc), lambda q, i, c_ref: (i, 2 * q + c_ref[0]))
    else:
        g_spec = pl.BlockSpec((tile, bc), lambda q, i, c_ref: ((2 * q + c_ref[0]) * nrt + i, 0))
    slot = pl.BlockSpec((1, tile, bc), lambda q, i, c_ref: (q, i, 0))
    return pl.pallas_call(
        body, name=name, out_shape=_sds((n, br, bc), recv.dtype),
        grid_spec=pltpu.PrefetchScalarGridSpec(num_scalar_prefetch=1, grid=(n, nrt), in_specs=[g_spec, slot],
                                               out_specs=slot),
        compiler_params=_params(("arbitrary", "arbitrary")))(core, g, recv)


def _adam_math(w, m, v, g):
    c1 = 1.0 / (1.0 - ADAM_B1 ** ADAM_STEP)
    c2 = 1.0 / (1.0 - ADAM_B2 ** ADAM_STEP)
    mm = ADAM_B1 * m + (1.0 - ADAM_B1) * g
    vv = ADAM_B2 * v + (1.0 - ADAM_B2) * (g * g)
    return -ADAM_LR * ((mm * c1) / (jnp.sqrt(vv * c2) + ADAM_EPS) + ADAM_WD * w), mm, vv


def _adamw(w, m, v, gparts, name):
    r, c = w.shape
    n = gparts.shape[0]
    tile = _tile(r, 256, 16)

    def body(w_ref, m_ref, v_ref, g_ref, go_ref, d_ref, mo_ref, vo_ref):
        g = g_ref[0].astype(F32)
        for j in range(1, n):
            g = g + g_ref[j].astype(F32)
        go_ref[...] = g
        d_ref[...], mo_ref[...], vo_ref[...] = _adam_math(w_ref[...], m_ref[...], v_ref[...], g)

    rs = lambda arr: _row_in(arr, tile)
    out = ((r, c), F32, (tile, c), lambda i: (i, 0))
    return _rows(name, body, r // tile, [rs(w), rs(m), rs(v), (gparts, (n, tile, c), lambda i: (0, i, 0))],
                 [out, out, out, out])


def _adamw_scattered(w, m, v, p, recv, name):
    r, c = w.shape
    n = recv.shape[0]
    tile = _tile(r, 256, 16)
    chip = (2 * lax.axis_index("x") + lax.axis_index("y")).astype(jnp.int32).reshape(1)

    def body(q_ref, w_ref, m_ref, v_ref, p_ref, g_ref, go_ref, d_ref, mo_ref, vo_ref):
        g = p_ref[0].astype(F32)
        for j in range(n):
            g = g + g_ref[j].astype(F32)
        go_ref[...] = g
        d_ref[...], mo_ref[...], vo_ref[...] = _adam_math(w_ref[...], m_ref[...], v_ref[...], g)

    row = pl.BlockSpec((tile, c), lambda i, q_ref: (i, 0))
    out = _sds((r, c), F32)
    return pl.pallas_call(
        body, name=name, out_shape=[out, out, out, out],
        grid_spec=pltpu.PrefetchScalarGridSpec(
            num_scalar_prefetch=1, grid=(r // tile,),
            in_specs=[row, row, row, pl.BlockSpec((1, tile, c), lambda i, q_ref: (q_ref[0], i, 0)),
                      pl.BlockSpec((n, tile, c), lambda i, q_ref: (0, i, 0))],
            out_specs=[row, row, row, row]),
        compiler_params=_params(("arbitrary",)))(chip, w, m, v, p, recv)


def _cmul(ar, ai, br, bi):
    return ar * br - ai * bi, ar * bi + ai * br


def _cpow(ar, ai, n):
    pr, pi = jnp.ones_like(ar), jnp.zeros_like(ar)
    br, bi = ar, ai
    while n:
        if n & 1:
            pr, pi = _cmul(pr, pi, br, bi)
        n >>= 1
        if n:
            br, bi = _cmul(br, bi, br, bi)
    return pr, pi


def _s5_scan_into(xr_ref, xi_ref, ar1, ai1, ns, fr_ref, fi_ref, hr_ref, hi_ref, reverse):
    st = ar1.shape[1]
    ar = jnp.broadcast_to(ar1, (SUBLANE, st))
    ai = jnp.broadcast_to(ai1, (SUBLANE, st))
    zero = jnp.zeros((SUBLANE, st), F32)
    zero1 = jnp.zeros((1, st), F32)

    def slab(k):
        return pl.ds(pl.multiple_of(k * SUBLANE, SUBLANE), SUBLANE)

    def pass1(j, carry):
        hr, hi = carry
        k = ns - 1 - j if reverse else j
        nr, ni = _cmul(ar, ai, hr, hi)
        return nr + xr_ref[slab(k), :], ni + xi_ref[slab(k), :]

    fr, fi = lax.fori_loop(0, ns, pass1, (zero, zero))
    fr_ref[...] = fr
    fi_ref[...] = fi
    pr, pi = _cpow(ar1, ai1, ns)
    order = list(range(N_DEV - 1, -1, -1)) if reverse else list(range(N_DEV))
    hr_ref[order[0]:order[0] + 1, :] = zero1
    hi_ref[order[0]:order[0] + 1, :] = zero1
    for a_, b_ in zip(order[:-1], order[1:]):
        cr, ci = _cmul(pr, pi, hr_ref[a_:a_ + 1, :], hi_ref[a_:a_ + 1, :])
        hr_ref[b_:b_ + 1, :] = cr + fr_ref[a_:a_ + 1, :]
        hi_ref[b_:b_ + 1, :] = ci + fi_ref[a_:a_ + 1, :]

    def pass2(j, carry):
        hr, hi = carry
        k = ns - 1 - j if reverse else j
        nr, ni = _cmul(ar, ai, hr, hi)
        nr = nr + xr_ref[slab(k), :]
        ni = ni + xi_ref[slab(k), :]
        xr_ref[slab(k), :] = nr
        xi_ref[slab(k), :] = ni
        return nr, ni

    lax.fori_loop(0, ns, pass2, (hr_ref[...], hi_ref[...]))


def _s5_specs(r, ch, st):
    u_spec = pl.BlockSpec((r, ch), lambda j: (0, j // 2))
    w_spec = pl.BlockSpec((1, ch, st), lambda j: (j, 0, 0))
    c_spec = pl.BlockSpec((1, st, ch), lambda j: (j, 0, 0))
    a_spec = pl.BlockSpec((1, 2, st), lambda j: (j, 0, 0))
    return u_spec, w_spec, c_spec, a_spec


def _s5_fwd(up, wre, wim, cre, cim, a, rev, name):
    r, s = up.shape
    nh, ch, st = wre.shape
    ns = r // N_DEV
    u_spec, w_spec, c_spec, a_spec = _s5_specs(r, ch, st)

    def body(u_ref, wre_ref, wim_ref, cre_ref, cim_ref, a_ref, y_ref, xr, xi, fr, fi, hr, hi):
        j = pl.program_id(0)
        for rb in range(N_DEV):
            rows = slice(rb * ns, (rb + 1) * ns)
            ub = u_ref[rows, :].astype(MXU_DTYPE)
            xr[rows, :] = jnp.dot(ub, wre_ref[0].astype(MXU_DTYPE), preferred_element_type=F32)
            xi[rows, :] = jnp.dot(ub, wim_ref[0].astype(MXU_DTYPE), preferred_element_type=F32)
        _s5_scan_into(xr, xi, a_ref[0, 0:1, :], a_ref[0, 1:2, :], ns, fr, fi, hr, hi, rev)
        for rb in range(N_DEV):
            rows = slice(rb * ns, (rb + 1) * ns)
            yb = (jnp.dot(xr[rows, :].astype(MXU_DTYPE), cre_ref[0].astype(MXU_DTYPE), preferred_element_type=F32)
                  - jnp.dot(xi[rows, :].astype(MXU_DTYPE), cim_ref[0].astype(MXU_DTYPE), preferred_element_type=F32))

            @pl.when(j % 2 == 0)
            def _():
                y_ref[rows, :] = yb

            @pl.when(j % 2 == 1)
            def _():
                y_ref[rows, :] += yb

    small = pltpu.VMEM((SUBLANE, st), F32)
    return pl.pallas_call(
        body, name=name, grid=(nh,), in_specs=[u_spec, w_spec, w_spec, c_spec, c_spec, a_spec],
        out_specs=pl.BlockSpec((r, ch), lambda j: (0, j // 2)), out_shape=_sds((r, s), F32),
        scratch_shapes=[pltpu.VMEM((r, st), F32), pltpu.VMEM((r, st), F32), small, small, small, small],
        compiler_params=_params(("arbitrary",)))(up, wre, wim, cre, cim, a)


def _s5_bwd(up, dyp, wre, wim, cre, cim, a, rev, name):
    r, s = up.shape
    nh, ch, st = wre.shape
    ns = r // N_DEV
    u_spec, w_spec, c_spec, a_spec = _s5_specs(r, ch, st)
    nt = (((1,), (1,)), ((), ()))
    tn = (((0,), (0,)), ((), ()))

    def body(u_ref, dy_ref, wre_ref, wim_ref, cre_ref, cim_ref, a_ref,
             du_ref, dwre_ref, dwim_ref, dcre_ref, dcim_ref, da_ref,
             hr, hi, gr, gi, fr, fi, sr, si, er, ei):
        j = pl.program_id(0)
        wre_b = wre_ref[0].astype(MXU_DTYPE)
        wim_b = wim_ref[0].astype(MXU_DTYPE)
        cre_b = cre_ref[0].astype(MXU_DTYPE)
        cim_b = cim_ref[0].astype(MXU_DTYPE)
        for rb in range(N_DEV):
            rows = slice(rb * ns, (rb + 1) * ns)
            ub = u_ref[rows, :].astype(MXU_DTYPE)
            hr[rows, :] = jnp.dot(ub, wre_b, preferred_element_type=F32)
            hi[rows, :] = jnp.dot(ub, wim_b, preferred_element_type=F32)
        ar1, ai1 = a_ref[0, 0:1, :], a_ref[0, 1:2, :]
        _s5_scan_into(hr, hi, ar1, ai1, ns, fr, fi, sr, si, rev)
        dcre = jnp.zeros((st, ch), F32)
        dcim = jnp.zeros((st, ch), F32)
        for rb in range(N_DEV):
            rows = slice(rb * ns, (rb + 1) * ns)
            dyb = dy_ref[rows, :].astype(MXU_DTYPE)
            gr[rows, :] = lax.dot_general(dyb, cre_b, nt, preferred_element_type=F32)
            gi[rows, :] = -lax.dot_general(dyb, cim_b, nt, preferred_element_type=F32)
            dcre += lax.dot_general(hr[rows, :].astype(MXU_DTYPE), dyb, tn, preferred_element_type=F32)
            dcim -= lax.dot_general(hi[rows, :].astype(MXU_DTYPE), dyb, tn, preferred_element_type=F32)
        dcre_ref[0] = dcre
        dcim_ref[0] = dcim
        _s5_scan_into(gr, gi, ar1, -ai1, ns, fr, fi, er, ei, not rev)

        def slab(k):
            return pl.ds(pl.multiple_of(k * SUBLANE, SUBLANE), SUBLANE)

        step_back = 1 if rev else -1

        def acc_step(k, carry):
            acr, aci = carry
            g_r, g_i = gr[slab(k), :], gi[slab(k), :]
            p_r, p_i = hr[slab(k + step_back), :], hi[slab(k + step_back), :]
            return acr + g_r * p_r + g_i * p_i, aci + g_i * p_r - g_r * p_i

        edge = (ns - 1) * SUBLANE if rev else 0
        g_r, g_i = gr[edge:edge + SUBLANE, :], gi[edge:edge + SUBLANE, :]
        p_r, p_i = sr[...], si[...]
        lo, hi_k = (0, ns - 1) if rev else (1, ns)
        acr, aci = lax.fori_loop(lo, hi_k, acc_step, (g_r * p_r + g_i * p_i, g_i * p_r - g_r * p_i))
        da_ref[0, 0:1, :] = jnp.sum(acr, axis=0, keepdims=True)
        da_ref[0, 1:2, :] = jnp.sum(aci, axis=0, keepdims=True)
        dwre = jnp.zeros((ch, st), F32)
        dwim = jnp.zeros((ch, st), F32)
        for rb in range(N_DEV):
            rows = slice(rb * ns, (rb + 1) * ns)
            grb = gr[rows, :].astype(MXU_DTYPE)
            gib = gi[rows, :].astype(MXU_DTYPE)
            ub = u_ref[rows, :].astype(MXU_DTYPE)
            dub = (lax.dot_general(grb, wre_b, nt, preferred_element_type=F32)
                   + lax.dot_general(gib, wim_b, nt, preferred_element_type=F32))
            dwre += lax.dot_general(ub, grb, tn, preferred_element_type=F32)
            dwim += lax.dot_general(ub, gib, tn, preferred_element_type=F32)

            @pl.when(j % 2 == 0)
            def _():
                du_ref[rows, :] = dub

            @pl.when(j % 2 == 1)
            def _():
                du_ref[rows, :] += dub

        dwre_ref[0] = dwre
        dwim_ref[0] = dwim

    small = pltpu.VMEM((SUBLANE, st), F32)
    big = pltpu.VMEM((r, st), F32)
    return pl.pallas_call(
        body, name=name, grid=(nh,), in_specs=[u_spec, u_spec, w_spec, w_spec, c_spec, c_spec, a_spec],
        out_specs=[pl.BlockSpec((r, ch), lambda j: (0, j // 2)), w_spec, w_spec, c_spec, c_spec, a_spec],
        out_shape=[_sds((r, s), F32), _sds(wre.shape, F32), _sds(wre.shape, F32), _sds(cre.shape, F32),
                   _sds(cre.shape, F32), _sds(a.shape, F32)],
        scratch_shapes=[big, big, big, big, small, small, small, small, small, small],
        compiler_params=_params(("arbitrary",)))(up, dyp, wre, wim, cre, cim, a)


def _rope(t, cos, sin):
    quarter = t.shape[1] // 4
    lane = lax.broadcasted_iota(jnp.int32, t.shape, 1)
    first = (lane // quarter) % 2 == 0
    partner = jnp.where(first, pltpu.roll(t, t.shape[1] - quarter, 1), pltpu.roll(t, quarter, 1))
    return t * cos + partner * sin


def _rope_t(d, cos, sin):
    quarter = d.shape[1] // 4
    ds_ = d * sin
    lane = lax.broadcasted_iota(jnp.int32, d.shape, 1)
    first = (lane // quarter) % 2 == 0
    partner = jnp.where(first, pltpu.roll(ds_, d.shape[1] - quarter, 1), pltpu.roll(ds_, quarter, 1))
    return d * cos + partner


def _chunk_of_step(s, nch, ncc, rev):
    if not rev:
        return s
    return jnp.where(s < ncc, ncc - 1 - s, nch + ncc - 1 - s)


def _ret_fwd(hm, cos, sin, decay, wend, win, gch, heads, dk, dv, q_off, ncc, rev, name):
    r = hm.shape[0]
    ch = RET_CHUNK
    nch = r // ch
    t_rows = r - ncc * ch
    qb, kb, vb = q_off // dk, (q_off + heads * dk) // dk, (q_off + 2 * heads * dk) // dv
    q_scale = dk ** -0.5
    nt = (((1,), (1,)), ((), ()))
    tn = (((0,), (0,)), ((), ()))
    cof = lambda s: _chunk_of_step(s, nch, ncc, rev)

    def body(q_ref, k_ref, v_ref, cos_ref, sin_ref, dec_ref, we_ref, wi_ref, g_ref, o_ref, sin_out, st):
        s = pl.program_id(1)

        @pl.when(s == 0)
        def _():
            st[...] = jnp.zeros_like(st)

        q = _rope(q_ref[...], cos_ref[...], sin_ref[...]) * q_scale
        k = _rope(k_ref[...], cos_ref[...], sin_ref[...])
        v = v_ref[...].astype(MXU_DTYPE)
        s_cur = st[...]
        sin_out[0, 0] = s_cur
        kw = (k * we_ref[0]).astype(MXU_DTYPE)
        qw = (q * wi_ref[0]).astype(MXU_DTYPE)
        scores = lax.dot_general(q.astype(MXU_DTYPE), k.astype(MXU_DTYPE), nt, preferred_element_type=F32) * dec_ref[0]
        o_ref[...] = (jnp.dot(scores.astype(MXU_DTYPE), v, preferred_element_type=F32)
                      + jnp.dot(qw, s_cur.astype(MXU_DTYPE), preferred_element_type=F32))
        st[...] = g_ref[0] * s_cur + lax.dot_general(kw, v, tn, preferred_element_type=F32)

    tab = lambda w: pl.BlockSpec((1, ch, w), lambda h, s: (h, 0, 0))
    return pl.pallas_call(
        body, name=name, grid=(heads, nch),
        in_specs=[pl.BlockSpec((ch, dk), lambda h, s: (cof(s), qb + h)),
                  pl.BlockSpec((ch, dk), lambda h, s: (cof(s), kb + h)),
                  pl.BlockSpec((ch, dv), lambda h, s: (cof(s), vb + h)),
                  pl.BlockSpec((ch, dk), lambda h, s: (cof(s), 0)),
                  pl.BlockSpec((ch, dk), lambda h, s: (cof(s), 0)),
                  tab(ch), tab(dk), tab(dk), tab(dv)],
        out_specs=[pl.BlockSpec((ch, dv), lambda h, s: (jnp.maximum(cof(s) - ncc, 0) if not rev
                                                         else jnp.where(s < ncc, nch - ncc - 1, cof(s) - ncc), h)),
                   pl.BlockSpec((1, 1, dk, dv), lambda h, s: (h, s, 0, 0))],
        out_shape=[_sds((t_rows, heads * dv), F32), _sds((heads, nch, dk, dv), F32)],
        scratch_shapes=[pltpu.VMEM((dk, dv), F32)],
        compiler_params=_params(("parallel", "arbitrary")))(hm, hm, hm, cos, sin, decay, wend, win, gch)


def _ret_bwd(hm, cos, sin, decay, wend, win, gch, s_in, do, heads, dk, dv, q_off, ncc, rev, name):
    r = hm.shape[0]
    ch = RET_CHUNK
    nch = r // ch
    qb, kb, vb = q_off // dk, (q_off + heads * dk) // dk, (q_off + 2 * heads * dk) // dv
    q_scale = dk ** -0.5
    nt = (((1,), (1,)), ((), ()))
    tn = (((0,), (0,)), ((), ()))
    cof = lambda rr: _chunk_of_step(nch - 1 - rr, nch, ncc, rev)

    def body(q_ref, k_ref, v_ref, cos_ref, sin_ref, dec_ref, we_ref, wi_ref, g_ref, sin_ref2, do_ref,
             dq_ref, dk_ref, dv_ref, ddec_ref, dwe_ref, dwi_ref, dg_ref, dst):
        rr = pl.program_id(1)
        n = cof(rr)

        @pl.when(rr == 0)
        def _():
            dst[...] = jnp.zeros_like(dst)
            ddec_ref[...] = jnp.zeros_like(ddec_ref)
            dwe_ref[...] = jnp.zeros_like(dwe_ref)
            dwi_ref[...] = jnp.zeros_like(dwi_ref)
            dg_ref[...] = jnp.zeros_like(dg_ref)

        cos_, sin_ = cos_ref[...], sin_ref[...]
        q = _rope(q_ref[...], cos_, sin_) * q_scale
        k = _rope(k_ref[...], cos_, sin_)
        v = v_ref[...].astype(MXU_DTYPE)
        qb_, kb_ = q.astype(MXU_DTYPE), k.astype(MXU_DTYPE)
        kw = (k * we_ref[0]).astype(MXU_DTYPE)
        qw = (q * wi_ref[0]).astype(MXU_DTYPE)
        sraw = lax.dot_general(qb_, kb_, nt, preferred_element_type=F32)
        scores = (sraw * dec_ref[0]).astype(MXU_DTYPE)
        d_o = jnp.where(n >= ncc, do_ref[...], 0.0).astype(MXU_DTYPE)
        s_n = sin_ref2[0, 0]
        s_nb = s_n.astype(MXU_DTYPE)
        ds1 = dst[...]
        ds1b = ds1.astype(MXU_DTYPE)
        dsc = lax.dot_general(d_o, v, nt, preferred_element_type=F32)
        dsr = (dsc * dec_ref[0]).astype(MXU_DTYPE)
        ddec_ref[0] += dsc * sraw
        t1 = lax.dot_general(d_o, s_nb, nt, preferred_element_type=F32)
        dq_r = jnp.dot(dsr, kb_, preferred_element_type=F32) + t1 * wi_ref[0]
        dwi_ref[0] += t1 * q
        t2 = lax.dot_general(v, ds1b, nt, preferred_element_type=F32)
        dk_r = lax.dot_general(dsr, qb_, tn, preferred_element_type=F32) + t2 * we_ref[0]
        dwe_ref[0] += t2 * k
        dv_ref[...] = (lax.dot_general(scores, d_o, tn, preferred_element_type=F32)
                       + jnp.dot(kw, ds1b, preferred_element_type=F32))
        dg_ref[0] += ds1 * s_n
        dst[...] = g_ref[0] * ds1 + lax.dot_general(qw, d_o, tn, preferred_element_type=F32)
        dq_ref[...] = _rope_t(dq_r, cos_, sin_) * q_scale
        dk_ref[...] = _rope_t(dk_r, cos_, sin_)

    tab = lambda w: pl.BlockSpec((1, ch, w), lambda h, rr: (h, 0, 0))
    return pl.pallas_call(
        body, name=name, grid=(heads, nch),
        in_specs=[pl.BlockSpec((ch, dk), lambda h, rr: (cof(rr), qb + h)),
                  pl.BlockSpec((ch, dk), lambda h, rr: (cof(rr), kb + h)),
                  pl.BlockSpec((ch, dv), lambda h, rr: (cof(rr), vb + h)),
                  pl.BlockSpec((ch, dk), lambda h, rr: (cof(rr), 0)),
                  pl.BlockSpec((ch, dk), lambda h, rr: (cof(rr), 0)),
                  tab(ch), tab(dk), tab(dk), tab(dv),
                  pl.BlockSpec((1, 1, dk, dv), lambda h, rr: (h, nch - 1 - rr, 0, 0)),
                  pl.BlockSpec((ch, dv), lambda h, rr: (jnp.maximum(cof(rr) - ncc, 0), h))],
        out_specs=[pl.BlockSpec((ch, dk), lambda h, rr: (cof(rr), h)),
                   pl.BlockSpec((ch, dk), lambda h, rr: (cof(rr), h)),
                   pl.BlockSpec((ch, dv), lambda h, rr: (cof(rr), h)),
                   tab(ch), tab(dk), tab(dk), tab(dv)],
        out_shape=[_sds((r, heads * dk), F32), _sds((r, heads * dk), F32), _sds((r, heads * dv), F32),
                   _sds(decay.shape, F32), _sds(wend.shape, F32), _sds(win.shape, F32), _sds(gch.shape, F32)],
        scratch_shapes=[pltpu.VMEM((dk, dv), F32)],
        compiler_params=_params(("parallel", "arbitrary")))(hm, hm, hm, cos, sin, decay, wend, win, gch, s_in, do)


_HBM = pl.BlockSpec(memory_space=pltpu.HBM)
_MESH = pl.DeviceIdType.MESH
ALL_GATHER_COLLECTIVE_ID = 1
SIBLING_COLLECTIVE_ID = 2
CHIPS_COLLECTIVE_ID = 3


def _axis_slice(ref, axis, start, size):
    idx = [slice(None)] * len(ref.shape)
    idx[axis] = pl.ds(start, size)
    return ref.at[tuple(idx)]


def _sibling_and_chip_peers():
    x, y, c = lax.axis_index("x"), lax.axis_index("y"), lax.axis_index("c")
    return [(x, y, 1 - c), (1 - x, y, c), (x, 1 - y, c), (1 - x, 1 - y, c)]


def _launch_exchange(body, name, operand, out_shape, sems, peers_fn, collective_id, on_sequencer):
    if not on_sequencer:
        return pl.pallas_call(body, name=name, out_shape=out_shape, in_specs=[_HBM], out_specs=_HBM,
                              scratch_shapes=sems)(operand)

    def sequencer_body(in_ref, out_ref, *sem_refs):
        peers = peers_fn()
        barrier = pltpu.get_barrier_semaphore()
        for peer in peers:
            pl.semaphore_signal(barrier, inc=1, device_id=peer, device_id_type=_MESH)
        pl.semaphore_wait(barrier, len(peers))
        body(in_ref, out_ref, *sem_refs)

    return pl.kernel(sequencer_body, out_type=out_shape, name=name,
                     mesh=plsc.ScalarSubcoreMesh(axis_name="sequencer", num_cores=1), scratch_types=sems,
                     compiler_params=pltpu.CompilerParams(collective_id=collective_id))(operand)


def _all_gather(shard, axis, name, on_sequencer=False):
    m = shard.shape[axis]
    out_shape = list(shard.shape)
    out_shape[axis] = N_DEV * m

    def body(x_ref, out_ref, send_sems, recv_sems, local_sem):
        x, y, c = lax.axis_index("x"), lax.axis_index("y"), lax.axis_index("c")
        me, sibling = (x, y, c), (x, y, 1 - c)
        chips = [(1 - x, y), (x, 1 - y), (1 - x, 1 - y)]

        def block(px, py, pc):
            return _axis_slice(out_ref, axis, (4 * px + 2 * py + pc) * m, m)

        def copy(k, blk, to, src=None):
            return pltpu.make_async_remote_copy(
                src_ref=block(*blk) if src is None else src, dst_ref=block(*blk), send_sem=send_sems.at[k],
                recv_sem=recv_sems.at[k], device_id=to, device_id_type=_MESH)

        mine = pltpu.make_async_copy(x_ref, block(*me), local_sem)
        mine.start()
        first = [copy(0, me, sibling, src=x_ref)]
        first += [copy(1 + j, me, (*chip, c), src=x_ref) for j, chip in enumerate(chips)]
        for cp in first:
            cp.start()
        passed = [copy(4 + j, (*chip, c), sibling) for j, chip in enumerate(chips)]
        for j, chip in enumerate(chips):
            copy(1 + j, (*chip, c), me).wait_recv()
            passed[j].start()
        copy(0, sibling, me).wait_recv()
        for j, chip in enumerate(chips):
            copy(4 + j, (*chip, 1 - c), me).wait_recv()
        for cp in first + passed:
            cp.wait_send()
        mine.wait()

    return _launch_exchange(
        body, name, shard, _sds(out_shape, shard.dtype),
        [pltpu.SemaphoreType.DMA((7,)), pltpu.SemaphoreType.DMA((7,)), pltpu.SemaphoreType.DMA(())],
        _sibling_and_chip_peers, ALL_GATHER_COLLECTIVE_ID, on_sequencer)


def _rs_sibling(g, axis, name, on_sequencer=False):
    m = g.shape[axis] // N_DEV
    blk_shape = list(g.shape)
    blk_shape[axis] = m
    n_chips = N_DEV // 2

    def body(g_ref, recv_ref, send_sems, recv_sems):
        x, y, c = lax.axis_index("x"), lax.axis_index("y"), lax.axis_index("c")
        sibling = (x, y, 1 - c)
        send = [pltpu.make_async_remote_copy(
            src_ref=_axis_slice(g_ref, axis, (2 * q + 1 - c) * m, m), dst_ref=recv_ref.at[q],
            send_sem=send_sems.at[q], recv_sem=recv_sems.at[q], device_id=sibling, device_id_type=_MESH)
            for q in range(n_chips)]
        for cp in send:
            cp.start()
        for cp in send:
            cp.wait_recv()
        for cp in send:
            cp.wait_send()

    return _launch_exchange(
        body, name, g, _sds([n_chips] + blk_shape, g.dtype),
        [pltpu.SemaphoreType.DMA((n_chips,)), pltpu.SemaphoreType.DMA((n_chips,))],
        lambda: _sibling_and_chip_peers()[:1], SIBLING_COLLECTIVE_ID, on_sequencer)


def _rs_chips(p, name, on_sequencer=False):
    n_peers = p.shape[0] - 1

    def body(p_ref, out_ref, send_sems, recv_sems):
        x, y, c = lax.axis_index("x"), lax.axis_index("y"), lax.axis_index("c")
        chips = [(1 - x, y), (x, 1 - y), (1 - x, 1 - y)]
        send = [pltpu.make_async_remote_copy(
            src_ref=p_ref.at[2 * cx + cy], dst_ref=out_ref.at[j], send_sem=send_sems.at[j],
            recv_sem=recv_sems.at[j], device_id=(cx, cy, c), device_id_type=_MESH)
            for j, (cx, cy) in enumerate(chips)]
        for cp in send:
            cp.start()
        for cp in send:
            cp.wait_recv()
        for cp in send:
            cp.wait_send()

    return _launch_exchange(
        body, name, p, _sds((n_peers,) + p.shape[1:], p.dtype),
        [pltpu.SemaphoreType.DMA((n_peers,)), pltpu.SemaphoreType.DMA((n_peers,))],
        lambda: _sibling_and_chip_peers()[1:], CHIPS_COLLECTIVE_ID, on_sequencer)


def _reduce_scatter(g, axis, name):
    sib = _rs_sibling(g, axis, name + "_d2d", on_sequencer=True)
    p = _pair_sum(g, sib, axis, name + "_pair")
    return p, _rs_chips(p, name + "_ici", on_sequencer=True)


def _s5_tables(lam_re, lam_im, log_step, b_re, b_im, c_re, c_im):
    g, p, cg = b_re.shape
    step = jnp.exp(log_step)[:, None]
    mag = jnp.exp(lam_re * step)
    a_re, a_im = mag * jnp.cos(lam_im * step), mag * jnp.sin(lam_im * step)
    den = lam_re * lam_re + lam_im * lam_im
    num_re, num_im = a_re - 1.0, a_im
    k_re = (num_re * lam_re + num_im * lam_im) / den
    k_im = (num_im * lam_re - num_re * lam_im) / den
    bb_re = k_re[..., None] * b_re - k_im[..., None] * b_im
    bb_im = k_re[..., None] * b_im + k_im[..., None] * b_re
    gt = g // SSM_TILE_GROUPS
    hg = SSM_HALF_GROUPS
    eye = jnp.eye(SSM_TILE_GROUPS, dtype=F32).reshape(SSM_TILE_GROUPS, 2, hg)

    def pack_b(bb):
        w = jnp.einsum("jhqpc,ghq->jhgcqp", bb.reshape(gt, 2, hg, p, cg), eye)
        return w.reshape(gt * 2, SSM_TILE_GROUPS * cg, hg * p)

    def pack_c(cc):
        w = jnp.einsum("jhqcp,ghq->jhqpgc", cc.reshape(gt, 2, hg, cg, p), eye)
        return w.reshape(gt * 2, hg * p, SSM_TILE_GROUPS * cg)

    a = jnp.stack([a_re.reshape(gt * 2, hg * p), a_im.reshape(gt * 2, hg * p)], axis=1)
    return pack_b(bb_re), pack_b(bb_im), pack_c(c_re), pack_c(c_im), a


def _ret_tables(decay_logit, rev, dk, dv):
    ch = RET_CHUNK
    h = decay_logit.shape[0]
    lg = jax.nn.log_sigmoid(decay_logit)[:, None]
    pos = jnp.arange(ch, dtype=F32)
    diff = pos[:, None] - pos[None, :]
    if rev:
        diff = -diff
        mask = diff > 0
        w_end = jnp.exp(lg * pos)
        w_in = jnp.exp(lg * (ch - pos))
    else:
        mask = diff >= 0
        w_end = jnp.exp(lg * (ch - 1.0 - pos))
        w_in = jnp.exp(lg * (pos + 1.0))
    decay = jnp.where(mask, jnp.exp(lg[:, :, None] * jnp.where(mask, diff, 0.0)), 0.0)
    g_chunk = jnp.exp(lg[:, 0] * ch)
    return (decay, jnp.broadcast_to(w_end[:, :, None], (h, ch, dk)), jnp.broadcast_to(w_in[:, :, None], (h, ch, dk)),
            jnp.broadcast_to(g_chunk[:, None, None], (h, dk, dv)))


def _rope_tables(t_rows, ncc, dk):
    quarter = dk // 4
    idx = np.arange(t_rows)
    row, col = idx // GRID_W, idx % GRID_W
    inv = ROPE_BASE ** (-np.arange(quarter, dtype=np.float32) / quarter)
    ang_r = row.astype(np.float32)[:, None] * inv
    ang_c = col.astype(np.float32)[:, None] * inv
    ang_r, ang_c = jnp.asarray(ang_r, F32), jnp.asarray(ang_c, F32)
    cos = jnp.concatenate([jnp.cos(ang_r), jnp.cos(ang_r), jnp.cos(ang_c), jnp.cos(ang_c)], axis=1)
    sin = jnp.concatenate([-jnp.sin(ang_r), jnp.sin(ang_r), -jnp.sin(ang_c), jnp.sin(ang_c)], axis=1)
    n_ctx = ncc * RET_CHUNK
    cos = jnp.concatenate([jnp.ones((n_ctx, dk), F32), cos], axis=0)
    sin = jnp.concatenate([jnp.zeros((n_ctx, dk), F32), sin], axis=0)
    return cos, sin


def _to_scan_layout(ctx_rows, lat_rows, rev):
    u = jnp.concatenate([lat_rows, ctx_rows] if rev else [ctx_rows, lat_rows], axis=0)
    r, w = u.shape
    return u.reshape(N_DEV, r // N_DEV, w).transpose(1, 0, 2).reshape(r, w)


def _from_scan_layout(yp, n_ctx, rev):
    r, w = yp.shape
    y = yp.reshape(r // N_DEV, N_DEV, w).transpose(1, 0, 2).reshape(r, w)
    return (y[r - n_ctx:], y[:r - n_ctx]) if rev else (y[:n_ctx], y[n_ctx:])


def _pack(parts, width):
    flat = jnp.concatenate([p.reshape(-1).astype(F32) for p in parts])
    n = flat.shape[0]
    quantum = width * SUBLANE
    padded = -(-n // quantum) * quantum
    return jnp.pad(flat, (0, padded - n)).reshape(padded // width, width)


def _unpack(flat2d, shapes):
    flat = flat2d.reshape(-1)
    out, off = [], 0
    for shp in shapes:
        n = int(np.prod(shp))
        out.append(flat[off:off + n].reshape(shp))
        off += n
    return out


def kernel(x, c, ctx, c_ctx, ada_w, ada_b, norm_g, ffn_w_in, ffn_w_out, mix_w_in, ssm_lam_re, ssm_lam_im, ssm_log_step, ssm_b_re, ssm_b_im, ssm_c_re, ssm_c_im, ssm_d, ssm_glu_w, ret_decay_logit, ret_w_proj, mix_w_out, loss_target, m_c_ctx, m_ada_w, m_ada_b, m_norm_g, m_ffn_w_in, m_ffn_w_out, m_mix_w_in, m_ssm_lam_re, m_ssm_lam_im, m_ssm_log_step, m_ssm_b_re, m_ssm_b_im, m_ssm_c_re, m_ssm_c_im, m_ssm_d, m_ssm_glu_w, m_ret_decay_logit, m_ret_w_proj, m_mix_w_out, v_c_ctx, v_ada_w, v_ada_b, v_norm_g, v_ffn_w_in, v_ffn_w_out, v_mix_w_in, v_ssm_lam_re, v_ssm_lam_im, v_ssm_log_step, v_ssm_b_re, v_ssm_b_im, v_ssm_c_re, v_ssm_c_im, v_ssm_d, v_ssm_glu_w, v_ret_decay_logit, v_ret_w_proj, v_mix_w_out):
    t_rows, d = x.shape[1], x.shape[2]
    n_ctx = ctx.shape[1]
    r = n_ctx + t_rows
    ssm_w = ssm_d.shape[1]
    heads = ret_decay_logit.shape[2]
    mi = mix_w_in.shape[2] * N_DEV
    dk = (mi - ssm_w - 2 * d) // (6 * heads)
    dv = 2 * dk
    qk_w, v_w = heads * dk, heads * dv
    q_off = ssm_w
    ncc = n_ctx // RET_CHUNK
    tile = n_ctx
    nct = 1
    wide_tile = _tile(n_ctx, 128, 16)
    assert r % (N_DEV * SUBLANE) == 0 and n_ctx % RET_CHUNK == 0 and t_rows % tile == 0
    me = 4 * lax.axis_index("x") + 2 * lax.axis_index("y") + lax.axis_index("c")
    g_off = ssm_w + 2 * qk_w + v_w
    gs_off = g_off + v_w

    bf = lambda w: w.astype(BF16)
    w_in1 = _all_gather(bf(ffn_w_in[0, 0]), 1, "ag_ffn1_in", on_sequencer=True)
    w_out1 = _all_gather(bf(ffn_w_out[0, 0]), 0, "ag_ffn1_out", on_sequencer=True)
    w_mix = _all_gather(bf(mix_w_in[0]), 1, "ag_mix_in", on_sequencer=True)
    w_glu = _all_gather(bf(ssm_glu_w[0]), 1, "ag_glu", on_sequencer=True)
    w_rp = _all_gather(bf(ret_w_proj[0]), 0, "ag_ret_proj", on_sequencer=True)
    w_mo = _all_gather(bf(mix_w_out[0]), 0, "ag_mix_out", on_sequencer=True)
    w_in2 = _all_gather(bf(ffn_w_in[0, 1]), 1, "ag_ffn2_in", on_sequencer=True)
    w_out2 = _all_gather(bf(ffn_w_out[0, 1]), 0, "ag_ffn2_out", on_sequencer=True)

    ng_cols = norm_g.shape[2]
    small0 = _pack([c[0], norm_g[0]], d)
    small0_all = _all_gather(small0, 0, "ag_cond").reshape(N_DEV, -1)
    c_all = small0_all[:, :d]
    g_full = small0_all[:, d:d + 6 * ng_cols].reshape(N_DEV, 6, ng_cols).transpose(1, 0, 2).reshape(6, d)
    g6 = g_full.reshape(6, 1, d)
    cc = jnp.concatenate([c_all, c_ctx[None, :], jnp.zeros((2 * SUBLANE - N_DEV - 1, d), F32)], axis=0)
    sc = _silu_rows(cc, "ada_silu")
    na = ada_w.shape[2]
    a_loc = _mm(sc, ada_w[0], "nn", F32, "ada_fwd", tm=16, tn=na, tk=512)
    a_all = _all_gather(a_loc, 0, "ag_ada").reshape(N_DEV, 2 * SUBLANE, na)
    ada_x = lax.dynamic_index_in_dim(a_all, me, axis=1, keepdims=False).reshape(9 * d) + ada_b[0]
    ada_c = a_all[:, N_DEV, :].reshape(9 * d) + ada_b[0]
    mods = jnp.stack([ada_c.reshape(9, d), ada_x.reshape(9, d)]).reshape(18, 1, d)

    xin = jnp.concatenate([ctx[0], x[0]], axis=0)
    u1 = _ada_pre_fwd(xin, g6, mods, 0, 0, nct, tile, "pre1")
    h1 = _mm(u1, w_in1, "nn", F32, "ffn1_in", tm=544)
    a1 = _swiglu_fwd(h1, wide_tile, "swiglu1")
    o1 = _mm(a1, w_out1, "nn", F32, "ffn1_out", tm=544, tn=d, tk=1408)
    x1 = _ada_post_fwd(xin, o1, g6, mods, 1, 0, 0.5, nct, tile, "post1")
    u2 = _ada_pre_fwd(x1, g6, mods, 2, 1, nct, tile, "pre2")
    hm = _mm(u2, w_mix, "nn", F32, "mix_in", tm=544)

    us_ctx, us_lat = hm[:n_ctx, :ssm_w], hm[n_ctx:, :ssm_w]
    dskip = ssm_d.reshape(1, 1, ssm_w)
    s5_tabs, s5_vjps, ups, y_dirs = [], [], [], []
    for dr in range(2):
        prm = (ssm_lam_re[0, dr], ssm_lam_im[0, dr], ssm_log_step[0, dr], ssm_b_re[0, dr], ssm_b_im[0, dr],
               ssm_c_re[0, dr], ssm_c_im[0, dr])
        tabs, vjp_fn = jax.vjp(_s5_tables, *prm)
        up = _to_scan_layout(us_ctx, us_lat, dr == 1)
        yp = _s5_fwd(up, *tabs, dr == 1, "s5_fwd%d" % dr)
        s5_tabs.append(tabs)
        s5_vjps.append(vjp_fn)
        ups.append(up)
        y_dirs.append(_from_scan_layout(yp, n_ctx, dr == 1)[1])
    a_ssm = _ssm_out_fwd(y_dirs[0], y_dirs[1], hm, dskip, nct, tile, "ssm_out")
    gab = _mm(a_ssm, w_glu, "nn", F32, "glu", tm=512, tn=2048, tk=ssm_w)

    cos, sin = _rope_tables(t_rows, ncc, dk)
    ret_tabs, ret_vjps, o_dirs, s_ins = [], [], [], []
    for dr in range(2):
        tabs, vjp_fn = jax.vjp(functools.partial(_ret_tables, rev=dr == 1, dk=dk, dv=dv), ret_decay_logit[0, dr])
        o_d, s_in = _ret_fwd(hm, cos, sin, *tabs, heads, dk, dv, q_off, ncc, dr == 1, "ret_fwd%d" % dr)
        ret_tabs.append(tabs)
        ret_vjps.append(vjp_fn)
        o_dirs.append(o_d)
        s_ins.append(s_in)
    ret_in = _ret_gate_fwd(o_dirs[0], o_dirs[1], hm, g_off, heads, dv, nct, tile, "ret_gate")
    rb = _mm(ret_in, w_rp, "nn", F32, "ret_proj", tm=512, tn=d, tk=v_w)
    merged = _merge_fwd(gab, rb, hm, gs_off, nct, tile, "merge")
    mix = _mm(merged, w_mo, "nn", F32, "mix_out", tm=512, tn=d, tk=d)
    x1x = x1[n_ctx:]
    x2 = _ada_post_fwd(x1x, mix, g6, mods, 3, 1, 1.0, 0, tile, "post2")
    u3 = _ada_pre_fwd(x2, g6, mods, 4, 2, 0, tile, "pre3")
    h3 = _mm(u3, w_in2, "nn", F32, "ffn2_in", tm=512)
    a3 = _swiglu_fwd(h3, wide_tile, "swiglu2")
    o3 = _mm(a3, w_out2, "nn", F32, "ffn2_out", tm=512, tn=d, tk=1408)
    x3 = _ada_post_fwd(x2, o3, g6, mods, 5, 2, 0.5, 0, tile, "post3")
    dy, lcols = _loss_grad(x3, loss_target[0], tile, "loss")
    loss = lax.psum(0.5 * jnp.sum(lcols) / d, MESH_AXES)

    dg6 = [None] * 6
    dmod = {}

    def add_mod(sel_rows, k, val):
        for sel, row in sel_rows:
            dmod[(sel, k)] = dmod.get((sel, k), 0.0) + val[row, 0]

    both, lat = [(0, 0), (1, 1)], [(1, 0)]
    do3, dg6[5], dgt = _ada_post_bwd(dy, o3, g6, mods, 5, 2, 0.5, 0, 1, tile, "post3_bwd")
    add_mod(lat, 8, dgt)
    da3 = _mm(do3, w_out2, "nt", F32, "ffn2_out_dx", tm=512, tn=1408, tk=d)
    gw_out2 = _mm(a3, do3, "tn", BF16, "ffn2_out_dw", tm=1408, tn=d, tk=512)
    dh3 = _swiglu_bwd(h3, da3, wide_tile, "swiglu2_bwd")
    du3 = _mm(dh3, w_in2, "nt", F32, "ffn2_in_dx", tm=512, tn=d, tk=1408)
    gw_in2 = _mm(u3, dh3, "tn", BF16, "ffn2_in_dw", tm=d, tn=1408, tk=512)
    dx2, dg6[4], dsh, dsc = _ada_pre_bwd(x2, du3, dy, g6, mods, 4, 2, 0, 1, tile, "pre3_bwd")
    add_mod(lat, 6, dsh)
    add_mod(lat, 7, dsc)
    dmix, dg6[3], dgt = _ada_post_bwd(dx2, mix, g6, mods, 3, 1, 1.0, 0, 1, tile, "post2_bwd")
    add_mod(lat, 5, dgt)
    dmerged = _mm(dmix, w_mo, "nt", F32, "mix_out_dx", tm=512, tn=d, tk=d)
    gw_mo = _mm(merged, dmix, "tn", BF16, "mix_out_dw", tm=d, tn=d, tk=512)
    dgab, drb, dgs, dgr = _merge_bwd(gab, rb, hm, gs_off, dmerged, nct, tile, "merge_bwd")
    da_ssm = _mm(dgab, w_glu, "nt", F32, "glu_dx", tm=512, tn=ssm_w, tk=2 * d)
    gw_glu = _mm(a_ssm, dgab, "tn", BF16, "glu_dw", tm=ssm_w, tn=2 * d, tk=512)
    dret_in = _mm(drb, w_rp, "nt", F32, "ret_proj_dx", tm=512, tn=v_w, tk=d)
    gw_rp = _mm(ret_in, drb, "tn", BF16, "ret_proj_dw", tm=v_w, tn=d, tk=512)
    d_o, dg_gate = _ret_gate_bwd(o_dirs[0], o_dirs[1], hm, g_off, dret_in, heads, dv, nct, tile, "ret_gate_bwd")
    dy_ssm, dus_direct, d_dskip = _ssm_out_bwd(y_dirs[0], y_dirs[1], hm, dskip, da_ssm, nct, tile, "ssm_out_bwd")
    dqkv, g_decay = [], []
    for dr in range(2):
        outs = _ret_bwd(hm, cos, sin, *ret_tabs[dr], s_ins[dr], d_o, heads, dk, dv, q_off, ncc, dr == 1,
                        "ret_bwd%d" % dr)
        dqkv.append(outs[:3])
        (gl,) = ret_vjps[dr](tuple(outs[3:]))
        g_decay.append(gl)
    g_s5, du_ctx, du_lat = [], [], [dus_direct]
    for dr in range(2):
        dyp = _to_scan_layout(jnp.zeros((n_ctx, ssm_w), F32), dy_ssm, dr == 1)
        outs = _s5_bwd(ups[dr], dyp, *s5_tabs[dr], dr == 1, "s5_bwd%d" % dr)
        part_ctx, part_lat = _from_scan_layout(outs[0], n_ctx, dr == 1)
        du_ctx.append(part_ctx)
        du_lat.append(part_lat)
        g_s5.append(s5_vjps[dr](tuple(outs[1:])))
    dus = jnp.concatenate([du_ctx[0] + du_ctx[1], du_lat[0] + du_lat[1] + du_lat[2]], axis=0)
    dhm = _assemble_dhm(dus, dqkv[0][0], dqkv[1][0], dqkv[0][1], dqkv[1][1], dqkv[0][2], dqkv[1][2],
                        dg_gate, dgs, dgr, n_ctx // wide_tile, wide_tile, "assemble_dhm")
    du2 = _mm(dhm, w_mix, "nt", F32, "mix_in_dx", tm=544, tn=d, tk=1408)
    gw_mix = _mm(u2, dhm, "tn", BF16, "mix_in_dw", tm=d, tn=1408, tk=544)
    dx1, dg6[2], dsh, dsc = _ada_pre_bwd(x1, du2, dx2, g6, mods, 2, 1, nct, 2, tile, "pre2_bwd", dres_x_only=True)
    add_mod(both, 3, dsh)
    add_mod(both, 4, dsc)
    do1, dg6[1], dgt = _ada_post_bwd(dx1, o1, g6, mods, 1, 0, 0.5, nct, 2, tile, "post1_bwd")
    add_mod(both, 2, dgt)
    da1 = _mm(do1, w_out1, "nt", F32, "ffn1_out_dx", tm=544, tn=1408, tk=d)
    gw_out1 = _mm(a1, do1, "tn", BF16, "ffn1_out_dw", tm=1408, tn=d, tk=544)
    dh1 = _swiglu_bwd(h1, da1, wide_tile, "swiglu1_bwd")
    du1 = _mm(dh1, w_in1, "nt", F32, "ffn1_in_dx", tm=544, tn=d, tk=1408)
    gw_in1 = _mm(u1, dh1, "tn", BF16, "ffn1_in_dw", tm=d, tn=1408, tk=544)
    dxin, dg6[0], dsh, dsc = _ada_pre_bwd(xin, du1, dx1, g6, mods, 0, 0, nct, 2, tile, "pre1_bwd")
    add_mod(both, 0, dsh)
    add_mod(both, 1, dsc)
    grad_x = dxin[n_ctx:][None]

    zero_d = jnp.zeros((d,), F32)
    d_ada_x = jnp.stack([dmod.get((1, k), zero_d) for k in range(9)]).reshape(9 * d)
    d_ada_c = jnp.stack([dmod.get((0, k), zero_d) for k in range(9)]).reshape(9 * d)
    dg_full = jnp.stack([g[0, 0] for g in dg6])
    s5_names = 7
    s5_stack = [jnp.stack([g_s5[0][i], g_s5[1][i]]) for i in range(s5_names)]
    small_parts = [d_ada_x, d_ada_c, dg_full] + s5_stack + [d_dskip, jnp.stack(g_decay)]
    small_shapes = [p.shape for p in small_parts]
    packed = _pack(small_parts, 1024)
    gathered = _all_gather(packed, 0, "ag_small_grads").reshape(N_DEV, -1, 1024)
    summed = _sum_leading(gathered, "sum_small_grads")
    sums = _unpack(summed, small_shapes)
    sum_dx, sum_dc, sum_dg = sums[0], sums[1], sums[2]
    grad_ada_b = (sum_dx + sum_dc)[None]
    dx_rows = gathered.reshape(N_DEV, -1)[:, :9 * d]
    col0 = me * na
    da_rows = jnp.concatenate([lax.dynamic_slice_in_dim(dx_rows, col0, na, axis=1),
                               lax.dynamic_slice_in_dim(sum_dc[None], col0, na, axis=1),
                               jnp.zeros((2 * SUBLANE - N_DEV - 1, na), F32)], axis=0)
    grad_ada_w = _mm(sc, da_rows, "tn", F32, "ada_dw", tm=512, tn=na, tk=16)
    d_sc = _mm(da_rows, ada_w[0], "nt", F32, "ada_dx", tm=16, tn=512, tk=na)
    d_sc_all = _all_gather(jnp.broadcast_to(d_sc[N_DEV:N_DEV + 1], (SUBLANE, d)), 0, "ag_dctx")
    d_sc_sum = _sum_leading(d_sc_all.reshape(N_DEV, SUBLANE, d), "sum_dctx")
    grad_c_ctx = _silu_grad_rows(jnp.broadcast_to(c_ctx[None], (SUBLANE, d)), d_sc_sum, "ctx_silu_bwd")[0]
    grad_norm_g = lax.dynamic_slice_in_dim(sum_dg, me * ng_cols, ng_cols, axis=1)[None]

    def big_update(w2d, m2d, v2d, gfull, axis, name):
        p, recv = _reduce_scatter(gfull, axis, "rs_" + name)
        return _adamw_scattered(w2d, m2d, v2d, p, recv, "adamw_" + name)

    upd = {}
    res_in = [big_update(ffn_w_in[0, l], m_ffn_w_in[0, l], v_ffn_w_in[0, l], gw, 1, "ffn%d_in" % (l + 1))
              for l, gw in enumerate([gw_in1, gw_in2])]
    upd["ffn_w_in"] = [jnp.stack([res_in[0][i], res_in[1][i]])[None] for i in range(4)]
    res_out = [big_update(ffn_w_out[0, l], m_ffn_w_out[0, l], v_ffn_w_out[0, l], gw, 0, "ffn%d_out" % (l + 1))
               for l, gw in enumerate([gw_out1, gw_out2])]
    upd["ffn_w_out"] = [jnp.stack([res_out[0][i], res_out[1][i]])[None] for i in range(4)]
    upd["mix_w_in"] = [o[None] for o in big_update(mix_w_in[0], m_mix_w_in[0], v_mix_w_in[0], gw_mix, 1, "mix_in")]
    upd["ssm_glu_w"] = [o[None] for o in big_update(ssm_glu_w[0], m_ssm_glu_w[0], v_ssm_glu_w[0], gw_glu, 1, "glu")]
    upd["ret_w_proj"] = [o[None] for o in big_update(ret_w_proj[0], m_ret_w_proj[0], v_ret_w_proj[0], gw_rp, 0,
                                                     "ret_proj")]
    upd["mix_w_out"] = [o[None] for o in big_update(mix_w_out[0], m_mix_w_out[0], v_mix_w_out[0], gw_mo, 0,
                                                    "mix_out")]
    upd["ada_w"] = [o[None] for o in _adamw(ada_w[0], m_ada_w[0], v_ada_w[0], grad_ada_w[None], "adamw_ada_w")]

    small_names = ["c_ctx", "ada_b", "norm_g", "ssm_lam_re", "ssm_lam_im", "ssm_log_step", "ssm_b_re", "ssm_b_im",
                   "ssm_c_re", "ssm_c_im", "ssm_d", "ret_decay_logit"]
    small_w = [c_ctx, ada_b, norm_g, ssm_lam_re, ssm_lam_im, ssm_log_step, ssm_b_re, ssm_b_im, ssm_c_re, ssm_c_im,
               ssm_d, ret_decay_logit]
    small_m = [m_c_ctx, m_ada_b, m_norm_g, m_ssm_lam_re, m_ssm_lam_im, m_ssm_log_step, m_ssm_b_re, m_ssm_b_im,
               m_ssm_c_re, m_ssm_c_im, m_ssm_d, m_ret_decay_logit]
    small_v = [v_c_ctx, v_ada_b, v_norm_g, v_ssm_lam_re, v_ssm_lam_im, v_ssm_log_step, v_ssm_b_re, v_ssm_b_im,
               v_ssm_c_re, v_ssm_c_im, v_ssm_d, v_ret_decay_logit]
    small_g = [grad_c_ctx, grad_ada_b, grad_norm_g] + [s[None] for s in sums[3:3 + s5_names]] + \
              [sums[3 + s5_names].reshape(ssm_d.shape), sums[4 + s5_names][None]]
    shapes = [w.shape for w in small_w]
    res = _adamw(_pack(small_w, 1024), _pack(small_m, 1024), _pack(small_v, 1024), _pack(small_g, 1024)[None],
                 "adamw_small")
    small_out = [_unpack(o, shapes) for o in res]
    for i, nm in enumerate(small_names):
        upd[nm] = [small_out[kind][i] for kind in range(4)]

    order = ["c_ctx", "ada_w", "ada_b", "norm_g", "ffn_w_in", "ffn_w_out", "mix_w_in", "ssm_lam_re", "ssm_lam_im",
             "ssm_log_step", "ssm_b_re", "ssm_b_im", "ssm_c_re", "ssm_c_im", "ssm_d", "ssm_glu_w", "ret_decay_logit",
             "ret_w_proj", "mix_w_out"]
    outs = [loss, grad_x]
    for kind in range(4):
        outs += [upd[nm][kind] for nm in order]
    return tuple(outs)
```

```python
import functools
import math

import jax
import jax.numpy as jnp
import numpy as np
from jax import lax
from jax.experimental import pallas as pl
from jax.experimental.pallas import tpu as pltpu
from jax.experimental.pallas import tpu_sc as plsc

F32 = jnp.float32
BF16 = jnp.bfloat16
MXU_DTYPE = jnp.bfloat16
MESH_AXES = ("x", "y", "c")
N_DEV = 8
V7X_VMEM_LIMIT_BYTES = 56 * 1024 * 1024
LANE = 128
SUBLANE = 8

GRID_W = 64
RET_CHUNK = 128
ROPE_BASE = 10000.0
NORM_EPS = 1e-6
ADAM_LR = 0.001
ADAM_B1 = 0.9
ADAM_B2 = 0.999
ADAM_EPS = 1e-08
ADAM_WD = 0.01
ADAM_STEP = 10
SSM_TILE_GROUPS = 8
SSM_HALF_GROUPS = 4


def _params(sem=None):
    return pltpu.CompilerParams(dimension_semantics=sem, vmem_limit_bytes=V7X_VMEM_LIMIT_BYTES)


def _tile(n, target, mult):
    best = None
    t = mult
    while t <= min(n, target):
        if n % t == 0:
            best = t
        t += mult
    return n if best is None else best


def _sds(shape, dtype):
    return jax.ShapeDtypeStruct(tuple(shape), dtype)


def _mm(a, b, dims, out_dtype, name, tm=512, tn=1408, tk=2048):
    if dims == "nn":
        (m, k), (k2, n) = a.shape, b.shape
    elif dims == "nt":
        (m, k), (n, k2) = a.shape, b.shape
    else:
        (k, m), (k2, n) = a.shape, b.shape
    assert k == k2, (a.shape, b.shape, dims)
    tm = _tile(m, tm, 16)
    tn = _tile(n, tn, LANE)
    tk = _tile(k, tk, LANE if dims != "tn" else 16)
    nk = k // tk
    dn = {"nn": (((1,), (0,)), ((), ())), "nt": (((1,), (1,)), ((), ())), "tn": (((0,), (0,)), ((), ()))}[dims]

    def product(a_ref, b_ref):
        return lax.dot_general(a_ref[...].astype(MXU_DTYPE), b_ref[...].astype(MXU_DTYPE), dn,
                               preferred_element_type=F32)

    def body_single(a_ref, b_ref, o_ref):
        o_ref[...] = product(a_ref, b_ref).astype(o_ref.dtype)

    def body(a_ref, b_ref, o_ref, acc_ref):
        kk = pl.program_id(2)

        @pl.when(kk == 0)
        def _():
            acc_ref[...] = product(a_ref, b_ref)

        @pl.when((kk > 0) & (kk < nk - 1))
        def _():
            acc_ref[...] += product(a_ref, b_ref)

        @pl.when(kk == nk - 1)
        def _():
            o_ref[...] = (acc_ref[...] + product(a_ref, b_ref)).astype(o_ref.dtype)

    if dims == "nn":
        a_spec = pl.BlockSpec((tm, tk), lambda j, i, kk: (i, kk))
        b_spec = pl.BlockSpec((tk, tn), lambda j, i, kk: (kk, j))
    elif dims == "nt":
        a_spec = pl.BlockSpec((tm, tk), lambda j, i, kk: (i, kk))
        b_spec = pl.BlockSpec((tn, tk), lambda j, i, kk: (j, kk))
    else:
        a_spec = pl.BlockSpec((tk, tm), lambda j, i, kk: (kk, i))
        b_spec = pl.BlockSpec((tk, tn), lambda j, i, kk: (kk, j))
    return pl.pallas_call(
        body_single if nk == 1 else body, name=name, grid=(n // tn, m // tm, nk), in_specs=[a_spec, b_spec],
        out_specs=pl.BlockSpec((tm, tn), lambda j, i, kk: (i, j)), out_shape=_sds((m, n), out_dtype),
        scratch_shapes=[] if nk == 1 else [pltpu.VMEM((tm, tn), F32)],
        compiler_params=_params(("parallel", "parallel", "arbitrary")))(a, b)


def _rows(name, body, n_tiles, ins, outs):
    in_specs = [pl.BlockSpec(blk, imap) for (_, blk, imap) in ins]
    out_specs = [pl.BlockSpec(blk, imap) for (_, _, blk, imap) in outs]
    out_shape = [_sds(shape, dt) for (shape, dt, _, _) in outs]
    res = pl.pallas_call(body, name=name, grid=(n_tiles,), in_specs=in_specs, out_specs=out_specs,
                         out_shape=out_shape, compiler_params=_params(("arbitrary",)))(*[a for (a, _, _) in ins])
    return res


def _row_in(arr, tile, width=None, col=0, x_only_offset=None):
    width = arr.shape[1] if width is None else width
    if x_only_offset is None:
        return (arr, (tile, width), lambda i: (i, col))
    return (arr, (tile, width), lambda i: (jnp.maximum(i - x_only_offset, 0), col))


def _vec_in(arr, idx_fn):
    return (arr, (1, 1, arr.shape[2]), lambda i: (idx_fn(i), 0, 0))


def _rms(h):
    return lax.rsqrt(jnp.mean(h * h, axis=-1, keepdims=True) + NORM_EPS)


def _sigmoid(z):
    return 1.0 / (1.0 + jnp.exp(-z))


def _ada_pre_fwd(h, g6, mods, gi, mi, nct, tile, name):
    r, d = h.shape
    sel = lambda i: jnp.where(i >= nct, 1, 0)

    def body(h_ref, g_ref, sh_ref, sc_ref, u_ref):
        hh = h_ref[...]
        n = hh * _rms(hh) * g_ref[0]
        u_ref[...] = (n * (1.0 + sc_ref[0]) + sh_ref[0]).astype(u_ref.dtype)

    (u,) = _rows(name, body, r // tile,
                 [_row_in(h, tile), _vec_in(g6, lambda i: gi), _vec_in(mods, lambda i: sel(i) * 9 + 3 * mi),
                  _vec_in(mods, lambda i: sel(i) * 9 + 3 * mi + 1)],
                 [((r, d), BF16, (tile, d), lambda i: (i, 0))])
    return u


def _ada_pre_bwd(h, du, dres, g6, mods, gi, mi, nct, nsel, tile, name, dres_x_only=False):
    r, d = h.shape
    sel = lambda i: jnp.where(i >= nct, 1, 0) if nsel == 2 else 0
    msel = lambda i: jnp.where(i >= nct, 1, 0)
    off = nct if dres_x_only else None

    def body(h_ref, du_ref, dr_ref, g_ref, sc_ref, dh_ref, dg_ref, dsh_ref, dsc_ref):
        i = pl.program_id(0)
        hh = h_ref[...]
        rr = _rms(hh)
        g = g_ref[0]
        hn = hh * rr
        n = hn * g
        du_ = du_ref[...].astype(F32)
        dn = du_ * (1.0 + sc_ref[0])

        @pl.when(i == 0)
        def _():
            dg_ref[...] = jnp.zeros_like(dg_ref)

        @pl.when((i == 0) | (i == nct))
        def _():
            dsh_ref[...] = jnp.zeros_like(dsh_ref)
            dsc_ref[...] = jnp.zeros_like(dsc_ref)

        dg_ref[0] += jnp.sum(dn * hn, axis=0, keepdims=True)
        dsh_ref[0] += jnp.sum(du_, axis=0, keepdims=True)
        dsc_ref[0] += jnp.sum(du_ * n, axis=0, keepdims=True)
        t = dn * g
        dh = rr * t - hn * (rr * jnp.mean(t * hn, axis=-1, keepdims=True))
        if dres_x_only:
            dh_ref[...] = dh + jnp.where(i >= nct, dr_ref[...], 0.0)
        else:
            dh_ref[...] = dh + dr_ref[...]

    dh, dg, dsh, dsc = _rows(
        name, body, r // tile,
        [_row_in(h, tile), _row_in(du, tile), _row_in(dres, tile, x_only_offset=off), _vec_in(g6, lambda i: gi),
         _vec_in(mods, lambda i: msel(i) * 9 + 3 * mi + 1)],
        [((r, d), F32, (tile, d), lambda i: (i, 0)), ((1, 1, d), F32, (1, 1, d), lambda i: (0, 0, 0)),
         ((nsel, 1, d), F32, (1, 1, d), lambda i: (sel(i), 0, 0)),
         ((nsel, 1, d), F32, (1, 1, d), lambda i: (sel(i), 0, 0))])
    return dh, dg, dsh, dsc


def _ada_post_fwd(h, o, g6, mods, gi, mi, res_w, nct, tile, name, h_x_only=False):
    r, d = o.shape
    sel = lambda i: jnp.where(i >= nct, 1, 0)

    def body(h_ref, o_ref, g_ref, gt_ref, y_ref):
        oo = o_ref[...]
        n = oo * _rms(oo) * g_ref[0]
        y_ref[...] = h_ref[...] + res_w * gt_ref[0] * n

    (y,) = _rows(name, body, r // tile,
                 [_row_in(h, tile), _row_in(o, tile), _vec_in(g6, lambda i: gi),
                  _vec_in(mods, lambda i: sel(i) * 9 + 3 * mi + 2)],
                 [((r, d), F32, (tile, d), lambda i: (i, 0))])
    return y


def _ada_post_bwd(dy, o, g6, mods, gi, mi, res_w, nct, nsel, tile, name):
    r, d = o.shape
    sel = lambda i: jnp.where(i >= nct, 1, 0) if nsel == 2 else 0
    msel = lambda i: jnp.where(i >= nct, 1, 0)

    def body(dy_ref, o_ref, g_ref, gt_ref, do_ref, dg_ref, dgt_ref):
        i = pl.program_id(0)
        oo = o_ref[...]
        rr = _rms(oo)
        g = g_ref[0]
        on = oo * rr
        dy_ = dy_ref[...] * res_w

        @pl.when(i == 0)
        def _():
            dg_ref[...] = jnp.zeros_like(dg_ref)

        @pl.when((i == 0) | (i == nct))
        def _():
            dgt_ref[...] = jnp.zeros_like(dgt_ref)

        dgt_ref[0] += jnp.sum(dy_ * (on * g), axis=0, keepdims=True)
        dn = dy_ * gt_ref[0]
        dg_ref[0] += jnp.sum(dn * on, axis=0, keepdims=True)
        t = dn * g
        do_ref[...] = (rr * t - on * (rr * jnp.mean(t * on, axis=-1, keepdims=True))).astype(do_ref.dtype)

    do, dg, dgt = _rows(
        name, body, r // tile,
        [_row_in(dy, tile), _row_in(o, tile), _vec_in(g6, lambda i: gi),
         _vec_in(mods, lambda i: msel(i) * 9 + 3 * mi + 2)],
        [((r, d), BF16, (tile, d), lambda i: (i, 0)), ((1, 1, d), F32, (1, 1, d), lambda i: (0, 0, 0)),
         ((nsel, 1, d), F32, (1, 1, d), lambda i: (sel(i), 0, 0))])
    return do, dg, dgt


def _swiglu_fwd(h, tile, name):
    r, w2 = h.shape
    f = w2 // 2

    def body(h_ref, a_ref):
        gt = h_ref[:, :f]
        up = h_ref[:, f:]
        a_ref[...] = (gt * _sigmoid(gt) * up).astype(a_ref.dtype)

    (a,) = _rows(name, body, r // tile, [_row_in(h, tile)], [((r, f), BF16, (tile, f), lambda i: (i, 0))])
    return a


def _swiglu_bwd(h, da, tile, name):
    r, w2 = h.shape
    f = w2 // 2

    def body(h_ref, da_ref, dh_ref):
        gt = h_ref[:, :f]
        up = h_ref[:, f:]
        d = da_ref[...]
        sg = _sigmoid(gt)
        dh_ref[:, :f] = (d * up * (sg * (1.0 + gt * (1.0 - sg)))).astype(dh_ref.dtype)
        dh_ref[:, f:] = (d * gt * sg).astype(dh_ref.dtype)

    (dh,) = _rows(name, body, r // tile, [_row_in(h, tile), _row_in(da, tile)],
                  [((r, w2), BF16, (tile, w2), lambda i: (i, 0))])
    return dh


def _gelu_parts(y):
    c0 = math.sqrt(2.0 / math.pi)
    inner = c0 * (y + 0.044715 * y * y * y)
    th = jnp.tanh(inner)
    return th, c0 * (1.0 + 3 * 0.044715 * y * y)


def _ssm_out_fwd(y0, y1, hm, dskip, nct, tile, name):
    t_rows, s = y0.shape

    def body(y0_ref, y1_ref, u_ref, d_ref, a_ref):
        y = y0_ref[...] + y1_ref[...] + d_ref[0] * u_ref[...]
        th, _ = _gelu_parts(y)
        a_ref[...] = (0.5 * y * (1.0 + th)).astype(a_ref.dtype)

    (a,) = _rows(name, body, t_rows // tile,
                 [_row_in(y0, tile), _row_in(y1, tile), (hm, (tile, s), lambda i: (i + nct, 0)),
                  _vec_in(dskip, lambda i: 0)],
                 [((t_rows, s), BF16, (tile, s), lambda i: (i, 0))])
    return a


def _ssm_out_bwd(y0, y1, hm, dskip, da, nct, tile, name):
    t_rows, s = y0.shape

    def body(y0_ref, y1_ref, u_ref, d_ref, da_ref, dy_ref, du_ref, dd_ref):
        i = pl.program_id(0)
        u = u_ref[...]
        y = y0_ref[...] + y1_ref[...] + d_ref[0] * u
        th, dinner = _gelu_parts(y)
        dy = da_ref[...] * (0.5 * (1.0 + th) + 0.5 * y * (1.0 - th * th) * dinner)
        dy_ref[...] = dy
        du_ref[...] = dy * d_ref[0]

        @pl.when(i == 0)
        def _():
            dd_ref[...] = jnp.zeros_like(dd_ref)

        dd_ref[0] += jnp.sum(dy * u, axis=0, keepdims=True)

    dy, du, dd = _rows(name, body, t_rows // tile,
                       [_row_in(y0, tile), _row_in(y1, tile), (hm, (tile, s), lambda i: (i + nct, 0)),
                        _vec_in(dskip, lambda i: 0), _row_in(da, tile)],
                       [((t_rows, s), F32, (tile, s), lambda i: (i, 0)), ((t_rows, s), F32, (tile, s), lambda i: (i, 0)),
                        ((1, 1, s), F32, (1, 1, s), lambda i: (0, 0, 0))])
    return dy, du, dd


def _col_pieces(arr, off, width, tile, nct, unit=None):
    pw = math.gcd(off, width if unit is None else unit)
    specs = [(arr, (tile, pw), functools.partial(lambda i, cb: (i + nct, cb), cb=off // pw + p))
             for p in range(width // pw)]
    return specs, pw


def _ret_gate_fwd(o0, o1, hm, g_off, heads, dv, nct, tile, name):
    t_rows, w = o0.shape
    g_specs, pw = _col_pieces(hm, g_off, w, tile, nct)
    ng = len(g_specs)

    def body(o0_ref, o1_ref, *refs):
        g_refs, r_ref = refs[:ng], refs[ng]
        for hd in range(heads):
            cs = slice(hd * dv, (hd + 1) * dv)
            o = o0_ref[:, cs] + o1_ref[:, cs]
            lo = (hd * dv) % pw
            g = g_refs[(hd * dv) // pw][:, lo:lo + dv]
            r_ref[:, cs] = (g * _sigmoid(g) * (o * _rms(o))).astype(r_ref.dtype)

    (ri,) = _rows(name, body, t_rows // tile, [_row_in(o0, tile), _row_in(o1, tile)] + g_specs,
                  [((t_rows, w), BF16, (tile, w), lambda i: (i, 0))])
    return ri


def _ret_gate_bwd(o0, o1, hm, g_off, dri, heads, dv, nct, tile, name):
    t_rows, w = o0.shape
    g_specs, pw = _col_pieces(hm, g_off, w, tile, nct)
    ng = len(g_specs)

    def body(o0_ref, o1_ref, d_ref, *refs):
        g_refs, do_ref, dg_ref = refs[:ng], refs[ng], refs[ng + 1]
        for hd in range(heads):
            cs = slice(hd * dv, (hd + 1) * dv)
            o = o0_ref[:, cs] + o1_ref[:, cs]
            lo = (hd * dv) % pw
            g = g_refs[(hd * dv) // pw][:, lo:lo + dv]
            d = d_ref[:, cs]
            rr = _rms(o)
            on = o * rr
            sg = _sigmoid(g)
            dg_ref[:, cs] = d * on * (sg * (1.0 + g * (1.0 - sg)))
            t = d * (g * sg)
            do_ref[:, cs] = rr * t - on * (rr * jnp.mean(t * on, axis=-1, keepdims=True))

    do, dg = _rows(name, body, t_rows // tile, [_row_in(o0, tile), _row_in(o1, tile), _row_in(dri, tile)] + g_specs,
                   [((t_rows, w), F32, (tile, w), lambda i: (i, 0)), ((t_rows, w), F32, (tile, w), lambda i: (i, 0))])
    return do, dg


def _merge_fwd(gab, rb, hm, gs_off, nct, tile, name):
    t_rows, d = rb.shape
    specs, pw = _col_pieces(hm, gs_off, 2 * d, tile, nct, unit=d)
    npc = d // pw

    def body(gab_ref, rb_ref, *refs):
        gs_refs, gr_refs, m_ref = refs[:npc], refs[npc:2 * npc], refs[2 * npc]
        for p in range(npc):
            cs = slice(p * pw, (p + 1) * pw)
            ga = gab_ref[:, cs]
            gb = gab_ref[:, d + p * pw:d + (p + 1) * pw]
            m_ref[:, cs] = (_sigmoid(gs_refs[p][...]) * (ga * _sigmoid(gb))
                            + _sigmoid(gr_refs[p][...]) * rb_ref[:, cs]).astype(m_ref.dtype)

    (mg,) = _rows(name, body, t_rows // tile, [_row_in(gab, tile), _row_in(rb, tile)] + specs,
                  [((t_rows, d), BF16, (tile, d), lambda i: (i, 0))])
    return mg


def _merge_bwd(gab, rb, hm, gs_off, dm, nct, tile, name):
    t_rows, d = rb.shape
    specs, pw = _col_pieces(hm, gs_off, 2 * d, tile, nct, unit=d)
    npc = d // pw

    def body(gab_ref, rb_ref, dm_ref, *refs):
        gs_refs, gr_refs = refs[:npc], refs[npc:2 * npc]
        dgab_ref, drb_ref, dgs_ref, dgr_ref = refs[2 * npc:]
        for p in range(npc):
            cs = slice(p * pw, (p + 1) * pw)
            cs2 = slice(d + p * pw, d + (p + 1) * pw)
            ga = gab_ref[:, cs]
            gb = gab_ref[:, cs2]
            dmm = dm_ref[:, cs]
            ss = _sigmoid(gs_refs[p][...])
            sr = _sigmoid(gr_refs[p][...])
            sb = _sigmoid(gb)
            dbr = dmm * ss
            dgab_ref[:, cs] = (dbr * sb).astype(dgab_ref.dtype)
            dgab_ref[:, cs2] = (dbr * ga * sb * (1.0 - sb)).astype(dgab_ref.dtype)
            drb_ref[:, cs] = (dmm * sr).astype(drb_ref.dtype)
            dgs_ref[:, cs] = dmm * (ga * sb) * ss * (1.0 - ss)
            dgr_ref[:, cs] = dmm * rb_ref[:, cs] * sr * (1.0 - sr)

    return _rows(name, body, t_rows // tile, [_row_in(gab, tile), _row_in(rb, tile), _row_in(dm, tile)] + specs,
                 [((t_rows, 2 * d), BF16, (tile, 2 * d), lambda i: (i, 0)), ((t_rows, d), BF16, (tile, d), lambda i: (i, 0)),
                  ((t_rows, d), F32, (tile, d), lambda i: (i, 0)), ((t_rows, d), F32, (tile, d), lambda i: (i, 0))])


def _assemble_dhm(dus, dq0, dq1, dk0, dk1, dv0, dv1, dg, dgs, dgr, nct, tile, name):
    r, s = dus.shape
    qk = dq0.shape[1]
    vw = dv0.shape[1]
    d = dgs.shape[1]
    mi = s + 2 * qk + 2 * vw + 2 * d
    c_q, c_k, c_v, c_g, c_gs, c_gr = s, s + qk, s + 2 * qk, s + 2 * qk + vw, s + 2 * qk + 2 * vw, s + 2 * qk + 2 * vw + d

    def body(dus_ref, dq0_ref, dq1_ref, dk0_ref, dk1_ref, dv0_ref, dv1_ref, dg_ref, dgs_ref, dgr_ref, o_ref):
        i = pl.program_id(0)
        lat = i >= nct
        o_ref[:, :s] = dus_ref[...].astype(o_ref.dtype)
        o_ref[:, c_q:c_k] = (dq0_ref[...] + dq1_ref[...]).astype(o_ref.dtype)
        o_ref[:, c_k:c_v] = (dk0_ref[...] + dk1_ref[...]).astype(o_ref.dtype)
        o_ref[:, c_v:c_g] = (dv0_ref[...] + dv1_ref[...]).astype(o_ref.dtype)
        o_ref[:, c_g:c_gs] = jnp.where(lat, dg_ref[...], 0.0).astype(o_ref.dtype)
        o_ref[:, c_gs:c_gr] = jnp.where(lat, dgs_ref[...], 0.0).astype(o_ref.dtype)
        o_ref[:, c_gr:] = jnp.where(lat, dgr_ref[...], 0.0).astype(o_ref.dtype)

    (out,) = _rows(name, body, r // tile,
                   [_row_in(dus, tile), _row_in(dq0, tile), _row_in(dq1, tile), _row_in(dk0, tile), _row_in(dk1, tile),
                    _row_in(dv0, tile), _row_in(dv1, tile), _row_in(dg, tile, x_only_offset=nct),
                    _row_in(dgs, tile, x_only_offset=nct), _row_in(dgr, tile, x_only_offset=nct)],
                   [((r, mi), BF16, (tile, mi), lambda i: (i, 0))])
    return out


def _loss_grad(y, target, tile, name):
    t_rows, d = y.shape

    def body(y_ref, t_ref, dy_ref, l_ref):
        i = pl.program_id(0)
        e = y_ref[...] - t_ref[...]
        dy_ref[...] = e * (1.0 / d)

        @pl.when(i == 0)
        def _():
            l_ref[...] = jnp.zeros_like(l_ref)

        l_ref[0] += jnp.sum(e * e, axis=0, keepdims=True)

    return _rows(name, body, t_rows // tile, [_row_in(y, tile), _row_in(target, tile)],
                 [((t_rows, d), F32, (tile, d), lambda i: (i, 0)), ((1, 1, d), F32, (1, 1, d), lambda i: (0, 0, 0))])


def _silu_rows(v, name):
    def body(v_ref, o_ref):
        z = v_ref[...]
        o_ref[...] = z * _sigmoid(z)

    (o,) = _rows(name, body, 1, [_row_in(v, v.shape[0])], [(v.shape, F32, v.shape, lambda i: (0, 0))])
    return o


def _silu_grad_rows(v, dv, name):
    def body(v_ref, d_ref, o_ref):
        z = v_ref[...]
        sg = _sigmoid(z)
        o_ref[...] = d_ref[...] * (sg * (1.0 + z * (1.0 - sg)))

    (o,) = _rows(name, body, 1, [_row_in(v, v.shape[0]), _row_in(dv, v.shape[0])],
                 [(v.shape, F32, v.shape, lambda i: (0, 0))])
    return o


def _sum_leading(g8, name):
    n, r, c = g8.shape
    tile = _tile(r, 256, SUBLANE)

    def body(g_ref, o_ref):
        acc = g_ref[0]
        for j in range(1, n):
            acc = acc + g_ref[j]
        o_ref[...] = acc

    (o,) = _rows(name, body, r // tile, [(g8, (n, tile, c), lambda i: (0, i, 0))],
                 [((r, c), F32, (tile, c), lambda i: (i, 0))])
    return o


def _pair_sum(g, recv, axis, name):
    n, br, bc = recv.shape
    tile = _tile(br, 256, 16)
    nrt = br // tile
    core = lax.axis_index("c").astype(jnp.int32).reshape(1)

    def body(c_ref, g_ref, r_ref, o_ref):
        o_ref[0] = (g_ref[...].astype(F32) + r_ref[0].astype(F32)).astype(o_ref.dtype)

    if axis == 1:
        g_spec = pl.BlockSpec((tile, bc), lambda q, i, c_ref: (i, 2 * q + c_ref[0]))
    else:
        g_spec = pl.BlockSpec((tile, bc), lambda q, i, c_ref: ((2 * q + c_ref[0]) * nrt + i, 0))
    slot = pl.BlockSpec((1, tile, bc), lambda q, i, c_ref: (q, i, 0))
    return pl.pallas_call(
        body, name=name, out_shape=_sds((n, br, bc), recv.dtype),
        grid_spec=pltpu.PrefetchScalarGridSpec(num_scalar_prefetch=1, grid=(n, nrt), in_specs=[g_spec, slot],
                                               out_specs=slot),
        compiler_params=_params(("arbitrary", "arbitrary")))(core, g, recv)


def _adam_math(w, m, v, g):
    c1 = 1.0 / (1.0 - ADAM_B1 ** ADAM_STEP)
    c2 = 1.0 / (1.0 - ADAM_B2 ** ADAM_STEP)
    mm = ADAM_B1 * m + (1.0 - ADAM_B1) * g
    vv = ADAM_B2 * v + (1.0 - ADAM_B2) * (g * g)
    return -ADAM_LR * ((mm * c1) / (jnp.sqrt(vv * c2) + ADAM_EPS) + ADAM_WD * w), mm, vv


def _adamw(w, m, v, gparts, name):
    r, c = w.shape
    n = gparts.shape[0]
    tile = _tile(r, 256, 16)

    def body(w_ref, m_ref, v_ref, g_ref, go_ref, d_ref, mo_ref, vo_ref):
        g = g_ref[0].astype(F32)
        for j in range(1, n):
            g = g + g_ref[j].astype(F32)
        go_ref[...] = g
        d_ref[...], mo_ref[...], vo_ref[...] = _adam_math(w_ref[...], m_ref[...], v_ref[...], g)

    rs = lambda arr: _row_in(arr, tile)
    out = ((r, c), F32, (tile, c), lambda i: (i, 0))
    return _rows(name, body, r // tile, [rs(w), rs(m), rs(v), (gparts, (n, tile, c), lambda i: (0, i, 0))],
                 [out, out, out, out])


def _adamw_scattered(w, m, v, p, recv, name):
    r, c = w.shape
    n = recv.shape[0]
    tile = _tile(r, 256, 16)
    chip = (2 * lax.axis_index("x") + lax.axis_index("y")).astype(jnp.int32).reshape(1)

    def body(q_ref, w_ref, m_ref, v_ref, p_ref, g_ref, go_ref, d_ref, mo_ref, vo_ref):
        g = p_ref[0].astype(F32)
        for j in range(n):
            g = g + g_ref[j].astype(F32)
        go_ref[...] = g
        d_ref[...], mo_ref[...], vo_ref[...] = _adam_math(w_ref[...], m_ref[...], v_ref[...], g)

    row = pl.BlockSpec((tile, c), lambda i, q_ref: (i, 0))
    out = _sds((r, c), F32)
    return pl.pallas_call(
        body, name=name, out_shape=[out, out, out, out],
        grid_spec=pltpu.PrefetchScalarGridSpec(
            num_scalar_prefetch=1, grid=(r // tile,),
            in_specs=[row, row, row, pl.BlockSpec((1, tile, c), lambda i, q_ref: (q_ref[0], i, 0)),
                      pl.BlockSpec((n, tile, c), lambda i, q_ref: (0, i, 0))],
            out_specs=[row, row, row, row]),
        compiler_params=_params(("arbitrary",)))(chip, w, m, v, p, recv)


def _cmul(ar, ai, br, bi):
    return ar * br - ai * bi, ar * bi + ai * br


def _cpow(ar, ai, n):
    pr, pi = jnp.ones_like(ar), jnp.zeros_like(ar)
    br, bi = ar, ai
    while n:
        if n & 1:
            pr, pi = _cmul(pr, pi, br, bi)
        n >>= 1
        if n:
            br, bi = _cmul(br, bi, br, bi)
    return pr, pi


def _s5_scan_into(xr_ref, xi_ref, ar1, ai1, ns, fr_ref, fi_ref, hr_ref, hi_ref, reverse):
    st = ar1.shape[1]
    ar = jnp.broadcast_to(ar1, (SUBLANE, st))
    ai = jnp.broadcast_to(ai1, (SUBLANE, st))
    zero = jnp.zeros((SUBLANE, st), F32)
    zero1 = jnp.zeros((1, st), F32)

    def slab(k):
        return pl.ds(pl.multiple_of(k * SUBLANE, SUBLANE), SUBLANE)

    def pass1(j, carry):
        hr, hi = carry
        k = ns - 1 - j if reverse else j
        nr, ni = _cmul(ar, ai, hr, hi)
        return nr + xr_ref[slab(k), :], ni + xi_ref[slab(k), :]

    fr, fi = lax.fori_loop(0, ns, pass1, (zero, zero))
    fr_ref[...] = fr
    fi_ref[...] = fi
    pr, pi = _cpow(ar1, ai1, ns)
    order = list(range(N_DEV - 1, -1, -1)) if reverse else list(range(N_DEV))
    hr_ref[order[0]:order[0] + 1, :] = zero1
    hi_ref[order[0]:order[0] + 1, :] = zero1
    for a_, b_ in zip(order[:-1], order[1:]):
        cr, ci = _cmul(pr, pi, hr_ref[a_:a_ + 1, :], hi_ref[a_:a_ + 1, :])
        hr_ref[b_:b_ + 1, :] = cr + fr_ref[a_:a_ + 1, :]
        hi_ref[b_:b_ + 1, :] = ci + fi_ref[a_:a_ + 1, :]

    def pass2(j, carry):
        hr, hi = carry
        k = ns - 1 - j if reverse else j
        nr, ni = _cmul(ar, ai, hr, hi)
        nr = nr + xr_ref[slab(k), :]
        ni = ni + xi_ref[slab(k), :]
        xr_ref[slab(k), :] = nr
        xi_ref[slab(k), :] = ni
        return nr, ni

    lax.fori_loop(0, ns, pass2, (hr_ref[...], hi_ref[...]))


def _s5_specs(r, ch, st):
    u_spec = pl.BlockSpec((r, ch), lambda j: (0, j // 2))
    w_spec = pl.BlockSpec((1, ch, st), lambda j: (j, 0, 0))
    c_spec = pl.BlockSpec((1, st, ch), lambda j: (j, 0, 0))
    a_spec = pl.BlockSpec((1, 2, st), lambda j: (j, 0, 0))
    return u_spec, w_spec, c_spec, a_spec


def _s5_fwd(up, wre, wim, cre, cim, a, rev, name):
    r, s = up.shape
    nh, ch, st = wre.shape
    ns = r // N_DEV
    u_spec, w_spec, c_spec, a_spec = _s5_specs(r, ch, st)

    def body(u_ref, wre_ref, wim_ref, cre_ref, cim_ref, a_ref, y_ref, xr, xi, fr, fi, hr, hi):
        j = pl.program_id(0)
        for rb in range(N_DEV):
            rows = slice(rb * ns, (rb + 1) * ns)
            ub = u_ref[rows, :].astype(MXU_DTYPE)
            xr[rows, :] = jnp.dot(ub, wre_ref[0].astype(MXU_DTYPE), preferred_element_type=F32)
            xi[rows, :] = jnp.dot(ub, wim_ref[0].astype(MXU_DTYPE), preferred_element_type=F32)
        _s5_scan_into(xr, xi, a_ref[0, 0:1, :], a_ref[0, 1:2, :], ns, fr, fi, hr, hi, rev)
        for rb in range(N_DEV):
            rows = slice(rb * ns, (rb + 1) * ns)
            yb = (jnp.dot(xr[rows, :].astype(MXU_DTYPE), cre_ref[0].astype(MXU_DTYPE), preferred_element_type=F32)
                  - jnp.dot(xi[rows, :].astype(MXU_DTYPE), cim_ref[0].astype(MXU_DTYPE), preferred_element_type=F32))

            @pl.when(j % 2 == 0)
            def _():
                y_ref[rows, :] = yb

            @pl.when(j % 2 == 1)
            def _():
                y_ref[rows, :] += yb

    small = pltpu.VMEM((SUBLANE, st), F32)
    return pl.pallas_call(
        body, name=name, grid=(nh,), in_specs=[u_spec, w_spec, w_spec, c_spec, c_spec, a_spec],
        out_specs=pl.BlockSpec((r, ch), lambda j: (0, j // 2)), out_shape=_sds((r, s), F32),
        scratch_shapes=[pltpu.VMEM((r, st), F32), pltpu.VMEM((r, st), F32), small, small, small, small],
        compiler_params=_params(("arbitrary",)))(up, wre, wim, cre, cim, a)


def _s5_bwd(up, dyp, wre, wim, cre, cim, a, rev, name):
    r, s = up.shape
    nh, ch, st = wre.shape
    ns = r // N_DEV
    u_spec, w_spec, c_spec, a_spec = _s5_specs(r, ch, st)
    nt = (((1,), (1,)), ((), ()))
    tn = (((0,), (0,)), ((), ()))

    def body(u_ref, dy_ref, wre_ref, wim_ref, cre_ref, cim_ref, a_ref,
             du_ref, dwre_ref, dwim_ref, dcre_ref, dcim_ref, da_ref,
             hr, hi, gr, gi, fr, fi, sr, si, er, ei):
        j = pl.program_id(0)
        wre_b = wre_ref[0].astype(MXU_DTYPE)
        wim_b = wim_ref[0].astype(MXU_DTYPE)
        cre_b = cre_ref[0].astype(MXU_DTYPE)
        cim_b = cim_ref[0].astype(MXU_DTYPE)
        for rb in range(N_DEV):
            rows = slice(rb * ns, (rb + 1) * ns)
            ub = u_ref[rows, :].astype(MXU_DTYPE)
            hr[rows, :] = jnp.dot(ub, wre_b, preferred_element_type=F32)
            hi[rows, :] = jnp.dot(ub, wim_b, preferred_element_type=F32)
        ar1, ai1 = a_ref[0, 0:1, :], a_ref[0, 1:2, :]
        _s5_scan_into(hr, hi, ar1, ai1, ns, fr, fi, sr, si, rev)
        dcre = jnp.zeros((st, ch), F32)
        dcim = jnp.zeros((st, ch), F32)
        for rb in range(N_DEV):
            rows = slice(rb * ns, (rb + 1) * ns)
            dyb = dy_ref[rows, :].astype(MXU_DTYPE)
            gr[rows, :] = lax.dot_general(dyb, cre_b, nt, preferred_element_type=F32)
            gi[rows, :] = -lax.dot_general(dyb, cim_b, nt, preferred_element_type=F32)
            dcre += lax.dot_general(hr[rows, :].astype(MXU_DTYPE), dyb, tn, preferred_element_type=F32)
            dcim -= lax.dot_general(hi[rows, :].astype(MXU_DTYPE), dyb, tn, preferred_element_type=F32)
        dcre_ref[0] = dcre
        dcim_ref[0] = dcim
        _s5_scan_into(gr, gi, ar1, -ai1, ns, fr, fi, er, ei, not rev)

        def slab(k):
            return pl.ds(pl.multiple_of(k * SUBLANE, SUBLANE), SUBLANE)

        step_back = 1 if rev else -1

        def acc_step(k, carry):
            acr, aci = carry
            g_r, g_i = gr[slab(k), :], gi[slab(k), :]
            p_r, p_i = hr[slab(k + step_back), :], hi[slab(k + step_back), :]
            return acr + g_r * p_r + g_i * p_i, aci + g_i * p_r - g_r * p_i

        edge = (ns - 1) * SUBLANE if rev else 0
        g_r, g_i = gr[edge:edge + SUBLANE, :], gi[edge:edge + SUBLANE, :]
        p_r, p_i = sr[...], si[...]
        lo, hi_k = (0, ns - 1) if rev else (1, ns)
        acr, aci = lax.fori_loop(lo, hi_k, acc_step, (g_r * p_r + g_i * p_i, g_i * p_r - g_r * p_i))
        da_ref[0, 0:1, :] = jnp.sum(acr, axis=0, keepdims=True)
        da_ref[0, 1:2, :] = jnp.sum(aci, axis=0, keepdims=True)
        dwre = jnp.zeros((ch, st), F32)
        dwim = jnp.zeros((ch, st), F32)
        for rb in range(N_DEV):
            rows = slice(rb * ns, (rb + 1) * ns)
            grb = gr[rows, :].astype(MXU_DTYPE)
            gib = gi[rows, :].astype(MXU_DTYPE)
            ub = u_ref[rows, :].astype(MXU_DTYPE)
            dub = (lax.dot_general(grb, wre_b, nt, preferred_element_type=F32)
                   + lax.dot_general(gib, wim_b, nt, preferred_element_type=F32))
            dwre += lax.dot_general(ub, grb, tn, preferred_element_type=F32)
            dwim += lax.dot_general(ub, gib, tn, preferred_element_type=F32)

            @pl.when(j % 2 == 0)
            def _():
                du_ref[rows, :] = dub

            @pl.when(j % 2 == 1)
            def _():
                du_ref[rows, :] += dub

        dwre_ref[0] = dwre
        dwim_ref[0] = dwim

    small = pltpu.VMEM((SUBLANE, st), F32)
    big = pltpu.VMEM((r, st), F32)
    return pl.pallas_call(
        body, name=name, grid=(nh,), in_specs=[u_spec, u_spec, w_spec, w_spec, c_spec, c_spec, a_spec],
        out_specs=[pl.BlockSpec((r, ch), lambda j: (0, j // 2)), w_spec, w_spec, c_spec, c_spec, a_spec],
        out_shape=[_sds((r, s), F32), _sds(wre.shape, F32), _sds(wre.shape, F32), _sds(cre.shape, F32),
                   _sds(cre.shape, F32), _sds(a.shape, F32)],
        scratch_shapes=[big, big, big, big, small, small, small, small, small, small],
        compiler_params=_params(("arbitrary",)))(up, dyp, wre, wim, cre, cim, a)


def _rope(t, cos, sin):
    quarter = t.shape[1] // 4
    lane = lax.broadcasted_iota(jnp.int32, t.shape, 1)
    first = (lane // quarter) % 2 == 0
    partner = jnp.where(first, pltpu.roll(t, t.shape[1] - quarter, 1), pltpu.roll(t, quarter, 1))
    return t * cos + partner * sin


def _rope_t(d, cos, sin):
    quarter = d.shape[1] // 4
    ds_ = d * sin
    lane = lax.broadcasted_iota(jnp.int32, d.shape, 1)
    first = (lane // quarter) % 2 == 0
    partner = jnp.where(first, pltpu.roll(ds_, d.shape[1] - quarter, 1), pltpu.roll(ds_, quarter, 1))
    return d * cos + partner


def _chunk_of_step(s, nch, ncc, rev):
    if not rev:
        return s
    return jnp.where(s < ncc, ncc - 1 - s, nch + ncc - 1 - s)


def _ret_fwd(hm, cos, sin, decay, wend, win, gch, heads, dk, dv, q_off, ncc, rev, name):
    r = hm.shape[0]
    ch = RET_CHUNK
    nch = r // ch
    t_rows = r - ncc * ch
    qb, kb, vb = q_off // dk, (q_off + heads * dk) // dk, (q_off + 2 * heads * dk) // dv
    q_scale = dk ** -0.5
    nt = (((1,), (1,)), ((), ()))
    tn = (((0,), (0,)), ((), ()))
    cof = lambda s: _chunk_of_step(s, nch, ncc, rev)

    def body(q_ref, k_ref, v_ref, cos_ref, sin_ref, dec_ref, we_ref, wi_ref, g_ref, o_ref, sin_out, st):
        s = pl.program_id(1)

        @pl.when(s == 0)
        def _():
            st[...] = jnp.zeros_like(st)

        q = _rope(q_ref[...], cos_ref[...], sin_ref[...]) * q_scale
        k = _rope(k_ref[...], cos_ref[...], sin_ref[...])
        v = v_ref[...].astype(MXU_DTYPE)
        s_cur = st[...]
        sin_out[0, 0] = s_cur
        kw = (k * we_ref[0]).astype(MXU_DTYPE)
        qw = (q * wi_ref[0]).astype(MXU_DTYPE)
        scores = lax.dot_general(q.astype(MXU_DTYPE), k.astype(MXU_DTYPE), nt, preferred_element_type=F32) * dec_ref[0]
        o_ref[...] = (jnp.dot(scores.astype(MXU_DTYPE), v, preferred_element_type=F32)
                      + jnp.dot(qw, s_cur.astype(MXU_DTYPE), preferred_element_type=F32))
        st[...] = g_ref[0] * s_cur + lax.dot_general(kw, v, tn, preferred_element_type=F32)

    tab = lambda w: pl.BlockSpec((1, ch, w), lambda h, s: (h, 0, 0))
    return pl.pallas_call(
        body, name=name, grid=(heads, nch),
        in_specs=[pl.BlockSpec((ch, dk), lambda h, s: (cof(s), qb + h)),
                  pl.BlockSpec((ch, dk), lambda h, s: (cof(s), kb + h)),
                  pl.BlockSpec((ch, dv), lambda h, s: (cof(s), vb + h)),
                  pl.BlockSpec((ch, dk), lambda h, s: (cof(s), 0)),
                  pl.BlockSpec((ch, dk), lambda h, s: (cof(s), 0)),
                  tab(ch), tab(dk), tab(dk), tab(dv)],
        out_specs=[pl.BlockSpec((ch, dv), lambda h, s: (jnp.maximum(cof(s) - ncc, 0) if not rev
                                                         else jnp.where(s < ncc, nch - ncc - 1, cof(s) - ncc), h)),
                   pl.BlockSpec((1, 1, dk, dv), lambda h, s: (h, s, 0, 0))],
        out_shape=[_sds((t_rows, heads * dv), F32), _sds((heads, nch, dk, dv), F32)],
        scratch_shapes=[pltpu.VMEM((dk, dv), F32)],
        compiler_params=_params(("parallel", "arbitrary")))(hm, hm, hm, cos, sin, decay, wend, win, gch)


def _ret_bwd(hm, cos, sin, decay, wend, win, gch, s_in, do, heads, dk, dv, q_off, ncc, rev, name):
    r = hm.shape[0]
    ch = RET_CHUNK
    nch = r // ch
    qb, kb, vb = q_off // dk, (q_off + heads * dk) // dk, (q_off + 2 * heads * dk) // dv
    q_scale = dk ** -0.5
    nt = (((1,), (1,)), ((), ()))
    tn = (((0,), (0,)), ((), ()))
    cof = lambda rr: _chunk_of_step(nch - 1 - rr, nch, ncc, rev)

    def body(q_ref, k_ref, v_ref, cos_ref, sin_ref, dec_ref, we_ref, wi_ref, g_ref, sin_ref2, do_ref,
             dq_ref, dk_ref, dv_ref, ddec_ref, dwe_ref, dwi_ref, dg_ref, dst):
        rr = pl.program_id(1)
        n = cof(rr)

        @pl.when(rr == 0)
        def _():
            dst[...] = jnp.zeros_like(dst)
            ddec_ref[...] = jnp.zeros_like(ddec_ref)
            dwe_ref[...] = jnp.zeros_like(dwe_ref)
            dwi_ref[...] = jnp.zeros_like(dwi_ref)
            dg_ref[...] = jnp.zeros_like(dg_ref)

        cos_, sin_ = cos_ref[...], sin_ref[...]
        q = _rope(q_ref[...], cos_, sin_) * q_scale
        k = _rope(k_ref[...], cos_, sin_)
        v = v_ref[...].astype(MXU_DTYPE)
        qb_, kb_ = q.astype(MXU_DTYPE), k.astype(MXU_DTYPE)
        kw = (k * we_ref[0]).astype(MXU_DTYPE)
        qw = (q * wi_ref[0]).astype(MXU_DTYPE)
        sraw = lax.dot_general(qb_, kb_, nt, preferred_element_type=F32)
        scores = (sraw * dec_ref[0]).astype(MXU_DTYPE)
        d_o = jnp.where(n >= ncc, do_ref[...], 0.0).astype(MXU_DTYPE)
        s_n = sin_ref2[0, 0]
        s_nb = s_n.astype(MXU_DTYPE)
        ds1 = dst[...]
        ds1b = ds1.astype(MXU_DTYPE)
        dsc = lax.dot_general(d_o, v, nt, preferred_element_type=F32)
        dsr = (dsc * dec_ref[0]).astype(MXU_DTYPE)
        ddec_ref[0] += dsc * sraw
        t1 = lax.dot_general(d_o, s_nb, nt, preferred_element_type=F32)
        dq_r = jnp.dot(dsr, kb_, preferred_element_type=F32) + t1 * wi_ref[0]
        dwi_ref[0] += t1 * q
        t2 = lax.dot_general(v, ds1b, nt, preferred_element_type=F32)
        dk_r = lax.dot_general(dsr, qb_, tn, preferred_element_type=F32) + t2 * we_ref[0]
        dwe_ref[0] += t2 * k
        dv_ref[...] = (lax.dot_general(scores, d_o, tn, preferred_element_type=F32)
                       + jnp.dot(kw, ds1b, preferred_element_type=F32))
        dg_ref[0] += ds1 * s_n
        dst[...] = g_ref[0] * ds1 + lax.dot_general(qw, d_o, tn, preferred_element_type=F32)
        dq_ref[...] = _rope_t(dq_r, cos_, sin_) * q_scale
        dk_ref[...] = _rope_t(dk_r, cos_, sin_)

    tab = lambda w: pl.BlockSpec((1, ch, w), lambda h, rr: (h, 0, 0))
    return pl.pallas_call(
        body, name=name, grid=(heads, nch),
        in_specs=[pl.BlockSpec((ch, dk), lambda h, rr: (cof(rr), qb + h)),
                  pl.BlockSpec((ch, dk), lambda h, rr: (cof(rr), kb + h)),
                  pl.BlockSpec((ch, dv), lambda h, rr: (cof(rr), vb + h)),
                  pl.BlockSpec((ch, dk), lambda h, rr: (cof(rr), 0)),
                  pl.BlockSpec((ch, dk), lambda h, rr: (cof(rr), 0)),
                  tab(ch), tab(dk), tab(dk), tab(dv),
                  pl.BlockSpec((1, 1, dk, dv), lambda h, rr: (h, nch - 1 - rr, 0, 0)),
                  pl.BlockSpec((ch, dv), lambda h, rr: (jnp.maximum(cof(rr) - ncc, 0), h))],
        out_specs=[pl.BlockSpec((ch, dk), lambda h, rr: (cof(rr), h)),
                   pl.BlockSpec((ch, dk), lambda h, rr: (cof(rr), h)),
                   pl.BlockSpec((ch, dv), lambda h, rr: (cof(rr), h)),
                   tab(ch), tab(dk), tab(dk), tab(dv)],
        out_shape=[_sds((r, heads * dk), F32), _sds((r, heads * dk), F32), _sds((r, heads * dv), F32),
                   _sds(decay.shape, F32), _sds(wend.shape, F32), _sds(win.shape, F32), _sds(gch.shape, F32)],
        scratch_shapes=[pltpu.VMEM((dk, dv), F32)],
        compiler_params=_params(("parallel", "arbitrary")))(hm, hm, hm, cos, sin, decay, wend, win, gch, s_in, do)


_HBM = pl.BlockSpec(memory_space=pltpu.HBM)
_MESH = pl.DeviceIdType.MESH
ALL_GATHER_COLLECTIVE_ID = 1
SIBLING_COLLECTIVE_ID = 2
CHIPS_COLLECTIVE_ID = 3


def _axis_slice(ref, axis, start, size):
    idx = [slice(None)] * len(ref.shape)
    idx[axis] = pl.ds(start, size)
    return ref.at[tuple(idx)]


def _sibling_and_chip_peers():
    x, y, c = lax.axis_index("x"), lax.axis_index("y"), lax.axis_index("c")
    return [(x, y, 1 - c), (1 - x, y, c), (x, 1 - y, c), (1 - x, 1 - y, c)]


def _launch_exchange(body, name, operand, out_shape, sems, peers_fn, collective_id, on_sequencer):
    if not on_sequencer:
        return pl.pallas_call(body, name=name, out_shape=out_shape, in_specs=[_HBM], out_specs=_HBM,
                              scratch_shapes=sems)(operand)

    def sequencer_body(in_ref, out_ref, *sem_refs):
        peers = peers_fn()
        barrier = pltpu.get_barrier_semaphore()
        for peer in peers:
            pl.semaphore_signal(barrier, inc=1, device_id=peer, device_id_type=_MESH)
        pl.semaphore_wait(barrier, len(peers))
        body(in_ref, out_ref, *sem_refs)

    return pl.kernel(sequencer_body, out_type=out_shape, name=name,
                     mesh=plsc.ScalarSubcoreMesh(axis_name="sequencer", num_cores=1), scratch_types=sems,
                     compiler_params=pltpu.CompilerParams(collective_id=collective_id))(operand)


def _all_gather(shard, axis, name, on_sequencer=False):
    m = shard.shape[axis]
    out_shape = list(shard.shape)
    out_shape[axis] = N_DEV * m

    def body(x_ref, out_ref, send_sems, recv_sems, local_sem):
        x, y, c = lax.axis_index("x"), lax.axis_index("y"), lax.axis_index("c")
        me, sibling = (x, y, c), (x, y, 1 - c)
        chips = [(1 - x, y), (x, 1 - y), (1 - x, 1 - y)]

        def block(px, py, pc):
            return _axis_slice(out_ref, axis, (4 * px + 2 * py + pc) * m, m)

        def copy(k, blk, to, src=None):
            return pltpu.make_async_remote_copy(
                src_ref=block(*blk) if src is None else src, dst_ref=block(*blk), send_sem=send_sems.at[k],
                recv_sem=recv_sems.at[k], device_id=to, device_id_type=_MESH)

        mine = pltpu.make_async_copy(x_ref, block(*me), local_sem)
        mine.start()
        first = [copy(0, me, sibling, src=x_ref)]
        first += [copy(1 + j, me, (*chip, c), src=x_ref) for j, chip in enumerate(chips)]
        for cp in first:
            cp.start()
        passed = [copy(4 + j, (*chip, c), sibling) for j, chip in enumerate(chips)]
        for j, chip in enumerate(chips):
            copy(1 + j, (*chip, c), me).wait_recv()
            passed[j].start()
        copy(0, sibling, me).wait_recv()
        for j, chip in enumerate(chips):
            copy(4 + j, (*chip, 1 - c), me).wait_recv()
        for cp in first + passed:
            cp.wait_send()
        mine.wait()

    return _launch_exchange(
        body, name, shard, _sds(out_shape, shard.dtype),
        [pltpu.SemaphoreType.DMA((7,)), pltpu.SemaphoreType.DMA((7,)), pltpu.SemaphoreType.DMA(())],
        _sibling_and_chip_peers, ALL_GATHER_COLLECTIVE_ID, on_sequencer)


def _rs_sibling(g, axis, name, on_sequencer=False):
    m = g.shape[axis] // N_DEV
    blk_shape = list(g.shape)
    blk_shape[axis] = m
    n_chips = N_DEV // 2

    def body(g_ref, recv_ref, send_sems, recv_sems):
        x, y, c = lax.axis_index("x"), lax.axis_index("y"), lax.axis_index("c")
        sibling = (x, y, 1 - c)
        send = [pltpu.make_async_remote_copy(
            src_ref=_axis_slice(g_ref, axis, (2 * q + 1 - c) * m, m), dst_ref=recv_ref.at[q],
            send_sem=send_sems.at[q], recv_sem=recv_sems.at[q], device_id=sibling, device_id_type=_MESH)
            for q in range(n_chips)]
        for cp in send:
            cp.start()
        for cp in send:
            cp.wait_recv()
        for cp in send:
            cp.wait_send()

    return _launch_exchange(
        body, name, g, _sds([n_chips] + blk_shape, g.dtype),
        [pltpu.SemaphoreType.DMA((n_chips,)), pltpu.SemaphoreType.DMA((n_chips,))],
        lambda: _sibling_and_chip_peers()[:1], SIBLING_COLLECTIVE_ID, on_sequencer)


def _rs_chips(p, name, on_sequencer=False):
    n_peers = p.shape[0] - 1

    def body(p_ref, out_ref, send_sems, recv_sems):
        x, y, c = lax.axis_index("x"), lax.axis_index("y"), lax.axis_index("c")
        chips = [(1 - x, y), (x, 1 - y), (1 - x, 1 - y)]
        send = [pltpu.make_async_remote_copy(
            src_ref=p_ref.at[2 * cx + cy], dst_ref=out_ref.at[j], send_sem=send_sems.at[j],
            recv_sem=recv_sems.at[j], device_id=(cx, cy, c), device_id_type=_MESH)
            for j, (cx, cy) in enumerate(chips)]
        for cp in send:
            cp.start()
        for cp in send:
            cp.wait_recv()
        for cp in send:
            cp.wait_send()

    return _launch_exchange(
        body, name, p, _sds((n_peers,) + p.shape[1:], p.dtype),
        [pltpu.SemaphoreType.DMA((n_peers,)), pltpu.SemaphoreType.DMA((n_peers,))],
        lambda: _sibling_and_chip_peers()[1:], CHIPS_COLLECTIVE_ID, on_sequencer)


def _reduce_scatter(g, axis, name):
    sib = _rs_sibling(g, axis, name + "_d2d", on_sequencer=True)
    p = _pair_sum(g, sib, axis, name + "_pair")
    return p, _rs_chips(p, name + "_ici", on_sequencer=True)


def _s5_tables(lam_re, lam_im, log_step, b_re, b_im, c_re, c_im):
    g, p, cg = b_re.shape
    step = jnp.exp(log_step)[:, None]
    mag = jnp.exp(lam_re * step)
    a_re, a_im = mag * jnp.cos(lam_im * step), mag * jnp.sin(lam_im * step)
    den = lam_re * lam_re + lam_im * lam_im
    num_re, num_im = a_re - 1.0, a_im
    k_re = (num_re * lam_re + num_im * lam_im) / den
    k_im = (num_im * lam_re - num_re * lam_im) / den
    bb_re = k_re[..., None] * b_re - k_im[..., None] * b_im
    bb_im = k_re[..., None] * b_im + k_im[..., None] * b_re
    gt = g // SSM_TILE_GROUPS
    hg = SSM_HALF_GROUPS
    eye = jnp.eye(SSM_TILE_GROUPS, dtype=F32).reshape(SSM_TILE_GROUPS, 2, hg)

    def pack_b(bb):
        w = jnp.einsum("jhqpc,ghq->jhgcqp", bb.reshape(gt, 2, hg, p, cg), eye)
        return w.reshape(gt * 2, SSM_TILE_GROUPS * cg, hg * p)

    def pack_c(cc):
        w = jnp.einsum("jhqcp,ghq->jhqpgc", cc.reshape(gt, 2, hg, cg, p), eye)
        return w.reshape(gt * 2, hg * p, SSM_TILE_GROUPS * cg)

    a = jnp.stack([a_re.reshape(gt * 2, hg * p), a_im.reshape(gt * 2, hg * p)], axis=1)
    return pack_b(bb_re), pack_b(bb_im), pack_c(c_re), pack_c(c_im), a


def _ret_tables(decay_logit, rev, dk, dv):
    ch = RET_CHUNK
    h = decay_logit.shape[0]
    lg = jax.nn.log_sigmoid(decay_logit)[:, None]
    pos = jnp.arange(ch, dtype=F32)
    diff = pos[:, None] - pos[None, :]
    if rev:
        diff = -diff
        mask = diff > 0
        w_end = jnp.exp(lg * pos)
        w_in = jnp.exp(lg * (ch - pos))
    else:
        mask = diff >= 0
        w_end = jnp.exp(lg * (ch - 1.0 - pos))
        w_in = jnp.exp(lg * (pos + 1.0))
    decay = jnp.where(mask, jnp.exp(lg[:, :, None] * jnp.where(mask, diff, 0.0)), 0.0)
    g_chunk = jnp.exp(lg[:, 0] * ch)
    return (decay, jnp.broadcast_to(w_end[:, :, None], (h, ch, dk)), jnp.broadcast_to(w_in[:, :, None], (h, ch, dk)),
            jnp.broadcast_to(g_chunk[:, None, None], (h, dk, dv)))


def _rope_tables(t_rows, ncc, dk):
    quarter = dk // 4
    idx = np.arange(t_rows)
    row, col = idx // GRID_W, idx % GRID_W
    inv = ROPE_BASE ** (-np.arange(quarter, dtype=np.float32) / quarter)
    ang_r = row.astype(np.float32)[:, None] * inv
    ang_c = col.astype(np.float32)[:, None] * inv
    ang_r, ang_c = jnp.asarray(ang_r, F32), jnp.asarray(ang_c, F32)
    cos = jnp.concatenate([jnp.cos(ang_r), jnp.cos(ang_r), jnp.cos(ang_c), jnp.cos(ang_c)], axis=1)
    sin = jnp.concatenate([-jnp.sin(ang_r), jnp.sin(ang_r), -jnp.sin(ang_c), jnp.sin(ang_c)], axis=1)
    n_ctx = ncc * RET_CHUNK
    cos = jnp.concatenate([jnp.ones((n_ctx, dk), F32), cos], axis=0)
    sin = jnp.concatenate([jnp.zeros((n_ctx, dk), F32), sin], axis=0)
    return cos, sin


def _to_scan_layout(ctx_rows, lat_rows, rev):
    u = jnp.concatenate([lat_rows, ctx_rows] if rev else [ctx_rows, lat_rows], axis=0)
    r, w = u.shape
    return u.reshape(N_DEV, r // N_DEV, w).transpose(1, 0, 2).reshape(r, w)


def _from_scan_layout(yp, n_ctx, rev):
    r, w = yp.shape
    y = yp.reshape(r // N_DEV, N_DEV, w).transpose(1, 0, 2).reshape(r, w)
    return (y[r - n_ctx:], y[:r - n_ctx]) if rev else (y[:n_ctx], y[n_ctx:])


def _pack(parts, width):
    rows = []
    for p in parts:
        flat = p.reshape(-1).astype(F32)
        n = flat.shape[0]
        rows.append(jnp.pad(flat, (0, -n % (SUBLANE * width))).reshape(-1, width))
    return jnp.concatenate(rows, axis=0)


def _packed_rows(n, width):
    return -(-n // (SUBLANE * width)) * SUBLANE


def _unpack(flat2d, shapes):
    width = flat2d.shape[1]
    out, row = [], 0
    for shp in shapes:
        n = int(np.prod(shp))
        nr = _packed_rows(n, width)
        out.append(flat2d[row:row + nr].reshape(-1)[:n].reshape(shp))
        row += nr
    return out


def kernel(x, c, ctx, c_ctx, ada_w, ada_b, norm_g, ffn_w_in, ffn_w_out, mix_w_in, ssm_lam_re, ssm_lam_im, ssm_log_step, ssm_b_re, ssm_b_im, ssm_c_re, ssm_c_im, ssm_d, ssm_glu_w, ret_decay_logit, ret_w_proj, mix_w_out, loss_target, m_c_ctx, m_ada_w, m_ada_b, m_norm_g, m_ffn_w_in, m_ffn_w_out, m_mix_w_in, m_ssm_lam_re, m_ssm_lam_im, m_ssm_log_step, m_ssm_b_re, m_ssm_b_im, m_ssm_c_re, m_ssm_c_im, m_ssm_d, m_ssm_glu_w, m_ret_decay_logit, m_ret_w_proj, m_mix_w_out, v_c_ctx, v_ada_w, v_ada_b, v_norm_g, v_ffn_w_in, v_ffn_w_out, v_mix_w_in, v_ssm_lam_re, v_ssm_lam_im, v_ssm_log_step, v_ssm_b_re, v_ssm_b_im, v_ssm_c_re, v_ssm_c_im, v_ssm_d, v_ssm_glu_w, v_ret_decay_logit, v_ret_w_proj, v_mix_w_out):
    t_rows, d = x.shape[1], x.shape[2]
    n_ctx = ctx.shape[1]
    r = n_ctx + t_rows
    ssm_w = ssm_d.shape[1]
    heads = ret_decay_logit.shape[2]
    mi = mix_w_in.shape[2] * N_DEV
    dk = (mi - ssm_w - 2 * d) // (6 * heads)
    dv = 2 * dk
    qk_w, v_w = heads * dk, heads * dv
    q_off = ssm_w
    ncc = n_ctx // RET_CHUNK
    tile = n_ctx
    nct = 1
    wide_tile = _tile(n_ctx, 128, 16)
    assert r % (N_DEV * SUBLANE) == 0 and n_ctx % RET_CHUNK == 0 and t_rows % tile == 0
    me = 4 * lax.axis_index("x") + 2 * lax.axis_index("y") + lax.axis_index("c")
    g_off = ssm_w + 2 * qk_w + v_w
    gs_off = g_off + v_w

    bf = lambda w: w.astype(BF16)
    w_in1 = _all_gather(bf(ffn_w_in[0, 0]), 1, "ag_ffn1_in", on_sequencer=True)
    w_out1 = _all_gather(bf(ffn_w_out[0, 0]), 0, "ag_ffn1_out", on_sequencer=True)
    w_mix = _all_gather(bf(mix_w_in[0]), 1, "ag_mix_in", on_sequencer=True)
    w_glu = _all_gather(bf(ssm_glu_w[0]), 1, "ag_glu", on_sequencer=True)
    w_rp = _all_gather(bf(ret_w_proj[0]), 0, "ag_ret_proj", on_sequencer=True)
    w_mo = _all_gather(bf(mix_w_out[0]), 0, "ag_mix_out", on_sequencer=True)
    w_in2 = _all_gather(bf(ffn_w_in[0, 1]), 1, "ag_ffn2_in", on_sequencer=True)
    w_out2 = _all_gather(bf(ffn_w_out[0, 1]), 0, "ag_ffn2_out", on_sequencer=True)

    ng_cols = norm_g.shape[2]
    small0 = _pack([c[0], norm_g[0]], d)
    small0_all = _all_gather(small0, 0, "ag_cond").reshape(N_DEV, -1)
    ng_at = _packed_rows(d, d) * d
    c_all = small0_all[:, :d]
    g_full = small0_all[:, ng_at:ng_at + 6 * ng_cols].reshape(N_DEV, 6, ng_cols).transpose(1, 0, 2).reshape(6, d)
    g6 = g_full.reshape(6, 1, d)
    cc = jnp.concatenate([c_all, c_ctx[None, :], jnp.zeros((2 * SUBLANE - N_DEV - 1, d), F32)], axis=0)
    sc = _silu_rows(cc, "ada_silu")
    na = ada_w.shape[2]
    a_loc = _mm(sc, ada_w[0], "nn", F32, "ada_fwd", tm=16, tn=na, tk=512)
    a_all = _all_gather(a_loc, 0, "ag_ada").reshape(N_DEV, 2 * SUBLANE, na)
    ada_x = lax.dynamic_index_in_dim(a_all, me, axis=1, keepdims=False).reshape(9 * d) + ada_b[0]
    ada_c = a_all[:, N_DEV, :].reshape(9 * d) + ada_b[0]
    mods = jnp.stack([ada_c.reshape(9, d), ada_x.reshape(9, d)]).reshape(18, 1, d)

    xin = jnp.concatenate([ctx[0], x[0]], axis=0)
    u1 = _ada_pre_fwd(xin, g6, mods, 0, 0, nct, tile, "pre1")
    h1 = _mm(u1, w_in1, "nn", F32, "ffn1_in", tm=544)
    a1 = _swiglu_fwd(h1, wide_tile, "swiglu1")
    o1 = _mm(a1, w_out1, "nn", F32, "ffn1_out", tm=544, tn=d, tk=1408)
    x1 = _ada_post_fwd(xin, o1, g6, mods, 1, 0, 0.5, nct, tile, "post1")
    u2 = _ada_pre_fwd(x1, g6, mods, 2, 1, nct, tile, "pre2")
    hm = _mm(u2, w_mix, "nn", F32, "mix_in", tm=544)

    us_ctx, us_lat = hm[:n_ctx, :ssm_w], hm[n_ctx:, :ssm_w]
    dskip = ssm_d.reshape(1, 1, ssm_w)
    s5_tabs, s5_vjps, ups, y_dirs = [], [], [], []
    for dr in range(2):
        prm = (ssm_lam_re[0, dr], ssm_lam_im[0, dr], ssm_log_step[0, dr], ssm_b_re[0, dr], ssm_b_im[0, dr],
               ssm_c_re[0, dr], ssm_c_im[0, dr])
        tabs, vjp_fn = jax.vjp(_s5_tables, *prm)
        up = _to_scan_layout(us_ctx, us_lat, dr == 1)
        yp = _s5_fwd(up, *tabs, dr == 1, "s5_fwd%d" % dr)
        s5_tabs.append(tabs)
        s5_vjps.append(vjp_fn)
        ups.append(up)
        y_dirs.append(_from_scan_layout(yp, n_ctx, dr == 1)[1])
    a_ssm = _ssm_out_fwd(y_dirs[0], y_dirs[1], hm, dskip, nct, tile, "ssm_out")
    gab = _mm(a_ssm, w_glu, "nn", F32, "glu", tm=512, tn=2048, tk=ssm_w)

    cos, sin = _rope_tables(t_rows, ncc, dk)
    ret_tabs, ret_vjps, o_dirs, s_ins = [], [], [], []
    for dr in range(2):
        tabs, vjp_fn = jax.vjp(functools.partial(_ret_tables, rev=dr == 1, dk=dk, dv=dv), ret_decay_logit[0, dr])
        o_d, s_in = _ret_fwd(hm, cos, sin, *tabs, heads, dk, dv, q_off, ncc, dr == 1, "ret_fwd%d" % dr)
        ret_tabs.append(tabs)
        ret_vjps.append(vjp_fn)
        o_dirs.append(o_d)
        s_ins.append(s_in)
    ret_in = _ret_gate_fwd(o_dirs[0], o_dirs[1], hm, g_off, heads, dv, nct, tile, "ret_gate")
    rb = _mm(ret_in, w_rp, "nn", F32, "ret_proj", tm=512, tn=d, tk=v_w)
    merged = _merge_fwd(gab, rb, hm, gs_off, nct, tile, "merge")
    mix = _mm(merged, w_mo, "nn", F32, "mix_out", tm=512, tn=d, tk=d)
    x1x = x1[n_ctx:]
    x2 = _ada_post_fwd(x1x, mix, g6, mods, 3, 1, 1.0, 0, tile, "post2")
    u3 = _ada_pre_fwd(x2, g6, mods, 4, 2, 0, tile, "pre3")
    h3 = _mm(u3, w_in2, "nn", F32, "ffn2_in", tm=512)
    a3 = _swiglu_fwd(h3, wide_tile, "swiglu2")
    o3 = _mm(a3, w_out2, "nn", F32, "ffn2_out", tm=512, tn=d, tk=1408)
    x3 = _ada_post_fwd(x2, o3, g6, mods, 5, 2, 0.5, 0, tile, "post3")
    dy, lcols = _loss_grad(x3, loss_target[0], tile, "loss")
    loss = lax.psum(0.5 * jnp.sum(lcols) / d, MESH_AXES)

    dg6 = [None] * 6
    dmod = {}

    def add_mod(sel_rows, k, val):
        for sel, row in sel_rows:
            dmod[(sel, k)] = dmod.get((sel, k), 0.0) + val[row, 0]

    both, lat = [(0, 0), (1, 1)], [(1, 0)]

    def after(val, dep):
        return lax.optimization_barrier((val, dep))[0]

    do3, dg6[5], dgt = _ada_post_bwd(dy, o3, g6, mods, 5, 2, 0.5, 0, 1, tile, "post3_bwd")
    add_mod(lat, 8, dgt)
    gw_out2 = _mm(a3, do3, "tn", BF16, "ffn2_out_dw", tm=1408, tn=1024, tk=2176)
    do3 = after(do3, gw_out2)
    da3 = _mm(do3, w_out2, "nt", F32, "ffn2_out_dx", tm=512, tn=1408, tk=d)
    dh3 = _swiglu_bwd(h3, da3, wide_tile, "swiglu2_bwd")
    gw_in2 = _mm(u3, dh3, "tn", BF16, "ffn2_in_dw", tm=1024, tn=1408, tk=2176)
    dh3 = after(dh3, gw_in2)
    du3 = _mm(dh3, w_in2, "nt", F32, "ffn2_in_dx", tm=512, tn=d, tk=1408)
    dx2, dg6[4], dsh, dsc = _ada_pre_bwd(x2, du3, dy, g6, mods, 4, 2, 0, 1, tile, "pre3_bwd")
    add_mod(lat, 6, dsh)
    add_mod(lat, 7, dsc)
    dmix, dg6[3], dgt = _ada_post_bwd(dx2, mix, g6, mods, 3, 1, 1.0, 0, 1, tile, "post2_bwd")
    add_mod(lat, 5, dgt)
    gw_mo = _mm(merged, dmix, "tn", BF16, "mix_out_dw", tm=1024, tn=1024, tk=2176)
    dmix = after(dmix, gw_mo)
    dmerged = _mm(dmix, w_mo, "nt", F32, "mix_out_dx", tm=512, tn=d, tk=d)
    dgab, drb, dgs, dgr = _merge_bwd(gab, rb, hm, gs_off, dmerged, nct, tile, "merge_bwd")
    gw_glu = _mm(a_ssm, dgab, "tn", BF16, "glu_dw", tm=1024, tn=1024, tk=2176)
    gw_rp = _mm(ret_in, drb, "tn", BF16, "ret_proj_dw", tm=1024, tn=1024, tk=2176)
    dgab, drb = after(dgab, gw_glu), after(drb, gw_rp)
    da_ssm = _mm(dgab, w_glu, "nt", F32, "glu_dx", tm=512, tn=ssm_w, tk=2 * d)
    dret_in = _mm(drb, w_rp, "nt", F32, "ret_proj_dx", tm=512, tn=v_w, tk=d)
    d_o, dg_gate = _ret_gate_bwd(o_dirs[0], o_dirs[1], hm, g_off, dret_in, heads, dv, nct, tile, "ret_gate_bwd")
    dy_ssm, dus_direct, d_dskip = _ssm_out_bwd(y_dirs[0], y_dirs[1], hm, dskip, da_ssm, nct, tile, "ssm_out_bwd")
    dqkv, g_decay = [], []
    for dr in range(2):
        outs = _ret_bwd(hm, cos, sin, *ret_tabs[dr], s_ins[dr], d_o, heads, dk, dv, q_off, ncc, dr == 1,
                        "ret_bwd%d" % dr)
        dqkv.append(outs[:3])
        (gl,) = ret_vjps[dr](tuple(outs[3:]))
        g_decay.append(gl)
    g_s5, du_ctx, du_lat = [], [], [dus_direct]
    for dr in range(2):
        dyp = _to_scan_layout(jnp.zeros((n_ctx, ssm_w), F32), dy_ssm, dr == 1)
        outs = _s5_bwd(ups[dr], dyp, *s5_tabs[dr], dr == 1, "s5_bwd%d" % dr)
        part_ctx, part_lat = _from_scan_layout(outs[0], n_ctx, dr == 1)
        du_ctx.append(part_ctx)
        du_lat.append(part_lat)
        g_s5.append(s5_vjps[dr](tuple(outs[1:])))
    dus = jnp.concatenate([du_ctx[0] + du_ctx[1], du_lat[0] + du_lat[1] + du_lat[2]], axis=0)
    dhm = _assemble_dhm(dus, dqkv[0][0], dqkv[1][0], dqkv[0][1], dqkv[1][1], dqkv[0][2], dqkv[1][2],
                        dg_gate, dgs, dgr, n_ctx // wide_tile, wide_tile, "assemble_dhm")
    gw_mix = _mm(u2, dhm, "tn", BF16, "mix_in_dw", tm=1024, tn=1408, tk=2176)
    dhm = after(dhm, gw_mix)
    du2 = _mm(dhm, w_mix, "nt", F32, "mix_in_dx", tm=544, tn=d, tk=1408)
    dx1, dg6[2], dsh, dsc = _ada_pre_bwd(x1, du2, dx2, g6, mods, 2, 1, nct, 2, tile, "pre2_bwd", dres_x_only=True)
    add_mod(both, 3, dsh)
    add_mod(both, 4, dsc)
    do1, dg6[1], dgt = _ada_post_bwd(dx1, o1, g6, mods, 1, 0, 0.5, nct, 2, tile, "post1_bwd")
    add_mod(both, 2, dgt)
    gw_out1 = _mm(a1, do1, "tn", BF16, "ffn1_out_dw", tm=1408, tn=1024, tk=2176)
    do1 = after(do1, gw_out1)
    da1 = _mm(do1, w_out1, "nt", F32, "ffn1_out_dx", tm=544, tn=1408, tk=d)
    dh1 = _swiglu_bwd(h1, da1, wide_tile, "swiglu1_bwd")
    gw_in1 = _mm(u1, dh1, "tn", BF16, "ffn1_in_dw", tm=1024, tn=1408, tk=2176)
    dh1 = after(dh1, gw_in1)
    du1 = _mm(dh1, w_in1, "nt", F32, "ffn1_in_dx", tm=544, tn=d, tk=1408)
    dxin, dg6[0], dsh, dsc = _ada_pre_bwd(xin, du1, dx1, g6, mods, 0, 0, nct, 2, tile, "pre1_bwd")
    add_mod(both, 0, dsh)
    add_mod(both, 1, dsc)
    grad_x = dxin[n_ctx:][None]

    zero_d = jnp.zeros((d,), F32)
    d_ada_x = jnp.stack([dmod.get((1, k), zero_d) for k in range(9)]).reshape(9 * d)
    d_ada_c = jnp.stack([dmod.get((0, k), zero_d) for k in range(9)]).reshape(9 * d)
    dg_full = jnp.stack([g[0, 0] for g in dg6])
    s5_names = 7
    s5_stack = [jnp.stack([g_s5[0][i], g_s5[1][i]]) for i in range(s5_names)]
    small_parts = [d_ada_x, d_ada_c, dg_full] + s5_stack + [d_dskip, jnp.stack(g_decay)]
    small_shapes = [p.shape for p in small_parts]
    packed = _pack(small_parts, 1024)
    gathered = _all_gather(packed, 0, "ag_small_grads").reshape(N_DEV, -1, 1024)
    summed = _sum_leading(gathered, "sum_small_grads")
    sums = _unpack(summed, small_shapes)
    sum_dx, sum_dc, sum_dg = sums[0], sums[1], sums[2]
    grad_ada_b = (sum_dx + sum_dc)[None]
    dx_rows = gathered.reshape(N_DEV, -1)[:, :9 * d]
    col0 = me * na
    da_rows = jnp.concatenate([lax.dynamic_slice_in_dim(dx_rows, col0, na, axis=1),
                               lax.dynamic_slice_in_dim(sum_dc[None], col0, na, axis=1),
                               jnp.zeros((2 * SUBLANE - N_DEV - 1, na), F32)], axis=0)
    grad_ada_w = _mm(sc, da_rows, "tn", F32, "ada_dw", tm=512, tn=na, tk=16)
    d_sc = _mm(da_rows, ada_w[0], "nt", F32, "ada_dx", tm=16, tn=512, tk=na)
    d_sc_all = _all_gather(jnp.broadcast_to(d_sc[N_DEV:N_DEV + 1], (SUBLANE, d)), 0, "ag_dctx")
    d_sc_sum = _sum_leading(d_sc_all.reshape(N_DEV, SUBLANE, d), "sum_dctx")
    grad_c_ctx = _silu_grad_rows(jnp.broadcast_to(c_ctx[None], (SUBLANE, d)), d_sc_sum, "ctx_silu_bwd")[0]
    grad_norm_g = lax.dynamic_slice_in_dim(sum_dg, me * ng_cols, ng_cols, axis=1)[None]

    def big_update(w2d, m2d, v2d, gfull, axis, name):
        p, recv = _reduce_scatter(gfull, axis, "rs_" + name)
        return _adamw_scattered(w2d, m2d, v2d, p, recv, "adamw_" + name)

    upd = {}
    res_in = [big_update(ffn_w_in[0, l], m_ffn_w_in[0, l], v_ffn_w_in[0, l], gw, 1, "ffn%d_in" % (l + 1))
              for l, gw in enumerate([gw_in1, gw_in2])]
    upd["ffn_w_in"] = [jnp.stack([res_in[0][i], res_in[1][i]])[None] for i in range(4)]
    res_out = [big_update(ffn_w_out[0, l], m_ffn_w_out[0, l], v_ffn_w_out[0, l], gw, 0, "ffn%d_out" % (l + 1))
               for l, gw in enumerate([gw_out1, gw_out2])]
    upd["ffn_w_out"] = [jnp.stack([res_out[0][i], res_out[1][i]])[None] for i in range(4)]
    upd["mix_w_in"] = [o[None] for o in big_update(mix_w_in[0], m_mix_w_in[0], v_mix_w_in[0], gw_mix, 1, "mix_in")]
    upd["ssm_glu_w"] = [o[None] for o in big_update(ssm_glu_w[0], m_ssm_glu_w[0], v_ssm_glu_w[0], gw_glu, 1, "glu")]
    upd["ret_w_proj"] = [o[None] for o in big_update(ret_w_proj[0], m_ret_w_proj[0], v_ret_w_proj[0], gw_rp, 0,
                                                     "ret_proj")]
    upd["mix_w_out"] = [o[None] for o in big_update(mix_w_out[0], m_mix_w_out[0], v_mix_w_out[0], gw_mo, 0,
                                                    "mix_out")]
    upd["ada_w"] = [o[None] for o in _adamw(ada_w[0], m_ada_w[0], v_ada_w[0], grad_ada_w[None], "adamw_ada_w")]

    small_names = ["c_ctx", "ada_b", "norm_g", "ssm_lam_re", "ssm_lam_im", "ssm_log_step", "ssm_b_re", "ssm_b_im",
                   "ssm_c_re", "ssm_c_im", "ssm_d", "ret_decay_logit"]
    small_w = [c_ctx, ada_b, norm_g, ssm_lam_re, ssm_lam_im, ssm_log_step, ssm_b_re, ssm_b_im, ssm_c_re, ssm_c_im,
               ssm_d, ret_decay_logit]
    small_m = [m_c_ctx, m_ada_b, m_norm_g, m_ssm_lam_re, m_ssm_lam_im, m_ssm_log_step, m_ssm_b_re, m_ssm_b_im,
               m_ssm_c_re, m_ssm_c_im, m_ssm_d, m_ret_decay_logit]
    small_v = [v_c_ctx, v_ada_b, v_norm_g, v_ssm_lam_re, v_ssm_lam_im, v_ssm_log_step, v_ssm_b_re, v_ssm_b_im,
               v_ssm_c_re, v_ssm_c_im, v_ssm_d, v_ret_decay_logit]
    small_g = [grad_c_ctx, grad_ada_b, grad_norm_g] + [s[None] for s in sums[3:3 + s5_names]] + \
              [sums[3 + s5_names].reshape(ssm_d.shape), sums[4 + s5_names][None]]
    shapes = [w.shape for w in small_w]
    res = _adamw(_pack(small_w, 1024), _pack(small_m, 1024), _pack(small_v, 1024), _pack(small_g, 1024)[None],
                 "adamw_small")
    small_out = [_unpack(o, shapes) for o in res]
    for i, nm in enumerate(small_names):
        upd[nm] = [small_out[kind][i] for kind in range(4)]

    order = ["c_ctx", "ada_w", "ada_b", "norm_g", "ffn_w_in", "ffn_w_out", "mix_w_in", "ssm_lam_re", "ssm_lam_im",
             "ssm_log_step", "ssm_b_re", "ssm_b_im", "ssm_c_re", "ssm_c_im", "ssm_d", "ssm_glu_w", "ret_decay_logit",
             "ret_w_proj", "mix_w_out"]
    outs = [loss, grad_x]
    for kind in range(4):
        outs += [upd[nm][kind] for nm in order]
    return tuple(outs)
```

```python
import functools
import math

import jax
import jax.numpy as jnp
import numpy as np
from jax import lax
from jax.experimental import pallas as pl
from jax.experimental.pallas import tpu as pltpu
from jax.experimental.pallas import tpu_sc as plsc

F32 = jnp.float32
BF16 = jnp.bfloat16
MXU_DTYPE = jnp.bfloat16
MESH_AXES = ("x", "y", "c")
N_DEV = 8
V7X_VMEM_LIMIT_BYTES = 56 * 1024 * 1024
LANE = 128
SUBLANE = 8

GRID_W = 64
RET_CHUNK = 128
ROPE_BASE = 10000.0
NORM_EPS = 1e-6
ADAM_LR = 0.001
ADAM_B1 = 0.9
ADAM_B2 = 0.999
ADAM_EPS = 1e-08
ADAM_WD = 0.01
ADAM_STEP = 10
SSM_TILE_GROUPS = 8
SSM_HALF_GROUPS = 4


def _params(sem=None):
    return pltpu.CompilerParams(dimension_semantics=sem, vmem_limit_bytes=V7X_VMEM_LIMIT_BYTES)


def _tile(n, target, mult):
    best = None
    t = mult
    while t <= min(n, target):
        if n % t == 0:
            best = t
        t += mult
    return n if best is None else best


def _sds(shape, dtype):
    return jax.ShapeDtypeStruct(tuple(shape), dtype)


def _mm(a, b, dims, out_dtype, name, tm=512, tn=1408, tk=2048):
    if dims == "nn":
        (m, k), (k2, n) = a.shape, b.shape
    elif dims == "nt":
        (m, k), (n, k2) = a.shape, b.shape
    else:
        (k, m), (k2, n) = a.shape, b.shape
    assert k == k2, (a.shape, b.shape, dims)
    tm = _tile(m, tm, 16)
    tn = _tile(n, tn, LANE)
    tk = _tile(k, tk, LANE if dims != "tn" else 16)
    nk = k // tk
    dn = {"nn": (((1,), (0,)), ((), ())), "nt": (((1,), (1,)), ((), ())), "tn": (((0,), (0,)), ((), ()))}[dims]

    def product(a_ref, b_ref):
        return lax.dot_general(a_ref[...].astype(MXU_DTYPE), b_ref[...].astype(MXU_DTYPE), dn,
                               preferred_element_type=F32)

    def body_single(a_ref, b_ref, o_ref):
        o_ref[...] = product(a_ref, b_ref).astype(o_ref.dtype)

    def body(a_ref, b_ref, o_ref, acc_ref):
        kk = pl.program_id(2)

        @pl.when(kk == 0)
        def _():
            acc_ref[...] = product(a_ref, b_ref)

        @pl.when((kk > 0) & (kk < nk - 1))
        def _():
            acc_ref[...] += product(a_ref, b_ref)

        @pl.when(kk == nk - 1)
        def _():
            o_ref[...] = (acc_ref[...] + product(a_ref, b_ref)).astype(o_ref.dtype)

    if dims == "nn":
        a_spec = pl.BlockSpec((tm, tk), lambda j, i, kk: (i, kk))
        b_spec = pl.BlockSpec((tk, tn), lambda j, i, kk: (kk, j))
    elif dims == "nt":
        a_spec = pl.BlockSpec((tm, tk), lambda j, i, kk: (i, kk))
        b_spec = pl.BlockSpec((tn, tk), lambda j, i, kk: (j, kk))
    else:
        a_spec = pl.BlockSpec((tk, tm), lambda j, i, kk: (kk, i))
        b_spec = pl.BlockSpec((tk, tn), lambda j, i, kk: (kk, j))
    return pl.pallas_call(
        body_single if nk == 1 else body, name=name, grid=(n // tn, m // tm, nk), in_specs=[a_spec, b_spec],
        out_specs=pl.BlockSpec((tm, tn), lambda j, i, kk: (i, j)), out_shape=_sds((m, n), out_dtype),
        scratch_shapes=[] if nk == 1 else [pltpu.VMEM((tm, tn), F32)],
        compiler_params=_params(("parallel", "parallel", "arbitrary")))(a, b)


def _rows(name, body, n_tiles, ins, outs):
    in_specs = [pl.BlockSpec(blk, imap) for (_, blk, imap) in ins]
    out_specs = [pl.BlockSpec(blk, imap) for (_, _, blk, imap) in outs]
    out_shape = [_sds(shape, dt) for (shape, dt, _, _) in outs]
    res = pl.pallas_call(body, name=name, grid=(n_tiles,), in_specs=in_specs, out_specs=out_specs,
                         out_shape=out_shape, compiler_params=_params(("arbitrary",)))(*[a for (a, _, _) in ins])
    return res


def _row_in(arr, tile, width=None, col=0, x_only_offset=None):
    width = arr.shape[1] if width is None else width
    if x_only_offset is None:
        return (arr, (tile, width), lambda i: (i, col))
    return (arr, (tile, width), lambda i: (jnp.maximum(i - x_only_offset, 0), col))


def _vec_in(arr, idx_fn):
    return (arr, (1, 1, arr.shape[2]), lambda i: (idx_fn(i), 0, 0))


def _rms(h):
    return lax.rsqrt(jnp.mean(h * h, axis=-1, keepdims=True) + NORM_EPS)


def _sigmoid(z):
    return 1.0 / (1.0 + jnp.exp(-z))


def _ada_pre_fwd(h, g6, mods, gi, mi, nct, tile, name):
    r, d = h.shape
    sel = lambda i: jnp.where(i >= nct, 1, 0)

    def body(h_ref, g_ref, sh_ref, sc_ref, u_ref):
        hh = h_ref[...]
        n = hh * _rms(hh) * g_ref[0]
        u_ref[...] = (n * (1.0 + sc_ref[0]) + sh_ref[0]).astype(u_ref.dtype)

    (u,) = _rows(name, body, r // tile,
                 [_row_in(h, tile), _vec_in(g6, lambda i: gi), _vec_in(mods, lambda i: sel(i) * 9 + 3 * mi),
                  _vec_in(mods, lambda i: sel(i) * 9 + 3 * mi + 1)],
                 [((r, d), BF16, (tile, d), lambda i: (i, 0))])
    return u


def _ada_pre_bwd(h, du, dres, g6, mods, gi, mi, nct, nsel, tile, name, dres_x_only=False):
    r, d = h.shape
    sel = lambda i: jnp.where(i >= nct, 1, 0) if nsel == 2 else 0
    msel = lambda i: jnp.where(i >= nct, 1, 0)
    off = nct if dres_x_only else None

    def body(h_ref, du_ref, dr_ref, g_ref, sc_ref, dh_ref, dg_ref, dsh_ref, dsc_ref):
        i = pl.program_id(0)
        hh = h_ref[...]
        rr = _rms(hh)
        g = g_ref[0]
        hn = hh * rr
        n = hn * g
        du_ = du_ref[...].astype(F32)
        dn = du_ * (1.0 + sc_ref[0])

        @pl.when(i == 0)
        def _():
            dg_ref[...] = jnp.zeros_like(dg_ref)

        @pl.when((i == 0) | (i == nct))
        def _():
            dsh_ref[...] = jnp.zeros_like(dsh_ref)
            dsc_ref[...] = jnp.zeros_like(dsc_ref)

        dg_ref[0] += jnp.sum(dn * hn, axis=0, keepdims=True)
        dsh_ref[0] += jnp.sum(du_, axis=0, keepdims=True)
        dsc_ref[0] += jnp.sum(du_ * n, axis=0, keepdims=True)
        t = dn * g
        dh = rr * t - hn * (rr * jnp.mean(t * hn, axis=-1, keepdims=True))
        if dres_x_only:
            dh_ref[...] = dh + jnp.where(i >= nct, dr_ref[...], 0.0)
        else:
            dh_ref[...] = dh + dr_ref[...]

    dh, dg, dsh, dsc = _rows(
        name, body, r // tile,
        [_row_in(h, tile), _row_in(du, tile), _row_in(dres, tile, x_only_offset=off), _vec_in(g6, lambda i: gi),
         _vec_in(mods, lambda i: msel(i) * 9 + 3 * mi + 1)],
        [((r, d), F32, (tile, d), lambda i: (i, 0)), ((1, 1, d), F32, (1, 1, d), lambda i: (0, 0, 0)),
         ((nsel, 1, d), F32, (1, 1, d), lambda i: (sel(i), 0, 0)),
         ((nsel, 1, d), F32, (1, 1, d), lambda i: (sel(i), 0, 0))])
    return dh, dg, dsh, dsc


def _ada_post_fwd(h, o, g6, mods, gi, mi, res_w, nct, tile, name, h_x_only=False):
    r, d = o.shape
    sel = lambda i: jnp.where(i >= nct, 1, 0)

    def body(h_ref, o_ref, g_ref, gt_ref, y_ref):
        oo = o_ref[...]
        n = oo * _rms(oo) * g_ref[0]
        y_ref[...] = h_ref[...] + res_w * gt_ref[0] * n

    (y,) = _rows(name, body, r // tile,
                 [_row_in(h, tile), _row_in(o, tile), _vec_in(g6, lambda i: gi),
                  _vec_in(mods, lambda i: sel(i) * 9 + 3 * mi + 2)],
                 [((r, d), F32, (tile, d), lambda i: (i, 0))])
    return y


def _ada_post_bwd(dy, o, g6, mods, gi, mi, res_w, nct, nsel, tile, name):
    r, d = o.shape
    sel = lambda i: jnp.where(i >= nct, 1, 0) if nsel == 2 else 0
    msel = lambda i: jnp.where(i >= nct, 1, 0)

    def body(dy_ref, o_ref, g_ref, gt_ref, do_ref, dg_ref, dgt_ref):
        i = pl.program_id(0)
        oo = o_ref[...]
        rr = _rms(oo)
        g = g_ref[0]
        on = oo * rr
        dy_ = dy_ref[...] * res_w

        @pl.when(i == 0)
        def _():
            dg_ref[...] = jnp.zeros_like(dg_ref)

        @pl.when((i == 0) | (i == nct))
        def _():
            dgt_ref[...] = jnp.zeros_like(dgt_ref)

        dgt_ref[0] += jnp.sum(dy_ * (on * g), axis=0, keepdims=True)
        dn = dy_ * gt_ref[0]
        dg_ref[0] += jnp.sum(dn * on, axis=0, keepdims=True)
        t = dn * g
        do_ref[...] = (rr * t - on * (rr * jnp.mean(t * on, axis=-1, keepdims=True))).astype(do_ref.dtype)

    do, dg, dgt = _rows(
        name, body, r // tile,
        [_row_in(dy, tile), _row_in(o, tile), _vec_in(g6, lambda i: gi),
         _vec_in(mods, lambda i: msel(i) * 9 + 3 * mi + 2)],
        [((r, d), BF16, (tile, d), lambda i: (i, 0)), ((1, 1, d), F32, (1, 1, d), lambda i: (0, 0, 0)),
         ((nsel, 1, d), F32, (1, 1, d), lambda i: (sel(i), 0, 0))])
    return do, dg, dgt


def _swiglu_fwd(h, tile, name):
    r, w2 = h.shape
    f = w2 // 2

    def body(h_ref, a_ref):
        gt = h_ref[:, :f]
        up = h_ref[:, f:]
        a_ref[...] = (gt * _sigmoid(gt) * up).astype(a_ref.dtype)

    (a,) = _rows(name, body, r // tile, [_row_in(h, tile)], [((r, f), BF16, (tile, f), lambda i: (i, 0))])
    return a


def _swiglu_bwd(h, da, tile, name):
    r, w2 = h.shape
    f = w2 // 2

    def body(h_ref, da_ref, dh_ref):
        gt = h_ref[:, :f]
        up = h_ref[:, f:]
        d = da_ref[...]
        sg = _sigmoid(gt)
        dh_ref[:, :f] = (d * up * (sg * (1.0 + gt * (1.0 - sg)))).astype(dh_ref.dtype)
        dh_ref[:, f:] = (d * gt * sg).astype(dh_ref.dtype)

    (dh,) = _rows(name, body, r // tile, [_row_in(h, tile), _row_in(da, tile)],
                  [((r, w2), BF16, (tile, w2), lambda i: (i, 0))])
    return dh


def _gelu_parts(y):
    c0 = math.sqrt(2.0 / math.pi)
    inner = c0 * (y + 0.044715 * y * y * y)
    th = jnp.tanh(inner)
    return th, c0 * (1.0 + 3 * 0.044715 * y * y)


def _ssm_out_fwd(y0, y1, hm, dskip, nct, tile, name):
    t_rows, s = y0.shape

    def body(y0_ref, y1_ref, u_ref, d_ref, a_ref):
        y = y0_ref[...] + y1_ref[...] + d_ref[0] * u_ref[...]
        th, _ = _gelu_parts(y)
        a_ref[...] = (0.5 * y * (1.0 + th)).astype(a_ref.dtype)

    (a,) = _rows(name, body, t_rows // tile,
                 [_row_in(y0, tile), _row_in(y1, tile), (hm, (tile, s), lambda i: (i + nct, 0)),
                  _vec_in(dskip, lambda i: 0)],
                 [((t_rows, s), BF16, (tile, s), lambda i: (i, 0))])
    return a


def _ssm_out_bwd(y0, y1, hm, dskip, da, nct, tile, name):
    t_rows, s = y0.shape

    def body(y0_ref, y1_ref, u_ref, d_ref, da_ref, dy_ref, du_ref, dd_ref):
        i = pl.program_id(0)
        u = u_ref[...]
        y = y0_ref[...] + y1_ref[...] + d_ref[0] * u
        th, dinner = _gelu_parts(y)
        dy = da_ref[...] * (0.5 * (1.0 + th) + 0.5 * y * (1.0 - th * th) * dinner)
        dy_ref[...] = dy
        du_ref[...] = dy * d_ref[0]

        @pl.when(i == 0)
        def _():
            dd_ref[...] = jnp.zeros_like(dd_ref)

        dd_ref[0] += jnp.sum(dy * u, axis=0, keepdims=True)

    dy, du, dd = _rows(name, body, t_rows // tile,
                       [_row_in(y0, tile), _row_in(y1, tile), (hm, (tile, s), lambda i: (i + nct, 0)),
                        _vec_in(dskip, lambda i: 0), _row_in(da, tile)],
                       [((t_rows, s), F32, (tile, s), lambda i: (i, 0)), ((t_rows, s), F32, (tile, s), lambda i: (i, 0)),
                        ((1, 1, s), F32, (1, 1, s), lambda i: (0, 0, 0))])
    return dy, du, dd


def _col_pieces(arr, off, width, tile, nct, unit=None):
    pw = math.gcd(off, width if unit is None else unit)
    specs = [(arr, (tile, pw), functools.partial(lambda i, cb: (i + nct, cb), cb=off // pw + p))
             for p in range(width // pw)]
    return specs, pw


def _ret_gate_fwd(o0, o1, hm, g_off, heads, dv, nct, tile, name):
    t_rows, w = o0.shape
    g_specs, pw = _col_pieces(hm, g_off, w, tile, nct)
    ng = len(g_specs)

    def body(o0_ref, o1_ref, *refs):
        g_refs, r_ref = refs[:ng], refs[ng]
        for hd in range(heads):
            cs = slice(hd * dv, (hd + 1) * dv)
            o = o0_ref[:, cs] + o1_ref[:, cs]
            lo = (hd * dv) % pw
            g = g_refs[(hd * dv) // pw][:, lo:lo + dv]
            r_ref[:, cs] = (g * _sigmoid(g) * (o * _rms(o))).astype(r_ref.dtype)

    (ri,) = _rows(name, body, t_rows // tile, [_row_in(o0, tile), _row_in(o1, tile)] + g_specs,
                  [((t_rows, w), BF16, (tile, w), lambda i: (i, 0))])
    return ri


def _ret_gate_bwd(o0, o1, hm, g_off, dri, heads, dv, nct, tile, name):
    t_rows, w = o0.shape
    g_specs, pw = _col_pieces(hm, g_off, w, tile, nct)
    ng = len(g_specs)

    def body(o0_ref, o1_ref, d_ref, *refs):
        g_refs, do_ref, dg_ref = refs[:ng], refs[ng], refs[ng + 1]
        for hd in range(heads):
            cs = slice(hd * dv, (hd + 1) * dv)
            o = o0_ref[:, cs] + o1_ref[:, cs]
            lo = (hd * dv) % pw
            g = g_refs[(hd * dv) // pw][:, lo:lo + dv]
            d = d_ref[:, cs]
            rr = _rms(o)
            on = o * rr
            sg = _sigmoid(g)
            dg_ref[:, cs] = d * on * (sg * (1.0 + g * (1.0 - sg)))
            t = d * (g * sg)
            do_ref[:, cs] = rr * t - on * (rr * jnp.mean(t * on, axis=-1, keepdims=True))

    do, dg = _rows(name, body, t_rows // tile, [_row_in(o0, tile), _row_in(o1, tile), _row_in(dri, tile)] + g_specs,
                   [((t_rows, w), F32, (tile, w), lambda i: (i, 0)), ((t_rows, w), F32, (tile, w), lambda i: (i, 0))])
    return do, dg


def _merge_fwd(gab, rb, hm, gs_off, nct, tile, name):
    t_rows, d = rb.shape
    specs, pw = _col_pieces(hm, gs_off, 2 * d, tile, nct, unit=d)
    npc = d // pw

    def body(gab_ref, rb_ref, *refs):
        gs_refs, gr_refs, m_ref = refs[:npc], refs[npc:2 * npc], refs[2 * npc]
        for p in range(npc):
            cs = slice(p * pw, (p + 1) * pw)
            ga = gab_ref[:, cs]
            gb = gab_ref[:, d + p * pw:d + (p + 1) * pw]
            m_ref[:, cs] = (_sigmoid(gs_refs[p][...]) * (ga * _sigmoid(gb))
                            + _sigmoid(gr_refs[p][...]) * rb_ref[:, cs]).astype(m_ref.dtype)

    (mg,) = _rows(name, body, t_rows // tile, [_row_in(gab, tile), _row_in(rb, tile)] + specs,
                  [((t_rows, d), BF16, (tile, d), lambda i: (i, 0))])
    return mg


def _merge_bwd(gab, rb, hm, gs_off, dm, nct, tile, name):
    t_rows, d = rb.shape
    specs, pw = _col_pieces(hm, gs_off, 2 * d, tile, nct, unit=d)
    npc = d // pw

    def body(gab_ref, rb_ref, dm_ref, *refs):
        gs_refs, gr_refs = refs[:npc], refs[npc:2 * npc]
        dgab_ref, drb_ref, dgs_ref, dgr_ref = refs[2 * npc:]
        for p in range(npc):
            cs = slice(p * pw, (p + 1) * pw)
            cs2 = slice(d + p * pw, d + (p + 1) * pw)
            ga = gab_ref[:, cs]
            gb = gab_ref[:, cs2]
            dmm = dm_ref[:, cs]
            ss = _sigmoid(gs_refs[p][...])
            sr = _sigmoid(gr_refs[p][...])
            sb = _sigmoid(gb)
            dbr = dmm * ss
            dgab_ref[:, cs] = (dbr * sb).astype(dgab_ref.dtype)
            dgab_ref[:, cs2] = (dbr * ga * sb * (1.0 - sb)).astype(dgab_ref.dtype)
            drb_ref[:, cs] = (dmm * sr).astype(drb_ref.dtype)
            dgs_ref[:, cs] = dmm * (ga * sb) * ss * (1.0 - ss)
            dgr_ref[:, cs] = dmm * rb_ref[:, cs] * sr * (1.0 - sr)

    return _rows(name, body, t_rows // tile, [_row_in(gab, tile), _row_in(rb, tile), _row_in(dm, tile)] + specs,
                 [((t_rows, 2 * d), BF16, (tile, 2 * d), lambda i: (i, 0)), ((t_rows, d), BF16, (tile, d), lambda i: (i, 0)),
                  ((t_rows, d), F32, (tile, d), lambda i: (i, 0)), ((t_rows, d), F32, (tile, d), lambda i: (i, 0))])


def _assemble_dhm(dus, dq0, dq1, dk0, dk1, dv0, dv1, dg, dgs, dgr, nct, tile, name):
    r, s = dus.shape
    qk = dq0.shape[1]
    vw = dv0.shape[1]
    d = dgs.shape[1]
    mi = s + 2 * qk + 2 * vw + 2 * d
    c_q, c_k, c_v, c_g, c_gs, c_gr = s, s + qk, s + 2 * qk, s + 2 * qk + vw, s + 2 * qk + 2 * vw, s + 2 * qk + 2 * vw + d

    def body(dus_ref, dq0_ref, dq1_ref, dk0_ref, dk1_ref, dv0_ref, dv1_ref, dg_ref, dgs_ref, dgr_ref, o_ref):
        i = pl.program_id(0)
        lat = i >= nct
        o_ref[:, :s] = dus_ref[...].astype(o_ref.dtype)
        o_ref[:, c_q:c_k] = (dq0_ref[...] + dq1_ref[...]).astype(o_ref.dtype)
        o_ref[:, c_k:c_v] = (dk0_ref[...] + dk1_ref[...]).astype(o_ref.dtype)
        o_ref[:, c_v:c_g] = (dv0_ref[...] + dv1_ref[...]).astype(o_ref.dtype)
        o_ref[:, c_g:c_gs] = jnp.where(lat, dg_ref[...], 0.0).astype(o_ref.dtype)
        o_ref[:, c_gs:c_gr] = jnp.where(lat, dgs_ref[...], 0.0).astype(o_ref.dtype)
        o_ref[:, c_gr:] = jnp.where(lat, dgr_ref[...], 0.0).astype(o_ref.dtype)

    (out,) = _rows(name, body, r // tile,
                   [_row_in(dus, tile), _row_in(dq0, tile), _row_in(dq1, tile), _row_in(dk0, tile), _row_in(dk1, tile),
                    _row_in(dv0, tile), _row_in(dv1, tile), _row_in(dg, tile, x_only_offset=nct),
                    _row_in(dgs, tile, x_only_offset=nct), _row_in(dgr, tile, x_only_offset=nct)],
                   [((r, mi), BF16, (tile, mi), lambda i: (i, 0))])
    return out


def _loss_grad(y, target, tile, name):
    t_rows, d = y.shape

    def body(y_ref, t_ref, dy_ref, l_ref):
        i = pl.program_id(0)
        e = y_ref[...] - t_ref[...]
        dy_ref[...] = e * (1.0 / d)

        @pl.when(i == 0)
        def _():
            l_ref[...] = jnp.zeros_like(l_ref)

        l_ref[0] += jnp.sum(e * e, axis=0, keepdims=True)

    return _rows(name, body, t_rows // tile, [_row_in(y, tile), _row_in(target, tile)],
                 [((t_rows, d), F32, (tile, d), lambda i: (i, 0)), ((1, 1, d), F32, (1, 1, d), lambda i: (0, 0, 0))])


def _silu_rows(v, name):
    def body(v_ref, o_ref):
        z = v_ref[...]
        o_ref[...] = z * _sigmoid(z)

    (o,) = _rows(name, body, 1, [_row_in(v, v.shape[0])], [(v.shape, F32, v.shape, lambda i: (0, 0))])
    return o


def _silu_grad_rows(v, dv, name):
    def body(v_ref, d_ref, o_ref):
        z = v_ref[...]
        sg = _sigmoid(z)
        o_ref[...] = d_ref[...] * (sg * (1.0 + z * (1.0 - sg)))

    (o,) = _rows(name, body, 1, [_row_in(v, v.shape[0]), _row_in(dv, v.shape[0])],
                 [(v.shape, F32, v.shape, lambda i: (0, 0))])
    return o


def _sum_leading(g8, name):
    n, r, c = g8.shape
    tile = _tile(r, 256, SUBLANE)

    def body(g_ref, o_ref):
        acc = g_ref[0]
        for j in range(1, n):
            acc = acc + g_ref[j]
        o_ref[...] = acc

    (o,) = _rows(name, body, r // tile, [(g8, (n, tile, c), lambda i: (0, i, 0))],
                 [((r, c), F32, (tile, c), lambda i: (i, 0))])
    return o


def _pair_sum(g, recv, axis, name):
    n, br, bc = recv.shape
    tile = _tile(br, 256, 16)
    nrt = br // tile
    core = lax.axis_index("c").astype(jnp.int32).reshape(1)

    def body(c_ref, g_ref, r_ref, o_ref):
        o_ref[0] = (g_ref[...].astype(F32) + r_ref[0].astype(F32)).astype(o_ref.dtype)

    if axis == 1:
        g_spec = pl.BlockSpec((tile, bc), lambda q, i, c_ref: (i, 2 * q + c_ref[0]))
    else:
        g_spec = pl.BlockSpec((tile, bc), lambda q, i, c_ref: ((2 * q + c_ref[0]) * nrt + i, 0))
    slot = pl.BlockSpec((1, tile, bc), lambda q, i, c_ref: (q, i, 0))
    return pl.pallas_call(
        body, name=name, out_shape=_sds((n, br, bc), recv.dtype),
        grid_spec=pltpu.PrefetchScalarGridSpec(num_scalar_prefetch=1, grid=(n, nrt), in_specs=[g_spec, slot],
                                               out_specs=slot),
        compiler_params=_params(("arbitrary", "arbitrary")))(core, g, recv)


def _adam_math(w, m, v, g):
    c1 = 1.0 / (1.0 - ADAM_B1 ** ADAM_STEP)
    c2 = 1.0 / (1.0 - ADAM_B2 ** ADAM_STEP)
    mm = ADAM_B1 * m + (1.0 - ADAM_B1) * g
    vv = ADAM_B2 * v + (1.0 - ADAM_B2) * (g * g)
    return -ADAM_LR * ((mm * c1) / (jnp.sqrt(vv * c2) + ADAM_EPS) + ADAM_WD * w), mm, vv


def _adamw(w, m, v, gparts, name):
    r, c = w.shape
    n = gparts.shape[0]
    tile = _tile(r, 256, 16)

    def body(w_ref, m_ref, v_ref, g_ref, go_ref, d_ref, mo_ref, vo_ref):
        g = g_ref[0].astype(F32)
        for j in range(1, n):
            g = g + g_ref[j].astype(F32)
        go_ref[...] = g
        d_ref[...], mo_ref[...], vo_ref[...] = _adam_math(w_ref[...], m_ref[...], v_ref[...], g)

    rs = lambda arr: _row_in(arr, tile)
    out = ((r, c), F32, (tile, c), lambda i: (i, 0))
    return _rows(name, body, r // tile, [rs(w), rs(m), rs(v), (gparts, (n, tile, c), lambda i: (0, i, 0))],
                 [out, out, out, out])


def _adamw_scattered(w, m, v, p, recv, name):
    r, c = w.shape
    n = recv.shape[0]
    tile = _tile(r, 256, 16)
    chip = (2 * lax.axis_index("x") + lax.axis_index("y")).astype(jnp.int32).reshape(1)

    def body(q_ref, w_ref, m_ref, v_ref, p_ref, g_ref, go_ref, d_ref, mo_ref, vo_ref):
        g = p_ref[0].astype(F32)
        for j in range(n):
            g = g + g_ref[j].astype(F32)
        go_ref[...] = g
        d_ref[...], mo_ref[...], vo_ref[...] = _adam_math(w_ref[...], m_ref[...], v_ref[...], g)

    row = pl.BlockSpec((tile, c), lambda i, q_ref: (i, 0))
    out = _sds((r, c), F32)
    return pl.pallas_call(
        body, name=name, out_shape=[out, out, out, out],
        grid_spec=pltpu.PrefetchScalarGridSpec(
            num_scalar_prefetch=1, grid=(r // tile,),
            in_specs=[row, row, row, pl.BlockSpec((1, tile, c), lambda i, q_ref: (q_ref[0], i, 0)),
                      pl.BlockSpec((n, tile, c), lambda i, q_ref: (0, i, 0))],
            out_specs=[row, row, row, row]),
        compiler_params=_params(("arbitrary",)))(chip, w, m, v, p, recv)


def _cmul(ar, ai, br, bi):
    return ar * br - ai * bi, ar * bi + ai * br


def _cpow(ar, ai, n):
    pr, pi = jnp.ones_like(ar), jnp.zeros_like(ar)
    br, bi = ar, ai
    while n:
        if n & 1:
            pr, pi = _cmul(pr, pi, br, bi)
        n >>= 1
        if n:
            br, bi = _cmul(br, bi, br, bi)
    return pr, pi


def _s5_scan_into(xr_ref, xi_ref, ar1, ai1, ns, fr_ref, fi_ref, hr_ref, hi_ref, reverse):
    st = ar1.shape[1]
    ar = jnp.broadcast_to(ar1, (SUBLANE, st))
    ai = jnp.broadcast_to(ai1, (SUBLANE, st))
    zero = jnp.zeros((SUBLANE, st), F32)
    zero1 = jnp.zeros((1, st), F32)

    def slab(k):
        return pl.ds(pl.multiple_of(k * SUBLANE, SUBLANE), SUBLANE)

    def pass1(j, carry):
        hr, hi = carry
        k = ns - 1 - j if reverse else j
        nr, ni = _cmul(ar, ai, hr, hi)
        return nr + xr_ref[slab(k), :], ni + xi_ref[slab(k), :]

    fr, fi = lax.fori_loop(0, ns, pass1, (zero, zero))
    fr_ref[...] = fr
    fi_ref[...] = fi
    pr, pi = _cpow(ar1, ai1, ns)
    order = list(range(N_DEV - 1, -1, -1)) if reverse else list(range(N_DEV))
    hr_ref[order[0]:order[0] + 1, :] = zero1
    hi_ref[order[0]:order[0] + 1, :] = zero1
    for a_, b_ in zip(order[:-1], order[1:]):
        cr, ci = _cmul(pr, pi, hr_ref[a_:a_ + 1, :], hi_ref[a_:a_ + 1, :])
        hr_ref[b_:b_ + 1, :] = cr + fr_ref[a_:a_ + 1, :]
        hi_ref[b_:b_ + 1, :] = ci + fi_ref[a_:a_ + 1, :]

    def pass2(j, carry):
        hr, hi = carry
        k = ns - 1 - j if reverse else j
        nr, ni = _cmul(ar, ai, hr, hi)
        nr = nr + xr_ref[slab(k), :]
        ni = ni + xi_ref[slab(k), :]
        xr_ref[slab(k), :] = nr
        xi_ref[slab(k), :] = ni
        return nr, ni

    lax.fori_loop(0, ns, pass2, (hr_ref[...], hi_ref[...]))


def _s5_specs(r, ch, st):
    u_spec = pl.BlockSpec((r, ch), lambda j: (0, j // 2))
    w_spec = pl.BlockSpec((1, ch, st), lambda j: (j, 0, 0))
    c_spec = pl.BlockSpec((1, st, ch), lambda j: (j, 0, 0))
    a_spec = pl.BlockSpec((1, 2, st), lambda j: (j, 0, 0))
    return u_spec, w_spec, c_spec, a_spec


def _s5_fwd(up, wre, wim, cre, cim, a, rev, name):
    r, s = up.shape
    nh, ch, st = wre.shape
    ns = r // N_DEV
    u_spec, w_spec, c_spec, a_spec = _s5_specs(r, ch, st)

    def body(u_ref, wre_ref, wim_ref, cre_ref, cim_ref, a_ref, y_ref, xr, xi, fr, fi, hr, hi):
        j = pl.program_id(0)
        for rb in range(N_DEV):
            rows = slice(rb * ns, (rb + 1) * ns)
            ub = u_ref[rows, :].astype(MXU_DTYPE)
            xr[rows, :] = jnp.dot(ub, wre_ref[0].astype(MXU_DTYPE), preferred_element_type=F32)
            xi[rows, :] = jnp.dot(ub, wim_ref[0].astype(MXU_DTYPE), preferred_element_type=F32)
        _s5_scan_into(xr, xi, a_ref[0, 0:1, :], a_ref[0, 1:2, :], ns, fr, fi, hr, hi, rev)
        for rb in range(N_DEV):
            rows = slice(rb * ns, (rb + 1) * ns)
            yb = (jnp.dot(xr[rows, :].astype(MXU_DTYPE), cre_ref[0].astype(MXU_DTYPE), preferred_element_type=F32)
                  - jnp.dot(xi[rows, :].astype(MXU_DTYPE), cim_ref[0].astype(MXU_DTYPE), preferred_element_type=F32))

            @pl.when(j % 2 == 0)
            def _():
                y_ref[rows, :] = yb

            @pl.when(j % 2 == 1)
            def _():
                y_ref[rows, :] += yb

    small = pltpu.VMEM((SUBLANE, st), F32)
    return pl.pallas_call(
        body, name=name, grid=(nh,), in_specs=[u_spec, w_spec, w_spec, c_spec, c_spec, a_spec],
        out_specs=pl.BlockSpec((r, ch), lambda j: (0, j // 2)), out_shape=_sds((r, s), F32),
        scratch_shapes=[pltpu.VMEM((r, st), F32), pltpu.VMEM((r, st), F32), small, small, small, small],
        compiler_params=_params(("arbitrary",)))(up, wre, wim, cre, cim, a)


def _s5_bwd(up, dyp, wre, wim, cre, cim, a, rev, name):
    r, s = up.shape
    nh, ch, st = wre.shape
    ns = r // N_DEV
    u_spec, w_spec, c_spec, a_spec = _s5_specs(r, ch, st)
    nt = (((1,), (1,)), ((), ()))
    tn = (((0,), (0,)), ((), ()))

    def body(u_ref, dy_ref, wre_ref, wim_ref, cre_ref, cim_ref, a_ref,
             du_ref, dwre_ref, dwim_ref, dcre_ref, dcim_ref, da_ref,
             hr, hi, gr, gi, fr, fi, sr, si, er, ei):
        j = pl.program_id(0)
        wre_b = wre_ref[0].astype(MXU_DTYPE)
        wim_b = wim_ref[0].astype(MXU_DTYPE)
        cre_b = cre_ref[0].astype(MXU_DTYPE)
        cim_b = cim_ref[0].astype(MXU_DTYPE)
        for rb in range(N_DEV):
            rows = slice(rb * ns, (rb + 1) * ns)
            ub = u_ref[rows, :].astype(MXU_DTYPE)
            hr[rows, :] = jnp.dot(ub, wre_b, preferred_element_type=F32)
            hi[rows, :] = jnp.dot(ub, wim_b, preferred_element_type=F32)
        ar1, ai1 = a_ref[0, 0:1, :], a_ref[0, 1:2, :]
        _s5_scan_into(hr, hi, ar1, ai1, ns, fr, fi, sr, si, rev)
        dcre = jnp.zeros((st, ch), F32)
        dcim = jnp.zeros((st, ch), F32)
        for rb in range(N_DEV):
            rows = slice(rb * ns, (rb + 1) * ns)
            dyb = dy_ref[rows, :].astype(MXU_DTYPE)
            gr[rows, :] = lax.dot_general(dyb, cre_b, nt, preferred_element_type=F32)
            gi[rows, :] = -lax.dot_general(dyb, cim_b, nt, preferred_element_type=F32)
            dcre += lax.dot_general(hr[rows, :].astype(MXU_DTYPE), dyb, tn, preferred_element_type=F32)
            dcim -= lax.dot_general(hi[rows, :].astype(MXU_DTYPE), dyb, tn, preferred_element_type=F32)
        dcre_ref[0] = dcre
        dcim_ref[0] = dcim
        _s5_scan_into(gr, gi, ar1, -ai1, ns, fr, fi, er, ei, not rev)

        def slab(k):
            return pl.ds(pl.multiple_of(k * SUBLANE, SUBLANE), SUBLANE)

        step_back = 1 if rev else -1

        def acc_step(k, carry):
            acr, aci = carry
            g_r, g_i = gr[slab(k), :], gi[slab(k), :]
            p_r, p_i = hr[slab(k + step_back), :], hi[slab(k + step_back), :]
            return acr + g_r * p_r + g_i * p_i, aci + g_i * p_r - g_r * p_i

        edge = (ns - 1) * SUBLANE if rev else 0
        g_r, g_i = gr[edge:edge + SUBLANE, :], gi[edge:edge + SUBLANE, :]
        p_r, p_i = sr[...], si[...]
        lo, hi_k = (0, ns - 1) if rev else (1, ns)
        acr, aci = lax.fori_loop(lo, hi_k, acc_step, (g_r * p_r + g_i * p_i, g_i * p_r - g_r * p_i))
        da_ref[0, 0:1, :] = jnp.sum(acr, axis=0, keepdims=True)
        da_ref[0, 1:2, :] = jnp.sum(aci, axis=0, keepdims=True)
        dwre = jnp.zeros((ch, st), F32)
        dwim = jnp.zeros((ch, st), F32)
        for rb in range(N_DEV):
            rows = slice(rb * ns, (rb + 1) * ns)
            grb = gr[rows, :].astype(MXU_DTYPE)
            gib = gi[rows, :].astype(MXU_DTYPE)
            ub = u_ref[rows, :].astype(MXU_DTYPE)
            dub = (lax.dot_general(grb, wre_b, nt, preferred_element_type=F32)
                   + lax.dot_general(gib, wim_b, nt, preferred_element_type=F32))
            dwre += lax.dot_general(ub, grb, tn, preferred_element_type=F32)
            dwim += lax.dot_general(ub, gib, tn, preferred_element_type=F32)

            @pl.when(j % 2 == 0)
            def _():
                du_ref[rows, :] = dub

            @pl.when(j % 2 == 1)
            def _():
                du_ref[rows, :] += dub

        dwre_ref[0] = dwre
        dwim_ref[0] = dwim

    small = pltpu.VMEM((SUBLANE, st), F32)
    big = pltpu.VMEM((r, st), F32)
    return pl.pallas_call(
        body, name=name, grid=(nh,), in_specs=[u_spec, u_spec, w_spec, w_spec, c_spec, c_spec, a_spec],
        out_specs=[pl.BlockSpec((r, ch), lambda j: (0, j // 2)), w_spec, w_spec, c_spec, c_spec, a_spec],
        out_shape=[_sds((r, s), F32), _sds(wre.shape, F32), _sds(wre.shape, F32), _sds(cre.shape, F32),
                   _sds(cre.shape, F32), _sds(a.shape, F32)],
        scratch_shapes=[big, big, big, big, small, small, small, small, small, small],
        compiler_params=_params(("arbitrary",)))(up, dyp, wre, wim, cre, cim, a)


def _rope(t, cos, sin):
    quarter = t.shape[1] // 4
    lane = lax.broadcasted_iota(jnp.int32, t.shape, 1)
    first = (lane // quarter) % 2 == 0
    partner = jnp.where(first, pltpu.roll(t, t.shape[1] - quarter, 1), pltpu.roll(t, quarter, 1))
    return t * cos + partner * sin


def _rope_t(d, cos, sin):
    quarter = d.shape[1] // 4
    ds_ = d * sin
    lane = lax.broadcasted_iota(jnp.int32, d.shape, 1)
    first = (lane // quarter) % 2 == 0
    partner = jnp.where(first, pltpu.roll(ds_, d.shape[1] - quarter, 1), pltpu.roll(ds_, quarter, 1))
    return d * cos + partner


def _chunk_of_step(s, nch, ncc, rev):
    if not rev:
        return s
    return jnp.where(s < ncc, ncc - 1 - s, nch + ncc - 1 - s)


def _ret_fwd(hm, cos, sin, decay, wend, win, gch, heads, dk, dv, q_off, ncc, rev, name):
    r = hm.shape[0]
    ch = RET_CHUNK
    nch = r // ch
    t_rows = r - ncc * ch
    qb, kb, vb = q_off // dk, (q_off + heads * dk) // dk, (q_off + 2 * heads * dk) // dv
    q_scale = dk ** -0.5
    nt = (((1,), (1,)), ((), ()))
    tn = (((0,), (0,)), ((), ()))
    cof = lambda s: _chunk_of_step(s, nch, ncc, rev)

    def body(q_ref, k_ref, v_ref, cos_ref, sin_ref, dec_ref, we_ref, wi_ref, g_ref, o_ref, sin_out, st):
        s = pl.program_id(1)

        @pl.when(s == 0)
        def _():
            st[...] = jnp.zeros_like(st)

        q = _rope(q_ref[...], cos_ref[...], sin_ref[...]) * q_scale
        k = _rope(k_ref[...], cos_ref[...], sin_ref[...])
        v = v_ref[...].astype(MXU_DTYPE)
        s_cur = st[...]
        sin_out[0, 0] = s_cur
        kw = (k * we_ref[0]).astype(MXU_DTYPE)
        qw = (q * wi_ref[0]).astype(MXU_DTYPE)
        scores = lax.dot_general(q.astype(MXU_DTYPE), k.astype(MXU_DTYPE), nt, preferred_element_type=F32) * dec_ref[0]
        o_ref[...] = (jnp.dot(scores.astype(MXU_DTYPE), v, preferred_element_type=F32)
                      + jnp.dot(qw, s_cur.astype(MXU_DTYPE), preferred_element_type=F32))
        st[...] = g_ref[0] * s_cur + lax.dot_general(kw, v, tn, preferred_element_type=F32)

    tab = lambda w: pl.BlockSpec((1, ch, w), lambda h, s: (h, 0, 0))
    return pl.pallas_call(
        body, name=name, grid=(heads, nch),
        in_specs=[pl.BlockSpec((ch, dk), lambda h, s: (cof(s), qb + h)),
                  pl.BlockSpec((ch, dk), lambda h, s: (cof(s), kb + h)),
                  pl.BlockSpec((ch, dv), lambda h, s: (cof(s), vb + h)),
                  pl.BlockSpec((ch, dk), lambda h, s: (cof(s), 0)),
                  pl.BlockSpec((ch, dk), lambda h, s: (cof(s), 0)),
                  tab(ch), tab(dk), tab(dk), tab(dv)],
        out_specs=[pl.BlockSpec((ch, dv), lambda h, s: (jnp.maximum(cof(s) - ncc, 0) if not rev
                                                         else jnp.where(s < ncc, nch - ncc - 1, cof(s) - ncc), h)),
                   pl.BlockSpec((1, 1, dk, dv), lambda h, s: (h, s, 0, 0))],
        out_shape=[_sds((t_rows, heads * dv), F32), _sds((heads, nch, dk, dv), F32)],
        scratch_shapes=[pltpu.VMEM((dk, dv), F32)],
        compiler_params=_params(("parallel", "arbitrary")))(hm, hm, hm, cos, sin, decay, wend, win, gch)


def _ret_bwd(hm, cos, sin, decay, wend, win, gch, s_in, do, heads, dk, dv, q_off, ncc, rev, name):
    r = hm.shape[0]
    ch = RET_CHUNK
    nch = r // ch
    qb, kb, vb = q_off // dk, (q_off + heads * dk) // dk, (q_off + 2 * heads * dk) // dv
    q_scale = dk ** -0.5
    nt = (((1,), (1,)), ((), ()))
    tn = (((0,), (0,)), ((), ()))
    cof = lambda rr: _chunk_of_step(nch - 1 - rr, nch, ncc, rev)

    def body(q_ref, k_ref, v_ref, cos_ref, sin_ref, dec_ref, we_ref, wi_ref, g_ref, sin_ref2, do_ref,
             dq_ref, dk_ref, dv_ref, ddec_ref, dwe_ref, dwi_ref, dg_ref, dst):
        rr = pl.program_id(1)
        n = cof(rr)

        @pl.when(rr == 0)
        def _():
            dst[...] = jnp.zeros_like(dst)
            ddec_ref[...] = jnp.zeros_like(ddec_ref)
            dwe_ref[...] = jnp.zeros_like(dwe_ref)
            dwi_ref[...] = jnp.zeros_like(dwi_ref)
            dg_ref[...] = jnp.zeros_like(dg_ref)

        cos_, sin_ = cos_ref[...], sin_ref[...]
        q = _rope(q_ref[...], cos_, sin_) * q_scale
        k = _rope(k_ref[...], cos_, sin_)
        v = v_ref[...].astype(MXU_DTYPE)
        qb_, kb_ = q.astype(MXU_DTYPE), k.astype(MXU_DTYPE)
        kw = (k * we_ref[0]).astype(MXU_DTYPE)
        qw = (q * wi_ref[0]).astype(MXU_DTYPE)
        sraw = lax.dot_general(qb_, kb_, nt, preferred_element_type=F32)
        scores = (sraw * dec_ref[0]).astype(MXU_DTYPE)
        d_o = jnp.where(n >= ncc, do_ref[...], 0.0).astype(MXU_DTYPE)
        s_n = sin_ref2[0, 0]
        s_nb = s_n.astype(MXU_DTYPE)
        ds1 = dst[...]
        ds1b = ds1.astype(MXU_DTYPE)
        dsc = lax.dot_general(d_o, v, nt, preferred_element_type=F32)
        dsr = (dsc * dec_ref[0]).astype(MXU_DTYPE)
        ddec_ref[0] += dsc * sraw
        t1 = lax.dot_general(d_o, s_nb, nt, preferred_element_type=F32)
        dq_r = jnp.dot(dsr, kb_, preferred_element_type=F32) + t1 * wi_ref[0]
        dwi_ref[0] += t1 * q
        t2 = lax.dot_general(v, ds1b, nt, preferred_element_type=F32)
        dk_r = lax.dot_general(dsr, qb_, tn, preferred_element_type=F32) + t2 * we_ref[0]
        dwe_ref[0] += t2 * k
        dv_ref[...] = (lax.dot_general(scores, d_o, tn, preferred_element_type=F32)
                       + jnp.dot(kw, ds1b, preferred_element_type=F32))
        dg_ref[0] += ds1 * s_n
        dst[...] = g_ref[0] * ds1 + lax.dot_general(qw, d_o, tn, preferred_element_type=F32)
        dq_ref[...] = _rope_t(dq_r, cos_, sin_) * q_scale
        dk_ref[...] = _rope_t(dk_r, cos_, sin_)

    tab = lambda w: pl.BlockSpec((1, ch, w), lambda h, rr: (h, 0, 0))
    return pl.pallas_call(
        body, name=name, grid=(heads, nch),
        in_specs=[pl.BlockSpec((ch, dk), lambda h, rr: (cof(rr), qb + h)),
                  pl.BlockSpec((ch, dk), lambda h, rr: (cof(rr), kb + h)),
                  pl.BlockSpec((ch, dv), lambda h, rr: (cof(rr), vb + h)),
                  pl.BlockSpec((ch, dk), lambda h, rr: (cof(rr), 0)),
                  pl.BlockSpec((ch, dk), lambda h, rr: (cof(rr), 0)),
                  tab(ch), tab(dk), tab(dk), tab(dv),
                  pl.BlockSpec((1, 1, dk, dv), lambda h, rr: (h, nch - 1 - rr, 0, 0)),
                  pl.BlockSpec((ch, dv), lambda h, rr: (jnp.maximum(cof(rr) - ncc, 0), h))],
        out_specs=[pl.BlockSpec((ch, dk), lambda h, rr: (cof(rr), h)),
                   pl.BlockSpec((ch, dk), lambda h, rr: (cof(rr), h)),
                   pl.BlockSpec((ch, dv), lambda h, rr: (cof(rr), h)),
                   tab(ch), tab(dk), tab(dk), tab(dv)],
        out_shape=[_sds((r, heads * dk), F32), _sds((r, heads * dk), F32), _sds((r, heads * dv), F32),
                   _sds(decay.shape, F32), _sds(wend.shape, F32), _sds(win.shape, F32), _sds(gch.shape, F32)],
        scratch_shapes=[pltpu.VMEM((dk, dv), F32)],
        compiler_params=_params(("parallel", "arbitrary")))(hm, hm, hm, cos, sin, decay, wend, win, gch, s_in, do)


_HBM = pl.BlockSpec(memory_space=pltpu.HBM)
_MESH = pl.DeviceIdType.MESH
ALL_GATHER_COLLECTIVE_ID = 1
SIBLING_COLLECTIVE_ID = 2
CHIPS_COLLECTIVE_ID = 3


def _axis_slice(ref, axis, start, size):
    idx = [slice(None)] * len(ref.shape)
    idx[axis] = pl.ds(start, size)
    return ref.at[tuple(idx)]


def _sibling_and_chip_peers():
    x, y, c = lax.axis_index("x"), lax.axis_index("y"), lax.axis_index("c")
    return [(x, y, 1 - c), (1 - x, y, c), (x, 1 - y, c), (1 - x, 1 - y, c)]


def _launch_exchange(body, name, operand, out_shape, sems, peers_fn, collective_id, on_sequencer):
    if not on_sequencer:
        return pl.pallas_call(body, name=name, out_shape=out_shape, in_specs=[_HBM], out_specs=_HBM,
                              scratch_shapes=sems)(operand)

    def sequencer_body(in_ref, out_ref, *sem_refs):
        peers = peers_fn()
        barrier = pltpu.get_barrier_semaphore()
        for peer in peers:
            pl.semaphore_signal(barrier, inc=1, device_id=peer, device_id_type=_MESH)
        pl.semaphore_wait(barrier, len(peers))
        body(in_ref, out_ref, *sem_refs)

    return pl.kernel(sequencer_body, out_type=out_shape, name=name,
                     mesh=plsc.ScalarSubcoreMesh(axis_name="sequencer", num_cores=1), scratch_types=sems,
                     compiler_params=pltpu.CompilerParams(collective_id=collective_id))(operand)


def _all_gather(shard, axis, name, on_sequencer=False):
    m = shard.shape[axis]
    out_shape = list(shard.shape)
    out_shape[axis] = N_DEV * m

    def body(x_ref, out_ref, send_sems, recv_sems, local_sem):
        x, y, c = lax.axis_index("x"), lax.axis_index("y"), lax.axis_index("c")
        me, sibling = (x, y, c), (x, y, 1 - c)
        chips = [(1 - x, y), (x, 1 - y), (1 - x, 1 - y)]

        def block(px, py, pc):
            return _axis_slice(out_ref, axis, (4 * px + 2 * py + pc) * m, m)

        def copy(k, blk, to, src=None):
            return pltpu.make_async_remote_copy(
                src_ref=block(*blk) if src is None else src, dst_ref=block(*blk), send_sem=send_sems.at[k],
                recv_sem=recv_sems.at[k], device_id=to, device_id_type=_MESH)

        mine = pltpu.make_async_copy(x_ref, block(*me), local_sem)
        mine.start()
        first = [copy(0, me, sibling, src=x_ref)]
        first += [copy(1 + j, me, (*chip, c), src=x_ref) for j, chip in enumerate(chips)]
        for cp in first:
            cp.start()
        passed = [copy(4 + j, (*chip, c), sibling) for j, chip in enumerate(chips)]
        for j, chip in enumerate(chips):
            copy(1 + j, (*chip, c), me).wait_recv()
            passed[j].start()
        copy(0, sibling, me).wait_recv()
        for j, chip in enumerate(chips):
            copy(4 + j, (*chip, 1 - c), me).wait_recv()
        for cp in first + passed:
            cp.wait_send()
        mine.wait()

    return _launch_exchange(
        body, name, shard, _sds(out_shape, shard.dtype),
        [pltpu.SemaphoreType.DMA((7,)), pltpu.SemaphoreType.DMA((7,)), pltpu.SemaphoreType.DMA(())],
        _sibling_and_chip_peers, ALL_GATHER_COLLECTIVE_ID, on_sequencer)


def _rs_sibling(g, axis, name, on_sequencer=False):
    m = g.shape[axis] // N_DEV
    blk_shape = list(g.shape)
    blk_shape[axis] = m
    n_chips = N_DEV // 2

    def body(g_ref, recv_ref, send_sems, recv_sems):
        x, y, c = lax.axis_index("x"), lax.axis_index("y"), lax.axis_index("c")
        sibling = (x, y, 1 - c)
        send = [pltpu.make_async_remote_copy(
            src_ref=_axis_slice(g_ref, axis, (2 * q + 1 - c) * m, m), dst_ref=recv_ref.at[q],
            send_sem=send_sems.at[q], recv_sem=recv_sems.at[q], device_id=sibling, device_id_type=_MESH)
            for q in range(n_chips)]
        for cp in send:
            cp.start()
        for cp in send:
            cp.wait_recv()
        for cp in send:
            cp.wait_send()

    return _launch_exchange(
        body, name, g, _sds([n_chips] + blk_shape, g.dtype),
        [pltpu.SemaphoreType.DMA((n_chips,)), pltpu.SemaphoreType.DMA((n_chips,))],
        lambda: _sibling_and_chip_peers()[:1], SIBLING_COLLECTIVE_ID, on_sequencer)


def _rs_chips(p, name, on_sequencer=False):
    n_peers = p.shape[0] - 1

    def body(p_ref, out_ref, send_sems, recv_sems):
        x, y, c = lax.axis_index("x"), lax.axis_index("y"), lax.axis_index("c")
        chips = [(1 - x, y), (x, 1 - y), (1 - x, 1 - y)]
        send = [pltpu.make_async_remote_copy(
            src_ref=p_ref.at[2 * cx + cy], dst_ref=out_ref.at[j], send_sem=send_sems.at[j],
            recv_sem=recv_sems.at[j], device_id=(cx, cy, c), device_id_type=_MESH)
            for j, (cx, cy) in enumerate(chips)]
        for cp in send:
            cp.start()
        for cp in send:
            cp.wait_recv()
        for cp in send:
            cp.wait_send()

    return _launch_exchange(
        body, name, p, _sds((n_peers,) + p.shape[1:], p.dtype),
        [pltpu.SemaphoreType.DMA((n_peers,)), pltpu.SemaphoreType.DMA((n_peers,))],
        lambda: _sibling_and_chip_peers()[1:], CHIPS_COLLECTIVE_ID, on_sequencer)


def _reduce_scatter(g, axis, name):
    sib = _rs_sibling(g, axis, name + "_d2d", on_sequencer=True)
    p = _pair_sum(g, sib, axis, name + "_pair")
    return p, _rs_chips(p, name + "_ici", on_sequencer=True)


def _s5_tables(lam_re, lam_im, log_step, b_re, b_im, c_re, c_im):
    g, p, cg = b_re.shape
    step = jnp.exp(log_step)[:, None]
    mag = jnp.exp(lam_re * step)
    a_re, a_im = mag * jnp.cos(lam_im * step), mag * jnp.sin(lam_im * step)
    den = lam_re * lam_re + lam_im * lam_im
    num_re, num_im = a_re - 1.0, a_im
    k_re = (num_re * lam_re + num_im * lam_im) / den
    k_im = (num_im * lam_re - num_re * lam_im) / den
    bb_re = k_re[..., None] * b_re - k_im[..., None] * b_im
    bb_im = k_re[..., None] * b_im + k_im[..., None] * b_re
    gt = g // SSM_TILE_GROUPS
    hg = SSM_HALF_GROUPS
    eye = jnp.eye(SSM_TILE_GROUPS, dtype=F32).reshape(SSM_TILE_GROUPS, 2, hg)

    def pack_b(bb):
        w = jnp.einsum("jhqpc,ghq->jhgcqp", bb.reshape(gt, 2, hg, p, cg), eye)
        return w.reshape(gt * 2, SSM_TILE_GROUPS * cg, hg * p)

    def pack_c(cc):
        w = jnp.einsum("jhqcp,ghq->jhqpgc", cc.reshape(gt, 2, hg, cg, p), eye)
        return w.reshape(gt * 2, hg * p, SSM_TILE_GROUPS * cg)

    a = jnp.stack([a_re.reshape(gt * 2, hg * p), a_im.reshape(gt * 2, hg * p)], axis=1)
    return pack_b(bb_re), pack_b(bb_im), pack_c(c_re), pack_c(c_im), a


def _ret_tables(decay_logit, rev, dk, dv):
    ch = RET_CHUNK
    h = decay_logit.shape[0]
    lg = jax.nn.log_sigmoid(decay_logit)[:, None]
    pos = jnp.arange(ch, dtype=F32)
    diff = pos[:, None] - pos[None, :]
    if rev:
        diff = -diff
        mask = diff > 0
        w_end = jnp.exp(lg * pos)
        w_in = jnp.exp(lg * (ch - pos))
    else:
        mask = diff >= 0
        w_end = jnp.exp(lg * (ch - 1.0 - pos))
        w_in = jnp.exp(lg * (pos + 1.0))
    decay = jnp.where(mask, jnp.exp(lg[:, :, None] * jnp.where(mask, diff, 0.0)), 0.0)
    g_chunk = jnp.exp(lg[:, 0] * ch)
    return (decay, jnp.broadcast_to(w_end[:, :, None], (h, ch, dk)), jnp.broadcast_to(w_in[:, :, None], (h, ch, dk)),
            jnp.broadcast_to(g_chunk[:, None, None], (h, dk, dv)))


def _rope_tables(t_rows, ncc, dk):
    quarter = dk // 4
    idx = np.arange(t_rows)
    row, col = idx // GRID_W, idx % GRID_W
    inv = ROPE_BASE ** (-np.arange(quarter, dtype=np.float32) / quarter)
    ang_r = row.astype(np.float32)[:, None] * inv
    ang_c = col.astype(np.float32)[:, None] * inv
    ang_r, ang_c = jnp.asarray(ang_r, F32), jnp.asarray(ang_c, F32)
    cos = jnp.concatenate([jnp.cos(ang_r), jnp.cos(ang_r), jnp.cos(ang_c), jnp.cos(ang_c)], axis=1)
    sin = jnp.concatenate([-jnp.sin(ang_r), jnp.sin(ang_r), -jnp.sin(ang_c), jnp.sin(ang_c)], axis=1)
    n_ctx = ncc * RET_CHUNK
    cos = jnp.concatenate([jnp.ones((n_ctx, dk), F32), cos], axis=0)
    sin = jnp.concatenate([jnp.zeros((n_ctx, dk), F32), sin], axis=0)
    return cos, sin


def _to_scan_layout(ctx_rows, lat_rows, rev):
    u = jnp.concatenate([lat_rows, ctx_rows] if rev else [ctx_rows, lat_rows], axis=0)
    r, w = u.shape
    return u.reshape(N_DEV, r // N_DEV, w).transpose(1, 0, 2).reshape(r, w)


def _from_scan_layout(yp, n_ctx, rev):
    r, w = yp.shape
    y = yp.reshape(r // N_DEV, N_DEV, w).transpose(1, 0, 2).reshape(r, w)
    return (y[r - n_ctx:], y[:r - n_ctx]) if rev else (y[:n_ctx], y[n_ctx:])


def _pack(parts, width):
    rows = []
    for p in parts:
        flat = p.reshape(-1).astype(F32)
        n = flat.shape[0]
        rows.append(jnp.pad(flat, (0, -n % (SUBLANE * width))).reshape(-1, width))
    return jnp.concatenate(rows, axis=0)


def _packed_rows(n, width):
    return -(-n // (SUBLANE * width)) * SUBLANE


def _unpack(flat2d, shapes):
    width = flat2d.shape[1]
    out, row = [], 0
    for shp in shapes:
        n = int(np.prod(shp))
        nr = _packed_rows(n, width)
        out.append(flat2d[row:row + nr].reshape(-1)[:n].reshape(shp))
        row += nr
    return out


def kernel(x, c, ctx, c_ctx, ada_w, ada_b, norm_g, ffn_w_in, ffn_w_out, mix_w_in, ssm_lam_re, ssm_lam_im, ssm_log_step, ssm_b_re, ssm_b_im, ssm_c_re, ssm_c_im, ssm_d, ssm_glu_w, ret_decay_logit, ret_w_proj, mix_w_out, loss_target, m_c_ctx, m_ada_w, m_ada_b, m_norm_g, m_ffn_w_in, m_ffn_w_out, m_mix_w_in, m_ssm_lam_re, m_ssm_lam_im, m_ssm_log_step, m_ssm_b_re, m_ssm_b_im, m_ssm_c_re, m_ssm_c_im, m_ssm_d, m_ssm_glu_w, m_ret_decay_logit, m_ret_w_proj, m_mix_w_out, v_c_ctx, v_ada_w, v_ada_b, v_norm_g, v_ffn_w_in, v_ffn_w_out, v_mix_w_in, v_ssm_lam_re, v_ssm_lam_im, v_ssm_log_step, v_ssm_b_re, v_ssm_b_im, v_ssm_c_re, v_ssm_c_im, v_ssm_d, v_ssm_glu_w, v_ret_decay_logit, v_ret_w_proj, v_mix_w_out):
    t_rows, d = x.shape[1], x.shape[2]
    n_ctx = ctx.shape[1]
    r = n_ctx + t_rows
    ssm_w = ssm_d.shape[1]
    heads = ret_decay_logit.shape[2]
    mi = mix_w_in.shape[2] * N_DEV
    dk = (mi - ssm_w - 2 * d) // (6 * heads)
    dv = 2 * dk
    qk_w, v_w = heads * dk, heads * dv
    q_off = ssm_w
    ncc = n_ctx // RET_CHUNK
    tile = n_ctx
    nct = 1
    wide_tile = _tile(n_ctx, 128, 16)
    assert r % (N_DEV * SUBLANE) == 0 and n_ctx % RET_CHUNK == 0 and t_rows % tile == 0
    me = 4 * lax.axis_index("x") + 2 * lax.axis_index("y") + lax.axis_index("c")
    g_off = ssm_w + 2 * qk_w + v_w
    gs_off = g_off + v_w

    bf = lambda w: w.astype(BF16)
    w_in1 = _all_gather(bf(ffn_w_in[0, 0]), 1, "ag_ffn1_in", on_sequencer=True)
    w_out1 = _all_gather(bf(ffn_w_out[0, 0]), 0, "ag_ffn1_out", on_sequencer=True)
    w_mix = _all_gather(bf(mix_w_in[0]), 1, "ag_mix_in", on_sequencer=True)
    w_glu = _all_gather(bf(ssm_glu_w[0]), 1, "ag_glu", on_sequencer=True)
    w_rp = _all_gather(bf(ret_w_proj[0]), 0, "ag_ret_proj", on_sequencer=True)
    w_mo = _all_gather(bf(mix_w_out[0]), 0, "ag_mix_out", on_sequencer=True)
    w_in2 = _all_gather(bf(ffn_w_in[0, 1]), 1, "ag_ffn2_in", on_sequencer=True)
    w_out2 = _all_gather(bf(ffn_w_out[0, 1]), 0, "ag_ffn2_out", on_sequencer=True)

    ng_cols = norm_g.shape[2]
    small0 = _pack([c[0], norm_g[0]], d)
    small0_all = _all_gather(small0, 0, "ag_cond").reshape(N_DEV, -1)
    ng_at = _packed_rows(d, d) * d
    c_all = small0_all[:, :d]
    g_full = small0_all[:, ng_at:ng_at + 6 * ng_cols].reshape(N_DEV, 6, ng_cols).transpose(1, 0, 2).reshape(6, d)
    g6 = g_full.reshape(6, 1, d)
    cc = jnp.concatenate([c_all, c_ctx[None, :], jnp.zeros((2 * SUBLANE - N_DEV - 1, d), F32)], axis=0)
    sc = _silu_rows(cc, "ada_silu")
    na = ada_w.shape[2]
    a_loc = _mm(sc, ada_w[0], "nn", F32, "ada_fwd", tm=16, tn=na, tk=512)
    a_all = _all_gather(a_loc, 0, "ag_ada").reshape(N_DEV, 2 * SUBLANE, na)
    ada_x = lax.dynamic_index_in_dim(a_all, me, axis=1, keepdims=False).reshape(9 * d) + ada_b[0]
    ada_c = a_all[:, N_DEV, :].reshape(9 * d) + ada_b[0]
    mods = jnp.stack([ada_c.reshape(9, d), ada_x.reshape(9, d)]).reshape(18, 1, d)

    xin = jnp.concatenate([ctx[0], x[0]], axis=0)
    u1 = _ada_pre_fwd(xin, g6, mods, 0, 0, nct, tile, "pre1")
    h1 = _mm(u1, w_in1, "nn", F32, "ffn1_in", tm=544)
    a1 = _swiglu_fwd(h1, wide_tile, "swiglu1")
    o1 = _mm(a1, w_out1, "nn", F32, "ffn1_out", tm=544, tn=d, tk=1408)
    x1 = _ada_post_fwd(xin, o1, g6, mods, 1, 0, 0.5, nct, tile, "post1")
    u2 = _ada_pre_fwd(x1, g6, mods, 2, 1, nct, tile, "pre2")
    hm = _mm(u2, w_mix, "nn", F32, "mix_in", tm=544)

    us_ctx, us_lat = hm[:n_ctx, :ssm_w], hm[n_ctx:, :ssm_w]
    dskip = ssm_d.reshape(1, 1, ssm_w)
    s5_tabs, s5_vjps, ups, y_dirs = [], [], [], []
    for dr in range(2):
        prm = (ssm_lam_re[0, dr], ssm_lam_im[0, dr], ssm_log_step[0, dr], ssm_b_re[0, dr], ssm_b_im[0, dr],
               ssm_c_re[0, dr], ssm_c_im[0, dr])
        tabs, vjp_fn = jax.vjp(_s5_tables, *prm)
        up = _to_scan_layout(us_ctx, us_lat, dr == 1)
        yp = _s5_fwd(up, *tabs, dr == 1, "s5_fwd%d" % dr)
        s5_tabs.append(tabs)
        s5_vjps.append(vjp_fn)
        ups.append(up)
        y_dirs.append(_from_scan_layout(yp, n_ctx, dr == 1)[1])
    a_ssm = _ssm_out_fwd(y_dirs[0], y_dirs[1], hm, dskip, nct, tile, "ssm_out")
    gab = _mm(a_ssm, w_glu, "nn", F32, "glu", tm=512, tn=2048, tk=ssm_w)

    cos, sin = _rope_tables(t_rows, ncc, dk)
    ret_tabs, ret_vjps, o_dirs, s_ins = [], [], [], []
    for dr in range(2):
        tabs, vjp_fn = jax.vjp(functools.partial(_ret_tables, rev=dr == 1, dk=dk, dv=dv), ret_decay_logit[0, dr])
        o_d, s_in = _ret_fwd(hm, cos, sin, *tabs, heads, dk, dv, q_off, ncc, dr == 1, "ret_fwd%d" % dr)
        ret_tabs.append(tabs)
        ret_vjps.append(vjp_fn)
        o_dirs.append(o_d)
        s_ins.append(s_in)
    ret_in = _ret_gate_fwd(o_dirs[0], o_dirs[1], hm, g_off, heads, dv, nct, tile, "ret_gate")
    rb = _mm(ret_in, w_rp, "nn", F32, "ret_proj", tm=512, tn=d, tk=v_w)
    merged = _merge_fwd(gab, rb, hm, gs_off, nct, tile, "merge")
    mix = _mm(merged, w_mo, "nn", F32, "mix_out", tm=512, tn=d, tk=d)
    x1x = x1[n_ctx:]
    x2 = _ada_post_fwd(x1x, mix, g6, mods, 3, 1, 1.0, 0, tile, "post2")
    u3 = _ada_pre_fwd(x2, g6, mods, 4, 2, 0, tile, "pre3")
    h3 = _mm(u3, w_in2, "nn", F32, "ffn2_in", tm=512)
    a3 = _swiglu_fwd(h3, wide_tile, "swiglu2")
    o3 = _mm(a3, w_out2, "nn", F32, "ffn2_out", tm=512, tn=d, tk=1408)
    x3 = _ada_post_fwd(x2, o3, g6, mods, 5, 2, 0.5, 0, tile, "post3")
    dy, lcols = _loss_grad(x3, loss_target[0], tile, "loss")
    loss = lax.psum(0.5 * jnp.sum(lcols) / d, MESH_AXES)

    dg6 = [None] * 6
    dmod = {}

    def add_mod(sel_rows, k, val):
        for sel, row in sel_rows:
            dmod[(sel, k)] = dmod.get((sel, k), 0.0) + val[row, 0]

    both, lat = [(0, 0), (1, 1)], [(1, 0)]

    def after(val, dep):
        return lax.optimization_barrier((val, dep))[0]

    def big_update(w2d, m2d, v2d, gfull, axis, name):
        p, recv = _reduce_scatter(gfull, axis, "rs_" + name)
        return _adamw_scattered(w2d, m2d, v2d, p, recv, "adamw_" + name)

    do3, dg6[5], dgt = _ada_post_bwd(dy, o3, g6, mods, 5, 2, 0.5, 0, 1, tile, "post3_bwd")
    add_mod(lat, 8, dgt)
    gw_out2 = _mm(a3, do3, "tn", BF16, "ffn2_out_dw", tm=1408, tn=1024, tk=2176)
    do3 = after(do3, gw_out2)
    up_out2 = big_update(ffn_w_out[0, 1], m_ffn_w_out[0, 1], v_ffn_w_out[0, 1], gw_out2, 0, "ffn2_out")
    da3 = _mm(do3, w_out2, "nt", F32, "ffn2_out_dx", tm=512, tn=1408, tk=d)
    dh3 = _swiglu_bwd(h3, da3, wide_tile, "swiglu2_bwd")
    gw_in2 = _mm(u3, dh3, "tn", BF16, "ffn2_in_dw", tm=1024, tn=1408, tk=2176)
    dh3 = after(dh3, gw_in2)
    up_in2 = big_update(ffn_w_in[0, 1], m_ffn_w_in[0, 1], v_ffn_w_in[0, 1], gw_in2, 1, "ffn2_in")
    du3 = _mm(dh3, w_in2, "nt", F32, "ffn2_in_dx", tm=512, tn=d, tk=1408)
    dx2, dg6[4], dsh, dsc = _ada_pre_bwd(x2, du3, dy, g6, mods, 4, 2, 0, 1, tile, "pre3_bwd")
    add_mod(lat, 6, dsh)
    add_mod(lat, 7, dsc)
    dmix, dg6[3], dgt = _ada_post_bwd(dx2, mix, g6, mods, 3, 1, 1.0, 0, 1, tile, "post2_bwd")
    add_mod(lat, 5, dgt)
    gw_mo = _mm(merged, dmix, "tn", BF16, "mix_out_dw", tm=1024, tn=1024, tk=2176)
    dmix = after(dmix, gw_mo)
    up_mo = big_update(mix_w_out[0], m_mix_w_out[0], v_mix_w_out[0], gw_mo, 0, "mix_out")
    dmerged = _mm(dmix, w_mo, "nt", F32, "mix_out_dx", tm=512, tn=d, tk=d)
    dgab, drb, dgs, dgr = _merge_bwd(gab, rb, hm, gs_off, dmerged, nct, tile, "merge_bwd")
    gw_glu = _mm(a_ssm, dgab, "tn", BF16, "glu_dw", tm=1024, tn=1024, tk=2176)
    gw_rp = _mm(ret_in, drb, "tn", BF16, "ret_proj_dw", tm=1024, tn=1024, tk=2176)
    dgab, drb = after(dgab, gw_glu), after(drb, gw_rp)
    up_glu = big_update(ssm_glu_w[0], m_ssm_glu_w[0], v_ssm_glu_w[0], gw_glu, 1, "glu")
    up_rp = big_update(ret_w_proj[0], m_ret_w_proj[0], v_ret_w_proj[0], gw_rp, 0, "ret_proj")
    da_ssm = _mm(dgab, w_glu, "nt", F32, "glu_dx", tm=512, tn=ssm_w, tk=2 * d)
    dret_in = _mm(drb, w_rp, "nt", F32, "ret_proj_dx", tm=512, tn=v_w, tk=d)
    d_o, dg_gate = _ret_gate_bwd(o_dirs[0], o_dirs[1], hm, g_off, dret_in, heads, dv, nct, tile, "ret_gate_bwd")
    dy_ssm, dus_direct, d_dskip = _ssm_out_bwd(y_dirs[0], y_dirs[1], hm, dskip, da_ssm, nct, tile, "ssm_out_bwd")
    g_s5, du_ctx, du_lat = [], [], [dus_direct]
    for dr in range(2):
        dyp = _to_scan_layout(jnp.zeros((n_ctx, ssm_w), F32), dy_ssm, dr == 1)
        if dr == 1:
            dyp = after(dyp, (up_out2[1], up_in2[1]))
        outs = _s5_bwd(ups[dr], dyp, *s5_tabs[dr], dr == 1, "s5_bwd%d" % dr)
        part_ctx, part_lat = _from_scan_layout(outs[0], n_ctx, dr == 1)
        du_ctx.append(part_ctx)
        du_lat.append(part_lat)
        g_s5.append(s5_vjps[dr](tuple(outs[1:])))
    dqkv, g_decay = [], []
    for dr in range(2):
        d_o_dr = after(d_o, (up_mo[1], up_glu[1], up_rp[1])) if dr == 1 else d_o
        outs = _ret_bwd(hm, cos, sin, *ret_tabs[dr], s_ins[dr], d_o_dr, heads, dk, dv, q_off, ncc, dr == 1,
                        "ret_bwd%d" % dr)
        dqkv.append(outs[:3])
        (gl,) = ret_vjps[dr](tuple(outs[3:]))
        g_decay.append(gl)
    s5_names = 7
    s5_stack = [jnp.stack([g_s5[0][i], g_s5[1][i]]) for i in range(s5_names)]
    early_parts = s5_stack + [jnp.stack(g_decay)]
    early_shapes = [p.shape for p in early_parts]
    early_all = _all_gather(_pack(early_parts, 1024), 0, "ag_s5_grads", on_sequencer=True)
    early_sums = _unpack(_sum_leading(early_all.reshape(N_DEV, -1, 1024), "sum_s5_grads"), early_shapes)
    dus = jnp.concatenate([du_ctx[0] + du_ctx[1], du_lat[0] + du_lat[1] + du_lat[2]], axis=0)
    dhm = _assemble_dhm(dus, dqkv[0][0], dqkv[1][0], dqkv[0][1], dqkv[1][1], dqkv[0][2], dqkv[1][2],
                        dg_gate, dgs, dgr, n_ctx // wide_tile, wide_tile, "assemble_dhm")
    gw_mix = _mm(u2, dhm, "tn", BF16, "mix_in_dw", tm=1024, tn=1408, tk=2176)
    dhm = after(dhm, gw_mix)
    up_mix = big_update(mix_w_in[0], m_mix_w_in[0], v_mix_w_in[0], gw_mix, 1, "mix_in")
    du2 = _mm(dhm, w_mix, "nt", F32, "mix_in_dx", tm=544, tn=d, tk=1408)
    dx1, dg6[2], dsh, dsc = _ada_pre_bwd(x1, du2, dx2, g6, mods, 2, 1, nct, 2, tile, "pre2_bwd", dres_x_only=True)
    add_mod(both, 3, dsh)
    add_mod(both, 4, dsc)
    do1, dg6[1], dgt = _ada_post_bwd(dx1, o1, g6, mods, 1, 0, 0.5, nct, 2, tile, "post1_bwd")
    add_mod(both, 2, dgt)
    gw_out1 = _mm(a1, do1, "tn", BF16, "ffn1_out_dw", tm=1408, tn=1024, tk=2176)
    do1 = after(do1, gw_out1)
    up_out1 = big_update(ffn_w_out[0, 0], m_ffn_w_out[0, 0], v_ffn_w_out[0, 0], gw_out1, 0, "ffn1_out")
    da1 = _mm(do1, w_out1, "nt", F32, "ffn1_out_dx", tm=544, tn=1408, tk=d)
    dh1 = _swiglu_bwd(h1, da1, wide_tile, "swiglu1_bwd")
    dh1 = after(dh1, (up_mix[1], early_sums[0]))
    gw_in1 = _mm(u1, dh1, "tn", BF16, "ffn1_in_dw", tm=1024, tn=1408, tk=2176)
    dh1 = after(dh1, gw_in1)
    up_in1 = big_update(ffn_w_in[0, 0], m_ffn_w_in[0, 0], v_ffn_w_in[0, 0], gw_in1, 1, "ffn1_in")
    du1 = _mm(dh1, w_in1, "nt", F32, "ffn1_in_dx", tm=544, tn=d, tk=1408)
    dxin, dg6[0], dsh, dsc = _ada_pre_bwd(xin, du1, dx1, g6, mods, 0, 0, nct, 2, tile, "pre1_bwd")
    add_mod(both, 0, dsh)
    add_mod(both, 1, dsc)
    grad_x = dxin[n_ctx:][None]

    zero_d = jnp.zeros((d,), F32)
    d_ada_x = jnp.stack([dmod.get((1, k), zero_d) for k in range(9)]).reshape(9 * d)
    d_ada_c = jnp.stack([dmod.get((0, k), zero_d) for k in range(9)]).reshape(9 * d)
    dg_full = jnp.stack([g[0, 0] for g in dg6])
    small_parts = [d_ada_x, d_ada_c, dg_full, d_dskip]
    small_shapes = [p.shape for p in small_parts]
    packed = _pack(small_parts, 1024)
    gathered = _all_gather(packed, 0, "ag_small_grads").reshape(N_DEV, -1, 1024)
    summed = _sum_leading(gathered, "sum_small_grads")
    sums = _unpack(summed, small_shapes)
    sum_dx, sum_dc, sum_dg = sums[0], sums[1], sums[2]
    grad_ada_b = (sum_dx + sum_dc)[None]
    dx_rows = gathered.reshape(N_DEV, -1)[:, :9 * d]
    col0 = me * na
    da_rows = jnp.concatenate([lax.dynamic_slice_in_dim(dx_rows, col0, na, axis=1),
                               lax.dynamic_slice_in_dim(sum_dc[None], col0, na, axis=1),
                               jnp.zeros((2 * SUBLANE - N_DEV - 1, na), F32)], axis=0)
    grad_ada_w = _mm(sc, da_rows, "tn", F32, "ada_dw", tm=512, tn=na, tk=16)
    d_sc = _mm(da_rows, ada_w[0], "nt", F32, "ada_dx", tm=16, tn=512, tk=na)
    d_sc_all = _all_gather(jnp.broadcast_to(d_sc[N_DEV:N_DEV + 1], (SUBLANE, d)), 0, "ag_dctx")
    d_sc_sum = _sum_leading(d_sc_all.reshape(N_DEV, SUBLANE, d), "sum_dctx")
    grad_c_ctx = _silu_grad_rows(jnp.broadcast_to(c_ctx[None], (SUBLANE, d)), d_sc_sum, "ctx_silu_bwd")[0]
    grad_norm_g = lax.dynamic_slice_in_dim(sum_dg, me * ng_cols, ng_cols, axis=1)[None]

    upd = {}
    upd["ffn_w_in"] = [jnp.stack([up_in1[i], up_in2[i]])[None] for i in range(4)]
    upd["ffn_w_out"] = [jnp.stack([up_out1[i], up_out2[i]])[None] for i in range(4)]
    upd["mix_w_in"] = [o[None] for o in up_mix]
    upd["ssm_glu_w"] = [o[None] for o in up_glu]
    upd["ret_w_proj"] = [o[None] for o in up_rp]
    upd["mix_w_out"] = [o[None] for o in up_mo]
    upd["ada_w"] = [o[None] for o in _adamw(ada_w[0], m_ada_w[0], v_ada_w[0], grad_ada_w[None], "adamw_ada_w")]

    small_names = ["c_ctx", "ada_b", "norm_g", "ssm_lam_re", "ssm_lam_im", "ssm_log_step", "ssm_b_re", "ssm_b_im",
                   "ssm_c_re", "ssm_c_im", "ssm_d", "ret_decay_logit"]
    small_w = [c_ctx, ada_b, norm_g, ssm_lam_re, ssm_lam_im, ssm_log_step, ssm_b_re, ssm_b_im, ssm_c_re, ssm_c_im,
               ssm_d, ret_decay_logit]
    small_m = [m_c_ctx, m_ada_b, m_norm_g, m_ssm_lam_re, m_ssm_lam_im, m_ssm_log_step, m_ssm_b_re, m_ssm_b_im,
               m_ssm_c_re, m_ssm_c_im, m_ssm_d, m_ret_decay_logit]
    small_v = [v_c_ctx, v_ada_b, v_norm_g, v_ssm_lam_re, v_ssm_lam_im, v_ssm_log_step, v_ssm_b_re, v_ssm_b_im,
               v_ssm_c_re, v_ssm_c_im, v_ssm_d, v_ret_decay_logit]
    small_g = [grad_c_ctx, grad_ada_b, grad_norm_g] + [s[None] for s in early_sums[:s5_names]] + \
              [sums[3].reshape(ssm_d.shape), early_sums[s5_names][None]]
    shapes = [w.shape for w in small_w]
    res = _adamw(_pack(small_w, 1024), _pack(small_m, 1024), _pack(small_v, 1024), _pack(small_g, 1024)[None],
                 "adamw_small")
    small_out = [_unpack(o, shapes) for o in res]
    for i, nm in enumerate(small_names):
        upd[nm] = [small_out[kind][i] for kind in range(4)]

    order = ["c_ctx", "ada_w", "ada_b", "norm_g", "ffn_w_in", "ffn_w_out", "mix_w_in", "ssm_lam_re", "ssm_lam_im",
             "ssm_log_step", "ssm_b_re", "ssm_b_im", "ssm_c_re", "ssm_c_im", "ssm_d", "ssm_glu_w", "ret_decay_logit",
             "ret_w_proj", "mix_w_out"]
    outs = [loss, grad_x]
    for kind in range(4):
        outs += [upd[nm][kind] for nm in order]
    return tuple(outs)
```

```python
import functools
import math

import jax
import jax.numpy as jnp
import numpy as np
from jax import lax
from jax.experimental import pallas as pl
from jax.experimental.pallas import tpu as pltpu
from jax.experimental.pallas import tpu_sc as plsc

F32 = jnp.float32
BF16 = jnp.bfloat16
MXU_DTYPE = jnp.bfloat16
MESH_AXES = ("x", "y", "c")
N_DEV = 8
V7X_VMEM_LIMIT_BYTES = 56 * 1024 * 1024
LANE = 128
SUBLANE = 8

GRID_W = 64
RET_CHUNK = 128
ROPE_BASE = 10000.0
NORM_EPS = 1e-6
ADAM_LR = 0.001
ADAM_B1 = 0.9
ADAM_B2 = 0.999
ADAM_EPS = 1e-08
ADAM_WD = 0.01
ADAM_STEP = 10
SSM_TILE_GROUPS = 8
SSM_HALF_GROUPS = 4


def _params(sem=None):
    return pltpu.CompilerParams(dimension_semantics=sem, vmem_limit_bytes=V7X_VMEM_LIMIT_BYTES)


def _tile(n, target, mult):
    best = None
    t = mult
    while t <= min(n, target):
        if n % t == 0:
            best = t
        t += mult
    return n if best is None else best


def _sds(shape, dtype):
    return jax.ShapeDtypeStruct(tuple(shape), dtype)


def _mm(a, b, dims, out_dtype, name, tm=512, tn=1408, tk=2048):
    if dims == "nn":
        (m, k), (k2, n) = a.shape, b.shape
    elif dims == "nt":
        (m, k), (n, k2) = a.shape, b.shape
    else:
        (k, m), (k2, n) = a.shape, b.shape
    assert k == k2, (a.shape, b.shape, dims)
    tm = _tile(m, tm, 16)
    tn = _tile(n, tn, LANE)
    tk = _tile(k, tk, LANE if dims != "tn" else 16)
    nk = k // tk
    dn = {"nn": (((1,), (0,)), ((), ())), "nt": (((1,), (1,)), ((), ())), "tn": (((0,), (0,)), ((), ()))}[dims]

    def product(a_ref, b_ref):
        return lax.dot_general(a_ref[...].astype(MXU_DTYPE), b_ref[...].astype(MXU_DTYPE), dn,
                               preferred_element_type=F32)

    def body_single(a_ref, b_ref, o_ref):
        o_ref[...] = product(a_ref, b_ref).astype(o_ref.dtype)

    def body(a_ref, b_ref, o_ref, acc_ref):
        kk = pl.program_id(2)

        @pl.when(kk == 0)
        def _():
            acc_ref[...] = product(a_ref, b_ref)

        @pl.when((kk > 0) & (kk < nk - 1))
        def _():
            acc_ref[...] += product(a_ref, b_ref)

        @pl.when(kk == nk - 1)
        def _():
            o_ref[...] = (acc_ref[...] + product(a_ref, b_ref)).astype(o_ref.dtype)

    if dims == "nn":
        a_spec = pl.BlockSpec((tm, tk), lambda j, i, kk: (i, kk))
        b_spec = pl.BlockSpec((tk, tn), lambda j, i, kk: (kk, j))
    elif dims == "nt":
        a_spec = pl.BlockSpec((tm, tk), lambda j, i, kk: (i, kk))
        b_spec = pl.BlockSpec((tn, tk), lambda j, i, kk: (j, kk))
    else:
        a_spec = pl.BlockSpec((tk, tm), lambda j, i, kk: (kk, i))
        b_spec = pl.BlockSpec((tk, tn), lambda j, i, kk: (kk, j))
    return pl.pallas_call(
        body_single if nk == 1 else body, name=name, grid=(n // tn, m // tm, nk), in_specs=[a_spec, b_spec],
        out_specs=pl.BlockSpec((tm, tn), lambda j, i, kk: (i, j)), out_shape=_sds((m, n), out_dtype),
        scratch_shapes=[] if nk == 1 else [pltpu.VMEM((tm, tn), F32)],
        compiler_params=_params(("parallel", "parallel", "arbitrary")))(a, b)


def _rows(name, body, n_tiles, ins, outs):
    in_specs = [pl.BlockSpec(blk, imap) for (_, blk, imap) in ins]
    out_specs = [pl.BlockSpec(blk, imap) for (_, _, blk, imap) in outs]
    out_shape = [_sds(shape, dt) for (shape, dt, _, _) in outs]
    res = pl.pallas_call(body, name=name, grid=(n_tiles,), in_specs=in_specs, out_specs=out_specs,
                         out_shape=out_shape, compiler_params=_params(("arbitrary",)))(*[a for (a, _, _) in ins])
    return res


def _row_in(arr, tile, width=None, col=0, x_only_offset=None):
    width = arr.shape[1] if width is None else width
    if x_only_offset is None:
        return (arr, (tile, width), lambda i: (i, col))
    return (arr, (tile, width), lambda i: (jnp.maximum(i - x_only_offset, 0), col))


def _vec_in(arr, idx_fn):
    return (arr, (1, 1, arr.shape[2]), lambda i: (idx_fn(i), 0, 0))


def _rms(h):
    return lax.rsqrt(jnp.mean(h * h, axis=-1, keepdims=True) + NORM_EPS)


def _sigmoid(z):
    return 1.0 / (1.0 + jnp.exp(-z))


def _ada_pre_fwd(h, g6, mods, gi, mi, nct, tile, name):
    r, d = h.shape
    sel = lambda i: jnp.where(i >= nct, 1, 0)

    def body(h_ref, g_ref, sh_ref, sc_ref, u_ref):
        hh = h_ref[...]
        n = hh * _rms(hh) * g_ref[0]
        u_ref[...] = (n * (1.0 + sc_ref[0]) + sh_ref[0]).astype(u_ref.dtype)

    (u,) = _rows(name, body, r // tile,
                 [_row_in(h, tile), _vec_in(g6, lambda i: gi), _vec_in(mods, lambda i: sel(i) * 9 + 3 * mi),
                  _vec_in(mods, lambda i: sel(i) * 9 + 3 * mi + 1)],
                 [((r, d), BF16, (tile, d), lambda i: (i, 0))])
    return u


def _ada_pre_bwd(h, du, dres, g6, mods, gi, mi, nct, nsel, tile, name, dres_x_only=False):
    r, d = h.shape
    sel = lambda i: jnp.where(i >= nct, 1, 0) if nsel == 2 else 0
    msel = lambda i: jnp.where(i >= nct, 1, 0)
    off = nct if dres_x_only else None

    def body(h_ref, du_ref, dr_ref, g_ref, sc_ref, dh_ref, dg_ref, dsh_ref, dsc_ref):
        i = pl.program_id(0)
        hh = h_ref[...]
        rr = _rms(hh)
        g = g_ref[0]
        hn = hh * rr
        n = hn * g
        du_ = du_ref[...].astype(F32)
        dn = du_ * (1.0 + sc_ref[0])

        @pl.when(i == 0)
        def _():
            dg_ref[...] = jnp.zeros_like(dg_ref)

        @pl.when((i == 0) | (i == nct))
        def _():
            dsh_ref[...] = jnp.zeros_like(dsh_ref)
            dsc_ref[...] = jnp.zeros_like(dsc_ref)

        dg_ref[0] += jnp.sum(dn * hn, axis=0, keepdims=True)
        dsh_ref[0] += jnp.sum(du_, axis=0, keepdims=True)
        dsc_ref[0] += jnp.sum(du_ * n, axis=0, keepdims=True)
        t = dn * g
        dh = rr * t - hn * (rr * jnp.mean(t * hn, axis=-1, keepdims=True))
        if dres_x_only:
            dh_ref[...] = dh + jnp.where(i >= nct, dr_ref[...], 0.0)
        else:
            dh_ref[...] = dh + dr_ref[...]

    dh, dg, dsh, dsc = _rows(
        name, body, r // tile,
        [_row_in(h, tile), _row_in(du, tile), _row_in(dres, tile, x_only_offset=off), _vec_in(g6, lambda i: gi),
         _vec_in(mods, lambda i: msel(i) * 9 + 3 * mi + 1)],
        [((r, d), F32, (tile, d), lambda i: (i, 0)), ((1, 1, d), F32, (1, 1, d), lambda i: (0, 0, 0)),
         ((nsel, 1, d), F32, (1, 1, d), lambda i: (sel(i), 0, 0)),
         ((nsel, 1, d), F32, (1, 1, d), lambda i: (sel(i), 0, 0))])
    return dh, dg, dsh, dsc


def _ada_post_fwd(h, o, g6, mods, gi, mi, res_w, nct, tile, name, h_x_only=False):
    r, d = o.shape
    sel = lambda i: jnp.where(i >= nct, 1, 0)

    def body(h_ref, o_ref, g_ref, gt_ref, y_ref):
        oo = o_ref[...]
        n = oo * _rms(oo) * g_ref[0]
        y_ref[...] = h_ref[...] + res_w * gt_ref[0] * n

    (y,) = _rows(name, body, r // tile,
                 [_row_in(h, tile), _row_in(o, tile), _vec_in(g6, lambda i: gi),
                  _vec_in(mods, lambda i: sel(i) * 9 + 3 * mi + 2)],
                 [((r, d), F32, (tile, d), lambda i: (i, 0))])
    return y


def _ada_post_bwd(dy, o, g6, mods, gi, mi, res_w, nct, nsel, tile, name):
    r, d = o.shape
    sel = lambda i: jnp.where(i >= nct, 1, 0) if nsel == 2 else 0
    msel = lambda i: jnp.where(i >= nct, 1, 0)

    def body(dy_ref, o_ref, g_ref, gt_ref, do_ref, dg_ref, dgt_ref):
        i = pl.program_id(0)
        oo = o_ref[...]
        rr = _rms(oo)
        g = g_ref[0]
        on = oo * rr
        dy_ = dy_ref[...] * res_w

        @pl.when(i == 0)
        def _():
            dg_ref[...] = jnp.zeros_like(dg_ref)

        @pl.when((i == 0) | (i == nct))
        def _():
            dgt_ref[...] = jnp.zeros_like(dgt_ref)

        dgt_ref[0] += jnp.sum(dy_ * (on * g), axis=0, keepdims=True)
        dn = dy_ * gt_ref[0]
        dg_ref[0] += jnp.sum(dn * on, axis=0, keepdims=True)
        t = dn * g
        do_ref[...] = (rr * t - on * (rr * jnp.mean(t * on, axis=-1, keepdims=True))).astype(do_ref.dtype)

    do, dg, dgt = _rows(
        name, body, r // tile,
        [_row_in(dy, tile), _row_in(o, tile), _vec_in(g6, lambda i: gi),
         _vec_in(mods, lambda i: msel(i) * 9 + 3 * mi + 2)],
        [((r, d), BF16, (tile, d), lambda i: (i, 0)), ((1, 1, d), F32, (1, 1, d), lambda i: (0, 0, 0)),
         ((nsel, 1, d), F32, (1, 1, d), lambda i: (sel(i), 0, 0))])
    return do, dg, dgt


def _swiglu_fwd(h, tile, name):
    r, w2 = h.shape
    f = w2 // 2

    def body(h_ref, a_ref):
        gt = h_ref[:, :f]
        up = h_ref[:, f:]
        a_ref[...] = (gt * _sigmoid(gt) * up).astype(a_ref.dtype)

    (a,) = _rows(name, body, r // tile, [_row_in(h, tile)], [((r, f), BF16, (tile, f), lambda i: (i, 0))])
    return a


def _swiglu_bwd(h, da, tile, name):
    r, w2 = h.shape
    f = w2 // 2

    def body(h_ref, da_ref, dh_ref):
        gt = h_ref[:, :f]
        up = h_ref[:, f:]
        d = da_ref[...]
        sg = _sigmoid(gt)
        dh_ref[:, :f] = (d * up * (sg * (1.0 + gt * (1.0 - sg)))).astype(dh_ref.dtype)
        dh_ref[:, f:] = (d * gt * sg).astype(dh_ref.dtype)

    (dh,) = _rows(name, body, r // tile, [_row_in(h, tile), _row_in(da, tile)],
                  [((r, w2), BF16, (tile, w2), lambda i: (i, 0))])
    return dh


def _gelu_parts(y):
    c0 = math.sqrt(2.0 / math.pi)
    inner = c0 * (y + 0.044715 * y * y * y)
    th = jnp.tanh(inner)
    return th, c0 * (1.0 + 3 * 0.044715 * y * y)


def _ssm_out_fwd(y0, y1, hm, dskip, nct, tile, name):
    t_rows, s = y0.shape

    def body(y0_ref, y1_ref, u_ref, d_ref, a_ref):
        y = y0_ref[...] + y1_ref[...] + d_ref[0] * u_ref[...]
        th, _ = _gelu_parts(y)
        a_ref[...] = (0.5 * y * (1.0 + th)).astype(a_ref.dtype)

    (a,) = _rows(name, body, t_rows // tile,
                 [_row_in(y0, tile), _row_in(y1, tile), (hm, (tile, s), lambda i: (i + nct, 0)),
                  _vec_in(dskip, lambda i: 0)],
                 [((t_rows, s), BF16, (tile, s), lambda i: (i, 0))])
    return a


def _ssm_out_bwd(y0, y1, hm, dskip, da, nct, tile, name):
    t_rows, s = y0.shape

    def body(y0_ref, y1_ref, u_ref, d_ref, da_ref, dy_ref, du_ref, dd_ref):
        i = pl.program_id(0)
        u = u_ref[...]
        y = y0_ref[...] + y1_ref[...] + d_ref[0] * u
        th, dinner = _gelu_parts(y)
        dy = da_ref[...] * (0.5 * (1.0 + th) + 0.5 * y * (1.0 - th * th) * dinner)
        dy_ref[...] = dy
        du_ref[...] = dy * d_ref[0]

        @pl.when(i == 0)
        def _():
            dd_ref[...] = jnp.zeros_like(dd_ref)

        dd_ref[0] += jnp.sum(dy * u, axis=0, keepdims=True)

    dy, du, dd = _rows(name, body, t_rows // tile,
                       [_row_in(y0, tile), _row_in(y1, tile), (hm, (tile, s), lambda i: (i + nct, 0)),
                        _vec_in(dskip, lambda i: 0), _row_in(da, tile)],
                       [((t_rows, s), F32, (tile, s), lambda i: (i, 0)), ((t_rows, s), F32, (tile, s), lambda i: (i, 0)),
                        ((1, 1, s), F32, (1, 1, s), lambda i: (0, 0, 0))])
    return dy, du, dd


def _col_pieces(arr, off, width, tile, nct, unit=None):
    pw = math.gcd(off, width if unit is None else unit)
    specs = [(arr, (tile, pw), functools.partial(lambda i, cb: (i + nct, cb), cb=off // pw + p))
             for p in range(width // pw)]
    return specs, pw


def _ret_gate_fwd(o0, o1, hm, g_off, heads, dv, nct, tile, name):
    t_rows, w = o0.shape
    g_specs, pw = _col_pieces(hm, g_off, w, tile, nct)
    ng = len(g_specs)

    def body(o0_ref, o1_ref, *refs):
        g_refs, r_ref = refs[:ng], refs[ng]
        for hd in range(heads):
            cs = slice(hd * dv, (hd + 1) * dv)
            o = o0_ref[:, cs] + o1_ref[:, cs]
            lo = (hd * dv) % pw
            g = g_refs[(hd * dv) // pw][:, lo:lo + dv]
            r_ref[:, cs] = (g * _sigmoid(g) * (o * _rms(o))).astype(r_ref.dtype)

    (ri,) = _rows(name, body, t_rows // tile, [_row_in(o0, tile), _row_in(o1, tile)] + g_specs,
                  [((t_rows, w), BF16, (tile, w), lambda i: (i, 0))])
    return ri


def _ret_gate_bwd(o0, o1, hm, g_off, dri, heads, dv, nct, tile, name):
    t_rows, w = o0.shape
    g_specs, pw = _col_pieces(hm, g_off, w, tile, nct)
    ng = len(g_specs)

    def body(o0_ref, o1_ref, d_ref, *refs):
        g_refs, do_ref, dg_ref = refs[:ng], refs[ng], refs[ng + 1]
        for hd in range(heads):
            cs = slice(hd * dv, (hd + 1) * dv)
            o = o0_ref[:, cs] + o1_ref[:, cs]
            lo = (hd * dv) % pw
            g = g_refs[(hd * dv) // pw][:, lo:lo + dv]
            d = d_ref[:, cs]
            rr = _rms(o)
            on = o * rr
            sg = _sigmoid(g)
            dg_ref[:, cs] = d * on * (sg * (1.0 + g * (1.0 - sg)))
            t = d * (g * sg)
            do_ref[:, cs] = rr * t - on * (rr * jnp.mean(t * on, axis=-1, keepdims=True))

    do, dg = _rows(name, body, t_rows // tile, [_row_in(o0, tile), _row_in(o1, tile), _row_in(dri, tile)] + g_specs,
                   [((t_rows, w), F32, (tile, w), lambda i: (i, 0)), ((t_rows, w), F32, (tile, w), lambda i: (i, 0))])
    return do, dg


def _merge_fwd(gab, rb, hm, gs_off, nct, tile, name):
    t_rows, d = rb.shape
    specs, pw = _col_pieces(hm, gs_off, 2 * d, tile, nct, unit=d)
    npc = d // pw

    def body(gab_ref, rb_ref, *refs):
        gs_refs, gr_refs, m_ref = refs[:npc], refs[npc:2 * npc], refs[2 * npc]
        for p in range(npc):
            cs = slice(p * pw, (p + 1) * pw)
            ga = gab_ref[:, cs]
            gb = gab_ref[:, d + p * pw:d + (p + 1) * pw]
            m_ref[:, cs] = (_sigmoid(gs_refs[p][...]) * (ga * _sigmoid(gb))
                            + _sigmoid(gr_refs[p][...]) * rb_ref[:, cs]).astype(m_ref.dtype)

    (mg,) = _rows(name, body, t_rows // tile, [_row_in(gab, tile), _row_in(rb, tile)] + specs,
                  [((t_rows, d), BF16, (tile, d), lambda i: (i, 0))])
    return mg


def _merge_bwd(gab, rb, hm, gs_off, dm, nct, tile, name):
    t_rows, d = rb.shape
    specs, pw = _col_pieces(hm, gs_off, 2 * d, tile, nct, unit=d)
    npc = d // pw

    def body(gab_ref, rb_ref, dm_ref, *refs):
        gs_refs, gr_refs = refs[:npc], refs[npc:2 * npc]
        dgab_ref, drb_ref, dgs_ref, dgr_ref = refs[2 * npc:]
        for p in range(npc):
            cs = slice(p * pw, (p + 1) * pw)
            cs2 = slice(d + p * pw, d + (p + 1) * pw)
            ga = gab_ref[:, cs]
            gb = gab_ref[:, cs2]
            dmm = dm_ref[:, cs]
            ss = _sigmoid(gs_refs[p][...])
            sr = _sigmoid(gr_refs[p][...])
            sb = _sigmoid(gb)
            dbr = dmm * ss
            dgab_ref[:, cs] = (dbr * sb).astype(dgab_ref.dtype)
            dgab_ref[:, cs2] = (dbr * ga * sb * (1.0 - sb)).astype(dgab_ref.dtype)
            drb_ref[:, cs] = (dmm * sr).astype(drb_ref.dtype)
            dgs_ref[:, cs] = dmm * (ga * sb) * ss * (1.0 - ss)
            dgr_ref[:, cs] = dmm * rb_ref[:, cs] * sr * (1.0 - sr)

    return _rows(name, body, t_rows // tile, [_row_in(gab, tile), _row_in(rb, tile), _row_in(dm, tile)] + specs,
                 [((t_rows, 2 * d), BF16, (tile, 2 * d), lambda i: (i, 0)), ((t_rows, d), BF16, (tile, d), lambda i: (i, 0)),
                  ((t_rows, d), F32, (tile, d), lambda i: (i, 0)), ((t_rows, d), F32, (tile, d), lambda i: (i, 0))])


def _assemble_dhm(dus, dq0, dq1, dk0, dk1, dv0, dv1, dg, dgs, dgr, nct, tile, name):
    r, s = dus.shape
    qk = dq0.shape[1]
    vw = dv0.shape[1]
    d = dgs.shape[1]
    mi = s + 2 * qk + 2 * vw + 2 * d
    c_q, c_k, c_v, c_g, c_gs, c_gr = s, s + qk, s + 2 * qk, s + 2 * qk + vw, s + 2 * qk + 2 * vw, s + 2 * qk + 2 * vw + d

    def body(dus_ref, dq0_ref, dq1_ref, dk0_ref, dk1_ref, dv0_ref, dv1_ref, dg_ref, dgs_ref, dgr_ref, o_ref):
        i = pl.program_id(0)
        lat = i >= nct
        o_ref[:, :s] = dus_ref[...].astype(o_ref.dtype)
        o_ref[:, c_q:c_k] = (dq0_ref[...] + dq1_ref[...]).astype(o_ref.dtype)
        o_ref[:, c_k:c_v] = (dk0_ref[...] + dk1_ref[...]).astype(o_ref.dtype)
        o_ref[:, c_v:c_g] = (dv0_ref[...] + dv1_ref[...]).astype(o_ref.dtype)
        o_ref[:, c_g:c_gs] = jnp.where(lat, dg_ref[...], 0.0).astype(o_ref.dtype)
        o_ref[:, c_gs:c_gr] = jnp.where(lat, dgs_ref[...], 0.0).astype(o_ref.dtype)
        o_ref[:, c_gr:] = jnp.where(lat, dgr_ref[...], 0.0).astype(o_ref.dtype)

    (out,) = _rows(name, body, r // tile,
                   [_row_in(dus, tile), _row_in(dq0, tile), _row_in(dq1, tile), _row_in(dk0, tile), _row_in(dk1, tile),
                    _row_in(dv0, tile), _row_in(dv1, tile), _row_in(dg, tile, x_only_offset=nct),
                    _row_in(dgs, tile, x_only_offset=nct), _row_in(dgr, tile, x_only_offset=nct)],
                   [((r, mi), BF16, (tile, mi), lambda i: (i, 0))])
    return out


def _loss_grad(y, target, tile, name):
    t_rows, d = y.shape

    def body(y_ref, t_ref, dy_ref, l_ref):
        i = pl.program_id(0)
        e = y_ref[...] - t_ref[...]
        dy_ref[...] = e * (1.0 / d)

        @pl.when(i == 0)
        def _():
            l_ref[...] = jnp.zeros_like(l_ref)

        l_ref[0] += jnp.sum(e * e, axis=0, keepdims=True)

    return _rows(name, body, t_rows // tile, [_row_in(y, tile), _row_in(target, tile)],
                 [((t_rows, d), F32, (tile, d), lambda i: (i, 0)), ((1, 1, d), F32, (1, 1, d), lambda i: (0, 0, 0))])


def _silu_rows(v, name):
    def body(v_ref, o_ref):
        z = v_ref[...]
        o_ref[...] = z * _sigmoid(z)

    (o,) = _rows(name, body, 1, [_row_in(v, v.shape[0])], [(v.shape, F32, v.shape, lambda i: (0, 0))])
    return o


def _silu_grad_rows(v, dv, name):
    def body(v_ref, d_ref, o_ref):
        z = v_ref[...]
        sg = _sigmoid(z)
        o_ref[...] = d_ref[...] * (sg * (1.0 + z * (1.0 - sg)))

    (o,) = _rows(name, body, 1, [_row_in(v, v.shape[0]), _row_in(dv, v.shape[0])],
                 [(v.shape, F32, v.shape, lambda i: (0, 0))])
    return o


def _sum_leading(g8, name):
    n, r, c = g8.shape
    tile = _tile(r, 256, SUBLANE)

    def body(g_ref, o_ref):
        acc = g_ref[0]
        for j in range(1, n):
            acc = acc + g_ref[j]
        o_ref[...] = acc

    (o,) = _rows(name, body, r // tile, [(g8, (n, tile, c), lambda i: (0, i, 0))],
                 [((r, c), F32, (tile, c), lambda i: (i, 0))])
    return o


def _pair_sum(g, recv, axis, name):
    n, br, bc = recv.shape
    tile = _tile(br, 256, 16)
    nrt = br // tile
    core = lax.axis_index("c").astype(jnp.int32).reshape(1)

    def body(c_ref, g_ref, r_ref, o_ref):
        o_ref[0] = (g_ref[...].astype(F32) + r_ref[0].astype(F32)).astype(o_ref.dtype)

    if axis == 1:
        g_spec = pl.BlockSpec((tile, bc), lambda q, i, c_ref: (i, 2 * q + c_ref[0]))
    else:
        g_spec = pl.BlockSpec((tile, bc), lambda q, i, c_ref: ((2 * q + c_ref[0]) * nrt + i, 0))
    slot = pl.BlockSpec((1, tile, bc), lambda q, i, c_ref: (q, i, 0))
    return pl.pallas_call(
        body, name=name, out_shape=_sds((n, br, bc), recv.dtype),
        grid_spec=pltpu.PrefetchScalarGridSpec(num_scalar_prefetch=1, grid=(n, nrt), in_specs=[g_spec, slot],
                                               out_specs=slot),
        compiler_params=_params(("arbitrary", "arbitrary")))(core, g, recv)


def _adam_math(w, m, v, g):
    c1 = 1.0 / (1.0 - ADAM_B1 ** ADAM_STEP)
    c2 = 1.0 / (1.0 - ADAM_B2 ** ADAM_STEP)
    mm = ADAM_B1 * m + (1.0 - ADAM_B1) * g
    vv = ADAM_B2 * v + (1.0 - ADAM_B2) * (g * g)
    return -ADAM_LR * ((mm * c1) / (jnp.sqrt(vv * c2) + ADAM_EPS) + ADAM_WD * w), mm, vv


def _adamw(w, m, v, gparts, name):
    r, c = w.shape
    n = gparts.shape[0]
    tile = _tile(r, 256, 16)

    def body(w_ref, m_ref, v_ref, g_ref, go_ref, d_ref, mo_ref, vo_ref):
        g = g_ref[0].astype(F32)
        for j in range(1, n):
            g = g + g_ref[j].astype(F32)
        go_ref[...] = g
        d_ref[...], mo_ref[...], vo_ref[...] = _adam_math(w_ref[...], m_ref[...], v_ref[...], g)

    rs = lambda arr: _row_in(arr, tile)
    out = ((r, c), F32, (tile, c), lambda i: (i, 0))
    return _rows(name, body, r // tile, [rs(w), rs(m), rs(v), (gparts, (n, tile, c), lambda i: (0, i, 0))],
                 [out, out, out, out])


def _adamw_scattered(w, m, v, p, recv, name):
    r, c = w.shape
    n = recv.shape[0]
    tile = _tile(r, 256, 16)
    chip = (2 * lax.axis_index("x") + lax.axis_index("y")).astype(jnp.int32).reshape(1)

    def body(q_ref, w_ref, m_ref, v_ref, p_ref, g_ref, go_ref, d_ref, mo_ref, vo_ref):
        g = p_ref[0].astype(F32)
        for j in range(n):
            g = g + g_ref[j].astype(F32)
        go_ref[...] = g
        d_ref[...], mo_ref[...], vo_ref[...] = _adam_math(w_ref[...], m_ref[...], v_ref[...], g)

    row = pl.BlockSpec((tile, c), lambda i, q_ref: (i, 0))
    out = _sds((r, c), F32)
    return pl.pallas_call(
        body, name=name, out_shape=[out, out, out, out],
        grid_spec=pltpu.PrefetchScalarGridSpec(
            num_scalar_prefetch=1, grid=(r // tile,),
            in_specs=[row, row, row, pl.BlockSpec((1, tile, c), lambda i, q_ref: (q_ref[0], i, 0)),
                      pl.BlockSpec((n, tile, c), lambda i, q_ref: (0, i, 0))],
            out_specs=[row, row, row, row]),
        compiler_params=_params(("arbitrary",)))(chip, w, m, v, p, recv)


def _cmul(ar, ai, br, bi):
    return ar * br - ai * bi, ar * bi + ai * br


def _cpow(ar, ai, n):
    pr, pi = jnp.ones_like(ar), jnp.zeros_like(ar)
    br, bi = ar, ai
    while n:
        if n & 1:
            pr, pi = _cmul(pr, pi, br, bi)
        n >>= 1
        if n:
            br, bi = _cmul(br, bi, br, bi)
    return pr, pi


def _s5_scan_into(xr_ref, xi_ref, ar1, ai1, ns, fr_ref, fi_ref, hr_ref, hi_ref, reverse):
    st = ar1.shape[1]
    ar = jnp.broadcast_to(ar1, (SUBLANE, st))
    ai = jnp.broadcast_to(ai1, (SUBLANE, st))
    zero = jnp.zeros((SUBLANE, st), F32)
    zero1 = jnp.zeros((1, st), F32)

    def slab(k):
        return pl.ds(pl.multiple_of(k * SUBLANE, SUBLANE), SUBLANE)

    def pass1(j, carry):
        hr, hi = carry
        k = ns - 1 - j if reverse else j
        nr, ni = _cmul(ar, ai, hr, hi)
        return nr + xr_ref[slab(k), :], ni + xi_ref[slab(k), :]

    fr, fi = lax.fori_loop(0, ns, pass1, (zero, zero))
    fr_ref[...] = fr
    fi_ref[...] = fi
    pr, pi = _cpow(ar1, ai1, ns)
    order = list(range(N_DEV - 1, -1, -1)) if reverse else list(range(N_DEV))
    hr_ref[order[0]:order[0] + 1, :] = zero1
    hi_ref[order[0]:order[0] + 1, :] = zero1
    for a_, b_ in zip(order[:-1], order[1:]):
        cr, ci = _cmul(pr, pi, hr_ref[a_:a_ + 1, :], hi_ref[a_:a_ + 1, :])
        hr_ref[b_:b_ + 1, :] = cr + fr_ref[a_:a_ + 1, :]
        hi_ref[b_:b_ + 1, :] = ci + fi_ref[a_:a_ + 1, :]

    def pass2(j, carry):
        hr, hi = carry
        k = ns - 1 - j if reverse else j
        nr, ni = _cmul(ar, ai, hr, hi)
        nr = nr + xr_ref[slab(k), :]
        ni = ni + xi_ref[slab(k), :]
        xr_ref[slab(k), :] = nr
        xi_ref[slab(k), :] = ni
        return nr, ni

    lax.fori_loop(0, ns, pass2, (hr_ref[...], hi_ref[...]))


def _s5_specs(r, ch, st):
    u_spec = pl.BlockSpec((r, ch), lambda j: (0, j // 2))
    w_spec = pl.BlockSpec((1, ch, st), lambda j: (j, 0, 0))
    c_spec = pl.BlockSpec((1, st, ch), lambda j: (j, 0, 0))
    a_spec = pl.BlockSpec((1, 2, st), lambda j: (j, 0, 0))
    return u_spec, w_spec, c_spec, a_spec


def _s5_fwd(up, wre, wim, cre, cim, a, rev, name):
    r, s = up.shape
    nh, ch, st = wre.shape
    ns = r // N_DEV
    u_spec, w_spec, c_spec, a_spec = _s5_specs(r, ch, st)

    def body(u_ref, wre_ref, wim_ref, cre_ref, cim_ref, a_ref, y_ref, xr, xi, fr, fi, hr, hi):
        j = pl.program_id(0)
        for rb in range(N_DEV):
            rows = slice(rb * ns, (rb + 1) * ns)
            ub = u_ref[rows, :].astype(MXU_DTYPE)
            xr[rows, :] = jnp.dot(ub, wre_ref[0].astype(MXU_DTYPE), preferred_element_type=F32)
            xi[rows, :] = jnp.dot(ub, wim_ref[0].astype(MXU_DTYPE), preferred_element_type=F32)
        _s5_scan_into(xr, xi, a_ref[0, 0:1, :], a_ref[0, 1:2, :], ns, fr, fi, hr, hi, rev)
        for rb in range(N_DEV):
            rows = slice(rb * ns, (rb + 1) * ns)
            yb = (jnp.dot(xr[rows, :].astype(MXU_DTYPE), cre_ref[0].astype(MXU_DTYPE), preferred_element_type=F32)
                  - jnp.dot(xi[rows, :].astype(MXU_DTYPE), cim_ref[0].astype(MXU_DTYPE), preferred_element_type=F32))

            @pl.when(j % 2 == 0)
            def _():
                y_ref[rows, :] = yb

            @pl.when(j % 2 == 1)
            def _():
                y_ref[rows, :] += yb

    small = pltpu.VMEM((SUBLANE, st), F32)
    return pl.pallas_call(
        body, name=name, grid=(nh,), in_specs=[u_spec, w_spec, w_spec, c_spec, c_spec, a_spec],
        out_specs=pl.BlockSpec((r, ch), lambda j: (0, j // 2)), out_shape=_sds((r, s), F32),
        scratch_shapes=[pltpu.VMEM((r, st), F32), pltpu.VMEM((r, st), F32), small, small, small, small],
        compiler_params=_params(("arbitrary",)))(up, wre, wim, cre, cim, a)


def _s5_bwd(up, dyp, wre, wim, cre, cim, a, rev, name):
    r, s = up.shape
    nh, ch, st = wre.shape
    ns = r // N_DEV
    u_spec, w_spec, c_spec, a_spec = _s5_specs(r, ch, st)
    nt = (((1,), (1,)), ((), ()))
    tn = (((0,), (0,)), ((), ()))

    def body(u_ref, dy_ref, wre_ref, wim_ref, cre_ref, cim_ref, a_ref,
             du_ref, dwre_ref, dwim_ref, dcre_ref, dcim_ref, da_ref,
             hr, hi, gr, gi, fr, fi, sr, si, er, ei):
        j = pl.program_id(0)
        wre_b = wre_ref[0].astype(MXU_DTYPE)
        wim_b = wim_ref[0].astype(MXU_DTYPE)
        cre_b = cre_ref[0].astype(MXU_DTYPE)
        cim_b = cim_ref[0].astype(MXU_DTYPE)
        for rb in range(N_DEV):
            rows = slice(rb * ns, (rb + 1) * ns)
            ub = u_ref[rows, :].astype(MXU_DTYPE)
            hr[rows, :] = jnp.dot(ub, wre_b, preferred_element_type=F32)
            hi[rows, :] = jnp.dot(ub, wim_b, preferred_element_type=F32)
        ar1, ai1 = a_ref[0, 0:1, :], a_ref[0, 1:2, :]
        _s5_scan_into(hr, hi, ar1, ai1, ns, fr, fi, sr, si, rev)
        dcre = jnp.zeros((st, ch), F32)
        dcim = jnp.zeros((st, ch), F32)
        for rb in range(N_DEV):
            rows = slice(rb * ns, (rb + 1) * ns)
            dyb = dy_ref[rows, :].astype(MXU_DTYPE)
            gr[rows, :] = lax.dot_general(dyb, cre_b, nt, preferred_element_type=F32)
            gi[rows, :] = -lax.dot_general(dyb, cim_b, nt, preferred_element_type=F32)
            dcre += lax.dot_general(hr[rows, :].astype(MXU_DTYPE), dyb, tn, preferred_element_type=F32)
            dcim -= lax.dot_general(hi[rows, :].astype(MXU_DTYPE), dyb, tn, preferred_element_type=F32)
        dcre_ref[0] = dcre
        dcim_ref[0] = dcim
        _s5_scan_into(gr, gi, ar1, -ai1, ns, fr, fi, er, ei, not rev)

        def slab(k):
            return pl.ds(pl.multiple_of(k * SUBLANE, SUBLANE), SUBLANE)

        step_back = 1 if rev else -1

        def acc_step(k, carry):
            acr, aci = carry
            g_r, g_i = gr[slab(k), :], gi[slab(k), :]
            p_r, p_i = hr[slab(k + step_back), :], hi[slab(k + step_back), :]
            return acr + g_r * p_r + g_i * p_i, aci + g_i * p_r - g_r * p_i

        edge = (ns - 1) * SUBLANE if rev else 0
        g_r, g_i = gr[edge:edge + SUBLANE, :], gi[edge:edge + SUBLANE, :]
        p_r, p_i = sr[...], si[...]
        lo, hi_k = (0, ns - 1) if rev else (1, ns)
        acr, aci = lax.fori_loop(lo, hi_k, acc_step, (g_r * p_r + g_i * p_i, g_i * p_r - g_r * p_i))
        da_ref[0, 0:1, :] = jnp.sum(acr, axis=0, keepdims=True)
        da_ref[0, 1:2, :] = jnp.sum(aci, axis=0, keepdims=True)
        dwre = jnp.zeros((ch, st), F32)
        dwim = jnp.zeros((ch, st), F32)
        for rb in range(N_DEV):
            rows = slice(rb * ns, (rb + 1) * ns)
            grb = gr[rows, :].astype(MXU_DTYPE)
            gib = gi[rows, :].astype(MXU_DTYPE)
            ub = u_ref[rows, :].astype(MXU_DTYPE)
            dub = (lax.dot_general(grb, wre_b, nt, preferred_element_type=F32)
                   + lax.dot_general(gib, wim_b, nt, preferred_element_type=F32))
            dwre += lax.dot_general(ub, grb, tn, preferred_element_type=F32)
            dwim += lax.dot_general(ub, gib, tn, preferred_element_type=F32)

            @pl.when(j % 2 == 0)
            def _():
                du_ref[rows, :] = dub

            @pl.when(j % 2 == 1)
            def _():
                du_ref[rows, :] += dub

        dwre_ref[0] = dwre
        dwim_ref[0] = dwim

    small = pltpu.VMEM((SUBLANE, st), F32)
    big = pltpu.VMEM((r, st), F32)
    return pl.pallas_call(
        body, name=name, grid=(nh,), in_specs=[u_spec, u_spec, w_spec, w_spec, c_spec, c_spec, a_spec],
        out_specs=[pl.BlockSpec((r, ch), lambda j: (0, j // 2)), w_spec, w_spec, c_spec, c_spec, a_spec],
        out_shape=[_sds((r, s), F32), _sds(wre.shape, F32), _sds(wre.shape, F32), _sds(cre.shape, F32),
                   _sds(cre.shape, F32), _sds(a.shape, F32)],
        scratch_shapes=[big, big, big, big, small, small, small, small, small, small],
        compiler_params=_params(("arbitrary",)))(up, dyp, wre, wim, cre, cim, a)


def _rope(t, cos, sin):
    quarter = t.shape[1] // 4
    lane = lax.broadcasted_iota(jnp.int32, t.shape, 1)
    first = (lane // quarter) % 2 == 0
    partner = jnp.where(first, pltpu.roll(t, t.shape[1] - quarter, 1), pltpu.roll(t, quarter, 1))
    return t * cos + partner * sin


def _rope_t(d, cos, sin):
    quarter = d.shape[1] // 4
    ds_ = d * sin
    lane = lax.broadcasted_iota(jnp.int32, d.shape, 1)
    first = (lane // quarter) % 2 == 0
    partner = jnp.where(first, pltpu.roll(ds_, d.shape[1] - quarter, 1), pltpu.roll(ds_, quarter, 1))
    return d * cos + partner


def _chunk_of_step(s, nch, ncc, rev):
    if not rev:
        return s
    return jnp.where(s < ncc, ncc - 1 - s, nch + ncc - 1 - s)


def _ret_fwd(hm, cos, sin, decay, wend, win, gch, heads, dk, dv, q_off, ncc, rev, name):
    r = hm.shape[0]
    ch = RET_CHUNK
    nch = r // ch
    t_rows = r - ncc * ch
    qb, kb, vb = q_off // dk, (q_off + heads * dk) // dk, (q_off + 2 * heads * dk) // dv
    q_scale = dk ** -0.5
    nt = (((1,), (1,)), ((), ()))
    tn = (((0,), (0,)), ((), ()))
    cof = lambda s: _chunk_of_step(s, nch, ncc, rev)

    def body(q_ref, k_ref, v_ref, cos_ref, sin_ref, dec_ref, we_ref, wi_ref, g_ref, o_ref, sin_out, st):
        s = pl.program_id(1)

        @pl.when(s == 0)
        def _():
            st[...] = jnp.zeros_like(st)

        q = _rope(q_ref[...], cos_ref[...], sin_ref[...]) * q_scale
        k = _rope(k_ref[...], cos_ref[...], sin_ref[...])
        v = v_ref[...].astype(MXU_DTYPE)
        s_cur = st[...]
        sin_out[0, 0] = s_cur
        kw = (k * we_ref[0]).astype(MXU_DTYPE)
        qw = (q * wi_ref[0]).astype(MXU_DTYPE)
        scores = lax.dot_general(q.astype(MXU_DTYPE), k.astype(MXU_DTYPE), nt, preferred_element_type=F32) * dec_ref[0]
        o_ref[...] = (jnp.dot(scores.astype(MXU_DTYPE), v, preferred_element_type=F32)
                      + jnp.dot(qw, s_cur.astype(MXU_DTYPE), preferred_element_type=F32))
        st[...] = g_ref[0] * s_cur + lax.dot_general(kw, v, tn, preferred_element_type=F32)

    tab = lambda w: pl.BlockSpec((1, ch, w), lambda h, s: (h, 0, 0))
    return pl.pallas_call(
        body, name=name, grid=(heads, nch),
        in_specs=[pl.BlockSpec((ch, dk), lambda h, s: (cof(s), qb + h)),
                  pl.BlockSpec((ch, dk), lambda h, s: (cof(s), kb + h)),
                  pl.BlockSpec((ch, dv), lambda h, s: (cof(s), vb + h)),
                  pl.BlockSpec((ch, dk), lambda h, s: (cof(s), 0)),
                  pl.BlockSpec((ch, dk), lambda h, s: (cof(s), 0)),
                  tab(ch), tab(dk), tab(dk), tab(dv)],
        out_specs=[pl.BlockSpec((ch, dv), lambda h, s: (jnp.maximum(cof(s) - ncc, 0) if not rev
                                                         else jnp.where(s < ncc, nch - ncc - 1, cof(s) - ncc), h)),
                   pl.BlockSpec((1, 1, dk, dv), lambda h, s: (h, s, 0, 0))],
        out_shape=[_sds((t_rows, heads * dv), F32), _sds((heads, nch, dk, dv), F32)],
        scratch_shapes=[pltpu.VMEM((dk, dv), F32)],
        compiler_params=_params(("parallel", "arbitrary")))(hm, hm, hm, cos, sin, decay, wend, win, gch)


def _ret_bwd(hm, cos, sin, decay, wend, win, gch, s_in, do, heads, dk, dv, q_off, ncc, rev, name):
    r = hm.shape[0]
    ch = RET_CHUNK
    nch = r // ch
    qb, kb, vb = q_off // dk, (q_off + heads * dk) // dk, (q_off + 2 * heads * dk) // dv
    q_scale = dk ** -0.5
    nt = (((1,), (1,)), ((), ()))
    tn = (((0,), (0,)), ((), ()))
    cof = lambda rr: _chunk_of_step(nch - 1 - rr, nch, ncc, rev)

    def body(q_ref, k_ref, v_ref, cos_ref, sin_ref, dec_ref, we_ref, wi_ref, g_ref, sin_ref2, do_ref,
             dq_ref, dk_ref, dv_ref, ddec_ref, dwe_ref, dwi_ref, dg_ref, dst):
        rr = pl.program_id(1)
        n = cof(rr)

        @pl.when(rr == 0)
        def _():
            dst[...] = jnp.zeros_like(dst)
            ddec_ref[...] = jnp.zeros_like(ddec_ref)
            dwe_ref[...] = jnp.zeros_like(dwe_ref)
            dwi_ref[...] = jnp.zeros_like(dwi_ref)
            dg_ref[...] = jnp.zeros_like(dg_ref)

        cos_, sin_ = cos_ref[...], sin_ref[...]
        q = _rope(q_ref[...], cos_, sin_) * q_scale
        k = _rope(k_ref[...], cos_, sin_)
        v = v_ref[...].astype(MXU_DTYPE)
        qb_, kb_ = q.astype(MXU_DTYPE), k.astype(MXU_DTYPE)
        kw = (k * we_ref[0]).astype(MXU_DTYPE)
        qw = (q * wi_ref[0]).astype(MXU_DTYPE)
        sraw = lax.dot_general(qb_, kb_, nt, preferred_element_type=F32)
        scores = (sraw * dec_ref[0]).astype(MXU_DTYPE)
        d_o = jnp.where(n >= ncc, do_ref[...], 0.0).astype(MXU_DTYPE)
        s_n = sin_ref2[0, 0]
        s_nb = s_n.astype(MXU_DTYPE)
        ds1 = dst[...]
        ds1b = ds1.astype(MXU_DTYPE)
        dsc = lax.dot_general(d_o, v, nt, preferred_element_type=F32)
        dsr = (dsc * dec_ref[0]).astype(MXU_DTYPE)
        ddec_ref[0] += dsc * sraw
        t1 = lax.dot_general(d_o, s_nb, nt, preferred_element_type=F32)
        dq_r = jnp.dot(dsr, kb_, preferred_element_type=F32) + t1 * wi_ref[0]
        dwi_ref[0] += t1 * q
        t2 = lax.dot_general(v, ds1b, nt, preferred_element_type=F32)
        dk_r = lax.dot_general(dsr, qb_, tn, preferred_element_type=F32) + t2 * we_ref[0]
        dwe_ref[0] += t2 * k
        dv_ref[...] = (lax.dot_general(scores, d_o, tn, preferred_element_type=F32)
                       + jnp.dot(kw, ds1b, preferred_element_type=F32))
        dg_ref[0] += ds1 * s_n
        dst[...] = g_ref[0] * ds1 + lax.dot_general(qw, d_o, tn, preferred_element_type=F32)
        dq_ref[...] = _rope_t(dq_r, cos_, sin_) * q_scale
        dk_ref[...] = _rope_t(dk_r, cos_, sin_)

    tab = lambda w: pl.BlockSpec((1, ch, w), lambda h, rr: (h, 0, 0))
    return pl.pallas_call(
        body, name=name, grid=(heads, nch),
        in_specs=[pl.BlockSpec((ch, dk), lambda h, rr: (cof(rr), qb + h)),
                  pl.BlockSpec((ch, dk), lambda h, rr: (cof(rr), kb + h)),
                  pl.BlockSpec((ch, dv), lambda h, rr: (cof(rr), vb + h)),
                  pl.BlockSpec((ch, dk), lambda h, rr: (cof(rr), 0)),
                  pl.BlockSpec((ch, dk), lambda h, rr: (cof(rr), 0)),
                  tab(ch), tab(dk), tab(dk), tab(dv),
                  pl.BlockSpec((1, 1, dk, dv), lambda h, rr: (h, nch - 1 - rr, 0, 0)),
                  pl.BlockSpec((ch, dv), lambda h, rr: (jnp.maximum(cof(rr) - ncc, 0), h))],
        out_specs=[pl.BlockSpec((ch, dk), lambda h, rr: (cof(rr), h)),
                   pl.BlockSpec((ch, dk), lambda h, rr: (cof(rr), h)),
                   pl.BlockSpec((ch, dv), lambda h, rr: (cof(rr), h)),
                   tab(ch), tab(dk), tab(dk), tab(dv)],
        out_shape=[_sds((r, heads * dk), F32), _sds((r, heads * dk), F32), _sds((r, heads * dv), F32),
                   _sds(decay.shape, F32), _sds(wend.shape, F32), _sds(win.shape, F32), _sds(gch.shape, F32)],
        scratch_shapes=[pltpu.VMEM((dk, dv), F32)],
        compiler_params=_params(("parallel", "arbitrary")))(hm, hm, hm, cos, sin, decay, wend, win, gch, s_in, do)


_HBM = pl.BlockSpec(memory_space=pltpu.HBM)
_MESH = pl.DeviceIdType.MESH
ALL_GATHER_COLLECTIVE_ID = 1
SIBLING_COLLECTIVE_ID = 2
CHIPS_COLLECTIVE_ID = 3


def _axis_slice(ref, axis, start, size):
    idx = [slice(None)] * len(ref.shape)
    idx[axis] = pl.ds(start, size)
    return ref.at[tuple(idx)]


def _sibling_and_chip_peers():
    x, y, c = lax.axis_index("x"), lax.axis_index("y"), lax.axis_index("c")
    return [(x, y, 1 - c), (1 - x, y, c), (x, 1 - y, c), (1 - x, 1 - y, c)]


def _launch_exchange(body, name, operand, out_shape, sems, peers_fn, collective_id, on_sequencer):
    if not on_sequencer:
        return pl.pallas_call(body, name=name, out_shape=out_shape, in_specs=[_HBM], out_specs=_HBM,
                              scratch_shapes=sems)(operand)

    def sequencer_body(in_ref, out_ref, *sem_refs):
        peers = peers_fn()
        barrier = pltpu.get_barrier_semaphore()
        for peer in peers:
            pl.semaphore_signal(barrier, inc=1, device_id=peer, device_id_type=_MESH)
        pl.semaphore_wait(barrier, len(peers))
        body(in_ref, out_ref, *sem_refs)

    return pl.kernel(sequencer_body, out_type=out_shape, name=name,
                     mesh=plsc.ScalarSubcoreMesh(axis_name="sequencer", num_cores=1), scratch_types=sems,
                     compiler_params=pltpu.CompilerParams(collective_id=collective_id))(operand)


def _all_gather(shard, axis, name, on_sequencer=False):
    m = shard.shape[axis]
    out_shape = list(shard.shape)
    out_shape[axis] = N_DEV * m

    def body(x_ref, out_ref, send_sems, recv_sems, local_sem):
        x, y, c = lax.axis_index("x"), lax.axis_index("y"), lax.axis_index("c")
        me, sibling = (x, y, c), (x, y, 1 - c)
        chips = [(1 - x, y), (x, 1 - y), (1 - x, 1 - y)]

        def block(px, py, pc):
            return _axis_slice(out_ref, axis, (4 * px + 2 * py + pc) * m, m)

        def copy(k, blk, to, src=None):
            return pltpu.make_async_remote_copy(
                src_ref=block(*blk) if src is None else src, dst_ref=block(*blk), send_sem=send_sems.at[k],
                recv_sem=recv_sems.at[k], device_id=to, device_id_type=_MESH)

        mine = pltpu.make_async_copy(x_ref, block(*me), local_sem)
        mine.start()
        first = [copy(0, me, sibling, src=x_ref)]
        first += [copy(1 + j, me, (*chip, c), src=x_ref) for j, chip in enumerate(chips)]
        for cp in first:
            cp.start()
        passed = [copy(4 + j, (*chip, c), sibling) for j, chip in enumerate(chips)]
        for j, chip in enumerate(chips):
            copy(1 + j, (*chip, c), me).wait_recv()
            passed[j].start()
        copy(0, sibling, me).wait_recv()
        for j, chip in enumerate(chips):
            copy(4 + j, (*chip, 1 - c), me).wait_recv()
        for cp in first + passed:
            cp.wait_send()
        mine.wait()

    return _launch_exchange(
        body, name, shard, _sds(out_shape, shard.dtype),
        [pltpu.SemaphoreType.DMA((7,)), pltpu.SemaphoreType.DMA((7,)), pltpu.SemaphoreType.DMA(())],
        _sibling_and_chip_peers, ALL_GATHER_COLLECTIVE_ID, on_sequencer)


def _rs_sibling(g, axis, name, on_sequencer=False):
    m = g.shape[axis] // N_DEV
    blk_shape = list(g.shape)
    blk_shape[axis] = m
    n_chips = N_DEV // 2

    def body(g_ref, recv_ref, send_sems, recv_sems):
        x, y, c = lax.axis_index("x"), lax.axis_index("y"), lax.axis_index("c")
        sibling = (x, y, 1 - c)
        send = [pltpu.make_async_remote_copy(
            src_ref=_axis_slice(g_ref, axis, (2 * q + 1 - c) * m, m), dst_ref=recv_ref.at[q],
            send_sem=send_sems.at[q], recv_sem=recv_sems.at[q], device_id=sibling, device_id_type=_MESH)
            for q in range(n_chips)]
        for cp in send:
            cp.start()
        for cp in send:
            cp.wait_recv()
        for cp in send:
            cp.wait_send()

    return _launch_exchange(
        body, name, g, _sds([n_chips] + blk_shape, g.dtype),
        [pltpu.SemaphoreType.DMA((n_chips,)), pltpu.SemaphoreType.DMA((n_chips,))],
        lambda: _sibling_and_chip_peers()[:1], SIBLING_COLLECTIVE_ID, on_sequencer)


def _rs_chips(p, name, on_sequencer=False):
    n_peers = p.shape[0] - 1

    def body(p_ref, out_ref, send_sems, recv_sems):
        x, y, c = lax.axis_index("x"), lax.axis_index("y"), lax.axis_index("c")
        chips = [(1 - x, y), (x, 1 - y), (1 - x, 1 - y)]
        send = [pltpu.make_async_remote_copy(
            src_ref=p_ref.at[2 * cx + cy], dst_ref=out_ref.at[j], send_sem=send_sems.at[j],
            recv_sem=recv_sems.at[j], device_id=(cx, cy, c), device_id_type=_MESH)
            for j, (cx, cy) in enumerate(chips)]
        for cp in send:
            cp.start()
        for cp in send:
            cp.wait_recv()
        for cp in send:
            cp.wait_send()

    return _launch_exchange(
        body, name, p, _sds((n_peers,) + p.shape[1:], p.dtype),
        [pltpu.SemaphoreType.DMA((n_peers,)), pltpu.SemaphoreType.DMA((n_peers,))],
        lambda: _sibling_and_chip_peers()[1:], CHIPS_COLLECTIVE_ID, on_sequencer)


def _reduce_scatter(g, axis, name):
    sib = _rs_sibling(g, axis, name + "_d2d", on_sequencer=True)
    p = _pair_sum(g, sib, axis, name + "_pair")
    return p, _rs_chips(p, name + "_ici", on_sequencer=True)


def _s5_tables(lam_re, lam_im, log_step, b_re, b_im, c_re, c_im):
    nd, g, p, cg = b_re.shape
    step = jnp.exp(log_step)[..., None]
    mag = jnp.exp(lam_re * step)
    a_re, a_im = mag * jnp.cos(lam_im * step), mag * jnp.sin(lam_im * step)
    den = lam_re * lam_re + lam_im * lam_im
    num_re, num_im = a_re - 1.0, a_im
    k_re = (num_re * lam_re + num_im * lam_im) / den
    k_im = (num_im * lam_re - num_re * lam_im) / den
    bb_re = k_re[..., None] * b_re - k_im[..., None] * b_im
    bb_im = k_re[..., None] * b_im + k_im[..., None] * b_re
    gt = g // SSM_TILE_GROUPS
    hg = SSM_HALF_GROUPS
    eye = jnp.eye(SSM_TILE_GROUPS, dtype=F32).reshape(SSM_TILE_GROUPS, 2, hg)

    def pack_b(bb):
        w = jnp.einsum("djhqpc,ghq->djhgcqp", bb.reshape(nd, gt, 2, hg, p, cg), eye)
        return w.reshape(nd, gt * 2, SSM_TILE_GROUPS * cg, hg * p)

    def pack_c(cc):
        w = jnp.einsum("djhqcp,ghq->djhqpgc", cc.reshape(nd, gt, 2, hg, cg, p), eye)
        return w.reshape(nd, gt * 2, hg * p, SSM_TILE_GROUPS * cg)

    a = jnp.stack([a_re.reshape(nd, gt * 2, hg * p), a_im.reshape(nd, gt * 2, hg * p)], axis=2)
    return pack_b(bb_re), pack_b(bb_im), pack_c(c_re), pack_c(c_im), a


def _ret_tables(decay_logit, dk, dv):
    ch = RET_CHUNK
    nd, h = decay_logit.shape
    lg = jax.nn.log_sigmoid(decay_logit)[:, :, None]
    pos = jnp.arange(ch, dtype=F32)
    fwd_diff = pos[:, None] - pos[None, :]
    diff = jnp.stack([fwd_diff, -fwd_diff])[:, None]
    mask = jnp.stack([fwd_diff >= 0, -fwd_diff > 0])[:, None]
    end_pos = jnp.stack([ch - 1.0 - pos, pos])[:, None]
    in_pos = jnp.stack([pos + 1.0, ch - pos])[:, None]
    w_end = jnp.exp(lg * end_pos)
    w_in = jnp.exp(lg * in_pos)
    decay = jnp.where(mask, jnp.exp(lg[..., None] * jnp.where(mask, diff, 0.0)), 0.0)
    g_chunk = jnp.exp(lg[..., 0] * ch)
    return (decay, jnp.broadcast_to(w_end[..., None], (nd, h, ch, dk)), jnp.broadcast_to(w_in[..., None], (nd, h, ch, dk)),
            jnp.broadcast_to(g_chunk[..., None, None], (nd, h, dk, dv)))


def _rope_tables(t_rows, ncc, dk):
    quarter = dk // 4
    idx = np.arange(t_rows)
    row, col = idx // GRID_W, idx % GRID_W
    inv = ROPE_BASE ** (-np.arange(quarter, dtype=np.float32) / quarter)
    ang_r = row.astype(np.float32)[:, None] * inv
    ang_c = col.astype(np.float32)[:, None] * inv
    ang_r, ang_c = jnp.asarray(ang_r, F32), jnp.asarray(ang_c, F32)
    cos = jnp.concatenate([jnp.cos(ang_r), jnp.cos(ang_r), jnp.cos(ang_c), jnp.cos(ang_c)], axis=1)
    sin = jnp.concatenate([-jnp.sin(ang_r), jnp.sin(ang_r), -jnp.sin(ang_c), jnp.sin(ang_c)], axis=1)
    n_ctx = ncc * RET_CHUNK
    cos = jnp.concatenate([jnp.ones((n_ctx, dk), F32), cos], axis=0)
    sin = jnp.concatenate([jnp.zeros((n_ctx, dk), F32), sin], axis=0)
    return cos, sin


def _to_scan_layout(ctx_rows, lat_rows, rev):
    u = jnp.concatenate([lat_rows, ctx_rows] if rev else [ctx_rows, lat_rows], axis=0)
    r, w = u.shape
    return u.reshape(N_DEV, r // N_DEV, w).transpose(1, 0, 2).reshape(r, w)


def _from_scan_layout(yp, n_ctx, rev):
    r, w = yp.shape
    y = yp.reshape(r // N_DEV, N_DEV, w).transpose(1, 0, 2).reshape(r, w)
    return (y[r - n_ctx:], y[:r - n_ctx]) if rev else (y[:n_ctx], y[n_ctx:])


def _pack(parts, width):
    rows = []
    for p in parts:
        flat = p.reshape(-1).astype(F32)
        n = flat.shape[0]
        rows.append(jnp.pad(flat, (0, -n % (SUBLANE * width))).reshape(-1, width))
    return jnp.concatenate(rows, axis=0)


def _packed_rows(n, width):
    return -(-n // (SUBLANE * width)) * SUBLANE


def _unpack(flat2d, shapes):
    width = flat2d.shape[1]
    out, row = [], 0
    for shp in shapes:
        n = int(np.prod(shp))
        nr = _packed_rows(n, width)
        out.append(flat2d[row:row + nr].reshape(-1)[:n].reshape(shp))
        row += nr
    return out


def kernel(x, c, ctx, c_ctx, ada_w, ada_b, norm_g, ffn_w_in, ffn_w_out, mix_w_in, ssm_lam_re, ssm_lam_im, ssm_log_step, ssm_b_re, ssm_b_im, ssm_c_re, ssm_c_im, ssm_d, ssm_glu_w, ret_decay_logit, ret_w_proj, mix_w_out, loss_target, m_c_ctx, m_ada_w, m_ada_b, m_norm_g, m_ffn_w_in, m_ffn_w_out, m_mix_w_in, m_ssm_lam_re, m_ssm_lam_im, m_ssm_log_step, m_ssm_b_re, m_ssm_b_im, m_ssm_c_re, m_ssm_c_im, m_ssm_d, m_ssm_glu_w, m_ret_decay_logit, m_ret_w_proj, m_mix_w_out, v_c_ctx, v_ada_w, v_ada_b, v_norm_g, v_ffn_w_in, v_ffn_w_out, v_mix_w_in, v_ssm_lam_re, v_ssm_lam_im, v_ssm_log_step, v_ssm_b_re, v_ssm_b_im, v_ssm_c_re, v_ssm_c_im, v_ssm_d, v_ssm_glu_w, v_ret_decay_logit, v_ret_w_proj, v_mix_w_out):
    t_rows, d = x.shape[1], x.shape[2]
    n_ctx = ctx.shape[1]
    r = n_ctx + t_rows
    ssm_w = ssm_d.shape[1]
    heads = ret_decay_logit.shape[2]
    mi = mix_w_in.shape[2] * N_DEV
    dk = (mi - ssm_w - 2 * d) // (6 * heads)
    dv = 2 * dk
    qk_w, v_w = heads * dk, heads * dv
    q_off = ssm_w
    ncc = n_ctx // RET_CHUNK
    tile = n_ctx
    nct = 1
    wide_tile = _tile(n_ctx, 128, 16)
    assert r % (N_DEV * SUBLANE) == 0 and n_ctx % RET_CHUNK == 0 and t_rows % tile == 0
    me = 4 * lax.axis_index("x") + 2 * lax.axis_index("y") + lax.axis_index("c")
    g_off = ssm_w + 2 * qk_w + v_w
    gs_off = g_off + v_w

    ng_cols = norm_g.shape[2]
    small0 = _pack([c[0], norm_g[0]], d)
    small0_all = _all_gather(small0, 0, "ag_cond")

    bf = lambda w: w.astype(BF16)
    small0_all, sh_in1, sh_out1, sh_mix = lax.optimization_barrier(
        (small0_all, bf(ffn_w_in[0, 0]), bf(ffn_w_out[0, 0]), bf(mix_w_in[0])))
    small0_all = small0_all.reshape(N_DEV, -1)
    w_in1 = _all_gather(sh_in1, 1, "ag_ffn1_in", on_sequencer=True)
    w_out1 = _all_gather(sh_out1, 0, "ag_ffn1_out", on_sequencer=True)
    w_mix = _all_gather(sh_mix, 1, "ag_mix_in", on_sequencer=True)
    w_glu = _all_gather(bf(ssm_glu_w[0]), 1, "ag_glu", on_sequencer=True)
    w_rp = _all_gather(bf(ret_w_proj[0]), 0, "ag_ret_proj", on_sequencer=True)
    w_mo = _all_gather(bf(mix_w_out[0]), 0, "ag_mix_out", on_sequencer=True)
    w_in2 = _all_gather(bf(ffn_w_in[0, 1]), 1, "ag_ffn2_in", on_sequencer=True)
    w_out2 = _all_gather(bf(ffn_w_out[0, 1]), 0, "ag_ffn2_out", on_sequencer=True)

    ng_at = _packed_rows(d, d) * d
    c_all = small0_all[:, :d]
    g_full = small0_all[:, ng_at:ng_at + 6 * ng_cols].reshape(N_DEV, 6, ng_cols).transpose(1, 0, 2).reshape(6, d)
    g6 = g_full.reshape(6, 1, d)
    cc = jnp.concatenate([c_all, c_ctx[None, :], jnp.zeros((2 * SUBLANE - N_DEV - 1, d), F32)], axis=0)
    sc = _silu_rows(cc, "ada_silu")
    na = ada_w.shape[2]
    a_loc = _mm(sc, ada_w[0], "nn", F32, "ada_fwd", tm=16, tn=na, tk=512)
    a_all = _all_gather(a_loc, 0, "ag_ada").reshape(N_DEV, 2 * SUBLANE, na)
    ada_x = lax.dynamic_index_in_dim(a_all, me, axis=1, keepdims=False).reshape(9 * d) + ada_b[0]
    ada_c = a_all[:, N_DEV, :].reshape(9 * d) + ada_b[0]
    mods = jnp.stack([ada_c.reshape(9, d), ada_x.reshape(9, d)]).reshape(18, 1, d)

    xin = jnp.concatenate([ctx[0], x[0]], axis=0)
    u1 = _ada_pre_fwd(xin, g6, mods, 0, 0, nct, tile, "pre1")
    h1 = _mm(u1, w_in1, "nn", F32, "ffn1_in", tm=544)
    a1 = _swiglu_fwd(h1, wide_tile, "swiglu1")
    o1 = _mm(a1, w_out1, "nn", F32, "ffn1_out", tm=544, tn=d, tk=1408)
    x1 = _ada_post_fwd(xin, o1, g6, mods, 1, 0, 0.5, nct, tile, "post1")
    u2 = _ada_pre_fwd(x1, g6, mods, 2, 1, nct, tile, "pre2")
    hm = _mm(u2, w_mix, "nn", F32, "mix_in", tm=544)

    us_ctx, us_lat = hm[:n_ctx, :ssm_w], hm[n_ctx:, :ssm_w]
    dskip = ssm_d.reshape(1, 1, ssm_w)
    s5_prm = (ssm_lam_re[0], ssm_lam_im[0], ssm_log_step[0], ssm_b_re[0], ssm_b_im[0], ssm_c_re[0], ssm_c_im[0])
    s5_tabs_both, s5_vjp = jax.vjp(_s5_tables, *s5_prm)
    s5_tabs, ups, y_dirs = [], [], []
    for dr in range(2):
        tabs = tuple(t[dr] for t in s5_tabs_both)
        up = _to_scan_layout(us_ctx, us_lat, dr == 1)
        yp = _s5_fwd(up, *tabs, dr == 1, "s5_fwd%d" % dr)
        s5_tabs.append(tabs)
        ups.append(up)
        y_dirs.append(_from_scan_layout(yp, n_ctx, dr == 1)[1])
    a_ssm = _ssm_out_fwd(y_dirs[0], y_dirs[1], hm, dskip, nct, tile, "ssm_out")
    gab = _mm(a_ssm, w_glu, "nn", F32, "glu", tm=512, tn=2048, tk=ssm_w)

    cos, sin = _rope_tables(t_rows, ncc, dk)
    ret_tabs_both, ret_vjp = jax.vjp(functools.partial(_ret_tables, dk=dk, dv=dv), ret_decay_logit[0])
    ret_tabs, o_dirs, s_ins = [], [], []
    for dr in range(2):
        tabs = tuple(t[dr] for t in ret_tabs_both)
        o_d, s_in = _ret_fwd(hm, cos, sin, *tabs, heads, dk, dv, q_off, ncc, dr == 1, "ret_fwd%d" % dr)
        ret_tabs.append(tabs)
        o_dirs.append(o_d)
        s_ins.append(s_in)
    ret_in = _ret_gate_fwd(o_dirs[0], o_dirs[1], hm, g_off, heads, dv, nct, tile, "ret_gate")
    rb = _mm(ret_in, w_rp, "nn", F32, "ret_proj", tm=512, tn=d, tk=v_w)
    merged = _merge_fwd(gab, rb, hm, gs_off, nct, tile, "merge")
    mix = _mm(merged, w_mo, "nn", F32, "mix_out", tm=512, tn=d, tk=d)
    x1x = x1[n_ctx:]
    x2 = _ada_post_fwd(x1x, mix, g6, mods, 3, 1, 1.0, 0, tile, "post2")
    u3 = _ada_pre_fwd(x2, g6, mods, 4, 2, 0, tile, "pre3")
    h3 = _mm(u3, w_in2, "nn", F32, "ffn2_in", tm=512)
    a3 = _swiglu_fwd(h3, wide_tile, "swiglu2")
    o3 = _mm(a3, w_out2, "nn", F32, "ffn2_out", tm=512, tn=d, tk=1408)
    x3 = _ada_post_fwd(x2, o3, g6, mods, 5, 2, 0.5, 0, tile, "post3")
    dy, lcols = _loss_grad(x3, loss_target[0], tile, "loss")
    loss_part = (0.5 * jnp.sum(lcols) / d).reshape(1)

    dg6 = [None] * 6
    dmod = {}

    def add_mod(sel_rows, k, val):
        for sel, row in sel_rows:
            dmod[(sel, k)] = dmod.get((sel, k), 0.0) + val[row, 0]

    both, lat = [(0, 0), (1, 1)], [(1, 0)]

    def tie(*vals):
        return lax.optimization_barrier(vals)

    def big_update(w2d, m2d, v2d, gfull, axis, name):
        p, recv = _reduce_scatter(gfull, axis, "rs_" + name)
        return _adamw_scattered(w2d, m2d, v2d, p, recv, "adamw_" + name)

    do3, dg6[5], dgt = _ada_post_bwd(dy, o3, g6, mods, 5, 2, 0.5, 0, 1, tile, "post3_bwd")
    add_mod(lat, 8, dgt)
    gw_out2 = _mm(a3, do3, "tn", BF16, "ffn2_out_dw", tm=1408, tn=1024, tk=2176)
    do3, gw_out2 = tie(do3, gw_out2)
    up_out2 = big_update(ffn_w_out[0, 1], m_ffn_w_out[0, 1], v_ffn_w_out[0, 1], gw_out2, 0, "ffn2_out")
    da3 = _mm(do3, w_out2, "nt", F32, "ffn2_out_dx", tm=512, tn=1408, tk=d)
    dh3 = _swiglu_bwd(h3, da3, wide_tile, "swiglu2_bwd")
    gw_in2 = _mm(u3, dh3, "tn", BF16, "ffn2_in_dw", tm=1024, tn=1408, tk=2176)
    dh3, gw_in2 = tie(dh3, gw_in2)
    up_in2 = big_update(ffn_w_in[0, 1], m_ffn_w_in[0, 1], v_ffn_w_in[0, 1], gw_in2, 1, "ffn2_in")
    du3 = _mm(dh3, w_in2, "nt", F32, "ffn2_in_dx", tm=512, tn=d, tk=1408)
    dx2, dg6[4], dsh, dsc = _ada_pre_bwd(x2, du3, dy, g6, mods, 4, 2, 0, 1, tile, "pre3_bwd")
    add_mod(lat, 6, dsh)
    add_mod(lat, 7, dsc)
    dmix, dg6[3], dgt = _ada_post_bwd(dx2, mix, g6, mods, 3, 1, 1.0, 0, 1, tile, "post2_bwd")
    add_mod(lat, 5, dgt)
    gw_mo = _mm(merged, dmix, "tn", BF16, "mix_out_dw", tm=1024, tn=1024, tk=2176)
    dmix, gw_mo = tie(dmix, gw_mo)
    up_mo = big_update(mix_w_out[0], m_mix_w_out[0], v_mix_w_out[0], gw_mo, 0, "mix_out")
    dmerged = _mm(dmix, w_mo, "nt", F32, "mix_out_dx", tm=512, tn=d, tk=d)
    dgab, drb, dgs, dgr = _merge_bwd(gab, rb, hm, gs_off, dmerged, nct, tile, "merge_bwd")
    gw_glu = _mm(a_ssm, dgab, "tn", BF16, "glu_dw", tm=1024, tn=1024, tk=2176)
    gw_rp = _mm(ret_in, drb, "tn", BF16, "ret_proj_dw", tm=1024, tn=1024, tk=2176)
    dgab, drb, gw_glu, gw_rp = tie(dgab, drb, gw_glu, gw_rp)
    up_glu = big_update(ssm_glu_w[0], m_ssm_glu_w[0], v_ssm_glu_w[0], gw_glu, 1, "glu")
    up_rp = big_update(ret_w_proj[0], m_ret_w_proj[0], v_ret_w_proj[0], gw_rp, 0, "ret_proj")
    da_ssm = _mm(dgab, w_glu, "nt", F32, "glu_dx", tm=512, tn=ssm_w, tk=2 * d)
    dret_in = _mm(drb, w_rp, "nt", F32, "ret_proj_dx", tm=512, tn=v_w, tk=d)
    d_o, dg_gate = _ret_gate_bwd(o_dirs[0], o_dirs[1], hm, g_off, dret_in, heads, dv, nct, tile, "ret_gate_bwd")
    dy_ssm, dus_direct, d_dskip = _ssm_out_bwd(y_dirs[0], y_dirs[1], hm, dskip, da_ssm, nct, tile, "ssm_out_bwd")
    s5_table_grads, du_ctx, du_lat = [], [], [dus_direct]
    for dr in range(2):
        dyp = _to_scan_layout(jnp.zeros((n_ctx, ssm_w), F32), dy_ssm, dr == 1)
        if dr == 1:
            dyp, up_out2, up_in2 = tie(dyp, up_out2, up_in2)
        outs = _s5_bwd(ups[dr], dyp, *s5_tabs[dr], dr == 1, "s5_bwd%d" % dr)
        part_ctx, part_lat = _from_scan_layout(outs[0], n_ctx, dr == 1)
        du_ctx.append(part_ctx)
        du_lat.append(part_lat)
        s5_table_grads.append(outs[1:])
    dqkv, ret_table_grads = [], []
    for dr in range(2):
        if dr == 1:
            d_o, up_mo, up_glu, up_rp = tie(d_o, up_mo, up_glu, up_rp)
        outs = _ret_bwd(hm, cos, sin, *ret_tabs[dr], s_ins[dr], d_o, heads, dk, dv, q_off, ncc, dr == 1,
                        "ret_bwd%d" % dr)
        dqkv.append(outs[:3])
        ret_table_grads.append(outs[3:])
    both_dirs = lambda grads: tuple(jnp.stack([g0, g1]) for g0, g1 in zip(*grads))
    early_parts = list(s5_vjp(both_dirs(s5_table_grads))) + list(ret_vjp(both_dirs(ret_table_grads)))
    s5_names = 7
    early_shapes = [p.shape for p in early_parts]
    early_all = _all_gather(_pack(early_parts, 1024), 0, "ag_s5_grads", on_sequencer=True)
    early_sum = _sum_leading(early_all.reshape(N_DEV, -1, 1024), "sum_s5_grads")
    dus = jnp.concatenate([du_ctx[0] + du_ctx[1], du_lat[0] + du_lat[1] + du_lat[2]], axis=0)
    dhm = _assemble_dhm(dus, dqkv[0][0], dqkv[1][0], dqkv[0][1], dqkv[1][1], dqkv[0][2], dqkv[1][2],
                        dg_gate, dgs, dgr, n_ctx // wide_tile, wide_tile, "assemble_dhm")
    gw_mix = _mm(u2, dhm, "tn", BF16, "mix_in_dw", tm=1024, tn=1408, tk=2176)
    dhm, gw_mix = tie(dhm, gw_mix)
    up_mix = big_update(mix_w_in[0], m_mix_w_in[0], v_mix_w_in[0], gw_mix, 1, "mix_in")
    du2 = _mm(dhm, w_mix, "nt", F32, "mix_in_dx", tm=544, tn=d, tk=1408)
    dx1, dg6[2], dsh, dsc = _ada_pre_bwd(x1, du2, dx2, g6, mods, 2, 1, nct, 2, tile, "pre2_bwd", dres_x_only=True)
    add_mod(both, 3, dsh)
    add_mod(both, 4, dsc)
    do1, dg6[1], dgt = _ada_post_bwd(dx1, o1, g6, mods, 1, 0, 0.5, nct, 2, tile, "post1_bwd")
    add_mod(both, 2, dgt)
    gw_out1 = _mm(a1, do1, "tn", BF16, "ffn1_out_dw", tm=1408, tn=1024, tk=2176)
    do1, gw_out1 = tie(do1, gw_out1)
    up_out1 = big_update(ffn_w_out[0, 0], m_ffn_w_out[0, 0], v_ffn_w_out[0, 0], gw_out1, 0, "ffn1_out")
    da1 = _mm(do1, w_out1, "nt", F32, "ffn1_out_dx", tm=544, tn=1408, tk=d)
    dh1 = _swiglu_bwd(h1, da1, wide_tile, "swiglu1_bwd")
    dh1, up_mix, early_sum = tie(dh1, up_mix, early_sum)
    early_sums = _unpack(early_sum, early_shapes)
    gw_in1 = _mm(u1, dh1, "tn", BF16, "ffn1_in_dw", tm=1024, tn=1408, tk=2176)
    dh1, gw_in1 = tie(dh1, gw_in1)
    up_in1 = big_update(ffn_w_in[0, 0], m_ffn_w_in[0, 0], v_ffn_w_in[0, 0], gw_in1, 1, "ffn1_in")
    du1 = _mm(dh1, w_in1, "nt", F32, "ffn1_in_dx", tm=544, tn=d, tk=1408)
    dxin, dg6[0], dsh, dsc = _ada_pre_bwd(xin, du1, dx1, g6, mods, 0, 0, nct, 2, tile, "pre1_bwd")
    add_mod(both, 0, dsh)
    add_mod(both, 1, dsc)
    grad_x = dxin[n_ctx:][None]

    zero_d = jnp.zeros((d,), F32)
    d_ada_x = jnp.stack([dmod.get((1, k), zero_d) for k in range(9)]).reshape(9 * d)
    d_ada_c = jnp.stack([dmod.get((0, k), zero_d) for k in range(9)]).reshape(9 * d)
    dg_full = jnp.stack([g[0, 0] for g in dg6])
    small_parts = [d_ada_x, d_ada_c, dg_full, d_dskip, loss_part]
    small_shapes = [p.shape for p in small_parts]
    packed = _pack(small_parts, 1024)
    gathered = _all_gather(packed, 0, "ag_small_grads").reshape(N_DEV, -1, 1024)
    summed = _sum_leading(gathered, "sum_small_grads")
    sums = _unpack(summed, small_shapes)
    sum_dx, sum_dc, sum_dg = sums[0], sums[1], sums[2]
    loss = sums[4][0]
    grad_ada_b = (sum_dx + sum_dc)[None]
    dx_rows = gathered.reshape(N_DEV, -1)[:, :9 * d]
    col0 = me * na
    da_rows = jnp.concatenate([lax.dynamic_slice_in_dim(dx_rows, col0, na, axis=1),
                               lax.dynamic_slice_in_dim(sum_dc[None], col0, na, axis=1),
                               jnp.zeros((2 * SUBLANE - N_DEV - 1, na), F32)], axis=0)
    grad_ada_w = _mm(sc, da_rows, "tn", F32, "ada_dw", tm=512, tn=na, tk=16)
    d_sc = _mm(da_rows, ada_w[0], "nt", F32, "ada_dx", tm=16, tn=512, tk=na)
    d_sc_all = _all_gather(jnp.broadcast_to(d_sc[N_DEV:N_DEV + 1], (SUBLANE, d)), 0, "ag_dctx")
    d_sc_sum = _sum_leading(d_sc_all.reshape(N_DEV, SUBLANE, d), "sum_dctx")
    grad_c_ctx = _silu_grad_rows(jnp.broadcast_to(c_ctx[None], (SUBLANE, d)), d_sc_sum, "ctx_silu_bwd")[0]
    grad_norm_g = lax.dynamic_slice_in_dim(sum_dg, me * ng_cols, ng_cols, axis=1)[None]

    upd = {}
    upd["ffn_w_in"] = [jnp.stack([up_in1[i], up_in2[i]])[None] for i in range(4)]
    upd["ffn_w_out"] = [jnp.stack([up_out1[i], up_out2[i]])[None] for i in range(4)]
    upd["mix_w_in"] = [o[None] for o in up_mix]
    upd["ssm_glu_w"] = [o[None] for o in up_glu]
    upd["ret_w_proj"] = [o[None] for o in up_rp]
    upd["mix_w_out"] = [o[None] for o in up_mo]
    upd["ada_w"] = [o[None] for o in _adamw(ada_w[0], m_ada_w[0], v_ada_w[0], grad_ada_w[None], "adamw_ada_w")]

    small_names = ["c_ctx", "ada_b", "norm_g", "ssm_lam_re", "ssm_lam_im", "ssm_log_step", "ssm_b_re", "ssm_b_im",
                   "ssm_c_re", "ssm_c_im", "ssm_d", "ret_decay_logit"]
    small_w = [c_ctx, ada_b, norm_g, ssm_lam_re, ssm_lam_im, ssm_log_step, ssm_b_re, ssm_b_im, ssm_c_re, ssm_c_im,
               ssm_d, ret_decay_logit]
    small_m = [m_c_ctx, m_ada_b, m_norm_g, m_ssm_lam_re, m_ssm_lam_im, m_ssm_log_step, m_ssm_b_re, m_ssm_b_im,
               m_ssm_c_re, m_ssm_c_im, m_ssm_d, m_ret_decay_logit]
    small_v = [v_c_ctx, v_ada_b, v_norm_g, v_ssm_lam_re, v_ssm_lam_im, v_ssm_log_step, v_ssm_b_re, v_ssm_b_im,
               v_ssm_c_re, v_ssm_c_im, v_ssm_d, v_ret_decay_logit]
    small_g = [grad_c_ctx, grad_ada_b, grad_norm_g] + [s[None] for s in early_sums[:s5_names]] + \
              [sums[3].reshape(ssm_d.shape), early_sums[s5_names][None]]
    shapes = [w.shape for w in small_w]
    res = _adamw(_pack(small_w, 1024), _pack(small_m, 1024), _pack(small_v, 1024), _pack(small_g, 1024)[None],
                 "adamw_small")
    small_out = [_unpack(o, shapes) for o in res]
    for i, nm in enumerate(small_names):
        upd[nm] = [small_out[kind][i] for kind in range(4)]

    order = ["c_ctx", "ada_w", "ada_b", "norm_g", "ffn_w_in", "ffn_w_out", "mix_w_in", "ssm_lam_re", "ssm_lam_im",
             "ssm_log_step", "ssm_b_re", "ssm_b_im", "ssm_c_re", "ssm_c_im", "ssm_d", "ssm_glu_w", "ret_decay_logit",
             "ret_w_proj", "mix_w_out"]
    outs = [loss, grad_x]
    for kind in range(4):
        outs += [upd[nm][kind] for nm in order]
    return tuple(outs)
```

```python
import functools
import math

import jax
import jax.numpy as jnp
import numpy as np
from jax import lax
from jax.experimental import pallas as pl
from jax.experimental.pallas import tpu as pltpu
from jax.experimental.pallas import tpu_sc as plsc

F32 = jnp.float32
BF16 = jnp.bfloat16
MXU_DTYPE = jnp.bfloat16
MESH_AXES = ("x", "y", "c")
N_DEV = 8
V7X_VMEM_LIMIT_BYTES = 56 * 1024 * 1024
LANE = 128
SUBLANE = 8

GRID_W = 64
RET_CHUNK = 128
ROPE_BASE = 10000.0
NORM_EPS = 1e-6
ADAM_LR = 0.001
ADAM_B1 = 0.9
ADAM_B2 = 0.999
ADAM_EPS = 1e-08
ADAM_WD = 0.01
ADAM_STEP = 10
SSM_TILE_GROUPS = 8
SSM_HALF_GROUPS = 4


def _params(sem=None):
    return pltpu.CompilerParams(dimension_semantics=sem, vmem_limit_bytes=V7X_VMEM_LIMIT_BYTES)


def _tile(n, target, mult):
    best = None
    t = mult
    while t <= min(n, target):
        if n % t == 0:
            best = t
        t += mult
    return n if best is None else best


def _sds(shape, dtype):
    return jax.ShapeDtypeStruct(tuple(shape), dtype)


def _mm(a, b, dims, out_dtype, name, tm=512, tn=1408, tk=2048):
    if dims == "nn":
        (m, k), (k2, n) = a.shape, b.shape
    elif dims == "nt":
        (m, k), (n, k2) = a.shape, b.shape
    else:
        (k, m), (k2, n) = a.shape, b.shape
    assert k == k2, (a.shape, b.shape, dims)
    tm = _tile(m, tm, 16)
    tn = _tile(n, tn, LANE)
    tk = _tile(k, tk, LANE if dims != "tn" else 16)
    nk = k // tk
    dn = {"nn": (((1,), (0,)), ((), ())), "nt": (((1,), (1,)), ((), ())), "tn": (((0,), (0,)), ((), ()))}[dims]

    def product(a_ref, b_ref):
        return lax.dot_general(a_ref[...].astype(MXU_DTYPE), b_ref[...].astype(MXU_DTYPE), dn,
                               preferred_element_type=F32)

    def body_single(a_ref, b_ref, o_ref):
        o_ref[...] = product(a_ref, b_ref).astype(o_ref.dtype)

    def body(a_ref, b_ref, o_ref, acc_ref):
        kk = pl.program_id(2)

        @pl.when(kk == 0)
        def _():
            acc_ref[...] = product(a_ref, b_ref)

        @pl.when((kk > 0) & (kk < nk - 1))
        def _():
            acc_ref[...] += product(a_ref, b_ref)

        @pl.when(kk == nk - 1)
        def _():
            o_ref[...] = (acc_ref[...] + product(a_ref, b_ref)).astype(o_ref.dtype)

    if dims == "nn":
        a_spec = pl.BlockSpec((tm, tk), lambda j, i, kk: (i, kk))
        b_spec = pl.BlockSpec((tk, tn), lambda j, i, kk: (kk, j))
    elif dims == "nt":
        a_spec = pl.BlockSpec((tm, tk), lambda j, i, kk: (i, kk))
        b_spec = pl.BlockSpec((tn, tk), lambda j, i, kk: (j, kk))
    else:
        a_spec = pl.BlockSpec((tk, tm), lambda j, i, kk: (kk, i))
        b_spec = pl.BlockSpec((tk, tn), lambda j, i, kk: (kk, j))
    return pl.pallas_call(
        body_single if nk == 1 else body, name=name, grid=(n // tn, m // tm, nk), in_specs=[a_spec, b_spec],
        out_specs=pl.BlockSpec((tm, tn), lambda j, i, kk: (i, j)), out_shape=_sds((m, n), out_dtype),
        scratch_shapes=[] if nk == 1 else [pltpu.VMEM((tm, tn), F32)],
        compiler_params=_params(("parallel", "parallel", "arbitrary")))(a, b)


def _rows(name, body, n_tiles, ins, outs):
    in_specs = [pl.BlockSpec(blk, imap) for (_, blk, imap) in ins]
    out_specs = [pl.BlockSpec(blk, imap) for (_, _, blk, imap) in outs]
    out_shape = [_sds(shape, dt) for (shape, dt, _, _) in outs]
    res = pl.pallas_call(body, name=name, grid=(n_tiles,), in_specs=in_specs, out_specs=out_specs,
                         out_shape=out_shape, compiler_params=_params(("arbitrary",)))(*[a for (a, _, _) in ins])
    return res


def _row_in(arr, tile, width=None, col=0, x_only_offset=None):
    width = arr.shape[1] if width is None else width
    if x_only_offset is None:
        return (arr, (tile, width), lambda i: (i, col))
    return (arr, (tile, width), lambda i: (jnp.maximum(i - x_only_offset, 0), col))


def _vec_in(arr, idx_fn):
    return (arr, (1, 1, arr.shape[2]), lambda i: (idx_fn(i), 0, 0))


def _rms(h):
    return lax.rsqrt(jnp.mean(h * h, axis=-1, keepdims=True) + NORM_EPS)


def _sigmoid(z):
    return 1.0 / (1.0 + jnp.exp(-z))


def _ada_pre_fwd(h, g6, mods, gi, mi, nct, tile, name):
    r, d = h.shape
    sel = lambda i: jnp.where(i >= nct, 1, 0)

    def body(h_ref, g_ref, sh_ref, sc_ref, u_ref):
        hh = h_ref[...]
        n = hh * _rms(hh) * g_ref[0]
        u_ref[...] = (n * (1.0 + sc_ref[0]) + sh_ref[0]).astype(u_ref.dtype)

    (u,) = _rows(name, body, r // tile,
                 [_row_in(h, tile), _vec_in(g6, lambda i: gi), _vec_in(mods, lambda i: sel(i) * 9 + 3 * mi),
                  _vec_in(mods, lambda i: sel(i) * 9 + 3 * mi + 1)],
                 [((r, d), BF16, (tile, d), lambda i: (i, 0))])
    return u


def _ada_pre_bwd(h, du, dres, g6, mods, gi, mi, nct, nsel, tile, name, dres_x_only=False):
    r, d = h.shape
    sel = lambda i: jnp.where(i >= nct, 1, 0) if nsel == 2 else 0
    msel = lambda i: jnp.where(i >= nct, 1, 0)
    off = nct if dres_x_only else None

    def body(h_ref, du_ref, dr_ref, g_ref, sc_ref, dh_ref, dg_ref, dsh_ref, dsc_ref):
        i = pl.program_id(0)
        hh = h_ref[...]
        rr = _rms(hh)
        g = g_ref[0]
        hn = hh * rr
        n = hn * g
        du_ = du_ref[...].astype(F32)
        dn = du_ * (1.0 + sc_ref[0])

        @pl.when(i == 0)
        def _():
            dg_ref[...] = jnp.zeros_like(dg_ref)

        @pl.when((i == 0) | (i == nct))
        def _():
            dsh_ref[...] = jnp.zeros_like(dsh_ref)
            dsc_ref[...] = jnp.zeros_like(dsc_ref)

        dg_ref[0] += jnp.sum(dn * hn, axis=0, keepdims=True)
        dsh_ref[0] += jnp.sum(du_, axis=0, keepdims=True)
        dsc_ref[0] += jnp.sum(du_ * n, axis=0, keepdims=True)
        t = dn * g
        dh = rr * t - hn * (rr * jnp.mean(t * hn, axis=-1, keepdims=True))
        if dres_x_only:
            dh_ref[...] = dh + jnp.where(i >= nct, dr_ref[...], 0.0)
        else:
            dh_ref[...] = dh + dr_ref[...]

    dh, dg, dsh, dsc = _rows(
        name, body, r // tile,
        [_row_in(h, tile), _row_in(du, tile), _row_in(dres, tile, x_only_offset=off), _vec_in(g6, lambda i: gi),
         _vec_in(mods, lambda i: msel(i) * 9 + 3 * mi + 1)],
        [((r, d), F32, (tile, d), lambda i: (i, 0)), ((1, 1, d), F32, (1, 1, d), lambda i: (0, 0, 0)),
         ((nsel, 1, d), F32, (1, 1, d), lambda i: (sel(i), 0, 0)),
         ((nsel, 1, d), F32, (1, 1, d), lambda i: (sel(i), 0, 0))])
    return dh, dg, dsh, dsc


def _ada_post_fwd(h, o, g6, mods, gi, mi, res_w, nct, tile, name, h_x_only=False):
    r, d = o.shape
    sel = lambda i: jnp.where(i >= nct, 1, 0)

    def body(h_ref, o_ref, g_ref, gt_ref, y_ref):
        oo = o_ref[...]
        n = oo * _rms(oo) * g_ref[0]
        y_ref[...] = h_ref[...] + res_w * gt_ref[0] * n

    (y,) = _rows(name, body, r // tile,
                 [_row_in(h, tile), _row_in(o, tile), _vec_in(g6, lambda i: gi),
                  _vec_in(mods, lambda i: sel(i) * 9 + 3 * mi + 2)],
                 [((r, d), F32, (tile, d), lambda i: (i, 0))])
    return y


def _ada_post_bwd(dy, o, g6, mods, gi, mi, res_w, nct, nsel, tile, name):
    r, d = o.shape
    sel = lambda i: jnp.where(i >= nct, 1, 0) if nsel == 2 else 0
    msel = lambda i: jnp.where(i >= nct, 1, 0)

    def body(dy_ref, o_ref, g_ref, gt_ref, do_ref, dg_ref, dgt_ref):
        i = pl.program_id(0)
        oo = o_ref[...]
        rr = _rms(oo)
        g = g_ref[0]
        on = oo * rr
        dy_ = dy_ref[...] * res_w

        @pl.when(i == 0)
        def _():
            dg_ref[...] = jnp.zeros_like(dg_ref)

        @pl.when((i == 0) | (i == nct))
        def _():
            dgt_ref[...] = jnp.zeros_like(dgt_ref)

        dgt_ref[0] += jnp.sum(dy_ * (on * g), axis=0, keepdims=True)
        dn = dy_ * gt_ref[0]
        dg_ref[0] += jnp.sum(dn * on, axis=0, keepdims=True)
        t = dn * g
        do_ref[...] = (rr * t - on * (rr * jnp.mean(t * on, axis=-1, keepdims=True))).astype(do_ref.dtype)

    do, dg, dgt = _rows(
        name, body, r // tile,
        [_row_in(dy, tile), _row_in(o, tile), _vec_in(g6, lambda i: gi),
         _vec_in(mods, lambda i: msel(i) * 9 + 3 * mi + 2)],
        [((r, d), BF16, (tile, d), lambda i: (i, 0)), ((1, 1, d), F32, (1, 1, d), lambda i: (0, 0, 0)),
         ((nsel, 1, d), F32, (1, 1, d), lambda i: (sel(i), 0, 0))])
    return do, dg, dgt


def _swiglu_fwd(h, tile, name):
    r, w2 = h.shape
    f = w2 // 2

    def body(h_ref, a_ref):
        gt = h_ref[:, :f]
        up = h_ref[:, f:]
        a_ref[...] = (gt * _sigmoid(gt) * up).astype(a_ref.dtype)

    (a,) = _rows(name, body, r // tile, [_row_in(h, tile)], [((r, f), BF16, (tile, f), lambda i: (i, 0))])
    return a


def _swiglu_bwd(h, da, tile, name):
    r, w2 = h.shape
    f = w2 // 2

    def body(h_ref, da_ref, dh_ref):
        gt = h_ref[:, :f]
        up = h_ref[:, f:]
        d = da_ref[...]
        sg = _sigmoid(gt)
        dh_ref[:, :f] = (d * up * (sg * (1.0 + gt * (1.0 - sg)))).astype(dh_ref.dtype)
        dh_ref[:, f:] = (d * gt * sg).astype(dh_ref.dtype)

    (dh,) = _rows(name, body, r // tile, [_row_in(h, tile), _row_in(da, tile)],
                  [((r, w2), BF16, (tile, w2), lambda i: (i, 0))])
    return dh


def _gelu_parts(y):
    c0 = math.sqrt(2.0 / math.pi)
    inner = c0 * (y + 0.044715 * y * y * y)
    th = jnp.tanh(inner)
    return th, c0 * (1.0 + 3 * 0.044715 * y * y)


def _ssm_out_fwd(y0, y1, hm, dskip, nct, tile, name):
    t_rows, s = y0.shape

    def body(y0_ref, y1_ref, u_ref, d_ref, a_ref):
        y = y0_ref[...] + y1_ref[...] + d_ref[0] * u_ref[...]
        th, _ = _gelu_parts(y)
        a_ref[...] = (0.5 * y * (1.0 + th)).astype(a_ref.dtype)

    (a,) = _rows(name, body, t_rows // tile,
                 [_row_in(y0, tile), _row_in(y1, tile), (hm, (tile, s), lambda i: (i + nct, 0)),
                  _vec_in(dskip, lambda i: 0)],
                 [((t_rows, s), BF16, (tile, s), lambda i: (i, 0))])
    return a


def _ssm_out_bwd(y0, y1, hm, dskip, da, nct, tile, name):
    t_rows, s = y0.shape

    def body(y0_ref, y1_ref, u_ref, d_ref, da_ref, dy_ref, du_ref, dd_ref):
        i = pl.program_id(0)
        u = u_ref[...]
        y = y0_ref[...] + y1_ref[...] + d_ref[0] * u
        th, dinner = _gelu_parts(y)
        dy = da_ref[...] * (0.5 * (1.0 + th) + 0.5 * y * (1.0 - th * th) * dinner)
        dy_ref[...] = dy
        du_ref[...] = dy * d_ref[0]

        @pl.when(i == 0)
        def _():
            dd_ref[...] = jnp.zeros_like(dd_ref)

        dd_ref[0] += jnp.sum(dy * u, axis=0, keepdims=True)

    dy, du, dd = _rows(name, body, t_rows // tile,
                       [_row_in(y0, tile), _row_in(y1, tile), (hm, (tile, s), lambda i: (i + nct, 0)),
                        _vec_in(dskip, lambda i: 0), _row_in(da, tile)],
                       [((t_rows, s), F32, (tile, s), lambda i: (i, 0)), ((t_rows, s), F32, (tile, s), lambda i: (i, 0)),
                        ((1, 1, s), F32, (1, 1, s), lambda i: (0, 0, 0))])
    return dy, du, dd


def _col_pieces(arr, off, width, tile, nct, unit=None):
    pw = math.gcd(off, width if unit is None else unit)
    specs = [(arr, (tile, pw), functools.partial(lambda i, cb: (i + nct, cb), cb=off // pw + p))
             for p in range(width // pw)]
    return specs, pw


def _ret_gate_fwd(o0, o1, hm, g_off, heads, dv, nct, tile, name):
    t_rows, w = o0.shape
    g_specs, pw = _col_pieces(hm, g_off, w, tile, nct)
    ng = len(g_specs)

    def body(o0_ref, o1_ref, *refs):
        g_refs, r_ref = refs[:ng], refs[ng]
        for hd in range(heads):
            cs = slice(hd * dv, (hd + 1) * dv)
            o = o0_ref[:, cs] + o1_ref[:, cs]
            lo = (hd * dv) % pw
            g = g_refs[(hd * dv) // pw][:, lo:lo + dv]
            r_ref[:, cs] = (g * _sigmoid(g) * (o * _rms(o))).astype(r_ref.dtype)

    (ri,) = _rows(name, body, t_rows // tile, [_row_in(o0, tile), _row_in(o1, tile)] + g_specs,
                  [((t_rows, w), BF16, (tile, w), lambda i: (i, 0))])
    return ri


def _ret_gate_bwd(o0, o1, hm, g_off, dri, heads, dv, nct, tile, name):
    t_rows, w = o0.shape
    g_specs, pw = _col_pieces(hm, g_off, w, tile, nct)
    ng = len(g_specs)

    def body(o0_ref, o1_ref, d_ref, *refs):
        g_refs, do_ref, dg_ref = refs[:ng], refs[ng], refs[ng + 1]
        for hd in range(heads):
            cs = slice(hd * dv, (hd + 1) * dv)
            o = o0_ref[:, cs] + o1_ref[:, cs]
            lo = (hd * dv) % pw
            g = g_refs[(hd * dv) // pw][:, lo:lo + dv]
            d = d_ref[:, cs]
            rr = _rms(o)
            on = o * rr
            sg = _sigmoid(g)
            dg_ref[:, cs] = d * on * (sg * (1.0 + g * (1.0 - sg)))
            t = d * (g * sg)
            do_ref[:, cs] = rr * t - on * (rr * jnp.mean(t * on, axis=-1, keepdims=True))

    do, dg = _rows(name, body, t_rows // tile, [_row_in(o0, tile), _row_in(o1, tile), _row_in(dri, tile)] + g_specs,
                   [((t_rows, w), F32, (tile, w), lambda i: (i, 0)), ((t_rows, w), F32, (tile, w), lambda i: (i, 0))])
    return do, dg


def _merge_fwd(gab, rb, hm, gs_off, nct, tile, name):
    t_rows, d = rb.shape
    specs, pw = _col_pieces(hm, gs_off, 2 * d, tile, nct, unit=d)
    npc = d // pw

    def body(gab_ref, rb_ref, *refs):
        gs_refs, gr_refs, m_ref = refs[:npc], refs[npc:2 * npc], refs[2 * npc]
        for p in range(npc):
            cs = slice(p * pw, (p + 1) * pw)
            ga = gab_ref[:, cs]
            gb = gab_ref[:, d + p * pw:d + (p + 1) * pw]
            m_ref[:, cs] = (_sigmoid(gs_refs[p][...]) * (ga * _sigmoid(gb))
                            + _sigmoid(gr_refs[p][...]) * rb_ref[:, cs]).astype(m_ref.dtype)

    (mg,) = _rows(name, body, t_rows // tile, [_row_in(gab, tile), _row_in(rb, tile)] + specs,
                  [((t_rows, d), BF16, (tile, d), lambda i: (i, 0))])
    return mg


def _merge_bwd(gab, rb, hm, gs_off, dm, nct, tile, name):
    t_rows, d = rb.shape
    specs, pw = _col_pieces(hm, gs_off, 2 * d, tile, nct, unit=d)
    npc = d // pw

    def body(gab_ref, rb_ref, dm_ref, *refs):
        gs_refs, gr_refs = refs[:npc], refs[npc:2 * npc]
        dgab_ref, drb_ref, dgs_ref, dgr_ref = refs[2 * npc:]
        for p in range(npc):
            cs = slice(p * pw, (p + 1) * pw)
            cs2 = slice(d + p * pw, d + (p + 1) * pw)
            ga = gab_ref[:, cs]
            gb = gab_ref[:, cs2]
            dmm = dm_ref[:, cs]
            ss = _sigmoid(gs_refs[p][...])
            sr = _sigmoid(gr_refs[p][...])
            sb = _sigmoid(gb)
            dbr = dmm * ss
            dgab_ref[:, cs] = (dbr * sb).astype(dgab_ref.dtype)
            dgab_ref[:, cs2] = (dbr * ga * sb * (1.0 - sb)).astype(dgab_ref.dtype)
            drb_ref[:, cs] = (dmm * sr).astype(drb_ref.dtype)
            dgs_ref[:, cs] = dmm * (ga * sb) * ss * (1.0 - ss)
            dgr_ref[:, cs] = dmm * rb_ref[:, cs] * sr * (1.0 - sr)

    return _rows(name, body, t_rows // tile, [_row_in(gab, tile), _row_in(rb, tile), _row_in(dm, tile)] + specs,
                 [((t_rows, 2 * d), BF16, (tile, 2 * d), lambda i: (i, 0)), ((t_rows, d), BF16, (tile, d), lambda i: (i, 0)),
                  ((t_rows, d), F32, (tile, d), lambda i: (i, 0)), ((t_rows, d), F32, (tile, d), lambda i: (i, 0))])


def _assemble_dhm(dus, dq0, dq1, dk0, dk1, dv0, dv1, dg, dgs, dgr, nct, tile, name):
    r, s = dus.shape
    qk = dq0.shape[1]
    vw = dv0.shape[1]
    d = dgs.shape[1]
    mi = s + 2 * qk + 2 * vw + 2 * d
    c_q, c_k, c_v, c_g, c_gs, c_gr = s, s + qk, s + 2 * qk, s + 2 * qk + vw, s + 2 * qk + 2 * vw, s + 2 * qk + 2 * vw + d

    def body(dus_ref, dq0_ref, dq1_ref, dk0_ref, dk1_ref, dv0_ref, dv1_ref, dg_ref, dgs_ref, dgr_ref, o_ref):
        i = pl.program_id(0)
        lat = i >= nct
        o_ref[:, :s] = dus_ref[...].astype(o_ref.dtype)
        o_ref[:, c_q:c_k] = (dq0_ref[...] + dq1_ref[...]).astype(o_ref.dtype)
        o_ref[:, c_k:c_v] = (dk0_ref[...] + dk1_ref[...]).astype(o_ref.dtype)
        o_ref[:, c_v:c_g] = (dv0_ref[...] + dv1_ref[...]).astype(o_ref.dtype)
        o_ref[:, c_g:c_gs] = jnp.where(lat, dg_ref[...], 0.0).astype(o_ref.dtype)
        o_ref[:, c_gs:c_gr] = jnp.where(lat, dgs_ref[...], 0.0).astype(o_ref.dtype)
        o_ref[:, c_gr:] = jnp.where(lat, dgr_ref[...], 0.0).astype(o_ref.dtype)

    (out,) = _rows(name, body, r // tile,
                   [_row_in(dus, tile), _row_in(dq0, tile), _row_in(dq1, tile), _row_in(dk0, tile), _row_in(dk1, tile),
                    _row_in(dv0, tile), _row_in(dv1, tile), _row_in(dg, tile, x_only_offset=nct),
                    _row_in(dgs, tile, x_only_offset=nct), _row_in(dgr, tile, x_only_offset=nct)],
                   [((r, mi), BF16, (tile, mi), lambda i: (i, 0))])
    return out


def _loss_grad(y, target, tile, name):
    t_rows, d = y.shape

    def body(y_ref, t_ref, dy_ref, l_ref):
        i = pl.program_id(0)
        e = y_ref[...] - t_ref[...]
        dy_ref[...] = e * (1.0 / d)

        @pl.when(i == 0)
        def _():
            l_ref[...] = jnp.zeros_like(l_ref)

        l_ref[0] += jnp.sum(e * e, axis=0, keepdims=True)

    return _rows(name, body, t_rows // tile, [_row_in(y, tile), _row_in(target, tile)],
                 [((t_rows, d), F32, (tile, d), lambda i: (i, 0)), ((1, 1, d), F32, (1, 1, d), lambda i: (0, 0, 0))])


def _silu_rows(v, name):
    def body(v_ref, o_ref):
        z = v_ref[...]
        o_ref[...] = z * _sigmoid(z)

    (o,) = _rows(name, body, 1, [_row_in(v, v.shape[0])], [(v.shape, F32, v.shape, lambda i: (0, 0))])
    return o


def _silu_grad_rows(v, dv, name):
    def body(v_ref, d_ref, o_ref):
        z = v_ref[...]
        sg = _sigmoid(z)
        o_ref[...] = d_ref[...] * (sg * (1.0 + z * (1.0 - sg)))

    (o,) = _rows(name, body, 1, [_row_in(v, v.shape[0]), _row_in(dv, v.shape[0])],
                 [(v.shape, F32, v.shape, lambda i: (0, 0))])
    return o


def _sum_leading(g8, name):
    n, r, c = g8.shape
    tile = _tile(r, 256, SUBLANE)

    def body(g_ref, o_ref):
        acc = g_ref[0]
        for j in range(1, n):
            acc = acc + g_ref[j]
        o_ref[...] = acc

    (o,) = _rows(name, body, r // tile, [(g8, (n, tile, c), lambda i: (0, i, 0))],
                 [((r, c), F32, (tile, c), lambda i: (i, 0))])
    return o


def _pair_sum(g, recv, axis, name):
    n, br, bc = recv.shape
    tile = _tile(br, 256, 16)
    nrt = br // tile
    core = lax.axis_index("c").astype(jnp.int32).reshape(1)

    def body(c_ref, g_ref, r_ref, o_ref):
        o_ref[0] = (g_ref[...].astype(F32) + r_ref[0].astype(F32)).astype(o_ref.dtype)

    if axis == 1:
        g_spec = pl.BlockSpec((tile, bc), lambda q, i, c_ref: (i, 2 * q + c_ref[0]))
    else:
        g_spec = pl.BlockSpec((tile, bc), lambda q, i, c_ref: ((2 * q + c_ref[0]) * nrt + i, 0))
    slot = pl.BlockSpec((1, tile, bc), lambda q, i, c_ref: (q, i, 0))
    return pl.pallas_call(
        body, name=name, out_shape=_sds((n, br, bc), recv.dtype),
        grid_spec=pltpu.PrefetchScalarGridSpec(num_scalar_prefetch=1, grid=(n, nrt), in_specs=[g_spec, slot],
                                               out_specs=slot),
        compiler_params=_params(("arbitrary", "arbitrary")))(core, g, recv)


def _adam_math(w, m, v, g):
    c1 = 1.0 / (1.0 - ADAM_B1 ** ADAM_STEP)
    c2 = 1.0 / (1.0 - ADAM_B2 ** ADAM_STEP)
    mm = ADAM_B1 * m + (1.0 - ADAM_B1) * g
    vv = ADAM_B2 * v + (1.0 - ADAM_B2) * (g * g)
    return -ADAM_LR * ((mm * c1) / (jnp.sqrt(vv * c2) + ADAM_EPS) + ADAM_WD * w), mm, vv


def _adamw(w, m, v, gparts, name):
    r, c = w.shape
    n = gparts.shape[0]
    tile = _tile(r, 256, 16)

    def body(w_ref, m_ref, v_ref, g_ref, go_ref, d_ref, mo_ref, vo_ref):
        g = g_ref[0].astype(F32)
        for j in range(1, n):
            g = g + g_ref[j].astype(F32)
        go_ref[...] = g
        d_ref[...], mo_ref[...], vo_ref[...] = _adam_math(w_ref[...], m_ref[...], v_ref[...], g)

    rs = lambda arr: _row_in(arr, tile)
    out = ((r, c), F32, (tile, c), lambda i: (i, 0))
    return _rows(name, body, r // tile, [rs(w), rs(m), rs(v), (gparts, (n, tile, c), lambda i: (0, i, 0))],
                 [out, out, out, out])


def _adamw_scattered(w, m, v, layer, p, recv, name, filled=None):
    nl, r, c = w.shape
    n = recv.shape[0]
    tile = _tile(r, 256, 16)
    chip = (2 * lax.axis_index("x") + lax.axis_index("y")).astype(jnp.int32).reshape(1)
    n_prev = 0 if filled is None else len(filled)

    def body(q_ref, w_ref, m_ref, v_ref, p_ref, g_ref, *rest):
        go_ref, d_ref, mo_ref, vo_ref = rest[n_prev:]
        g = p_ref[0].astype(F32)
        for j in range(n):
            g = g + g_ref[j].astype(F32)
        go_ref[0] = g
        d_ref[0], mo_ref[0], vo_ref[0] = _adam_math(w_ref[0], m_ref[0], v_ref[0], g)

    slab = pl.BlockSpec((1, tile, c), lambda i, q_ref: (layer, i, 0))
    anywhere = pl.BlockSpec(memory_space=pl.ANY)
    out = _sds((nl, r, c), F32)
    prev = [] if filled is None else list(filled)
    return pl.pallas_call(
        body, name=name, out_shape=[out, out, out, out],
        grid_spec=pltpu.PrefetchScalarGridSpec(
            num_scalar_prefetch=1, grid=(r // tile,),
            in_specs=[slab, slab, slab, pl.BlockSpec((1, tile, c), lambda i, q_ref: (q_ref[0], i, 0)),
                      pl.BlockSpec((n, tile, c), lambda i, q_ref: (0, i, 0))] + [anywhere] * n_prev,
            out_specs=[slab, slab, slab, slab]),
        input_output_aliases={6 + j: j for j in range(n_prev)},
        compiler_params=_params(("arbitrary",)))(chip, w, m, v, p, recv, *prev)


def _cmul(ar, ai, br, bi):
    return ar * br - ai * bi, ar * bi + ai * br


def _cpow(ar, ai, n):
    pr, pi = jnp.ones_like(ar), jnp.zeros_like(ar)
    br, bi = ar, ai
    while n:
        if n & 1:
            pr, pi = _cmul(pr, pi, br, bi)
        n >>= 1
        if n:
            br, bi = _cmul(br, bi, br, bi)
    return pr, pi


def _s5_scan_into(xr_ref, xi_ref, ar1, ai1, ns, fr_ref, fi_ref, hr_ref, hi_ref, reverse):
    st = ar1.shape[1]
    ar = jnp.broadcast_to(ar1, (SUBLANE, st))
    ai = jnp.broadcast_to(ai1, (SUBLANE, st))
    zero = jnp.zeros((SUBLANE, st), F32)
    zero1 = jnp.zeros((1, st), F32)

    def slab(k):
        return pl.ds(pl.multiple_of(k * SUBLANE, SUBLANE), SUBLANE)

    def pass1(j, carry):
        hr, hi = carry
        k = ns - 1 - j if reverse else j
        nr, ni = _cmul(ar, ai, hr, hi)
        return nr + xr_ref[slab(k), :], ni + xi_ref[slab(k), :]

    fr, fi = lax.fori_loop(0, ns, pass1, (zero, zero))
    fr_ref[...] = fr
    fi_ref[...] = fi
    pr, pi = _cpow(ar1, ai1, ns)
    order = list(range(N_DEV - 1, -1, -1)) if reverse else list(range(N_DEV))
    hr_ref[order[0]:order[0] + 1, :] = zero1
    hi_ref[order[0]:order[0] + 1, :] = zero1
    for a_, b_ in zip(order[:-1], order[1:]):
        cr, ci = _cmul(pr, pi, hr_ref[a_:a_ + 1, :], hi_ref[a_:a_ + 1, :])
        hr_ref[b_:b_ + 1, :] = cr + fr_ref[a_:a_ + 1, :]
        hi_ref[b_:b_ + 1, :] = ci + fi_ref[a_:a_ + 1, :]

    def pass2(j, carry):
        hr, hi = carry
        k = ns - 1 - j if reverse else j
        nr, ni = _cmul(ar, ai, hr, hi)
        nr = nr + xr_ref[slab(k), :]
        ni = ni + xi_ref[slab(k), :]
        xr_ref[slab(k), :] = nr
        xi_ref[slab(k), :] = ni
        return nr, ni

    lax.fori_loop(0, ns, pass2, (hr_ref[...], hi_ref[...]))


def _s5_specs(r, ch, st):
    u_spec = pl.BlockSpec((r, ch), lambda j: (0, j // 2))
    w_spec = pl.BlockSpec((1, ch, st), lambda j: (j, 0, 0))
    c_spec = pl.BlockSpec((1, st, ch), lambda j: (j, 0, 0))
    a_spec = pl.BlockSpec((1, 2, st), lambda j: (j, 0, 0))
    return u_spec, w_spec, c_spec, a_spec


def _s5_fwd(up, wre, wim, cre, cim, a, rev, name):
    r, s = up.shape
    nh, ch, st = wre.shape
    ns = r // N_DEV
    u_spec, w_spec, c_spec, a_spec = _s5_specs(r, ch, st)

    def body(u_ref, wre_ref, wim_ref, cre_ref, cim_ref, a_ref, y_ref, xr, xi, fr, fi, hr, hi):
        j = pl.program_id(0)
        for rb in range(N_DEV):
            rows = slice(rb * ns, (rb + 1) * ns)
            ub = u_ref[rows, :].astype(MXU_DTYPE)
            xr[rows, :] = jnp.dot(ub, wre_ref[0].astype(MXU_DTYPE), preferred_element_type=F32)
            xi[rows, :] = jnp.dot(ub, wim_ref[0].astype(MXU_DTYPE), preferred_element_type=F32)
        _s5_scan_into(xr, xi, a_ref[0, 0:1, :], a_ref[0, 1:2, :], ns, fr, fi, hr, hi, rev)
        for rb in range(N_DEV):
            rows = slice(rb * ns, (rb + 1) * ns)
            yb = (jnp.dot(xr[rows, :].astype(MXU_DTYPE), cre_ref[0].astype(MXU_DTYPE), preferred_element_type=F32)
                  - jnp.dot(xi[rows, :].astype(MXU_DTYPE), cim_ref[0].astype(MXU_DTYPE), preferred_element_type=F32))

            @pl.when(j % 2 == 0)
            def _():
                y_ref[rows, :] = yb

            @pl.when(j % 2 == 1)
            def _():
                y_ref[rows, :] += yb

    small = pltpu.VMEM((SUBLANE, st), F32)
    return pl.pallas_call(
        body, name=name, grid=(nh,), in_specs=[u_spec, w_spec, w_spec, c_spec, c_spec, a_spec],
        out_specs=pl.BlockSpec((r, ch), lambda j: (0, j // 2)), out_shape=_sds((r, s), F32),
        scratch_shapes=[pltpu.VMEM((r, st), F32), pltpu.VMEM((r, st), F32), small, small, small, small],
        compiler_params=_params(("arbitrary",)))(up, wre, wim, cre, cim, a)


def _s5_bwd(up, dyp, wre, wim, cre, cim, a, rev, name):
    r, s = up.shape
    nh, ch, st = wre.shape
    ns = r // N_DEV
    u_spec, w_spec, c_spec, a_spec = _s5_specs(r, ch, st)
    nt = (((1,), (1,)), ((), ()))
    tn = (((0,), (0,)), ((), ()))

    def body(u_ref, dy_ref, wre_ref, wim_ref, cre_ref, cim_ref, a_ref,
             du_ref, dwre_ref, dwim_ref, dcre_ref, dcim_ref, da_ref,
             hr, hi, gr, gi, fr, fi, sr, si, er, ei):
        j = pl.program_id(0)
        wre_b = wre_ref[0].astype(MXU_DTYPE)
        wim_b = wim_ref[0].astype(MXU_DTYPE)
        cre_b = cre_ref[0].astype(MXU_DTYPE)
        cim_b = cim_ref[0].astype(MXU_DTYPE)
        for rb in range(N_DEV):
            rows = slice(rb * ns, (rb + 1) * ns)
            ub = u_ref[rows, :].astype(MXU_DTYPE)
            hr[rows, :] = jnp.dot(ub, wre_b, preferred_element_type=F32)
            hi[rows, :] = jnp.dot(ub, wim_b, preferred_element_type=F32)
        ar1, ai1 = a_ref[0, 0:1, :], a_ref[0, 1:2, :]
        _s5_scan_into(hr, hi, ar1, ai1, ns, fr, fi, sr, si, rev)
        dcre = jnp.zeros((st, ch), F32)
        dcim = jnp.zeros((st, ch), F32)
        for rb in range(N_DEV):
            rows = slice(rb * ns, (rb + 1) * ns)
            dyb = dy_ref[rows, :].astype(MXU_DTYPE)
            gr[rows, :] = lax.dot_general(dyb, cre_b, nt, preferred_element_type=F32)
            gi[rows, :] = -lax.dot_general(dyb, cim_b, nt, preferred_element_type=F32)
            dcre += lax.dot_general(hr[rows, :].astype(MXU_DTYPE), dyb, tn, preferred_element_type=F32)
            dcim -= lax.dot_general(hi[rows, :].astype(MXU_DTYPE), dyb, tn, preferred_element_type=F32)
        dcre_ref[0] = dcre
        dcim_ref[0] = dcim
        _s5_scan_into(gr, gi, ar1, -ai1, ns, fr, fi, er, ei, not rev)

        def slab(k):
            return pl.ds(pl.multiple_of(k * SUBLANE, SUBLANE), SUBLANE)

        step_back = 1 if rev else -1

        def acc_step(k, carry):
            acr, aci = carry
            g_r, g_i = gr[slab(k), :], gi[slab(k), :]
            p_r, p_i = hr[slab(k + step_back), :], hi[slab(k + step_back), :]
            return acr + g_r * p_r + g_i * p_i, aci + g_i * p_r - g_r * p_i

        edge = (ns - 1) * SUBLANE if rev else 0
        g_r, g_i = gr[edge:edge + SUBLANE, :], gi[edge:edge + SUBLANE, :]
        p_r, p_i = sr[...], si[...]
        lo, hi_k = (0, ns - 1) if rev else (1, ns)
        acr, aci = lax.fori_loop(lo, hi_k, acc_step, (g_r * p_r + g_i * p_i, g_i * p_r - g_r * p_i))
        da_ref[0, 0:1, :] = jnp.sum(acr, axis=0, keepdims=True)
        da_ref[0, 1:2, :] = jnp.sum(aci, axis=0, keepdims=True)
        dwre = jnp.zeros((ch, st), F32)
        dwim = jnp.zeros((ch, st), F32)
        for rb in range(N_DEV):
            rows = slice(rb * ns, (rb + 1) * ns)
            grb = gr[rows, :].astype(MXU_DTYPE)
            gib = gi[rows, :].astype(MXU_DTYPE)
            ub = u_ref[rows, :].astype(MXU_DTYPE)
            dub = (lax.dot_general(grb, wre_b, nt, preferred_element_type=F32)
                   + lax.dot_general(gib, wim_b, nt, preferred_element_type=F32))
            dwre += lax.dot_general(ub, grb, tn, preferred_element_type=F32)
            dwim += lax.dot_general(ub, gib, tn, preferred_element_type=F32)

            @pl.when(j % 2 == 0)
            def _():
                du_ref[rows, :] = dub

            @pl.when(j % 2 == 1)
            def _():
                du_ref[rows, :] += dub

        dwre_ref[0] = dwre
        dwim_ref[0] = dwim

    small = pltpu.VMEM((SUBLANE, st), F32)
    big = pltpu.VMEM((r, st), F32)
    return pl.pallas_call(
        body, name=name, grid=(nh,), in_specs=[u_spec, u_spec, w_spec, w_spec, c_spec, c_spec, a_spec],
        out_specs=[pl.BlockSpec((r, ch), lambda j: (0, j // 2)), w_spec, w_spec, c_spec, c_spec, a_spec],
        out_shape=[_sds((r, s), F32), _sds(wre.shape, F32), _sds(wre.shape, F32), _sds(cre.shape, F32),
                   _sds(cre.shape, F32), _sds(a.shape, F32)],
        scratch_shapes=[big, big, big, big, small, small, small, small, small, small],
        compiler_params=_params(("arbitrary",)))(up, dyp, wre, wim, cre, cim, a)


def _rope(t, cos, sin):
    quarter = t.shape[1] // 4
    lane = lax.broadcasted_iota(jnp.int32, t.shape, 1)
    first = (lane // quarter) % 2 == 0
    partner = jnp.where(first, pltpu.roll(t, t.shape[1] - quarter, 1), pltpu.roll(t, quarter, 1))
    return t * cos + partner * sin


def _rope_t(d, cos, sin):
    quarter = d.shape[1] // 4
    ds_ = d * sin
    lane = lax.broadcasted_iota(jnp.int32, d.shape, 1)
    first = (lane // quarter) % 2 == 0
    partner = jnp.where(first, pltpu.roll(ds_, d.shape[1] - quarter, 1), pltpu.roll(ds_, quarter, 1))
    return d * cos + partner


def _chunk_of_step(s, nch, ncc, rev):
    if not rev:
        return s
    return jnp.where(s < ncc, ncc - 1 - s, nch + ncc - 1 - s)


def _heads_per_step(heads, dk, dv, q_off):
    for hpg in range(heads, 0, -1):
        if (heads % hpg == 0 and q_off % (hpg * dk) == 0 and (heads * dk) % (hpg * dk) == 0
                and (q_off + 2 * heads * dk) % (hpg * dv) == 0):
            return hpg
    return 1


def _ret_fwd(hm, cos, sin, decay, wend, win, gch, heads, dk, dv, q_off, ncc, rev, name):
    r = hm.shape[0]
    ch = RET_CHUNK
    nch = r // ch
    t_rows = r - ncc * ch
    hpg = _heads_per_step(heads, dk, dv, q_off)
    qb, kb, vb = q_off // (hpg * dk), (q_off + heads * dk) // (hpg * dk), (q_off + 2 * heads * dk) // (hpg * dv)
    q_scale = dk ** -0.5
    nt = (((1,), (1,)), ((), ()))
    tn = (((0,), (0,)), ((), ()))
    cof = lambda s: _chunk_of_step(s, nch, ncc, rev)

    def body(q_ref, k_ref, v_ref, cos_ref, sin_ref, dec_ref, we_ref, wi_ref, g_ref, o_ref, sin_out, st):
        s = pl.program_id(1)

        @pl.when(s == 0)
        def _():
            st[...] = jnp.zeros_like(st)

        cos_, sin_ = cos_ref[...], sin_ref[...]
        for hl in range(hpg):
            ks, vs = slice(hl * dk, (hl + 1) * dk), slice(hl * dv, (hl + 1) * dv)
            q = _rope(q_ref[:, ks], cos_, sin_) * q_scale
            k = _rope(k_ref[:, ks], cos_, sin_)
            v = v_ref[:, vs].astype(MXU_DTYPE)
            s_cur = st[hl]
            sin_out[hl, 0] = s_cur
            kw = (k * we_ref[hl]).astype(MXU_DTYPE)
            qw = (q * wi_ref[hl]).astype(MXU_DTYPE)
            scores = lax.dot_general(q.astype(MXU_DTYPE), k.astype(MXU_DTYPE), nt,
                                     preferred_element_type=F32) * dec_ref[hl]
            o_ref[:, vs] = (jnp.dot(scores.astype(MXU_DTYPE), v, preferred_element_type=F32)
                            + jnp.dot(qw, s_cur.astype(MXU_DTYPE), preferred_element_type=F32))
            st[hl] = g_ref[hl] * s_cur + lax.dot_general(kw, v, tn, preferred_element_type=F32)

    tab = lambda w: pl.BlockSpec((hpg, ch, w), lambda h, s: (h, 0, 0))
    return pl.pallas_call(
        body, name=name, grid=(heads // hpg, nch),
        in_specs=[pl.BlockSpec((ch, hpg * dk), lambda h, s: (cof(s), qb + h)),
                  pl.BlockSpec((ch, hpg * dk), lambda h, s: (cof(s), kb + h)),
                  pl.BlockSpec((ch, hpg * dv), lambda h, s: (cof(s), vb + h)),
                  pl.BlockSpec((ch, dk), lambda h, s: (cof(s), 0)),
                  pl.BlockSpec((ch, dk), lambda h, s: (cof(s), 0)),
                  tab(ch), tab(dk), tab(dk), tab(dv)],
        out_specs=[pl.BlockSpec((ch, hpg * dv), lambda h, s: (jnp.maximum(cof(s) - ncc, 0) if not rev
                                                               else jnp.where(s < ncc, nch - ncc - 1, cof(s) - ncc), h)),
                   pl.BlockSpec((hpg, 1, dk, dv), lambda h, s: (h, s, 0, 0))],
        out_shape=[_sds((t_rows, heads * dv), F32), _sds((heads, nch, dk, dv), F32)],
        scratch_shapes=[pltpu.VMEM((hpg, dk, dv), F32)],
        compiler_params=_params(("parallel", "arbitrary")))(hm, hm, hm, cos, sin, decay, wend, win, gch)


def _ret_bwd(hm, cos, sin, decay, wend, win, gch, s_in, do, heads, dk, dv, q_off, ncc, rev, name):
    r = hm.shape[0]
    ch = RET_CHUNK
    nch = r // ch
    hpg = _heads_per_step(heads, dk, dv, q_off)
    qb, kb, vb = q_off // (hpg * dk), (q_off + heads * dk) // (hpg * dk), (q_off + 2 * heads * dk) // (hpg * dv)
    q_scale = dk ** -0.5
    nt = (((1,), (1,)), ((), ()))
    tn = (((0,), (0,)), ((), ()))
    cof = lambda rr: _chunk_of_step(nch - 1 - rr, nch, ncc, rev)

    def body(q_ref, k_ref, v_ref, cos_ref, sin_ref, dec_ref, we_ref, wi_ref, g_ref, sin_ref2, do_ref,
             dq_ref, dk_ref, dv_ref, ddec_ref, dwe_ref, dwi_ref, dg_ref, dst):
        rr = pl.program_id(1)
        n = cof(rr)

        @pl.when(rr == 0)
        def _():
            dst[...] = jnp.zeros_like(dst)
            ddec_ref[...] = jnp.zeros_like(ddec_ref)
            dwe_ref[...] = jnp.zeros_like(dwe_ref)
            dwi_ref[...] = jnp.zeros_like(dwi_ref)
            dg_ref[...] = jnp.zeros_like(dg_ref)

        cos_, sin_ = cos_ref[...], sin_ref[...]
        for hl in range(hpg):
            ks, vs = slice(hl * dk, (hl + 1) * dk), slice(hl * dv, (hl + 1) * dv)
            q = _rope(q_ref[:, ks], cos_, sin_) * q_scale
            k = _rope(k_ref[:, ks], cos_, sin_)
            v = v_ref[:, vs].astype(MXU_DTYPE)
            qb_, kb_ = q.astype(MXU_DTYPE), k.astype(MXU_DTYPE)
            kw = (k * we_ref[hl]).astype(MXU_DTYPE)
            qw = (q * wi_ref[hl]).astype(MXU_DTYPE)
            sraw = lax.dot_general(qb_, kb_, nt, preferred_element_type=F32)
            scores = (sraw * dec_ref[hl]).astype(MXU_DTYPE)
            d_o = jnp.where(n >= ncc, do_ref[:, vs], 0.0).astype(MXU_DTYPE)
            s_n = sin_ref2[hl, 0]
            s_nb = s_n.astype(MXU_DTYPE)
            ds1 = dst[hl]
            ds1b = ds1.astype(MXU_DTYPE)
            dsc = lax.dot_general(d_o, v, nt, preferred_element_type=F32)
            dsr = (dsc * dec_ref[hl]).astype(MXU_DTYPE)
            ddec_ref[hl] += dsc * sraw
            t1 = lax.dot_general(d_o, s_nb, nt, preferred_element_type=F32)
            dq_r = jnp.dot(dsr, kb_, preferred_element_type=F32) + t1 * wi_ref[hl]
            dwi_ref[hl] += t1 * q
            t2 = lax.dot_general(v, ds1b, nt, preferred_element_type=F32)
            dk_r = lax.dot_general(dsr, qb_, tn, preferred_element_type=F32) + t2 * we_ref[hl]
            dwe_ref[hl] += t2 * k
            dv_ref[:, vs] = (lax.dot_general(scores, d_o, tn, preferred_element_type=F32)
                             + jnp.dot(kw, ds1b, preferred_element_type=F32))
            dg_ref[hl] += ds1 * s_n
            dst[hl] = g_ref[hl] * ds1 + lax.dot_general(qw, d_o, tn, preferred_element_type=F32)
            dq_ref[:, ks] = _rope_t(dq_r, cos_, sin_) * q_scale
            dk_ref[:, ks] = _rope_t(dk_r, cos_, sin_)

    tab = lambda w: pl.BlockSpec((hpg, ch, w), lambda h, rr: (h, 0, 0))
    return pl.pallas_call(
        body, name=name, grid=(heads // hpg, nch),
        in_specs=[pl.BlockSpec((ch, hpg * dk), lambda h, rr: (cof(rr), qb + h)),
                  pl.BlockSpec((ch, hpg * dk), lambda h, rr: (cof(rr), kb + h)),
                  pl.BlockSpec((ch, hpg * dv), lambda h, rr: (cof(rr), vb + h)),
                  pl.BlockSpec((ch, dk), lambda h, rr: (cof(rr), 0)),
                  pl.BlockSpec((ch, dk), lambda h, rr: (cof(rr), 0)),
                  tab(ch), tab(dk), tab(dk), tab(dv),
                  pl.BlockSpec((hpg, 1, dk, dv), lambda h, rr: (h, nch - 1 - rr, 0, 0)),
                  pl.BlockSpec((ch, hpg * dv), lambda h, rr: (jnp.maximum(cof(rr) - ncc, 0), h))],
        out_specs=[pl.BlockSpec((ch, hpg * dk), lambda h, rr: (cof(rr), h)),
                   pl.BlockSpec((ch, hpg * dk), lambda h, rr: (cof(rr), h)),
                   pl.BlockSpec((ch, hpg * dv), lambda h, rr: (cof(rr), h)),
                   tab(ch), tab(dk), tab(dk), tab(dv)],
        out_shape=[_sds((r, heads * dk), F32), _sds((r, heads * dk), F32), _sds((r, heads * dv), F32),
                   _sds(decay.shape, F32), _sds(wend.shape, F32), _sds(win.shape, F32), _sds(gch.shape, F32)],
        scratch_shapes=[pltpu.VMEM((hpg, dk, dv), F32)],
        compiler_params=_params(("parallel", "arbitrary")))(hm, hm, hm, cos, sin, decay, wend, win, gch, s_in, do)


_HBM = pl.BlockSpec(memory_space=pltpu.HBM)
_MESH = pl.DeviceIdType.MESH
ALL_GATHER_COLLECTIVE_ID = 1
SIBLING_COLLECTIVE_ID = 2
CHIPS_COLLECTIVE_ID = 3


def _axis_slice(ref, axis, start, size):
    idx = [slice(None)] * len(ref.shape)
    idx[axis] = pl.ds(start, size)
    return ref.at[tuple(idx)]


def _sibling_and_chip_peers():
    x, y, c = lax.axis_index("x"), lax.axis_index("y"), lax.axis_index("c")
    return [(x, y, 1 - c), (1 - x, y, c), (x, 1 - y, c), (1 - x, 1 - y, c)]


def _launch_exchange(body, name, operand, out_shape, sems, peers_fn, collective_id, on_sequencer):
    if not on_sequencer:
        return pl.pallas_call(body, name=name, out_shape=out_shape, in_specs=[_HBM], out_specs=_HBM,
                              scratch_shapes=sems)(operand)

    def sequencer_body(in_ref, out_ref, *sem_refs):
        peers = peers_fn()
        barrier = pltpu.get_barrier_semaphore()
        for peer in peers:
            pl.semaphore_signal(barrier, inc=1, device_id=peer, device_id_type=_MESH)
        pl.semaphore_wait(barrier, len(peers))
        body(in_ref, out_ref, *sem_refs)

    return pl.kernel(sequencer_body, out_type=out_shape, name=name,
                     mesh=plsc.ScalarSubcoreMesh(axis_name="sequencer", num_cores=1), scratch_types=sems,
                     compiler_params=pltpu.CompilerParams(collective_id=collective_id))(operand)


def _all_gather(shard, axis, name, on_sequencer=False):
    m = shard.shape[axis]
    out_shape = list(shard.shape)
    out_shape[axis] = N_DEV * m

    def body(x_ref, out_ref, send_sems, recv_sems, local_sem):
        x, y, c = lax.axis_index("x"), lax.axis_index("y"), lax.axis_index("c")
        me, sibling = (x, y, c), (x, y, 1 - c)
        chips = [(1 - x, y), (x, 1 - y), (1 - x, 1 - y)]

        def block(px, py, pc):
            return _axis_slice(out_ref, axis, (4 * px + 2 * py + pc) * m, m)

        def copy(k, blk, to, src=None):
            return pltpu.make_async_remote_copy(
                src_ref=block(*blk) if src is None else src, dst_ref=block(*blk), send_sem=send_sems.at[k],
                recv_sem=recv_sems.at[k], device_id=to, device_id_type=_MESH)

        mine = pltpu.make_async_copy(x_ref, block(*me), local_sem)
        mine.start()
        first = [copy(0, me, sibling, src=x_ref)]
        first += [copy(1 + j, me, (*chip, c), src=x_ref) for j, chip in enumerate(chips)]
        for cp in first:
            cp.start()
        passed = [copy(4 + j, (*chip, c), sibling) for j, chip in enumerate(chips)]
        for j, chip in enumerate(chips):
            copy(1 + j, (*chip, c), me).wait_recv()
            passed[j].start()
        copy(0, sibling, me).wait_recv()
        for j, chip in enumerate(chips):
            copy(4 + j, (*chip, 1 - c), me).wait_recv()
        for cp in first + passed:
            cp.wait_send()
        mine.wait()

    return _launch_exchange(
        body, name, shard, _sds(out_shape, shard.dtype),
        [pltpu.SemaphoreType.DMA((7,)), pltpu.SemaphoreType.DMA((7,)), pltpu.SemaphoreType.DMA(())],
        _sibling_and_chip_peers, ALL_GATHER_COLLECTIVE_ID, on_sequencer)


def _rs_sibling(g, axis, name, on_sequencer=False):
    m = g.shape[axis] // N_DEV
    blk_shape = list(g.shape)
    blk_shape[axis] = m
    n_chips = N_DEV // 2

    def body(g_ref, recv_ref, send_sems, recv_sems):
        x, y, c = lax.axis_index("x"), lax.axis_index("y"), lax.axis_index("c")
        sibling = (x, y, 1 - c)
        send = [pltpu.make_async_remote_copy(
            src_ref=_axis_slice(g_ref, axis, (2 * q + 1 - c) * m, m), dst_ref=recv_ref.at[q],
            send_sem=send_sems.at[q], recv_sem=recv_sems.at[q], device_id=sibling, device_id_type=_MESH)
            for q in range(n_chips)]
        for cp in send:
            cp.start()
        for cp in send:
            cp.wait_recv()
        for cp in send:
            cp.wait_send()

    return _launch_exchange(
        body, name, g, _sds([n_chips] + blk_shape, g.dtype),
        [pltpu.SemaphoreType.DMA((n_chips,)), pltpu.SemaphoreType.DMA((n_chips,))],
        lambda: _sibling_and_chip_peers()[:1], SIBLING_COLLECTIVE_ID, on_sequencer)


def _rs_chips(p, name, on_sequencer=False):
    n_peers = p.shape[0] - 1

    def body(p_ref, out_ref, send_sems, recv_sems):
        x, y, c = lax.axis_index("x"), lax.axis_index("y"), lax.axis_index("c")
        chips = [(1 - x, y), (x, 1 - y), (1 - x, 1 - y)]
        send = [pltpu.make_async_remote_copy(
            src_ref=p_ref.at[2 * cx + cy], dst_ref=out_ref.at[j], send_sem=send_sems.at[j],
            recv_sem=recv_sems.at[j], device_id=(cx, cy, c), device_id_type=_MESH)
            for j, (cx, cy) in enumerate(chips)]
        for cp in send:
            cp.start()
        for cp in send:
            cp.wait_recv()
        for cp in send:
            cp.wait_send()

    return _launch_exchange(
        body, name, p, _sds((n_peers,) + p.shape[1:], p.dtype),
        [pltpu.SemaphoreType.DMA((n_peers,)), pltpu.SemaphoreType.DMA((n_peers,))],
        lambda: _sibling_and_chip_peers()[1:], CHIPS_COLLECTIVE_ID, on_sequencer)


def _reduce_scatter(g, axis, name):
    sib = _rs_sibling(g, axis, name + "_d2d", on_sequencer=True)
    p = _pair_sum(g, sib, axis, name + "_pair")
    return p, _rs_chips(p, name + "_ici", on_sequencer=True)


def _s5_tables(lam_re, lam_im, log_step, b_re, b_im, c_re, c_im):
    nd, g, p, cg = b_re.shape
    step = jnp.exp(log_step)[..., None]
    mag = jnp.exp(lam_re * step)
    a_re, a_im = mag * jnp.cos(lam_im * step), mag * jnp.sin(lam_im * step)
    den = lam_re * lam_re + lam_im * lam_im
    num_re, num_im = a_re - 1.0, a_im
    k_re = (num_re * lam_re + num_im * lam_im) / den
    k_im = (num_im * lam_re - num_re * lam_im) / den
    bb_re = k_re[..., None] * b_re - k_im[..., None] * b_im
    bb_im = k_re[..., None] * b_im + k_im[..., None] * b_re
    gt = g // SSM_TILE_GROUPS
    hg = SSM_HALF_GROUPS
    eye = jnp.eye(SSM_TILE_GROUPS, dtype=F32).reshape(SSM_TILE_GROUPS, 2, hg)

    def pack_b(bb):
        w = jnp.einsum("djhqpc,ghq->djhgcqp", bb.reshape(nd, gt, 2, hg, p, cg), eye)
        return w.reshape(nd, gt * 2, SSM_TILE_GROUPS * cg, hg * p)

    def pack_c(cc):
        w = jnp.einsum("djhqcp,ghq->djhqpgc", cc.reshape(nd, gt, 2, hg, cg, p), eye)
        return w.reshape(nd, gt * 2, hg * p, SSM_TILE_GROUPS * cg)

    a = jnp.stack([a_re.reshape(nd, gt * 2, hg * p), a_im.reshape(nd, gt * 2, hg * p)], axis=2)
    return pack_b(bb_re), pack_b(bb_im), pack_c(c_re), pack_c(c_im), a


def _ret_tables(decay_logit, dk, dv):
    ch = RET_CHUNK
    nd, h = decay_logit.shape
    lg = jax.nn.log_sigmoid(decay_logit)[:, :, None]
    pos = jnp.arange(ch, dtype=F32)
    fwd_diff = pos[:, None] - pos[None, :]
    diff = jnp.stack([fwd_diff, -fwd_diff])[:, None]
    mask = jnp.stack([fwd_diff >= 0, -fwd_diff > 0])[:, None]
    end_pos = jnp.stack([ch - 1.0 - pos, pos])[:, None]
    in_pos = jnp.stack([pos + 1.0, ch - pos])[:, None]
    w_end = jnp.exp(lg * end_pos)
    w_in = jnp.exp(lg * in_pos)
    decay = jnp.where(mask, jnp.exp(lg[..., None] * jnp.where(mask, diff, 0.0)), 0.0)
    g_chunk = jnp.exp(lg[..., 0] * ch)
    return (decay, jnp.broadcast_to(w_end[..., None], (nd, h, ch, dk)), jnp.broadcast_to(w_in[..., None], (nd, h, ch, dk)),
            jnp.broadcast_to(g_chunk[..., None, None], (nd, h, dk, dv)))


def _rope_tables(t_rows, ncc, dk):
    quarter = dk // 4
    idx = np.arange(t_rows)
    row, col = idx // GRID_W, idx % GRID_W
    inv = ROPE_BASE ** (-np.arange(quarter, dtype=np.float32) / quarter)
    ang_r = row.astype(np.float32)[:, None] * inv
    ang_c = col.astype(np.float32)[:, None] * inv
    ang_r, ang_c = jnp.asarray(ang_r, F32), jnp.asarray(ang_c, F32)
    cos = jnp.concatenate([jnp.cos(ang_r), jnp.cos(ang_r), jnp.cos(ang_c), jnp.cos(ang_c)], axis=1)
    sin = jnp.concatenate([-jnp.sin(ang_r), jnp.sin(ang_r), -jnp.sin(ang_c), jnp.sin(ang_c)], axis=1)
    n_ctx = ncc * RET_CHUNK
    cos = jnp.concatenate([jnp.ones((n_ctx, dk), F32), cos], axis=0)
    sin = jnp.concatenate([jnp.zeros((n_ctx, dk), F32), sin], axis=0)
    return cos, sin


def _to_scan_layout(ctx_rows, lat_rows, rev):
    u = jnp.concatenate([lat_rows, ctx_rows] if rev else [ctx_rows, lat_rows], axis=0)
    r, w = u.shape
    return u.reshape(N_DEV, r // N_DEV, w).transpose(1, 0, 2).reshape(r, w)


def _from_scan_layout(yp, n_ctx, rev):
    r, w = yp.shape
    y = yp.reshape(r // N_DEV, N_DEV, w).transpose(1, 0, 2).reshape(r, w)
    return (y[r - n_ctx:], y[:r - n_ctx]) if rev else (y[:n_ctx], y[n_ctx:])


def _pack(parts, width):
    rows = []
    for p in parts:
        flat = p.reshape(-1).astype(F32)
        n = flat.shape[0]
        rows.append(jnp.pad(flat, (0, -n % (SUBLANE * width))).reshape(-1, width))
    return jnp.concatenate(rows, axis=0)


def _packed_rows(n, width):
    return -(-n // (SUBLANE * width)) * SUBLANE


def _unpack(flat2d, shapes):
    width = flat2d.shape[1]
    out, row = [], 0
    for shp in shapes:
        n = int(np.prod(shp))
        nr = _packed_rows(n, width)
        out.append(flat2d[row:row + nr].reshape(-1)[:n].reshape(shp))
        row += nr
    return out


def kernel(x, c, ctx, c_ctx, ada_w, ada_b, norm_g, ffn_w_in, ffn_w_out, mix_w_in, ssm_lam_re, ssm_lam_im, ssm_log_step, ssm_b_re, ssm_b_im, ssm_c_re, ssm_c_im, ssm_d, ssm_glu_w, ret_decay_logit, ret_w_proj, mix_w_out, loss_target, m_c_ctx, m_ada_w, m_ada_b, m_norm_g, m_ffn_w_in, m_ffn_w_out, m_mix_w_in, m_ssm_lam_re, m_ssm_lam_im, m_ssm_log_step, m_ssm_b_re, m_ssm_b_im, m_ssm_c_re, m_ssm_c_im, m_ssm_d, m_ssm_glu_w, m_ret_decay_logit, m_ret_w_proj, m_mix_w_out, v_c_ctx, v_ada_w, v_ada_b, v_norm_g, v_ffn_w_in, v_ffn_w_out, v_mix_w_in, v_ssm_lam_re, v_ssm_lam_im, v_ssm_log_step, v_ssm_b_re, v_ssm_b_im, v_ssm_c_re, v_ssm_c_im, v_ssm_d, v_ssm_glu_w, v_ret_decay_logit, v_ret_w_proj, v_mix_w_out):
    t_rows, d = x.shape[1], x.shape[2]
    n_ctx = ctx.shape[1]
    r = n_ctx + t_rows
    ssm_w = ssm_d.shape[1]
    heads = ret_decay_logit.shape[2]
    mi = mix_w_in.shape[2] * N_DEV
    dk = (mi - ssm_w - 2 * d) // (6 * heads)
    dv = 2 * dk
    qk_w, v_w = heads * dk, heads * dv
    q_off = ssm_w
    ncc = n_ctx // RET_CHUNK
    tile = n_ctx
    nct = 1
    wide_tile = _tile(n_ctx, 128, 16)
    assert r % (N_DEV * SUBLANE) == 0 and n_ctx % RET_CHUNK == 0 and t_rows % tile == 0
    me = 4 * lax.axis_index("x") + 2 * lax.axis_index("y") + lax.axis_index("c")
    g_off = ssm_w + 2 * qk_w + v_w
    gs_off = g_off + v_w

    ng_cols = norm_g.shape[2]
    small0 = _pack([c[0], norm_g[0]], d)
    small0_all = _all_gather(small0, 0, "ag_cond")

    bf = lambda w: w.astype(BF16)
    small0_all, sh_in1, sh_out1, sh_mix = lax.optimization_barrier(
        (small0_all, bf(ffn_w_in[0, 0]), bf(ffn_w_out[0, 0]), bf(mix_w_in[0])))
    small0_all = small0_all.reshape(N_DEV, -1)
    w_in1 = _all_gather(sh_in1, 1, "ag_ffn1_in", on_sequencer=True)
    w_out1 = _all_gather(sh_out1, 0, "ag_ffn1_out", on_sequencer=True)
    w_mix = _all_gather(sh_mix, 1, "ag_mix_in", on_sequencer=True)
    w_glu = _all_gather(bf(ssm_glu_w[0]), 1, "ag_glu", on_sequencer=True)
    w_rp = _all_gather(bf(ret_w_proj[0]), 0, "ag_ret_proj", on_sequencer=True)
    w_mo = _all_gather(bf(mix_w_out[0]), 0, "ag_mix_out", on_sequencer=True)
    w_in2 = _all_gather(bf(ffn_w_in[0, 1]), 1, "ag_ffn2_in", on_sequencer=True)
    w_out2 = _all_gather(bf(ffn_w_out[0, 1]), 0, "ag_ffn2_out", on_sequencer=True)

    ng_at = _packed_rows(d, d) * d
    c_all = small0_all[:, :d]
    g_full = small0_all[:, ng_at:ng_at + 6 * ng_cols].reshape(N_DEV, 6, ng_cols).transpose(1, 0, 2).reshape(6, d)
    g6 = g_full.reshape(6, 1, d)
    cc = jnp.concatenate([c_all, c_ctx[None, :], jnp.zeros((2 * SUBLANE - N_DEV - 1, d), F32)], axis=0)
    sc = _silu_rows(cc, "ada_silu")
    na = ada_w.shape[2]
    a_loc = _mm(sc, ada_w[0], "nn", F32, "ada_fwd", tm=16, tn=na, tk=512)
    a_all = _all_gather(a_loc, 0, "ag_ada").reshape(N_DEV, 2 * SUBLANE, na)
    ada_x = lax.dynamic_index_in_dim(a_all, me, axis=1, keepdims=False).reshape(9 * d) + ada_b[0]
    ada_c = a_all[:, N_DEV, :].reshape(9 * d) + ada_b[0]
    mods = jnp.stack([ada_c.reshape(9, d), ada_x.reshape(9, d)]).reshape(18, 1, d)

    xin = jnp.concatenate([ctx[0], x[0]], axis=0)
    u1 = _ada_pre_fwd(xin, g6, mods, 0, 0, nct, tile, "pre1")
    h1 = _mm(u1, w_in1, "nn", F32, "ffn1_in", tm=544)
    a1 = _swiglu_fwd(h1, wide_tile, "swiglu1")
    o1 = _mm(a1, w_out1, "nn", F32, "ffn1_out", tm=544, tn=d, tk=1408)
    x1 = _ada_post_fwd(xin, o1, g6, mods, 1, 0, 0.5, nct, tile, "post1")
    u2 = _ada_pre_fwd(x1, g6, mods, 2, 1, nct, tile, "pre2")
    hm = _mm(u2, w_mix, "nn", F32, "mix_in", tm=544)

    us_ctx, us_lat = hm[:n_ctx, :ssm_w], hm[n_ctx:, :ssm_w]
    dskip = ssm_d.reshape(1, 1, ssm_w)
    s5_prm = (ssm_lam_re[0], ssm_lam_im[0], ssm_log_step[0], ssm_b_re[0], ssm_b_im[0], ssm_c_re[0], ssm_c_im[0])
    s5_tabs_both, s5_vjp = jax.vjp(_s5_tables, *s5_prm)
    s5_tabs, ups, y_dirs = [], [], []
    for dr in range(2):
        tabs = tuple(t[dr] for t in s5_tabs_both)
        up = _to_scan_layout(us_ctx, us_lat, dr == 1)
        yp = _s5_fwd(up, *tabs, dr == 1, "s5_fwd%d" % dr)
        s5_tabs.append(tabs)
        ups.append(up)
        y_dirs.append(_from_scan_layout(yp, n_ctx, dr == 1)[1])
    a_ssm = _ssm_out_fwd(y_dirs[0], y_dirs[1], hm, dskip, nct, tile, "ssm_out")
    gab = _mm(a_ssm, w_glu, "nn", F32, "glu", tm=512, tn=2048, tk=ssm_w)

    cos, sin = _rope_tables(t_rows, ncc, dk)
    ret_tabs_both, ret_vjp = jax.vjp(functools.partial(_ret_tables, dk=dk, dv=dv), ret_decay_logit[0])
    ret_tabs, o_dirs, s_ins = [], [], []
    for dr in range(2):
        tabs = tuple(t[dr] for t in ret_tabs_both)
        o_d, s_in = _ret_fwd(hm, cos, sin, *tabs, heads, dk, dv, q_off, ncc, dr == 1, "ret_fwd%d" % dr)
        ret_tabs.append(tabs)
        o_dirs.append(o_d)
        s_ins.append(s_in)
    ret_in = _ret_gate_fwd(o_dirs[0], o_dirs[1], hm, g_off, heads, dv, nct, tile, "ret_gate")
    rb = _mm(ret_in, w_rp, "nn", F32, "ret_proj", tm=512, tn=d, tk=v_w)
    merged = _merge_fwd(gab, rb, hm, gs_off, nct, tile, "merge")
    mix = _mm(merged, w_mo, "nn", F32, "mix_out", tm=512, tn=d, tk=d)
    x1x = x1[n_ctx:]
    x2 = _ada_post_fwd(x1x, mix, g6, mods, 3, 1, 1.0, 0, tile, "post2")
    u3 = _ada_pre_fwd(x2, g6, mods, 4, 2, 0, tile, "pre3")
    h3 = _mm(u3, w_in2, "nn", F32, "ffn2_in", tm=512)
    a3 = _swiglu_fwd(h3, wide_tile, "swiglu2")
    o3 = _mm(a3, w_out2, "nn", F32, "ffn2_out", tm=512, tn=d, tk=1408)
    x3 = _ada_post_fwd(x2, o3, g6, mods, 5, 2, 0.5, 0, tile, "post3")
    dy, lcols = _loss_grad(x3, loss_target[0], tile, "loss")
    loss_part = (0.5 * jnp.sum(lcols) / d).reshape(1)

    dg6 = [None] * 6
    dmod = {}

    def add_mod(sel_rows, k, val):
        for sel, row in sel_rows:
            dmod[(sel, k)] = dmod.get((sel, k), 0.0) + val[row, 0]

    both, lat = [(0, 0), (1, 1)], [(1, 0)]

    def tie(*vals):
        return lax.optimization_barrier(vals)

    def big_update(w3d, m3d, v3d, layer, gfull, axis, name, filled=None):
        p, recv = _reduce_scatter(gfull, axis, "rs_" + name)
        return _adamw_scattered(w3d, m3d, v3d, layer, p, recv, "adamw_" + name, filled)

    do3, dg6[5], dgt = _ada_post_bwd(dy, o3, g6, mods, 5, 2, 0.5, 0, 1, tile, "post3_bwd")
    add_mod(lat, 8, dgt)
    gw_out2 = _mm(a3, do3, "tn", BF16, "ffn2_out_dw", tm=1408, tn=1024, tk=2176)
    do3, gw_out2 = tie(do3, gw_out2)
    up_out2 = big_update(ffn_w_out[0], m_ffn_w_out[0], v_ffn_w_out[0], 1, gw_out2, 0, "ffn2_out")
    da3 = _mm(do3, w_out2, "nt", F32, "ffn2_out_dx", tm=512, tn=1408, tk=d)
    dh3 = _swiglu_bwd(h3, da3, wide_tile, "swiglu2_bwd")
    gw_in2 = _mm(u3, dh3, "tn", BF16, "ffn2_in_dw", tm=1024, tn=1408, tk=2176)
    dh3, gw_in2 = tie(dh3, gw_in2)
    up_in2 = big_update(ffn_w_in[0], m_ffn_w_in[0], v_ffn_w_in[0], 1, gw_in2, 1, "ffn2_in")
    du3 = _mm(dh3, w_in2, "nt", F32, "ffn2_in_dx", tm=512, tn=d, tk=1408)
    dx2, dg6[4], dsh, dsc = _ada_pre_bwd(x2, du3, dy, g6, mods, 4, 2, 0, 1, tile, "pre3_bwd")
    add_mod(lat, 6, dsh)
    add_mod(lat, 7, dsc)
    dmix, dg6[3], dgt = _ada_post_bwd(dx2, mix, g6, mods, 3, 1, 1.0, 0, 1, tile, "post2_bwd")
    add_mod(lat, 5, dgt)
    gw_mo = _mm(merged, dmix, "tn", BF16, "mix_out_dw", tm=1024, tn=1024, tk=2176)
    dmix, gw_mo = tie(dmix, gw_mo)
    up_mo = big_update(mix_w_out, m_mix_w_out, v_mix_w_out, 0, gw_mo, 0, "mix_out")
    dmerged = _mm(dmix, w_mo, "nt", F32, "mix_out_dx", tm=512, tn=d, tk=d)
    dgab, drb, dgs, dgr = _merge_bwd(gab, rb, hm, gs_off, dmerged, nct, tile, "merge_bwd")
    gw_glu = _mm(a_ssm, dgab, "tn", BF16, "glu_dw", tm=1024, tn=1024, tk=2176)
    gw_rp = _mm(ret_in, drb, "tn", BF16, "ret_proj_dw", tm=1024, tn=1024, tk=2176)
    dgab, drb, gw_glu, gw_rp = tie(dgab, drb, gw_glu, gw_rp)
    up_glu = big_update(ssm_glu_w, m_ssm_glu_w, v_ssm_glu_w, 0, gw_glu, 1, "glu")
    up_rp = big_update(ret_w_proj, m_ret_w_proj, v_ret_w_proj, 0, gw_rp, 0, "ret_proj")
    da_ssm = _mm(dgab, w_glu, "nt", F32, "glu_dx", tm=512, tn=ssm_w, tk=2 * d)
    dret_in = _mm(drb, w_rp, "nt", F32, "ret_proj_dx", tm=512, tn=v_w, tk=d)
    d_o, dg_gate = _ret_gate_bwd(o_dirs[0], o_dirs[1], hm, g_off, dret_in, heads, dv, nct, tile, "ret_gate_bwd")
    dy_ssm, dus_direct, d_dskip = _ssm_out_bwd(y_dirs[0], y_dirs[1], hm, dskip, da_ssm, nct, tile, "ssm_out_bwd")
    s5_table_grads, du_ctx, du_lat = [], [], [dus_direct]
    for dr in range(2):
        dyp = _to_scan_layout(jnp.zeros((n_ctx, ssm_w), F32), dy_ssm, dr == 1)
        if dr == 1:
            dyp, up_out2, up_in2 = tie(dyp, up_out2, up_in2)
        outs = _s5_bwd(ups[dr], dyp, *s5_tabs[dr], dr == 1, "s5_bwd%d" % dr)
        part_ctx, part_lat = _from_scan_layout(outs[0], n_ctx, dr == 1)
        du_ctx.append(part_ctx)
        du_lat.append(part_lat)
        s5_table_grads.append(outs[1:])
    dqkv, ret_table_grads = [], []
    for dr in range(2):
        if dr == 1:
            d_o, up_mo, up_glu, up_rp = tie(d_o, up_mo, up_glu, up_rp)
        outs = _ret_bwd(hm, cos, sin, *ret_tabs[dr], s_ins[dr], d_o, heads, dk, dv, q_off, ncc, dr == 1,
                        "ret_bwd%d" % dr)
        dqkv.append(outs[:3])
        ret_table_grads.append(outs[3:])
    both_dirs = lambda grads: tuple(jnp.stack([g0, g1]) for g0, g1 in zip(*grads))
    early_parts = list(s5_vjp(both_dirs(s5_table_grads))) + list(ret_vjp(both_dirs(ret_table_grads)))
    s5_names = 7
    early_shapes = [p.shape for p in early_parts]
    early_all = _all_gather(_pack(early_parts, 1024), 0, "ag_s5_grads", on_sequencer=True)
    early_sum = _sum_leading(early_all.reshape(N_DEV, -1, 1024), "sum_s5_grads")
    dus = jnp.concatenate([du_ctx[0] + du_ctx[1], du_lat[0] + du_lat[1] + du_lat[2]], axis=0)
    dhm = _assemble_dhm(dus, dqkv[0][0], dqkv[1][0], dqkv[0][1], dqkv[1][1], dqkv[0][2], dqkv[1][2],
                        dg_gate, dgs, dgr, n_ctx // wide_tile, wide_tile, "assemble_dhm")
    gw_mix = _mm(u2, dhm, "tn", BF16, "mix_in_dw", tm=1024, tn=1408, tk=2176)
    dhm, gw_mix = tie(dhm, gw_mix)
    up_mix = big_update(mix_w_in, m_mix_w_in, v_mix_w_in, 0, gw_mix, 1, "mix_in")
    du2 = _mm(dhm, w_mix, "nt", F32, "mix_in_dx", tm=544, tn=d, tk=1408)
    dx1, dg6[2], dsh, dsc = _ada_pre_bwd(x1, du2, dx2, g6, mods, 2, 1, nct, 2, tile, "pre2_bwd", dres_x_only=True)
    add_mod(both, 3, dsh)
    add_mod(both, 4, dsc)
    do1, dg6[1], dgt = _ada_post_bwd(dx1, o1, g6, mods, 1, 0, 0.5, nct, 2, tile, "post1_bwd")
    add_mod(both, 2, dgt)
    gw_out1 = _mm(a1, do1, "tn", BF16, "ffn1_out_dw", tm=1408, tn=1024, tk=2176)
    do1, gw_out1 = tie(do1, gw_out1)
    up_out1 = big_update(ffn_w_out[0], m_ffn_w_out[0], v_ffn_w_out[0], 0, gw_out1, 0, "ffn1_out", filled=up_out2)
    da1 = _mm(do1, w_out1, "nt", F32, "ffn1_out_dx", tm=544, tn=1408, tk=d)
    dh1 = _swiglu_bwd(h1, da1, wide_tile, "swiglu1_bwd")
    dh1, up_mix, early_sum = tie(dh1, up_mix, early_sum)
    early_sums = _unpack(early_sum, early_shapes)
    gw_in1 = _mm(u1, dh1, "tn", BF16, "ffn1_in_dw", tm=1024, tn=1408, tk=2176)
    dh1, gw_in1 = tie(dh1, gw_in1)
    up_in1 = big_update(ffn_w_in[0], m_ffn_w_in[0], v_ffn_w_in[0], 0, gw_in1, 1, "ffn1_in", filled=up_in2)
    du1 = _mm(dh1, w_in1, "nt", F32, "ffn1_in_dx", tm=544, tn=d, tk=1408)
    dxin, dg6[0], dsh, dsc = _ada_pre_bwd(xin, du1, dx1, g6, mods, 0, 0, nct, 2, tile, "pre1_bwd")
    add_mod(both, 0, dsh)
    add_mod(both, 1, dsc)
    grad_x = dxin[n_ctx:][None]

    zero_d = jnp.zeros((d,), F32)
    d_ada_x = jnp.stack([dmod.get((1, k), zero_d) for k in range(9)]).reshape(9 * d)
    d_ada_c = jnp.stack([dmod.get((0, k), zero_d) for k in range(9)]).reshape(9 * d)
    dg_full = jnp.stack([g[0, 0] for g in dg6])
    small_parts = [d_ada_x, d_ada_c, dg_full, d_dskip, loss_part]
    small_shapes = [p.shape for p in small_parts]
    packed = _pack(small_parts, 1024)
    gathered = _all_gather(packed, 0, "ag_small_grads").reshape(N_DEV, -1, 1024)
    summed = _sum_leading(gathered, "sum_small_grads")
    sums = _unpack(summed, small_shapes)
    sum_dx, sum_dc, sum_dg = sums[0], sums[1], sums[2]
    loss = sums[4][0]
    grad_ada_b = (sum_dx + sum_dc)[None]
    dx_rows = gathered.reshape(N_DEV, -1)[:, :9 * d]
    col0 = me * na
    da_rows = jnp.concatenate([lax.dynamic_slice_in_dim(dx_rows, col0, na, axis=1),
                               lax.dynamic_slice_in_dim(sum_dc[None], col0, na, axis=1),
                               jnp.zeros((2 * SUBLANE - N_DEV - 1, na), F32)], axis=0)
    grad_ada_w = _mm(sc, da_rows, "tn", F32, "ada_dw", tm=512, tn=na, tk=16)
    d_sc = _mm(da_rows, ada_w[0], "nt", F32, "ada_dx", tm=16, tn=512, tk=na)
    d_sc_all = _all_gather(jnp.broadcast_to(d_sc[N_DEV:N_DEV + 1], (SUBLANE, d)), 0, "ag_dctx")
    d_sc_sum = _sum_leading(d_sc_all.reshape(N_DEV, SUBLANE, d), "sum_dctx")
    grad_c_ctx = _silu_grad_rows(jnp.broadcast_to(c_ctx[None], (SUBLANE, d)), d_sc_sum, "ctx_silu_bwd")[0]
    grad_norm_g = lax.dynamic_slice_in_dim(sum_dg, me * ng_cols, ng_cols, axis=1)[None]

    upd = {}
    upd["ffn_w_in"] = [o[None] for o in up_in1]
    upd["ffn_w_out"] = [o[None] for o in up_out1]
    upd["mix_w_in"] = list(up_mix)
    upd["ssm_glu_w"] = list(up_glu)
    upd["ret_w_proj"] = list(up_rp)
    upd["mix_w_out"] = list(up_mo)
    upd["ada_w"] = [o[None] for o in _adamw(ada_w[0], m_ada_w[0], v_ada_w[0], grad_ada_w[None], "adamw_ada_w")]

    small_names = ["c_ctx", "ada_b", "norm_g", "ssm_lam_re", "ssm_lam_im", "ssm_log_step", "ssm_b_re", "ssm_b_im",
                   "ssm_c_re", "ssm_c_im", "ssm_d", "ret_decay_logit"]
    small_w = [c_ctx, ada_b, norm_g, ssm_lam_re, ssm_lam_im, ssm_log_step, ssm_b_re, ssm_b_im, ssm_c_re, ssm_c_im,
               ssm_d, ret_decay_logit]
    small_m = [m_c_ctx, m_ada_b, m_norm_g, m_ssm_lam_re, m_ssm_lam_im, m_ssm_log_step, m_ssm_b_re, m_ssm_b_im,
               m_ssm_c_re, m_ssm_c_im, m_ssm_d, m_ret_decay_logit]
    small_v = [v_c_ctx, v_ada_b, v_norm_g, v_ssm_lam_re, v_ssm_lam_im, v_ssm_log_step, v_ssm_b_re, v_ssm_b_im,
               v_ssm_c_re, v_ssm_c_im, v_ssm_d, v_ret_decay_logit]
    small_g = [grad_c_ctx, grad_ada_b, grad_norm_g] + [s[None] for s in early_sums[:s5_names]] + \
              [sums[3].reshape(ssm_d.shape), early_sums[s5_names][None]]
    shapes = [w.shape for w in small_w]
    res = _adamw(_pack(small_w, 1024), _pack(small_m, 1024), _pack(small_v, 1024), _pack(small_g, 1024)[None],
                 "adamw_small")
    small_out = [_unpack(o, shapes) for o in res]
    for i, nm in enumerate(small_names):
        upd[nm] = [small_out[kind][i] for kind in range(4)]

    order = ["c_ctx", "ada_w", "ada_b", "norm_g", "ffn_w_in", "ffn_w_out", "mix_w_in", "ssm_lam_re", "ssm_lam_im",
             "ssm_log_step", "ssm_b_re", "ssm_b_im", "ssm_c_re", "ssm_c_im", "ssm_d", "ssm_glu_w", "ret_decay_logit",
             "ret_w_proj", "mix_w_out"]
    outs = [loss, grad_x]
    for kind in range(4):
        outs += [upd[nm][kind] for nm in order]
    return tuple(outs)
```

```python
import functools
import math

import jax
import jax.numpy as jnp
import numpy as np
from jax import lax
from jax.experimental import pallas as pl
from jax.experimental.pallas import tpu as pltpu
from jax.experimental.pallas import tpu_sc as plsc

F32 = jnp.float32
BF16 = jnp.bfloat16
MXU_DTYPE = jnp.bfloat16
MESH_AXES = ("x", "y", "c")
N_DEV = 8
V7X_VMEM_LIMIT_BYTES = 56 * 1024 * 1024
LANE = 128
SUBLANE = 8

GRID_W = 64
RET_CHUNK = 128
ROPE_BASE = 10000.0
NORM_EPS = 1e-6
ADAM_LR = 0.001
ADAM_B1 = 0.9
ADAM_B2 = 0.999
ADAM_EPS = 1e-08
ADAM_WD = 0.01
ADAM_STEP = 10
SSM_TILE_GROUPS = 8
SSM_HALF_GROUPS = 4
N_SEG = 16


def _params(sem=None):
    return pltpu.CompilerParams(dimension_semantics=sem, vmem_limit_bytes=V7X_VMEM_LIMIT_BYTES)


def _tile(n, target, mult):
    best = None
    t = mult
    while t <= min(n, target):
        if n % t == 0:
            best = t
        t += mult
    return n if best is None else best


def _sds(shape, dtype):
    return jax.ShapeDtypeStruct(tuple(shape), dtype)


def _mm(a, b, dims, out_dtype, name, tm=512, tn=1408, tk=2048):
    if dims == "nn":
        (m, k), (k2, n) = a.shape, b.shape
    elif dims == "nt":
        (m, k), (n, k2) = a.shape, b.shape
    else:
        (k, m), (k2, n) = a.shape, b.shape
    assert k == k2, (a.shape, b.shape, dims)
    tm = _tile(m, tm, 16)
    tn = _tile(n, tn, LANE)
    tk = _tile(k, tk, LANE if dims != "tn" else 16)
    nk = k // tk
    dn = {"nn": (((1,), (0,)), ((), ())), "nt": (((1,), (1,)), ((), ())), "tn": (((0,), (0,)), ((), ()))}[dims]

    def product(a_ref, b_ref):
        return lax.dot_general(a_ref[...].astype(MXU_DTYPE), b_ref[...].astype(MXU_DTYPE), dn,
                               preferred_element_type=F32)

    def body_single(a_ref, b_ref, o_ref):
        o_ref[...] = product(a_ref, b_ref).astype(o_ref.dtype)

    def body(a_ref, b_ref, o_ref, acc_ref):
        kk = pl.program_id(2)

        @pl.when(kk == 0)
        def _():
            acc_ref[...] = product(a_ref, b_ref)

        @pl.when((kk > 0) & (kk < nk - 1))
        def _():
            acc_ref[...] += product(a_ref, b_ref)

        @pl.when(kk == nk - 1)
        def _():
            o_ref[...] = (acc_ref[...] + product(a_ref, b_ref)).astype(o_ref.dtype)

    if dims == "nn":
        a_spec = pl.BlockSpec((tm, tk), lambda j, i, kk: (i, kk))
        b_spec = pl.BlockSpec((tk, tn), lambda j, i, kk: (kk, j))
    elif dims == "nt":
        a_spec = pl.BlockSpec((tm, tk), lambda j, i, kk: (i, kk))
        b_spec = pl.BlockSpec((tn, tk), lambda j, i, kk: (j, kk))
    else:
        a_spec = pl.BlockSpec((tk, tm), lambda j, i, kk: (kk, i))
        b_spec = pl.BlockSpec((tk, tn), lambda j, i, kk: (kk, j))
    return pl.pallas_call(
        body_single if nk == 1 else body, name=name, grid=(n // tn, m // tm, nk), in_specs=[a_spec, b_spec],
        out_specs=pl.BlockSpec((tm, tn), lambda j, i, kk: (i, j)), out_shape=_sds((m, n), out_dtype),
        scratch_shapes=[] if nk == 1 else [pltpu.VMEM((tm, tn), F32)],
        compiler_params=_params(("parallel", "parallel", "arbitrary")))(a, b)


def _rows(name, body, n_tiles, ins, outs):
    in_specs = [pl.BlockSpec(blk, imap) for (_, blk, imap) in ins]
    out_specs = [pl.BlockSpec(blk, imap) for (_, _, blk, imap) in outs]
    out_shape = [_sds(shape, dt) for (shape, dt, _, _) in outs]
    res = pl.pallas_call(body, name=name, grid=(n_tiles,), in_specs=in_specs, out_specs=out_specs,
                         out_shape=out_shape, compiler_params=_params(("arbitrary",)))(*[a for (a, _, _) in ins])
    return res


def _row_in(arr, tile, width=None, col=0, x_only_offset=None):
    width = arr.shape[1] if width is None else width
    if x_only_offset is None:
        return (arr, (tile, width), lambda i: (i, col))
    return (arr, (tile, width), lambda i: (jnp.maximum(i - x_only_offset, 0), col))


def _vec_in(arr, idx_fn):
    return (arr, (1, 1, arr.shape[2]), lambda i: (idx_fn(i), 0, 0))


def _rms(h):
    return lax.rsqrt(jnp.mean(h * h, axis=-1, keepdims=True) + NORM_EPS)


def _sigmoid(z):
    return 1.0 / (1.0 + jnp.exp(-z))


def _ada_pre_fwd(h, g6, mods, gi, mi, nct, tile, name):
    r, d = h.shape
    sel = lambda i: jnp.where(i >= nct, 1, 0)

    def body(h_ref, g_ref, sh_ref, sc_ref, u_ref):
        hh = h_ref[...]
        n = hh * _rms(hh) * g_ref[0]
        u_ref[...] = (n * (1.0 + sc_ref[0]) + sh_ref[0]).astype(u_ref.dtype)

    (u,) = _rows(name, body, r // tile,
                 [_row_in(h, tile), _vec_in(g6, lambda i: gi), _vec_in(mods, lambda i: sel(i) * 9 + 3 * mi),
                  _vec_in(mods, lambda i: sel(i) * 9 + 3 * mi + 1)],
                 [((r, d), BF16, (tile, d), lambda i: (i, 0))])
    return u


def _ada_pre_bwd(h, du, dres, g6, mods, gi, mi, nct, nsel, tile, name, dres_x_only=False):
    r, d = h.shape
    sel = lambda i: jnp.where(i >= nct, 1, 0) if nsel == 2 else 0
    msel = lambda i: jnp.where(i >= nct, 1, 0)
    off = nct if dres_x_only else None

    def body(h_ref, du_ref, dr_ref, g_ref, sc_ref, dh_ref, dg_ref, dsh_ref, dsc_ref):
        i = pl.program_id(0)
        hh = h_ref[...]
        rr = _rms(hh)
        g = g_ref[0]
        hn = hh * rr
        n = hn * g
        du_ = du_ref[...].astype(F32)
        dn = du_ * (1.0 + sc_ref[0])

        @pl.when(i == 0)
        def _():
            dg_ref[...] = jnp.zeros_like(dg_ref)

        @pl.when((i == 0) | (i == nct))
        def _():
            dsh_ref[...] = jnp.zeros_like(dsh_ref)
            dsc_ref[...] = jnp.zeros_like(dsc_ref)

        dg_ref[0] += jnp.sum(dn * hn, axis=0, keepdims=True)
        dsh_ref[0] += jnp.sum(du_, axis=0, keepdims=True)
        dsc_ref[0] += jnp.sum(du_ * n, axis=0, keepdims=True)
        t = dn * g
        dh = rr * t - hn * (rr * jnp.mean(t * hn, axis=-1, keepdims=True))
        if dres_x_only:
            dh_ref[...] = dh + jnp.where(i >= nct, dr_ref[...], 0.0)
        else:
            dh_ref[...] = dh + dr_ref[...]

    dh, dg, dsh, dsc = _rows(
        name, body, r // tile,
        [_row_in(h, tile), _row_in(du, tile), _row_in(dres, tile, x_only_offset=off), _vec_in(g6, lambda i: gi),
         _vec_in(mods, lambda i: msel(i) * 9 + 3 * mi + 1)],
        [((r, d), F32, (tile, d), lambda i: (i, 0)), ((1, 1, d), F32, (1, 1, d), lambda i: (0, 0, 0)),
         ((nsel, 1, d), F32, (1, 1, d), lambda i: (sel(i), 0, 0)),
         ((nsel, 1, d), F32, (1, 1, d), lambda i: (sel(i), 0, 0))])
    return dh, dg, dsh, dsc


def _ada_post_fwd(h, o, g6, mods, gi, mi, res_w, nct, tile, name, h_x_only=False):
    r, d = o.shape
    sel = lambda i: jnp.where(i >= nct, 1, 0)

    def body(h_ref, o_ref, g_ref, gt_ref, y_ref):
        oo = o_ref[...]
        n = oo * _rms(oo) * g_ref[0]
        y_ref[...] = h_ref[...] + res_w * gt_ref[0] * n

    (y,) = _rows(name, body, r // tile,
                 [_row_in(h, tile), _row_in(o, tile), _vec_in(g6, lambda i: gi),
                  _vec_in(mods, lambda i: sel(i) * 9 + 3 * mi + 2)],
                 [((r, d), F32, (tile, d), lambda i: (i, 0))])
    return y


def _ada_post_bwd(dy, o, g6, mods, gi, mi, res_w, nct, nsel, tile, name):
    r, d = o.shape
    sel = lambda i: jnp.where(i >= nct, 1, 0) if nsel == 2 else 0
    msel = lambda i: jnp.where(i >= nct, 1, 0)

    def body(dy_ref, o_ref, g_ref, gt_ref, do_ref, dg_ref, dgt_ref):
        i = pl.program_id(0)
        oo = o_ref[...]
        rr = _rms(oo)
        g = g_ref[0]
        on = oo * rr
        dy_ = dy_ref[...] * res_w

        @pl.when(i == 0)
        def _():
            dg_ref[...] = jnp.zeros_like(dg_ref)

        @pl.when((i == 0) | (i == nct))
        def _():
            dgt_ref[...] = jnp.zeros_like(dgt_ref)

        dgt_ref[0] += jnp.sum(dy_ * (on * g), axis=0, keepdims=True)
        dn = dy_ * gt_ref[0]
        dg_ref[0] += jnp.sum(dn * on, axis=0, keepdims=True)
        t = dn * g
        do_ref[...] = (rr * t - on * (rr * jnp.mean(t * on, axis=-1, keepdims=True))).astype(do_ref.dtype)

    do, dg, dgt = _rows(
        name, body, r // tile,
        [_row_in(dy, tile), _row_in(o, tile), _vec_in(g6, lambda i: gi),
         _vec_in(mods, lambda i: msel(i) * 9 + 3 * mi + 2)],
        [((r, d), BF16, (tile, d), lambda i: (i, 0)), ((1, 1, d), F32, (1, 1, d), lambda i: (0, 0, 0)),
         ((nsel, 1, d), F32, (1, 1, d), lambda i: (sel(i), 0, 0))])
    return do, dg, dgt


def _swiglu_fwd(h, tile, name):
    r, w2 = h.shape
    f = w2 // 2

    def body(h_ref, a_ref):
        gt = h_ref[:, :f]
        up = h_ref[:, f:]
        a_ref[...] = (gt * _sigmoid(gt) * up).astype(a_ref.dtype)

    (a,) = _rows(name, body, r // tile, [_row_in(h, tile)], [((r, f), BF16, (tile, f), lambda i: (i, 0))])
    return a


def _swiglu_bwd(h, da, tile, name):
    r, w2 = h.shape
    f = w2 // 2

    def body(h_ref, da_ref, dh_ref):
        gt = h_ref[:, :f]
        up = h_ref[:, f:]
        d = da_ref[...]
        sg = _sigmoid(gt)
        dh_ref[:, :f] = (d * up * (sg * (1.0 + gt * (1.0 - sg)))).astype(dh_ref.dtype)
        dh_ref[:, f:] = (d * gt * sg).astype(dh_ref.dtype)

    (dh,) = _rows(name, body, r // tile, [_row_in(h, tile), _row_in(da, tile)],
                  [((r, w2), BF16, (tile, w2), lambda i: (i, 0))])
    return dh


def _gelu_parts(y):
    c0 = math.sqrt(2.0 / math.pi)
    inner = c0 * (y + 0.044715 * y * y * y)
    th = jnp.tanh(inner)
    return th, c0 * (1.0 + 3 * 0.044715 * y * y)


def _ssm_out_fwd(y0, y1, hm, dskip, nct, tile, name):
    t_rows, s = y0.shape

    def body(y0_ref, y1_ref, u_ref, d_ref, a_ref):
        y = y0_ref[...] + y1_ref[...] + d_ref[0] * u_ref[...]
        th, _ = _gelu_parts(y)
        a_ref[...] = (0.5 * y * (1.0 + th)).astype(a_ref.dtype)

    (a,) = _rows(name, body, t_rows // tile,
                 [_row_in(y0, tile), _row_in(y1, tile), (hm, (tile, s), lambda i: (i + nct, 0)),
                  _vec_in(dskip, lambda i: 0)],
                 [((t_rows, s), BF16, (tile, s), lambda i: (i, 0))])
    return a


def _ssm_out_bwd(y0, y1, hm, dskip, da, nct, tile, name):
    t_rows, s = y0.shape

    def body(y0_ref, y1_ref, u_ref, d_ref, da_ref, dy_ref, du_ref, dd_ref):
        i = pl.program_id(0)
        u = u_ref[...]
        y = y0_ref[...] + y1_ref[...] + d_ref[0] * u
        th, dinner = _gelu_parts(y)
        dy = da_ref[...] * (0.5 * (1.0 + th) + 0.5 * y * (1.0 - th * th) * dinner)
        dy_ref[...] = dy
        du_ref[...] = dy * d_ref[0]

        @pl.when(i == 0)
        def _():
            dd_ref[...] = jnp.zeros_like(dd_ref)

        dd_ref[0] += jnp.sum(dy * u, axis=0, keepdims=True)

    dy, du, dd = _rows(name, body, t_rows // tile,
                       [_row_in(y0, tile), _row_in(y1, tile), (hm, (tile, s), lambda i: (i + nct, 0)),
                        _vec_in(dskip, lambda i: 0), _row_in(da, tile)],
                       [((t_rows, s), F32, (tile, s), lambda i: (i, 0)), ((t_rows, s), F32, (tile, s), lambda i: (i, 0)),
                        ((1, 1, s), F32, (1, 1, s), lambda i: (0, 0, 0))])
    return dy, du, dd


def _col_pieces(arr, off, width, tile, nct, unit=None):
    pw = math.gcd(off, width if unit is None else unit)
    specs = [(arr, (tile, pw), functools.partial(lambda i, cb: (i + nct, cb), cb=off // pw + p))
             for p in range(width // pw)]
    return specs, pw


def _ret_gate_fwd(o0, o1, hm, g_off, heads, dv, nct, tile, name):
    t_rows, w = o0.shape
    g_specs, pw = _col_pieces(hm, g_off, w, tile, nct)
    ng = len(g_specs)

    def body(o0_ref, o1_ref, *refs):
        g_refs, r_ref = refs[:ng], refs[ng]
        for hd in range(heads):
            cs = slice(hd * dv, (hd + 1) * dv)
            o = o0_ref[:, cs] + o1_ref[:, cs]
            lo = (hd * dv) % pw
            g = g_refs[(hd * dv) // pw][:, lo:lo + dv]
            r_ref[:, cs] = (g * _sigmoid(g) * (o * _rms(o))).astype(r_ref.dtype)

    (ri,) = _rows(name, body, t_rows // tile, [_row_in(o0, tile), _row_in(o1, tile)] + g_specs,
                  [((t_rows, w), BF16, (tile, w), lambda i: (i, 0))])
    return ri


def _ret_gate_bwd(o0, o1, hm, g_off, dri, heads, dv, nct, tile, name):
    t_rows, w = o0.shape
    g_specs, pw = _col_pieces(hm, g_off, w, tile, nct)
    ng = len(g_specs)

    def body(o0_ref, o1_ref, d_ref, *refs):
        g_refs, do_ref, dg_ref = refs[:ng], refs[ng], refs[ng + 1]
        for hd in range(heads):
            cs = slice(hd * dv, (hd + 1) * dv)
            o = o0_ref[:, cs] + o1_ref[:, cs]
            lo = (hd * dv) % pw
            g = g_refs[(hd * dv) // pw][:, lo:lo + dv]
            d = d_ref[:, cs]
            rr = _rms(o)
            on = o * rr
            sg = _sigmoid(g)
            dg_ref[:, cs] = d * on * (sg * (1.0 + g * (1.0 - sg)))
            t = d * (g * sg)
            do_ref[:, cs] = rr * t - on * (rr * jnp.mean(t * on, axis=-1, keepdims=True))

    do, dg = _rows(name, body, t_rows // tile, [_row_in(o0, tile), _row_in(o1, tile), _row_in(dri, tile)] + g_specs,
                   [((t_rows, w), F32, (tile, w), lambda i: (i, 0)), ((t_rows, w), F32, (tile, w), lambda i: (i, 0))])
    return do, dg


def _merge_fwd(gab, rb, hm, gs_off, nct, tile, name):
    t_rows, d = rb.shape
    specs, pw = _col_pieces(hm, gs_off, 2 * d, tile, nct, unit=d)
    npc = d // pw

    def body(gab_ref, rb_ref, *refs):
        gs_refs, gr_refs, m_ref = refs[:npc], refs[npc:2 * npc], refs[2 * npc]
        for p in range(npc):
            cs = slice(p * pw, (p + 1) * pw)
            ga = gab_ref[:, cs]
            gb = gab_ref[:, d + p * pw:d + (p + 1) * pw]
            m_ref[:, cs] = (_sigmoid(gs_refs[p][...]) * (ga * _sigmoid(gb))
                            + _sigmoid(gr_refs[p][...]) * rb_ref[:, cs]).astype(m_ref.dtype)

    (mg,) = _rows(name, body, t_rows // tile, [_row_in(gab, tile), _row_in(rb, tile)] + specs,
                  [((t_rows, d), BF16, (tile, d), lambda i: (i, 0))])
    return mg


def _merge_bwd(gab, rb, hm, gs_off, dm, nct, tile, name):
    t_rows, d = rb.shape
    specs, pw = _col_pieces(hm, gs_off, 2 * d, tile, nct, unit=d)
    npc = d // pw

    def body(gab_ref, rb_ref, dm_ref, *refs):
        gs_refs, gr_refs = refs[:npc], refs[npc:2 * npc]
        dgab_ref, drb_ref, dgs_ref, dgr_ref = refs[2 * npc:]
        for p in range(npc):
            cs = slice(p * pw, (p + 1) * pw)
            cs2 = slice(d + p * pw, d + (p + 1) * pw)
            ga = gab_ref[:, cs]
            gb = gab_ref[:, cs2]
            dmm = dm_ref[:, cs]
            ss = _sigmoid(gs_refs[p][...])
            sr = _sigmoid(gr_refs[p][...])
            sb = _sigmoid(gb)
            dbr = dmm * ss
            dgab_ref[:, cs] = (dbr * sb).astype(dgab_ref.dtype)
            dgab_ref[:, cs2] = (dbr * ga * sb * (1.0 - sb)).astype(dgab_ref.dtype)
            drb_ref[:, cs] = (dmm * sr).astype(drb_ref.dtype)
            dgs_ref[:, cs] = dmm * (ga * sb) * ss * (1.0 - ss)
            dgr_ref[:, cs] = dmm * rb_ref[:, cs] * sr * (1.0 - sr)

    return _rows(name, body, t_rows // tile, [_row_in(gab, tile), _row_in(rb, tile), _row_in(dm, tile)] + specs,
                 [((t_rows, 2 * d), BF16, (tile, 2 * d), lambda i: (i, 0)), ((t_rows, d), BF16, (tile, d), lambda i: (i, 0)),
                  ((t_rows, d), F32, (tile, d), lambda i: (i, 0)), ((t_rows, d), F32, (tile, d), lambda i: (i, 0))])


def _assemble_dhm(dus, dq0, dq1, dk0, dk1, dv0, dv1, dg, dgs, dgr, nct, tile, name):
    r, s = dus.shape
    qk = dq0.shape[1]
    vw = dv0.shape[1]
    d = dgs.shape[1]
    mi = s + 2 * qk + 2 * vw + 2 * d
    c_q, c_k, c_v, c_g, c_gs, c_gr = s, s + qk, s + 2 * qk, s + 2 * qk + vw, s + 2 * qk + 2 * vw, s + 2 * qk + 2 * vw + d

    def body(dus_ref, dq0_ref, dq1_ref, dk0_ref, dk1_ref, dv0_ref, dv1_ref, dg_ref, dgs_ref, dgr_ref, o_ref):
        i = pl.program_id(0)
        lat = i >= nct
        o_ref[:, :s] = dus_ref[...].astype(o_ref.dtype)
        o_ref[:, c_q:c_k] = (dq0_ref[...] + dq1_ref[...]).astype(o_ref.dtype)
        o_ref[:, c_k:c_v] = (dk0_ref[...] + dk1_ref[...]).astype(o_ref.dtype)
        o_ref[:, c_v:c_g] = (dv0_ref[...] + dv1_ref[...]).astype(o_ref.dtype)
        o_ref[:, c_g:c_gs] = jnp.where(lat, dg_ref[...], 0.0).astype(o_ref.dtype)
        o_ref[:, c_gs:c_gr] = jnp.where(lat, dgs_ref[...], 0.0).astype(o_ref.dtype)
        o_ref[:, c_gr:] = jnp.where(lat, dgr_ref[...], 0.0).astype(o_ref.dtype)

    (out,) = _rows(name, body, r // tile,
                   [_row_in(dus, tile), _row_in(dq0, tile), _row_in(dq1, tile), _row_in(dk0, tile), _row_in(dk1, tile),
                    _row_in(dv0, tile), _row_in(dv1, tile), _row_in(dg, tile, x_only_offset=nct),
                    _row_in(dgs, tile, x_only_offset=nct), _row_in(dgr, tile, x_only_offset=nct)],
                   [((r, mi), BF16, (tile, mi), lambda i: (i, 0))])
    return out


def _loss_grad(y, target, tile, name):
    t_rows, d = y.shape

    def body(y_ref, t_ref, dy_ref, l_ref):
        i = pl.program_id(0)
        e = y_ref[...] - t_ref[...]
        dy_ref[...] = e * (1.0 / d)

        @pl.when(i == 0)
        def _():
            l_ref[...] = jnp.zeros_like(l_ref)

        l_ref[0] += jnp.sum(e * e, axis=0, keepdims=True)

    return _rows(name, body, t_rows // tile, [_row_in(y, tile), _row_in(target, tile)],
                 [((t_rows, d), F32, (tile, d), lambda i: (i, 0)), ((1, 1, d), F32, (1, 1, d), lambda i: (0, 0, 0))])


def _silu_rows(v, name):
    def body(v_ref, o_ref):
        z = v_ref[...]
        o_ref[...] = z * _sigmoid(z)

    (o,) = _rows(name, body, 1, [_row_in(v, v.shape[0])], [(v.shape, F32, v.shape, lambda i: (0, 0))])
    return o


def _silu_grad_rows(v, dv, name):
    def body(v_ref, d_ref, o_ref):
        z = v_ref[...]
        sg = _sigmoid(z)
        o_ref[...] = d_ref[...] * (sg * (1.0 + z * (1.0 - sg)))

    (o,) = _rows(name, body, 1, [_row_in(v, v.shape[0]), _row_in(dv, v.shape[0])],
                 [(v.shape, F32, v.shape, lambda i: (0, 0))])
    return o


def _sum_leading(g8, name):
    n, r, c = g8.shape
    tile = _tile(r, 256, SUBLANE)

    def body(g_ref, o_ref):
        acc = g_ref[0]
        for j in range(1, n):
            acc = acc + g_ref[j]
        o_ref[...] = acc

    (o,) = _rows(name, body, r // tile, [(g8, (n, tile, c), lambda i: (0, i, 0))],
                 [((r, c), F32, (tile, c), lambda i: (i, 0))])
    return o


def _pair_sum(g, recv, axis, name):
    n, br, bc = recv.shape
    tile = _tile(br, 256, 16)
    nrt = br // tile
    core = lax.axis_index("c").astype(jnp.int32).reshape(1)

    def body(c_ref, g_ref, r_ref, o_ref):
        o_ref[0] = (g_ref[...].astype(F32) + r_ref[0].astype(F32)).astype(o_ref.dtype)

    if axis == 1:
        g_spec = pl.BlockSpec((tile, bc), lambda q, i, c_ref: (i, 2 * q + c_ref[0]))
    else:
        g_spec = pl.BlockSpec((tile, bc), lambda q, i, c_ref: ((2 * q + c_ref[0]) * nrt + i, 0))
    slot = pl.BlockSpec((1, tile, bc), lambda q, i, c_ref: (q, i, 0))
    return pl.pallas_call(
        body, name=name, out_shape=_sds((n, br, bc), recv.dtype),
        grid_spec=pltpu.PrefetchScalarGridSpec(num_scalar_prefetch=1, grid=(n, nrt), in_specs=[g_spec, slot],
                                               out_specs=slot),
        compiler_params=_params(("arbitrary", "arbitrary")))(core, g, recv)


def _adam_math(w, m, v, g):
    c1 = 1.0 / (1.0 - ADAM_B1 ** ADAM_STEP)
    c2 = 1.0 / (1.0 - ADAM_B2 ** ADAM_STEP)
    mm = ADAM_B1 * m + (1.0 - ADAM_B1) * g
    vv = ADAM_B2 * v + (1.0 - ADAM_B2) * (g * g)
    return -ADAM_LR * ((mm * c1) / (jnp.sqrt(vv * c2) + ADAM_EPS) + ADAM_WD * w), mm, vv


def _adamw(w, m, v, gparts, name):
    r, c = w.shape
    n = gparts.shape[0]
    tile = _tile(r, 256, 16)

    def body(w_ref, m_ref, v_ref, g_ref, go_ref, d_ref, mo_ref, vo_ref):
        g = g_ref[0].astype(F32)
        for j in range(1, n):
            g = g + g_ref[j].astype(F32)
        go_ref[...] = g
        d_ref[...], mo_ref[...], vo_ref[...] = _adam_math(w_ref[...], m_ref[...], v_ref[...], g)

    rs = lambda arr: _row_in(arr, tile)
    out = ((r, c), F32, (tile, c), lambda i: (i, 0))
    return _rows(name, body, r // tile, [rs(w), rs(m), rs(v), (gparts, (n, tile, c), lambda i: (0, i, 0))],
                 [out, out, out, out])


def _adamw_scattered(w, m, v, layer, p, recv, name, filled=None):
    nl, r, c = w.shape
    n = recv.shape[0]
    tile = _tile(r, 256, 16)
    chip = (2 * lax.axis_index("x") + lax.axis_index("y")).astype(jnp.int32).reshape(1)
    n_prev = 0 if filled is None else len(filled)

    def body(q_ref, w_ref, m_ref, v_ref, p_ref, g_ref, *rest):
        go_ref, d_ref, mo_ref, vo_ref = rest[n_prev:]
        g = p_ref[0].astype(F32)
        for j in range(n):
            g = g + g_ref[j].astype(F32)
        go_ref[0] = g
        d_ref[0], mo_ref[0], vo_ref[0] = _adam_math(w_ref[0], m_ref[0], v_ref[0], g)

    slab = pl.BlockSpec((1, tile, c), lambda i, q_ref: (layer, i, 0))
    anywhere = pl.BlockSpec(memory_space=pl.ANY)
    out = _sds((nl, r, c), F32)
    prev = [] if filled is None else list(filled)
    return pl.pallas_call(
        body, name=name, out_shape=[out, out, out, out],
        grid_spec=pltpu.PrefetchScalarGridSpec(
            num_scalar_prefetch=1, grid=(r // tile,),
            in_specs=[slab, slab, slab, pl.BlockSpec((1, tile, c), lambda i, q_ref: (q_ref[0], i, 0)),
                      pl.BlockSpec((n, tile, c), lambda i, q_ref: (0, i, 0))] + [anywhere] * n_prev,
            out_specs=[slab, slab, slab, slab]),
        input_output_aliases={6 + j: j for j in range(n_prev)},
        compiler_params=_params(("arbitrary",)))(chip, w, m, v, p, recv, *prev)


def _cmul(ar, ai, br, bi):
    return ar * br - ai * bi, ar * bi + ai * br


def _cpow(ar, ai, n):
    pr, pi = jnp.ones_like(ar), jnp.zeros_like(ar)
    br, bi = ar, ai
    while n:
        if n & 1:
            pr, pi = _cmul(pr, pi, br, bi)
        n >>= 1
        if n:
            br, bi = _cmul(br, bi, br, bi)
    return pr, pi


def _s5_scan_into(x_ref, ar1, ai1, ns, fin_ref, hin_ref, reverse):
    st = ar1.shape[1]
    ar = jnp.broadcast_to(ar1, (N_SEG, st))
    ai = jnp.broadcast_to(ai1, (N_SEG, st))
    zero = jnp.zeros((N_SEG, st), F32)

    def slab(k):
        return pl.ds(pl.multiple_of(k * N_SEG, N_SEG), N_SEG)

    def pass1(j, carry):
        hr, hi = carry
        k = ns - 1 - j if reverse else j
        nr, ni = _cmul(ar, ai, hr, hi)
        return nr + x_ref[slab(k), :st], ni + x_ref[slab(k), st:]

    fr, fi = lax.fori_loop(0, ns, pass1, (zero, zero))
    fin_ref[:, :st] = fr
    fin_ref[:, st:] = fi
    pr, pi = _cpow(ar1, ai1, ns)
    order = list(range(N_SEG - 1, -1, -1)) if reverse else list(range(N_SEG))
    hin_ref[order[0]:order[0] + 1, :] = jnp.zeros((1, 2 * st), F32)
    for a_, b_ in zip(order[:-1], order[1:]):
        cr, ci = _cmul(pr, pi, hin_ref[a_:a_ + 1, :st], hin_ref[a_:a_ + 1, st:])
        hin_ref[b_:b_ + 1, :st] = cr + fin_ref[a_:a_ + 1, :st]
        hin_ref[b_:b_ + 1, st:] = ci + fin_ref[a_:a_ + 1, st:]

    def pass2(j, carry):
        hr, hi = carry
        k = ns - 1 - j if reverse else j
        nr, ni = _cmul(ar, ai, hr, hi)
        nr = nr + x_ref[slab(k), :st]
        ni = ni + x_ref[slab(k), st:]
        x_ref[slab(k), :st] = nr
        x_ref[slab(k), st:] = ni
        return nr, ni

    lax.fori_loop(0, ns, pass2, (hin_ref[:, :st], hin_ref[:, st:]))


def _s5_specs(r, ch, st):
    u_spec = pl.BlockSpec((r, ch), lambda j: (0, j // 2))
    w_spec = pl.BlockSpec((1, ch, 2 * st), lambda j: (j, 0, 0))
    c_spec = pl.BlockSpec((1, 2 * st, ch), lambda j: (j, 0, 0))
    a_spec = pl.BlockSpec((1, 2, st), lambda j: (j, 0, 0))
    return u_spec, w_spec, c_spec, a_spec


def _s5_fwd(up, w, c, a, rev, name):
    r, s = up.shape
    nh, ch, st2 = w.shape
    st = st2 // 2
    ns = r // N_SEG
    nb = r // N_DEV
    u_spec, w_spec, c_spec, a_spec = _s5_specs(r, ch, st)

    def body(u_ref, w_ref, c_ref, a_ref, y_ref, x, fin, hin):
        j = pl.program_id(0)
        w_b = w_ref[0].astype(MXU_DTYPE)
        c_b = c_ref[0].astype(MXU_DTYPE)
        for rb in range(N_DEV):
            rows = slice(rb * nb, (rb + 1) * nb)
            x[rows, :] = jnp.dot(u_ref[rows, :].astype(MXU_DTYPE), w_b, preferred_element_type=F32)
        _s5_scan_into(x, a_ref[0, 0:1, :], a_ref[0, 1:2, :], ns, fin, hin, rev)
        for rb in range(N_DEV):
            rows = slice(rb * nb, (rb + 1) * nb)
            yb = jnp.dot(x[rows, :].astype(MXU_DTYPE), c_b, preferred_element_type=F32)

            @pl.when(j % 2 == 0)
            def _():
                y_ref[rows, :] = yb

            @pl.when(j % 2 == 1)
            def _():
                y_ref[rows, :] += yb

    small = pltpu.VMEM((N_SEG, st2), F32)
    return pl.pallas_call(
        body, name=name, grid=(nh,), in_specs=[u_spec, w_spec, c_spec, a_spec],
        out_specs=pl.BlockSpec((r, ch), lambda j: (0, j // 2)), out_shape=_sds((r, s), F32),
        scratch_shapes=[pltpu.VMEM((r, st2), F32), small, small],
        compiler_params=_params(("arbitrary",)))(up, w, c, a)


def _s5_bwd(up, dyp, w, c, a, rev, name):
    r, s = up.shape
    nh, ch, st2 = w.shape
    st = st2 // 2
    ns = r // N_SEG
    nb = r // N_DEV
    u_spec, w_spec, c_spec, a_spec = _s5_specs(r, ch, st)
    nt = (((1,), (1,)), ((), ()))
    tn = (((0,), (0,)), ((), ()))

    def body(u_ref, dy_ref, w_ref, c_ref, a_ref, du_ref, dw_ref, dc_ref, da_ref, h, g, fin, sin_, ein):
        j = pl.program_id(0)
        w_b = w_ref[0].astype(MXU_DTYPE)
        c_b = c_ref[0].astype(MXU_DTYPE)
        for rb in range(N_DEV):
            rows = slice(rb * nb, (rb + 1) * nb)
            h[rows, :] = jnp.dot(u_ref[rows, :].astype(MXU_DTYPE), w_b, preferred_element_type=F32)
        ar1, ai1 = a_ref[0, 0:1, :], a_ref[0, 1:2, :]
        _s5_scan_into(h, ar1, ai1, ns, fin, sin_, rev)
        dc = jnp.zeros((st2, ch), F32)
        for rb in range(N_DEV):
            rows = slice(rb * nb, (rb + 1) * nb)
            dyb = dy_ref[rows, :].astype(MXU_DTYPE)
            g[rows, :] = lax.dot_general(dyb, c_b, nt, preferred_element_type=F32)
            dc += lax.dot_general(h[rows, :].astype(MXU_DTYPE), dyb, tn, preferred_element_type=F32)
        dc_ref[0] = dc
        _s5_scan_into(g, ar1, -ai1, ns, fin, ein, not rev)

        def slab(k):
            return pl.ds(pl.multiple_of(k * N_SEG, N_SEG), N_SEG)

        step_back = 1 if rev else -1

        def acc_step(k, carry):
            acr, aci = carry
            g_r, g_i = g[slab(k), :st], g[slab(k), st:]
            p_r, p_i = h[slab(k + step_back), :st], h[slab(k + step_back), st:]
            return acr + g_r * p_r + g_i * p_i, aci + g_i * p_r - g_r * p_i

        edge = (ns - 1) * N_SEG if rev else 0
        g_r, g_i = g[edge:edge + N_SEG, :st], g[edge:edge + N_SEG, st:]
        p_r, p_i = sin_[:, :st], sin_[:, st:]
        lo, hi_k = (0, ns - 1) if rev else (1, ns)
        acr, aci = lax.fori_loop(lo, hi_k, acc_step, (g_r * p_r + g_i * p_i, g_i * p_r - g_r * p_i))
        da_ref[0, 0:1, :] = jnp.sum(acr, axis=0, keepdims=True)
        da_ref[0, 1:2, :] = jnp.sum(aci, axis=0, keepdims=True)
        dw = jnp.zeros((ch, st2), F32)
        for rb in range(N_DEV):
            rows = slice(rb * nb, (rb + 1) * nb)
            gb = g[rows, :].astype(MXU_DTYPE)
            dub = lax.dot_general(gb, w_b, nt, preferred_element_type=F32)
            dw += lax.dot_general(u_ref[rows, :].astype(MXU_DTYPE), gb, tn, preferred_element_type=F32)

            @pl.when(j % 2 == 0)
            def _():
                du_ref[rows, :] = dub

            @pl.when(j % 2 == 1)
            def _():
                du_ref[rows, :] += dub

        dw_ref[0] = dw

    small = pltpu.VMEM((N_SEG, st2), F32)
    big = pltpu.VMEM((r, st2), F32)
    return pl.pallas_call(
        body, name=name, grid=(nh,), in_specs=[u_spec, u_spec, w_spec, c_spec, a_spec],
        out_specs=[pl.BlockSpec((r, ch), lambda j: (0, j // 2)), w_spec, c_spec, a_spec],
        out_shape=[_sds((r, s), F32), _sds(w.shape, F32), _sds(c.shape, F32), _sds(a.shape, F32)],
        scratch_shapes=[big, big, small, small, small],
        compiler_params=_params(("arbitrary",)))(up, dyp, w, c, a)


def _rope(t, cos, sin):
    quarter = t.shape[1] // 4
    lane = lax.broadcasted_iota(jnp.int32, t.shape, 1)
    first = (lane // quarter) % 2 == 0
    partner = jnp.where(first, pltpu.roll(t, t.shape[1] - quarter, 1), pltpu.roll(t, quarter, 1))
    return t * cos + partner * sin


def _rope_t(d, cos, sin):
    quarter = d.shape[1] // 4
    ds_ = d * sin
    lane = lax.broadcasted_iota(jnp.int32, d.shape, 1)
    first = (lane // quarter) % 2 == 0
    partner = jnp.where(first, pltpu.roll(ds_, d.shape[1] - quarter, 1), pltpu.roll(ds_, quarter, 1))
    return d * cos + partner


def _chunk_of_step(s, nch, ncc, rev):
    if not rev:
        return s
    return jnp.where(s < ncc, ncc - 1 - s, nch + ncc - 1 - s)


def _heads_per_step(heads, dk, dv, q_off):
    for hpg in range(heads, 0, -1):
        if (heads % hpg == 0 and q_off % (hpg * dk) == 0 and (heads * dk) % (hpg * dk) == 0
                and (q_off + 2 * heads * dk) % (hpg * dv) == 0):
            return hpg
    return 1


def _ret_fwd(hm, cos, sin, decay, wend, win, gch, heads, dk, dv, q_off, ncc, rev, name):
    r = hm.shape[0]
    ch = RET_CHUNK
    nch = r // ch
    t_rows = r - ncc * ch
    hpg = _heads_per_step(heads, dk, dv, q_off)
    qb, kb, vb = q_off // (hpg * dk), (q_off + heads * dk) // (hpg * dk), (q_off + 2 * heads * dk) // (hpg * dv)
    q_scale = dk ** -0.5
    nt = (((1,), (1,)), ((), ()))
    tn = (((0,), (0,)), ((), ()))
    cof = lambda s: _chunk_of_step(s, nch, ncc, rev)

    def body(q_ref, k_ref, v_ref, cos_ref, sin_ref, dec_ref, we_ref, wi_ref, g_ref, o_ref, sin_out, st):
        s = pl.program_id(1)

        @pl.when(s == 0)
        def _():
            st[...] = jnp.zeros_like(st)

        cos_, sin_ = cos_ref[...], sin_ref[...]
        for hl in range(hpg):
            ks, vs = slice(hl * dk, (hl + 1) * dk), slice(hl * dv, (hl + 1) * dv)
            q = _rope(q_ref[:, ks], cos_, sin_) * q_scale
            k = _rope(k_ref[:, ks], cos_, sin_)
            v = v_ref[:, vs].astype(MXU_DTYPE)
            s_cur = st[hl]
            sin_out[hl, 0] = s_cur
            kw = (k * we_ref[hl]).astype(MXU_DTYPE)
            qw = (q * wi_ref[hl]).astype(MXU_DTYPE)
            scores = lax.dot_general(q.astype(MXU_DTYPE), k.astype(MXU_DTYPE), nt,
                                     preferred_element_type=F32) * dec_ref[hl]
            o_ref[:, vs] = (jnp.dot(scores.astype(MXU_DTYPE), v, preferred_element_type=F32)
                            + jnp.dot(qw, s_cur.astype(MXU_DTYPE), preferred_element_type=F32))
            st[hl] = g_ref[hl] * s_cur + lax.dot_general(kw, v, tn, preferred_element_type=F32)

    tab = lambda w: pl.BlockSpec((hpg, ch, w), lambda h, s: (h, 0, 0))
    return pl.pallas_call(
        body, name=name, grid=(heads // hpg, nch),
        in_specs=[pl.BlockSpec((ch, hpg * dk), lambda h, s: (cof(s), qb + h)),
                  pl.BlockSpec((ch, hpg * dk), lambda h, s: (cof(s), kb + h)),
                  pl.BlockSpec((ch, hpg * dv), lambda h, s: (cof(s), vb + h)),
                  pl.BlockSpec((ch, dk), lambda h, s: (cof(s), 0)),
                  pl.BlockSpec((ch, dk), lambda h, s: (cof(s), 0)),
                  tab(ch), tab(dk), tab(dk), tab(dv)],
        out_specs=[pl.BlockSpec((ch, hpg * dv), lambda h, s: (jnp.maximum(cof(s) - ncc, 0) if not rev
                                                               else jnp.where(s < ncc, nch - ncc - 1, cof(s) - ncc), h)),
                   pl.BlockSpec((hpg, 1, dk, dv), lambda h, s: (h, s, 0, 0))],
        out_shape=[_sds((t_rows, heads * dv), F32), _sds((heads, nch, dk, dv), F32)],
        scratch_shapes=[pltpu.VMEM((hpg, dk, dv), F32)],
        compiler_params=_params(("parallel", "arbitrary")))(hm, hm, hm, cos, sin, decay, wend, win, gch)


def _ret_bwd(hm, cos, sin, decay, wend, win, gch, s_in, do, heads, dk, dv, q_off, ncc, rev, name):
    r = hm.shape[0]
    ch = RET_CHUNK
    nch = r // ch
    hpg = _heads_per_step(heads, dk, dv, q_off)
    qb, kb, vb = q_off // (hpg * dk), (q_off + heads * dk) // (hpg * dk), (q_off + 2 * heads * dk) // (hpg * dv)
    q_scale = dk ** -0.5
    nt = (((1,), (1,)), ((), ()))
    tn = (((0,), (0,)), ((), ()))
    cof = lambda rr: _chunk_of_step(nch - 1 - rr, nch, ncc, rev)

    def body(q_ref, k_ref, v_ref, cos_ref, sin_ref, dec_ref, we_ref, wi_ref, g_ref, sin_ref2, do_ref,
             dq_ref, dk_ref, dv_ref, ddec_ref, dwe_ref, dwi_ref, dg_ref, dst):
        rr = pl.program_id(1)
        n = cof(rr)

        @pl.when(rr == 0)
        def _():
            dst[...] = jnp.zeros_like(dst)
            ddec_ref[...] = jnp.zeros_like(ddec_ref)
            dwe_ref[...] = jnp.zeros_like(dwe_ref)
            dwi_ref[...] = jnp.zeros_like(dwi_ref)
            dg_ref[...] = jnp.zeros_like(dg_ref)

        cos_, sin_ = cos_ref[...], sin_ref[...]
        for hl in range(hpg):
            ks, vs = slice(hl * dk, (hl + 1) * dk), slice(hl * dv, (hl + 1) * dv)
            q = _rope(q_ref[:, ks], cos_, sin_) * q_scale
            k = _rope(k_ref[:, ks], cos_, sin_)
            v = v_ref[:, vs].astype(MXU_DTYPE)
            qb_, kb_ = q.astype(MXU_DTYPE), k.astype(MXU_DTYPE)
            kw = (k * we_ref[hl]).astype(MXU_DTYPE)
            qw = (q * wi_ref[hl]).astype(MXU_DTYPE)
            sraw = lax.dot_general(qb_, kb_, nt, preferred_element_type=F32)
            scores = (sraw * dec_ref[hl]).astype(MXU_DTYPE)
            d_o = jnp.where(n >= ncc, do_ref[:, vs], 0.0).astype(MXU_DTYPE)
            s_n = sin_ref2[hl, 0]
            s_nb = s_n.astype(MXU_DTYPE)
            ds1 = dst[hl]
            ds1b = ds1.astype(MXU_DTYPE)
            dsc = lax.dot_general(d_o, v, nt, preferred_element_type=F32)
            dsr = (dsc * dec_ref[hl]).astype(MXU_DTYPE)
            ddec_ref[hl] += dsc * sraw
            t1 = lax.dot_general(d_o, s_nb, nt, preferred_element_type=F32)
            dq_r = jnp.dot(dsr, kb_, preferred_element_type=F32) + t1 * wi_ref[hl]
            dwi_ref[hl] += t1 * q
            t2 = lax.dot_general(v, ds1b, nt, preferred_element_type=F32)
            dk_r = lax.dot_general(dsr, qb_, tn, preferred_element_type=F32) + t2 * we_ref[hl]
            dwe_ref[hl] += t2 * k
            dv_ref[:, vs] = (lax.dot_general(scores, d_o, tn, preferred_element_type=F32)
                             + jnp.dot(kw, ds1b, preferred_element_type=F32))
            dg_ref[hl] += ds1 * s_n
            dst[hl] = g_ref[hl] * ds1 + lax.dot_general(qw, d_o, tn, preferred_element_type=F32)
            dq_ref[:, ks] = _rope_t(dq_r, cos_, sin_) * q_scale
            dk_ref[:, ks] = _rope_t(dk_r, cos_, sin_)

    tab = lambda w: pl.BlockSpec((hpg, ch, w), lambda h, rr: (h, 0, 0))
    return pl.pallas_call(
        body, name=name, grid=(heads // hpg, nch),
        in_specs=[pl.BlockSpec((ch, hpg * dk), lambda h, rr: (cof(rr), qb + h)),
                  pl.BlockSpec((ch, hpg * dk), lambda h, rr: (cof(rr), kb + h)),
                  pl.BlockSpec((ch, hpg * dv), lambda h, rr: (cof(rr), vb + h)),
                  pl.BlockSpec((ch, dk), lambda h, rr: (cof(rr), 0)),
                  pl.BlockSpec((ch, dk), lambda h, rr: (cof(rr), 0)),
                  tab(ch), tab(dk), tab(dk), tab(dv),
                  pl.BlockSpec((hpg, 1, dk, dv), lambda h, rr: (h, nch - 1 - rr, 0, 0)),
                  pl.BlockSpec((ch, hpg * dv), lambda h, rr: (jnp.maximum(cof(rr) - ncc, 0), h))],
        out_specs=[pl.BlockSpec((ch, hpg * dk), lambda h, rr: (cof(rr), h)),
                   pl.BlockSpec((ch, hpg * dk), lambda h, rr: (cof(rr), h)),
                   pl.BlockSpec((ch, hpg * dv), lambda h, rr: (cof(rr), h)),
                   tab(ch), tab(dk), tab(dk), tab(dv)],
        out_shape=[_sds((r, heads * dk), F32), _sds((r, heads * dk), F32), _sds((r, heads * dv), F32),
                   _sds(decay.shape, F32), _sds(wend.shape, F32), _sds(win.shape, F32), _sds(gch.shape, F32)],
        scratch_shapes=[pltpu.VMEM((hpg, dk, dv), F32)],
        compiler_params=_params(("parallel", "arbitrary")))(hm, hm, hm, cos, sin, decay, wend, win, gch, s_in, do)


_HBM = pl.BlockSpec(memory_space=pltpu.HBM)
_MESH = pl.DeviceIdType.MESH
ALL_GATHER_COLLECTIVE_ID = 1
SIBLING_COLLECTIVE_ID = 2
CHIPS_COLLECTIVE_ID = 3


def _axis_slice(ref, axis, start, size):
    idx = [slice(None)] * len(ref.shape)
    idx[axis] = pl.ds(start, size)
    return ref.at[tuple(idx)]


def _sibling_and_chip_peers():
    x, y, c = lax.axis_index("x"), lax.axis_index("y"), lax.axis_index("c")
    return [(x, y, 1 - c), (1 - x, y, c), (x, 1 - y, c), (1 - x, 1 - y, c)]


def _launch_exchange(body, name, operand, out_shape, sems, peers_fn, collective_id, on_sequencer):
    if not on_sequencer:
        return pl.pallas_call(body, name=name, out_shape=out_shape, in_specs=[_HBM], out_specs=_HBM,
                              scratch_shapes=sems)(operand)

    def sequencer_body(in_ref, out_ref, *sem_refs):
        peers = peers_fn()
        barrier = pltpu.get_barrier_semaphore()
        for peer in peers:
            pl.semaphore_signal(barrier, inc=1, device_id=peer, device_id_type=_MESH)
        pl.semaphore_wait(barrier, len(peers))
        body(in_ref, out_ref, *sem_refs)

    return pl.kernel(sequencer_body, out_type=out_shape, name=name,
                     mesh=plsc.ScalarSubcoreMesh(axis_name="sequencer", num_cores=1), scratch_types=sems,
                     compiler_params=pltpu.CompilerParams(collective_id=collective_id))(operand)


def _all_gather(shard, axis, name, on_sequencer=False):
    m = shard.shape[axis]
    out_shape = list(shard.shape)
    out_shape[axis] = N_DEV * m

    def body(x_ref, out_ref, send_sems, recv_sems, local_sem):
        x, y, c = lax.axis_index("x"), lax.axis_index("y"), lax.axis_index("c")
        me, sibling = (x, y, c), (x, y, 1 - c)
        chips = [(1 - x, y), (x, 1 - y), (1 - x, 1 - y)]

        def block(px, py, pc):
            return _axis_slice(out_ref, axis, (4 * px + 2 * py + pc) * m, m)

        def copy(k, blk, to, src=None):
            return pltpu.make_async_remote_copy(
                src_ref=block(*blk) if src is None else src, dst_ref=block(*blk), send_sem=send_sems.at[k],
                recv_sem=recv_sems.at[k], device_id=to, device_id_type=_MESH)

        mine = pltpu.make_async_copy(x_ref, block(*me), local_sem)
        mine.start()
        first = [copy(0, me, sibling, src=x_ref)]
        first += [copy(1 + j, me, (*chip, c), src=x_ref) for j, chip in enumerate(chips)]
        for cp in first:
            cp.start()
        passed = [copy(4 + j, (*chip, c), sibling) for j, chip in enumerate(chips)]
        for j, chip in enumerate(chips):
            copy(1 + j, (*chip, c), me).wait_recv()
            passed[j].start()
        copy(0, sibling, me).wait_recv()
        for j, chip in enumerate(chips):
            copy(4 + j, (*chip, 1 - c), me).wait_recv()
        for cp in first + passed:
            cp.wait_send()
        mine.wait()

    return _launch_exchange(
        body, name, shard, _sds(out_shape, shard.dtype),
        [pltpu.SemaphoreType.DMA((7,)), pltpu.SemaphoreType.DMA((7,)), pltpu.SemaphoreType.DMA(())],
        _sibling_and_chip_peers, ALL_GATHER_COLLECTIVE_ID, on_sequencer)


def _rs_sibling(g, axis, name, on_sequencer=False):
    m = g.shape[axis] // N_DEV
    blk_shape = list(g.shape)
    blk_shape[axis] = m
    n_chips = N_DEV // 2

    def body(g_ref, recv_ref, send_sems, recv_sems):
        x, y, c = lax.axis_index("x"), lax.axis_index("y"), lax.axis_index("c")
        sibling = (x, y, 1 - c)
        send = [pltpu.make_async_remote_copy(
            src_ref=_axis_slice(g_ref, axis, (2 * q + 1 - c) * m, m), dst_ref=recv_ref.at[q],
            send_sem=send_sems.at[q], recv_sem=recv_sems.at[q], device_id=sibling, device_id_type=_MESH)
            for q in range(n_chips)]
        for cp in send:
            cp.start()
        for cp in send:
            cp.wait_recv()
        for cp in send:
            cp.wait_send()

    return _launch_exchange(
        body, name, g, _sds([n_chips] + blk_shape, g.dtype),
        [pltpu.SemaphoreType.DMA((n_chips,)), pltpu.SemaphoreType.DMA((n_chips,))],
        lambda: _sibling_and_chip_peers()[:1], SIBLING_COLLECTIVE_ID, on_sequencer)


def _rs_chips(p, name, on_sequencer=False):
    n_peers = p.shape[0] - 1

    def body(p_ref, out_ref, send_sems, recv_sems):
        x, y, c = lax.axis_index("x"), lax.axis_index("y"), lax.axis_index("c")
        chips = [(1 - x, y), (x, 1 - y), (1 - x, 1 - y)]
        send = [pltpu.make_async_remote_copy(
            src_ref=p_ref.at[2 * cx + cy], dst_ref=out_ref.at[j], send_sem=send_sems.at[j],
            recv_sem=recv_sems.at[j], device_id=(cx, cy, c), device_id_type=_MESH)
            for j, (cx, cy) in enumerate(chips)]
        for cp in send:
            cp.start()
        for cp in send:
            cp.wait_recv()
        for cp in send:
            cp.wait_send()

    return _launch_exchange(
        body, name, p, _sds((n_peers,) + p.shape[1:], p.dtype),
        [pltpu.SemaphoreType.DMA((n_peers,)), pltpu.SemaphoreType.DMA((n_peers,))],
        lambda: _sibling_and_chip_peers()[1:], CHIPS_COLLECTIVE_ID, on_sequencer)


def _reduce_scatter(g, axis, name):
    sib = _rs_sibling(g, axis, name + "_d2d", on_sequencer=True)
    p = _pair_sum(g, sib, axis, name + "_pair")
    return p, _rs_chips(p, name + "_ici", on_sequencer=True)


def _s5_tables(lam_re, lam_im, log_step, b_re, b_im, c_re, c_im):
    nd, g, p, cg = b_re.shape
    step = jnp.exp(log_step)[..., None]
    mag = jnp.exp(lam_re * step)
    a_re, a_im = mag * jnp.cos(lam_im * step), mag * jnp.sin(lam_im * step)
    den = lam_re * lam_re + lam_im * lam_im
    num_re, num_im = a_re - 1.0, a_im
    k_re = (num_re * lam_re + num_im * lam_im) / den
    k_im = (num_im * lam_re - num_re * lam_im) / den
    bb_re = k_re[..., None] * b_re - k_im[..., None] * b_im
    bb_im = k_re[..., None] * b_im + k_im[..., None] * b_re
    gt = g // SSM_TILE_GROUPS
    hg = SSM_HALF_GROUPS
    eye = jnp.eye(SSM_TILE_GROUPS, dtype=F32).reshape(SSM_TILE_GROUPS, 2, hg)

    def pack_b(bb):
        w = jnp.einsum("djhqpc,ghq->djhgcqp", bb.reshape(nd, gt, 2, hg, p, cg), eye)
        return w.reshape(nd, gt * 2, SSM_TILE_GROUPS * cg, hg * p)

    def pack_c(cc):
        w = jnp.einsum("djhqcp,ghq->djhqpgc", cc.reshape(nd, gt, 2, hg, cg, p), eye)
        return w.reshape(nd, gt * 2, hg * p, SSM_TILE_GROUPS * cg)

    a = jnp.stack([a_re.reshape(nd, gt * 2, hg * p), a_im.reshape(nd, gt * 2, hg * p)], axis=2)
    w = jnp.concatenate([pack_b(bb_re), pack_b(bb_im)], axis=-1)
    c = jnp.concatenate([pack_c(c_re), -pack_c(c_im)], axis=-2)
    return w, c, a


def _ret_tables(decay_logit, dk, dv):
    ch = RET_CHUNK
    nd, h = decay_logit.shape
    lg = jax.nn.log_sigmoid(decay_logit)[:, :, None]
    pos = jnp.arange(ch, dtype=F32)
    fwd_diff = pos[:, None] - pos[None, :]
    diff = jnp.stack([fwd_diff, -fwd_diff])[:, None]
    mask = jnp.stack([fwd_diff >= 0, -fwd_diff > 0])[:, None]
    end_pos = jnp.stack([ch - 1.0 - pos, pos])[:, None]
    in_pos = jnp.stack([pos + 1.0, ch - pos])[:, None]
    w_end = jnp.exp(lg * end_pos)
    w_in = jnp.exp(lg * in_pos)
    decay = jnp.where(mask, jnp.exp(lg[..., None] * jnp.where(mask, diff, 0.0)), 0.0)
    g_chunk = jnp.exp(lg[..., 0] * ch)
    return (decay, jnp.broadcast_to(w_end[..., None], (nd, h, ch, dk)), jnp.broadcast_to(w_in[..., None], (nd, h, ch, dk)),
            jnp.broadcast_to(g_chunk[..., None, None], (nd, h, dk, dv)))


def _rope_tables(t_rows, ncc, dk):
    quarter = dk // 4
    idx = np.arange(t_rows)
    row, col = idx // GRID_W, idx % GRID_W
    inv = ROPE_BASE ** (-np.arange(quarter, dtype=np.float32) / quarter)
    ang_r = row.astype(np.float32)[:, None] * inv
    ang_c = col.astype(np.float32)[:, None] * inv
    ang_r, ang_c = jnp.asarray(ang_r, F32), jnp.asarray(ang_c, F32)
    cos = jnp.concatenate([jnp.cos(ang_r), jnp.cos(ang_r), jnp.cos(ang_c), jnp.cos(ang_c)], axis=1)
    sin = jnp.concatenate([-jnp.sin(ang_r), jnp.sin(ang_r), -jnp.sin(ang_c), jnp.sin(ang_c)], axis=1)
    n_ctx = ncc * RET_CHUNK
    cos = jnp.concatenate([jnp.ones((n_ctx, dk), F32), cos], axis=0)
    sin = jnp.concatenate([jnp.zeros((n_ctx, dk), F32), sin], axis=0)
    return cos, sin


def _to_scan_layout(ctx_rows, lat_rows, rev):
    u = jnp.concatenate([lat_rows, ctx_rows] if rev else [ctx_rows, lat_rows], axis=0)
    r, w = u.shape
    return u.reshape(N_SEG, r // N_SEG, w).transpose(1, 0, 2).reshape(r, w)


def _from_scan_layout(yp, n_ctx, rev):
    r, w = yp.shape
    y = yp.reshape(r // N_SEG, N_SEG, w).transpose(1, 0, 2).reshape(r, w)
    return (y[r - n_ctx:], y[:r - n_ctx]) if rev else (y[:n_ctx], y[n_ctx:])


def _pack(parts, width):
    rows = []
    for p in parts:
        flat = p.reshape(-1).astype(F32)
        n = flat.shape[0]
        rows.append(jnp.pad(flat, (0, -n % (SUBLANE * width))).reshape(-1, width))
    return jnp.concatenate(rows, axis=0)


def _packed_rows(n, width):
    return -(-n // (SUBLANE * width)) * SUBLANE


def _unpack(flat2d, shapes):
    width = flat2d.shape[1]
    out, row = [], 0
    for shp in shapes:
        n = int(np.prod(shp))
        nr = _packed_rows(n, width)
        out.append(flat2d[row:row + nr].reshape(-1)[:n].reshape(shp))
        row += nr
    return out


def kernel(x, c, ctx, c_ctx, ada_w, ada_b, norm_g, ffn_w_in, ffn_w_out, mix_w_in, ssm_lam_re, ssm_lam_im, ssm_log_step, ssm_b_re, ssm_b_im, ssm_c_re, ssm_c_im, ssm_d, ssm_glu_w, ret_decay_logit, ret_w_proj, mix_w_out, loss_target, m_c_ctx, m_ada_w, m_ada_b, m_norm_g, m_ffn_w_in, m_ffn_w_out, m_mix_w_in, m_ssm_lam_re, m_ssm_lam_im, m_ssm_log_step, m_ssm_b_re, m_ssm_b_im, m_ssm_c_re, m_ssm_c_im, m_ssm_d, m_ssm_glu_w, m_ret_decay_logit, m_ret_w_proj, m_mix_w_out, v_c_ctx, v_ada_w, v_ada_b, v_norm_g, v_ffn_w_in, v_ffn_w_out, v_mix_w_in, v_ssm_lam_re, v_ssm_lam_im, v_ssm_log_step, v_ssm_b_re, v_ssm_b_im, v_ssm_c_re, v_ssm_c_im, v_ssm_d, v_ssm_glu_w, v_ret_decay_logit, v_ret_w_proj, v_mix_w_out):
    t_rows, d = x.shape[1], x.shape[2]
    n_ctx = ctx.shape[1]
    r = n_ctx + t_rows
    ssm_w = ssm_d.shape[1]
    heads = ret_decay_logit.shape[2]
    mi = mix_w_in.shape[2] * N_DEV
    dk = (mi - ssm_w - 2 * d) // (6 * heads)
    dv = 2 * dk
    qk_w, v_w = heads * dk, heads * dv
    q_off = ssm_w
    ncc = n_ctx // RET_CHUNK
    tile = n_ctx
    nct = 1
    wide_tile = _tile(n_ctx, 128, 16)
    assert r % (N_SEG * SUBLANE) == 0 and n_ctx % RET_CHUNK == 0 and t_rows % tile == 0
    me = 4 * lax.axis_index("x") + 2 * lax.axis_index("y") + lax.axis_index("c")
    g_off = ssm_w + 2 * qk_w + v_w
    gs_off = g_off + v_w

    ng_cols = norm_g.shape[2]
    small0 = _pack([c[0], norm_g[0]], d)
    small0_all = _all_gather(small0, 0, "ag_cond")

    bf = lambda w: w.astype(BF16)
    small0_all, sh_in1, sh_out1, sh_mix = lax.optimization_barrier(
        (small0_all, bf(ffn_w_in[0, 0]), bf(ffn_w_out[0, 0]), bf(mix_w_in[0])))
    small0_all = small0_all.reshape(N_DEV, -1)
    w_in1 = _all_gather(sh_in1, 1, "ag_ffn1_in", on_sequencer=True)
    w_out1 = _all_gather(sh_out1, 0, "ag_ffn1_out", on_sequencer=True)
    w_mix = _all_gather(sh_mix, 1, "ag_mix_in", on_sequencer=True)
    w_glu = _all_gather(bf(ssm_glu_w[0]), 1, "ag_glu", on_sequencer=True)
    w_rp = _all_gather(bf(ret_w_proj[0]), 0, "ag_ret_proj", on_sequencer=True)
    w_mo = _all_gather(bf(mix_w_out[0]), 0, "ag_mix_out", on_sequencer=True)
    w_in2 = _all_gather(bf(ffn_w_in[0, 1]), 1, "ag_ffn2_in", on_sequencer=True)
    w_out2 = _all_gather(bf(ffn_w_out[0, 1]), 0, "ag_ffn2_out", on_sequencer=True)

    ng_at = _packed_rows(d, d) * d
    c_all = small0_all[:, :d]
    g_full = small0_all[:, ng_at:ng_at + 6 * ng_cols].reshape(N_DEV, 6, ng_cols).transpose(1, 0, 2).reshape(6, d)
    g6 = g_full.reshape(6, 1, d)
    cc = jnp.concatenate([c_all, c_ctx[None, :], jnp.zeros((2 * SUBLANE - N_DEV - 1, d), F32)], axis=0)
    sc = _silu_rows(cc, "ada_silu")
    na = ada_w.shape[2]
    a_loc = _mm(sc, ada_w[0], "nn", F32, "ada_fwd", tm=16, tn=na, tk=512)
    a_all = _all_gather(a_loc, 0, "ag_ada").reshape(N_DEV, 2 * SUBLANE, na)
    ada_x = lax.dynamic_index_in_dim(a_all, me, axis=1, keepdims=False).reshape(9 * d) + ada_b[0]
    ada_c = a_all[:, N_DEV, :].reshape(9 * d) + ada_b[0]
    mods = jnp.stack([ada_c.reshape(9, d), ada_x.reshape(9, d)]).reshape(18, 1, d)

    xin = jnp.concatenate([ctx[0], x[0]], axis=0)
    u1 = _ada_pre_fwd(xin, g6, mods, 0, 0, nct, tile, "pre1")
    h1 = _mm(u1, w_in1, "nn", F32, "ffn1_in", tm=544)
    a1 = _swiglu_fwd(h1, wide_tile, "swiglu1")
    o1 = _mm(a1, w_out1, "nn", F32, "ffn1_out", tm=544, tn=d, tk=1408)
    x1 = _ada_post_fwd(xin, o1, g6, mods, 1, 0, 0.5, nct, tile, "post1")
    u2 = _ada_pre_fwd(x1, g6, mods, 2, 1, nct, tile, "pre2")
    hm = _mm(u2, w_mix, "nn", F32, "mix_in", tm=544)

    us_ctx, us_lat = hm[:n_ctx, :ssm_w], hm[n_ctx:, :ssm_w]
    dskip = ssm_d.reshape(1, 1, ssm_w)
    s5_prm = (ssm_lam_re[0], ssm_lam_im[0], ssm_log_step[0], ssm_b_re[0], ssm_b_im[0], ssm_c_re[0], ssm_c_im[0])
    s5_tabs_both, s5_vjp = jax.vjp(_s5_tables, *s5_prm)
    s5_tabs, ups, y_dirs = [], [], []
    for dr in range(2):
        tabs = tuple(t[dr] for t in s5_tabs_both)
        up = _to_scan_layout(us_ctx, us_lat, dr == 1)
        yp = _s5_fwd(up, *tabs, dr == 1, "s5_fwd%d" % dr)
        s5_tabs.append(tabs)
        ups.append(up)
        y_dirs.append(_from_scan_layout(yp, n_ctx, dr == 1)[1])
    a_ssm = _ssm_out_fwd(y_dirs[0], y_dirs[1], hm, dskip, nct, tile, "ssm_out")
    gab = _mm(a_ssm, w_glu, "nn", F32, "glu", tm=512, tn=2048, tk=ssm_w)

    cos, sin = _rope_tables(t_rows, ncc, dk)
    ret_tabs_both, ret_vjp = jax.vjp(functools.partial(_ret_tables, dk=dk, dv=dv), ret_decay_logit[0])
    ret_tabs, o_dirs, s_ins = [], [], []
    for dr in range(2):
        tabs = tuple(t[dr] for t in ret_tabs_both)
        o_d, s_in = _ret_fwd(hm, cos, sin, *tabs, heads, dk, dv, q_off, ncc, dr == 1, "ret_fwd%d" % dr)
        ret_tabs.append(tabs)
        o_dirs.append(o_d)
        s_ins.append(s_in)
    ret_in = _ret_gate_fwd(o_dirs[0], o_dirs[1], hm, g_off, heads, dv, nct, tile, "ret_gate")
    rb = _mm(ret_in, w_rp, "nn", F32, "ret_proj", tm=512, tn=d, tk=v_w)
    merged = _merge_fwd(gab, rb, hm, gs_off, nct, tile, "merge")
    mix = _mm(merged, w_mo, "nn", F32, "mix_out", tm=512, tn=d, tk=d)
    x1x = x1[n_ctx:]
    x2 = _ada_post_fwd(x1x, mix, g6, mods, 3, 1, 1.0, 0, tile, "post2")
    u3 = _ada_pre_fwd(x2, g6, mods, 4, 2, 0, tile, "pre3")
    h3 = _mm(u3, w_in2, "nn", F32, "ffn2_in", tm=512)
    a3 = _swiglu_fwd(h3, wide_tile, "swiglu2")
    o3 = _mm(a3, w_out2, "nn", F32, "ffn2_out", tm=512, tn=d, tk=1408)
    x3 = _ada_post_fwd(x2, o3, g6, mods, 5, 2, 0.5, 0, tile, "post3")
    dy, lcols = _loss_grad(x3, loss_target[0], tile, "loss")
    loss_part = (0.5 * jnp.sum(lcols) / d).reshape(1)

    dg6 = [None] * 6
    dmod = {}

    def add_mod(sel_rows, k, val):
        for sel, row in sel_rows:
            dmod[(sel, k)] = dmod.get((sel, k), 0.0) + val[row, 0]

    both, lat = [(0, 0), (1, 1)], [(1, 0)]

    def tie(*vals):
        return lax.optimization_barrier(vals)

    def big_update(w3d, m3d, v3d, layer, gfull, axis, name, filled=None):
        p, recv = _reduce_scatter(gfull, axis, "rs_" + name)
        return _adamw_scattered(w3d, m3d, v3d, layer, p, recv, "adamw_" + name, filled)

    do3, dg6[5], dgt = _ada_post_bwd(dy, o3, g6, mods, 5, 2, 0.5, 0, 1, tile, "post3_bwd")
    add_mod(lat, 8, dgt)
    gw_out2 = _mm(a3, do3, "tn", BF16, "ffn2_out_dw", tm=1408, tn=1024, tk=2176)
    do3, gw_out2 = tie(do3, gw_out2)
    up_out2 = big_update(ffn_w_out[0], m_ffn_w_out[0], v_ffn_w_out[0], 1, gw_out2, 0, "ffn2_out")
    da3 = _mm(do3, w_out2, "nt", F32, "ffn2_out_dx", tm=512, tn=1408, tk=d)
    dh3 = _swiglu_bwd(h3, da3, wide_tile, "swiglu2_bwd")
    gw_in2 = _mm(u3, dh3, "tn", BF16, "ffn2_in_dw", tm=1024, tn=1408, tk=2176)
    dh3, gw_in2 = tie(dh3, gw_in2)
    up_in2 = big_update(ffn_w_in[0], m_ffn_w_in[0], v_ffn_w_in[0], 1, gw_in2, 1, "ffn2_in")
    du3 = _mm(dh3, w_in2, "nt", F32, "ffn2_in_dx", tm=512, tn=d, tk=1408)
    dx2, dg6[4], dsh, dsc = _ada_pre_bwd(x2, du3, dy, g6, mods, 4, 2, 0, 1, tile, "pre3_bwd")
    add_mod(lat, 6, dsh)
    add_mod(lat, 7, dsc)
    dmix, dg6[3], dgt = _ada_post_bwd(dx2, mix, g6, mods, 3, 1, 1.0, 0, 1, tile, "post2_bwd")
    add_mod(lat, 5, dgt)
    gw_mo = _mm(merged, dmix, "tn", BF16, "mix_out_dw", tm=1024, tn=1024, tk=2176)
    dmix, gw_mo = tie(dmix, gw_mo)
    up_mo = big_update(mix_w_out, m_mix_w_out, v_mix_w_out, 0, gw_mo, 0, "mix_out")
    dmerged = _mm(dmix, w_mo, "nt", F32, "mix_out_dx", tm=512, tn=d, tk=d)
    dgab, drb, dgs, dgr = _merge_bwd(gab, rb, hm, gs_off, dmerged, nct, tile, "merge_bwd")
    gw_glu = _mm(a_ssm, dgab, "tn", BF16, "glu_dw", tm=1024, tn=1024, tk=2176)
    gw_rp = _mm(ret_in, drb, "tn", BF16, "ret_proj_dw", tm=1024, tn=1024, tk=2176)
    dgab, drb, gw_glu, gw_rp = tie(dgab, drb, gw_glu, gw_rp)
    up_glu = big_update(ssm_glu_w, m_ssm_glu_w, v_ssm_glu_w, 0, gw_glu, 1, "glu")
    up_rp = big_update(ret_w_proj, m_ret_w_proj, v_ret_w_proj, 0, gw_rp, 0, "ret_proj")
    da_ssm = _mm(dgab, w_glu, "nt", F32, "glu_dx", tm=512, tn=ssm_w, tk=2 * d)
    dret_in = _mm(drb, w_rp, "nt", F32, "ret_proj_dx", tm=512, tn=v_w, tk=d)
    d_o, dg_gate = _ret_gate_bwd(o_dirs[0], o_dirs[1], hm, g_off, dret_in, heads, dv, nct, tile, "ret_gate_bwd")
    dy_ssm, dus_direct, d_dskip = _ssm_out_bwd(y_dirs[0], y_dirs[1], hm, dskip, da_ssm, nct, tile, "ssm_out_bwd")
    s5_table_grads, du_ctx, du_lat = [], [], [dus_direct]
    for dr in range(2):
        dyp = _to_scan_layout(jnp.zeros((n_ctx, ssm_w), F32), dy_ssm, dr == 1)
        if dr == 1:
            dyp, up_out2, up_in2 = tie(dyp, up_out2, up_in2)
        outs = _s5_bwd(ups[dr], dyp, *s5_tabs[dr], dr == 1, "s5_bwd%d" % dr)
        part_ctx, part_lat = _from_scan_layout(outs[0], n_ctx, dr == 1)
        du_ctx.append(part_ctx)
        du_lat.append(part_lat)
        s5_table_grads.append(outs[1:])
    dqkv, ret_table_grads = [], []
    for dr in range(2):
        if dr == 1:
            d_o, up_mo, up_glu, up_rp = tie(d_o, up_mo, up_glu, up_rp)
        outs = _ret_bwd(hm, cos, sin, *ret_tabs[dr], s_ins[dr], d_o, heads, dk, dv, q_off, ncc, dr == 1,
                        "ret_bwd%d" % dr)
        dqkv.append(outs[:3])
        ret_table_grads.append(outs[3:])
    both_dirs = lambda grads: tuple(jnp.stack([g0, g1]) for g0, g1 in zip(*grads))
    early_parts = list(s5_vjp(both_dirs(s5_table_grads))) + list(ret_vjp(both_dirs(ret_table_grads)))
    s5_names = 7
    early_shapes = [p.shape for p in early_parts]
    early_all = _all_gather(_pack(early_parts, 1024), 0, "ag_s5_grads", on_sequencer=True)
    early_sum = _sum_leading(early_all.reshape(N_DEV, -1, 1024), "sum_s5_grads")
    dus = jnp.concatenate([du_ctx[0] + du_ctx[1], du_lat[0] + du_lat[1] + du_lat[2]], axis=0)
    dhm = _assemble_dhm(dus, dqkv[0][0], dqkv[1][0], dqkv[0][1], dqkv[1][1], dqkv[0][2], dqkv[1][2],
                        dg_gate, dgs, dgr, n_ctx // wide_tile, wide_tile, "assemble_dhm")
    gw_mix = _mm(u2, dhm, "tn", BF16, "mix_in_dw", tm=1024, tn=1408, tk=2176)
    dhm, gw_mix = tie(dhm, gw_mix)
    up_mix = big_update(mix_w_in, m_mix_w_in, v_mix_w_in, 0, gw_mix, 1, "mix_in")
    du2 = _mm(dhm, w_mix, "nt", F32, "mix_in_dx", tm=544, tn=d, tk=1408)
    dx1, dg6[2], dsh, dsc = _ada_pre_bwd(x1, du2, dx2, g6, mods, 2, 1, nct, 2, tile, "pre2_bwd", dres_x_only=True)
    add_mod(both, 3, dsh)
    add_mod(both, 4, dsc)
    do1, dg6[1], dgt = _ada_post_bwd(dx1, o1, g6, mods, 1, 0, 0.5, nct, 2, tile, "post1_bwd")
    add_mod(both, 2, dgt)
    gw_out1 = _mm(a1, do1, "tn", BF16, "ffn1_out_dw", tm=1408, tn=1024, tk=2176)
    do1, gw_out1 = tie(do1, gw_out1)
    up_out1 = big_update(ffn_w_out[0], m_ffn_w_out[0], v_ffn_w_out[0], 0, gw_out1, 0, "ffn1_out", filled=up_out2)
    da1 = _mm(do1, w_out1, "nt", F32, "ffn1_out_dx", tm=544, tn=1408, tk=d)
    dh1 = _swiglu_bwd(h1, da1, wide_tile, "swiglu1_bwd")
    dh1, up_mix, early_sum = tie(dh1, up_mix, early_sum)
    early_sums = _unpack(early_sum, early_shapes)
    gw_in1 = _mm(u1, dh1, "tn", BF16, "ffn1_in_dw", tm=1024, tn=1408, tk=2176)
    dh1, gw_in1 = tie(dh1, gw_in1)
    up_in1 = big_update(ffn_w_in[0], m_ffn_w_in[0], v_ffn_w_in[0], 0, gw_in1, 1, "ffn1_in", filled=up_in2)
    du1 = _mm(dh1, w_in1, "nt", F32, "ffn1_in_dx", tm=544, tn=d, tk=1408)
    dxin, dg6[0], dsh, dsc = _ada_pre_bwd(xin, du1, dx1, g6, mods, 0, 0, nct, 2, tile, "pre1_bwd")
    add_mod(both, 0, dsh)
    add_mod(both, 1, dsc)
    grad_x = dxin[n_ctx:][None]

    zero_d = jnp.zeros((d,), F32)
    d_ada_x = jnp.stack([dmod.get((1, k), zero_d) for k in range(9)]).reshape(9 * d)
    d_ada_c = jnp.stack([dmod.get((0, k), zero_d) for k in range(9)]).reshape(9 * d)
    dg_full = jnp.stack([g[0, 0] for g in dg6])
    small_parts = [d_ada_x, d_ada_c, dg_full, d_dskip, loss_part]
    small_shapes = [p.shape for p in small_parts]
    packed = _pack(small_parts, 1024)
    gathered = _all_gather(packed, 0, "ag_small_grads").reshape(N_DEV, -1, 1024)
    summed = _sum_leading(gathered, "sum_small_grads")
    sums = _unpack(summed, small_shapes)
    sum_dx, sum_dc, sum_dg = sums[0], sums[1], sums[2]
    loss = sums[4][0]
    grad_ada_b = (sum_dx + sum_dc)[None]
    dx_rows = gathered.reshape(N_DEV, -1)[:, :9 * d]
    col0 = me * na
    da_rows = jnp.concatenate([lax.dynamic_slice_in_dim(dx_rows, col0, na, axis=1),
                               lax.dynamic_slice_in_dim(sum_dc[None], col0, na, axis=1),
                               jnp.zeros((2 * SUBLANE - N_DEV - 1, na), F32)], axis=0)
    grad_ada_w = _mm(sc, da_rows, "tn", F32, "ada_dw", tm=512, tn=na, tk=16)
    d_sc = _mm(da_rows, ada_w[0], "nt", F32, "ada_dx", tm=16, tn=512, tk=na)
    d_sc_all = _all_gather(jnp.broadcast_to(d_sc[N_DEV:N_DEV + 1], (SUBLANE, d)), 0, "ag_dctx")
    d_sc_sum = _sum_leading(d_sc_all.reshape(N_DEV, SUBLANE, d), "sum_dctx")
    grad_c_ctx = _silu_grad_rows(jnp.broadcast_to(c_ctx[None], (SUBLANE, d)), d_sc_sum, "ctx_silu_bwd")[0]
    grad_norm_g = lax.dynamic_slice_in_dim(sum_dg, me * ng_cols, ng_cols, axis=1)[None]

    upd = {}
    upd["ffn_w_in"] = [o[None] for o in up_in1]
    upd["ffn_w_out"] = [o[None] for o in up_out1]
    upd["mix_w_in"] = list(up_mix)
    upd["ssm_glu_w"] = list(up_glu)
    upd["ret_w_proj"] = list(up_rp)
    upd["mix_w_out"] = list(up_mo)
    upd["ada_w"] = [o[None] for o in _adamw(ada_w[0], m_ada_w[0], v_ada_w[0], grad_ada_w[None], "adamw_ada_w")]

    small_names = ["c_ctx", "ada_b", "norm_g", "ssm_lam_re", "ssm_lam_im", "ssm_log_step", "ssm_b_re", "ssm_b_im",
                   "ssm_c_re", "ssm_c_im", "ssm_d", "ret_decay_logit"]
    small_w = [c_ctx, ada_b, norm_g, ssm_lam_re, ssm_lam_im, ssm_log_step, ssm_b_re, ssm_b_im, ssm_c_re, ssm_c_im,
               ssm_d, ret_decay_logit]
    small_m = [m_c_ctx, m_ada_b, m_norm_g, m_ssm_lam_re, m_ssm_lam_im, m_ssm_log_step, m_ssm_b_re, m_ssm_b_im,
               m_ssm_c_re, m_ssm_c_im, m_ssm_d, m_ret_decay_logit]
    small_v = [v_c_ctx, v_ada_b, v_norm_g, v_ssm_lam_re, v_ssm_lam_im, v_ssm_log_step, v_ssm_b_re, v_ssm_b_im,
               v_ssm_c_re, v_ssm_c_im, v_ssm_d, v_ret_decay_logit]
    small_g = [grad_c_ctx, grad_ada_b, grad_norm_g] + [s[None] for s in early_sums[:s5_names]] + \
              [sums[3].reshape(ssm_d.shape), early_sums[s5_names][None]]
    shapes = [w.shape for w in small_w]
    res = _adamw(_pack(small_w, 1024), _pack(small_m, 1024), _pack(small_v, 1024), _pack(small_g, 1024)[None],
                 "adamw_small")
    small_out = [_unpack(o, shapes) for o in res]
    for i, nm in enumerate(small_names):
        upd[nm] = [small_out[kind][i] for kind in range(4)]

    order = ["c_ctx", "ada_w", "ada_b", "norm_g", "ffn_w_in", "ffn_w_out", "mix_w_in", "ssm_lam_re", "ssm_lam_im",
             "ssm_log_step", "ssm_b_re", "ssm_b_im", "ssm_c_re", "ssm_c_im", "ssm_d", "ssm_glu_w", "ret_decay_logit",
             "ret_w_proj", "mix_w_out"]
    outs = [loss, grad_x]
    for kind in range(4):
        outs += [upd[nm][kind] for nm in order]
    return tuple(outs)
```

```python
import functools
import math

import jax
import jax.numpy as jnp
import numpy as np
from jax import lax
from jax.experimental import pallas as pl
from jax.experimental.pallas import tpu as pltpu
from jax.experimental.pallas import tpu_sc as plsc

F32 = jnp.float32
BF16 = jnp.bfloat16
MXU_DTYPE = jnp.bfloat16
MESH_AXES = ("x", "y", "c")
N_DEV = 8
V7X_VMEM_LIMIT_BYTES = 56 * 1024 * 1024
LANE = 128
SUBLANE = 8

GRID_W = 64
RET_CHUNK = 128
ROPE_BASE = 10000.0
NORM_EPS = 1e-6
ADAM_LR = 0.001
ADAM_B1 = 0.9
ADAM_B2 = 0.999
ADAM_EPS = 1e-08
ADAM_WD = 0.01
ADAM_STEP = 10
SSM_TILE_GROUPS = 8
SSM_HALF_GROUPS = 4
N_SEG = 16


def _params(sem=None):
    return pltpu.CompilerParams(dimension_semantics=sem, vmem_limit_bytes=V7X_VMEM_LIMIT_BYTES)


def _tile(n, target, mult):
    best = None
    t = mult
    while t <= min(n, target):
        if n % t == 0:
            best = t
        t += mult
    return n if best is None else best


def _sds(shape, dtype):
    return jax.ShapeDtypeStruct(tuple(shape), dtype)


def _mm(a, b, dims, out_dtype, name, tm=512, tn=1408, tk=2048):
    if dims == "nn":
        (m, k), (k2, n) = a.shape, b.shape
    elif dims == "nt":
        (m, k), (n, k2) = a.shape, b.shape
    else:
        (k, m), (k2, n) = a.shape, b.shape
    assert k == k2, (a.shape, b.shape, dims)
    tm = _tile(m, tm, 16)
    tn = _tile(n, tn, LANE)
    tk = _tile(k, tk, LANE if dims != "tn" else 16)
    nk = k // tk
    dn = {"nn": (((1,), (0,)), ((), ())), "nt": (((1,), (1,)), ((), ())), "tn": (((0,), (0,)), ((), ()))}[dims]

    def product(a_ref, b_ref):
        return lax.dot_general(a_ref[...].astype(MXU_DTYPE), b_ref[...].astype(MXU_DTYPE), dn,
                               preferred_element_type=F32)

    def body_single(a_ref, b_ref, o_ref):
        o_ref[...] = product(a_ref, b_ref).astype(o_ref.dtype)

    def body(a_ref, b_ref, o_ref, acc_ref):
        kk = pl.program_id(2)

        @pl.when(kk == 0)
        def _():
            acc_ref[...] = product(a_ref, b_ref)

        @pl.when((kk > 0) & (kk < nk - 1))
        def _():
            acc_ref[...] += product(a_ref, b_ref)

        @pl.when(kk == nk - 1)
        def _():
            o_ref[...] = (acc_ref[...] + product(a_ref, b_ref)).astype(o_ref.dtype)

    if dims == "nn":
        a_spec = pl.BlockSpec((tm, tk), lambda j, i, kk: (i, kk))
        b_spec = pl.BlockSpec((tk, tn), lambda j, i, kk: (kk, j))
    elif dims == "nt":
        a_spec = pl.BlockSpec((tm, tk), lambda j, i, kk: (i, kk))
        b_spec = pl.BlockSpec((tn, tk), lambda j, i, kk: (j, kk))
    else:
        a_spec = pl.BlockSpec((tk, tm), lambda j, i, kk: (kk, i))
        b_spec = pl.BlockSpec((tk, tn), lambda j, i, kk: (kk, j))
    return pl.pallas_call(
        body_single if nk == 1 else body, name=name, grid=(n // tn, m // tm, nk), in_specs=[a_spec, b_spec],
        out_specs=pl.BlockSpec((tm, tn), lambda j, i, kk: (i, j)), out_shape=_sds((m, n), out_dtype),
        scratch_shapes=[] if nk == 1 else [pltpu.VMEM((tm, tn), F32)],
        compiler_params=_params(("parallel", "parallel", "arbitrary")))(a, b)


def _rows(name, body, n_tiles, ins, outs):
    in_specs = [pl.BlockSpec(blk, imap) for (_, blk, imap) in ins]
    out_specs = [pl.BlockSpec(blk, imap) for (_, _, blk, imap) in outs]
    out_shape = [_sds(shape, dt) for (shape, dt, _, _) in outs]
    res = pl.pallas_call(body, name=name, grid=(n_tiles,), in_specs=in_specs, out_specs=out_specs,
                         out_shape=out_shape, compiler_params=_params(("arbitrary",)))(*[a for (a, _, _) in ins])
    return res


def _row_in(arr, tile, width=None, col=0, x_only_offset=None):
    width = arr.shape[1] if width is None else width
    if x_only_offset is None:
        return (arr, (tile, width), lambda i: (i, col))
    return (arr, (tile, width), lambda i: (jnp.maximum(i - x_only_offset, 0), col))


def _vec_in(arr, idx_fn):
    return (arr, (1, 1, arr.shape[2]), lambda i: (idx_fn(i), 0, 0))


def _rms(h):
    return lax.rsqrt(jnp.mean(h * h, axis=-1, keepdims=True) + NORM_EPS)


def _sigmoid(z):
    return 1.0 / (1.0 + jnp.exp(-z))


def _ada_pre_fwd(h, g6, mods, gi, mi, nct, tile, name):
    r, d = h.shape
    sel = lambda i: jnp.where(i >= nct, 1, 0)

    def body(h_ref, g_ref, sh_ref, sc_ref, u_ref):
        hh = h_ref[...]
        n = hh * _rms(hh) * g_ref[0]
        u_ref[...] = (n * (1.0 + sc_ref[0]) + sh_ref[0]).astype(u_ref.dtype)

    (u,) = _rows(name, body, r // tile,
                 [_row_in(h, tile), _vec_in(g6, lambda i: gi), _vec_in(mods, lambda i: sel(i) * 9 + 3 * mi),
                  _vec_in(mods, lambda i: sel(i) * 9 + 3 * mi + 1)],
                 [((r, d), BF16, (tile, d), lambda i: (i, 0))])
    return u


def _ada_pre_bwd(h, du, dres, g6, mods, gi, mi, nct, nsel, tile, name, dres_x_only=False):
    r, d = h.shape
    sel = lambda i: jnp.where(i >= nct, 1, 0) if nsel == 2 else 0
    msel = lambda i: jnp.where(i >= nct, 1, 0)
    off = nct if dres_x_only else None

    def body(h_ref, du_ref, dr_ref, g_ref, sc_ref, dh_ref, dg_ref, dsh_ref, dsc_ref):
        i = pl.program_id(0)
        hh = h_ref[...]
        rr = _rms(hh)
        g = g_ref[0]
        hn = hh * rr
        n = hn * g
        du_ = du_ref[...].astype(F32)
        dn = du_ * (1.0 + sc_ref[0])

        @pl.when(i == 0)
        def _():
            dg_ref[...] = jnp.zeros_like(dg_ref)

        @pl.when((i == 0) | (i == nct))
        def _():
            dsh_ref[...] = jnp.zeros_like(dsh_ref)
            dsc_ref[...] = jnp.zeros_like(dsc_ref)

        dg_ref[0] += jnp.sum(dn * hn, axis=0, keepdims=True)
        dsh_ref[0] += jnp.sum(du_, axis=0, keepdims=True)
        dsc_ref[0] += jnp.sum(du_ * n, axis=0, keepdims=True)
        t = dn * g
        dh = rr * t - hn * (rr * jnp.mean(t * hn, axis=-1, keepdims=True))
        if dres_x_only:
            dh_ref[...] = dh + jnp.where(i >= nct, dr_ref[...], 0.0)
        else:
            dh_ref[...] = dh + dr_ref[...]

    dh, dg, dsh, dsc = _rows(
        name, body, r // tile,
        [_row_in(h, tile), _row_in(du, tile), _row_in(dres, tile, x_only_offset=off), _vec_in(g6, lambda i: gi),
         _vec_in(mods, lambda i: msel(i) * 9 + 3 * mi + 1)],
        [((r, d), F32, (tile, d), lambda i: (i, 0)), ((1, 1, d), F32, (1, 1, d), lambda i: (0, 0, 0)),
         ((nsel, 1, d), F32, (1, 1, d), lambda i: (sel(i), 0, 0)),
         ((nsel, 1, d), F32, (1, 1, d), lambda i: (sel(i), 0, 0))])
    return dh, dg, dsh, dsc


def _ada_post_fwd(h, o, g6, mods, gi, mi, res_w, nct, tile, name, h_x_only=False):
    r, d = o.shape
    sel = lambda i: jnp.where(i >= nct, 1, 0)

    def body(h_ref, o_ref, g_ref, gt_ref, y_ref):
        oo = o_ref[...]
        n = oo * _rms(oo) * g_ref[0]
        y_ref[...] = h_ref[...] + res_w * gt_ref[0] * n

    (y,) = _rows(name, body, r // tile,
                 [_row_in(h, tile), _row_in(o, tile), _vec_in(g6, lambda i: gi),
                  _vec_in(mods, lambda i: sel(i) * 9 + 3 * mi + 2)],
                 [((r, d), F32, (tile, d), lambda i: (i, 0))])
    return y


def _ada_post_bwd(dy, o, g6, mods, gi, mi, res_w, nct, nsel, tile, name):
    r, d = o.shape
    sel = lambda i: jnp.where(i >= nct, 1, 0) if nsel == 2 else 0
    msel = lambda i: jnp.where(i >= nct, 1, 0)

    def body(dy_ref, o_ref, g_ref, gt_ref, do_ref, dg_ref, dgt_ref):
        i = pl.program_id(0)
        oo = o_ref[...]
        rr = _rms(oo)
        g = g_ref[0]
        on = oo * rr
        dy_ = dy_ref[...] * res_w

        @pl.when(i == 0)
        def _():
            dg_ref[...] = jnp.zeros_like(dg_ref)

        @pl.when((i == 0) | (i == nct))
        def _():
            dgt_ref[...] = jnp.zeros_like(dgt_ref)

        dgt_ref[0] += jnp.sum(dy_ * (on * g), axis=0, keepdims=True)
        dn = dy_ * gt_ref[0]
        dg_ref[0] += jnp.sum(dn * on, axis=0, keepdims=True)
        t = dn * g
        do_ref[...] = (rr * t - on * (rr * jnp.mean(t * on, axis=-1, keepdims=True))).astype(do_ref.dtype)

    do, dg, dgt = _rows(
        name, body, r // tile,
        [_row_in(dy, tile), _row_in(o, tile), _vec_in(g6, lambda i: gi),
         _vec_in(mods, lambda i: msel(i) * 9 + 3 * mi + 2)],
        [((r, d), BF16, (tile, d), lambda i: (i, 0)), ((1, 1, d), F32, (1, 1, d), lambda i: (0, 0, 0)),
         ((nsel, 1, d), F32, (1, 1, d), lambda i: (sel(i), 0, 0))])
    return do, dg, dgt


def _swiglu_fwd(h, tile, name):
    r, w2 = h.shape
    f = w2 // 2

    def body(h_ref, a_ref):
        gt = h_ref[:, :f].astype(F32)
        up = h_ref[:, f:].astype(F32)
        a_ref[...] = (gt * _sigmoid(gt) * up).astype(a_ref.dtype)

    (a,) = _rows(name, body, r // tile, [_row_in(h, tile)], [((r, f), BF16, (tile, f), lambda i: (i, 0))])
    return a


def _swiglu_bwd(h, da, tile, name):
    r, w2 = h.shape
    f = w2 // 2

    def body(h_ref, da_ref, dh_ref):
        gt = h_ref[:, :f].astype(F32)
        up = h_ref[:, f:].astype(F32)
        d = da_ref[...]
        sg = _sigmoid(gt)
        dh_ref[:, :f] = (d * up * (sg * (1.0 + gt * (1.0 - sg)))).astype(dh_ref.dtype)
        dh_ref[:, f:] = (d * gt * sg).astype(dh_ref.dtype)

    (dh,) = _rows(name, body, r // tile, [_row_in(h, tile), _row_in(da, tile)],
                  [((r, w2), BF16, (tile, w2), lambda i: (i, 0))])
    return dh


def _gelu_parts(y):
    c0 = math.sqrt(2.0 / math.pi)
    inner = c0 * (y + 0.044715 * y * y * y)
    th = jnp.tanh(inner)
    return th, c0 * (1.0 + 3 * 0.044715 * y * y)


def _ssm_out_fwd(y0, y1, hm, dskip, nct, tile, name):
    t_rows, s = y0.shape

    def body(y0_ref, y1_ref, u_ref, d_ref, a_ref):
        y = y0_ref[...] + y1_ref[...] + d_ref[0] * u_ref[...]
        th, _ = _gelu_parts(y)
        a_ref[...] = (0.5 * y * (1.0 + th)).astype(a_ref.dtype)

    (a,) = _rows(name, body, t_rows // tile,
                 [_row_in(y0, tile), _row_in(y1, tile), (hm, (tile, s), lambda i: (i + nct, 0)),
                  _vec_in(dskip, lambda i: 0)],
                 [((t_rows, s), BF16, (tile, s), lambda i: (i, 0))])
    return a


def _ssm_out_bwd(y0, y1, hm, dskip, da, nct, tile, name):
    t_rows, s = y0.shape

    def body(y0_ref, y1_ref, u_ref, d_ref, da_ref, dy_ref, du_ref, dd_ref):
        i = pl.program_id(0)
        u = u_ref[...]
        y = y0_ref[...] + y1_ref[...] + d_ref[0] * u
        th, dinner = _gelu_parts(y)
        dy = da_ref[...] * (0.5 * (1.0 + th) + 0.5 * y * (1.0 - th * th) * dinner)
        dy_ref[...] = dy
        du_ref[...] = dy * d_ref[0]

        @pl.when(i == 0)
        def _():
            dd_ref[...] = jnp.zeros_like(dd_ref)

        dd_ref[0] += jnp.sum(dy * u, axis=0, keepdims=True)

    dy, du, dd = _rows(name, body, t_rows // tile,
                       [_row_in(y0, tile), _row_in(y1, tile), (hm, (tile, s), lambda i: (i + nct, 0)),
                        _vec_in(dskip, lambda i: 0), _row_in(da, tile)],
                       [((t_rows, s), F32, (tile, s), lambda i: (i, 0)), ((t_rows, s), F32, (tile, s), lambda i: (i, 0)),
                        ((1, 1, s), F32, (1, 1, s), lambda i: (0, 0, 0))])
    return dy, du, dd


def _col_pieces(arr, off, width, tile, nct, unit=None):
    pw = math.gcd(off, width if unit is None else unit)
    specs = [(arr, (tile, pw), functools.partial(lambda i, cb: (i + nct, cb), cb=off // pw + p))
             for p in range(width // pw)]
    return specs, pw


def _ret_gate_fwd(o0, o1, hm, g_off, heads, dv, nct, tile, name):
    t_rows, w = o0.shape
    g_specs, pw = _col_pieces(hm, g_off, w, tile, nct)
    ng = len(g_specs)

    def body(o0_ref, o1_ref, *refs):
        g_refs, r_ref = refs[:ng], refs[ng]
        for hd in range(heads):
            cs = slice(hd * dv, (hd + 1) * dv)
            o = o0_ref[:, cs] + o1_ref[:, cs]
            lo = (hd * dv) % pw
            g = g_refs[(hd * dv) // pw][:, lo:lo + dv]
            r_ref[:, cs] = (g * _sigmoid(g) * (o * _rms(o))).astype(r_ref.dtype)

    (ri,) = _rows(name, body, t_rows // tile, [_row_in(o0, tile), _row_in(o1, tile)] + g_specs,
                  [((t_rows, w), BF16, (tile, w), lambda i: (i, 0))])
    return ri


def _ret_gate_bwd(o0, o1, hm, g_off, dri, heads, dv, nct, tile, name):
    t_rows, w = o0.shape
    g_specs, pw = _col_pieces(hm, g_off, w, tile, nct)
    ng = len(g_specs)

    def body(o0_ref, o1_ref, d_ref, *refs):
        g_refs, do_ref, dg_ref = refs[:ng], refs[ng], refs[ng + 1]
        for hd in range(heads):
            cs = slice(hd * dv, (hd + 1) * dv)
            o = o0_ref[:, cs] + o1_ref[:, cs]
            lo = (hd * dv) % pw
            g = g_refs[(hd * dv) // pw][:, lo:lo + dv]
            d = d_ref[:, cs]
            rr = _rms(o)
            on = o * rr
            sg = _sigmoid(g)
            dg_ref[:, cs] = d * on * (sg * (1.0 + g * (1.0 - sg)))
            t = d * (g * sg)
            do_ref[:, cs] = rr * t - on * (rr * jnp.mean(t * on, axis=-1, keepdims=True))

    do, dg = _rows(name, body, t_rows // tile, [_row_in(o0, tile), _row_in(o1, tile), _row_in(dri, tile)] + g_specs,
                   [((t_rows, w), F32, (tile, w), lambda i: (i, 0)), ((t_rows, w), F32, (tile, w), lambda i: (i, 0))])
    return do, dg


def _merge_fwd(gab, rb, hm, gs_off, nct, tile, name):
    t_rows, d = rb.shape
    specs, pw = _col_pieces(hm, gs_off, 2 * d, tile, nct, unit=d)
    npc = d // pw

    def body(gab_ref, rb_ref, *refs):
        gs_refs, gr_refs, m_ref = refs[:npc], refs[npc:2 * npc], refs[2 * npc]
        for p in range(npc):
            cs = slice(p * pw, (p + 1) * pw)
            ga = gab_ref[:, cs]
            gb = gab_ref[:, d + p * pw:d + (p + 1) * pw]
            m_ref[:, cs] = (_sigmoid(gs_refs[p][...]) * (ga * _sigmoid(gb))
                            + _sigmoid(gr_refs[p][...]) * rb_ref[:, cs]).astype(m_ref.dtype)

    (mg,) = _rows(name, body, t_rows // tile, [_row_in(gab, tile), _row_in(rb, tile)] + specs,
                  [((t_rows, d), BF16, (tile, d), lambda i: (i, 0))])
    return mg


def _merge_bwd(gab, rb, hm, gs_off, dm, nct, tile, name):
    t_rows, d = rb.shape
    specs, pw = _col_pieces(hm, gs_off, 2 * d, tile, nct, unit=d)
    npc = d // pw

    def body(gab_ref, rb_ref, dm_ref, *refs):
        gs_refs, gr_refs = refs[:npc], refs[npc:2 * npc]
        dgab_ref, drb_ref, dgs_ref, dgr_ref = refs[2 * npc:]
        for p in range(npc):
            cs = slice(p * pw, (p + 1) * pw)
            cs2 = slice(d + p * pw, d + (p + 1) * pw)
            ga = gab_ref[:, cs]
            gb = gab_ref[:, cs2]
            dmm = dm_ref[:, cs]
            ss = _sigmoid(gs_refs[p][...])
            sr = _sigmoid(gr_refs[p][...])
            sb = _sigmoid(gb)
            dbr = dmm * ss
            dgab_ref[:, cs] = (dbr * sb).astype(dgab_ref.dtype)
            dgab_ref[:, cs2] = (dbr * ga * sb * (1.0 - sb)).astype(dgab_ref.dtype)
            drb_ref[:, cs] = (dmm * sr).astype(drb_ref.dtype)
            dgs_ref[:, cs] = dmm * (ga * sb) * ss * (1.0 - ss)
            dgr_ref[:, cs] = dmm * rb_ref[:, cs] * sr * (1.0 - sr)

    return _rows(name, body, t_rows // tile, [_row_in(gab, tile), _row_in(rb, tile), _row_in(dm, tile)] + specs,
                 [((t_rows, 2 * d), BF16, (tile, 2 * d), lambda i: (i, 0)), ((t_rows, d), BF16, (tile, d), lambda i: (i, 0)),
                  ((t_rows, d), F32, (tile, d), lambda i: (i, 0)), ((t_rows, d), F32, (tile, d), lambda i: (i, 0))])


def _assemble_dhm(dus, dq0, dq1, dk0, dk1, dv0, dv1, dg, dgs, dgr, nct, tile, name):
    r, s = dus.shape
    qk = dq0.shape[1]
    vw = dv0.shape[1]
    d = dgs.shape[1]
    mi = s + 2 * qk + 2 * vw + 2 * d
    c_q, c_k, c_v, c_g, c_gs, c_gr = s, s + qk, s + 2 * qk, s + 2 * qk + vw, s + 2 * qk + 2 * vw, s + 2 * qk + 2 * vw + d

    def body(dus_ref, dq0_ref, dq1_ref, dk0_ref, dk1_ref, dv0_ref, dv1_ref, dg_ref, dgs_ref, dgr_ref, o_ref):
        i = pl.program_id(0)
        lat = i >= nct
        o_ref[:, :s] = dus_ref[...].astype(o_ref.dtype)
        o_ref[:, c_q:c_k] = (dq0_ref[...] + dq1_ref[...]).astype(o_ref.dtype)
        o_ref[:, c_k:c_v] = (dk0_ref[...] + dk1_ref[...]).astype(o_ref.dtype)
        o_ref[:, c_v:c_g] = (dv0_ref[...] + dv1_ref[...]).astype(o_ref.dtype)
        o_ref[:, c_g:c_gs] = jnp.where(lat, dg_ref[...], 0.0).astype(o_ref.dtype)
        o_ref[:, c_gs:c_gr] = jnp.where(lat, dgs_ref[...], 0.0).astype(o_ref.dtype)
        o_ref[:, c_gr:] = jnp.where(lat, dgr_ref[...], 0.0).astype(o_ref.dtype)

    (out,) = _rows(name, body, r // tile,
                   [_row_in(dus, tile), _row_in(dq0, tile), _row_in(dq1, tile), _row_in(dk0, tile), _row_in(dk1, tile),
                    _row_in(dv0, tile), _row_in(dv1, tile), _row_in(dg, tile, x_only_offset=nct),
                    _row_in(dgs, tile, x_only_offset=nct), _row_in(dgr, tile, x_only_offset=nct)],
                   [((r, mi), BF16, (tile, mi), lambda i: (i, 0))])
    return out


def _loss_grad(y, target, tile, name):
    t_rows, d = y.shape

    def body(y_ref, t_ref, dy_ref, l_ref):
        i = pl.program_id(0)
        e = y_ref[...] - t_ref[...]
        dy_ref[...] = e * (1.0 / d)

        @pl.when(i == 0)
        def _():
            l_ref[...] = jnp.zeros_like(l_ref)

        l_ref[0] += jnp.sum(e * e, axis=0, keepdims=True)

    return _rows(name, body, t_rows // tile, [_row_in(y, tile), _row_in(target, tile)],
                 [((t_rows, d), F32, (tile, d), lambda i: (i, 0)), ((1, 1, d), F32, (1, 1, d), lambda i: (0, 0, 0))])


def _silu_rows(v, name):
    def body(v_ref, o_ref):
        z = v_ref[...]
        o_ref[...] = z * _sigmoid(z)

    (o,) = _rows(name, body, 1, [_row_in(v, v.shape[0])], [(v.shape, F32, v.shape, lambda i: (0, 0))])
    return o


def _silu_grad_rows(v, dv, name):
    def body(v_ref, d_ref, o_ref):
        z = v_ref[...]
        sg = _sigmoid(z)
        o_ref[...] = d_ref[...] * (sg * (1.0 + z * (1.0 - sg)))

    (o,) = _rows(name, body, 1, [_row_in(v, v.shape[0]), _row_in(dv, v.shape[0])],
                 [(v.shape, F32, v.shape, lambda i: (0, 0))])
    return o


def _sum_leading(g8, name):
    n, r, c = g8.shape
    tile = _tile(r, 256, SUBLANE)

    def body(g_ref, o_ref):
        acc = g_ref[0]
        for j in range(1, n):
            acc = acc + g_ref[j]
        o_ref[...] = acc

    (o,) = _rows(name, body, r // tile, [(g8, (n, tile, c), lambda i: (0, i, 0))],
                 [((r, c), F32, (tile, c), lambda i: (i, 0))])
    return o


def _pair_sum(g, recv, axis, name):
    n, br, bc = recv.shape
    tile = _tile(br, 256, 16)
    nrt = br // tile
    core = lax.axis_index("c").astype(jnp.int32).reshape(1)

    def body(c_ref, g_ref, r_ref, o_ref):
        o_ref[0] = (g_ref[...].astype(F32) + r_ref[0].astype(F32)).astype(o_ref.dtype)

    if axis == 1:
        g_spec = pl.BlockSpec((tile, bc), lambda q, i, c_ref: (i, 2 * q + c_ref[0]))
    else:
        g_spec = pl.BlockSpec((tile, bc), lambda q, i, c_ref: ((2 * q + c_ref[0]) * nrt + i, 0))
    slot = pl.BlockSpec((1, tile, bc), lambda q, i, c_ref: (q, i, 0))
    return pl.pallas_call(
        body, name=name, out_shape=_sds((n, br, bc), recv.dtype),
        grid_spec=pltpu.PrefetchScalarGridSpec(num_scalar_prefetch=1, grid=(n, nrt), in_specs=[g_spec, slot],
                                               out_specs=slot),
        compiler_params=_params(("arbitrary", "arbitrary")))(core, g, recv)


def _adam_math(w, m, v, g):
    c1 = 1.0 / (1.0 - ADAM_B1 ** ADAM_STEP)
    c2 = 1.0 / (1.0 - ADAM_B2 ** ADAM_STEP)
    mm = ADAM_B1 * m + (1.0 - ADAM_B1) * g
    vv = ADAM_B2 * v + (1.0 - ADAM_B2) * (g * g)
    return -ADAM_LR * ((mm * c1) / (jnp.sqrt(vv * c2) + ADAM_EPS) + ADAM_WD * w), mm, vv


def _adamw(w, m, v, gparts, name):
    r, c = w.shape
    n = gparts.shape[0]
    tile = _tile(r, 256, 16)

    def body(w_ref, m_ref, v_ref, g_ref, go_ref, d_ref, mo_ref, vo_ref):
        g = g_ref[0].astype(F32)
        for j in range(1, n):
            g = g + g_ref[j].astype(F32)
        go_ref[...] = g
        d_ref[...], mo_ref[...], vo_ref[...] = _adam_math(w_ref[...], m_ref[...], v_ref[...], g)

    rs = lambda arr: _row_in(arr, tile)
    out = ((r, c), F32, (tile, c), lambda i: (i, 0))
    return _rows(name, body, r // tile, [rs(w), rs(m), rs(v), (gparts, (n, tile, c), lambda i: (0, i, 0))],
                 [out, out, out, out])


def _adamw_scattered(w, m, v, layer, p, recv, name, filled=None):
    nl, r, c = w.shape
    n = recv.shape[0]
    tile = _tile(r, 256, 16)
    chip = (2 * lax.axis_index("x") + lax.axis_index("y")).astype(jnp.int32).reshape(1)
    n_prev = 0 if filled is None else len(filled)

    def body(q_ref, w_ref, m_ref, v_ref, p_ref, g_ref, *rest):
        go_ref, d_ref, mo_ref, vo_ref = rest[n_prev:]
        g = p_ref[0].astype(F32)
        for j in range(n):
            g = g + g_ref[j].astype(F32)
        go_ref[0] = g
        d_ref[0], mo_ref[0], vo_ref[0] = _adam_math(w_ref[0], m_ref[0], v_ref[0], g)

    slab = pl.BlockSpec((1, tile, c), lambda i, q_ref: (layer, i, 0))
    anywhere = pl.BlockSpec(memory_space=pl.ANY)
    out = _sds((nl, r, c), F32)
    prev = [] if filled is None else list(filled)
    return pl.pallas_call(
        body, name=name, out_shape=[out, out, out, out],
        grid_spec=pltpu.PrefetchScalarGridSpec(
            num_scalar_prefetch=1, grid=(r // tile,),
            in_specs=[slab, slab, slab, pl.BlockSpec((1, tile, c), lambda i, q_ref: (q_ref[0], i, 0)),
                      pl.BlockSpec((n, tile, c), lambda i, q_ref: (0, i, 0))] + [anywhere] * n_prev,
            out_specs=[slab, slab, slab, slab]),
        input_output_aliases={6 + j: j for j in range(n_prev)},
        compiler_params=_params(("arbitrary",)))(chip, w, m, v, p, recv, *prev)


def _cmul(ar, ai, br, bi):
    return ar * br - ai * bi, ar * bi + ai * br


def _cpow(ar, ai, n):
    pr, pi = jnp.ones_like(ar), jnp.zeros_like(ar)
    br, bi = ar, ai
    while n:
        if n & 1:
            pr, pi = _cmul(pr, pi, br, bi)
        n >>= 1
        if n:
            br, bi = _cmul(br, bi, br, bi)
    return pr, pi


def _s5_scan_into(x_ref, ar1, ai1, ns, fin_ref, hin_ref, reverse):
    st = ar1.shape[1]
    ar = jnp.broadcast_to(ar1, (N_SEG, st))
    ai = jnp.broadcast_to(ai1, (N_SEG, st))
    zero = jnp.zeros((N_SEG, st), F32)

    def slab(k):
        return pl.ds(pl.multiple_of(k * N_SEG, N_SEG), N_SEG)

    def pass1(j, carry):
        hr, hi = carry
        k = ns - 1 - j if reverse else j
        nr, ni = _cmul(ar, ai, hr, hi)
        return nr + x_ref[slab(k), :st], ni + x_ref[slab(k), st:]

    fr, fi = lax.fori_loop(0, ns, pass1, (zero, zero))
    fin_ref[:, :st] = fr
    fin_ref[:, st:] = fi
    pr, pi = _cpow(ar1, ai1, ns)
    order = list(range(N_SEG - 1, -1, -1)) if reverse else list(range(N_SEG))
    hin_ref[order[0]:order[0] + 1, :] = jnp.zeros((1, 2 * st), F32)
    for a_, b_ in zip(order[:-1], order[1:]):
        cr, ci = _cmul(pr, pi, hin_ref[a_:a_ + 1, :st], hin_ref[a_:a_ + 1, st:])
        hin_ref[b_:b_ + 1, :st] = cr + fin_ref[a_:a_ + 1, :st]
        hin_ref[b_:b_ + 1, st:] = ci + fin_ref[a_:a_ + 1, st:]

    def pass2(j, carry):
        hr, hi = carry
        k = ns - 1 - j if reverse else j
        nr, ni = _cmul(ar, ai, hr, hi)
        nr = nr + x_ref[slab(k), :st]
        ni = ni + x_ref[slab(k), st:]
        x_ref[slab(k), :st] = nr
        x_ref[slab(k), st:] = ni
        return nr, ni

    lax.fori_loop(0, ns, pass2, (hin_ref[:, :st], hin_ref[:, st:]))


def _s5_specs(r, ch, st):
    u_spec = pl.BlockSpec((r, ch), lambda j: (0, j // 2))
    w_spec = pl.BlockSpec((1, ch, 2 * st), lambda j: (j, 0, 0))
    c_spec = pl.BlockSpec((1, 2 * st, ch), lambda j: (j, 0, 0))
    a_spec = pl.BlockSpec((1, 2, st), lambda j: (j, 0, 0))
    return u_spec, w_spec, c_spec, a_spec


def _s5_fwd(up, w, c, a, rev, name):
    r, s = up.shape
    nh, ch, st2 = w.shape
    st = st2 // 2
    ns = r // N_SEG
    nb = r // N_DEV
    u_spec, w_spec, c_spec, a_spec = _s5_specs(r, ch, st)

    def body(u_ref, w_ref, c_ref, a_ref, y_ref, x, fin, hin):
        j = pl.program_id(0)
        w_b = w_ref[0].astype(MXU_DTYPE)
        c_b = c_ref[0].astype(MXU_DTYPE)
        for rb in range(N_DEV):
            rows = slice(rb * nb, (rb + 1) * nb)
            x[rows, :] = jnp.dot(u_ref[rows, :].astype(MXU_DTYPE), w_b, preferred_element_type=F32)
        _s5_scan_into(x, a_ref[0, 0:1, :], a_ref[0, 1:2, :], ns, fin, hin, rev)
        for rb in range(N_DEV):
            rows = slice(rb * nb, (rb + 1) * nb)
            yb = jnp.dot(x[rows, :].astype(MXU_DTYPE), c_b, preferred_element_type=F32)

            @pl.when(j % 2 == 0)
            def _():
                y_ref[rows, :] = yb

            @pl.when(j % 2 == 1)
            def _():
                y_ref[rows, :] += yb

    small = pltpu.VMEM((N_SEG, st2), F32)
    return pl.pallas_call(
        body, name=name, grid=(nh,), in_specs=[u_spec, w_spec, c_spec, a_spec],
        out_specs=pl.BlockSpec((r, ch), lambda j: (0, j // 2)), out_shape=_sds((r, s), F32),
        scratch_shapes=[pltpu.VMEM((r, st2), F32), small, small],
        compiler_params=_params(("arbitrary",)))(up, w, c, a)


def _s5_bwd(up, dyp, w, c, a, rev, name):
    r, s = up.shape
    nh, ch, st2 = w.shape
    st = st2 // 2
    ns = r // N_SEG
    nb = r // N_DEV
    u_spec, w_spec, c_spec, a_spec = _s5_specs(r, ch, st)
    nt = (((1,), (1,)), ((), ()))
    tn = (((0,), (0,)), ((), ()))

    def body(u_ref, dy_ref, w_ref, c_ref, a_ref, du_ref, dw_ref, dc_ref, da_ref, h, g, fin, sin_, ein):
        j = pl.program_id(0)
        w_b = w_ref[0].astype(MXU_DTYPE)
        c_b = c_ref[0].astype(MXU_DTYPE)
        for rb in range(N_DEV):
            rows = slice(rb * nb, (rb + 1) * nb)
            h[rows, :] = jnp.dot(u_ref[rows, :].astype(MXU_DTYPE), w_b, preferred_element_type=F32)
        ar1, ai1 = a_ref[0, 0:1, :], a_ref[0, 1:2, :]
        _s5_scan_into(h, ar1, ai1, ns, fin, sin_, rev)
        dc = jnp.zeros((st2, ch), F32)
        for rb in range(N_DEV):
            rows = slice(rb * nb, (rb + 1) * nb)
            dyb = dy_ref[rows, :].astype(MXU_DTYPE)
            g[rows, :] = lax.dot_general(dyb, c_b, nt, preferred_element_type=F32)
            dc += lax.dot_general(h[rows, :].astype(MXU_DTYPE), dyb, tn, preferred_element_type=F32)
        dc_ref[0] = dc
        _s5_scan_into(g, ar1, -ai1, ns, fin, ein, not rev)

        def slab(k):
            return pl.ds(pl.multiple_of(k * N_SEG, N_SEG), N_SEG)

        step_back = 1 if rev else -1

        def acc_step(k, carry):
            acr, aci = carry
            g_r, g_i = g[slab(k), :st], g[slab(k), st:]
            p_r, p_i = h[slab(k + step_back), :st], h[slab(k + step_back), st:]
            return acr + g_r * p_r + g_i * p_i, aci + g_i * p_r - g_r * p_i

        edge = (ns - 1) * N_SEG if rev else 0
        g_r, g_i = g[edge:edge + N_SEG, :st], g[edge:edge + N_SEG, st:]
        p_r, p_i = sin_[:, :st], sin_[:, st:]
        lo, hi_k = (0, ns - 1) if rev else (1, ns)
        acr, aci = lax.fori_loop(lo, hi_k, acc_step, (g_r * p_r + g_i * p_i, g_i * p_r - g_r * p_i))
        da_ref[0, 0:1, :] = jnp.sum(acr, axis=0, keepdims=True)
        da_ref[0, 1:2, :] = jnp.sum(aci, axis=0, keepdims=True)
        dw = jnp.zeros((ch, st2), F32)
        for rb in range(N_DEV):
            rows = slice(rb * nb, (rb + 1) * nb)
            gb = g[rows, :].astype(MXU_DTYPE)
            dub = lax.dot_general(gb, w_b, nt, preferred_element_type=F32)
            dw += lax.dot_general(u_ref[rows, :].astype(MXU_DTYPE), gb, tn, preferred_element_type=F32)

            @pl.when(j % 2 == 0)
            def _():
                du_ref[rows, :] = dub

            @pl.when(j % 2 == 1)
            def _():
                du_ref[rows, :] += dub

        dw_ref[0] = dw

    small = pltpu.VMEM((N_SEG, st2), F32)
    big = pltpu.VMEM((r, st2), F32)
    return pl.pallas_call(
        body, name=name, grid=(nh,), in_specs=[u_spec, u_spec, w_spec, c_spec, a_spec],
        out_specs=[pl.BlockSpec((r, ch), lambda j: (0, j // 2)), w_spec, c_spec, a_spec],
        out_shape=[_sds((r, s), F32), _sds(w.shape, F32), _sds(c.shape, F32), _sds(a.shape, F32)],
        scratch_shapes=[big, big, small, small, small],
        compiler_params=_params(("arbitrary",)))(up, dyp, w, c, a)


def _rope(t, cos, sin):
    quarter = t.shape[1] // 4
    lane = lax.broadcasted_iota(jnp.int32, t.shape, 1)
    first = (lane // quarter) % 2 == 0
    partner = jnp.where(first, pltpu.roll(t, t.shape[1] - quarter, 1), pltpu.roll(t, quarter, 1))
    return t * cos + partner * sin


def _rope_t(d, cos, sin):
    quarter = d.shape[1] // 4
    ds_ = d * sin
    lane = lax.broadcasted_iota(jnp.int32, d.shape, 1)
    first = (lane // quarter) % 2 == 0
    partner = jnp.where(first, pltpu.roll(ds_, d.shape[1] - quarter, 1), pltpu.roll(ds_, quarter, 1))
    return d * cos + partner


def _chunk_of_step(s, nch, ncc, rev):
    if not rev:
        return s
    return jnp.where(s < ncc, ncc - 1 - s, nch + ncc - 1 - s)


def _heads_per_step(heads, dk, dv, q_off):
    for hpg in range(heads, 0, -1):
        if (heads % hpg == 0 and q_off % (hpg * dk) == 0 and (heads * dk) % (hpg * dk) == 0
                and (q_off + 2 * heads * dk) % (hpg * dv) == 0):
            return hpg
    return 1


def _ret_fwd(hm, cos, sin, decay, wend, win, gch, heads, dk, dv, q_off, ncc, rev, name):
    r = hm.shape[0]
    ch = RET_CHUNK
    nch = r // ch
    t_rows = r - ncc * ch
    hpg = _heads_per_step(heads, dk, dv, q_off)
    qb, kb, vb = q_off // (hpg * dk), (q_off + heads * dk) // (hpg * dk), (q_off + 2 * heads * dk) // (hpg * dv)
    q_scale = dk ** -0.5
    nt = (((1,), (1,)), ((), ()))
    tn = (((0,), (0,)), ((), ()))
    cof = lambda s: _chunk_of_step(s, nch, ncc, rev)

    def body(q_ref, k_ref, v_ref, cos_ref, sin_ref, dec_ref, we_ref, wi_ref, g_ref, o_ref, sin_out, st):
        s = pl.program_id(1)

        @pl.when(s == 0)
        def _():
            st[...] = jnp.zeros_like(st)

        cos_, sin_ = cos_ref[...], sin_ref[...]
        for hl in range(hpg):
            ks, vs = slice(hl * dk, (hl + 1) * dk), slice(hl * dv, (hl + 1) * dv)
            q = _rope(q_ref[:, ks], cos_, sin_) * q_scale
            k = _rope(k_ref[:, ks], cos_, sin_)
            v = v_ref[:, vs].astype(MXU_DTYPE)
            s_cur = st[hl]
            sin_out[hl, 0] = s_cur
            kw = (k * we_ref[hl]).astype(MXU_DTYPE)
            qw = (q * wi_ref[hl]).astype(MXU_DTYPE)
            scores = lax.dot_general(q.astype(MXU_DTYPE), k.astype(MXU_DTYPE), nt,
                                     preferred_element_type=F32) * dec_ref[hl]
            o_ref[:, vs] = (jnp.dot(scores.astype(MXU_DTYPE), v, preferred_element_type=F32)
                            + jnp.dot(qw, s_cur.astype(MXU_DTYPE), preferred_element_type=F32))
            st[hl] = g_ref[hl] * s_cur + lax.dot_general(kw, v, tn, preferred_element_type=F32)

    tab = lambda w: pl.BlockSpec((hpg, ch, w), lambda h, s: (h, 0, 0))
    return pl.pallas_call(
        body, name=name, grid=(heads // hpg, nch),
        in_specs=[pl.BlockSpec((ch, hpg * dk), lambda h, s: (cof(s), qb + h)),
                  pl.BlockSpec((ch, hpg * dk), lambda h, s: (cof(s), kb + h)),
                  pl.BlockSpec((ch, hpg * dv), lambda h, s: (cof(s), vb + h)),
                  pl.BlockSpec((ch, dk), lambda h, s: (cof(s), 0)),
                  pl.BlockSpec((ch, dk), lambda h, s: (cof(s), 0)),
                  tab(ch), tab(dk), tab(dk), tab(dv)],
        out_specs=[pl.BlockSpec((ch, hpg * dv), lambda h, s: (jnp.maximum(cof(s) - ncc, 0) if not rev
                                                               else jnp.where(s < ncc, nch - ncc - 1, cof(s) - ncc), h)),
                   pl.BlockSpec((hpg, 1, dk, dv), lambda h, s: (h, s, 0, 0))],
        out_shape=[_sds((t_rows, heads * dv), F32), _sds((heads, nch, dk, dv), F32)],
        scratch_shapes=[pltpu.VMEM((hpg, dk, dv), F32)],
        compiler_params=_params(("parallel", "arbitrary")))(hm, hm, hm, cos, sin, decay, wend, win, gch)


def _ret_bwd(hm, cos, sin, decay, wend, win, gch, s_in, do, heads, dk, dv, q_off, ncc, rev, name):
    r = hm.shape[0]
    ch = RET_CHUNK
    nch = r // ch
    hpg = _heads_per_step(heads, dk, dv, q_off)
    qb, kb, vb = q_off // (hpg * dk), (q_off + heads * dk) // (hpg * dk), (q_off + 2 * heads * dk) // (hpg * dv)
    q_scale = dk ** -0.5
    nt = (((1,), (1,)), ((), ()))
    tn = (((0,), (0,)), ((), ()))
    cof = lambda rr: _chunk_of_step(nch - 1 - rr, nch, ncc, rev)

    def body(q_ref, k_ref, v_ref, cos_ref, sin_ref, dec_ref, we_ref, wi_ref, g_ref, sin_ref2, do_ref,
             dq_ref, dk_ref, dv_ref, ddec_ref, dwe_ref, dwi_ref, dg_ref, dst):
        rr = pl.program_id(1)
        n = cof(rr)

        @pl.when(rr == 0)
        def _():
            dst[...] = jnp.zeros_like(dst)
            ddec_ref[...] = jnp.zeros_like(ddec_ref)
            dwe_ref[...] = jnp.zeros_like(dwe_ref)
            dwi_ref[...] = jnp.zeros_like(dwi_ref)
            dg_ref[...] = jnp.zeros_like(dg_ref)

        cos_, sin_ = cos_ref[...], sin_ref[...]
        for hl in range(hpg):
            ks, vs = slice(hl * dk, (hl + 1) * dk), slice(hl * dv, (hl + 1) * dv)
            q = _rope(q_ref[:, ks], cos_, sin_) * q_scale
            k = _rope(k_ref[:, ks], cos_, sin_)
            v = v_ref[:, vs].astype(MXU_DTYPE)
            qb_, kb_ = q.astype(MXU_DTYPE), k.astype(MXU_DTYPE)
            kw = (k * we_ref[hl]).astype(MXU_DTYPE)
            qw = (q * wi_ref[hl]).astype(MXU_DTYPE)
            sraw = lax.dot_general(qb_, kb_, nt, preferred_element_type=F32)
            scores = (sraw * dec_ref[hl]).astype(MXU_DTYPE)
            d_o = jnp.where(n >= ncc, do_ref[:, vs], 0.0).astype(MXU_DTYPE)
            s_n = sin_ref2[hl, 0]
            s_nb = s_n.astype(MXU_DTYPE)
            ds1 = dst[hl]
            ds1b = ds1.astype(MXU_DTYPE)
            dsc = lax.dot_general(d_o, v, nt, preferred_element_type=F32)
            dsr = (dsc * dec_ref[hl]).astype(MXU_DTYPE)
            ddec_ref[hl] += dsc * sraw
            t1 = lax.dot_general(d_o, s_nb, nt, preferred_element_type=F32)
            dq_r = jnp.dot(dsr, kb_, preferred_element_type=F32) + t1 * wi_ref[hl]
            dwi_ref[hl] += t1 * q
            t2 = lax.dot_general(v, ds1b, nt, preferred_element_type=F32)
            dk_r = lax.dot_general(dsr, qb_, tn, preferred_element_type=F32) + t2 * we_ref[hl]
            dwe_ref[hl] += t2 * k
            dv_ref[:, vs] = (lax.dot_general(scores, d_o, tn, preferred_element_type=F32)
                             + jnp.dot(kw, ds1b, preferred_element_type=F32))
            dg_ref[hl] += ds1 * s_n
            dst[hl] = g_ref[hl] * ds1 + lax.dot_general(qw, d_o, tn, preferred_element_type=F32)
            dq_ref[:, ks] = _rope_t(dq_r, cos_, sin_) * q_scale
            dk_ref[:, ks] = _rope_t(dk_r, cos_, sin_)

    tab = lambda w: pl.BlockSpec((hpg, ch, w), lambda h, rr: (h, 0, 0))
    return pl.pallas_call(
        body, name=name, grid=(heads // hpg, nch),
        in_specs=[pl.BlockSpec((ch, hpg * dk), lambda h, rr: (cof(rr), qb + h)),
                  pl.BlockSpec((ch, hpg * dk), lambda h, rr: (cof(rr), kb + h)),
                  pl.BlockSpec((ch, hpg * dv), lambda h, rr: (cof(rr), vb + h)),
                  pl.BlockSpec((ch, dk), lambda h, rr: (cof(rr), 0)),
                  pl.BlockSpec((ch, dk), lambda h, rr: (cof(rr), 0)),
                  tab(ch), tab(dk), tab(dk), tab(dv),
                  pl.BlockSpec((hpg, 1, dk, dv), lambda h, rr: (h, nch - 1 - rr, 0, 0)),
                  pl.BlockSpec((ch, hpg * dv), lambda h, rr: (jnp.maximum(cof(rr) - ncc, 0), h))],
        out_specs=[pl.BlockSpec((ch, hpg * dk), lambda h, rr: (cof(rr), h)),
                   pl.BlockSpec((ch, hpg * dk), lambda h, rr: (cof(rr), h)),
                   pl.BlockSpec((ch, hpg * dv), lambda h, rr: (cof(rr), h)),
                   tab(ch), tab(dk), tab(dk), tab(dv)],
        out_shape=[_sds((r, heads * dk), F32), _sds((r, heads * dk), F32), _sds((r, heads * dv), F32),
                   _sds(decay.shape, F32), _sds(wend.shape, F32), _sds(win.shape, F32), _sds(gch.shape, F32)],
        scratch_shapes=[pltpu.VMEM((hpg, dk, dv), F32)],
        compiler_params=_params(("parallel", "arbitrary")))(hm, hm, hm, cos, sin, decay, wend, win, gch, s_in, do)


_HBM = pl.BlockSpec(memory_space=pltpu.HBM)
_MESH = pl.DeviceIdType.MESH
ALL_GATHER_COLLECTIVE_ID = 1
SIBLING_COLLECTIVE_ID = 2
CHIPS_COLLECTIVE_ID = 3


def _axis_slice(ref, axis, start, size):
    idx = [slice(None)] * len(ref.shape)
    idx[axis] = pl.ds(start, size)
    return ref.at[tuple(idx)]


def _sibling_and_chip_peers():
    x, y, c = lax.axis_index("x"), lax.axis_index("y"), lax.axis_index("c")
    return [(x, y, 1 - c), (1 - x, y, c), (x, 1 - y, c), (1 - x, 1 - y, c)]


def _launch_exchange(body, name, operand, out_shape, sems, peers_fn, collective_id, on_sequencer):
    if not on_sequencer:
        return pl.pallas_call(body, name=name, out_shape=out_shape, in_specs=[_HBM], out_specs=_HBM,
                              scratch_shapes=sems)(operand)

    def sequencer_body(in_ref, out_ref, *sem_refs):
        peers = peers_fn()
        barrier = pltpu.get_barrier_semaphore()
        for peer in peers:
            pl.semaphore_signal(barrier, inc=1, device_id=peer, device_id_type=_MESH)
        pl.semaphore_wait(barrier, len(peers))
        body(in_ref, out_ref, *sem_refs)

    return pl.kernel(sequencer_body, out_type=out_shape, name=name,
                     mesh=plsc.ScalarSubcoreMesh(axis_name="sequencer", num_cores=1), scratch_types=sems,
                     compiler_params=pltpu.CompilerParams(collective_id=collective_id))(operand)


def _all_gather(shard, axis, name, on_sequencer=False):
    m = shard.shape[axis]
    out_shape = list(shard.shape)
    out_shape[axis] = N_DEV * m

    def body(x_ref, out_ref, send_sems, recv_sems, local_sem):
        x, y, c = lax.axis_index("x"), lax.axis_index("y"), lax.axis_index("c")
        me, sibling = (x, y, c), (x, y, 1 - c)
        chips = [(1 - x, y), (x, 1 - y), (1 - x, 1 - y)]

        def block(px, py, pc):
            return _axis_slice(out_ref, axis, (4 * px + 2 * py + pc) * m, m)

        def copy(k, blk, to, src=None):
            return pltpu.make_async_remote_copy(
                src_ref=block(*blk) if src is None else src, dst_ref=block(*blk), send_sem=send_sems.at[k],
                recv_sem=recv_sems.at[k], device_id=to, device_id_type=_MESH)

        mine = pltpu.make_async_copy(x_ref, block(*me), local_sem)
        mine.start()
        first = [copy(0, me, sibling, src=x_ref)]
        first += [copy(1 + j, me, (*chip, c), src=x_ref) for j, chip in enumerate(chips)]
        for cp in first:
            cp.start()
        passed = [copy(4 + j, (*chip, c), sibling) for j, chip in enumerate(chips)]
        for j, chip in enumerate(chips):
            copy(1 + j, (*chip, c), me).wait_recv()
            passed[j].start()
        copy(0, sibling, me).wait_recv()
        for j, chip in enumerate(chips):
            copy(4 + j, (*chip, 1 - c), me).wait_recv()
        for cp in first + passed:
            cp.wait_send()
        mine.wait()

    return _launch_exchange(
        body, name, shard, _sds(out_shape, shard.dtype),
        [pltpu.SemaphoreType.DMA((7,)), pltpu.SemaphoreType.DMA((7,)), pltpu.SemaphoreType.DMA(())],
        _sibling_and_chip_peers, ALL_GATHER_COLLECTIVE_ID, on_sequencer)


def _rs_sibling(g, axis, name, on_sequencer=False):
    m = g.shape[axis] // N_DEV
    blk_shape = list(g.shape)
    blk_shape[axis] = m
    n_chips = N_DEV // 2

    def body(g_ref, recv_ref, send_sems, recv_sems):
        x, y, c = lax.axis_index("x"), lax.axis_index("y"), lax.axis_index("c")
        sibling = (x, y, 1 - c)
        send = [pltpu.make_async_remote_copy(
            src_ref=_axis_slice(g_ref, axis, (2 * q + 1 - c) * m, m), dst_ref=recv_ref.at[q],
            send_sem=send_sems.at[q], recv_sem=recv_sems.at[q], device_id=sibling, device_id_type=_MESH)
            for q in range(n_chips)]
        for cp in send:
            cp.start()
        for cp in send:
            cp.wait_recv()
        for cp in send:
            cp.wait_send()

    return _launch_exchange(
        body, name, g, _sds([n_chips] + blk_shape, g.dtype),
        [pltpu.SemaphoreType.DMA((n_chips,)), pltpu.SemaphoreType.DMA((n_chips,))],
        lambda: _sibling_and_chip_peers()[:1], SIBLING_COLLECTIVE_ID, on_sequencer)


def _rs_chips(p, name, on_sequencer=False):
    n_peers = p.shape[0] - 1

    def body(p_ref, out_ref, send_sems, recv_sems):
        x, y, c = lax.axis_index("x"), lax.axis_index("y"), lax.axis_index("c")
        chips = [(1 - x, y), (x, 1 - y), (1 - x, 1 - y)]
        send = [pltpu.make_async_remote_copy(
            src_ref=p_ref.at[2 * cx + cy], dst_ref=out_ref.at[j], send_sem=send_sems.at[j],
            recv_sem=recv_sems.at[j], device_id=(cx, cy, c), device_id_type=_MESH)
            for j, (cx, cy) in enumerate(chips)]
        for cp in send:
            cp.start()
        for cp in send:
            cp.wait_recv()
        for cp in send:
            cp.wait_send()

    return _launch_exchange(
        body, name, p, _sds((n_peers,) + p.shape[1:], p.dtype),
        [pltpu.SemaphoreType.DMA((n_peers,)), pltpu.SemaphoreType.DMA((n_peers,))],
        lambda: _sibling_and_chip_peers()[1:], CHIPS_COLLECTIVE_ID, on_sequencer)


def _reduce_scatter(g, axis, name):
    sib = _rs_sibling(g, axis, name + "_d2d", on_sequencer=True)
    p = _pair_sum(g, sib, axis, name + "_pair")
    return p, _rs_chips(p, name + "_ici", on_sequencer=True)


def _s5_tables(lam_re, lam_im, log_step, b_re, b_im, c_re, c_im):
    nd, g, p, cg = b_re.shape
    step = jnp.exp(log_step)[..., None]
    mag = jnp.exp(lam_re * step)
    a_re, a_im = mag * jnp.cos(lam_im * step), mag * jnp.sin(lam_im * step)
    den = lam_re * lam_re + lam_im * lam_im
    num_re, num_im = a_re - 1.0, a_im
    k_re = (num_re * lam_re + num_im * lam_im) / den
    k_im = (num_im * lam_re - num_re * lam_im) / den
    bb_re = k_re[..., None] * b_re - k_im[..., None] * b_im
    bb_im = k_re[..., None] * b_im + k_im[..., None] * b_re
    gt = g // SSM_TILE_GROUPS
    hg = SSM_HALF_GROUPS
    eye = jnp.eye(SSM_TILE_GROUPS, dtype=F32).reshape(SSM_TILE_GROUPS, 2, hg)

    def pack_b(bb):
        w = jnp.einsum("djhqpc,ghq->djhgcqp", bb.reshape(nd, gt, 2, hg, p, cg), eye)
        return w.reshape(nd, gt * 2, SSM_TILE_GROUPS * cg, hg * p)

    def pack_c(cc):
        w = jnp.einsum("djhqcp,ghq->djhqpgc", cc.reshape(nd, gt, 2, hg, cg, p), eye)
        return w.reshape(nd, gt * 2, hg * p, SSM_TILE_GROUPS * cg)

    a = jnp.stack([a_re.reshape(nd, gt * 2, hg * p), a_im.reshape(nd, gt * 2, hg * p)], axis=2)
    w = jnp.concatenate([pack_b(bb_re), pack_b(bb_im)], axis=-1)
    c = jnp.concatenate([pack_c(c_re), -pack_c(c_im)], axis=-2)
    return w, c, a


def _ret_tables(decay_logit, dk, dv):
    ch = RET_CHUNK
    nd, h = decay_logit.shape
    lg = jax.nn.log_sigmoid(decay_logit)[:, :, None]
    pos = jnp.arange(ch, dtype=F32)
    fwd_diff = pos[:, None] - pos[None, :]
    diff = jnp.stack([fwd_diff, -fwd_diff])[:, None]
    mask = jnp.stack([fwd_diff >= 0, -fwd_diff > 0])[:, None]
    end_pos = jnp.stack([ch - 1.0 - pos, pos])[:, None]
    in_pos = jnp.stack([pos + 1.0, ch - pos])[:, None]
    w_end = jnp.exp(lg * end_pos)
    w_in = jnp.exp(lg * in_pos)
    decay = jnp.where(mask, jnp.exp(lg[..., None] * jnp.where(mask, diff, 0.0)), 0.0)
    g_chunk = jnp.exp(lg[..., 0] * ch)
    return (decay, jnp.broadcast_to(w_end[..., None], (nd, h, ch, dk)), jnp.broadcast_to(w_in[..., None], (nd, h, ch, dk)),
            jnp.broadcast_to(g_chunk[..., None, None], (nd, h, dk, dv)))


def _rope_tables(t_rows, ncc, dk):
    quarter = dk // 4
    idx = np.arange(t_rows)
    row, col = idx // GRID_W, idx % GRID_W
    inv = ROPE_BASE ** (-np.arange(quarter, dtype=np.float32) / quarter)
    ang_r = row.astype(np.float32)[:, None] * inv
    ang_c = col.astype(np.float32)[:, None] * inv
    ang_r, ang_c = jnp.asarray(ang_r, F32), jnp.asarray(ang_c, F32)
    cos = jnp.concatenate([jnp.cos(ang_r), jnp.cos(ang_r), jnp.cos(ang_c), jnp.cos(ang_c)], axis=1)
    sin = jnp.concatenate([-jnp.sin(ang_r), jnp.sin(ang_r), -jnp.sin(ang_c), jnp.sin(ang_c)], axis=1)
    n_ctx = ncc * RET_CHUNK
    cos = jnp.concatenate([jnp.ones((n_ctx, dk), F32), cos], axis=0)
    sin = jnp.concatenate([jnp.zeros((n_ctx, dk), F32), sin], axis=0)
    return cos, sin


def _to_scan_layout(ctx_rows, lat_rows, rev):
    u = jnp.concatenate([lat_rows, ctx_rows] if rev else [ctx_rows, lat_rows], axis=0)
    r, w = u.shape
    return u.reshape(N_SEG, r // N_SEG, w).transpose(1, 0, 2).reshape(r, w)


def _from_scan_layout(yp, n_ctx, rev):
    r, w = yp.shape
    y = yp.reshape(r // N_SEG, N_SEG, w).transpose(1, 0, 2).reshape(r, w)
    return (y[r - n_ctx:], y[:r - n_ctx]) if rev else (y[:n_ctx], y[n_ctx:])


def _pack(parts, width):
    rows = []
    for p in parts:
        flat = p.reshape(-1).astype(F32)
        n = flat.shape[0]
        rows.append(jnp.pad(flat, (0, -n % (SUBLANE * width))).reshape(-1, width))
    return jnp.concatenate(rows, axis=0)


def _packed_rows(n, width):
    return -(-n // (SUBLANE * width)) * SUBLANE


def _unpack(flat2d, shapes):
    width = flat2d.shape[1]
    out, row = [], 0
    for shp in shapes:
        n = int(np.prod(shp))
        nr = _packed_rows(n, width)
        out.append(flat2d[row:row + nr].reshape(-1)[:n].reshape(shp))
        row += nr
    return out


def kernel(x, c, ctx, c_ctx, ada_w, ada_b, norm_g, ffn_w_in, ffn_w_out, mix_w_in, ssm_lam_re, ssm_lam_im, ssm_log_step, ssm_b_re, ssm_b_im, ssm_c_re, ssm_c_im, ssm_d, ssm_glu_w, ret_decay_logit, ret_w_proj, mix_w_out, loss_target, m_c_ctx, m_ada_w, m_ada_b, m_norm_g, m_ffn_w_in, m_ffn_w_out, m_mix_w_in, m_ssm_lam_re, m_ssm_lam_im, m_ssm_log_step, m_ssm_b_re, m_ssm_b_im, m_ssm_c_re, m_ssm_c_im, m_ssm_d, m_ssm_glu_w, m_ret_decay_logit, m_ret_w_proj, m_mix_w_out, v_c_ctx, v_ada_w, v_ada_b, v_norm_g, v_ffn_w_in, v_ffn_w_out, v_mix_w_in, v_ssm_lam_re, v_ssm_lam_im, v_ssm_log_step, v_ssm_b_re, v_ssm_b_im, v_ssm_c_re, v_ssm_c_im, v_ssm_d, v_ssm_glu_w, v_ret_decay_logit, v_ret_w_proj, v_mix_w_out):
    t_rows, d = x.shape[1], x.shape[2]
    n_ctx = ctx.shape[1]
    r = n_ctx + t_rows
    ssm_w = ssm_d.shape[1]
    heads = ret_decay_logit.shape[2]
    mi = mix_w_in.shape[2] * N_DEV
    dk = (mi - ssm_w - 2 * d) // (6 * heads)
    dv = 2 * dk
    qk_w, v_w = heads * dk, heads * dv
    q_off = ssm_w
    ncc = n_ctx // RET_CHUNK
    tile = n_ctx
    nct = 1
    wide_tile = _tile(n_ctx, 128, 16)
    assert r % (N_SEG * SUBLANE) == 0 and n_ctx % RET_CHUNK == 0 and t_rows % tile == 0
    me = 4 * lax.axis_index("x") + 2 * lax.axis_index("y") + lax.axis_index("c")
    g_off = ssm_w + 2 * qk_w + v_w
    gs_off = g_off + v_w

    ng_cols = norm_g.shape[2]
    small0 = _pack([c[0], norm_g[0]], d)
    small0_all = _all_gather(small0, 0, "ag_cond")

    bf = lambda w: w.astype(BF16)
    small0_all, sh_in1, sh_out1, sh_mix = lax.optimization_barrier(
        (small0_all, bf(ffn_w_in[0, 0]), bf(ffn_w_out[0, 0]), bf(mix_w_in[0])))
    small0_all = small0_all.reshape(N_DEV, -1)
    w_in1 = _all_gather(sh_in1, 1, "ag_ffn1_in", on_sequencer=True)
    w_out1 = _all_gather(sh_out1, 0, "ag_ffn1_out", on_sequencer=True)
    w_mix = _all_gather(sh_mix, 1, "ag_mix_in", on_sequencer=True)
    w_glu = _all_gather(bf(ssm_glu_w[0]), 1, "ag_glu", on_sequencer=True)
    w_rp = _all_gather(bf(ret_w_proj[0]), 0, "ag_ret_proj", on_sequencer=True)
    w_mo = _all_gather(bf(mix_w_out[0]), 0, "ag_mix_out", on_sequencer=True)
    w_in2 = _all_gather(bf(ffn_w_in[0, 1]), 1, "ag_ffn2_in", on_sequencer=True)
    w_out2 = _all_gather(bf(ffn_w_out[0, 1]), 0, "ag_ffn2_out", on_sequencer=True)

    ng_at = _packed_rows(d, d) * d
    c_all = small0_all[:, :d]
    g_full = small0_all[:, ng_at:ng_at + 6 * ng_cols].reshape(N_DEV, 6, ng_cols).transpose(1, 0, 2).reshape(6, d)
    g6 = g_full.reshape(6, 1, d)
    cc = jnp.concatenate([c_all, c_ctx[None, :], jnp.zeros((2 * SUBLANE - N_DEV - 1, d), F32)], axis=0)
    sc = _silu_rows(cc, "ada_silu")
    na = ada_w.shape[2]
    a_loc = _mm(sc, ada_w[0], "nn", F32, "ada_fwd", tm=16, tn=na, tk=512)
    a_all = _all_gather(a_loc, 0, "ag_ada").reshape(N_DEV, 2 * SUBLANE, na)
    ada_x = lax.dynamic_index_in_dim(a_all, me, axis=1, keepdims=False).reshape(9 * d) + ada_b[0]
    ada_c = a_all[:, N_DEV, :].reshape(9 * d) + ada_b[0]
    mods = jnp.stack([ada_c.reshape(9, d), ada_x.reshape(9, d)]).reshape(18, 1, d)

    xin = jnp.concatenate([ctx[0], x[0]], axis=0)
    u1 = _ada_pre_fwd(xin, g6, mods, 0, 0, nct, tile, "pre1")
    h1 = _mm(u1, w_in1, "nn", BF16, "ffn1_in", tm=544, tn=1024)
    a1 = _swiglu_fwd(h1, wide_tile, "swiglu1")
    o1 = _mm(a1, w_out1, "nn", F32, "ffn1_out", tm=544, tn=1024, tk=2816)
    x1 = _ada_post_fwd(xin, o1, g6, mods, 1, 0, 0.5, nct, tile, "post1")
    u2 = _ada_pre_fwd(x1, g6, mods, 2, 1, nct, tile, "pre2")
    hm = _mm(u2, w_mix, "nn", F32, "mix_in", tm=544, tn=1024)

    us_ctx, us_lat = hm[:n_ctx, :ssm_w], hm[n_ctx:, :ssm_w]
    dskip = ssm_d.reshape(1, 1, ssm_w)
    s5_prm = (ssm_lam_re[0], ssm_lam_im[0], ssm_log_step[0], ssm_b_re[0], ssm_b_im[0], ssm_c_re[0], ssm_c_im[0])
    s5_tabs_both, s5_vjp = jax.vjp(_s5_tables, *s5_prm)
    s5_tabs, ups, y_dirs = [], [], []
    for dr in range(2):
        tabs = tuple(t[dr] for t in s5_tabs_both)
        up = _to_scan_layout(us_ctx, us_lat, dr == 1)
        yp = _s5_fwd(up, *tabs, dr == 1, "s5_fwd%d" % dr)
        s5_tabs.append(tabs)
        ups.append(up)
        y_dirs.append(_from_scan_layout(yp, n_ctx, dr == 1)[1])
    a_ssm = _ssm_out_fwd(y_dirs[0], y_dirs[1], hm, dskip, nct, tile, "ssm_out")
    gab = _mm(a_ssm, w_glu, "nn", F32, "glu", tm=512, tn=2048, tk=ssm_w)

    cos, sin = _rope_tables(t_rows, ncc, dk)
    ret_tabs_both, ret_vjp = jax.vjp(functools.partial(_ret_tables, dk=dk, dv=dv), ret_decay_logit[0])
    ret_tabs, o_dirs, s_ins = [], [], []
    for dr in range(2):
        tabs = tuple(t[dr] for t in ret_tabs_both)
        o_d, s_in = _ret_fwd(hm, cos, sin, *tabs, heads, dk, dv, q_off, ncc, dr == 1, "ret_fwd%d" % dr)
        ret_tabs.append(tabs)
        o_dirs.append(o_d)
        s_ins.append(s_in)
    ret_in = _ret_gate_fwd(o_dirs[0], o_dirs[1], hm, g_off, heads, dv, nct, tile, "ret_gate")
    rb = _mm(ret_in, w_rp, "nn", F32, "ret_proj", tm=512, tn=d, tk=v_w)
    merged = _merge_fwd(gab, rb, hm, gs_off, nct, tile, "merge")
    mix = _mm(merged, w_mo, "nn", F32, "mix_out", tm=512, tn=d, tk=d)
    x1x = x1[n_ctx:]
    x2 = _ada_post_fwd(x1x, mix, g6, mods, 3, 1, 1.0, 0, tile, "post2")
    u3 = _ada_pre_fwd(x2, g6, mods, 4, 2, 0, tile, "pre3")
    h3 = _mm(u3, w_in2, "nn", BF16, "ffn2_in", tm=512, tn=1024)
    a3 = _swiglu_fwd(h3, wide_tile, "swiglu2")
    o3 = _mm(a3, w_out2, "nn", F32, "ffn2_out", tm=512, tn=1024, tk=2816)
    x3 = _ada_post_fwd(x2, o3, g6, mods, 5, 2, 0.5, 0, tile, "post3")
    dy, lcols = _loss_grad(x3, loss_target[0], tile, "loss")
    loss_part = (0.5 * jnp.sum(lcols) / d).reshape(1)

    dg6 = [None] * 6
    dmod = {}

    def add_mod(sel_rows, k, val):
        for sel, row in sel_rows:
            dmod[(sel, k)] = dmod.get((sel, k), 0.0) + val[row, 0]

    both, lat = [(0, 0), (1, 1)], [(1, 0)]

    def tie(*vals):
        return lax.optimization_barrier(vals)

    def big_update(w3d, m3d, v3d, layer, gfull, axis, name, filled=None):
        p, recv = _reduce_scatter(gfull, axis, "rs_" + name)
        return _adamw_scattered(w3d, m3d, v3d, layer, p, recv, "adamw_" + name, filled)

    do3, dg6[5], dgt = _ada_post_bwd(dy, o3, g6, mods, 5, 2, 0.5, 0, 1, tile, "post3_bwd")
    add_mod(lat, 8, dgt)
    gw_out2 = _mm(a3, do3, "tn", BF16, "ffn2_out_dw", tm=1408, tn=1024, tk=2176)
    do3, gw_out2 = tie(do3, gw_out2)
    up_out2 = big_update(ffn_w_out[0], m_ffn_w_out[0], v_ffn_w_out[0], 1, gw_out2, 0, "ffn2_out")
    da3 = _mm(do3, w_out2, "nt", F32, "ffn2_out_dx", tm=512, tn=2816, tk=d)
    dh3 = _swiglu_bwd(h3, da3, wide_tile, "swiglu2_bwd")
    gw_in2 = _mm(u3, dh3, "tn", BF16, "ffn2_in_dw", tm=1024, tn=1024, tk=2176)
    dh3, gw_in2 = tie(dh3, gw_in2)
    up_in2 = big_update(ffn_w_in[0], m_ffn_w_in[0], v_ffn_w_in[0], 1, gw_in2, 1, "ffn2_in")
    du3 = _mm(dh3, w_in2, "nt", F32, "ffn2_in_dx", tm=512, tn=d, tk=1024)
    dx2, dg6[4], dsh, dsc = _ada_pre_bwd(x2, du3, dy, g6, mods, 4, 2, 0, 1, tile, "pre3_bwd")
    add_mod(lat, 6, dsh)
    add_mod(lat, 7, dsc)
    dmix, dg6[3], dgt = _ada_post_bwd(dx2, mix, g6, mods, 3, 1, 1.0, 0, 1, tile, "post2_bwd")
    add_mod(lat, 5, dgt)
    gw_mo = _mm(merged, dmix, "tn", BF16, "mix_out_dw", tm=1024, tn=1024, tk=2176)
    dmix, gw_mo = tie(dmix, gw_mo)
    up_mo = big_update(mix_w_out, m_mix_w_out, v_mix_w_out, 0, gw_mo, 0, "mix_out")
    dmerged = _mm(dmix, w_mo, "nt", F32, "mix_out_dx", tm=512, tn=d, tk=d)
    dgab, drb, dgs, dgr = _merge_bwd(gab, rb, hm, gs_off, dmerged, nct, tile, "merge_bwd")
    gw_glu = _mm(a_ssm, dgab, "tn", BF16, "glu_dw", tm=1024, tn=1024, tk=2176)
    gw_rp = _mm(ret_in, drb, "tn", BF16, "ret_proj_dw", tm=1024, tn=1024, tk=2176)
    dgab, drb, gw_glu, gw_rp = tie(dgab, drb, gw_glu, gw_rp)
    up_glu = big_update(ssm_glu_w, m_ssm_glu_w, v_ssm_glu_w, 0, gw_glu, 1, "glu")
    up_rp = big_update(ret_w_proj, m_ret_w_proj, v_ret_w_proj, 0, gw_rp, 0, "ret_proj")
    da_ssm = _mm(dgab, w_glu, "nt", F32, "glu_dx", tm=512, tn=ssm_w, tk=2 * d)
    dret_in = _mm(drb, w_rp, "nt", F32, "ret_proj_dx", tm=512, tn=v_w, tk=d)
    d_o, dg_gate = _ret_gate_bwd(o_dirs[0], o_dirs[1], hm, g_off, dret_in, heads, dv, nct, tile, "ret_gate_bwd")
    dy_ssm, dus_direct, d_dskip = _ssm_out_bwd(y_dirs[0], y_dirs[1], hm, dskip, da_ssm, nct, tile, "ssm_out_bwd")
    s5_table_grads, du_ctx, du_lat = [], [], [dus_direct]
    for dr in range(2):
        dyp = _to_scan_layout(jnp.zeros((n_ctx, ssm_w), F32), dy_ssm, dr == 1)
        if dr == 1:
            dyp, up_out2, up_in2 = tie(dyp, up_out2, up_in2)
        outs = _s5_bwd(ups[dr], dyp, *s5_tabs[dr], dr == 1, "s5_bwd%d" % dr)
        part_ctx, part_lat = _from_scan_layout(outs[0], n_ctx, dr == 1)
        du_ctx.append(part_ctx)
        du_lat.append(part_lat)
        s5_table_grads.append(outs[1:])
    dqkv, ret_table_grads = [], []
    for dr in range(2):
        if dr == 1:
            d_o, up_mo, up_glu, up_rp = tie(d_o, up_mo, up_glu, up_rp)
        outs = _ret_bwd(hm, cos, sin, *ret_tabs[dr], s_ins[dr], d_o, heads, dk, dv, q_off, ncc, dr == 1,
                        "ret_bwd%d" % dr)
        dqkv.append(outs[:3])
        ret_table_grads.append(outs[3:])
    both_dirs = lambda grads: tuple(jnp.stack([g0, g1]) for g0, g1 in zip(*grads))
    early_parts = list(s5_vjp(both_dirs(s5_table_grads))) + list(ret_vjp(both_dirs(ret_table_grads)))
    s5_names = 7
    early_shapes = [p.shape for p in early_parts]
    early_all = _all_gather(_pack(early_parts, 1024), 0, "ag_s5_grads", on_sequencer=True)
    early_sum = _sum_leading(early_all.reshape(N_DEV, -1, 1024), "sum_s5_grads")
    dus = jnp.concatenate([du_ctx[0] + du_ctx[1], du_lat[0] + du_lat[1] + du_lat[2]], axis=0)
    dhm = _assemble_dhm(dus, dqkv[0][0], dqkv[1][0], dqkv[0][1], dqkv[1][1], dqkv[0][2], dqkv[1][2],
                        dg_gate, dgs, dgr, n_ctx // wide_tile, wide_tile, "assemble_dhm")
    gw_mix = _mm(u2, dhm, "tn", BF16, "mix_in_dw", tm=1024, tn=1024, tk=2176)
    dhm, gw_mix = tie(dhm, gw_mix)
    up_mix = big_update(mix_w_in, m_mix_w_in, v_mix_w_in, 0, gw_mix, 1, "mix_in")
    du2 = _mm(dhm, w_mix, "nt", F32, "mix_in_dx", tm=544, tn=d, tk=1024)
    dx1, dg6[2], dsh, dsc = _ada_pre_bwd(x1, du2, dx2, g6, mods, 2, 1, nct, 2, tile, "pre2_bwd", dres_x_only=True)
    add_mod(both, 3, dsh)
    add_mod(both, 4, dsc)
    do1, dg6[1], dgt = _ada_post_bwd(dx1, o1, g6, mods, 1, 0, 0.5, nct, 2, tile, "post1_bwd")
    add_mod(both, 2, dgt)
    gw_out1 = _mm(a1, do1, "tn", BF16, "ffn1_out_dw", tm=1408, tn=1024, tk=2176)
    do1, gw_out1 = tie(do1, gw_out1)
    up_out1 = big_update(ffn_w_out[0], m_ffn_w_out[0], v_ffn_w_out[0], 0, gw_out1, 0, "ffn1_out", filled=up_out2)
    da1 = _mm(do1, w_out1, "nt", F32, "ffn1_out_dx", tm=544, tn=2816, tk=d)
    dh1 = _swiglu_bwd(h1, da1, wide_tile, "swiglu1_bwd")
    dh1, up_mix, early_sum = tie(dh1, up_mix, early_sum)
    early_sums = _unpack(early_sum, early_shapes)
    gw_in1 = _mm(u1, dh1, "tn", BF16, "ffn1_in_dw", tm=1024, tn=1024, tk=2176)
    dh1, gw_in1 = tie(dh1, gw_in1)
    up_in1 = big_update(ffn_w_in[0], m_ffn_w_in[0], v_ffn_w_in[0], 0, gw_in1, 1, "ffn1_in", filled=up_in2)
    du1 = _mm(dh1, w_in1, "nt", F32, "ffn1_in_dx", tm=544, tn=d, tk=1024)
    dxin, dg6[0], dsh, dsc = _ada_pre_bwd(xin, du1, dx1, g6, mods, 0, 0, nct, 2, tile, "pre1_bwd")
    add_mod(both, 0, dsh)
    add_mod(both, 1, dsc)
    grad_x = dxin[n_ctx:][None]

    zero_d = jnp.zeros((d,), F32)
    d_ada_x = jnp.stack([dmod.get((1, k), zero_d) for k in range(9)]).reshape(9 * d)
    d_ada_c = jnp.stack([dmod.get((0, k), zero_d) for k in range(9)]).reshape(9 * d)
    dg_full = jnp.stack([g[0, 0] for g in dg6])
    small_parts = [d_ada_x, d_ada_c, dg_full, d_dskip, loss_part]
    small_shapes = [p.shape for p in small_parts]
    packed = _pack(small_parts, 1024)
    gathered = _all_gather(packed, 0, "ag_small_grads").reshape(N_DEV, -1, 1024)
    summed = _sum_leading(gathered, "sum_small_grads")
    sums = _unpack(summed, small_shapes)
    sum_dx, sum_dc, sum_dg = sums[0], sums[1], sums[2]
    loss = sums[4][0]
    grad_ada_b = (sum_dx + sum_dc)[None]
    dx_rows = gathered.reshape(N_DEV, -1)[:, :9 * d]
    col0 = me * na
    da_rows = jnp.concatenate([lax.dynamic_slice_in_dim(dx_rows, col0, na, axis=1),
                               lax.dynamic_slice_in_dim(sum_dc[None], col0, na, axis=1),
                               jnp.zeros((2 * SUBLANE - N_DEV - 1, na), F32)], axis=0)
    grad_ada_w = _mm(sc, da_rows, "tn", F32, "ada_dw", tm=512, tn=na, tk=16)
    d_sc = _mm(da_rows, ada_w[0], "nt", F32, "ada_dx", tm=16, tn=512, tk=na)
    d_sc_all = _all_gather(jnp.broadcast_to(d_sc[N_DEV:N_DEV + 1], (SUBLANE, d)), 0, "ag_dctx")
    d_sc_sum = _sum_leading(d_sc_all.reshape(N_DEV, SUBLANE, d), "sum_dctx")
    grad_c_ctx = _silu_grad_rows(jnp.broadcast_to(c_ctx[None], (SUBLANE, d)), d_sc_sum, "ctx_silu_bwd")[0]
    grad_norm_g = lax.dynamic_slice_in_dim(sum_dg, me * ng_cols, ng_cols, axis=1)[None]

    upd = {}
    upd["ffn_w_in"] = [o[None] for o in up_in1]
    upd["ffn_w_out"] = [o[None] for o in up_out1]
    upd["mix_w_in"] = list(up_mix)
    upd["ssm_glu_w"] = list(up_glu)
    upd["ret_w_proj"] = list(up_rp)
    upd["mix_w_out"] = list(up_mo)
    upd["ada_w"] = [o[None] for o in _adamw(ada_w[0], m_ada_w[0], v_ada_w[0], grad_ada_w[None], "adamw_ada_w")]

    small_names = ["c_ctx", "ada_b", "norm_g", "ssm_lam_re", "ssm_lam_im", "ssm_log_step", "ssm_b_re", "ssm_b_im",
                   "ssm_c_re", "ssm_c_im", "ssm_d", "ret_decay_logit"]
    small_w = [c_ctx, ada_b, norm_g, ssm_lam_re, ssm_lam_im, ssm_log_step, ssm_b_re, ssm_b_im, ssm_c_re, ssm_c_im,
               ssm_d, ret_decay_logit]
    small_m = [m_c_ctx, m_ada_b, m_norm_g, m_ssm_lam_re, m_ssm_lam_im, m_ssm_log_step, m_ssm_b_re, m_ssm_b_im,
               m_ssm_c_re, m_ssm_c_im, m_ssm_d, m_ret_decay_logit]
    small_v = [v_c_ctx, v_ada_b, v_norm_g, v_ssm_lam_re, v_ssm_lam_im, v_ssm_log_step, v_ssm_b_re, v_ssm_b_im,
               v_ssm_c_re, v_ssm_c_im, v_ssm_d, v_ret_decay_logit]
    small_g = [grad_c_ctx, grad_ada_b, grad_norm_g] + [s[None] for s in early_sums[:s5_names]] + \
              [sums[3].reshape(ssm_d.shape), early_sums[s5_names][None]]
    shapes = [w.shape for w in small_w]
    res = _adamw(_pack(small_w, 1024), _pack(small_m, 1024), _pack(small_v, 1024), _pack(small_g, 1024)[None],
                 "adamw_small")
    small_out = [_unpack(o, shapes) for o in res]
    for i, nm in enumerate(small_names):
        upd[nm] = [small_out[kind][i] for kind in range(4)]

    order = ["c_ctx", "ada_w", "ada_b", "norm_g", "ffn_w_in", "ffn_w_out", "mix_w_in", "ssm_lam_re", "ssm_lam_im",
             "ssm_log_step", "ssm_b_re", "ssm_b_im", "ssm_c_re", "ssm_c_im", "ssm_d", "ssm_glu_w", "ret_decay_logit",
             "ret_w_proj", "mix_w_out"]
    outs = [loss, grad_x]
    for kind in range(4):
        outs += [upd[nm][kind] for nm in order]
    return tuple(outs)
```

```python
import functools
import math

import jax
import jax.numpy as jnp
import numpy as np
from jax import lax
from jax.experimental import pallas as pl
from jax.experimental.pallas import tpu as pltpu
from jax.experimental.pallas import tpu_sc as plsc

F32 = jnp.float32
BF16 = jnp.bfloat16
MXU_DTYPE = jnp.bfloat16
MESH_AXES = ("x", "y", "c")
N_DEV = 8
V7X_VMEM_LIMIT_BYTES = 56 * 1024 * 1024
LANE = 128
SUBLANE = 8

GRID_W = 64
RET_CHUNK = 128
ROPE_BASE = 10000.0
NORM_EPS = 1e-6
ADAM_LR = 0.001
ADAM_B1 = 0.9
ADAM_B2 = 0.999
ADAM_EPS = 1e-08
ADAM_WD = 0.01
ADAM_STEP = 10
SSM_TILE_GROUPS = 8
SSM_HALF_GROUPS = 4
N_SEG = 16


def _params(sem=None):
    return pltpu.CompilerParams(dimension_semantics=sem, vmem_limit_bytes=V7X_VMEM_LIMIT_BYTES)


def _tile(n, target, mult):
    best = None
    t = mult
    while t <= min(n, target):
        if n % t == 0:
            best = t
        t += mult
    return n if best is None else best


def _sds(shape, dtype):
    return jax.ShapeDtypeStruct(tuple(shape), dtype)


def _mm(a, b, dims, out_dtype, name, tm=512, tn=1408, tk=2048):
    if dims == "nn":
        (m, k), (k2, n) = a.shape, b.shape
    elif dims == "nt":
        (m, k), (n, k2) = a.shape, b.shape
    else:
        (k, m), (k2, n) = a.shape, b.shape
    assert k == k2, (a.shape, b.shape, dims)
    tm = _tile(m, tm, 16)
    tn = _tile(n, tn, LANE)
    tk = _tile(k, tk, LANE if dims != "tn" else 16)
    nk = k // tk
    dn = {"nn": (((1,), (0,)), ((), ())), "nt": (((1,), (1,)), ((), ())), "tn": (((0,), (0,)), ((), ()))}[dims]

    def product(a_ref, b_ref):
        return lax.dot_general(a_ref[...].astype(MXU_DTYPE), b_ref[...].astype(MXU_DTYPE), dn,
                               preferred_element_type=F32)

    def body_single(a_ref, b_ref, o_ref):
        o_ref[...] = product(a_ref, b_ref).astype(o_ref.dtype)

    def body(a_ref, b_ref, o_ref, acc_ref):
        kk = pl.program_id(2)

        @pl.when(kk == 0)
        def _():
            acc_ref[...] = product(a_ref, b_ref)

        @pl.when((kk > 0) & (kk < nk - 1))
        def _():
            acc_ref[...] += product(a_ref, b_ref)

        @pl.when(kk == nk - 1)
        def _():
            o_ref[...] = (acc_ref[...] + product(a_ref, b_ref)).astype(o_ref.dtype)

    if dims == "nn":
        a_spec = pl.BlockSpec((tm, tk), lambda j, i, kk: (i, kk))
        b_spec = pl.BlockSpec((tk, tn), lambda j, i, kk: (kk, j))
    elif dims == "nt":
        a_spec = pl.BlockSpec((tm, tk), lambda j, i, kk: (i, kk))
        b_spec = pl.BlockSpec((tn, tk), lambda j, i, kk: (j, kk))
    else:
        a_spec = pl.BlockSpec((tk, tm), lambda j, i, kk: (kk, i))
        b_spec = pl.BlockSpec((tk, tn), lambda j, i, kk: (kk, j))
    return pl.pallas_call(
        body_single if nk == 1 else body, name=name, grid=(n // tn, m // tm, nk), in_specs=[a_spec, b_spec],
        out_specs=pl.BlockSpec((tm, tn), lambda j, i, kk: (i, j)), out_shape=_sds((m, n), out_dtype),
        scratch_shapes=[] if nk == 1 else [pltpu.VMEM((tm, tn), F32)],
        compiler_params=_params(("parallel", "parallel", "arbitrary")))(a, b)


def _rows(name, body, n_tiles, ins, outs):
    in_specs = [pl.BlockSpec(blk, imap) for (_, blk, imap) in ins]
    out_specs = [pl.BlockSpec(blk, imap) for (_, _, blk, imap) in outs]
    out_shape = [_sds(shape, dt) for (shape, dt, _, _) in outs]
    res = pl.pallas_call(body, name=name, grid=(n_tiles,), in_specs=in_specs, out_specs=out_specs,
                         out_shape=out_shape, compiler_params=_params(("arbitrary",)))(*[a for (a, _, _) in ins])
    return res


def _row_in(arr, tile, width=None, col=0, x_only_offset=None):
    width = arr.shape[1] if width is None else width
    if x_only_offset is None:
        return (arr, (tile, width), lambda i: (i, col))
    return (arr, (tile, width), lambda i: (jnp.maximum(i - x_only_offset, 0), col))


def _vec_in(arr, idx_fn):
    return (arr, (1, 1, arr.shape[2]), lambda i: (idx_fn(i), 0, 0))


def _rms(h):
    return lax.rsqrt(jnp.mean(h * h, axis=-1, keepdims=True) + NORM_EPS)


def _sigmoid(z):
    return 1.0 / (1.0 + jnp.exp(-z))


def _ada_pre_fwd(h, g6, mods, gi, mi, nct, tile, name):
    r, d = h.shape
    sel = lambda i: jnp.where(i >= nct, 1, 0)

    def body(h_ref, g_ref, sh_ref, sc_ref, u_ref):
        hh = h_ref[...]
        n = hh * _rms(hh) * g_ref[0]
        u_ref[...] = (n * (1.0 + sc_ref[0]) + sh_ref[0]).astype(u_ref.dtype)

    (u,) = _rows(name, body, r // tile,
                 [_row_in(h, tile), _vec_in(g6, lambda i: gi), _vec_in(mods, lambda i: sel(i) * 9 + 3 * mi),
                  _vec_in(mods, lambda i: sel(i) * 9 + 3 * mi + 1)],
                 [((r, d), BF16, (tile, d), lambda i: (i, 0))])
    return u


def _ada_pre_bwd(h, du, dres, g6, mods, gi, mi, nct, nsel, tile, name, dres_x_only=False):
    r, d = h.shape
    sel = lambda i: jnp.where(i >= nct, 1, 0) if nsel == 2 else 0
    msel = lambda i: jnp.where(i >= nct, 1, 0)
    off = nct if dres_x_only else None

    def body(h_ref, du_ref, dr_ref, g_ref, sc_ref, dh_ref, dg_ref, dsh_ref, dsc_ref):
        i = pl.program_id(0)
        hh = h_ref[...]
        rr = _rms(hh)
        g = g_ref[0]
        hn = hh * rr
        n = hn * g
        du_ = du_ref[...].astype(F32)
        dn = du_ * (1.0 + sc_ref[0])

        @pl.when(i == 0)
        def _():
            dg_ref[...] = jnp.zeros_like(dg_ref)

        @pl.when((i == 0) | (i == nct))
        def _():
            dsh_ref[...] = jnp.zeros_like(dsh_ref)
            dsc_ref[...] = jnp.zeros_like(dsc_ref)

        dg_ref[0] += jnp.sum(dn * hn, axis=0, keepdims=True)
        dsh_ref[0] += jnp.sum(du_, axis=0, keepdims=True)
        dsc_ref[0] += jnp.sum(du_ * n, axis=0, keepdims=True)
        t = dn * g
        dh = rr * t - hn * (rr * jnp.mean(t * hn, axis=-1, keepdims=True))
        if dres_x_only:
            dh_ref[...] = dh + jnp.where(i >= nct, dr_ref[...], 0.0)
        else:
            dh_ref[...] = dh + dr_ref[...]

    dh, dg, dsh, dsc = _rows(
        name, body, r // tile,
        [_row_in(h, tile), _row_in(du, tile), _row_in(dres, tile, x_only_offset=off), _vec_in(g6, lambda i: gi),
         _vec_in(mods, lambda i: msel(i) * 9 + 3 * mi + 1)],
        [((r, d), F32, (tile, d), lambda i: (i, 0)), ((1, 1, d), F32, (1, 1, d), lambda i: (0, 0, 0)),
         ((nsel, 1, d), F32, (1, 1, d), lambda i: (sel(i), 0, 0)),
         ((nsel, 1, d), F32, (1, 1, d), lambda i: (sel(i), 0, 0))])
    return dh, dg, dsh, dsc


def _ada_post_fwd(h, o, g6, mods, gi, mi, res_w, nct, tile, name, h_x_only=False):
    r, d = o.shape
    sel = lambda i: jnp.where(i >= nct, 1, 0)

    def body(h_ref, o_ref, g_ref, gt_ref, y_ref):
        oo = o_ref[...]
        n = oo * _rms(oo) * g_ref[0]
        y_ref[...] = h_ref[...] + res_w * gt_ref[0] * n

    (y,) = _rows(name, body, r // tile,
                 [_row_in(h, tile), _row_in(o, tile), _vec_in(g6, lambda i: gi),
                  _vec_in(mods, lambda i: sel(i) * 9 + 3 * mi + 2)],
                 [((r, d), F32, (tile, d), lambda i: (i, 0))])
    return y


def _ada_post_bwd(dy, o, g6, mods, gi, mi, res_w, nct, nsel, tile, name):
    r, d = o.shape
    sel = lambda i: jnp.where(i >= nct, 1, 0) if nsel == 2 else 0
    msel = lambda i: jnp.where(i >= nct, 1, 0)

    def body(dy_ref, o_ref, g_ref, gt_ref, do_ref, dg_ref, dgt_ref):
        i = pl.program_id(0)
        oo = o_ref[...]
        rr = _rms(oo)
        g = g_ref[0]
        on = oo * rr
        dy_ = dy_ref[...] * res_w

        @pl.when(i == 0)
        def _():
            dg_ref[...] = jnp.zeros_like(dg_ref)

        @pl.when((i == 0) | (i == nct))
        def _():
            dgt_ref[...] = jnp.zeros_like(dgt_ref)

        dgt_ref[0] += jnp.sum(dy_ * (on * g), axis=0, keepdims=True)
        dn = dy_ * gt_ref[0]
        dg_ref[0] += jnp.sum(dn * on, axis=0, keepdims=True)
        t = dn * g
        do_ref[...] = (rr * t - on * (rr * jnp.mean(t * on, axis=-1, keepdims=True))).astype(do_ref.dtype)

    do, dg, dgt = _rows(
        name, body, r // tile,
        [_row_in(dy, tile), _row_in(o, tile), _vec_in(g6, lambda i: gi),
         _vec_in(mods, lambda i: msel(i) * 9 + 3 * mi + 2)],
        [((r, d), BF16, (tile, d), lambda i: (i, 0)), ((1, 1, d), F32, (1, 1, d), lambda i: (0, 0, 0)),
         ((nsel, 1, d), F32, (1, 1, d), lambda i: (sel(i), 0, 0))])
    return do, dg, dgt


def _swiglu_fwd(h, tile, name):
    r, w2 = h.shape
    f = w2 // 2

    def body(h_ref, a_ref):
        gt = h_ref[:, :f].astype(F32)
        up = h_ref[:, f:].astype(F32)
        a_ref[...] = (gt * _sigmoid(gt) * up).astype(a_ref.dtype)

    (a,) = _rows(name, body, r // tile, [_row_in(h, tile)], [((r, f), BF16, (tile, f), lambda i: (i, 0))])
    return a


def _swiglu_bwd(h, da, tile, name):
    r, w2 = h.shape
    f = w2 // 2

    def body(h_ref, da_ref, dh_ref):
        gt = h_ref[:, :f].astype(F32)
        up = h_ref[:, f:].astype(F32)
        d = da_ref[...].astype(F32)
        sg = _sigmoid(gt)
        dh_ref[:, :f] = (d * up * (sg * (1.0 + gt * (1.0 - sg)))).astype(dh_ref.dtype)
        dh_ref[:, f:] = (d * gt * sg).astype(dh_ref.dtype)

    (dh,) = _rows(name, body, r // tile, [_row_in(h, tile), _row_in(da, tile)],
                  [((r, w2), BF16, (tile, w2), lambda i: (i, 0))])
    return dh


def _gelu_parts(y):
    c0 = math.sqrt(2.0 / math.pi)
    inner = c0 * (y + 0.044715 * y * y * y)
    th = jnp.tanh(inner)
    return th, c0 * (1.0 + 3 * 0.044715 * y * y)


def _ssm_out_fwd(y0, y1, hm, dskip, nct, tile, name):
    t_rows, s = y0.shape

    def body(y0_ref, y1_ref, u_ref, d_ref, a_ref):
        y = y0_ref[...] + y1_ref[...] + d_ref[0] * u_ref[...]
        th, _ = _gelu_parts(y)
        a_ref[...] = (0.5 * y * (1.0 + th)).astype(a_ref.dtype)

    (a,) = _rows(name, body, t_rows // tile,
                 [_row_in(y0, tile), _row_in(y1, tile), (hm, (tile, s), lambda i: (i + nct, 0)),
                  _vec_in(dskip, lambda i: 0)],
                 [((t_rows, s), BF16, (tile, s), lambda i: (i, 0))])
    return a


def _ssm_out_bwd(y0, y1, hm, dskip, da, nct, tile, name):
    t_rows, s = y0.shape

    def body(y0_ref, y1_ref, u_ref, d_ref, da_ref, dy_ref, du_ref, dd_ref):
        i = pl.program_id(0)
        u = u_ref[...]
        y = y0_ref[...] + y1_ref[...] + d_ref[0] * u
        th, dinner = _gelu_parts(y)
        dy = da_ref[...] * (0.5 * (1.0 + th) + 0.5 * y * (1.0 - th * th) * dinner)
        dy_ref[...] = dy
        du_ref[...] = dy * d_ref[0]

        @pl.when(i == 0)
        def _():
            dd_ref[...] = jnp.zeros_like(dd_ref)

        dd_ref[0] += jnp.sum(dy * u, axis=0, keepdims=True)

    dy, du, dd = _rows(name, body, t_rows // tile,
                       [_row_in(y0, tile), _row_in(y1, tile), (hm, (tile, s), lambda i: (i + nct, 0)),
                        _vec_in(dskip, lambda i: 0), _row_in(da, tile)],
                       [((t_rows, s), F32, (tile, s), lambda i: (i, 0)), ((t_rows, s), F32, (tile, s), lambda i: (i, 0)),
                        ((1, 1, s), F32, (1, 1, s), lambda i: (0, 0, 0))])
    return dy, du, dd


def _col_pieces(arr, off, width, tile, nct, unit=None):
    pw = math.gcd(off, width if unit is None else unit)
    specs = [(arr, (tile, pw), functools.partial(lambda i, cb: (i + nct, cb), cb=off // pw + p))
             for p in range(width // pw)]
    return specs, pw


def _ret_gate_fwd(o0, o1, hm, g_off, heads, dv, nct, tile, name):
    t_rows, w = o0.shape
    g_specs, pw = _col_pieces(hm, g_off, w, tile, nct)
    ng = len(g_specs)

    def body(o0_ref, o1_ref, *refs):
        g_refs, r_ref = refs[:ng], refs[ng]
        for hd in range(heads):
            cs = slice(hd * dv, (hd + 1) * dv)
            o = o0_ref[:, cs] + o1_ref[:, cs]
            lo = (hd * dv) % pw
            g = g_refs[(hd * dv) // pw][:, lo:lo + dv]
            r_ref[:, cs] = (g * _sigmoid(g) * (o * _rms(o))).astype(r_ref.dtype)

    (ri,) = _rows(name, body, t_rows // tile, [_row_in(o0, tile), _row_in(o1, tile)] + g_specs,
                  [((t_rows, w), BF16, (tile, w), lambda i: (i, 0))])
    return ri


def _ret_gate_bwd(o0, o1, hm, g_off, dri, heads, dv, nct, tile, name):
    t_rows, w = o0.shape
    g_specs, pw = _col_pieces(hm, g_off, w, tile, nct)
    ng = len(g_specs)

    def body(o0_ref, o1_ref, d_ref, *refs):
        g_refs, do_ref, dg_ref = refs[:ng], refs[ng], refs[ng + 1]
        for hd in range(heads):
            cs = slice(hd * dv, (hd + 1) * dv)
            o = o0_ref[:, cs] + o1_ref[:, cs]
            lo = (hd * dv) % pw
            g = g_refs[(hd * dv) // pw][:, lo:lo + dv]
            d = d_ref[:, cs]
            rr = _rms(o)
            on = o * rr
            sg = _sigmoid(g)
            dg_ref[:, cs] = d * on * (sg * (1.0 + g * (1.0 - sg)))
            t = d * (g * sg)
            do_ref[:, cs] = rr * t - on * (rr * jnp.mean(t * on, axis=-1, keepdims=True))

    do, dg = _rows(name, body, t_rows // tile, [_row_in(o0, tile), _row_in(o1, tile), _row_in(dri, tile)] + g_specs,
                   [((t_rows, w), F32, (tile, w), lambda i: (i, 0)), ((t_rows, w), F32, (tile, w), lambda i: (i, 0))])
    return do, dg


def _merge_fwd(gab, rb, hm, gs_off, nct, tile, name):
    t_rows, d = rb.shape
    specs, pw = _col_pieces(hm, gs_off, 2 * d, tile, nct, unit=d)
    npc = d // pw

    def body(gab_ref, rb_ref, *refs):
        gs_refs, gr_refs, m_ref = refs[:npc], refs[npc:2 * npc], refs[2 * npc]
        for p in range(npc):
            cs = slice(p * pw, (p + 1) * pw)
            ga = gab_ref[:, cs]
            gb = gab_ref[:, d + p * pw:d + (p + 1) * pw]
            m_ref[:, cs] = (_sigmoid(gs_refs[p][...]) * (ga * _sigmoid(gb))
                            + _sigmoid(gr_refs[p][...]) * rb_ref[:, cs]).astype(m_ref.dtype)

    (mg,) = _rows(name, body, t_rows // tile, [_row_in(gab, tile), _row_in(rb, tile)] + specs,
                  [((t_rows, d), BF16, (tile, d), lambda i: (i, 0))])
    return mg


def _merge_bwd(gab, rb, hm, gs_off, dm, nct, tile, name):
    t_rows, d = rb.shape
    specs, pw = _col_pieces(hm, gs_off, 2 * d, tile, nct, unit=d)
    npc = d // pw

    def body(gab_ref, rb_ref, dm_ref, *refs):
        gs_refs, gr_refs = refs[:npc], refs[npc:2 * npc]
        dgab_ref, drb_ref, dgs_ref, dgr_ref = refs[2 * npc:]
        for p in range(npc):
            cs = slice(p * pw, (p + 1) * pw)
            cs2 = slice(d + p * pw, d + (p + 1) * pw)
            ga = gab_ref[:, cs]
            gb = gab_ref[:, cs2]
            dmm = dm_ref[:, cs]
            ss = _sigmoid(gs_refs[p][...])
            sr = _sigmoid(gr_refs[p][...])
            sb = _sigmoid(gb)
            dbr = dmm * ss
            dgab_ref[:, cs] = (dbr * sb).astype(dgab_ref.dtype)
            dgab_ref[:, cs2] = (dbr * ga * sb * (1.0 - sb)).astype(dgab_ref.dtype)
            drb_ref[:, cs] = (dmm * sr).astype(drb_ref.dtype)
            dgs_ref[:, cs] = dmm * (ga * sb) * ss * (1.0 - ss)
            dgr_ref[:, cs] = dmm * rb_ref[:, cs] * sr * (1.0 - sr)

    return _rows(name, body, t_rows // tile, [_row_in(gab, tile), _row_in(rb, tile), _row_in(dm, tile)] + specs,
                 [((t_rows, 2 * d), BF16, (tile, 2 * d), lambda i: (i, 0)), ((t_rows, d), BF16, (tile, d), lambda i: (i, 0)),
                  ((t_rows, d), F32, (tile, d), lambda i: (i, 0)), ((t_rows, d), F32, (tile, d), lambda i: (i, 0))])


def _assemble_dhm(dus, dq0, dq1, dk0, dk1, dv0, dv1, dg, dgs, dgr, nct, tile, name):
    r, s = dus.shape
    qk = dq0.shape[1]
    vw = dv0.shape[1]
    d = dgs.shape[1]
    mi = s + 2 * qk + 2 * vw + 2 * d
    c_q, c_k, c_v, c_g, c_gs, c_gr = s, s + qk, s + 2 * qk, s + 2 * qk + vw, s + 2 * qk + 2 * vw, s + 2 * qk + 2 * vw + d

    def body(dus_ref, dq0_ref, dq1_ref, dk0_ref, dk1_ref, dv0_ref, dv1_ref, dg_ref, dgs_ref, dgr_ref, o_ref):
        i = pl.program_id(0)
        lat = i >= nct
        o_ref[:, :s] = dus_ref[...].astype(o_ref.dtype)
        o_ref[:, c_q:c_k] = (dq0_ref[...] + dq1_ref[...]).astype(o_ref.dtype)
        o_ref[:, c_k:c_v] = (dk0_ref[...] + dk1_ref[...]).astype(o_ref.dtype)
        o_ref[:, c_v:c_g] = (dv0_ref[...] + dv1_ref[...]).astype(o_ref.dtype)
        o_ref[:, c_g:c_gs] = jnp.where(lat, dg_ref[...], 0.0).astype(o_ref.dtype)
        o_ref[:, c_gs:c_gr] = jnp.where(lat, dgs_ref[...], 0.0).astype(o_ref.dtype)
        o_ref[:, c_gr:] = jnp.where(lat, dgr_ref[...], 0.0).astype(o_ref.dtype)

    (out,) = _rows(name, body, r // tile,
                   [_row_in(dus, tile), _row_in(dq0, tile), _row_in(dq1, tile), _row_in(dk0, tile), _row_in(dk1, tile),
                    _row_in(dv0, tile), _row_in(dv1, tile), _row_in(dg, tile, x_only_offset=nct),
                    _row_in(dgs, tile, x_only_offset=nct), _row_in(dgr, tile, x_only_offset=nct)],
                   [((r, mi), BF16, (tile, mi), lambda i: (i, 0))])
    return out


def _loss_grad(y, target, tile, name):
    t_rows, d = y.shape

    def body(y_ref, t_ref, dy_ref, l_ref):
        i = pl.program_id(0)
        e = y_ref[...] - t_ref[...]
        dy_ref[...] = e * (1.0 / d)

        @pl.when(i == 0)
        def _():
            l_ref[...] = jnp.zeros_like(l_ref)

        l_ref[0] += jnp.sum(e * e, axis=0, keepdims=True)

    return _rows(name, body, t_rows // tile, [_row_in(y, tile), _row_in(target, tile)],
                 [((t_rows, d), F32, (tile, d), lambda i: (i, 0)), ((1, 1, d), F32, (1, 1, d), lambda i: (0, 0, 0))])


def _silu_rows(v, name):
    def body(v_ref, o_ref):
        z = v_ref[...]
        o_ref[...] = z * _sigmoid(z)

    (o,) = _rows(name, body, 1, [_row_in(v, v.shape[0])], [(v.shape, F32, v.shape, lambda i: (0, 0))])
    return o


def _silu_grad_rows(v, dv, name):
    def body(v_ref, d_ref, o_ref):
        z = v_ref[...]
        sg = _sigmoid(z)
        o_ref[...] = d_ref[...] * (sg * (1.0 + z * (1.0 - sg)))

    (o,) = _rows(name, body, 1, [_row_in(v, v.shape[0]), _row_in(dv, v.shape[0])],
                 [(v.shape, F32, v.shape, lambda i: (0, 0))])
    return o


def _sum_leading(g8, name):
    n, r, c = g8.shape
    tile = _tile(r, 256, SUBLANE)

    def body(g_ref, o_ref):
        acc = g_ref[0]
        for j in range(1, n):
            acc = acc + g_ref[j]
        o_ref[...] = acc

    (o,) = _rows(name, body, r // tile, [(g8, (n, tile, c), lambda i: (0, i, 0))],
                 [((r, c), F32, (tile, c), lambda i: (i, 0))])
    return o


def _pair_sum(g, recv, axis, name):
    n, br, bc = recv.shape
    tile = _tile(br, 256, 16)
    nrt = br // tile
    core = lax.axis_index("c").astype(jnp.int32).reshape(1)

    def body(c_ref, g_ref, r_ref, o_ref):
        o_ref[0] = (g_ref[...].astype(F32) + r_ref[0].astype(F32)).astype(o_ref.dtype)

    if axis == 1:
        g_spec = pl.BlockSpec((tile, bc), lambda q, i, c_ref: (i, 2 * q + c_ref[0]))
    else:
        g_spec = pl.BlockSpec((tile, bc), lambda q, i, c_ref: ((2 * q + c_ref[0]) * nrt + i, 0))
    slot = pl.BlockSpec((1, tile, bc), lambda q, i, c_ref: (q, i, 0))
    return pl.pallas_call(
        body, name=name, out_shape=_sds((n, br, bc), recv.dtype),
        grid_spec=pltpu.PrefetchScalarGridSpec(num_scalar_prefetch=1, grid=(n, nrt), in_specs=[g_spec, slot],
                                               out_specs=slot),
        compiler_params=_params(("arbitrary", "arbitrary")))(core, g, recv)


def _adam_math(w, m, v, g):
    c1 = 1.0 / (1.0 - ADAM_B1 ** ADAM_STEP)
    c2 = 1.0 / (1.0 - ADAM_B2 ** ADAM_STEP)
    mm = ADAM_B1 * m + (1.0 - ADAM_B1) * g
    vv = ADAM_B2 * v + (1.0 - ADAM_B2) * (g * g)
    return -ADAM_LR * ((mm * c1) / (jnp.sqrt(vv * c2) + ADAM_EPS) + ADAM_WD * w), mm, vv


def _adamw(w, m, v, gparts, name):
    r, c = w.shape
    n = gparts.shape[0]
    tile = _tile(r, 256, 16)

    def body(w_ref, m_ref, v_ref, g_ref, go_ref, d_ref, mo_ref, vo_ref):
        g = g_ref[0].astype(F32)
        for j in range(1, n):
            g = g + g_ref[j].astype(F32)
        go_ref[...] = g
        d_ref[...], mo_ref[...], vo_ref[...] = _adam_math(w_ref[...], m_ref[...], v_ref[...], g)

    rs = lambda arr: _row_in(arr, tile)
    out = ((r, c), F32, (tile, c), lambda i: (i, 0))
    return _rows(name, body, r // tile, [rs(w), rs(m), rs(v), (gparts, (n, tile, c), lambda i: (0, i, 0))],
                 [out, out, out, out])


def _adamw_scattered(w, m, v, layer, p, recv, name, filled=None):
    nl, r, c = w.shape
    n = recv.shape[0]
    tile = _tile(r, 256, 16)
    chip = (2 * lax.axis_index("x") + lax.axis_index("y")).astype(jnp.int32).reshape(1)
    n_prev = 0 if filled is None else len(filled)

    def body(q_ref, w_ref, m_ref, v_ref, p_ref, g_ref, *rest):
        go_ref, d_ref, mo_ref, vo_ref = rest[n_prev:]
        g = p_ref[0].astype(F32)
        for j in range(n):
            g = g + g_ref[j].astype(F32)
        go_ref[0] = g
        d_ref[0], mo_ref[0], vo_ref[0] = _adam_math(w_ref[0], m_ref[0], v_ref[0], g)

    slab = pl.BlockSpec((1, tile, c), lambda i, q_ref: (layer, i, 0))
    anywhere = pl.BlockSpec(memory_space=pl.ANY)
    out = _sds((nl, r, c), F32)
    prev = [] if filled is None else list(filled)
    return pl.pallas_call(
        body, name=name, out_shape=[out, out, out, out],
        grid_spec=pltpu.PrefetchScalarGridSpec(
            num_scalar_prefetch=1, grid=(r // tile,),
            in_specs=[slab, slab, slab, pl.BlockSpec((1, tile, c), lambda i, q_ref: (q_ref[0], i, 0)),
                      pl.BlockSpec((n, tile, c), lambda i, q_ref: (0, i, 0))] + [anywhere] * n_prev,
            out_specs=[slab, slab, slab, slab]),
        input_output_aliases={6 + j: j for j in range(n_prev)},
        compiler_params=_params(("arbitrary",)))(chip, w, m, v, p, recv, *prev)


def _cmul(ar, ai, br, bi):
    return ar * br - ai * bi, ar * bi + ai * br


def _cpow(ar, ai, n):
    pr, pi = jnp.ones_like(ar), jnp.zeros_like(ar)
    br, bi = ar, ai
    while n:
        if n & 1:
            pr, pi = _cmul(pr, pi, br, bi)
        n >>= 1
        if n:
            br, bi = _cmul(br, bi, br, bi)
    return pr, pi


def _s5_scan_into(x_ref, ar1, ai1, ns, fin_ref, hin_ref, reverse, paired=None):
    st = ar1.shape[1]
    ar = jnp.broadcast_to(ar1, (N_SEG, st))
    ai = jnp.broadcast_to(ai1, (N_SEG, st))
    zero = jnp.zeros((N_SEG, st), F32)

    def slab(k):
        if isinstance(k, int):
            return pl.ds(k * N_SEG, N_SEG)
        return pl.ds(pl.multiple_of(k * N_SEG, N_SEG), N_SEG)

    def pass1(j, carry):
        hr, hi = carry
        k = ns - 1 - j if reverse else j
        nr, ni = _cmul(ar, ai, hr, hi)
        return nr + x_ref[slab(k), :st], ni + x_ref[slab(k), st:]

    fr, fi = lax.fori_loop(0, ns, pass1, (zero, zero))
    fin_ref[:, :st] = fr
    fin_ref[:, st:] = fi
    pr, pi = _cpow(ar1, ai1, ns)
    order = list(range(N_SEG - 1, -1, -1)) if reverse else list(range(N_SEG))
    hin_ref[order[0]:order[0] + 1, :] = jnp.zeros((1, 2 * st), F32)
    for a_, b_ in zip(order[:-1], order[1:]):
        cr, ci = _cmul(pr, pi, hin_ref[a_:a_ + 1, :st], hin_ref[a_:a_ + 1, st:])
        hin_ref[b_:b_ + 1, :st] = cr + fin_ref[a_:a_ + 1, :st]
        hin_ref[b_:b_ + 1, st:] = ci + fin_ref[a_:a_ + 1, st:]

    def step2(k, hr, hi):
        nr, ni = _cmul(ar, ai, hr, hi)
        nr = nr + x_ref[slab(k), :st]
        ni = ni + x_ref[slab(k), st:]
        x_ref[slab(k), :st] = nr
        x_ref[slab(k), st:] = ni
        return nr, ni

    if paired is None:
        def pass2(j, carry):
            return step2(ns - 1 - j if reverse else j, *carry)

        lax.fori_loop(0, ns, pass2, (hin_ref[:, :st], hin_ref[:, st:]))
        return None
    p_ref, p_edge_ref, shift = paired

    def pass2_paired(j, carry):
        hr, hi, acr, aci = carry
        k = ns - 1 - j if reverse else j
        nr, ni = step2(k, hr, hi)
        p_r, p_i = p_ref[slab(k + shift), :st], p_ref[slab(k + shift), st:]
        return nr, ni, acr + nr * p_r + ni * p_i, aci + ni * p_r - nr * p_i

    hr, hi, acr, aci = lax.fori_loop(0, ns - 1, pass2_paired, (hin_ref[:, :st], hin_ref[:, st:], zero, zero))
    nr, ni = step2(0 if reverse else ns - 1, hr, hi)
    p_r, p_i = p_edge_ref[:, :st], p_edge_ref[:, st:]
    return acr + nr * p_r + ni * p_i, aci + ni * p_r - nr * p_i


def _s5_specs(r, ch, st):
    u_spec = pl.BlockSpec((r, ch), lambda j: (0, j // 2))
    w_spec = pl.BlockSpec((1, ch, 2 * st), lambda j: (j, 0, 0))
    c_spec = pl.BlockSpec((1, 2 * st, ch), lambda j: (j, 0, 0))
    a_spec = pl.BlockSpec((1, 2, st), lambda j: (j, 0, 0))
    return u_spec, w_spec, c_spec, a_spec


def _s5_fwd(up, w, c, a, rev, name):
    r, s = up.shape
    nh, ch, st2 = w.shape
    st = st2 // 2
    ns = r // N_SEG
    nb = r // N_DEV
    u_spec, w_spec, c_spec, a_spec = _s5_specs(r, ch, st)

    def body(u_ref, w_ref, c_ref, a_ref, y_ref, x, fin, hin):
        j = pl.program_id(0)
        w_b = w_ref[0].astype(MXU_DTYPE)
        c_b = c_ref[0].astype(MXU_DTYPE)
        for rb in range(N_DEV):
            rows = slice(rb * nb, (rb + 1) * nb)
            x[rows, :] = jnp.dot(u_ref[rows, :].astype(MXU_DTYPE), w_b, preferred_element_type=F32)
        _s5_scan_into(x, a_ref[0, 0:1, :], a_ref[0, 1:2, :], ns, fin, hin, rev)
        for rb in range(N_DEV):
            rows = slice(rb * nb, (rb + 1) * nb)
            yb = jnp.dot(x[rows, :].astype(MXU_DTYPE), c_b, preferred_element_type=F32)

            @pl.when(j % 2 == 0)
            def _():
                y_ref[rows, :] = yb

            @pl.when(j % 2 == 1)
            def _():
                y_ref[rows, :] += yb

    small = pltpu.VMEM((N_SEG, st2), F32)
    return pl.pallas_call(
        body, name=name, grid=(nh,), in_specs=[u_spec, w_spec, c_spec, a_spec],
        out_specs=pl.BlockSpec((r, ch), lambda j: (0, j // 2)), out_shape=_sds((r, s), F32),
        scratch_shapes=[pltpu.VMEM((r, st2), F32), small, small],
        compiler_params=_params(("arbitrary",)))(up, w, c, a)


def _s5_bwd(up, dyp, w, c, a, rev, name):
    r, s = up.shape
    nh, ch, st2 = w.shape
    st = st2 // 2
    ns = r // N_SEG
    nb = r // N_DEV
    u_spec, w_spec, c_spec, a_spec = _s5_specs(r, ch, st)
    nt = (((1,), (1,)), ((), ()))
    tn = (((0,), (0,)), ((), ()))

    def body(u_ref, dy_ref, w_ref, c_ref, a_ref, du_ref, dw_ref, dc_ref, da_ref, h, g, fin, sin_, ein):
        j = pl.program_id(0)
        w_b = w_ref[0].astype(MXU_DTYPE)
        c_b = c_ref[0].astype(MXU_DTYPE)
        for rb in range(N_DEV):
            rows = slice(rb * nb, (rb + 1) * nb)
            h[rows, :] = jnp.dot(u_ref[rows, :].astype(MXU_DTYPE), w_b, preferred_element_type=F32)
        ar1, ai1 = a_ref[0, 0:1, :], a_ref[0, 1:2, :]
        _s5_scan_into(h, ar1, ai1, ns, fin, sin_, rev)
        dc = jnp.zeros((st2, ch), F32)
        for rb in range(N_DEV):
            rows = slice(rb * nb, (rb + 1) * nb)
            dyb = dy_ref[rows, :].astype(MXU_DTYPE)
            g[rows, :] = lax.dot_general(dyb, c_b, nt, preferred_element_type=F32)
            dc += lax.dot_general(h[rows, :].astype(MXU_DTYPE), dyb, tn, preferred_element_type=F32)
        dc_ref[0] = dc
        acr, aci = _s5_scan_into(g, ar1, -ai1, ns, fin, ein, not rev, paired=(h, sin_, 1 if rev else -1))
        da_ref[0, 0:1, :] = jnp.sum(acr, axis=0, keepdims=True)
        da_ref[0, 1:2, :] = jnp.sum(aci, axis=0, keepdims=True)
        dw = jnp.zeros((ch, st2), F32)
        for rb in range(N_DEV):
            rows = slice(rb * nb, (rb + 1) * nb)
            gb = g[rows, :].astype(MXU_DTYPE)
            dub = lax.dot_general(gb, w_b, nt, preferred_element_type=F32)
            dw += lax.dot_general(u_ref[rows, :].astype(MXU_DTYPE), gb, tn, preferred_element_type=F32)

            @pl.when(j % 2 == 0)
            def _():
                du_ref[rows, :] = dub

            @pl.when(j % 2 == 1)
            def _():
                du_ref[rows, :] += dub

        dw_ref[0] = dw

    small = pltpu.VMEM((N_SEG, st2), F32)
    big = pltpu.VMEM((r, st2), F32)
    return pl.pallas_call(
        body, name=name, grid=(nh,), in_specs=[u_spec, u_spec, w_spec, c_spec, a_spec],
        out_specs=[pl.BlockSpec((r, ch), lambda j: (0, j // 2)), w_spec, c_spec, a_spec],
        out_shape=[_sds((r, s), F32), _sds(w.shape, F32), _sds(c.shape, F32), _sds(a.shape, F32)],
        scratch_shapes=[big, big, small, small, small],
        compiler_params=_params(("arbitrary",)))(up, dyp, w, c, a)


def _rope(t, cos, sin):
    quarter = t.shape[1] // 4
    lane = lax.broadcasted_iota(jnp.int32, t.shape, 1)
    first = (lane // quarter) % 2 == 0
    partner = jnp.where(first, pltpu.roll(t, t.shape[1] - quarter, 1), pltpu.roll(t, quarter, 1))
    return t * cos + partner * sin


def _rope_t(d, cos, sin):
    quarter = d.shape[1] // 4
    ds_ = d * sin
    lane = lax.broadcasted_iota(jnp.int32, d.shape, 1)
    first = (lane // quarter) % 2 == 0
    partner = jnp.where(first, pltpu.roll(ds_, d.shape[1] - quarter, 1), pltpu.roll(ds_, quarter, 1))
    return d * cos + partner


def _chunk_of_step(s, nch, ncc, rev):
    if not rev:
        return s
    return jnp.where(s < ncc, ncc - 1 - s, nch + ncc - 1 - s)


def _heads_per_step(heads, dk, dv, q_off):
    v_off = q_off + 2 * heads * dk
    for hpg in range(heads, 0, -1):
        if heads % hpg == 0 and q_off % (hpg * dk) == 0:
            piece = math.gcd(v_off, hpg * dv)
            if piece % dv == 0:
                return hpg, piece
    return 1, dv


def _v_specs(hpg, dv, piece, v_off, ch, chunk_of):
    n_pieces = hpg * dv // piece
    return [pl.BlockSpec((ch, piece), functools.partial(
        lambda h, s, p: (chunk_of(s), v_off // piece + h * n_pieces + p), p=p)) for p in range(n_pieces)]


def _v_of_head(v_refs, hl, dv, piece):
    lo = (hl * dv) % piece
    return v_refs[(hl * dv) // piece][:, lo:lo + dv]


def _ret_fwd(hm, cos, sin, decay, wend, win, gch, heads, dk, dv, q_off, ncc, rev, name):
    r = hm.shape[0]
    ch = RET_CHUNK
    nch = r // ch
    t_rows = r - ncc * ch
    hpg, piece = _heads_per_step(heads, dk, dv, q_off)
    qb, kb = q_off // (hpg * dk), (q_off + heads * dk) // (hpg * dk)
    q_scale = dk ** -0.5
    nt = (((1,), (1,)), ((), ()))
    tn = (((0,), (0,)), ((), ()))
    cof = lambda s: _chunk_of_step(s, nch, ncc, rev)
    v_specs = _v_specs(hpg, dv, piece, q_off + 2 * heads * dk, ch, cof)
    nv = len(v_specs)

    def body(q_ref, k_ref, *refs):
        v_refs = refs[:nv]
        cos_ref, sin_ref, dec_ref, we_ref, wi_ref, g_ref, o_ref, sin_out, st = refs[nv:]
        s = pl.program_id(1)

        @pl.when(s == 0)
        def _():
            st[...] = jnp.zeros_like(st)

        cos_, sin_ = cos_ref[...], sin_ref[...]
        for hl in range(hpg):
            ks, vs = slice(hl * dk, (hl + 1) * dk), slice(hl * dv, (hl + 1) * dv)
            q = _rope(q_ref[:, ks], cos_, sin_) * q_scale
            k = _rope(k_ref[:, ks], cos_, sin_)
            v = _v_of_head(v_refs, hl, dv, piece).astype(MXU_DTYPE)
            s_cur = st[hl]
            sin_out[hl, 0] = s_cur
            kw = (k * we_ref[hl]).astype(MXU_DTYPE)
            qw = (q * wi_ref[hl]).astype(MXU_DTYPE)
            scores = lax.dot_general(q.astype(MXU_DTYPE), k.astype(MXU_DTYPE), nt,
                                     preferred_element_type=F32) * dec_ref[hl]
            o_ref[:, vs] = (jnp.dot(scores.astype(MXU_DTYPE), v, preferred_element_type=F32)
                            + jnp.dot(qw, s_cur.astype(MXU_DTYPE), preferred_element_type=F32))
            st[hl] = g_ref[hl] * s_cur + lax.dot_general(kw, v, tn, preferred_element_type=F32)

    tab = lambda w: pl.BlockSpec((hpg, ch, w), lambda h, s: (h, 0, 0))
    return pl.pallas_call(
        body, name=name, grid=(heads // hpg, nch),
        in_specs=[pl.BlockSpec((ch, hpg * dk), lambda h, s: (cof(s), qb + h)),
                  pl.BlockSpec((ch, hpg * dk), lambda h, s: (cof(s), kb + h))] + v_specs +
                 [pl.BlockSpec((ch, dk), lambda h, s: (cof(s), 0)),
                  pl.BlockSpec((ch, dk), lambda h, s: (cof(s), 0)),
                  tab(ch), tab(dk), tab(dk), tab(dv)],
        out_specs=[pl.BlockSpec((ch, hpg * dv), lambda h, s: (jnp.maximum(cof(s) - ncc, 0) if not rev
                                                               else jnp.where(s < ncc, nch - ncc - 1, cof(s) - ncc), h)),
                   pl.BlockSpec((hpg, 1, dk, dv), lambda h, s: (h, s, 0, 0))],
        out_shape=[_sds((t_rows, heads * dv), F32), _sds((heads, nch, dk, dv), F32)],
        scratch_shapes=[pltpu.VMEM((hpg, dk, dv), F32)],
        compiler_params=_params(("parallel", "arbitrary")))(hm, hm, *([hm] * nv), cos, sin, decay, wend, win, gch)


def _ret_bwd(hm, cos, sin, decay, wend, win, gch, s_in, do, heads, dk, dv, q_off, ncc, rev, name):
    r = hm.shape[0]
    ch = RET_CHUNK
    nch = r // ch
    hpg, piece = _heads_per_step(heads, dk, dv, q_off)
    qb, kb = q_off // (hpg * dk), (q_off + heads * dk) // (hpg * dk)
    q_scale = dk ** -0.5
    nt = (((1,), (1,)), ((), ()))
    tn = (((0,), (0,)), ((), ()))
    cof = lambda rr: _chunk_of_step(nch - 1 - rr, nch, ncc, rev)
    v_specs = _v_specs(hpg, dv, piece, q_off + 2 * heads * dk, ch, cof)
    nv = len(v_specs)

    def body(q_ref, k_ref, *refs):
        v_refs = refs[:nv]
        (cos_ref, sin_ref, dec_ref, we_ref, wi_ref, g_ref, sin_ref2, do_ref,
         dq_ref, dk_ref, dv_ref, ddec_ref, dwe_ref, dwi_ref, dg_ref, dst) = refs[nv:]
        rr = pl.program_id(1)
        n = cof(rr)

        @pl.when(rr == 0)
        def _():
            dst[...] = jnp.zeros_like(dst)
            ddec_ref[...] = jnp.zeros_like(ddec_ref)
            dwe_ref[...] = jnp.zeros_like(dwe_ref)
            dwi_ref[...] = jnp.zeros_like(dwi_ref)
            dg_ref[...] = jnp.zeros_like(dg_ref)

        cos_, sin_ = cos_ref[...], sin_ref[...]
        for hl in range(hpg):
            ks, vs = slice(hl * dk, (hl + 1) * dk), slice(hl * dv, (hl + 1) * dv)
            q = _rope(q_ref[:, ks], cos_, sin_) * q_scale
            k = _rope(k_ref[:, ks], cos_, sin_)
            v = _v_of_head(v_refs, hl, dv, piece).astype(MXU_DTYPE)
            qb_, kb_ = q.astype(MXU_DTYPE), k.astype(MXU_DTYPE)
            kw = (k * we_ref[hl]).astype(MXU_DTYPE)
            qw = (q * wi_ref[hl]).astype(MXU_DTYPE)
            sraw = lax.dot_general(qb_, kb_, nt, preferred_element_type=F32)
            scores = (sraw * dec_ref[hl]).astype(MXU_DTYPE)
            d_o = jnp.where(n >= ncc, do_ref[:, vs], 0.0).astype(MXU_DTYPE)
            s_n = sin_ref2[hl, 0]
            s_nb = s_n.astype(MXU_DTYPE)
            ds1 = dst[hl]
            ds1b = ds1.astype(MXU_DTYPE)
            dsc = lax.dot_general(d_o, v, nt, preferred_element_type=F32)
            dsr = (dsc * dec_ref[hl]).astype(MXU_DTYPE)
            ddec_ref[hl] += dsc * sraw
            t1 = lax.dot_general(d_o, s_nb, nt, preferred_element_type=F32)
            dq_r = jnp.dot(dsr, kb_, preferred_element_type=F32) + t1 * wi_ref[hl]
            dwi_ref[hl] += t1 * q
            t2 = lax.dot_general(v, ds1b, nt, preferred_element_type=F32)
            dk_r = lax.dot_general(dsr, qb_, tn, preferred_element_type=F32) + t2 * we_ref[hl]
            dwe_ref[hl] += t2 * k
            dv_ref[:, vs] = (lax.dot_general(scores, d_o, tn, preferred_element_type=F32)
                             + jnp.dot(kw, ds1b, preferred_element_type=F32))
            dg_ref[hl] += ds1 * s_n
            dst[hl] = g_ref[hl] * ds1 + lax.dot_general(qw, d_o, tn, preferred_element_type=F32)
            dq_ref[:, ks] = _rope_t(dq_r, cos_, sin_) * q_scale
            dk_ref[:, ks] = _rope_t(dk_r, cos_, sin_)

    tab = lambda w: pl.BlockSpec((hpg, ch, w), lambda h, rr: (h, 0, 0))
    return pl.pallas_call(
        body, name=name, grid=(heads // hpg, nch),
        in_specs=[pl.BlockSpec((ch, hpg * dk), lambda h, rr: (cof(rr), qb + h)),
                  pl.BlockSpec((ch, hpg * dk), lambda h, rr: (cof(rr), kb + h))] + v_specs +
                 [pl.BlockSpec((ch, dk), lambda h, rr: (cof(rr), 0)),
                  pl.BlockSpec((ch, dk), lambda h, rr: (cof(rr), 0)),
                  tab(ch), tab(dk), tab(dk), tab(dv),
                  pl.BlockSpec((hpg, 1, dk, dv), lambda h, rr: (h, nch - 1 - rr, 0, 0)),
                  pl.BlockSpec((ch, hpg * dv), lambda h, rr: (jnp.maximum(cof(rr) - ncc, 0), h))],
        out_specs=[pl.BlockSpec((ch, hpg * dk), lambda h, rr: (cof(rr), h)),
                   pl.BlockSpec((ch, hpg * dk), lambda h, rr: (cof(rr), h)),
                   pl.BlockSpec((ch, hpg * dv), lambda h, rr: (cof(rr), h)),
                   tab(ch), tab(dk), tab(dk), tab(dv)],
        out_shape=[_sds((r, heads * dk), F32), _sds((r, heads * dk), F32), _sds((r, heads * dv), F32),
                   _sds(decay.shape, F32), _sds(wend.shape, F32), _sds(win.shape, F32), _sds(gch.shape, F32)],
        scratch_shapes=[pltpu.VMEM((hpg, dk, dv), F32)],
        compiler_params=_params(("parallel", "arbitrary")))(hm, hm, *([hm] * nv), cos, sin, decay, wend, win, gch, s_in, do)


_HBM = pl.BlockSpec(memory_space=pltpu.HBM)
_MESH = pl.DeviceIdType.MESH
ALL_GATHER_COLLECTIVE_ID = 1
SIBLING_COLLECTIVE_ID = 2
CHIPS_COLLECTIVE_ID = 3


def _axis_slice(ref, axis, start, size):
    idx = [slice(None)] * len(ref.shape)
    idx[axis] = pl.ds(start, size)
    return ref.at[tuple(idx)]


def _sibling_and_chip_peers():
    x, y, c = lax.axis_index("x"), lax.axis_index("y"), lax.axis_index("c")
    return [(x, y, 1 - c), (1 - x, y, c), (x, 1 - y, c), (1 - x, 1 - y, c)]


def _launch_exchange(body, name, operand, out_shape, sems, peers_fn, collective_id, on_sequencer):
    if not on_sequencer:
        return pl.pallas_call(body, name=name, out_shape=out_shape, in_specs=[_HBM], out_specs=_HBM,
                              scratch_shapes=sems)(operand)

    def sequencer_body(in_ref, out_ref, *sem_refs):
        peers = peers_fn()
        barrier = pltpu.get_barrier_semaphore()
        for peer in peers:
            pl.semaphore_signal(barrier, inc=1, device_id=peer, device_id_type=_MESH)
        pl.semaphore_wait(barrier, len(peers))
        body(in_ref, out_ref, *sem_refs)

    return pl.kernel(sequencer_body, out_type=out_shape, name=name,
                     mesh=plsc.ScalarSubcoreMesh(axis_name="sequencer", num_cores=1), scratch_types=sems,
                     compiler_params=pltpu.CompilerParams(collective_id=collective_id))(operand)


def _all_gather(shard, axis, name, on_sequencer=False):
    m = shard.shape[axis]
    out_shape = list(shard.shape)
    out_shape[axis] = N_DEV * m

    def body(x_ref, out_ref, send_sems, recv_sems, local_sem):
        x, y, c = lax.axis_index("x"), lax.axis_index("y"), lax.axis_index("c")
        me, sibling = (x, y, c), (x, y, 1 - c)
        chips = [(1 - x, y), (x, 1 - y), (1 - x, 1 - y)]

        def block(px, py, pc):
            return _axis_slice(out_ref, axis, (4 * px + 2 * py + pc) * m, m)

        def copy(k, blk, to, src=None):
            return pltpu.make_async_remote_copy(
                src_ref=block(*blk) if src is None else src, dst_ref=block(*blk), send_sem=send_sems.at[k],
                recv_sem=recv_sems.at[k], device_id=to, device_id_type=_MESH)

        mine = pltpu.make_async_copy(x_ref, block(*me), local_sem)
        mine.start()
        first = [copy(0, me, sibling, src=x_ref)]
        first += [copy(1 + j, me, (*chip, c), src=x_ref) for j, chip in enumerate(chips)]
        for cp in first:
            cp.start()
        passed = [copy(4 + j, (*chip, c), sibling) for j, chip in enumerate(chips)]
        for j, chip in enumerate(chips):
            copy(1 + j, (*chip, c), me).wait_recv()
            passed[j].start()
        copy(0, sibling, me).wait_recv()
        for j, chip in enumerate(chips):
            copy(4 + j, (*chip, 1 - c), me).wait_recv()
        for cp in first + passed:
            cp.wait_send()
        mine.wait()

    return _launch_exchange(
        body, name, shard, _sds(out_shape, shard.dtype),
        [pltpu.SemaphoreType.DMA((7,)), pltpu.SemaphoreType.DMA((7,)), pltpu.SemaphoreType.DMA(())],
        _sibling_and_chip_peers, ALL_GATHER_COLLECTIVE_ID, on_sequencer)


def _rs_sibling(g, axis, name, on_sequencer=False):
    m = g.shape[axis] // N_DEV
    blk_shape = list(g.shape)
    blk_shape[axis] = m
    n_chips = N_DEV // 2

    def body(g_ref, recv_ref, send_sems, recv_sems):
        x, y, c = lax.axis_index("x"), lax.axis_index("y"), lax.axis_index("c")
        sibling = (x, y, 1 - c)
        send = [pltpu.make_async_remote_copy(
            src_ref=_axis_slice(g_ref, axis, (2 * q + 1 - c) * m, m), dst_ref=recv_ref.at[q],
            send_sem=send_sems.at[q], recv_sem=recv_sems.at[q], device_id=sibling, device_id_type=_MESH)
            for q in range(n_chips)]
        for cp in send:
            cp.start()
        for cp in send:
            cp.wait_recv()
        for cp in send:
            cp.wait_send()

    return _launch_exchange(
        body, name, g, _sds([n_chips] + blk_shape, g.dtype),
        [pltpu.SemaphoreType.DMA((n_chips,)), pltpu.SemaphoreType.DMA((n_chips,))],
        lambda: _sibling_and_chip_peers()[:1], SIBLING_COLLECTIVE_ID, on_sequencer)


def _rs_chips(p, name, on_sequencer=False):
    n_peers = p.shape[0] - 1

    def body(p_ref, out_ref, send_sems, recv_sems):
        x, y, c = lax.axis_index("x"), lax.axis_index("y"), lax.axis_index("c")
        chips = [(1 - x, y), (x, 1 - y), (1 - x, 1 - y)]
        send = [pltpu.make_async_remote_copy(
            src_ref=p_ref.at[2 * cx + cy], dst_ref=out_ref.at[j], send_sem=send_sems.at[j],
            recv_sem=recv_sems.at[j], device_id=(cx, cy, c), device_id_type=_MESH)
            for j, (cx, cy) in enumerate(chips)]
        for cp in send:
            cp.start()
        for cp in send:
            cp.wait_recv()
        for cp in send:
            cp.wait_send()

    return _launch_exchange(
        body, name, p, _sds((n_peers,) + p.shape[1:], p.dtype),
        [pltpu.SemaphoreType.DMA((n_peers,)), pltpu.SemaphoreType.DMA((n_peers,))],
        lambda: _sibling_and_chip_peers()[1:], CHIPS_COLLECTIVE_ID, on_sequencer)


def _reduce_scatter(g, axis, name):
    sib = _rs_sibling(g, axis, name + "_d2d", on_sequencer=True)
    p = _pair_sum(g, sib, axis, name + "_pair")
    return p, _rs_chips(p, name + "_ici", on_sequencer=True)


def _s5_tables(lam_re, lam_im, log_step, b_re, b_im, c_re, c_im):
    nd, g, p, cg = b_re.shape
    step = jnp.exp(log_step)[..., None]
    mag = jnp.exp(lam_re * step)
    a_re, a_im = mag * jnp.cos(lam_im * step), mag * jnp.sin(lam_im * step)
    den = lam_re * lam_re + lam_im * lam_im
    num_re, num_im = a_re - 1.0, a_im
    k_re = (num_re * lam_re + num_im * lam_im) / den
    k_im = (num_im * lam_re - num_re * lam_im) / den
    bb_re = k_re[..., None] * b_re - k_im[..., None] * b_im
    bb_im = k_re[..., None] * b_im + k_im[..., None] * b_re
    gt = g // SSM_TILE_GROUPS
    hg = SSM_HALF_GROUPS
    eye = jnp.eye(SSM_TILE_GROUPS, dtype=F32).reshape(SSM_TILE_GROUPS, 2, hg)

    def pack_b(bb):
        w = jnp.einsum("djhqpc,ghq->djhgcqp", bb.reshape(nd, gt, 2, hg, p, cg), eye)
        return w.reshape(nd, gt * 2, SSM_TILE_GROUPS * cg, hg * p)

    def pack_c(cc):
        w = jnp.einsum("djhqcp,ghq->djhqpgc", cc.reshape(nd, gt, 2, hg, cg, p), eye)
        return w.reshape(nd, gt * 2, hg * p, SSM_TILE_GROUPS * cg)

    a = jnp.stack([a_re.reshape(nd, gt * 2, hg * p), a_im.reshape(nd, gt * 2, hg * p)], axis=2)
    w = jnp.concatenate([pack_b(bb_re), pack_b(bb_im)], axis=-1)
    c = jnp.concatenate([pack_c(c_re), -pack_c(c_im)], axis=-2)
    return w, c, a


def _ret_tables(decay_logit, dk, dv):
    ch = RET_CHUNK
    nd, h = decay_logit.shape
    lg = jax.nn.log_sigmoid(decay_logit)[:, :, None]
    pos = jnp.arange(ch, dtype=F32)
    fwd_diff = pos[:, None] - pos[None, :]
    diff = jnp.stack([fwd_diff, -fwd_diff])[:, None]
    mask = jnp.stack([fwd_diff >= 0, -fwd_diff > 0])[:, None]
    end_pos = jnp.stack([ch - 1.0 - pos, pos])[:, None]
    in_pos = jnp.stack([pos + 1.0, ch - pos])[:, None]
    w_end = jnp.exp(lg * end_pos)
    w_in = jnp.exp(lg * in_pos)
    decay = jnp.where(mask, jnp.exp(lg[..., None] * jnp.where(mask, diff, 0.0)), 0.0)
    g_chunk = jnp.exp(lg[..., 0] * ch)
    return (decay, jnp.broadcast_to(w_end[..., None], (nd, h, ch, dk)), jnp.broadcast_to(w_in[..., None], (nd, h, ch, dk)),
            jnp.broadcast_to(g_chunk[..., None, None], (nd, h, dk, dv)))


def _rope_tables(t_rows, ncc, dk):
    quarter = dk // 4
    idx = np.arange(t_rows)
    row, col = idx // GRID_W, idx % GRID_W
    inv = ROPE_BASE ** (-np.arange(quarter, dtype=np.float32) / quarter)
    ang_r = row.astype(np.float32)[:, None] * inv
    ang_c = col.astype(np.float32)[:, None] * inv
    ang_r, ang_c = jnp.asarray(ang_r, F32), jnp.asarray(ang_c, F32)
    cos = jnp.concatenate([jnp.cos(ang_r), jnp.cos(ang_r), jnp.cos(ang_c), jnp.cos(ang_c)], axis=1)
    sin = jnp.concatenate([-jnp.sin(ang_r), jnp.sin(ang_r), -jnp.sin(ang_c), jnp.sin(ang_c)], axis=1)
    n_ctx = ncc * RET_CHUNK
    cos = jnp.concatenate([jnp.ones((n_ctx, dk), F32), cos], axis=0)
    sin = jnp.concatenate([jnp.zeros((n_ctx, dk), F32), sin], axis=0)
    return cos, sin


def _to_scan_layout(ctx_rows, lat_rows, rev):
    u = jnp.concatenate([lat_rows, ctx_rows] if rev else [ctx_rows, lat_rows], axis=0)
    r, w = u.shape
    return u.reshape(N_SEG, r // N_SEG, w).transpose(1, 0, 2).reshape(r, w)


def _from_scan_layout(yp, n_ctx, rev):
    r, w = yp.shape
    y = yp.reshape(r // N_SEG, N_SEG, w).transpose(1, 0, 2).reshape(r, w)
    return (y[r - n_ctx:], y[:r - n_ctx]) if rev else (y[:n_ctx], y[n_ctx:])


def _pack(parts, width):
    rows = []
    for p in parts:
        flat = p.reshape(-1).astype(F32)
        n = flat.shape[0]
        rows.append(jnp.pad(flat, (0, -n % (SUBLANE * width))).reshape(-1, width))
    return jnp.concatenate(rows, axis=0)


def _packed_rows(n, width):
    return -(-n // (SUBLANE * width)) * SUBLANE


def _unpack(flat2d, shapes):
    width = flat2d.shape[1]
    out, row = [], 0
    for shp in shapes:
        n = int(np.prod(shp))
        nr = _packed_rows(n, width)
        out.append(flat2d[row:row + nr].reshape(-1)[:n].reshape(shp))
        row += nr
    return out


def kernel(x, c, ctx, c_ctx, ada_w, ada_b, norm_g, ffn_w_in, ffn_w_out, mix_w_in, ssm_lam_re, ssm_lam_im, ssm_log_step, ssm_b_re, ssm_b_im, ssm_c_re, ssm_c_im, ssm_d, ssm_glu_w, ret_decay_logit, ret_w_proj, mix_w_out, loss_target, m_c_ctx, m_ada_w, m_ada_b, m_norm_g, m_ffn_w_in, m_ffn_w_out, m_mix_w_in, m_ssm_lam_re, m_ssm_lam_im, m_ssm_log_step, m_ssm_b_re, m_ssm_b_im, m_ssm_c_re, m_ssm_c_im, m_ssm_d, m_ssm_glu_w, m_ret_decay_logit, m_ret_w_proj, m_mix_w_out, v_c_ctx, v_ada_w, v_ada_b, v_norm_g, v_ffn_w_in, v_ffn_w_out, v_mix_w_in, v_ssm_lam_re, v_ssm_lam_im, v_ssm_log_step, v_ssm_b_re, v_ssm_b_im, v_ssm_c_re, v_ssm_c_im, v_ssm_d, v_ssm_glu_w, v_ret_decay_logit, v_ret_w_proj, v_mix_w_out):
    t_rows, d = x.shape[1], x.shape[2]
    n_ctx = ctx.shape[1]
    r = n_ctx + t_rows
    ssm_w = ssm_d.shape[1]
    heads = ret_decay_logit.shape[2]
    mi = mix_w_in.shape[2] * N_DEV
    dk = (mi - ssm_w - 2 * d) // (6 * heads)
    dv = 2 * dk
    qk_w, v_w = heads * dk, heads * dv
    q_off = ssm_w
    ncc = n_ctx // RET_CHUNK
    tile = n_ctx
    nct = 1
    wide_tile = _tile(n_ctx, 128, 16)
    assert r % (N_SEG * SUBLANE) == 0 and n_ctx % RET_CHUNK == 0 and t_rows % tile == 0
    me = 4 * lax.axis_index("x") + 2 * lax.axis_index("y") + lax.axis_index("c")
    g_off = ssm_w + 2 * qk_w + v_w
    gs_off = g_off + v_w

    ng_cols = norm_g.shape[2]
    small0 = _pack([c[0], norm_g[0]], d)
    small0_all = _all_gather(small0, 0, "ag_cond")

    bf = lambda w: w.astype(BF16)
    small0_all, sh_in1, sh_out1, sh_mix = lax.optimization_barrier(
        (small0_all, bf(ffn_w_in[0, 0]), bf(ffn_w_out[0, 0]), bf(mix_w_in[0])))
    small0_all = small0_all.reshape(N_DEV, -1)
    w_in1 = _all_gather(sh_in1, 1, "ag_ffn1_in", on_sequencer=True)
    w_out1 = _all_gather(sh_out1, 0, "ag_ffn1_out", on_sequencer=True)
    w_mix = _all_gather(sh_mix, 1, "ag_mix_in", on_sequencer=True)
    w_glu = _all_gather(bf(ssm_glu_w[0]), 1, "ag_glu", on_sequencer=True)
    w_rp = _all_gather(bf(ret_w_proj[0]), 0, "ag_ret_proj", on_sequencer=True)
    w_mo = _all_gather(bf(mix_w_out[0]), 0, "ag_mix_out", on_sequencer=True)
    w_in2 = _all_gather(bf(ffn_w_in[0, 1]), 1, "ag_ffn2_in", on_sequencer=True)
    w_out2 = _all_gather(bf(ffn_w_out[0, 1]), 0, "ag_ffn2_out", on_sequencer=True)

    ng_at = _packed_rows(d, d) * d
    c_all = small0_all[:, :d]
    g_full = small0_all[:, ng_at:ng_at + 6 * ng_cols].reshape(N_DEV, 6, ng_cols).transpose(1, 0, 2).reshape(6, d)
    g6 = g_full.reshape(6, 1, d)
    cc = jnp.concatenate([c_all, c_ctx[None, :], jnp.zeros((2 * SUBLANE - N_DEV - 1, d), F32)], axis=0)
    sc = _silu_rows(cc, "ada_silu")
    na = ada_w.shape[2]
    a_loc = _mm(sc, ada_w[0], "nn", F32, "ada_fwd", tm=16, tn=na, tk=512)
    a_all = _all_gather(a_loc, 0, "ag_ada").reshape(N_DEV, 2 * SUBLANE, na)
    ada_x = lax.dynamic_index_in_dim(a_all, me, axis=1, keepdims=False).reshape(9 * d) + ada_b[0]
    ada_c = a_all[:, N_DEV, :].reshape(9 * d) + ada_b[0]
    mods = jnp.stack([ada_c.reshape(9, d), ada_x.reshape(9, d)]).reshape(18, 1, d)

    xin = jnp.concatenate([ctx[0], x[0]], axis=0)
    u1 = _ada_pre_fwd(xin, g6, mods, 0, 0, nct, tile, "pre1")
    h1 = _mm(u1, w_in1, "nn", BF16, "ffn1_in", tm=544, tn=1024)
    a1 = _swiglu_fwd(h1, wide_tile, "swiglu1")
    o1 = _mm(a1, w_out1, "nn", F32, "ffn1_out", tm=544, tn=1024, tk=2816)
    x1 = _ada_post_fwd(xin, o1, g6, mods, 1, 0, 0.5, nct, tile, "post1")
    u2 = _ada_pre_fwd(x1, g6, mods, 2, 1, nct, tile, "pre2")
    hm = _mm(u2, w_mix, "nn", F32, "mix_in", tm=544, tn=1024)

    us_ctx, us_lat = hm[:n_ctx, :ssm_w], hm[n_ctx:, :ssm_w]
    dskip = ssm_d.reshape(1, 1, ssm_w)
    s5_prm = (ssm_lam_re[0], ssm_lam_im[0], ssm_log_step[0], ssm_b_re[0], ssm_b_im[0], ssm_c_re[0], ssm_c_im[0])
    s5_tabs_both, s5_vjp = jax.vjp(_s5_tables, *s5_prm)
    s5_tabs, ups, y_dirs = [], [], []
    for dr in range(2):
        tabs = tuple(t[dr] for t in s5_tabs_both)
        up = _to_scan_layout(us_ctx, us_lat, dr == 1)
        yp = _s5_fwd(up, *tabs, dr == 1, "s5_fwd%d" % dr)
        s5_tabs.append(tabs)
        ups.append(up)
        y_dirs.append(_from_scan_layout(yp, n_ctx, dr == 1)[1])
    a_ssm = _ssm_out_fwd(y_dirs[0], y_dirs[1], hm, dskip, nct, tile, "ssm_out")
    gab = _mm(a_ssm, w_glu, "nn", F32, "glu", tm=512, tn=2048, tk=ssm_w)

    cos, sin = _rope_tables(t_rows, ncc, dk)
    ret_tabs_both, ret_vjp = jax.vjp(functools.partial(_ret_tables, dk=dk, dv=dv), ret_decay_logit[0])
    ret_tabs, o_dirs, s_ins = [], [], []
    for dr in range(2):
        tabs = tuple(t[dr] for t in ret_tabs_both)
        o_d, s_in = _ret_fwd(hm, cos, sin, *tabs, heads, dk, dv, q_off, ncc, dr == 1, "ret_fwd%d" % dr)
        ret_tabs.append(tabs)
        o_dirs.append(o_d)
        s_ins.append(s_in)
    ret_in = _ret_gate_fwd(o_dirs[0], o_dirs[1], hm, g_off, heads, dv, nct, tile, "ret_gate")
    rb = _mm(ret_in, w_rp, "nn", F32, "ret_proj", tm=512, tn=d, tk=v_w)
    merged = _merge_fwd(gab, rb, hm, gs_off, nct, tile, "merge")
    mix = _mm(merged, w_mo, "nn", F32, "mix_out", tm=512, tn=d, tk=d)
    x1x = x1[n_ctx:]
    x2 = _ada_post_fwd(x1x, mix, g6, mods, 3, 1, 1.0, 0, tile, "post2")
    u3 = _ada_pre_fwd(x2, g6, mods, 4, 2, 0, tile, "pre3")
    h3 = _mm(u3, w_in2, "nn", BF16, "ffn2_in", tm=512, tn=1024)
    a3 = _swiglu_fwd(h3, wide_tile, "swiglu2")
    o3 = _mm(a3, w_out2, "nn", F32, "ffn2_out", tm=512, tn=1024, tk=2816)
    x3 = _ada_post_fwd(x2, o3, g6, mods, 5, 2, 0.5, 0, tile, "post3")
    dy, lcols = _loss_grad(x3, loss_target[0], tile, "loss")
    loss_part = (0.5 * jnp.sum(lcols) / d).reshape(1)

    dg6 = [None] * 6
    dmod = {}

    def add_mod(sel_rows, k, val):
        for sel, row in sel_rows:
            dmod[(sel, k)] = dmod.get((sel, k), 0.0) + val[row, 0]

    both, lat = [(0, 0), (1, 1)], [(1, 0)]

    def tie(*vals):
        return lax.optimization_barrier(vals)

    def big_update(w3d, m3d, v3d, layer, gfull, axis, name, filled=None):
        p, recv = _reduce_scatter(gfull, axis, "rs_" + name)
        return _adamw_scattered(w3d, m3d, v3d, layer, p, recv, "adamw_" + name, filled)

    do3, dg6[5], dgt = _ada_post_bwd(dy, o3, g6, mods, 5, 2, 0.5, 0, 1, tile, "post3_bwd")
    add_mod(lat, 8, dgt)
    gw_out2 = _mm(a3, do3, "tn", BF16, "ffn2_out_dw", tm=1408, tn=1024, tk=2176)
    do3, gw_out2 = tie(do3, gw_out2)
    up_out2 = big_update(ffn_w_out[0], m_ffn_w_out[0], v_ffn_w_out[0], 1, gw_out2, 0, "ffn2_out")
    da3 = _mm(do3, w_out2, "nt", BF16, "ffn2_out_dx", tm=512, tn=2816, tk=d)
    dh3 = _swiglu_bwd(h3, da3, wide_tile, "swiglu2_bwd")
    gw_in2 = _mm(u3, dh3, "tn", BF16, "ffn2_in_dw", tm=1024, tn=1024, tk=2176)
    dh3, gw_in2 = tie(dh3, gw_in2)
    up_in2 = big_update(ffn_w_in[0], m_ffn_w_in[0], v_ffn_w_in[0], 1, gw_in2, 1, "ffn2_in")
    du3 = _mm(dh3, w_in2, "nt", F32, "ffn2_in_dx", tm=512, tn=d, tk=1024)
    dx2, dg6[4], dsh, dsc = _ada_pre_bwd(x2, du3, dy, g6, mods, 4, 2, 0, 1, tile, "pre3_bwd")
    add_mod(lat, 6, dsh)
    add_mod(lat, 7, dsc)
    dmix, dg6[3], dgt = _ada_post_bwd(dx2, mix, g6, mods, 3, 1, 1.0, 0, 1, tile, "post2_bwd")
    add_mod(lat, 5, dgt)
    gw_mo = _mm(merged, dmix, "tn", BF16, "mix_out_dw", tm=1024, tn=1024, tk=2176)
    dmix, gw_mo = tie(dmix, gw_mo)
    up_mo = big_update(mix_w_out, m_mix_w_out, v_mix_w_out, 0, gw_mo, 0, "mix_out")
    dmerged = _mm(dmix, w_mo, "nt", F32, "mix_out_dx", tm=512, tn=d, tk=d)
    dgab, drb, dgs, dgr = _merge_bwd(gab, rb, hm, gs_off, dmerged, nct, tile, "merge_bwd")
    gw_glu = _mm(a_ssm, dgab, "tn", BF16, "glu_dw", tm=1024, tn=1024, tk=2176)
    gw_rp = _mm(ret_in, drb, "tn", BF16, "ret_proj_dw", tm=1024, tn=1024, tk=2176)
    dgab, drb, gw_glu, gw_rp = tie(dgab, drb, gw_glu, gw_rp)
    up_glu = big_update(ssm_glu_w, m_ssm_glu_w, v_ssm_glu_w, 0, gw_glu, 1, "glu")
    up_rp = big_update(ret_w_proj, m_ret_w_proj, v_ret_w_proj, 0, gw_rp, 0, "ret_proj")
    da_ssm = _mm(dgab, w_glu, "nt", F32, "glu_dx", tm=512, tn=ssm_w, tk=2 * d)
    dret_in = _mm(drb, w_rp, "nt", F32, "ret_proj_dx", tm=512, tn=v_w, tk=d)
    d_o, dg_gate = _ret_gate_bwd(o_dirs[0], o_dirs[1], hm, g_off, dret_in, heads, dv, nct, tile, "ret_gate_bwd")
    dy_ssm, dus_direct, d_dskip = _ssm_out_bwd(y_dirs[0], y_dirs[1], hm, dskip, da_ssm, nct, tile, "ssm_out_bwd")
    s5_table_grads, du_ctx, du_lat = [], [], [dus_direct]
    for dr in range(2):
        dyp = _to_scan_layout(jnp.zeros((n_ctx, ssm_w), F32), dy_ssm, dr == 1)
        if dr == 1:
            dyp, up_out2, up_in2 = tie(dyp, up_out2, up_in2)
        outs = _s5_bwd(ups[dr], dyp, *s5_tabs[dr], dr == 1, "s5_bwd%d" % dr)
        part_ctx, part_lat = _from_scan_layout(outs[0], n_ctx, dr == 1)
        du_ctx.append(part_ctx)
        du_lat.append(part_lat)
        s5_table_grads.append(outs[1:])
    dqkv, ret_table_grads = [], []
    for dr in range(2):
        if dr == 1:
            d_o, up_mo, up_glu, up_rp = tie(d_o, up_mo, up_glu, up_rp)
        outs = _ret_bwd(hm, cos, sin, *ret_tabs[dr], s_ins[dr], d_o, heads, dk, dv, q_off, ncc, dr == 1,
                        "ret_bwd%d" % dr)
        dqkv.append(outs[:3])
        ret_table_grads.append(outs[3:])
    both_dirs = lambda grads: tuple(jnp.stack([g0, g1]) for g0, g1 in zip(*grads))
    early_parts = list(s5_vjp(both_dirs(s5_table_grads))) + list(ret_vjp(both_dirs(ret_table_grads)))
    s5_names = 7
    early_shapes = [p.shape for p in early_parts]
    early_all = _all_gather(_pack(early_parts, 1024), 0, "ag_s5_grads", on_sequencer=True)
    early_sum = _sum_leading(early_all.reshape(N_DEV, -1, 1024), "sum_s5_grads")
    dus = jnp.concatenate([du_ctx[0] + du_ctx[1], du_lat[0] + du_lat[1] + du_lat[2]], axis=0)
    dhm = _assemble_dhm(dus, dqkv[0][0], dqkv[1][0], dqkv[0][1], dqkv[1][1], dqkv[0][2], dqkv[1][2],
                        dg_gate, dgs, dgr, n_ctx // wide_tile, wide_tile, "assemble_dhm")
    gw_mix = _mm(u2, dhm, "tn", BF16, "mix_in_dw", tm=1024, tn=1024, tk=2176)
    dhm, gw_mix = tie(dhm, gw_mix)
    up_mix = big_update(mix_w_in, m_mix_w_in, v_mix_w_in, 0, gw_mix, 1, "mix_in")
    du2 = _mm(dhm, w_mix, "nt", F32, "mix_in_dx", tm=544, tn=d, tk=1024)
    dx1, dg6[2], dsh, dsc = _ada_pre_bwd(x1, du2, dx2, g6, mods, 2, 1, nct, 2, tile, "pre2_bwd", dres_x_only=True)
    add_mod(both, 3, dsh)
    add_mod(both, 4, dsc)
    do1, dg6[1], dgt = _ada_post_bwd(dx1, o1, g6, mods, 1, 0, 0.5, nct, 2, tile, "post1_bwd")
    add_mod(both, 2, dgt)
    gw_out1 = _mm(a1, do1, "tn", BF16, "ffn1_out_dw", tm=1408, tn=1024, tk=2176)
    do1, gw_out1 = tie(do1, gw_out1)
    up_out1 = big_update(ffn_w_out[0], m_ffn_w_out[0], v_ffn_w_out[0], 0, gw_out1, 0, "ffn1_out", filled=up_out2)
    da1 = _mm(do1, w_out1, "nt", BF16, "ffn1_out_dx", tm=544, tn=2816, tk=d)
    dh1 = _swiglu_bwd(h1, da1, wide_tile, "swiglu1_bwd")
    dh1, up_mix, early_sum = tie(dh1, up_mix, early_sum)
    early_sums = _unpack(early_sum, early_shapes)
    gw_in1 = _mm(u1, dh1, "tn", BF16, "ffn1_in_dw", tm=1024, tn=1024, tk=2176)
    dh1, gw_in1 = tie(dh1, gw_in1)
    up_in1 = big_update(ffn_w_in[0], m_ffn_w_in[0], v_ffn_w_in[0], 0, gw_in1, 1, "ffn1_in", filled=up_in2)
    du1 = _mm(dh1, w_in1, "nt", F32, "ffn1_in_dx", tm=544, tn=d, tk=1024)
    dxin, dg6[0], dsh, dsc = _ada_pre_bwd(xin, du1, dx1, g6, mods, 0, 0, nct, 2, tile, "pre1_bwd")
    add_mod(both, 0, dsh)
    add_mod(both, 1, dsc)
    grad_x = dxin[n_ctx:][None]

    zero_d = jnp.zeros((d,), F32)
    d_ada_x = jnp.stack([dmod.get((1, k), zero_d) for k in range(9)]).reshape(9 * d)
    d_ada_c = jnp.stack([dmod.get((0, k), zero_d) for k in range(9)]).reshape(9 * d)
    dg_full = jnp.stack([g[0, 0] for g in dg6])
    small_parts = [d_ada_x, d_ada_c, dg_full, d_dskip, loss_part]
    small_shapes = [p.shape for p in small_parts]
    packed = _pack(small_parts, 1024)
    gathered = _all_gather(packed, 0, "ag_small_grads").reshape(N_DEV, -1, 1024)
    summed = _sum_leading(gathered, "sum_small_grads")
    sums = _unpack(summed, small_shapes)
    sum_dx, sum_dc, sum_dg = sums[0], sums[1], sums[2]
    loss = sums[4][0]
    grad_ada_b = (sum_dx + sum_dc)[None]
    dx_rows = gathered.reshape(N_DEV, -1)[:, :9 * d]
    col0 = me * na
    da_rows = jnp.concatenate([lax.dynamic_slice_in_dim(dx_rows, col0, na, axis=1),
                               lax.dynamic_slice_in_dim(sum_dc[None], col0, na, axis=1),
                               jnp.zeros((2 * SUBLANE - N_DEV - 1, na), F32)], axis=0)
    grad_ada_w = _mm(sc, da_rows, "tn", F32, "ada_dw", tm=512, tn=na, tk=16)
    d_sc = _mm(da_rows, ada_w[0], "nt", F32, "ada_dx", tm=16, tn=512, tk=na)
    d_sc_all = _all_gather(jnp.broadcast_to(d_sc[N_DEV:N_DEV + 1], (SUBLANE, d)), 0, "ag_dctx")
    d_sc_sum = _sum_leading(d_sc_all.reshape(N_DEV, SUBLANE, d), "sum_dctx")
    grad_c_ctx = _silu_grad_rows(jnp.broadcast_to(c_ctx[None], (SUBLANE, d)), d_sc_sum, "ctx_silu_bwd")[0]
    grad_norm_g = lax.dynamic_slice_in_dim(sum_dg, me * ng_cols, ng_cols, axis=1)[None]

    upd = {}
    upd["ffn_w_in"] = [o[None] for o in up_in1]
    upd["ffn_w_out"] = [o[None] for o in up_out1]
    upd["mix_w_in"] = list(up_mix)
    upd["ssm_glu_w"] = list(up_glu)
    upd["ret_w_proj"] = list(up_rp)
    upd["mix_w_out"] = list(up_mo)
    upd["ada_w"] = [o[None] for o in _adamw(ada_w[0], m_ada_w[0], v_ada_w[0], grad_ada_w[None], "adamw_ada_w")]

    small_names = ["c_ctx", "ada_b", "norm_g", "ssm_lam_re", "ssm_lam_im", "ssm_log_step", "ssm_b_re", "ssm_b_im",
                   "ssm_c_re", "ssm_c_im", "ssm_d", "ret_decay_logit"]
    small_w = [c_ctx, ada_b, norm_g, ssm_lam_re, ssm_lam_im, ssm_log_step, ssm_b_re, ssm_b_im, ssm_c_re, ssm_c_im,
               ssm_d, ret_decay_logit]
    small_m = [m_c_ctx, m_ada_b, m_norm_g, m_ssm_lam_re, m_ssm_lam_im, m_ssm_log_step, m_ssm_b_re, m_ssm_b_im,
               m_ssm_c_re, m_ssm_c_im, m_ssm_d, m_ret_decay_logit]
    small_v = [v_c_ctx, v_ada_b, v_norm_g, v_ssm_lam_re, v_ssm_lam_im, v_ssm_log_step, v_ssm_b_re, v_ssm_b_im,
               v_ssm_c_re, v_ssm_c_im, v_ssm_d, v_ret_decay_logit]
    small_g = [grad_c_ctx, grad_ada_b, grad_norm_g] + [s[None] for s in early_sums[:s5_names]] + \
              [sums[3].reshape(ssm_d.shape), early_sums[s5_names][None]]
    shapes = [w.shape for w in small_w]
    res = _adamw(_pack(small_w, 1024), _pack(small_m, 1024), _pack(small_v, 1024), _pack(small_g, 1024)[None],
                 "adamw_small")
    small_out = [_unpack(o, shapes) for o in res]
    for i, nm in enumerate(small_names):
        upd[nm] = [small_out[kind][i] for kind in range(4)]

    order = ["c_ctx", "ada_w", "ada_b", "norm_g", "ffn_w_in", "ffn_w_out", "mix_w_in", "ssm_lam_re", "ssm_lam_im",
             "ssm_log_step", "ssm_b_re", "ssm_b_im", "ssm_c_re", "ssm_c_im", "ssm_d", "ssm_glu_w", "ret_decay_logit",
             "ret_w_proj", "mix_w_out"]
    outs = [loss, grad_x]
    for kind in range(4):
        outs += [upd[nm][kind] for nm in order]
    return tuple(outs)
```

```python
import functools
import math

import jax
import jax.numpy as jnp
import numpy as np
from jax import lax
from jax.experimental import pallas as pl
from jax.experimental.pallas import tpu as pltpu
from jax.experimental.pallas import tpu_sc as plsc

F32 = jnp.float32
BF16 = jnp.bfloat16
MXU_DTYPE = jnp.bfloat16
MESH_AXES = ("x", "y", "c")
N_DEV = 8
V7X_VMEM_LIMIT_BYTES = 56 * 1024 * 1024
LANE = 128
SUBLANE = 8

GRID_W = 64
RET_CHUNK = 128
ROPE_BASE = 10000.0
NORM_EPS = 1e-6
ADAM_LR = 0.001
ADAM_B1 = 0.9
ADAM_B2 = 0.999
ADAM_EPS = 1e-08
ADAM_WD = 0.01
ADAM_STEP = 10
SSM_TILE_GROUPS = 8
SSM_HALF_GROUPS = 4
N_SEG = 16


def _params(sem=None):
    return pltpu.CompilerParams(dimension_semantics=sem, vmem_limit_bytes=V7X_VMEM_LIMIT_BYTES)


def _tile(n, target, mult):
    best = None
    t = mult
    while t <= min(n, target):
        if n % t == 0:
            best = t
        t += mult
    return n if best is None else best


def _sds(shape, dtype):
    return jax.ShapeDtypeStruct(tuple(shape), dtype)


def _mm(a, b, dims, out_dtype, name, tm=512, tn=1408, tk=2048):
    if dims == "nn":
        (m, k), (k2, n) = a.shape, b.shape
    elif dims == "nt":
        (m, k), (n, k2) = a.shape, b.shape
    else:
        (k, m), (k2, n) = a.shape, b.shape
    assert k == k2, (a.shape, b.shape, dims)
    tm = _tile(m, tm, 16)
    tn = _tile(n, tn, LANE)
    tk = _tile(k, tk, LANE if dims != "tn" else 16)
    nk = k // tk
    dn = {"nn": (((1,), (0,)), ((), ())), "nt": (((1,), (1,)), ((), ())), "tn": (((0,), (0,)), ((), ()))}[dims]

    def product(a_ref, b_ref):
        return lax.dot_general(a_ref[...].astype(MXU_DTYPE), b_ref[...].astype(MXU_DTYPE), dn,
                               preferred_element_type=F32)

    def body_single(a_ref, b_ref, o_ref):
        o_ref[...] = product(a_ref, b_ref).astype(o_ref.dtype)

    def body(a_ref, b_ref, o_ref, acc_ref):
        kk = pl.program_id(2)

        @pl.when(kk == 0)
        def _():
            acc_ref[...] = product(a_ref, b_ref)

        @pl.when((kk > 0) & (kk < nk - 1))
        def _():
            acc_ref[...] += product(a_ref, b_ref)

        @pl.when(kk == nk - 1)
        def _():
            o_ref[...] = (acc_ref[...] + product(a_ref, b_ref)).astype(o_ref.dtype)

    if dims == "nn":
        a_spec = pl.BlockSpec((tm, tk), lambda j, i, kk: (i, kk))
        b_spec = pl.BlockSpec((tk, tn), lambda j, i, kk: (kk, j))
    elif dims == "nt":
        a_spec = pl.BlockSpec((tm, tk), lambda j, i, kk: (i, kk))
        b_spec = pl.BlockSpec((tn, tk), lambda j, i, kk: (j, kk))
    else:
        a_spec = pl.BlockSpec((tk, tm), lambda j, i, kk: (kk, i))
        b_spec = pl.BlockSpec((tk, tn), lambda j, i, kk: (kk, j))
    return pl.pallas_call(
        body_single if nk == 1 else body, name=name, grid=(n // tn, m // tm, nk), in_specs=[a_spec, b_spec],
        out_specs=pl.BlockSpec((tm, tn), lambda j, i, kk: (i, j)), out_shape=_sds((m, n), out_dtype),
        scratch_shapes=[] if nk == 1 else [pltpu.VMEM((tm, tn), F32)],
        compiler_params=_params(("parallel", "parallel", "arbitrary")))(a, b)


def _mm_k_part(a, b, part, n_parts, partial_sum, out_dtype, name, tm=512, tn=1024):
    m, k = a.shape
    kp, n = b.shape
    assert kp * n_parts == k
    tm = _tile(m, tm, 16)
    tn = _tile(n, tn, LANE)

    def body(a_ref, b_ref, *refs):
        acc = jnp.dot(a_ref[...].astype(MXU_DTYPE), b_ref[...].astype(MXU_DTYPE), preferred_element_type=F32)
        if partial_sum is not None:
            acc = acc + refs[0][...]
        refs[-1][...] = acc.astype(refs[-1].dtype)

    tile = pl.BlockSpec((tm, tn), lambda j, i: (i, j))
    in_specs = [pl.BlockSpec((tm, kp), lambda j, i: (i, part)), pl.BlockSpec((kp, tn), lambda j, i: (0, j))]
    args = [a, b]
    if partial_sum is not None:
        in_specs.append(tile)
        args.append(partial_sum)
    return pl.pallas_call(
        body, name=name, grid=(n // tn, m // tm), in_specs=in_specs, out_specs=tile,
        out_shape=_sds((m, n), out_dtype), compiler_params=_params(("parallel", "parallel")))(*args)


def _rows(name, body, n_tiles, ins, outs):
    in_specs = [pl.BlockSpec(blk, imap) for (_, blk, imap) in ins]
    out_specs = [pl.BlockSpec(blk, imap) for (_, _, blk, imap) in outs]
    out_shape = [_sds(shape, dt) for (shape, dt, _, _) in outs]
    res = pl.pallas_call(body, name=name, grid=(n_tiles,), in_specs=in_specs, out_specs=out_specs,
                         out_shape=out_shape, compiler_params=_params(("arbitrary",)))(*[a for (a, _, _) in ins])
    return res


def _row_in(arr, tile, width=None, col=0, x_only_offset=None):
    width = arr.shape[1] if width is None else width
    if x_only_offset is None:
        return (arr, (tile, width), lambda i: (i, col))
    return (arr, (tile, width), lambda i: (jnp.maximum(i - x_only_offset, 0), col))


def _vec_in(arr, idx_fn):
    return (arr, (1, 1, arr.shape[2]), lambda i: (idx_fn(i), 0, 0))


def _rms(h):
    return lax.rsqrt(jnp.mean(h * h, axis=-1, keepdims=True) + NORM_EPS)


def _sigmoid(z):
    return 1.0 / (1.0 + jnp.exp(-z))


def _ada_pre_fwd(h, g6, mods, gi, mi, nct, tile, name):
    r, d = h.shape
    sel = lambda i: jnp.where(i >= nct, 1, 0)

    def body(h_ref, g_ref, sh_ref, sc_ref, u_ref):
        hh = h_ref[...]
        n = hh * _rms(hh) * g_ref[0]
        u_ref[...] = (n * (1.0 + sc_ref[0]) + sh_ref[0]).astype(u_ref.dtype)

    (u,) = _rows(name, body, r // tile,
                 [_row_in(h, tile), _vec_in(g6, lambda i: gi), _vec_in(mods, lambda i: sel(i) * 9 + 3 * mi),
                  _vec_in(mods, lambda i: sel(i) * 9 + 3 * mi + 1)],
                 [((r, d), BF16, (tile, d), lambda i: (i, 0))])
    return u


def _ada_pre_bwd(h, du, dres, g6, mods, gi, mi, nct, nsel, tile, name, dres_x_only=False):
    r, d = h.shape
    sel = lambda i: jnp.where(i >= nct, 1, 0) if nsel == 2 else 0
    msel = lambda i: jnp.where(i >= nct, 1, 0)
    off = nct if dres_x_only else None

    def body(h_ref, du_ref, dr_ref, g_ref, sc_ref, dh_ref, dg_ref, dsh_ref, dsc_ref):
        i = pl.program_id(0)
        hh = h_ref[...]
        rr = _rms(hh)
        g = g_ref[0]
        hn = hh * rr
        n = hn * g
        du_ = du_ref[...].astype(F32)
        dn = du_ * (1.0 + sc_ref[0])

        @pl.when(i == 0)
        def _():
            dg_ref[...] = jnp.zeros_like(dg_ref)

        @pl.when((i == 0) | (i == nct))
        def _():
            dsh_ref[...] = jnp.zeros_like(dsh_ref)
            dsc_ref[...] = jnp.zeros_like(dsc_ref)

        dg_ref[0] += jnp.sum(dn * hn, axis=0, keepdims=True)
        dsh_ref[0] += jnp.sum(du_, axis=0, keepdims=True)
        dsc_ref[0] += jnp.sum(du_ * n, axis=0, keepdims=True)
        t = dn * g
        dh = rr * t - hn * (rr * jnp.mean(t * hn, axis=-1, keepdims=True))
        if dres_x_only:
            dh_ref[...] = dh + jnp.where(i >= nct, dr_ref[...], 0.0)
        else:
            dh_ref[...] = dh + dr_ref[...]

    dh, dg, dsh, dsc = _rows(
        name, body, r // tile,
        [_row_in(h, tile), _row_in(du, tile), _row_in(dres, tile, x_only_offset=off), _vec_in(g6, lambda i: gi),
         _vec_in(mods, lambda i: msel(i) * 9 + 3 * mi + 1)],
        [((r, d), F32, (tile, d), lambda i: (i, 0)), ((1, 1, d), F32, (1, 1, d), lambda i: (0, 0, 0)),
         ((nsel, 1, d), F32, (1, 1, d), lambda i: (sel(i), 0, 0)),
         ((nsel, 1, d), F32, (1, 1, d), lambda i: (sel(i), 0, 0))])
    return dh, dg, dsh, dsc


def _ada_post_fwd(h, o, g6, mods, gi, mi, res_w, nct, tile, name, h_x_only=False):
    r, d = o.shape
    sel = lambda i: jnp.where(i >= nct, 1, 0)

    def body(h_ref, o_ref, g_ref, gt_ref, y_ref):
        oo = o_ref[...]
        n = oo * _rms(oo) * g_ref[0]
        y_ref[...] = h_ref[...] + res_w * gt_ref[0] * n

    (y,) = _rows(name, body, r // tile,
                 [_row_in(h, tile), _row_in(o, tile), _vec_in(g6, lambda i: gi),
                  _vec_in(mods, lambda i: sel(i) * 9 + 3 * mi + 2)],
                 [((r, d), F32, (tile, d), lambda i: (i, 0))])
    return y


def _ada_post_bwd(dy, o, g6, mods, gi, mi, res_w, nct, nsel, tile, name):
    r, d = o.shape
    sel = lambda i: jnp.where(i >= nct, 1, 0) if nsel == 2 else 0
    msel = lambda i: jnp.where(i >= nct, 1, 0)

    def body(dy_ref, o_ref, g_ref, gt_ref, do_ref, dg_ref, dgt_ref):
        i = pl.program_id(0)
        oo = o_ref[...]
        rr = _rms(oo)
        g = g_ref[0]
        on = oo * rr
        dy_ = dy_ref[...] * res_w

        @pl.when(i == 0)
        def _():
            dg_ref[...] = jnp.zeros_like(dg_ref)

        @pl.when((i == 0) | (i == nct))
        def _():
            dgt_ref[...] = jnp.zeros_like(dgt_ref)

        dgt_ref[0] += jnp.sum(dy_ * (on * g), axis=0, keepdims=True)
        dn = dy_ * gt_ref[0]
        dg_ref[0] += jnp.sum(dn * on, axis=0, keepdims=True)
        t = dn * g
        do_ref[...] = (rr * t - on * (rr * jnp.mean(t * on, axis=-1, keepdims=True))).astype(do_ref.dtype)

    do, dg, dgt = _rows(
        name, body, r // tile,
        [_row_in(dy, tile), _row_in(o, tile), _vec_in(g6, lambda i: gi),
         _vec_in(mods, lambda i: msel(i) * 9 + 3 * mi + 2)],
        [((r, d), BF16, (tile, d), lambda i: (i, 0)), ((1, 1, d), F32, (1, 1, d), lambda i: (0, 0, 0)),
         ((nsel, 1, d), F32, (1, 1, d), lambda i: (sel(i), 0, 0))])
    return do, dg, dgt


def _swiglu_fwd(h, tile, name):
    r, w2 = h.shape
    f = w2 // 2

    def body(h_ref, a_ref):
        gt = h_ref[:, :f].astype(F32)
        up = h_ref[:, f:].astype(F32)
        a_ref[...] = (gt * _sigmoid(gt) * up).astype(a_ref.dtype)

    (a,) = _rows(name, body, r // tile, [_row_in(h, tile)], [((r, f), BF16, (tile, f), lambda i: (i, 0))])
    return a


def _swiglu_bwd(h, da, tile, name):
    r, w2 = h.shape
    f = w2 // 2

    def body(h_ref, da_ref, dh_ref):
        gt = h_ref[:, :f].astype(F32)
        up = h_ref[:, f:].astype(F32)
        d = da_ref[...].astype(F32)
        sg = _sigmoid(gt)
        dh_ref[:, :f] = (d * up * (sg * (1.0 + gt * (1.0 - sg)))).astype(dh_ref.dtype)
        dh_ref[:, f:] = (d * gt * sg).astype(dh_ref.dtype)

    (dh,) = _rows(name, body, r // tile, [_row_in(h, tile), _row_in(da, tile)],
                  [((r, w2), BF16, (tile, w2), lambda i: (i, 0))])
    return dh


def _gelu_parts(y):
    c0 = math.sqrt(2.0 / math.pi)
    inner = c0 * (y + 0.044715 * y * y * y)
    th = jnp.tanh(inner)
    return th, c0 * (1.0 + 3 * 0.044715 * y * y)


def _ssm_out_fwd(y0, y1, hm, dskip, nct, tile, name):
    t_rows, s = y0.shape

    def body(y0_ref, y1_ref, u_ref, d_ref, a_ref):
        y = y0_ref[...] + y1_ref[...] + d_ref[0] * u_ref[...]
        th, _ = _gelu_parts(y)
        a_ref[...] = (0.5 * y * (1.0 + th)).astype(a_ref.dtype)

    (a,) = _rows(name, body, t_rows // tile,
                 [_row_in(y0, tile), _row_in(y1, tile), (hm, (tile, s), lambda i: (i + nct, 0)),
                  _vec_in(dskip, lambda i: 0)],
                 [((t_rows, s), BF16, (tile, s), lambda i: (i, 0))])
    return a


def _ssm_out_bwd(y0, y1, hm, dskip, da, nct, tile, name):
    t_rows, s = y0.shape

    def body(y0_ref, y1_ref, u_ref, d_ref, da_ref, dy_ref, du_ref, dd_ref):
        i = pl.program_id(0)
        u = u_ref[...]
        y = y0_ref[...] + y1_ref[...] + d_ref[0] * u
        th, dinner = _gelu_parts(y)
        dy = da_ref[...] * (0.5 * (1.0 + th) + 0.5 * y * (1.0 - th * th) * dinner)
        dy_ref[...] = dy
        du_ref[...] = dy * d_ref[0]

        @pl.when(i == 0)
        def _():
            dd_ref[...] = jnp.zeros_like(dd_ref)

        dd_ref[0] += jnp.sum(dy * u, axis=0, keepdims=True)

    dy, du, dd = _rows(name, body, t_rows // tile,
                       [_row_in(y0, tile), _row_in(y1, tile), (hm, (tile, s), lambda i: (i + nct, 0)),
                        _vec_in(dskip, lambda i: 0), _row_in(da, tile)],
                       [((t_rows, s), F32, (tile, s), lambda i: (i, 0)), ((t_rows, s), F32, (tile, s), lambda i: (i, 0)),
                        ((1, 1, s), F32, (1, 1, s), lambda i: (0, 0, 0))])
    return dy, du, dd


def _col_pieces(arr, off, width, tile, nct, unit=None):
    pw = math.gcd(off, width if unit is None else unit)
    specs = [(arr, (tile, pw), functools.partial(lambda i, cb: (i + nct, cb), cb=off // pw + p))
             for p in range(width // pw)]
    return specs, pw


def _ret_gate_fwd(o0, o1, hm, g_off, heads, dv, nct, tile, name):
    t_rows, w = o0.shape
    g_specs, pw = _col_pieces(hm, g_off, w, tile, nct)
    ng = len(g_specs)

    def body(o0_ref, o1_ref, *refs):
        g_refs, r_ref = refs[:ng], refs[ng]
        for hd in range(heads):
            cs = slice(hd * dv, (hd + 1) * dv)
            o = o0_ref[:, cs] + o1_ref[:, cs]
            lo = (hd * dv) % pw
            g = g_refs[(hd * dv) // pw][:, lo:lo + dv]
            r_ref[:, cs] = (g * _sigmoid(g) * (o * _rms(o))).astype(r_ref.dtype)

    (ri,) = _rows(name, body, t_rows // tile, [_row_in(o0, tile), _row_in(o1, tile)] + g_specs,
                  [((t_rows, w), BF16, (tile, w), lambda i: (i, 0))])
    return ri


def _ret_gate_bwd(o0, o1, hm, g_off, dri, heads, dv, nct, tile, name):
    t_rows, w = o0.shape
    g_specs, pw = _col_pieces(hm, g_off, w, tile, nct)
    ng = len(g_specs)

    def body(o0_ref, o1_ref, d_ref, *refs):
        g_refs, do_ref, dg_ref = refs[:ng], refs[ng], refs[ng + 1]
        for hd in range(heads):
            cs = slice(hd * dv, (hd + 1) * dv)
            o = o0_ref[:, cs] + o1_ref[:, cs]
            lo = (hd * dv) % pw
            g = g_refs[(hd * dv) // pw][:, lo:lo + dv]
            d = d_ref[:, cs]
            rr = _rms(o)
            on = o * rr
            sg = _sigmoid(g)
            dg_ref[:, cs] = (d * on * (sg * (1.0 + g * (1.0 - sg)))).astype(dg_ref.dtype)
            t = d * (g * sg)
            do_ref[:, cs] = rr * t - on * (rr * jnp.mean(t * on, axis=-1, keepdims=True))

    do, dg = _rows(name, body, t_rows // tile, [_row_in(o0, tile), _row_in(o1, tile), _row_in(dri, tile)] + g_specs,
                   [((t_rows, w), F32, (tile, w), lambda i: (i, 0)), ((t_rows, w), BF16, (tile, w), lambda i: (i, 0))])
    return do, dg


def _merge_fwd(gab, rb, hm, gs_off, nct, tile, name):
    t_rows, d = rb.shape
    specs, pw = _col_pieces(hm, gs_off, 2 * d, tile, nct, unit=d)
    npc = d // pw

    def body(gab_ref, rb_ref, *refs):
        gs_refs, gr_refs, m_ref = refs[:npc], refs[npc:2 * npc], refs[2 * npc]
        for p in range(npc):
            cs = slice(p * pw, (p + 1) * pw)
            ga = gab_ref[:, cs]
            gb = gab_ref[:, d + p * pw:d + (p + 1) * pw]
            m_ref[:, cs] = (_sigmoid(gs_refs[p][...]) * (ga * _sigmoid(gb))
                            + _sigmoid(gr_refs[p][...]) * rb_ref[:, cs]).astype(m_ref.dtype)

    (mg,) = _rows(name, body, t_rows // tile, [_row_in(gab, tile), _row_in(rb, tile)] + specs,
                  [((t_rows, d), BF16, (tile, d), lambda i: (i, 0))])
    return mg


def _merge_bwd(gab, rb, hm, gs_off, dm, nct, tile, name):
    t_rows, d = rb.shape
    specs, pw = _col_pieces(hm, gs_off, 2 * d, tile, nct, unit=d)
    npc = d // pw

    def body(gab_ref, rb_ref, dm_ref, *refs):
        gs_refs, gr_refs = refs[:npc], refs[npc:2 * npc]
        dgab_ref, drb_ref, dgs_ref, dgr_ref = refs[2 * npc:]
        for p in range(npc):
            cs = slice(p * pw, (p + 1) * pw)
            cs2 = slice(d + p * pw, d + (p + 1) * pw)
            ga = gab_ref[:, cs]
            gb = gab_ref[:, cs2]
            dmm = dm_ref[:, cs]
            ss = _sigmoid(gs_refs[p][...])
            sr = _sigmoid(gr_refs[p][...])
            sb = _sigmoid(gb)
            dbr = dmm * ss
            dgab_ref[:, cs] = (dbr * sb).astype(dgab_ref.dtype)
            dgab_ref[:, cs2] = (dbr * ga * sb * (1.0 - sb)).astype(dgab_ref.dtype)
            drb_ref[:, cs] = (dmm * sr).astype(drb_ref.dtype)
            dgs_ref[:, cs] = (dmm * (ga * sb) * ss * (1.0 - ss)).astype(dgs_ref.dtype)
            dgr_ref[:, cs] = (dmm * rb_ref[:, cs] * sr * (1.0 - sr)).astype(dgr_ref.dtype)

    return _rows(name, body, t_rows // tile, [_row_in(gab, tile), _row_in(rb, tile), _row_in(dm, tile)] + specs,
                 [((t_rows, 2 * d), BF16, (tile, 2 * d), lambda i: (i, 0)), ((t_rows, d), BF16, (tile, d), lambda i: (i, 0)),
                  ((t_rows, d), BF16, (tile, d), lambda i: (i, 0)), ((t_rows, d), BF16, (tile, d), lambda i: (i, 0))])


def _assemble_dhm(dus, dq0, dq1, dk0, dk1, dv0, dv1, dg, dgs, dgr, nct, tile, name):
    r, s = dus.shape
    qk = dq0.shape[1]
    vw = dv0.shape[1]
    d = dgs.shape[1]
    mi = s + 2 * qk + 2 * vw + 2 * d
    c_q, c_k, c_v, c_g, c_gs, c_gr = s, s + qk, s + 2 * qk, s + 2 * qk + vw, s + 2 * qk + 2 * vw, s + 2 * qk + 2 * vw + d

    def body(dus_ref, dq0_ref, dq1_ref, dk0_ref, dk1_ref, dv0_ref, dv1_ref, dg_ref, dgs_ref, dgr_ref, o_ref):
        i = pl.program_id(0)
        lat = i >= nct
        o_ref[:, :s] = dus_ref[...].astype(o_ref.dtype)
        o_ref[:, c_q:c_k] = (dq0_ref[...] + dq1_ref[...]).astype(o_ref.dtype)
        o_ref[:, c_k:c_v] = (dk0_ref[...] + dk1_ref[...]).astype(o_ref.dtype)
        o_ref[:, c_v:c_g] = (dv0_ref[...] + dv1_ref[...]).astype(o_ref.dtype)
        o_ref[:, c_g:c_gs] = jnp.where(lat, dg_ref[...], 0.0).astype(o_ref.dtype)
        o_ref[:, c_gs:c_gr] = jnp.where(lat, dgs_ref[...], 0.0).astype(o_ref.dtype)
        o_ref[:, c_gr:] = jnp.where(lat, dgr_ref[...], 0.0).astype(o_ref.dtype)

    (out,) = _rows(name, body, r // tile,
                   [_row_in(dus, tile), _row_in(dq0, tile), _row_in(dq1, tile), _row_in(dk0, tile), _row_in(dk1, tile),
                    _row_in(dv0, tile), _row_in(dv1, tile), _row_in(dg, tile, x_only_offset=nct),
                    _row_in(dgs, tile, x_only_offset=nct), _row_in(dgr, tile, x_only_offset=nct)],
                   [((r, mi), BF16, (tile, mi), lambda i: (i, 0))])
    return out


def _loss_grad(y, target, tile, name):
    t_rows, d = y.shape

    def body(y_ref, t_ref, dy_ref, l_ref):
        i = pl.program_id(0)
        e = y_ref[...] - t_ref[...]
        dy_ref[...] = e * (1.0 / d)

        @pl.when(i == 0)
        def _():
            l_ref[...] = jnp.zeros_like(l_ref)

        l_ref[0] += jnp.sum(e * e, axis=0, keepdims=True)

    return _rows(name, body, t_rows // tile, [_row_in(y, tile), _row_in(target, tile)],
                 [((t_rows, d), F32, (tile, d), lambda i: (i, 0)), ((1, 1, d), F32, (1, 1, d), lambda i: (0, 0, 0))])


def _silu_rows(v, name):
    def body(v_ref, o_ref):
        z = v_ref[...]
        o_ref[...] = z * _sigmoid(z)

    (o,) = _rows(name, body, 1, [_row_in(v, v.shape[0])], [(v.shape, F32, v.shape, lambda i: (0, 0))])
    return o


def _silu_grad_rows(v, dv, name):
    def body(v_ref, d_ref, o_ref):
        z = v_ref[...]
        sg = _sigmoid(z)
        o_ref[...] = d_ref[...] * (sg * (1.0 + z * (1.0 - sg)))

    (o,) = _rows(name, body, 1, [_row_in(v, v.shape[0]), _row_in(dv, v.shape[0])],
                 [(v.shape, F32, v.shape, lambda i: (0, 0))])
    return o


def _sum_leading(g8, name):
    n, r, c = g8.shape
    tile = _tile(r, 256, SUBLANE)

    def body(g_ref, o_ref):
        acc = g_ref[0]
        for j in range(1, n):
            acc = acc + g_ref[j]
        o_ref[...] = acc

    (o,) = _rows(name, body, r // tile, [(g8, (n, tile, c), lambda i: (0, i, 0))],
                 [((r, c), F32, (tile, c), lambda i: (i, 0))])
    return o


def _pair_sum(g, recv, axis, name):
    n, br, bc = recv.shape
    tile = _tile(br, 256, 16)
    nrt = br // tile
    core = lax.axis_index("c").astype(jnp.int32).reshape(1)

    def body(c_ref, g_ref, r_ref, o_ref):
        o_ref[0] = (g_ref[...].astype(F32) + r_ref[0].astype(F32)).astype(o_ref.dtype)

    if axis == 1:
        g_spec = pl.BlockSpec((tile, bc), lambda q, i, c_ref: (i, 2 * q + c_ref[0]))
    else:
        g_spec = pl.BlockSpec((tile, bc), lambda q, i, c_ref: ((2 * q + c_ref[0]) * nrt + i, 0))
    slot = pl.BlockSpec((1, tile, bc), lambda q, i, c_ref: (q, i, 0))
    return pl.pallas_call(
        body, name=name, out_shape=_sds((n, br, bc), recv.dtype),
        grid_spec=pltpu.PrefetchScalarGridSpec(num_scalar_prefetch=1, grid=(n, nrt), in_specs=[g_spec, slot],
                                               out_specs=slot),
        compiler_params=_params(("arbitrary", "arbitrary")))(core, g, recv)


def _adam_math(w, m, v, g):
    c1 = 1.0 / (1.0 - ADAM_B1 ** ADAM_STEP)
    c2 = 1.0 / (1.0 - ADAM_B2 ** ADAM_STEP)
    mm = ADAM_B1 * m + (1.0 - ADAM_B1) * g
    vv = ADAM_B2 * v + (1.0 - ADAM_B2) * (g * g)
    return -ADAM_LR * ((mm * c1) / (jnp.sqrt(vv * c2) + ADAM_EPS) + ADAM_WD * w), mm, vv


def _adamw(w, m, v, gparts, name):
    r, c = w.shape
    n = gparts.shape[0]
    tile = _tile(r, 256, 16)

    def body(w_ref, m_ref, v_ref, g_ref, go_ref, d_ref, mo_ref, vo_ref):
        g = g_ref[0].astype(F32)
        for j in range(1, n):
            g = g + g_ref[j].astype(F32)
        go_ref[...] = g
        d_ref[...], mo_ref[...], vo_ref[...] = _adam_math(w_ref[...], m_ref[...], v_ref[...], g)

    rs = lambda arr: _row_in(arr, tile)
    out = ((r, c), F32, (tile, c), lambda i: (i, 0))
    return _rows(name, body, r // tile, [rs(w), rs(m), rs(v), (gparts, (n, tile, c), lambda i: (0, i, 0))],
                 [out, out, out, out])


def _adamw_scattered(w, m, v, layer, p, recv, name, filled=None):
    nl, r, c = w.shape
    n = recv.shape[0]
    tile = _tile(r, 256, 16)
    chip = (2 * lax.axis_index("x") + lax.axis_index("y")).astype(jnp.int32).reshape(1)
    n_prev = 0 if filled is None else len(filled)

    def body(q_ref, w_ref, m_ref, v_ref, p_ref, g_ref, *rest):
        go_ref, d_ref, mo_ref, vo_ref = rest[n_prev:]
        g = p_ref[0].astype(F32)
        for j in range(n):
            g = g + g_ref[j].astype(F32)
        go_ref[0] = g
        d_ref[0], mo_ref[0], vo_ref[0] = _adam_math(w_ref[0], m_ref[0], v_ref[0], g)

    slab = pl.BlockSpec((1, tile, c), lambda i, q_ref: (layer, i, 0))
    anywhere = pl.BlockSpec(memory_space=pl.ANY)
    out = _sds((nl, r, c), F32)
    prev = [] if filled is None else list(filled)
    return pl.pallas_call(
        body, name=name, out_shape=[out, out, out, out],
        grid_spec=pltpu.PrefetchScalarGridSpec(
            num_scalar_prefetch=1, grid=(r // tile,),
            in_specs=[slab, slab, slab, pl.BlockSpec((1, tile, c), lambda i, q_ref: (q_ref[0], i, 0)),
                      pl.BlockSpec((n, tile, c), lambda i, q_ref: (0, i, 0))] + [anywhere] * n_prev,
            out_specs=[slab, slab, slab, slab]),
        input_output_aliases={6 + j: j for j in range(n_prev)},
        compiler_params=_params(("arbitrary",)))(chip, w, m, v, p, recv, *prev)


def _cmul(ar, ai, br, bi):
    return ar * br - ai * bi, ar * bi + ai * br


def _cpow(ar, ai, n):
    pr, pi = jnp.ones_like(ar), jnp.zeros_like(ar)
    br, bi = ar, ai
    while n:
        if n & 1:
            pr, pi = _cmul(pr, pi, br, bi)
        n >>= 1
        if n:
            br, bi = _cmul(br, bi, br, bi)
    return pr, pi


def _s5_scan_into(x_ref, ar1, ai1, ns, fin_ref, hin_ref, reverse, paired=None):
    st = ar1.shape[1]
    ar = jnp.broadcast_to(ar1, (N_SEG, st))
    ai = jnp.broadcast_to(ai1, (N_SEG, st))
    zero = jnp.zeros((N_SEG, st), F32)

    def slab(k):
        if isinstance(k, int):
            return pl.ds(k * N_SEG, N_SEG)
        return pl.ds(pl.multiple_of(k * N_SEG, N_SEG), N_SEG)

    def pass1(j, carry):
        hr, hi = carry
        k = ns - 1 - j if reverse else j
        nr, ni = _cmul(ar, ai, hr, hi)
        return nr + x_ref[slab(k), :st], ni + x_ref[slab(k), st:]

    fr, fi = lax.fori_loop(0, ns, pass1, (zero, zero))
    fin_ref[:, :st] = fr
    fin_ref[:, st:] = fi
    pr, pi = _cpow(ar1, ai1, ns)
    order = list(range(N_SEG - 1, -1, -1)) if reverse else list(range(N_SEG))
    hin_ref[order[0]:order[0] + 1, :] = jnp.zeros((1, 2 * st), F32)
    for a_, b_ in zip(order[:-1], order[1:]):
        cr, ci = _cmul(pr, pi, hin_ref[a_:a_ + 1, :st], hin_ref[a_:a_ + 1, st:])
        hin_ref[b_:b_ + 1, :st] = cr + fin_ref[a_:a_ + 1, :st]
        hin_ref[b_:b_ + 1, st:] = ci + fin_ref[a_:a_ + 1, st:]

    def step2(k, hr, hi):
        nr, ni = _cmul(ar, ai, hr, hi)
        nr = nr + x_ref[slab(k), :st]
        ni = ni + x_ref[slab(k), st:]
        x_ref[slab(k), :st] = nr
        x_ref[slab(k), st:] = ni
        return nr, ni

    if paired is None:
        def pass2(j, carry):
            return step2(ns - 1 - j if reverse else j, *carry)

        lax.fori_loop(0, ns, pass2, (hin_ref[:, :st], hin_ref[:, st:]))
        return None
    p_ref, p_edge_ref, shift = paired

    def pass2_paired(j, carry):
        hr, hi, acr, aci = carry
        k = ns - 1 - j if reverse else j
        nr, ni = step2(k, hr, hi)
        p_r, p_i = p_ref[slab(k + shift), :st], p_ref[slab(k + shift), st:]
        return nr, ni, acr + nr * p_r + ni * p_i, aci + ni * p_r - nr * p_i

    hr, hi, acr, aci = lax.fori_loop(0, ns - 1, pass2_paired, (hin_ref[:, :st], hin_ref[:, st:], zero, zero))
    nr, ni = step2(0 if reverse else ns - 1, hr, hi)
    p_r, p_i = p_edge_ref[:, :st], p_edge_ref[:, st:]
    return acr + nr * p_r + ni * p_i, aci + ni * p_r - nr * p_i


def _s5_specs(r, ch, st):
    u_spec = pl.BlockSpec((r, ch), lambda j: (0, j // 2))
    w_spec = pl.BlockSpec((1, ch, 2 * st), lambda j: (j, 0, 0))
    c_spec = pl.BlockSpec((1, 2 * st, ch), lambda j: (j, 0, 0))
    a_spec = pl.BlockSpec((1, 2, st), lambda j: (j, 0, 0))
    return u_spec, w_spec, c_spec, a_spec


def _s5_fwd(up, w, c, a, rev, name):
    r, s = up.shape
    nh, ch, st2 = w.shape
    st = st2 // 2
    ns = r // N_SEG
    nb = r // N_DEV
    u_spec, w_spec, c_spec, a_spec = _s5_specs(r, ch, st)

    def body(u_ref, w_ref, c_ref, a_ref, y_ref, x, fin, hin):
        j = pl.program_id(0)
        w_b = w_ref[0].astype(MXU_DTYPE)
        c_b = c_ref[0].astype(MXU_DTYPE)
        for rb in range(N_DEV):
            rows = slice(rb * nb, (rb + 1) * nb)
            x[rows, :] = jnp.dot(u_ref[rows, :].astype(MXU_DTYPE), w_b, preferred_element_type=F32)
        _s5_scan_into(x, a_ref[0, 0:1, :], a_ref[0, 1:2, :], ns, fin, hin, rev)
        for rb in range(N_DEV):
            rows = slice(rb * nb, (rb + 1) * nb)
            yb = jnp.dot(x[rows, :].astype(MXU_DTYPE), c_b, preferred_element_type=F32)

            @pl.when(j % 2 == 0)
            def _():
                y_ref[rows, :] = yb

            @pl.when(j % 2 == 1)
            def _():
                y_ref[rows, :] += yb

    small = pltpu.VMEM((N_SEG, st2), F32)
    return pl.pallas_call(
        body, name=name, grid=(nh,), in_specs=[u_spec, w_spec, c_spec, a_spec],
        out_specs=pl.BlockSpec((r, ch), lambda j: (0, j // 2)), out_shape=_sds((r, s), F32),
        scratch_shapes=[pltpu.VMEM((r, st2), F32), small, small],
        compiler_params=_params(("arbitrary",)))(up, w, c, a)


def _s5_bwd(up, dyp, w, c, a, rev, name):
    r, s = up.shape
    nh, ch, st2 = w.shape
    st = st2 // 2
    ns = r // N_SEG
    nb = r // N_DEV
    u_spec, w_spec, c_spec, a_spec = _s5_specs(r, ch, st)
    nt = (((1,), (1,)), ((), ()))
    tn = (((0,), (0,)), ((), ()))

    def body(u_ref, dy_ref, w_ref, c_ref, a_ref, du_ref, dw_ref, dc_ref, da_ref, h, g, fin, sin_, ein):
        j = pl.program_id(0)
        w_b = w_ref[0].astype(MXU_DTYPE)
        c_b = c_ref[0].astype(MXU_DTYPE)
        for rb in range(N_DEV):
            rows = slice(rb * nb, (rb + 1) * nb)
            h[rows, :] = jnp.dot(u_ref[rows, :].astype(MXU_DTYPE), w_b, preferred_element_type=F32)
        ar1, ai1 = a_ref[0, 0:1, :], a_ref[0, 1:2, :]
        _s5_scan_into(h, ar1, ai1, ns, fin, sin_, rev)
        dc = jnp.zeros((st2, ch), F32)
        for rb in range(N_DEV):
            rows = slice(rb * nb, (rb + 1) * nb)
            dyb = dy_ref[rows, :].astype(MXU_DTYPE)
            g[rows, :] = lax.dot_general(dyb, c_b, nt, preferred_element_type=F32)
            dc += lax.dot_general(h[rows, :].astype(MXU_DTYPE), dyb, tn, preferred_element_type=F32)
        dc_ref[0] = dc
        acr, aci = _s5_scan_into(g, ar1, -ai1, ns, fin, ein, not rev, paired=(h, sin_, 1 if rev else -1))
        da_ref[0, 0:1, :] = jnp.sum(acr, axis=0, keepdims=True)
        da_ref[0, 1:2, :] = jnp.sum(aci, axis=0, keepdims=True)
        dw = jnp.zeros((ch, st2), F32)
        for rb in range(N_DEV):
            rows = slice(rb * nb, (rb + 1) * nb)
            gb = g[rows, :].astype(MXU_DTYPE)
            dub = lax.dot_general(gb, w_b, nt, preferred_element_type=F32)
            dw += lax.dot_general(u_ref[rows, :].astype(MXU_DTYPE), gb, tn, preferred_element_type=F32)

            @pl.when(j % 2 == 0)
            def _():
                du_ref[rows, :] = dub

            @pl.when(j % 2 == 1)
            def _():
                du_ref[rows, :] += dub

        dw_ref[0] = dw

    small = pltpu.VMEM((N_SEG, st2), F32)
    big = pltpu.VMEM((r, st2), F32)
    return pl.pallas_call(
        body, name=name, grid=(nh,), in_specs=[u_spec, u_spec, w_spec, c_spec, a_spec],
        out_specs=[pl.BlockSpec((r, ch), lambda j: (0, j // 2)), w_spec, c_spec, a_spec],
        out_shape=[_sds((r, s), F32), _sds(w.shape, F32), _sds(c.shape, F32), _sds(a.shape, F32)],
        scratch_shapes=[big, big, small, small, small],
        compiler_params=_params(("arbitrary",)))(up, dyp, w, c, a)


def _rope(t, cos, sin):
    quarter = t.shape[1] // 4
    lane = lax.broadcasted_iota(jnp.int32, t.shape, 1)
    first = (lane // quarter) % 2 == 0
    partner = jnp.where(first, pltpu.roll(t, t.shape[1] - quarter, 1), pltpu.roll(t, quarter, 1))
    return t * cos + partner * sin


def _rope_t(d, cos, sin):
    quarter = d.shape[1] // 4
    ds_ = d * sin
    lane = lax.broadcasted_iota(jnp.int32, d.shape, 1)
    first = (lane // quarter) % 2 == 0
    partner = jnp.where(first, pltpu.roll(ds_, d.shape[1] - quarter, 1), pltpu.roll(ds_, quarter, 1))
    return d * cos + partner


def _chunk_of_step(s, nch, ncc, rev):
    if not rev:
        return s
    return jnp.where(s < ncc, ncc - 1 - s, nch + ncc - 1 - s)


def _heads_per_step(heads, dk, dv, q_off):
    v_off = q_off + 2 * heads * dk
    for hpg in range(heads, 0, -1):
        if heads % hpg == 0 and q_off % (hpg * dk) == 0:
            piece = math.gcd(v_off, hpg * dv)
            if piece % dv == 0:
                return hpg, piece
    return 1, dv


def _v_specs(hpg, dv, piece, v_off, ch, chunk_of):
    n_pieces = hpg * dv // piece
    return [pl.BlockSpec((ch, piece), functools.partial(
        lambda h, s, p: (chunk_of(s), v_off // piece + h * n_pieces + p), p=p)) for p in range(n_pieces)]


def _v_of_head(v_refs, hl, dv, piece):
    lo = (hl * dv) % piece
    return v_refs[(hl * dv) // piece][:, lo:lo + dv]


def _ret_fwd(hm, cos, sin, decay, wend, win, gch, heads, dk, dv, q_off, ncc, rev, name):
    r = hm.shape[0]
    ch = RET_CHUNK
    nch = r // ch
    t_rows = r - ncc * ch
    hpg, piece = _heads_per_step(heads, dk, dv, q_off)
    qb, kb = q_off // (hpg * dk), (q_off + heads * dk) // (hpg * dk)
    q_scale = dk ** -0.5
    nt = (((1,), (1,)), ((), ()))
    tn = (((0,), (0,)), ((), ()))
    cof = lambda s: _chunk_of_step(s, nch, ncc, rev)
    v_specs = _v_specs(hpg, dv, piece, q_off + 2 * heads * dk, ch, cof)
    nv = len(v_specs)

    def body(q_ref, k_ref, *refs):
        v_refs = refs[:nv]
        cos_ref, sin_ref, dec_ref, we_ref, wi_ref, g_ref, o_ref, sin_out, st = refs[nv:]
        s = pl.program_id(1)

        @pl.when(s == 0)
        def _():
            st[...] = jnp.zeros_like(st)

        cos_, sin_ = cos_ref[...], sin_ref[...]
        for hl in range(hpg):
            ks, vs = slice(hl * dk, (hl + 1) * dk), slice(hl * dv, (hl + 1) * dv)
            q = _rope(q_ref[:, ks], cos_, sin_) * q_scale
            k = _rope(k_ref[:, ks], cos_, sin_)
            v = _v_of_head(v_refs, hl, dv, piece).astype(MXU_DTYPE)
            s_cur = st[hl]
            sin_out[hl, 0] = s_cur
            kw = (k * we_ref[hl]).astype(MXU_DTYPE)
            qw = (q * wi_ref[hl]).astype(MXU_DTYPE)
            scores = lax.dot_general(q.astype(MXU_DTYPE), k.astype(MXU_DTYPE), nt,
                                     preferred_element_type=F32) * dec_ref[hl]
            o_ref[:, vs] = (jnp.dot(scores.astype(MXU_DTYPE), v, preferred_element_type=F32)
                            + jnp.dot(qw, s_cur.astype(MXU_DTYPE), preferred_element_type=F32))
            st[hl] = g_ref[hl] * s_cur + lax.dot_general(kw, v, tn, preferred_element_type=F32)

    tab = lambda w: pl.BlockSpec((hpg, ch, w), lambda h, s: (h, 0, 0))
    return pl.pallas_call(
        body, name=name, grid=(heads // hpg, nch),
        in_specs=[pl.BlockSpec((ch, hpg * dk), lambda h, s: (cof(s), qb + h)),
                  pl.BlockSpec((ch, hpg * dk), lambda h, s: (cof(s), kb + h))] + v_specs +
                 [pl.BlockSpec((ch, dk), lambda h, s: (cof(s), 0)),
                  pl.BlockSpec((ch, dk), lambda h, s: (cof(s), 0)),
                  tab(ch), tab(dk), tab(dk), tab(dv)],
        out_specs=[pl.BlockSpec((ch, hpg * dv), lambda h, s: (jnp.maximum(cof(s) - ncc, 0) if not rev
                                                               else jnp.where(s < ncc, nch - ncc - 1, cof(s) - ncc), h)),
                   pl.BlockSpec((hpg, 1, dk, dv), lambda h, s: (h, s, 0, 0))],
        out_shape=[_sds((t_rows, heads * dv), F32), _sds((heads, nch, dk, dv), F32)],
        scratch_shapes=[pltpu.VMEM((hpg, dk, dv), F32)],
        compiler_params=_params(("parallel", "arbitrary")))(hm, hm, *([hm] * nv), cos, sin, decay, wend, win, gch)


def _ret_bwd(hm, cos, sin, decay, wend, win, gch, s_in, do, heads, dk, dv, q_off, ncc, rev, name):
    r = hm.shape[0]
    ch = RET_CHUNK
    nch = r // ch
    hpg, piece = _heads_per_step(heads, dk, dv, q_off)
    qb, kb = q_off // (hpg * dk), (q_off + heads * dk) // (hpg * dk)
    q_scale = dk ** -0.5
    nt = (((1,), (1,)), ((), ()))
    tn = (((0,), (0,)), ((), ()))
    cof = lambda rr: _chunk_of_step(nch - 1 - rr, nch, ncc, rev)
    v_specs = _v_specs(hpg, dv, piece, q_off + 2 * heads * dk, ch, cof)
    nv = len(v_specs)

    def body(q_ref, k_ref, *refs):
        v_refs = refs[:nv]
        (cos_ref, sin_ref, dec_ref, we_ref, wi_ref, g_ref, sin_ref2, do_ref,
         dq_ref, dk_ref, dv_ref, ddec_ref, dwe_ref, dwi_ref, dg_ref, dst) = refs[nv:]
        rr = pl.program_id(1)
        n = cof(rr)

        @pl.when(rr == 0)
        def _():
            dst[...] = jnp.zeros_like(dst)
            ddec_ref[...] = jnp.zeros_like(ddec_ref)
            dwe_ref[...] = jnp.zeros_like(dwe_ref)
            dwi_ref[...] = jnp.zeros_like(dwi_ref)
            dg_ref[...] = jnp.zeros_like(dg_ref)

        cos_, sin_ = cos_ref[...], sin_ref[...]
        for hl in range(hpg):
            ks, vs = slice(hl * dk, (hl + 1) * dk), slice(hl * dv, (hl + 1) * dv)
            q = _rope(q_ref[:, ks], cos_, sin_) * q_scale
            k = _rope(k_ref[:, ks], cos_, sin_)
            v = _v_of_head(v_refs, hl, dv, piece).astype(MXU_DTYPE)
            qb_, kb_ = q.astype(MXU_DTYPE), k.astype(MXU_DTYPE)
            kw = (k * we_ref[hl]).astype(MXU_DTYPE)
            qw = (q * wi_ref[hl]).astype(MXU_DTYPE)
            sraw = lax.dot_general(qb_, kb_, nt, preferred_element_type=F32)
            scores = (sraw * dec_ref[hl]).astype(MXU_DTYPE)
            d_o = jnp.where(n >= ncc, do_ref[:, vs], 0.0).astype(MXU_DTYPE)
            s_n = sin_ref2[hl, 0]
            s_nb = s_n.astype(MXU_DTYPE)
            ds1 = dst[hl]
            ds1b = ds1.astype(MXU_DTYPE)
            dsc = lax.dot_general(d_o, v, nt, preferred_element_type=F32)
            dsr = (dsc * dec_ref[hl]).astype(MXU_DTYPE)
            ddec_ref[hl] += dsc * sraw
            t1 = lax.dot_general(d_o, s_nb, nt, preferred_element_type=F32)
            dq_r = jnp.dot(dsr, kb_, preferred_element_type=F32) + t1 * wi_ref[hl]
            dwi_ref[hl] += t1 * q
            t2 = lax.dot_general(v, ds1b, nt, preferred_element_type=F32)
            dk_r = lax.dot_general(dsr, qb_, tn, preferred_element_type=F32) + t2 * we_ref[hl]
            dwe_ref[hl] += t2 * k
            dv_ref[:, vs] = (lax.dot_general(scores, d_o, tn, preferred_element_type=F32)
                             + jnp.dot(kw, ds1b, preferred_element_type=F32))
            dg_ref[hl] += ds1 * s_n
            dst[hl] = g_ref[hl] * ds1 + lax.dot_general(qw, d_o, tn, preferred_element_type=F32)
            dq_ref[:, ks] = _rope_t(dq_r, cos_, sin_) * q_scale
            dk_ref[:, ks] = _rope_t(dk_r, cos_, sin_)

    tab = lambda w: pl.BlockSpec((hpg, ch, w), lambda h, rr: (h, 0, 0))
    return pl.pallas_call(
        body, name=name, grid=(heads // hpg, nch),
        in_specs=[pl.BlockSpec((ch, hpg * dk), lambda h, rr: (cof(rr), qb + h)),
                  pl.BlockSpec((ch, hpg * dk), lambda h, rr: (cof(rr), kb + h))] + v_specs +
                 [pl.BlockSpec((ch, dk), lambda h, rr: (cof(rr), 0)),
                  pl.BlockSpec((ch, dk), lambda h, rr: (cof(rr), 0)),
                  tab(ch), tab(dk), tab(dk), tab(dv),
                  pl.BlockSpec((hpg, 1, dk, dv), lambda h, rr: (h, nch - 1 - rr, 0, 0)),
                  pl.BlockSpec((ch, hpg * dv), lambda h, rr: (jnp.maximum(cof(rr) - ncc, 0), h))],
        out_specs=[pl.BlockSpec((ch, hpg * dk), lambda h, rr: (cof(rr), h)),
                   pl.BlockSpec((ch, hpg * dk), lambda h, rr: (cof(rr), h)),
                   pl.BlockSpec((ch, hpg * dv), lambda h, rr: (cof(rr), h)),
                   tab(ch), tab(dk), tab(dk), tab(dv)],
        out_shape=[_sds((r, heads * dk), F32), _sds((r, heads * dk), F32), _sds((r, heads * dv), F32),
                   _sds(decay.shape, F32), _sds(wend.shape, F32), _sds(win.shape, F32), _sds(gch.shape, F32)],
        scratch_shapes=[pltpu.VMEM((hpg, dk, dv), F32)],
        compiler_params=_params(("parallel", "arbitrary")))(hm, hm, *([hm] * nv), cos, sin, decay, wend, win, gch, s_in, do)


_HBM = pl.BlockSpec(memory_space=pltpu.HBM)
_MESH = pl.DeviceIdType.MESH
ALL_GATHER_COLLECTIVE_ID = 1
SIBLING_COLLECTIVE_ID = 2
CHIPS_COLLECTIVE_ID = 3


def _axis_slice(ref, axis, start, size):
    idx = [slice(None)] * len(ref.shape)
    idx[axis] = pl.ds(start, size)
    return ref.at[tuple(idx)]


def _sibling_and_chip_peers():
    x, y, c = lax.axis_index("x"), lax.axis_index("y"), lax.axis_index("c")
    return [(x, y, 1 - c), (1 - x, y, c), (x, 1 - y, c), (1 - x, 1 - y, c)]


def _launch_exchange(body, name, operand, out_shape, sems, peers_fn, collective_id, on_sequencer):
    if not on_sequencer:
        return pl.pallas_call(body, name=name, out_shape=out_shape, in_specs=[_HBM], out_specs=_HBM,
                              scratch_shapes=sems)(operand)

    def sequencer_body(in_ref, out_ref, *sem_refs):
        peers = peers_fn()
        barrier = pltpu.get_barrier_semaphore()
        for peer in peers:
            pl.semaphore_signal(barrier, inc=1, device_id=peer, device_id_type=_MESH)
        pl.semaphore_wait(barrier, len(peers))
        body(in_ref, out_ref, *sem_refs)

    return pl.kernel(sequencer_body, out_type=out_shape, name=name,
                     mesh=plsc.ScalarSubcoreMesh(axis_name="sequencer", num_cores=1), scratch_types=sems,
                     compiler_params=pltpu.CompilerParams(collective_id=collective_id))(operand)


def _all_gather(shard, axis, name, on_sequencer=False):
    m = shard.shape[axis]
    out_shape = list(shard.shape)
    out_shape[axis] = N_DEV * m

    def body(x_ref, out_ref, send_sems, recv_sems, local_sem):
        x, y, c = lax.axis_index("x"), lax.axis_index("y"), lax.axis_index("c")
        me, sibling = (x, y, c), (x, y, 1 - c)
        chips = [(1 - x, y), (x, 1 - y), (1 - x, 1 - y)]

        def block(px, py, pc):
            return _axis_slice(out_ref, axis, (4 * px + 2 * py + pc) * m, m)

        def copy(k, blk, to, src=None):
            return pltpu.make_async_remote_copy(
                src_ref=block(*blk) if src is None else src, dst_ref=block(*blk), send_sem=send_sems.at[k],
                recv_sem=recv_sems.at[k], device_id=to, device_id_type=_MESH)

        mine = pltpu.make_async_copy(x_ref, block(*me), local_sem)
        mine.start()
        first = [copy(0, me, sibling, src=x_ref)]
        first += [copy(1 + j, me, (*chip, c), src=x_ref) for j, chip in enumerate(chips)]
        for cp in first:
            cp.start()
        passed = [copy(4 + j, (*chip, c), sibling) for j, chip in enumerate(chips)]
        for j, chip in enumerate(chips):
            copy(1 + j, (*chip, c), me).wait_recv()
            passed[j].start()
        copy(0, sibling, me).wait_recv()
        for j, chip in enumerate(chips):
            copy(4 + j, (*chip, 1 - c), me).wait_recv()
        for cp in first + passed:
            cp.wait_send()
        mine.wait()

    return _launch_exchange(
        body, name, shard, _sds(out_shape, shard.dtype),
        [pltpu.SemaphoreType.DMA((7,)), pltpu.SemaphoreType.DMA((7,)), pltpu.SemaphoreType.DMA(())],
        _sibling_and_chip_peers, ALL_GATHER_COLLECTIVE_ID, on_sequencer)


def _rs_sibling(g, axis, name, on_sequencer=False):
    m = g.shape[axis] // N_DEV
    blk_shape = list(g.shape)
    blk_shape[axis] = m
    n_chips = N_DEV // 2

    def body(g_ref, recv_ref, send_sems, recv_sems):
        x, y, c = lax.axis_index("x"), lax.axis_index("y"), lax.axis_index("c")
        sibling = (x, y, 1 - c)
        send = [pltpu.make_async_remote_copy(
            src_ref=_axis_slice(g_ref, axis, (2 * q + 1 - c) * m, m), dst_ref=recv_ref.at[q],
            send_sem=send_sems.at[q], recv_sem=recv_sems.at[q], device_id=sibling, device_id_type=_MESH)
            for q in range(n_chips)]
        for cp in send:
            cp.start()
        for cp in send:
            cp.wait_recv()
        for cp in send:
            cp.wait_send()

    return _launch_exchange(
        body, name, g, _sds([n_chips] + blk_shape, g.dtype),
        [pltpu.SemaphoreType.DMA((n_chips,)), pltpu.SemaphoreType.DMA((n_chips,))],
        lambda: _sibling_and_chip_peers()[:1], SIBLING_COLLECTIVE_ID, on_sequencer)


def _rs_chips(p, name, on_sequencer=False):
    n_peers = p.shape[0] - 1

    def body(p_ref, out_ref, send_sems, recv_sems):
        x, y, c = lax.axis_index("x"), lax.axis_index("y"), lax.axis_index("c")
        chips = [(1 - x, y), (x, 1 - y), (1 - x, 1 - y)]
        send = [pltpu.make_async_remote_copy(
            src_ref=p_ref.at[2 * cx + cy], dst_ref=out_ref.at[j], send_sem=send_sems.at[j],
            recv_sem=recv_sems.at[j], device_id=(cx, cy, c), device_id_type=_MESH)
            for j, (cx, cy) in enumerate(chips)]
        for cp in send:
            cp.start()
        for cp in send:
            cp.wait_recv()
        for cp in send:
            cp.wait_send()

    return _launch_exchange(
        body, name, p, _sds((n_peers,) + p.shape[1:], p.dtype),
        [pltpu.SemaphoreType.DMA((n_peers,)), pltpu.SemaphoreType.DMA((n_peers,))],
        lambda: _sibling_and_chip_peers()[1:], CHIPS_COLLECTIVE_ID, on_sequencer)


def _reduce_scatter(g, axis, name):
    sib = _rs_sibling(g, axis, name + "_d2d", on_sequencer=True)
    p = _pair_sum(g, sib, axis, name + "_pair")
    return p, _rs_chips(p, name + "_ici", on_sequencer=True)


def _s5_tables(lam_re, lam_im, log_step, b_re, b_im, c_re, c_im):
    nd, g, p, cg = b_re.shape
    step = jnp.exp(log_step)[..., None]
    mag = jnp.exp(lam_re * step)
    a_re, a_im = mag * jnp.cos(lam_im * step), mag * jnp.sin(lam_im * step)
    den = lam_re * lam_re + lam_im * lam_im
    num_re, num_im = a_re - 1.0, a_im
    k_re = (num_re * lam_re + num_im * lam_im) / den
    k_im = (num_im * lam_re - num_re * lam_im) / den
    bb_re = k_re[..., None] * b_re - k_im[..., None] * b_im
    bb_im = k_re[..., None] * b_im + k_im[..., None] * b_re
    gt = g // SSM_TILE_GROUPS
    hg = SSM_HALF_GROUPS
    eye = jnp.eye(SSM_TILE_GROUPS, dtype=F32).reshape(SSM_TILE_GROUPS, 2, hg)

    def pack_b(bb):
        w = jnp.einsum("djhqpc,ghq->djhgcqp", bb.reshape(nd, gt, 2, hg, p, cg), eye)
        return w.reshape(nd, gt * 2, SSM_TILE_GROUPS * cg, hg * p)

    def pack_c(cc):
        w = jnp.einsum("djhqcp,ghq->djhqpgc", cc.reshape(nd, gt, 2, hg, cg, p), eye)
        return w.reshape(nd, gt * 2, hg * p, SSM_TILE_GROUPS * cg)

    a = jnp.stack([a_re.reshape(nd, gt * 2, hg * p), a_im.reshape(nd, gt * 2, hg * p)], axis=2)
    w = jnp.concatenate([pack_b(bb_re), pack_b(bb_im)], axis=-1)
    c = jnp.concatenate([pack_c(c_re), -pack_c(c_im)], axis=-2)
    return w, c, a


def _ret_tables(decay_logit, dk, dv):
    ch = RET_CHUNK
    nd, h = decay_logit.shape
    lg = jax.nn.log_sigmoid(decay_logit)[:, :, None]
    pos = jnp.arange(ch, dtype=F32)
    fwd_diff = pos[:, None] - pos[None, :]
    diff = jnp.stack([fwd_diff, -fwd_diff])[:, None]
    mask = jnp.stack([fwd_diff >= 0, -fwd_diff > 0])[:, None]
    end_pos = jnp.stack([ch - 1.0 - pos, pos])[:, None]
    in_pos = jnp.stack([pos + 1.0, ch - pos])[:, None]
    w_end = jnp.exp(lg * end_pos)
    w_in = jnp.exp(lg * in_pos)
    decay = jnp.where(mask, jnp.exp(lg[..., None] * jnp.where(mask, diff, 0.0)), 0.0)
    g_chunk = jnp.exp(lg[..., 0] * ch)
    return (decay, jnp.broadcast_to(w_end[..., None], (nd, h, ch, dk)), jnp.broadcast_to(w_in[..., None], (nd, h, ch, dk)),
            jnp.broadcast_to(g_chunk[..., None, None], (nd, h, dk, dv)))


def _rope_tables(t_rows, ncc, dk):
    quarter = dk // 4
    idx = np.arange(t_rows)
    row, col = idx // GRID_W, idx % GRID_W
    inv = ROPE_BASE ** (-np.arange(quarter, dtype=np.float32) / quarter)
    ang_r = row.astype(np.float32)[:, None] * inv
    ang_c = col.astype(np.float32)[:, None] * inv
    ang_r, ang_c = jnp.asarray(ang_r, F32), jnp.asarray(ang_c, F32)
    cos = jnp.concatenate([jnp.cos(ang_r), jnp.cos(ang_r), jnp.cos(ang_c), jnp.cos(ang_c)], axis=1)
    sin = jnp.concatenate([-jnp.sin(ang_r), jnp.sin(ang_r), -jnp.sin(ang_c), jnp.sin(ang_c)], axis=1)
    n_ctx = ncc * RET_CHUNK
    cos = jnp.concatenate([jnp.ones((n_ctx, dk), F32), cos], axis=0)
    sin = jnp.concatenate([jnp.zeros((n_ctx, dk), F32), sin], axis=0)
    return cos, sin


def _to_scan_layout(ctx_rows, lat_rows, rev):
    u = jnp.concatenate([lat_rows, ctx_rows] if rev else [ctx_rows, lat_rows], axis=0)
    r, w = u.shape
    return u.reshape(N_SEG, r // N_SEG, w).transpose(1, 0, 2).reshape(r, w)


def _from_scan_layout(yp, n_ctx, rev):
    r, w = yp.shape
    y = yp.reshape(r // N_SEG, N_SEG, w).transpose(1, 0, 2).reshape(r, w)
    return (y[r - n_ctx:], y[:r - n_ctx]) if rev else (y[:n_ctx], y[n_ctx:])


def _pack(parts, width):
    rows = []
    for p in parts:
        flat = p.reshape(-1).astype(F32)
        n = flat.shape[0]
        rows.append(jnp.pad(flat, (0, -n % (SUBLANE * width))).reshape(-1, width))
    return jnp.concatenate(rows, axis=0)


def _packed_rows(n, width):
    return -(-n // (SUBLANE * width)) * SUBLANE


def _unpack(flat2d, shapes):
    width = flat2d.shape[1]
    out, row = [], 0
    for shp in shapes:
        n = int(np.prod(shp))
        nr = _packed_rows(n, width)
        out.append(flat2d[row:row + nr].reshape(-1)[:n].reshape(shp))
        row += nr
    return out


def kernel(x, c, ctx, c_ctx, ada_w, ada_b, norm_g, ffn_w_in, ffn_w_out, mix_w_in, ssm_lam_re, ssm_lam_im, ssm_log_step, ssm_b_re, ssm_b_im, ssm_c_re, ssm_c_im, ssm_d, ssm_glu_w, ret_decay_logit, ret_w_proj, mix_w_out, loss_target, m_c_ctx, m_ada_w, m_ada_b, m_norm_g, m_ffn_w_in, m_ffn_w_out, m_mix_w_in, m_ssm_lam_re, m_ssm_lam_im, m_ssm_log_step, m_ssm_b_re, m_ssm_b_im, m_ssm_c_re, m_ssm_c_im, m_ssm_d, m_ssm_glu_w, m_ret_decay_logit, m_ret_w_proj, m_mix_w_out, v_c_ctx, v_ada_w, v_ada_b, v_norm_g, v_ffn_w_in, v_ffn_w_out, v_mix_w_in, v_ssm_lam_re, v_ssm_lam_im, v_ssm_log_step, v_ssm_b_re, v_ssm_b_im, v_ssm_c_re, v_ssm_c_im, v_ssm_d, v_ssm_glu_w, v_ret_decay_logit, v_ret_w_proj, v_mix_w_out):
    t_rows, d = x.shape[1], x.shape[2]
    n_ctx = ctx.shape[1]
    r = n_ctx + t_rows
    ssm_w = ssm_d.shape[1]
    heads = ret_decay_logit.shape[2]
    mi = mix_w_in.shape[2] * N_DEV
    dk = (mi - ssm_w - 2 * d) // (6 * heads)
    dv = 2 * dk
    qk_w, v_w = heads * dk, heads * dv
    q_off = ssm_w
    ncc = n_ctx // RET_CHUNK
    tile = n_ctx
    nct = 1
    wide_tile = _tile(n_ctx, 128, 16)
    assert r % (N_SEG * SUBLANE) == 0 and n_ctx % RET_CHUNK == 0 and t_rows % tile == 0
    me = 4 * lax.axis_index("x") + 2 * lax.axis_index("y") + lax.axis_index("c")
    g_off = ssm_w + 2 * qk_w + v_w
    gs_off = g_off + v_w

    ng_cols = norm_g.shape[2]
    small0 = _pack([c[0], norm_g[0]], d)
    small0_all = _all_gather(small0, 0, "ag_cond")

    bf = lambda w: w.astype(BF16)
    small0_all, sh_in1, sh_out1, sh_mix = lax.optimization_barrier(
        (small0_all, bf(ffn_w_in[0, 0]), bf(ffn_w_out[0, 0]), bf(mix_w_in[0])))
    small0_all = small0_all.reshape(N_DEV, -1)
    half_k = d // 2
    w_in1_top = _all_gather(sh_in1[:half_k], 1, "ag_ffn1_in_top", on_sequencer=True)
    w_in1_bot = _all_gather(sh_in1[half_k:], 1, "ag_ffn1_in_bot", on_sequencer=True)
    w_out1 = _all_gather(sh_out1, 0, "ag_ffn1_out", on_sequencer=True)
    w_mix = _all_gather(sh_mix, 1, "ag_mix_in", on_sequencer=True)
    w_glu = _all_gather(bf(ssm_glu_w[0]), 1, "ag_glu", on_sequencer=True)
    w_rp = _all_gather(bf(ret_w_proj[0]), 0, "ag_ret_proj", on_sequencer=True)
    w_mo = _all_gather(bf(mix_w_out[0]), 0, "ag_mix_out", on_sequencer=True)
    w_in2 = _all_gather(bf(ffn_w_in[0, 1]), 1, "ag_ffn2_in", on_sequencer=True)
    w_out2 = _all_gather(bf(ffn_w_out[0, 1]), 0, "ag_ffn2_out", on_sequencer=True)

    ng_at = _packed_rows(d, d) * d
    c_all = small0_all[:, :d]
    g_full = small0_all[:, ng_at:ng_at + 6 * ng_cols].reshape(N_DEV, 6, ng_cols).transpose(1, 0, 2).reshape(6, d)
    g6 = g_full.reshape(6, 1, d)
    cc = jnp.concatenate([c_all, c_ctx[None, :], jnp.zeros((2 * SUBLANE - N_DEV - 1, d), F32)], axis=0)
    sc = _silu_rows(cc, "ada_silu")
    na = ada_w.shape[2]
    a_loc = _mm(sc, ada_w[0], "nn", F32, "ada_fwd", tm=16, tn=na, tk=512)
    a_all = _all_gather(a_loc, 0, "ag_ada").reshape(N_DEV, 2 * SUBLANE, na)
    ada_x = lax.dynamic_index_in_dim(a_all, me, axis=1, keepdims=False).reshape(9 * d) + ada_b[0]
    ada_c = a_all[:, N_DEV, :].reshape(9 * d) + ada_b[0]
    mods = jnp.stack([ada_c.reshape(9, d), ada_x.reshape(9, d)]).reshape(18, 1, d)

    xin = jnp.concatenate([ctx[0], x[0]], axis=0)
    u1 = _ada_pre_fwd(xin, g6, mods, 0, 0, nct, tile, "pre1")
    h1_top = _mm_k_part(u1, w_in1_top, 0, 2, None, F32, "ffn1_in_top", tm=544)
    h1 = _mm_k_part(u1, w_in1_bot, 1, 2, h1_top, BF16, "ffn1_in_bot", tm=544)
    w_in1 = jnp.concatenate([w_in1_top, w_in1_bot], axis=0)
    a1 = _swiglu_fwd(h1, wide_tile, "swiglu1")
    o1 = _mm(a1, w_out1, "nn", F32, "ffn1_out", tm=544, tn=1024, tk=2816)
    x1 = _ada_post_fwd(xin, o1, g6, mods, 1, 0, 0.5, nct, tile, "post1")
    u2 = _ada_pre_fwd(x1, g6, mods, 2, 1, nct, tile, "pre2")
    hm = _mm(u2, w_mix, "nn", F32, "mix_in", tm=544, tn=1024)

    us_ctx, us_lat = hm[:n_ctx, :ssm_w], hm[n_ctx:, :ssm_w]
    dskip = ssm_d.reshape(1, 1, ssm_w)
    s5_prm = (ssm_lam_re[0], ssm_lam_im[0], ssm_log_step[0], ssm_b_re[0], ssm_b_im[0], ssm_c_re[0], ssm_c_im[0])
    s5_tabs_both, s5_vjp = jax.vjp(_s5_tables, *s5_prm)
    s5_tabs, ups, y_dirs = [], [], []
    for dr in range(2):
        tabs = tuple(t[dr] for t in s5_tabs_both)
        up = _to_scan_layout(us_ctx, us_lat, dr == 1)
        yp = _s5_fwd(up, *tabs, dr == 1, "s5_fwd%d" % dr)
        s5_tabs.append(tabs)
        ups.append(up)
        y_dirs.append(_from_scan_layout(yp, n_ctx, dr == 1)[1])
    a_ssm = _ssm_out_fwd(y_dirs[0], y_dirs[1], hm, dskip, nct, tile, "ssm_out")
    gab = _mm(a_ssm, w_glu, "nn", F32, "glu", tm=512, tn=2048, tk=ssm_w)

    cos, sin = _rope_tables(t_rows, ncc, dk)
    ret_tabs_both, ret_vjp = jax.vjp(functools.partial(_ret_tables, dk=dk, dv=dv), ret_decay_logit[0])
    ret_tabs, o_dirs, s_ins = [], [], []
    for dr in range(2):
        tabs = tuple(t[dr] for t in ret_tabs_both)
        o_d, s_in = _ret_fwd(hm, cos, sin, *tabs, heads, dk, dv, q_off, ncc, dr == 1, "ret_fwd%d" % dr)
        ret_tabs.append(tabs)
        o_dirs.append(o_d)
        s_ins.append(s_in)
    ret_in = _ret_gate_fwd(o_dirs[0], o_dirs[1], hm, g_off, heads, dv, nct, tile, "ret_gate")
    rb = _mm(ret_in, w_rp, "nn", F32, "ret_proj", tm=512, tn=d, tk=v_w)
    merged = _merge_fwd(gab, rb, hm, gs_off, nct, tile, "merge")
    mix = _mm(merged, w_mo, "nn", F32, "mix_out", tm=512, tn=d, tk=d)
    x1x = x1[n_ctx:]
    x2 = _ada_post_fwd(x1x, mix, g6, mods, 3, 1, 1.0, 0, tile, "post2")
    u3 = _ada_pre_fwd(x2, g6, mods, 4, 2, 0, tile, "pre3")
    h3 = _mm(u3, w_in2, "nn", BF16, "ffn2_in", tm=512, tn=1024)
    a3 = _swiglu_fwd(h3, wide_tile, "swiglu2")
    o3 = _mm(a3, w_out2, "nn", F32, "ffn2_out", tm=512, tn=1024, tk=2816)
    x3 = _ada_post_fwd(x2, o3, g6, mods, 5, 2, 0.5, 0, tile, "post3")
    dy, lcols = _loss_grad(x3, loss_target[0], tile, "loss")
    loss_part = (0.5 * jnp.sum(lcols) / d).reshape(1)

    dg6 = [None] * 6
    dmod = {}

    def add_mod(sel_rows, k, val):
        for sel, row in sel_rows:
            dmod[(sel, k)] = dmod.get((sel, k), 0.0) + val[row, 0]

    both, lat = [(0, 0), (1, 1)], [(1, 0)]

    def tie(*vals):
        return lax.optimization_barrier(vals)

    def big_update(w3d, m3d, v3d, layer, gfull, axis, name, filled=None):
        p, recv = _reduce_scatter(gfull, axis, "rs_" + name)
        return _adamw_scattered(w3d, m3d, v3d, layer, p, recv, "adamw_" + name, filled)

    do3, dg6[5], dgt = _ada_post_bwd(dy, o3, g6, mods, 5, 2, 0.5, 0, 1, tile, "post3_bwd")
    add_mod(lat, 8, dgt)
    gw_out2 = _mm(a3, do3, "tn", BF16, "ffn2_out_dw", tm=1408, tn=1024, tk=2176)
    do3, gw_out2 = tie(do3, gw_out2)
    up_out2 = big_update(ffn_w_out[0], m_ffn_w_out[0], v_ffn_w_out[0], 1, gw_out2, 0, "ffn2_out")
    da3 = _mm(do3, w_out2, "nt", BF16, "ffn2_out_dx", tm=512, tn=2816, tk=d)
    dh3 = _swiglu_bwd(h3, da3, wide_tile, "swiglu2_bwd")
    gw_in2 = _mm(u3, dh3, "tn", BF16, "ffn2_in_dw", tm=1024, tn=1024, tk=2176)
    dh3, gw_in2 = tie(dh3, gw_in2)
    up_in2 = big_update(ffn_w_in[0], m_ffn_w_in[0], v_ffn_w_in[0], 1, gw_in2, 1, "ffn2_in")
    du3 = _mm(dh3, w_in2, "nt", F32, "ffn2_in_dx", tm=512, tn=d, tk=1024)
    dx2, dg6[4], dsh, dsc = _ada_pre_bwd(x2, du3, dy, g6, mods, 4, 2, 0, 1, tile, "pre3_bwd")
    add_mod(lat, 6, dsh)
    add_mod(lat, 7, dsc)
    dmix, dg6[3], dgt = _ada_post_bwd(dx2, mix, g6, mods, 3, 1, 1.0, 0, 1, tile, "post2_bwd")
    add_mod(lat, 5, dgt)
    gw_mo = _mm(merged, dmix, "tn", BF16, "mix_out_dw", tm=1024, tn=1024, tk=2176)
    dmix, gw_mo = tie(dmix, gw_mo)
    up_mo = big_update(mix_w_out, m_mix_w_out, v_mix_w_out, 0, gw_mo, 0, "mix_out")
    dmerged = _mm(dmix, w_mo, "nt", F32, "mix_out_dx", tm=512, tn=d, tk=d)
    dgab, drb, dgs, dgr = _merge_bwd(gab, rb, hm, gs_off, dmerged, nct, tile, "merge_bwd")
    gw_glu = _mm(a_ssm, dgab, "tn", BF16, "glu_dw", tm=1024, tn=1024, tk=2176)
    gw_rp = _mm(ret_in, drb, "tn", BF16, "ret_proj_dw", tm=1024, tn=1024, tk=2176)
    dgab, drb, gw_glu, gw_rp = tie(dgab, drb, gw_glu, gw_rp)
    up_glu = big_update(ssm_glu_w, m_ssm_glu_w, v_ssm_glu_w, 0, gw_glu, 1, "glu")
    up_rp = big_update(ret_w_proj, m_ret_w_proj, v_ret_w_proj, 0, gw_rp, 0, "ret_proj")
    da_ssm = _mm(dgab, w_glu, "nt", F32, "glu_dx", tm=512, tn=ssm_w, tk=2 * d)
    dret_in = _mm(drb, w_rp, "nt", F32, "ret_proj_dx", tm=512, tn=v_w, tk=d)
    d_o, dg_gate = _ret_gate_bwd(o_dirs[0], o_dirs[1], hm, g_off, dret_in, heads, dv, nct, tile, "ret_gate_bwd")
    dy_ssm, dus_direct, d_dskip = _ssm_out_bwd(y_dirs[0], y_dirs[1], hm, dskip, da_ssm, nct, tile, "ssm_out_bwd")
    s5_table_grads, du_ctx, du_lat = [], [], [dus_direct]
    for dr in range(2):
        dyp = _to_scan_layout(jnp.zeros((n_ctx, ssm_w), F32), dy_ssm, dr == 1)
        if dr == 1:
            dyp, up_out2, up_in2 = tie(dyp, up_out2, up_in2)
        outs = _s5_bwd(ups[dr], dyp, *s5_tabs[dr], dr == 1, "s5_bwd%d" % dr)
        part_ctx, part_lat = _from_scan_layout(outs[0], n_ctx, dr == 1)
        du_ctx.append(part_ctx)
        du_lat.append(part_lat)
        s5_table_grads.append(outs[1:])
    dqkv, ret_table_grads = [], []
    for dr in range(2):
        if dr == 1:
            d_o, up_mo, up_glu, up_rp = tie(d_o, up_mo, up_glu, up_rp)
        outs = _ret_bwd(hm, cos, sin, *ret_tabs[dr], s_ins[dr], d_o, heads, dk, dv, q_off, ncc, dr == 1,
                        "ret_bwd%d" % dr)
        dqkv.append(outs[:3])
        ret_table_grads.append(outs[3:])
    both_dirs = lambda grads: tuple(jnp.stack([g0, g1]) for g0, g1 in zip(*grads))
    early_parts = list(s5_vjp(both_dirs(s5_table_grads))) + list(ret_vjp(both_dirs(ret_table_grads)))
    s5_names = 7
    early_shapes = [p.shape for p in early_parts]
    early_all = _all_gather(_pack(early_parts, 1024), 0, "ag_s5_grads", on_sequencer=True)
    early_sum = _sum_leading(early_all.reshape(N_DEV, -1, 1024), "sum_s5_grads")
    dus = jnp.concatenate([du_ctx[0] + du_ctx[1], du_lat[0] + du_lat[1] + du_lat[2]], axis=0)
    dhm = _assemble_dhm(dus, dqkv[0][0], dqkv[1][0], dqkv[0][1], dqkv[1][1], dqkv[0][2], dqkv[1][2],
                        dg_gate, dgs, dgr, n_ctx // wide_tile, wide_tile, "assemble_dhm")
    gw_mix = _mm(u2, dhm, "tn", BF16, "mix_in_dw", tm=1024, tn=1024, tk=2176)
    dhm, gw_mix = tie(dhm, gw_mix)
    up_mix = big_update(mix_w_in, m_mix_w_in, v_mix_w_in, 0, gw_mix, 1, "mix_in")
    du2 = _mm(dhm, w_mix, "nt", F32, "mix_in_dx", tm=544, tn=d, tk=1024)
    dx1, dg6[2], dsh, dsc = _ada_pre_bwd(x1, du2, dx2, g6, mods, 2, 1, nct, 2, tile, "pre2_bwd", dres_x_only=True)
    add_mod(both, 3, dsh)
    add_mod(both, 4, dsc)
    do1, dg6[1], dgt = _ada_post_bwd(dx1, o1, g6, mods, 1, 0, 0.5, nct, 2, tile, "post1_bwd")
    add_mod(both, 2, dgt)
    gw_out1 = _mm(a1, do1, "tn", BF16, "ffn1_out_dw", tm=1408, tn=1024, tk=2176)
    do1, gw_out1 = tie(do1, gw_out1)
    up_out1 = big_update(ffn_w_out[0], m_ffn_w_out[0], v_ffn_w_out[0], 0, gw_out1, 0, "ffn1_out", filled=up_out2)
    da1 = _mm(do1, w_out1, "nt", BF16, "ffn1_out_dx", tm=544, tn=2816, tk=d)
    dh1 = _swiglu_bwd(h1, da1, wide_tile, "swiglu1_bwd")
    dh1, up_mix, early_sum = tie(dh1, up_mix, early_sum)
    early_sums = _unpack(early_sum, early_shapes)
    gw_in1 = _mm(u1, dh1, "tn", BF16, "ffn1_in_dw", tm=1024, tn=1024, tk=2176)
    dh1, gw_in1 = tie(dh1, gw_in1)
    up_in1 = big_update(ffn_w_in[0], m_ffn_w_in[0], v_ffn_w_in[0], 0, gw_in1, 1, "ffn1_in", filled=up_in2)
    du1 = _mm(dh1, w_in1, "nt", F32, "ffn1_in_dx", tm=544, tn=d, tk=1024)
    dxin, dg6[0], dsh, dsc = _ada_pre_bwd(xin, du1, dx1, g6, mods, 0, 0, nct, 2, tile, "pre1_bwd")
    add_mod(both, 0, dsh)
    add_mod(both, 1, dsc)
    grad_x = dxin[n_ctx:][None]

    zero_d = jnp.zeros((d,), F32)
    d_ada_x = jnp.stack([dmod.get((1, k), zero_d) for k in range(9)]).reshape(9 * d)
    d_ada_c = jnp.stack([dmod.get((0, k), zero_d) for k in range(9)]).reshape(9 * d)
    dg_full = jnp.stack([g[0, 0] for g in dg6])
    small_parts = [d_ada_x, d_ada_c, dg_full, d_dskip, loss_part]
    small_shapes = [p.shape for p in small_parts]
    packed = _pack(small_parts, 1024)
    gathered = _all_gather(packed, 0, "ag_small_grads").reshape(N_DEV, -1, 1024)
    summed = _sum_leading(gathered, "sum_small_grads")
    sums = _unpack(summed, small_shapes)
    sum_dx, sum_dc, sum_dg = sums[0], sums[1], sums[2]
    loss = sums[4][0]
    grad_ada_b = (sum_dx + sum_dc)[None]
    dx_rows = gathered.reshape(N_DEV, -1)[:, :9 * d]
    col0 = me * na
    da_rows = jnp.concatenate([lax.dynamic_slice_in_dim(dx_rows, col0, na, axis=1),
                               lax.dynamic_slice_in_dim(sum_dc[None], col0, na, axis=1),
                               jnp.zeros((2 * SUBLANE - N_DEV - 1, na), F32)], axis=0)
    grad_ada_w = _mm(sc, da_rows, "tn", F32, "ada_dw", tm=512, tn=na, tk=16)
    d_sc = _mm(da_rows, ada_w[0], "nt", F32, "ada_dx", tm=16, tn=512, tk=na)
    d_sc_all = _all_gather(jnp.broadcast_to(d_sc[N_DEV:N_DEV + 1], (SUBLANE, d)), 0, "ag_dctx")
    d_sc_sum = _sum_leading(d_sc_all.reshape(N_DEV, SUBLANE, d), "sum_dctx")
    grad_c_ctx = _silu_grad_rows(jnp.broadcast_to(c_ctx[None], (SUBLANE, d)), d_sc_sum, "ctx_silu_bwd")[0]
    grad_norm_g = lax.dynamic_slice_in_dim(sum_dg, me * ng_cols, ng_cols, axis=1)[None]

    upd = {}
    upd["ffn_w_in"] = [o[None] for o in up_in1]
    upd["ffn_w_out"] = [o[None] for o in up_out1]
    upd["mix_w_in"] = list(up_mix)
    upd["ssm_glu_w"] = list(up_glu)
    upd["ret_w_proj"] = list(up_rp)
    upd["mix_w_out"] = list(up_mo)
    upd["ada_w"] = [o[None] for o in _adamw(ada_w[0], m_ada_w[0], v_ada_w[0], grad_ada_w[None], "adamw_ada_w")]

    small_names = ["c_ctx", "ada_b", "norm_g", "ssm_lam_re", "ssm_lam_im", "ssm_log_step", "ssm_b_re", "ssm_b_im",
                   "ssm_c_re", "ssm_c_im", "ssm_d", "ret_decay_logit"]
    small_w = [c_ctx, ada_b, norm_g, ssm_lam_re, ssm_lam_im, ssm_log_step, ssm_b_re, ssm_b_im, ssm_c_re, ssm_c_im,
               ssm_d, ret_decay_logit]
    small_m = [m_c_ctx, m_ada_b, m_norm_g, m_ssm_lam_re, m_ssm_lam_im, m_ssm_log_step, m_ssm_b_re, m_ssm_b_im,
               m_ssm_c_re, m_ssm_c_im, m_ssm_d, m_ret_decay_logit]
    small_v = [v_c_ctx, v_ada_b, v_norm_g, v_ssm_lam_re, v_ssm_lam_im, v_ssm_log_step, v_ssm_b_re, v_ssm_b_im,
               v_ssm_c_re, v_ssm_c_im, v_ssm_d, v_ret_decay_logit]
    small_g = [grad_c_ctx, grad_ada_b, grad_norm_g] + [s[None] for s in early_sums[:s5_names]] + \
              [sums[3].reshape(ssm_d.shape), early_sums[s5_names][None]]
    shapes = [w.shape for w in small_w]
    res = _adamw(_pack(small_w, 1024), _pack(small_m, 1024), _pack(small_v, 1024), _pack(small_g, 1024)[None],
                 "adamw_small")
    small_out = [_unpack(o, shapes) for o in res]
    for i, nm in enumerate(small_names):
        upd[nm] = [small_out[kind][i] for kind in range(4)]

    order = ["c_ctx", "ada_w", "ada_b", "norm_g", "ffn_w_in", "ffn_w_out", "mix_w_in", "ssm_lam_re", "ssm_lam_im",
             "ssm_log_step", "ssm_b_re", "ssm_b_im", "ssm_c_re", "ssm_c_im", "ssm_d", "ssm_glu_w", "ret_decay_logit",
             "ret_w_proj", "mix_w_out"]
    outs = [loss, grad_x]
    for kind in range(4):
        outs += [upd[nm][kind] for nm in order]
    return tuple(outs)
```

```python
import functools
import math

import jax
import jax.numpy as jnp
import numpy as np
from jax import lax
from jax.experimental import pallas as pl
from jax.experimental.pallas import tpu as pltpu
from jax.experimental.pallas import tpu_sc as plsc

F32 = jnp.float32
BF16 = jnp.bfloat16
MXU_DTYPE = jnp.bfloat16
MESH_AXES = ("x", "y", "c")
N_DEV = 8
V7X_VMEM_LIMIT_BYTES = 56 * 1024 * 1024
LANE = 128
SUBLANE = 8

GRID_W = 64
RET_CHUNK = 128
ROPE_BASE = 10000.0
NORM_EPS = 1e-6
ADAM_LR = 0.001
ADAM_B1 = 0.9
ADAM_B2 = 0.999
ADAM_EPS = 1e-08
ADAM_WD = 0.01
ADAM_STEP = 10
SSM_TILE_GROUPS = 8
SSM_HALF_GROUPS = 4
N_SEG = 16


def _params(sem=None):
    return pltpu.CompilerParams(dimension_semantics=sem, vmem_limit_bytes=V7X_VMEM_LIMIT_BYTES)


def _tile(n, target, mult):
    best = None
    t = mult
    while t <= min(n, target):
        if n % t == 0:
            best = t
        t += mult
    return n if best is None else best


def _sds(shape, dtype):
    return jax.ShapeDtypeStruct(tuple(shape), dtype)


def _mm(a, b, dims, out_dtype, name, tm=512, tn=1408, tk=2048):
    if dims == "nn":
        (m, k), (k2, n) = a.shape, b.shape
    elif dims == "nt":
        (m, k), (n, k2) = a.shape, b.shape
    else:
        (k, m), (k2, n) = a.shape, b.shape
    assert k == k2, (a.shape, b.shape, dims)
    tm = _tile(m, tm, 16)
    tn = _tile(n, tn, LANE)
    tk = _tile(k, tk, LANE if dims != "tn" else 16)
    nk = k // tk
    dn = {"nn": (((1,), (0,)), ((), ())), "nt": (((1,), (1,)), ((), ())), "tn": (((0,), (0,)), ((), ()))}[dims]

    def product(a_ref, b_ref):
        return lax.dot_general(a_ref[...].astype(MXU_DTYPE), b_ref[...].astype(MXU_DTYPE), dn,
                               preferred_element_type=F32)

    def body_single(a_ref, b_ref, o_ref):
        o_ref[...] = product(a_ref, b_ref).astype(o_ref.dtype)

    def body(a_ref, b_ref, o_ref, acc_ref):
        kk = pl.program_id(2)

        @pl.when(kk == 0)
        def _():
            acc_ref[...] = product(a_ref, b_ref)

        @pl.when((kk > 0) & (kk < nk - 1))
        def _():
            acc_ref[...] += product(a_ref, b_ref)

        @pl.when(kk == nk - 1)
        def _():
            o_ref[...] = (acc_ref[...] + product(a_ref, b_ref)).astype(o_ref.dtype)

    if dims == "nn":
        a_spec = pl.BlockSpec((tm, tk), lambda j, i, kk: (i, kk))
        b_spec = pl.BlockSpec((tk, tn), lambda j, i, kk: (kk, j))
    elif dims == "nt":
        a_spec = pl.BlockSpec((tm, tk), lambda j, i, kk: (i, kk))
        b_spec = pl.BlockSpec((tn, tk), lambda j, i, kk: (j, kk))
    else:
        a_spec = pl.BlockSpec((tk, tm), lambda j, i, kk: (kk, i))
        b_spec = pl.BlockSpec((tk, tn), lambda j, i, kk: (kk, j))
    return pl.pallas_call(
        body_single if nk == 1 else body, name=name, grid=(n // tn, m // tm, nk), in_specs=[a_spec, b_spec],
        out_specs=pl.BlockSpec((tm, tn), lambda j, i, kk: (i, j)), out_shape=_sds((m, n), out_dtype),
        scratch_shapes=[] if nk == 1 else [pltpu.VMEM((tm, tn), F32)],
        compiler_params=_params(("parallel", "parallel", "arbitrary")))(a, b)


def _mm_k_part(a, b, part, n_parts, partial_sum, out_dtype, name, tm=512, tn=1024):
    m, k = a.shape
    kp, n = b.shape
    assert kp * n_parts == k
    tm = _tile(m, tm, 16)
    tn = _tile(n, tn, LANE)

    def body(a_ref, b_ref, *refs):
        acc = jnp.dot(a_ref[...].astype(MXU_DTYPE), b_ref[...].astype(MXU_DTYPE), preferred_element_type=F32)
        if partial_sum is not None:
            acc = acc + refs[0][...]
        refs[-1][...] = acc.astype(refs[-1].dtype)

    tile = pl.BlockSpec((tm, tn), lambda j, i: (i, j))
    in_specs = [pl.BlockSpec((tm, kp), lambda j, i: (i, part)), pl.BlockSpec((kp, tn), lambda j, i: (0, j))]
    args = [a, b]
    if partial_sum is not None:
        in_specs.append(tile)
        args.append(partial_sum)
    return pl.pallas_call(
        body, name=name, grid=(n // tn, m // tm), in_specs=in_specs, out_specs=tile,
        out_shape=_sds((m, n), out_dtype), compiler_params=_params(("parallel", "parallel")))(*args)


def _mm_swiglu(a, b, name, part=0, n_parts=1, partial_sum=None, tm=512, tn=512):
    m, k = a.shape
    kp, f2 = b.shape
    f = f2 // 2
    assert kp * n_parts == k
    tm = _tile(m, tm, 16)
    tn = _tile(f, tn, 2 * LANE)
    nj = f // tn

    def body(a_ref, bg_ref, bu_ref, *refs):
        av = a_ref[...].astype(MXU_DTYPE)
        gate = jnp.dot(av, bg_ref[...].astype(MXU_DTYPE), preferred_element_type=F32)
        up = jnp.dot(av, bu_ref[...].astype(MXU_DTYPE), preferred_element_type=F32)
        if partial_sum is not None:
            gate = gate + refs[0][...]
            up = up + refs[1][...]
        g_ref, u_ref, act_ref = refs[-3:]
        g_ref[...] = gate.astype(g_ref.dtype)
        u_ref[...] = up.astype(u_ref.dtype)
        act_ref[...] = (gate * _sigmoid(gate) * up).astype(act_ref.dtype)

    tile = pl.BlockSpec((tm, tn), lambda j, i: (i, j))
    in_specs = [pl.BlockSpec((tm, kp), lambda j, i: (i, part)), pl.BlockSpec((kp, tn), lambda j, i: (0, j)),
                pl.BlockSpec((kp, tn), lambda j, i: (0, j + nj))]
    args = [a, b, b]
    if partial_sum is not None:
        in_specs += [tile, pl.BlockSpec((tm, tn), lambda j, i: (i, j + nj))]
        args += [partial_sum, partial_sum]
    out = _sds((m, f), BF16)
    return pl.pallas_call(
        body, name=name, grid=(nj, m // tm), in_specs=in_specs, out_specs=[tile, tile, tile],
        out_shape=[out, out, out], compiler_params=_params(("parallel", "parallel")))(*args)


def _rows(name, body, n_tiles, ins, outs):
    in_specs = [pl.BlockSpec(blk, imap) for (_, blk, imap) in ins]
    out_specs = [pl.BlockSpec(blk, imap) for (_, _, blk, imap) in outs]
    out_shape = [_sds(shape, dt) for (shape, dt, _, _) in outs]
    res = pl.pallas_call(body, name=name, grid=(n_tiles,), in_specs=in_specs, out_specs=out_specs,
                         out_shape=out_shape, compiler_params=_params(("arbitrary",)))(*[a for (a, _, _) in ins])
    return res


def _row_in(arr, tile, width=None, col=0, x_only_offset=None):
    width = arr.shape[1] if width is None else width
    if x_only_offset is None:
        return (arr, (tile, width), lambda i: (i, col))
    return (arr, (tile, width), lambda i: (jnp.maximum(i - x_only_offset, 0), col))


def _vec_in(arr, idx_fn):
    return (arr, (1, 1, arr.shape[2]), lambda i: (idx_fn(i), 0, 0))


def _rms(h):
    return lax.rsqrt(jnp.mean(h * h, axis=-1, keepdims=True) + NORM_EPS)


def _sigmoid(z):
    return 1.0 / (1.0 + jnp.exp(-z))


def _ada_pre_fwd(h, g6, mods, gi, mi, nct, tile, name):
    r, d = h.shape
    sel = lambda i: jnp.where(i >= nct, 1, 0)

    def body(h_ref, g_ref, sh_ref, sc_ref, u_ref):
        hh = h_ref[...]
        n = hh * _rms(hh) * g_ref[0]
        u_ref[...] = (n * (1.0 + sc_ref[0]) + sh_ref[0]).astype(u_ref.dtype)

    (u,) = _rows(name, body, r // tile,
                 [_row_in(h, tile), _vec_in(g6, lambda i: gi), _vec_in(mods, lambda i: sel(i) * 9 + 3 * mi),
                  _vec_in(mods, lambda i: sel(i) * 9 + 3 * mi + 1)],
                 [((r, d), BF16, (tile, d), lambda i: (i, 0))])
    return u


def _ada_pre_bwd(h, du, dres, g6, mods, gi, mi, nct, nsel, tile, name, dres_x_only=False):
    r, d = h.shape
    sel = lambda i: jnp.where(i >= nct, 1, 0) if nsel == 2 else 0
    msel = lambda i: jnp.where(i >= nct, 1, 0)
    off = nct if dres_x_only else None

    def body(h_ref, du_ref, dr_ref, g_ref, sc_ref, dh_ref, dg_ref, dsh_ref, dsc_ref):
        i = pl.program_id(0)
        hh = h_ref[...]
        rr = _rms(hh)
        g = g_ref[0]
        hn = hh * rr
        n = hn * g
        du_ = du_ref[...].astype(F32)
        dn = du_ * (1.0 + sc_ref[0])

        @pl.when(i == 0)
        def _():
            dg_ref[...] = jnp.zeros_like(dg_ref)

        @pl.when((i == 0) | (i == nct))
        def _():
            dsh_ref[...] = jnp.zeros_like(dsh_ref)
            dsc_ref[...] = jnp.zeros_like(dsc_ref)

        dg_ref[0] += jnp.sum(dn * hn, axis=0, keepdims=True)
        dsh_ref[0] += jnp.sum(du_, axis=0, keepdims=True)
        dsc_ref[0] += jnp.sum(du_ * n, axis=0, keepdims=True)
        t = dn * g
        dh = rr * t - hn * (rr * jnp.mean(t * hn, axis=-1, keepdims=True))
        if dres_x_only:
            dh_ref[...] = dh + jnp.where(i >= nct, dr_ref[...], 0.0)
        else:
            dh_ref[...] = dh + dr_ref[...]

    dh, dg, dsh, dsc = _rows(
        name, body, r // tile,
        [_row_in(h, tile), _row_in(du, tile), _row_in(dres, tile, x_only_offset=off), _vec_in(g6, lambda i: gi),
         _vec_in(mods, lambda i: msel(i) * 9 + 3 * mi + 1)],
        [((r, d), F32, (tile, d), lambda i: (i, 0)), ((1, 1, d), F32, (1, 1, d), lambda i: (0, 0, 0)),
         ((nsel, 1, d), F32, (1, 1, d), lambda i: (sel(i), 0, 0)),
         ((nsel, 1, d), F32, (1, 1, d), lambda i: (sel(i), 0, 0))])
    return dh, dg, dsh, dsc


def _ada_post_fwd(h, o, g6, mods, gi, mi, res_w, nct, tile, name, h_tile_offset=0):
    r, d = o.shape
    sel = lambda i: jnp.where(i >= nct, 1, 0)

    def body(h_ref, o_ref, g_ref, gt_ref, y_ref):
        oo = o_ref[...]
        n = oo * _rms(oo) * g_ref[0]
        y_ref[...] = h_ref[...] + res_w * gt_ref[0] * n

    (y,) = _rows(name, body, r // tile,
                 [(h, (tile, d), lambda i: (i + h_tile_offset, 0)), _row_in(o, tile), _vec_in(g6, lambda i: gi),
                  _vec_in(mods, lambda i: sel(i) * 9 + 3 * mi + 2)],
                 [((r, d), F32, (tile, d), lambda i: (i, 0))])
    return y


def _ada_post_bwd(dy, o, g6, mods, gi, mi, res_w, nct, nsel, tile, name):
    r, d = o.shape
    sel = lambda i: jnp.where(i >= nct, 1, 0) if nsel == 2 else 0
    msel = lambda i: jnp.where(i >= nct, 1, 0)

    def body(dy_ref, o_ref, g_ref, gt_ref, do_ref, dg_ref, dgt_ref):
        i = pl.program_id(0)
        oo = o_ref[...]
        rr = _rms(oo)
        g = g_ref[0]
        on = oo * rr
        dy_ = dy_ref[...] * res_w

        @pl.when(i == 0)
        def _():
            dg_ref[...] = jnp.zeros_like(dg_ref)

        @pl.when((i == 0) | (i == nct))
        def _():
            dgt_ref[...] = jnp.zeros_like(dgt_ref)

        dgt_ref[0] += jnp.sum(dy_ * (on * g), axis=0, keepdims=True)
        dn = dy_ * gt_ref[0]
        dg_ref[0] += jnp.sum(dn * on, axis=0, keepdims=True)
        t = dn * g
        do_ref[...] = (rr * t - on * (rr * jnp.mean(t * on, axis=-1, keepdims=True))).astype(do_ref.dtype)

    do, dg, dgt = _rows(
        name, body, r // tile,
        [_row_in(dy, tile), _row_in(o, tile), _vec_in(g6, lambda i: gi),
         _vec_in(mods, lambda i: msel(i) * 9 + 3 * mi + 2)],
        [((r, d), BF16, (tile, d), lambda i: (i, 0)), ((1, 1, d), F32, (1, 1, d), lambda i: (0, 0, 0)),
         ((nsel, 1, d), F32, (1, 1, d), lambda i: (sel(i), 0, 0))])
    return do, dg, dgt


def _swiglu_bwd(gate, up, da, tile, name):
    r, f = gate.shape

    def body(g_ref, u_ref, da_ref, dh_ref):
        gt = g_ref[...].astype(F32)
        d = da_ref[...].astype(F32)
        sg = _sigmoid(gt)
        dh_ref[:, :f] = (d * u_ref[...].astype(F32) * (sg * (1.0 + gt * (1.0 - sg)))).astype(dh_ref.dtype)
        dh_ref[:, f:] = (d * gt * sg).astype(dh_ref.dtype)

    (dh,) = _rows(name, body, r // tile, [_row_in(gate, tile), _row_in(up, tile), _row_in(da, tile)],
                  [((r, 2 * f), BF16, (tile, 2 * f), lambda i: (i, 0))])
    return dh


def _gelu_parts(y):
    c0 = math.sqrt(2.0 / math.pi)
    inner = c0 * (y + 0.044715 * y * y * y)
    th = jnp.tanh(inner)
    return th, c0 * (1.0 + 3 * 0.044715 * y * y)


def _ssm_out_fwd(y0, y1, hm, dskip, nct, tile, name):
    t_rows, s = y0.shape

    def body(y0_ref, y1_ref, u_ref, d_ref, a_ref):
        y = y0_ref[...] + y1_ref[...] + d_ref[0] * u_ref[...]
        th, _ = _gelu_parts(y)
        a_ref[...] = (0.5 * y * (1.0 + th)).astype(a_ref.dtype)

    (a,) = _rows(name, body, t_rows // tile,
                 [_row_in(y0, tile), _row_in(y1, tile), (hm, (tile, s), lambda i: (i + nct, 0)),
                  _vec_in(dskip, lambda i: 0)],
                 [((t_rows, s), BF16, (tile, s), lambda i: (i, 0))])
    return a


def _ssm_out_bwd(y0, y1, hm, dskip, da, nct, tile, name):
    t_rows, s = y0.shape

    def body(y0_ref, y1_ref, u_ref, d_ref, da_ref, dy_ref, du_ref, dd_ref):
        i = pl.program_id(0)
        u = u_ref[...]
        y = y0_ref[...] + y1_ref[...] + d_ref[0] * u
        th, dinner = _gelu_parts(y)
        dy = da_ref[...] * (0.5 * (1.0 + th) + 0.5 * y * (1.0 - th * th) * dinner)
        dy_ref[...] = dy
        du_ref[...] = dy * d_ref[0]

        @pl.when(i == 0)
        def _():
            dd_ref[...] = jnp.zeros_like(dd_ref)

        dd_ref[0] += jnp.sum(dy * u, axis=0, keepdims=True)

    dy, du, dd = _rows(name, body, t_rows // tile,
                       [_row_in(y0, tile), _row_in(y1, tile), (hm, (tile, s), lambda i: (i + nct, 0)),
                        _vec_in(dskip, lambda i: 0), _row_in(da, tile)],
                       [((t_rows, s), F32, (tile, s), lambda i: (i, 0)), ((t_rows, s), F32, (tile, s), lambda i: (i, 0)),
                        ((1, 1, s), F32, (1, 1, s), lambda i: (0, 0, 0))])
    return dy, du, dd


def _col_pieces(arr, off, width, tile, nct, unit=None):
    pw = math.gcd(off, width if unit is None else unit)
    specs = [(arr, (tile, pw), functools.partial(lambda i, cb: (i + nct, cb), cb=off // pw + p))
             for p in range(width // pw)]
    return specs, pw


def _ret_gate_fwd(o0, o1, hm, g_off, heads, dv, nct, tile, name):
    t_rows, w = o0.shape
    g_specs, pw = _col_pieces(hm, g_off, w, tile, nct)
    ng = len(g_specs)

    def body(o0_ref, o1_ref, *refs):
        g_refs, r_ref = refs[:ng], refs[ng]
        for hd in range(heads):
            cs = slice(hd * dv, (hd + 1) * dv)
            o = o0_ref[:, cs] + o1_ref[:, cs]
            lo = (hd * dv) % pw
            g = g_refs[(hd * dv) // pw][:, lo:lo + dv]
            r_ref[:, cs] = (g * _sigmoid(g) * (o * _rms(o))).astype(r_ref.dtype)

    (ri,) = _rows(name, body, t_rows // tile, [_row_in(o0, tile), _row_in(o1, tile)] + g_specs,
                  [((t_rows, w), BF16, (tile, w), lambda i: (i, 0))])
    return ri


def _ret_gate_bwd(o0, o1, hm, g_off, dri, heads, dv, nct, tile, name):
    t_rows, w = o0.shape
    g_specs, pw = _col_pieces(hm, g_off, w, tile, nct)
    ng = len(g_specs)

    def body(o0_ref, o1_ref, d_ref, *refs):
        g_refs, do_ref, dg_ref = refs[:ng], refs[ng], refs[ng + 1]
        for hd in range(heads):
            cs = slice(hd * dv, (hd + 1) * dv)
            o = o0_ref[:, cs] + o1_ref[:, cs]
            lo = (hd * dv) % pw
            g = g_refs[(hd * dv) // pw][:, lo:lo + dv]
            d = d_ref[:, cs]
            rr = _rms(o)
            on = o * rr
            sg = _sigmoid(g)
            dg_ref[:, cs] = (d * on * (sg * (1.0 + g * (1.0 - sg)))).astype(dg_ref.dtype)
            t = d * (g * sg)
            do_ref[:, cs] = rr * t - on * (rr * jnp.mean(t * on, axis=-1, keepdims=True))

    do, dg = _rows(name, body, t_rows // tile, [_row_in(o0, tile), _row_in(o1, tile), _row_in(dri, tile)] + g_specs,
                   [((t_rows, w), F32, (tile, w), lambda i: (i, 0)), ((t_rows, w), BF16, (tile, w), lambda i: (i, 0))])
    return do, dg


def _merge_fwd(gab, rb, hm, gs_off, nct, tile, name):
    t_rows, d = rb.shape
    specs, pw = _col_pieces(hm, gs_off, 2 * d, tile, nct, unit=d)
    npc = d // pw

    def body(gab_ref, rb_ref, *refs):
        gs_refs, gr_refs, m_ref = refs[:npc], refs[npc:2 * npc], refs[2 * npc]
        for p in range(npc):
            cs = slice(p * pw, (p + 1) * pw)
            ga = gab_ref[:, cs]
            gb = gab_ref[:, d + p * pw:d + (p + 1) * pw]
            m_ref[:, cs] = (_sigmoid(gs_refs[p][...]) * (ga * _sigmoid(gb))
                            + _sigmoid(gr_refs[p][...]) * rb_ref[:, cs]).astype(m_ref.dtype)

    (mg,) = _rows(name, body, t_rows // tile, [_row_in(gab, tile), _row_in(rb, tile)] + specs,
                  [((t_rows, d), BF16, (tile, d), lambda i: (i, 0))])
    return mg


def _merge_bwd(gab, rb, hm, gs_off, dm, nct, tile, name):
    t_rows, d = rb.shape
    specs, pw = _col_pieces(hm, gs_off, 2 * d, tile, nct, unit=d)
    npc = d // pw

    def body(gab_ref, rb_ref, dm_ref, *refs):
        gs_refs, gr_refs = refs[:npc], refs[npc:2 * npc]
        dgab_ref, drb_ref, dgs_ref, dgr_ref = refs[2 * npc:]
        for p in range(npc):
            cs = slice(p * pw, (p + 1) * pw)
            cs2 = slice(d + p * pw, d + (p + 1) * pw)
            ga = gab_ref[:, cs]
            gb = gab_ref[:, cs2]
            dmm = dm_ref[:, cs]
            ss = _sigmoid(gs_refs[p][...])
            sr = _sigmoid(gr_refs[p][...])
            sb = _sigmoid(gb)
            dbr = dmm * ss
            dgab_ref[:, cs] = (dbr * sb).astype(dgab_ref.dtype)
            dgab_ref[:, cs2] = (dbr * ga * sb * (1.0 - sb)).astype(dgab_ref.dtype)
            drb_ref[:, cs] = (dmm * sr).astype(drb_ref.dtype)
            dgs_ref[:, cs] = (dmm * (ga * sb) * ss * (1.0 - ss)).astype(dgs_ref.dtype)
            dgr_ref[:, cs] = (dmm * rb_ref[:, cs] * sr * (1.0 - sr)).astype(dgr_ref.dtype)

    return _rows(name, body, t_rows // tile, [_row_in(gab, tile), _row_in(rb, tile), _row_in(dm, tile)] + specs,
                 [((t_rows, 2 * d), BF16, (tile, 2 * d), lambda i: (i, 0)), ((t_rows, d), BF16, (tile, d), lambda i: (i, 0)),
                  ((t_rows, d), BF16, (tile, d), lambda i: (i, 0)), ((t_rows, d), BF16, (tile, d), lambda i: (i, 0))])


def _assemble_dhm(dus, dq0, dq1, dk0, dk1, dv0, dv1, dg, dgs, dgr, nct, tile, name):
    r, s = dus.shape
    qk = dq0.shape[1]
    vw = dv0.shape[1]
    d = dgs.shape[1]
    mi = s + 2 * qk + 2 * vw + 2 * d
    c_q, c_k, c_v, c_g, c_gs, c_gr = s, s + qk, s + 2 * qk, s + 2 * qk + vw, s + 2 * qk + 2 * vw, s + 2 * qk + 2 * vw + d

    def body(dus_ref, dq0_ref, dq1_ref, dk0_ref, dk1_ref, dv0_ref, dv1_ref, dg_ref, dgs_ref, dgr_ref, o_ref):
        i = pl.program_id(0)
        lat = i >= nct
        o_ref[:, :s] = dus_ref[...].astype(o_ref.dtype)
        o_ref[:, c_q:c_k] = (dq0_ref[...] + dq1_ref[...]).astype(o_ref.dtype)
        o_ref[:, c_k:c_v] = (dk0_ref[...] + dk1_ref[...]).astype(o_ref.dtype)
        o_ref[:, c_v:c_g] = (dv0_ref[...] + dv1_ref[...]).astype(o_ref.dtype)
        o_ref[:, c_g:c_gs] = jnp.where(lat, dg_ref[...], 0.0).astype(o_ref.dtype)
        o_ref[:, c_gs:c_gr] = jnp.where(lat, dgs_ref[...], 0.0).astype(o_ref.dtype)
        o_ref[:, c_gr:] = jnp.where(lat, dgr_ref[...], 0.0).astype(o_ref.dtype)

    (out,) = _rows(name, body, r // tile,
                   [_row_in(dus, tile), _row_in(dq0, tile), _row_in(dq1, tile), _row_in(dk0, tile), _row_in(dk1, tile),
                    _row_in(dv0, tile), _row_in(dv1, tile), _row_in(dg, tile, x_only_offset=nct),
                    _row_in(dgs, tile, x_only_offset=nct), _row_in(dgr, tile, x_only_offset=nct)],
                   [((r, mi), BF16, (tile, mi), lambda i: (i, 0))])
    return out


def _loss_grad(y, target, tile, name):
    t_rows, d = y.shape

    def body(y_ref, t_ref, dy_ref, l_ref):
        i = pl.program_id(0)
        e = y_ref[...] - t_ref[...]
        dy_ref[...] = e * (1.0 / d)

        @pl.when(i == 0)
        def _():
            l_ref[...] = jnp.zeros_like(l_ref)

        l_ref[0] += jnp.sum(e * e, axis=0, keepdims=True)

    return _rows(name, body, t_rows // tile, [_row_in(y, tile), _row_in(target, tile)],
                 [((t_rows, d), F32, (tile, d), lambda i: (i, 0)), ((1, 1, d), F32, (1, 1, d), lambda i: (0, 0, 0))])


def _silu_rows(v, name):
    def body(v_ref, o_ref):
        z = v_ref[...]
        o_ref[...] = z * _sigmoid(z)

    (o,) = _rows(name, body, 1, [_row_in(v, v.shape[0])], [(v.shape, F32, v.shape, lambda i: (0, 0))])
    return o


def _silu_grad_rows(v, dv, name):
    def body(v_ref, d_ref, o_ref):
        z = v_ref[...]
        sg = _sigmoid(z)
        o_ref[...] = d_ref[...] * (sg * (1.0 + z * (1.0 - sg)))

    (o,) = _rows(name, body, 1, [_row_in(v, v.shape[0]), _row_in(dv, v.shape[0])],
                 [(v.shape, F32, v.shape, lambda i: (0, 0))])
    return o


def _sum_leading(g8, name):
    n, r, c = g8.shape
    tile = _tile(r, 256, SUBLANE)

    def body(g_ref, o_ref):
        acc = g_ref[0]
        for j in range(1, n):
            acc = acc + g_ref[j]
        o_ref[...] = acc

    (o,) = _rows(name, body, r // tile, [(g8, (n, tile, c), lambda i: (0, i, 0))],
                 [((r, c), F32, (tile, c), lambda i: (i, 0))])
    return o


def _pair_sum(g, recv, axis, name):
    n, br, bc = recv.shape
    tile = _tile(br, 256, 16)
    nrt = br // tile
    core = lax.axis_index("c").astype(jnp.int32).reshape(1)

    def body(c_ref, g_ref, r_ref, o_ref):
        o_ref[0] = (g_ref[...].astype(F32) + r_ref[0].astype(F32)).astype(o_ref.dtype)

    if axis == 1:
        g_spec = pl.BlockSpec((tile, bc), lambda q, i, c_ref: (i, 2 * q + c_ref[0]))
    else:
        g_spec = pl.BlockSpec((tile, bc), lambda q, i, c_ref: ((2 * q + c_ref[0]) * nrt + i, 0))
    slot = pl.BlockSpec((1, tile, bc), lambda q, i, c_ref: (q, i, 0))
    return pl.pallas_call(
        body, name=name, out_shape=_sds((n, br, bc), recv.dtype),
        grid_spec=pltpu.PrefetchScalarGridSpec(num_scalar_prefetch=1, grid=(n, nrt), in_specs=[g_spec, slot],
                                               out_specs=slot),
        compiler_params=_params(("arbitrary", "arbitrary")))(core, g, recv)


def _adam_math(w, m, v, g):
    c1 = 1.0 / (1.0 - ADAM_B1 ** ADAM_STEP)
    c2 = 1.0 / (1.0 - ADAM_B2 ** ADAM_STEP)
    mm = ADAM_B1 * m + (1.0 - ADAM_B1) * g
    vv = ADAM_B2 * v + (1.0 - ADAM_B2) * (g * g)
    return -ADAM_LR * ((mm * c1) / (jnp.sqrt(vv * c2) + ADAM_EPS) + ADAM_WD * w), mm, vv


def _adamw(w, m, v, gparts, name):
    r, c = w.shape
    n = gparts.shape[0]
    tile = _tile(r, 256, 16)

    def body(w_ref, m_ref, v_ref, g_ref, go_ref, d_ref, mo_ref, vo_ref):
        g = g_ref[0].astype(F32)
        for j in range(1, n):
            g = g + g_ref[j].astype(F32)
        go_ref[...] = g
        d_ref[...], mo_ref[...], vo_ref[...] = _adam_math(w_ref[...], m_ref[...], v_ref[...], g)

    rs = lambda arr: _row_in(arr, tile)
    out = ((r, c), F32, (tile, c), lambda i: (i, 0))
    return _rows(name, body, r // tile, [rs(w), rs(m), rs(v), (gparts, (n, tile, c), lambda i: (0, i, 0))],
                 [out, out, out, out])


def _adamw_scattered(w, m, v, layer, p, recv, name, filled=None):
    nl, r, c = w.shape
    n = recv.shape[0]
    tile = _tile(r, 256, 16)
    chip = (2 * lax.axis_index("x") + lax.axis_index("y")).astype(jnp.int32).reshape(1)
    n_prev = 0 if filled is None else len(filled)

    def body(q_ref, w_ref, m_ref, v_ref, p_ref, g_ref, *rest):
        go_ref, d_ref, mo_ref, vo_ref = rest[n_prev:]
        g = p_ref[0].astype(F32)
        for j in range(n):
            g = g + g_ref[j].astype(F32)
        go_ref[0] = g
        d_ref[0], mo_ref[0], vo_ref[0] = _adam_math(w_ref[0], m_ref[0], v_ref[0], g)

    slab = pl.BlockSpec((1, tile, c), lambda i, q_ref: (layer, i, 0))
    anywhere = pl.BlockSpec(memory_space=pl.ANY)
    out = _sds((nl, r, c), F32)
    prev = [] if filled is None else list(filled)
    return pl.pallas_call(
        body, name=name, out_shape=[out, out, out, out],
        grid_spec=pltpu.PrefetchScalarGridSpec(
            num_scalar_prefetch=1, grid=(r // tile,),
            in_specs=[slab, slab, slab, pl.BlockSpec((1, tile, c), lambda i, q_ref: (q_ref[0], i, 0)),
                      pl.BlockSpec((n, tile, c), lambda i, q_ref: (0, i, 0))] + [anywhere] * n_prev,
            out_specs=[slab, slab, slab, slab]),
        input_output_aliases={6 + j: j for j in range(n_prev)},
        compiler_params=_params(("arbitrary",)))(chip, w, m, v, p, recv, *prev)


def _cmul(ar, ai, br, bi):
    return ar * br - ai * bi, ar * bi + ai * br


def _cpow(ar, ai, n):
    pr, pi = jnp.ones_like(ar), jnp.zeros_like(ar)
    br, bi = ar, ai
    while n:
        if n & 1:
            pr, pi = _cmul(pr, pi, br, bi)
        n >>= 1
        if n:
            br, bi = _cmul(br, bi, br, bi)
    return pr, pi


def _s5_scan_into(x_ref, ar1, ai1, ns, fin_ref, hin_ref, reverse, paired=None):
    st = ar1.shape[1]
    ar = jnp.broadcast_to(ar1, (N_SEG, st))
    ai = jnp.broadcast_to(ai1, (N_SEG, st))
    zero = jnp.zeros((N_SEG, st), F32)

    def slab(k):
        if isinstance(k, int):
            return pl.ds(k * N_SEG, N_SEG)
        return pl.ds(pl.multiple_of(k * N_SEG, N_SEG), N_SEG)

    def pass1(j, carry):
        hr, hi = carry
        k = ns - 1 - j if reverse else j
        nr, ni = _cmul(ar, ai, hr, hi)
        return nr + x_ref[slab(k), :st], ni + x_ref[slab(k), st:]

    fr, fi = lax.fori_loop(0, ns, pass1, (zero, zero))
    fin_ref[:, :st] = fr
    fin_ref[:, st:] = fi
    pr, pi = _cpow(ar1, ai1, ns)
    order = list(range(N_SEG - 1, -1, -1)) if reverse else list(range(N_SEG))
    hin_ref[order[0]:order[0] + 1, :] = jnp.zeros((1, 2 * st), F32)
    for a_, b_ in zip(order[:-1], order[1:]):
        cr, ci = _cmul(pr, pi, hin_ref[a_:a_ + 1, :st], hin_ref[a_:a_ + 1, st:])
        hin_ref[b_:b_ + 1, :st] = cr + fin_ref[a_:a_ + 1, :st]
        hin_ref[b_:b_ + 1, st:] = ci + fin_ref[a_:a_ + 1, st:]

    def step2(k, hr, hi):
        nr, ni = _cmul(ar, ai, hr, hi)
        nr = nr + x_ref[slab(k), :st]
        ni = ni + x_ref[slab(k), st:]
        x_ref[slab(k), :st] = nr
        x_ref[slab(k), st:] = ni
        return nr, ni

    if paired is None:
        def pass2(j, carry):
            return step2(ns - 1 - j if reverse else j, *carry)

        lax.fori_loop(0, ns, pass2, (hin_ref[:, :st], hin_ref[:, st:]))
        return None
    p_ref, p_edge_ref, shift = paired

    def pass2_paired(j, carry):
        hr, hi, acr, aci = carry
        k = ns - 1 - j if reverse else j
        nr, ni = step2(k, hr, hi)
        p_r, p_i = p_ref[slab(k + shift), :st], p_ref[slab(k + shift), st:]
        return nr, ni, acr + nr * p_r + ni * p_i, aci + ni * p_r - nr * p_i

    hr, hi, acr, aci = lax.fori_loop(0, ns - 1, pass2_paired, (hin_ref[:, :st], hin_ref[:, st:], zero, zero))
    nr, ni = step2(0 if reverse else ns - 1, hr, hi)
    p_r, p_i = p_edge_ref[:, :st], p_edge_ref[:, st:]
    return acr + nr * p_r + ni * p_i, aci + ni * p_r - nr * p_i


def _s5_specs(r, ch, st):
    u_spec = pl.BlockSpec((r, ch), lambda j: (0, j // 2))
    w_spec = pl.BlockSpec((1, ch, 2 * st), lambda j: (j, 0, 0))
    c_spec = pl.BlockSpec((1, 2 * st, ch), lambda j: (j, 0, 0))
    a_spec = pl.BlockSpec((1, 2, st), lambda j: (j, 0, 0))
    return u_spec, w_spec, c_spec, a_spec


def _s5_fwd(up, w, c, a, rev, name):
    r, s = up.shape
    nh, ch, st2 = w.shape
    st = st2 // 2
    ns = r // N_SEG
    nb = r // N_DEV
    u_spec, w_spec, c_spec, a_spec = _s5_specs(r, ch, st)

    def body(u_ref, w_ref, c_ref, a_ref, y_ref, x, fin, hin):
        j = pl.program_id(0)
        w_b = w_ref[0].astype(MXU_DTYPE)
        c_b = c_ref[0].astype(MXU_DTYPE)
        for rb in range(N_DEV):
            rows = slice(rb * nb, (rb + 1) * nb)
            x[rows, :] = jnp.dot(u_ref[rows, :].astype(MXU_DTYPE), w_b, preferred_element_type=F32)
        _s5_scan_into(x, a_ref[0, 0:1, :], a_ref[0, 1:2, :], ns, fin, hin, rev)
        for rb in range(N_DEV):
            rows = slice(rb * nb, (rb + 1) * nb)
            yb = jnp.dot(x[rows, :].astype(MXU_DTYPE), c_b, preferred_element_type=F32)

            @pl.when(j % 2 == 0)
            def _():
                y_ref[rows, :] = yb

            @pl.when(j % 2 == 1)
            def _():
                y_ref[rows, :] += yb

    small = pltpu.VMEM((N_SEG, st2), F32)
    return pl.pallas_call(
        body, name=name, grid=(nh,), in_specs=[u_spec, w_spec, c_spec, a_spec],
        out_specs=pl.BlockSpec((r, ch), lambda j: (0, j // 2)), out_shape=_sds((r, s), F32),
        scratch_shapes=[pltpu.VMEM((r, st2), F32), small, small],
        compiler_params=_params(("arbitrary",)))(up, w, c, a)


def _s5_bwd(up, dyp, w, c, a, rev, name):
    r, s = up.shape
    nh, ch, st2 = w.shape
    st = st2 // 2
    ns = r // N_SEG
    nb = r // N_DEV
    u_spec, w_spec, c_spec, a_spec = _s5_specs(r, ch, st)
    nt = (((1,), (1,)), ((), ()))
    tn = (((0,), (0,)), ((), ()))

    def body(u_ref, dy_ref, w_ref, c_ref, a_ref, du_ref, dw_ref, dc_ref, da_ref, h, g, fin, sin_, ein):
        j = pl.program_id(0)
        w_b = w_ref[0].astype(MXU_DTYPE)
        c_b = c_ref[0].astype(MXU_DTYPE)
        for rb in range(N_DEV):
            rows = slice(rb * nb, (rb + 1) * nb)
            h[rows, :] = jnp.dot(u_ref[rows, :].astype(MXU_DTYPE), w_b, preferred_element_type=F32)
        ar1, ai1 = a_ref[0, 0:1, :], a_ref[0, 1:2, :]
        _s5_scan_into(h, ar1, ai1, ns, fin, sin_, rev)
        dc = jnp.zeros((st2, ch), F32)
        for rb in range(N_DEV):
            rows = slice(rb * nb, (rb + 1) * nb)
            dyb = dy_ref[rows, :].astype(MXU_DTYPE)
            g[rows, :] = lax.dot_general(dyb, c_b, nt, preferred_element_type=F32)
            dc += lax.dot_general(h[rows, :].astype(MXU_DTYPE), dyb, tn, preferred_element_type=F32)
        dc_ref[0] = dc
        acr, aci = _s5_scan_into(g, ar1, -ai1, ns, fin, ein, not rev, paired=(h, sin_, 1 if rev else -1))
        da_ref[0, 0:1, :] = jnp.sum(acr, axis=0, keepdims=True)
        da_ref[0, 1:2, :] = jnp.sum(aci, axis=0, keepdims=True)
        dw = jnp.zeros((ch, st2), F32)
        for rb in range(N_DEV):
            rows = slice(rb * nb, (rb + 1) * nb)
            gb = g[rows, :].astype(MXU_DTYPE)
            dub = lax.dot_general(gb, w_b, nt, preferred_element_type=F32)
            dw += lax.dot_general(u_ref[rows, :].astype(MXU_DTYPE), gb, tn, preferred_element_type=F32)

            @pl.when(j % 2 == 0)
            def _():
                du_ref[rows, :] = dub

            @pl.when(j % 2 == 1)
            def _():
                du_ref[rows, :] += dub

        dw_ref[0] = dw

    small = pltpu.VMEM((N_SEG, st2), F32)
    big = pltpu.VMEM((r, st2), F32)
    return pl.pallas_call(
        body, name=name, grid=(nh,), in_specs=[u_spec, u_spec, w_spec, c_spec, a_spec],
        out_specs=[pl.BlockSpec((r, ch), lambda j: (0, j // 2)), w_spec, c_spec, a_spec],
        out_shape=[_sds((r, s), F32), _sds(w.shape, F32), _sds(c.shape, F32), _sds(a.shape, F32)],
        scratch_shapes=[big, big, small, small, small],
        compiler_params=_params(("arbitrary",)))(up, dyp, w, c, a)


def _rope(t, cos, sin):
    quarter = t.shape[1] // 4
    lane = lax.broadcasted_iota(jnp.int32, t.shape, 1)
    first = (lane // quarter) % 2 == 0
    partner = jnp.where(first, pltpu.roll(t, t.shape[1] - quarter, 1), pltpu.roll(t, quarter, 1))
    return t * cos + partner * sin


def _rope_t(d, cos, sin):
    quarter = d.shape[1] // 4
    ds_ = d * sin
    lane = lax.broadcasted_iota(jnp.int32, d.shape, 1)
    first = (lane // quarter) % 2 == 0
    partner = jnp.where(first, pltpu.roll(ds_, d.shape[1] - quarter, 1), pltpu.roll(ds_, quarter, 1))
    return d * cos + partner


def _chunk_of_step(s, nch, ncc, rev):
    if not rev:
        return s
    return jnp.where(s < ncc, ncc - 1 - s, nch + ncc - 1 - s)


def _heads_per_step(heads, dk, dv, q_off):
    v_off = q_off + 2 * heads * dk
    for hpg in range(heads, 0, -1):
        if heads % hpg == 0 and q_off % (hpg * dk) == 0:
            piece = math.gcd(v_off, hpg * dv)
            if piece % dv == 0:
                return hpg, piece
    return 1, dv


def _v_specs(hpg, dv, piece, v_off, ch, chunk_of):
    n_pieces = hpg * dv // piece
    return [pl.BlockSpec((ch, piece), functools.partial(
        lambda h, s, p: (chunk_of(s), v_off // piece + h * n_pieces + p), p=p)) for p in range(n_pieces)]


def _v_of_head(v_refs, hl, dv, piece):
    lo = (hl * dv) % piece
    return v_refs[(hl * dv) // piece][:, lo:lo + dv]


def _ret_fwd(hm, cos, sin, decay, wend, win, gch, heads, dk, dv, q_off, ncc, rev, name):
    r = hm.shape[0]
    ch = RET_CHUNK
    nch = r // ch
    t_rows = r - ncc * ch
    hpg, piece = _heads_per_step(heads, dk, dv, q_off)
    qb, kb = q_off // (hpg * dk), (q_off + heads * dk) // (hpg * dk)
    q_scale = dk ** -0.5
    nt = (((1,), (1,)), ((), ()))
    tn = (((0,), (0,)), ((), ()))
    cof = lambda s: _chunk_of_step(s, nch, ncc, rev)
    v_specs = _v_specs(hpg, dv, piece, q_off + 2 * heads * dk, ch, cof)
    nv = len(v_specs)

    def body(q_ref, k_ref, *refs):
        v_refs = refs[:nv]
        cos_ref, sin_ref, dec_ref, we_ref, wi_ref, g_ref, o_ref, sin_out, st = refs[nv:]
        s = pl.program_id(1)

        @pl.when(s == 0)
        def _():
            st[...] = jnp.zeros_like(st)

        cos_, sin_ = cos_ref[...], sin_ref[...]
        for hl in range(hpg):
            ks, vs = slice(hl * dk, (hl + 1) * dk), slice(hl * dv, (hl + 1) * dv)
            q = _rope(q_ref[:, ks], cos_, sin_) * q_scale
            k = _rope(k_ref[:, ks], cos_, sin_)
            v = _v_of_head(v_refs, hl, dv, piece).astype(MXU_DTYPE)
            s_cur = st[hl]
            sin_out[hl, 0] = s_cur
            kw = (k * we_ref[hl]).astype(MXU_DTYPE)
            qw = (q * wi_ref[hl]).astype(MXU_DTYPE)
            scores = lax.dot_general(q.astype(MXU_DTYPE), k.astype(MXU_DTYPE), nt,
                                     preferred_element_type=F32) * dec_ref[hl]
            o_ref[:, vs] = (jnp.dot(scores.astype(MXU_DTYPE), v, preferred_element_type=F32)
                            + jnp.dot(qw, s_cur.astype(MXU_DTYPE), preferred_element_type=F32))
            st[hl] = g_ref[hl] * s_cur + lax.dot_general(kw, v, tn, preferred_element_type=F32)

    tab = lambda w: pl.BlockSpec((hpg, ch, w), lambda h, s: (h, 0, 0))
    return pl.pallas_call(
        body, name=name, grid=(heads // hpg, nch),
        in_specs=[pl.BlockSpec((ch, hpg * dk), lambda h, s: (cof(s), qb + h)),
                  pl.BlockSpec((ch, hpg * dk), lambda h, s: (cof(s), kb + h))] + v_specs +
                 [pl.BlockSpec((ch, dk), lambda h, s: (cof(s), 0)),
                  pl.BlockSpec((ch, dk), lambda h, s: (cof(s), 0)),
                  tab(ch), tab(dk), tab(dk), tab(dv)],
        out_specs=[pl.BlockSpec((ch, hpg * dv), lambda h, s: (jnp.maximum(cof(s) - ncc, 0) if not rev
                                                               else jnp.where(s < ncc, nch - ncc - 1, cof(s) - ncc), h)),
                   pl.BlockSpec((hpg, 1, dk, dv), lambda h, s: (h, s, 0, 0))],
        out_shape=[_sds((t_rows, heads * dv), F32), _sds((heads, nch, dk, dv), F32)],
        scratch_shapes=[pltpu.VMEM((hpg, dk, dv), F32)],
        compiler_params=_params(("parallel", "arbitrary")))(hm, hm, *([hm] * nv), cos, sin, decay, wend, win, gch)


def _ret_bwd(hm, cos, sin, decay, wend, win, gch, s_in, do, heads, dk, dv, q_off, ncc, rev, name):
    r = hm.shape[0]
    ch = RET_CHUNK
    nch = r // ch
    hpg, piece = _heads_per_step(heads, dk, dv, q_off)
    qb, kb = q_off // (hpg * dk), (q_off + heads * dk) // (hpg * dk)
    q_scale = dk ** -0.5
    nt = (((1,), (1,)), ((), ()))
    tn = (((0,), (0,)), ((), ()))
    cof = lambda rr: _chunk_of_step(nch - 1 - rr, nch, ncc, rev)
    v_specs = _v_specs(hpg, dv, piece, q_off + 2 * heads * dk, ch, cof)
    nv = len(v_specs)

    def body(q_ref, k_ref, *refs):
        v_refs = refs[:nv]
        (cos_ref, sin_ref, dec_ref, we_ref, wi_ref, g_ref, sin_ref2, do_ref,
         dq_ref, dk_ref, dv_ref, ddec_ref, dwe_ref, dwi_ref, dg_ref, dst) = refs[nv:]
        rr = pl.program_id(1)
        n = cof(rr)

        @pl.when(rr == 0)
        def _():
            dst[...] = jnp.zeros_like(dst)
            ddec_ref[...] = jnp.zeros_like(ddec_ref)
            dwe_ref[...] = jnp.zeros_like(dwe_ref)
            dwi_ref[...] = jnp.zeros_like(dwi_ref)
            dg_ref[...] = jnp.zeros_like(dg_ref)

        cos_, sin_ = cos_ref[...], sin_ref[...]
        for hl in range(hpg):
            ks, vs = slice(hl * dk, (hl + 1) * dk), slice(hl * dv, (hl + 1) * dv)
            q = _rope(q_ref[:, ks], cos_, sin_) * q_scale
            k = _rope(k_ref[:, ks], cos_, sin_)
            v = _v_of_head(v_refs, hl, dv, piece).astype(MXU_DTYPE)
            qb_, kb_ = q.astype(MXU_DTYPE), k.astype(MXU_DTYPE)
            kw = (k * we_ref[hl]).astype(MXU_DTYPE)
            qw = (q * wi_ref[hl]).astype(MXU_DTYPE)
            sraw = lax.dot_general(qb_, kb_, nt, preferred_element_type=F32)
            scores = (sraw * dec_ref[hl]).astype(MXU_DTYPE)
            d_o = jnp.where(n >= ncc, do_ref[:, vs], 0.0).astype(MXU_DTYPE)
            s_n = sin_ref2[hl, 0]
            s_nb = s_n.astype(MXU_DTYPE)
            ds1 = dst[hl]
            ds1b = ds1.astype(MXU_DTYPE)
            dsc = lax.dot_general(d_o, v, nt, preferred_element_type=F32)
            dsr = (dsc * dec_ref[hl]).astype(MXU_DTYPE)
            ddec_ref[hl] += dsc * sraw
            t1 = lax.dot_general(d_o, s_nb, nt, preferred_element_type=F32)
            dq_r = jnp.dot(dsr, kb_, preferred_element_type=F32) + t1 * wi_ref[hl]
            dwi_ref[hl] += t1 * q
            t2 = lax.dot_general(v, ds1b, nt, preferred_element_type=F32)
            dk_r = lax.dot_general(dsr, qb_, tn, preferred_element_type=F32) + t2 * we_ref[hl]
            dwe_ref[hl] += t2 * k
            dv_ref[:, vs] = (lax.dot_general(scores, d_o, tn, preferred_element_type=F32)
                             + jnp.dot(kw, ds1b, preferred_element_type=F32))
            dg_ref[hl] += ds1 * s_n
            dst[hl] = g_ref[hl] * ds1 + lax.dot_general(qw, d_o, tn, preferred_element_type=F32)
            dq_ref[:, ks] = _rope_t(dq_r, cos_, sin_) * q_scale
            dk_ref[:, ks] = _rope_t(dk_r, cos_, sin_)

    tab = lambda w: pl.BlockSpec((hpg, ch, w), lambda h, rr: (h, 0, 0))
    return pl.pallas_call(
        body, name=name, grid=(heads // hpg, nch),
        in_specs=[pl.BlockSpec((ch, hpg * dk), lambda h, rr: (cof(rr), qb + h)),
                  pl.BlockSpec((ch, hpg * dk), lambda h, rr: (cof(rr), kb + h))] + v_specs +
                 [pl.BlockSpec((ch, dk), lambda h, rr: (cof(rr), 0)),
                  pl.BlockSpec((ch, dk), lambda h, rr: (cof(rr), 0)),
                  tab(ch), tab(dk), tab(dk), tab(dv),
                  pl.BlockSpec((hpg, 1, dk, dv), lambda h, rr: (h, nch - 1 - rr, 0, 0)),
                  pl.BlockSpec((ch, hpg * dv), lambda h, rr: (jnp.maximum(cof(rr) - ncc, 0), h))],
        out_specs=[pl.BlockSpec((ch, hpg * dk), lambda h, rr: (cof(rr), h)),
                   pl.BlockSpec((ch, hpg * dk), lambda h, rr: (cof(rr), h)),
                   pl.BlockSpec((ch, hpg * dv), lambda h, rr: (cof(rr), h)),
                   tab(ch), tab(dk), tab(dk), tab(dv)],
        out_shape=[_sds((r, heads * dk), F32), _sds((r, heads * dk), F32), _sds((r, heads * dv), F32),
                   _sds(decay.shape, F32), _sds(wend.shape, F32), _sds(win.shape, F32), _sds(gch.shape, F32)],
        scratch_shapes=[pltpu.VMEM((hpg, dk, dv), F32)],
        compiler_params=_params(("parallel", "arbitrary")))(hm, hm, *([hm] * nv), cos, sin, decay, wend, win, gch, s_in, do)


_HBM = pl.BlockSpec(memory_space=pltpu.HBM)
_MESH = pl.DeviceIdType.MESH
ALL_GATHER_COLLECTIVE_ID = 1
SIBLING_COLLECTIVE_ID = 2
CHIPS_COLLECTIVE_ID = 3


def _axis_slice(ref, axis, start, size):
    idx = [slice(None)] * len(ref.shape)
    idx[axis] = pl.ds(start, size)
    return ref.at[tuple(idx)]


def _sibling_and_chip_peers():
    x, y, c = lax.axis_index("x"), lax.axis_index("y"), lax.axis_index("c")
    return [(x, y, 1 - c), (1 - x, y, c), (x, 1 - y, c), (1 - x, 1 - y, c)]


def _launch_exchange(body, name, operand, out_shape, sems, peers_fn, collective_id, on_sequencer):
    if not on_sequencer:
        return pl.pallas_call(body, name=name, out_shape=out_shape, in_specs=[_HBM], out_specs=_HBM,
                              scratch_shapes=sems)(operand)

    def sequencer_body(in_ref, out_ref, *sem_refs):
        peers = peers_fn()
        barrier = pltpu.get_barrier_semaphore()
        for peer in peers:
            pl.semaphore_signal(barrier, inc=1, device_id=peer, device_id_type=_MESH)
        pl.semaphore_wait(barrier, len(peers))
        body(in_ref, out_ref, *sem_refs)

    return pl.kernel(sequencer_body, out_type=out_shape, name=name,
                     mesh=plsc.ScalarSubcoreMesh(axis_name="sequencer", num_cores=1), scratch_types=sems,
                     compiler_params=pltpu.CompilerParams(collective_id=collective_id))(operand)


def _all_gather(shard, axis, name, on_sequencer=False):
    m = shard.shape[axis]
    out_shape = list(shard.shape)
    out_shape[axis] = N_DEV * m

    def body(x_ref, out_ref, send_sems, recv_sems, local_sem):
        x, y, c = lax.axis_index("x"), lax.axis_index("y"), lax.axis_index("c")
        me, sibling = (x, y, c), (x, y, 1 - c)
        chips = [(1 - x, y), (x, 1 - y), (1 - x, 1 - y)]

        def block(px, py, pc):
            return _axis_slice(out_ref, axis, (4 * px + 2 * py + pc) * m, m)

        def copy(k, blk, to, src=None):
            return pltpu.make_async_remote_copy(
                src_ref=block(*blk) if src is None else src, dst_ref=block(*blk), send_sem=send_sems.at[k],
                recv_sem=recv_sems.at[k], device_id=to, device_id_type=_MESH)

        mine = pltpu.make_async_copy(x_ref, block(*me), local_sem)
        mine.start()
        first = [copy(0, me, sibling, src=x_ref)]
        first += [copy(1 + j, me, (*chip, c), src=x_ref) for j, chip in enumerate(chips)]
        for cp in first:
            cp.start()
        passed = [copy(4 + j, (*chip, c), sibling) for j, chip in enumerate(chips)]
        for j, chip in enumerate(chips):
            copy(1 + j, (*chip, c), me).wait_recv()
            passed[j].start()
        copy(0, sibling, me).wait_recv()
        for j, chip in enumerate(chips):
            copy(4 + j, (*chip, 1 - c), me).wait_recv()
        for cp in first + passed:
            cp.wait_send()
        mine.wait()

    return _launch_exchange(
        body, name, shard, _sds(out_shape, shard.dtype),
        [pltpu.SemaphoreType.DMA((7,)), pltpu.SemaphoreType.DMA((7,)), pltpu.SemaphoreType.DMA(())],
        _sibling_and_chip_peers, ALL_GATHER_COLLECTIVE_ID, on_sequencer)


def _rs_sibling(g, axis, name, on_sequencer=False):
    m = g.shape[axis] // N_DEV
    blk_shape = list(g.shape)
    blk_shape[axis] = m
    n_chips = N_DEV // 2

    def body(g_ref, recv_ref, send_sems, recv_sems):
        x, y, c = lax.axis_index("x"), lax.axis_index("y"), lax.axis_index("c")
        sibling = (x, y, 1 - c)
        send = [pltpu.make_async_remote_copy(
            src_ref=_axis_slice(g_ref, axis, (2 * q + 1 - c) * m, m), dst_ref=recv_ref.at[q],
            send_sem=send_sems.at[q], recv_sem=recv_sems.at[q], device_id=sibling, device_id_type=_MESH)
            for q in range(n_chips)]
        for cp in send:
            cp.start()
        for cp in send:
            cp.wait_recv()
        for cp in send:
            cp.wait_send()

    return _launch_exchange(
        body, name, g, _sds([n_chips] + blk_shape, g.dtype),
        [pltpu.SemaphoreType.DMA((n_chips,)), pltpu.SemaphoreType.DMA((n_chips,))],
        lambda: _sibling_and_chip_peers()[:1], SIBLING_COLLECTIVE_ID, on_sequencer)


def _rs_chips(p, name, on_sequencer=False):
    n_peers = p.shape[0] - 1

    def body(p_ref, out_ref, send_sems, recv_sems):
        x, y, c = lax.axis_index("x"), lax.axis_index("y"), lax.axis_index("c")
        chips = [(1 - x, y), (x, 1 - y), (1 - x, 1 - y)]
        send = [pltpu.make_async_remote_copy(
            src_ref=p_ref.at[2 * cx + cy], dst_ref=out_ref.at[j], send_sem=send_sems.at[j],
            recv_sem=recv_sems.at[j], device_id=(cx, cy, c), device_id_type=_MESH)
            for j, (cx, cy) in enumerate(chips)]
        for cp in send:
            cp.start()
        for cp in send:
            cp.wait_recv()
        for cp in send:
            cp.wait_send()

    return _launch_exchange(
        body, name, p, _sds((n_peers,) + p.shape[1:], p.dtype),
        [pltpu.SemaphoreType.DMA((n_peers,)), pltpu.SemaphoreType.DMA((n_peers,))],
        lambda: _sibling_and_chip_peers()[1:], CHIPS_COLLECTIVE_ID, on_sequencer)


def _reduce_scatter(g, axis, name):
    sib = _rs_sibling(g, axis, name + "_d2d", on_sequencer=True)
    p = _pair_sum(g, sib, axis, name + "_pair")
    return p, _rs_chips(p, name + "_ici", on_sequencer=True)


def _s5_tables(lam_re, lam_im, log_step, b_re, b_im, c_re, c_im):
    nd, g, p, cg = b_re.shape
    step = jnp.exp(log_step)[..., None]
    mag = jnp.exp(lam_re * step)
    a_re, a_im = mag * jnp.cos(lam_im * step), mag * jnp.sin(lam_im * step)
    den = lam_re * lam_re + lam_im * lam_im
    num_re, num_im = a_re - 1.0, a_im
    k_re = (num_re * lam_re + num_im * lam_im) / den
    k_im = (num_im * lam_re - num_re * lam_im) / den
    bb_re = k_re[..., None] * b_re - k_im[..., None] * b_im
    bb_im = k_re[..., None] * b_im + k_im[..., None] * b_re
    gt = g // SSM_TILE_GROUPS
    hg = SSM_HALF_GROUPS
    eye = jnp.eye(SSM_TILE_GROUPS, dtype=F32).reshape(SSM_TILE_GROUPS, 2, hg)

    def pack_b(bb):
        w = jnp.einsum("djhqpc,ghq->djhgcqp", bb.reshape(nd, gt, 2, hg, p, cg), eye)
        return w.reshape(nd, gt * 2, SSM_TILE_GROUPS * cg, hg * p)

    def pack_c(cc):
        w = jnp.einsum("djhqcp,ghq->djhqpgc", cc.reshape(nd, gt, 2, hg, cg, p), eye)
        return w.reshape(nd, gt * 2, hg * p, SSM_TILE_GROUPS * cg)

    a = jnp.stack([a_re.reshape(nd, gt * 2, hg * p), a_im.reshape(nd, gt * 2, hg * p)], axis=2)
    w = jnp.concatenate([pack_b(bb_re), pack_b(bb_im)], axis=-1)
    c = jnp.concatenate([pack_c(c_re), -pack_c(c_im)], axis=-2)
    return w, c, a


def _ret_tables(decay_logit, dk, dv):
    ch = RET_CHUNK
    nd, h = decay_logit.shape
    lg = jax.nn.log_sigmoid(decay_logit)[:, :, None]
    pos = jnp.arange(ch, dtype=F32)
    fwd_diff = pos[:, None] - pos[None, :]
    diff = jnp.stack([fwd_diff, -fwd_diff])[:, None]
    mask = jnp.stack([fwd_diff >= 0, -fwd_diff > 0])[:, None]
    end_pos = jnp.stack([ch - 1.0 - pos, pos])[:, None]
    in_pos = jnp.stack([pos + 1.0, ch - pos])[:, None]
    w_end = jnp.exp(lg * end_pos)
    w_in = jnp.exp(lg * in_pos)
    decay = jnp.where(mask, jnp.exp(lg[..., None] * jnp.where(mask, diff, 0.0)), 0.0)
    g_chunk = jnp.exp(lg[..., 0] * ch)
    return (decay, jnp.broadcast_to(w_end[..., None], (nd, h, ch, dk)), jnp.broadcast_to(w_in[..., None], (nd, h, ch, dk)),
            jnp.broadcast_to(g_chunk[..., None, None], (nd, h, dk, dv)))


def _rope_tables(t_rows, ncc, dk):
    quarter = dk // 4
    idx = np.arange(t_rows)
    row, col = idx // GRID_W, idx % GRID_W
    inv = ROPE_BASE ** (-np.arange(quarter, dtype=np.float32) / quarter)
    ang_r = row.astype(np.float32)[:, None] * inv
    ang_c = col.astype(np.float32)[:, None] * inv
    ang_r, ang_c = jnp.asarray(ang_r, F32), jnp.asarray(ang_c, F32)
    cos = jnp.concatenate([jnp.cos(ang_r), jnp.cos(ang_r), jnp.cos(ang_c), jnp.cos(ang_c)], axis=1)
    sin = jnp.concatenate([-jnp.sin(ang_r), jnp.sin(ang_r), -jnp.sin(ang_c), jnp.sin(ang_c)], axis=1)
    n_ctx = ncc * RET_CHUNK
    cos = jnp.concatenate([jnp.ones((n_ctx, dk), F32), cos], axis=0)
    sin = jnp.concatenate([jnp.zeros((n_ctx, dk), F32), sin], axis=0)
    return cos, sin


def _to_scan_layout(ctx_rows, lat_rows, rev):
    u = jnp.concatenate([lat_rows, ctx_rows] if rev else [ctx_rows, lat_rows], axis=0)
    r, w = u.shape
    return u.reshape(N_SEG, r // N_SEG, w).transpose(1, 0, 2).reshape(r, w)


def _from_scan_layout(yp, n_ctx, rev):
    r, w = yp.shape
    y = yp.reshape(r // N_SEG, N_SEG, w).transpose(1, 0, 2).reshape(r, w)
    return (y[r - n_ctx:], y[:r - n_ctx]) if rev else (y[:n_ctx], y[n_ctx:])


def _pack(parts, width):
    rows = []
    for p in parts:
        flat = p.reshape(-1).astype(F32)
        n = flat.shape[0]
        rows.append(jnp.pad(flat, (0, -n % (SUBLANE * width))).reshape(-1, width))
    return jnp.concatenate(rows, axis=0)


def _packed_rows(n, width):
    return -(-n // (SUBLANE * width)) * SUBLANE


def _unpack(flat2d, shapes):
    width = flat2d.shape[1]
    out, row = [], 0
    for shp in shapes:
        n = int(np.prod(shp))
        nr = _packed_rows(n, width)
        out.append(flat2d[row:row + nr].reshape(-1)[:n].reshape(shp))
        row += nr
    return out


def kernel(x, c, ctx, c_ctx, ada_w, ada_b, norm_g, ffn_w_in, ffn_w_out, mix_w_in, ssm_lam_re, ssm_lam_im, ssm_log_step, ssm_b_re, ssm_b_im, ssm_c_re, ssm_c_im, ssm_d, ssm_glu_w, ret_decay_logit, ret_w_proj, mix_w_out, loss_target, m_c_ctx, m_ada_w, m_ada_b, m_norm_g, m_ffn_w_in, m_ffn_w_out, m_mix_w_in, m_ssm_lam_re, m_ssm_lam_im, m_ssm_log_step, m_ssm_b_re, m_ssm_b_im, m_ssm_c_re, m_ssm_c_im, m_ssm_d, m_ssm_glu_w, m_ret_decay_logit, m_ret_w_proj, m_mix_w_out, v_c_ctx, v_ada_w, v_ada_b, v_norm_g, v_ffn_w_in, v_ffn_w_out, v_mix_w_in, v_ssm_lam_re, v_ssm_lam_im, v_ssm_log_step, v_ssm_b_re, v_ssm_b_im, v_ssm_c_re, v_ssm_c_im, v_ssm_d, v_ssm_glu_w, v_ret_decay_logit, v_ret_w_proj, v_mix_w_out):
    t_rows, d = x.shape[1], x.shape[2]
    n_ctx = ctx.shape[1]
    r = n_ctx + t_rows
    ssm_w = ssm_d.shape[1]
    heads = ret_decay_logit.shape[2]
    mi = mix_w_in.shape[2] * N_DEV
    dk = (mi - ssm_w - 2 * d) // (6 * heads)
    dv = 2 * dk
    qk_w, v_w = heads * dk, heads * dv
    q_off = ssm_w
    ncc = n_ctx // RET_CHUNK
    tile = n_ctx
    nct = 1
    wide_tile = _tile(n_ctx, 128, 16)
    assert r % (N_SEG * SUBLANE) == 0 and n_ctx % RET_CHUNK == 0 and t_rows % tile == 0
    me = 4 * lax.axis_index("x") + 2 * lax.axis_index("y") + lax.axis_index("c")
    g_off = ssm_w + 2 * qk_w + v_w
    gs_off = g_off + v_w

    ng_cols = norm_g.shape[2]
    small0 = _pack([c[0], norm_g[0]], d)
    small0_all = _all_gather(small0, 0, "ag_cond")

    bf = lambda w: w.astype(BF16)
    small0_all, sh_in1, sh_out1, sh_mix = lax.optimization_barrier(
        (small0_all, bf(ffn_w_in[0, 0]), bf(ffn_w_out[0, 0]), bf(mix_w_in[0])))
    small0_all = small0_all.reshape(N_DEV, -1)
    half_k = d // 2
    w_in1_top = _all_gather(sh_in1[:half_k], 1, "ag_ffn1_in_top", on_sequencer=True)
    w_in1_bot = _all_gather(sh_in1[half_k:], 1, "ag_ffn1_in_bot", on_sequencer=True)
    w_out1 = _all_gather(sh_out1, 0, "ag_ffn1_out", on_sequencer=True)
    w_mix = _all_gather(sh_mix, 1, "ag_mix_in", on_sequencer=True)
    w_glu = _all_gather(bf(ssm_glu_w[0]), 1, "ag_glu", on_sequencer=True)
    w_rp = _all_gather(bf(ret_w_proj[0]), 0, "ag_ret_proj", on_sequencer=True)
    w_mo = _all_gather(bf(mix_w_out[0]), 0, "ag_mix_out", on_sequencer=True)
    w_in2 = _all_gather(bf(ffn_w_in[0, 1]), 1, "ag_ffn2_in", on_sequencer=True)
    w_out2 = _all_gather(bf(ffn_w_out[0, 1]), 0, "ag_ffn2_out", on_sequencer=True)

    ng_at = _packed_rows(d, d) * d
    c_all = small0_all[:, :d]
    g_full = small0_all[:, ng_at:ng_at + 6 * ng_cols].reshape(N_DEV, 6, ng_cols).transpose(1, 0, 2).reshape(6, d)
    g6 = g_full.reshape(6, 1, d)
    cc = jnp.concatenate([c_all, c_ctx[None, :], jnp.zeros((2 * SUBLANE - N_DEV - 1, d), F32)], axis=0)
    sc = _silu_rows(cc, "ada_silu")
    na = ada_w.shape[2]
    a_loc = _mm(sc, ada_w[0], "nn", F32, "ada_fwd", tm=16, tn=na, tk=512)
    a_all = _all_gather(a_loc, 0, "ag_ada").reshape(N_DEV, 2 * SUBLANE, na)
    ada_x = lax.dynamic_index_in_dim(a_all, me, axis=1, keepdims=False).reshape(9 * d) + ada_b[0]
    ada_c = a_all[:, N_DEV, :].reshape(9 * d) + ada_b[0]
    mods = jnp.stack([ada_c.reshape(9, d), ada_x.reshape(9, d)]).reshape(18, 1, d)

    xin = jnp.concatenate([ctx[0], x[0]], axis=0)
    u1 = _ada_pre_fwd(xin, g6, mods, 0, 0, nct, tile, "pre1")
    h1_top = _mm_k_part(u1, w_in1_top, 0, 2, None, F32, "ffn1_in_top", tm=544)
    g1, up1, a1 = _mm_swiglu(u1, w_in1_bot, "ffn1_in_bot", 1, 2, h1_top, tm=544)
    w_in1 = jnp.concatenate([w_in1_top, w_in1_bot], axis=0)
    o1 = _mm(a1, w_out1, "nn", F32, "ffn1_out", tm=544, tn=1024, tk=2816)
    x1 = _ada_post_fwd(xin, o1, g6, mods, 1, 0, 0.5, nct, tile, "post1")
    u2 = _ada_pre_fwd(x1, g6, mods, 2, 1, nct, tile, "pre2")
    hm = _mm(u2, w_mix, "nn", F32, "mix_in", tm=544, tn=1024)

    us_ctx, us_lat = hm[:n_ctx, :ssm_w], hm[n_ctx:, :ssm_w]
    dskip = ssm_d.reshape(1, 1, ssm_w)
    s5_prm = (ssm_lam_re[0], ssm_lam_im[0], ssm_log_step[0], ssm_b_re[0], ssm_b_im[0], ssm_c_re[0], ssm_c_im[0])
    s5_tabs_both, s5_vjp = jax.vjp(_s5_tables, *s5_prm)
    s5_tabs, ups, y_dirs = [], [], []
    for dr in range(2):
        tabs = tuple(t[dr] for t in s5_tabs_both)
        up = _to_scan_layout(us_ctx, us_lat, dr == 1)
        yp = _s5_fwd(up, *tabs, dr == 1, "s5_fwd%d" % dr)
        s5_tabs.append(tabs)
        ups.append(up)
        y_dirs.append(_from_scan_layout(yp, n_ctx, dr == 1)[1])
    a_ssm = _ssm_out_fwd(y_dirs[0], y_dirs[1], hm, dskip, nct, tile, "ssm_out")
    gab = _mm(a_ssm, w_glu, "nn", F32, "glu", tm=512, tn=2048, tk=ssm_w)

    cos, sin = _rope_tables(t_rows, ncc, dk)
    ret_tabs_both, ret_vjp = jax.vjp(functools.partial(_ret_tables, dk=dk, dv=dv), ret_decay_logit[0])
    ret_tabs, o_dirs, s_ins = [], [], []
    for dr in range(2):
        tabs = tuple(t[dr] for t in ret_tabs_both)
        o_d, s_in = _ret_fwd(hm, cos, sin, *tabs, heads, dk, dv, q_off, ncc, dr == 1, "ret_fwd%d" % dr)
        ret_tabs.append(tabs)
        o_dirs.append(o_d)
        s_ins.append(s_in)
    ret_in = _ret_gate_fwd(o_dirs[0], o_dirs[1], hm, g_off, heads, dv, nct, tile, "ret_gate")
    rb = _mm(ret_in, w_rp, "nn", F32, "ret_proj", tm=512, tn=d, tk=v_w)
    merged = _merge_fwd(gab, rb, hm, gs_off, nct, tile, "merge")
    mix = _mm(merged, w_mo, "nn", F32, "mix_out", tm=512, tn=d, tk=d)
    x2 = _ada_post_fwd(x1, mix, g6, mods, 3, 1, 1.0, 0, tile, "post2", h_tile_offset=nct)
    u3 = _ada_pre_fwd(x2, g6, mods, 4, 2, 0, tile, "pre3")
    g3, up3, a3 = _mm_swiglu(u3, w_in2, "ffn2_in", tm=512)
    o3 = _mm(a3, w_out2, "nn", F32, "ffn2_out", tm=512, tn=1024, tk=2816)
    x3 = _ada_post_fwd(x2, o3, g6, mods, 5, 2, 0.5, 0, tile, "post3")
    dy, lcols = _loss_grad(x3, loss_target[0], tile, "loss")
    loss_part = (0.5 * jnp.sum(lcols) / d).reshape(1)

    dg6 = [None] * 6
    dmod = {}

    def add_mod(sel_rows, k, val):
        for sel, row in sel_rows:
            dmod[(sel, k)] = dmod.get((sel, k), 0.0) + val[row, 0]

    both, lat = [(0, 0), (1, 1)], [(1, 0)]

    def tie(*vals):
        return lax.optimization_barrier(vals)

    def big_update(w3d, m3d, v3d, layer, gfull, axis, name, filled=None):
        p, recv = _reduce_scatter(gfull, axis, "rs_" + name)
        return _adamw_scattered(w3d, m3d, v3d, layer, p, recv, "adamw_" + name, filled)

    do3, dg6[5], dgt = _ada_post_bwd(dy, o3, g6, mods, 5, 2, 0.5, 0, 1, tile, "post3_bwd")
    add_mod(lat, 8, dgt)
    gw_out2 = _mm(a3, do3, "tn", BF16, "ffn2_out_dw", tm=1408, tn=1024, tk=2176)
    do3, gw_out2 = tie(do3, gw_out2)
    up_out2 = big_update(ffn_w_out[0], m_ffn_w_out[0], v_ffn_w_out[0], 1, gw_out2, 0, "ffn2_out")
    da3 = _mm(do3, w_out2, "nt", BF16, "ffn2_out_dx", tm=512, tn=2816, tk=d)
    dh3 = _swiglu_bwd(g3, up3, da3, wide_tile, "swiglu2_bwd")
    gw_in2 = _mm(u3, dh3, "tn", BF16, "ffn2_in_dw", tm=1024, tn=1024, tk=2176)
    dh3, gw_in2 = tie(dh3, gw_in2)
    up_in2 = big_update(ffn_w_in[0], m_ffn_w_in[0], v_ffn_w_in[0], 1, gw_in2, 1, "ffn2_in")
    du3 = _mm(dh3, w_in2, "nt", F32, "ffn2_in_dx", tm=512, tn=d, tk=1024)
    dx2, dg6[4], dsh, dsc = _ada_pre_bwd(x2, du3, dy, g6, mods, 4, 2, 0, 1, tile, "pre3_bwd")
    add_mod(lat, 6, dsh)
    add_mod(lat, 7, dsc)
    dmix, dg6[3], dgt = _ada_post_bwd(dx2, mix, g6, mods, 3, 1, 1.0, 0, 1, tile, "post2_bwd")
    add_mod(lat, 5, dgt)
    gw_mo = _mm(merged, dmix, "tn", BF16, "mix_out_dw", tm=1024, tn=1024, tk=2176)
    dmix, gw_mo = tie(dmix, gw_mo)
    up_mo = big_update(mix_w_out, m_mix_w_out, v_mix_w_out, 0, gw_mo, 0, "mix_out")
    dmerged = _mm(dmix, w_mo, "nt", F32, "mix_out_dx", tm=512, tn=d, tk=d)
    dgab, drb, dgs, dgr = _merge_bwd(gab, rb, hm, gs_off, dmerged, nct, tile, "merge_bwd")
    gw_glu = _mm(a_ssm, dgab, "tn", BF16, "glu_dw", tm=1024, tn=1024, tk=2176)
    gw_rp = _mm(ret_in, drb, "tn", BF16, "ret_proj_dw", tm=1024, tn=1024, tk=2176)
    dgab, drb, gw_glu, gw_rp = tie(dgab, drb, gw_glu, gw_rp)
    up_glu = big_update(ssm_glu_w, m_ssm_glu_w, v_ssm_glu_w, 0, gw_glu, 1, "glu")
    up_rp = big_update(ret_w_proj, m_ret_w_proj, v_ret_w_proj, 0, gw_rp, 0, "ret_proj")
    da_ssm = _mm(dgab, w_glu, "nt", F32, "glu_dx", tm=512, tn=ssm_w, tk=2 * d)
    dret_in = _mm(drb, w_rp, "nt", F32, "ret_proj_dx", tm=512, tn=v_w, tk=d)
    d_o, dg_gate = _ret_gate_bwd(o_dirs[0], o_dirs[1], hm, g_off, dret_in, heads, dv, nct, tile, "ret_gate_bwd")
    dy_ssm, dus_direct, d_dskip = _ssm_out_bwd(y_dirs[0], y_dirs[1], hm, dskip, da_ssm, nct, tile, "ssm_out_bwd")
    s5_table_grads, du_ctx, du_lat = [], [], [dus_direct]
    for dr in range(2):
        dyp = _to_scan_layout(jnp.zeros((n_ctx, ssm_w), F32), dy_ssm, dr == 1)
        if dr == 1:
            dyp, up_out2, up_in2 = tie(dyp, up_out2, up_in2)
        outs = _s5_bwd(ups[dr], dyp, *s5_tabs[dr], dr == 1, "s5_bwd%d" % dr)
        part_ctx, part_lat = _from_scan_layout(outs[0], n_ctx, dr == 1)
        du_ctx.append(part_ctx)
        du_lat.append(part_lat)
        s5_table_grads.append(outs[1:])
    dqkv, ret_table_grads = [], []
    for dr in range(2):
        if dr == 1:
            d_o, up_mo, up_glu, up_rp = tie(d_o, up_mo, up_glu, up_rp)
        outs = _ret_bwd(hm, cos, sin, *ret_tabs[dr], s_ins[dr], d_o, heads, dk, dv, q_off, ncc, dr == 1,
                        "ret_bwd%d" % dr)
        dqkv.append(outs[:3])
        ret_table_grads.append(outs[3:])
    both_dirs = lambda grads: tuple(jnp.stack([g0, g1]) for g0, g1 in zip(*grads))
    early_parts = list(s5_vjp(both_dirs(s5_table_grads))) + list(ret_vjp(both_dirs(ret_table_grads)))
    s5_names = 7
    early_shapes = [p.shape for p in early_parts]
    early_all = _all_gather(_pack(early_parts, 1024), 0, "ag_s5_grads", on_sequencer=True)
    early_sum = _sum_leading(early_all.reshape(N_DEV, -1, 1024), "sum_s5_grads")
    dus = jnp.concatenate([du_ctx[0] + du_ctx[1], du_lat[0] + du_lat[1] + du_lat[2]], axis=0)
    dhm = _assemble_dhm(dus, dqkv[0][0], dqkv[1][0], dqkv[0][1], dqkv[1][1], dqkv[0][2], dqkv[1][2],
                        dg_gate, dgs, dgr, n_ctx // wide_tile, wide_tile, "assemble_dhm")
    gw_mix = _mm(u2, dhm, "tn", BF16, "mix_in_dw", tm=1024, tn=1024, tk=2176)
    dhm, gw_mix = tie(dhm, gw_mix)
    up_mix = big_update(mix_w_in, m_mix_w_in, v_mix_w_in, 0, gw_mix, 1, "mix_in")
    du2 = _mm(dhm, w_mix, "nt", F32, "mix_in_dx", tm=544, tn=d, tk=1024)
    dx1, dg6[2], dsh, dsc = _ada_pre_bwd(x1, du2, dx2, g6, mods, 2, 1, nct, 2, tile, "pre2_bwd", dres_x_only=True)
    add_mod(both, 3, dsh)
    add_mod(both, 4, dsc)
    do1, dg6[1], dgt = _ada_post_bwd(dx1, o1, g6, mods, 1, 0, 0.5, nct, 2, tile, "post1_bwd")
    add_mod(both, 2, dgt)
    gw_out1 = _mm(a1, do1, "tn", BF16, "ffn1_out_dw", tm=1408, tn=1024, tk=2176)
    do1, gw_out1 = tie(do1, gw_out1)
    up_out1 = big_update(ffn_w_out[0], m_ffn_w_out[0], v_ffn_w_out[0], 0, gw_out1, 0, "ffn1_out", filled=up_out2)
    da1 = _mm(do1, w_out1, "nt", BF16, "ffn1_out_dx", tm=544, tn=2816, tk=d)
    dh1 = _swiglu_bwd(g1, up1, da1, wide_tile, "swiglu1_bwd")
    dh1, up_mix, early_sum = tie(dh1, up_mix, early_sum)
    early_sums = _unpack(early_sum, early_shapes)
    gw_in1 = _mm(u1, dh1, "tn", BF16, "ffn1_in_dw", tm=1024, tn=1024, tk=2176)
    dh1, gw_in1 = tie(dh1, gw_in1)
    up_in1 = big_update(ffn_w_in[0], m_ffn_w_in[0], v_ffn_w_in[0], 0, gw_in1, 1, "ffn1_in", filled=up_in2)
    du1 = _mm(dh1, w_in1, "nt", F32, "ffn1_in_dx", tm=544, tn=d, tk=1024)
    dxin, dg6[0], dsh, dsc = _ada_pre_bwd(xin, du1, dx1, g6, mods, 0, 0, nct, 2, tile, "pre1_bwd")
    add_mod(both, 0, dsh)
    add_mod(both, 1, dsc)
    grad_x = dxin[n_ctx:][None]

    zero_d = jnp.zeros((d,), F32)
    d_ada_x = jnp.stack([dmod.get((1, k), zero_d) for k in range(9)]).reshape(9 * d)
    d_ada_c = jnp.stack([dmod.get((0, k), zero_d) for k in range(9)]).reshape(9 * d)
    dg_full = jnp.stack([g[0, 0] for g in dg6])
    small_parts = [d_ada_x, d_ada_c, dg_full, d_dskip, loss_part]
    small_shapes = [p.shape for p in small_parts]
    packed = _pack(small_parts, 1024)
    gathered = _all_gather(packed, 0, "ag_small_grads").reshape(N_DEV, -1, 1024)
    summed = _sum_leading(gathered, "sum_small_grads")
    sums = _unpack(summed, small_shapes)
    sum_dx, sum_dc, sum_dg = sums[0], sums[1], sums[2]
    loss = sums[4][0]
    grad_ada_b = (sum_dx + sum_dc)[None]
    dx_rows = gathered.reshape(N_DEV, -1)[:, :9 * d]
    col0 = me * na
    da_rows = jnp.concatenate([lax.dynamic_slice_in_dim(dx_rows, col0, na, axis=1),
                               lax.dynamic_slice_in_dim(sum_dc[None], col0, na, axis=1),
                               jnp.zeros((2 * SUBLANE - N_DEV - 1, na), F32)], axis=0)
    grad_ada_w = _mm(sc, da_rows, "tn", F32, "ada_dw", tm=512, tn=na, tk=16)
    d_sc = _mm(da_rows, ada_w[0], "nt", F32, "ada_dx", tm=16, tn=512, tk=na)
    d_sc_all = _all_gather(jnp.broadcast_to(d_sc[N_DEV:N_DEV + 1], (SUBLANE, d)), 0, "ag_dctx")
    d_sc_sum = _sum_leading(d_sc_all.reshape(N_DEV, SUBLANE, d), "sum_dctx")
    grad_c_ctx = _silu_grad_rows(jnp.broadcast_to(c_ctx[None], (SUBLANE, d)), d_sc_sum, "ctx_silu_bwd")[0]
    grad_norm_g = lax.dynamic_slice_in_dim(sum_dg, me * ng_cols, ng_cols, axis=1)[None]

    upd = {}
    upd["ffn_w_in"] = [o[None] for o in up_in1]
    upd["ffn_w_out"] = [o[None] for o in up_out1]
    upd["mix_w_in"] = list(up_mix)
    upd["ssm_glu_w"] = list(up_glu)
    upd["ret_w_proj"] = list(up_rp)
    upd["mix_w_out"] = list(up_mo)
    upd["ada_w"] = [o[None] for o in _adamw(ada_w[0], m_ada_w[0], v_ada_w[0], grad_ada_w[None], "adamw_ada_w")]

    small_names = ["c_ctx", "ada_b", "norm_g", "ssm_lam_re", "ssm_lam_im", "ssm_log_step", "ssm_b_re", "ssm_b_im",
                   "ssm_c_re", "ssm_c_im", "ssm_d", "ret_decay_logit"]
    small_w = [c_ctx, ada_b, norm_g, ssm_lam_re, ssm_lam_im, ssm_log_step, ssm_b_re, ssm_b_im, ssm_c_re, ssm_c_im,
               ssm_d, ret_decay_logit]
    small_m = [m_c_ctx, m_ada_b, m_norm_g, m_ssm_lam_re, m_ssm_lam_im, m_ssm_log_step, m_ssm_b_re, m_ssm_b_im,
               m_ssm_c_re, m_ssm_c_im, m_ssm_d, m_ret_decay_logit]
    small_v = [v_c_ctx, v_ada_b, v_norm_g, v_ssm_lam_re, v_ssm_lam_im, v_ssm_log_step, v_ssm_b_re, v_ssm_b_im,
               v_ssm_c_re, v_ssm_c_im, v_ssm_d, v_ret_decay_logit]
    small_g = [grad_c_ctx, grad_ada_b, grad_norm_g] + [s[None] for s in early_sums[:s5_names]] + \
              [sums[3].reshape(ssm_d.shape), early_sums[s5_names][None]]
    shapes = [w.shape for w in small_w]
    res = _adamw(_pack(small_w, 1024), _pack(small_m, 1024), _pack(small_v, 1024), _pack(small_g, 1024)[None],
                 "adamw_small")
    small_out = [_unpack(o, shapes) for o in res]
    for i, nm in enumerate(small_names):
        upd[nm] = [small_out[kind][i] for kind in range(4)]

    order = ["c_ctx", "ada_w", "ada_b", "norm_g", "ffn_w_in", "ffn_w_out", "mix_w_in", "ssm_lam_re", "ssm_lam_im",
             "ssm_log_step", "ssm_b_re", "ssm_b_im", "ssm_c_re", "ssm_c_im", "ssm_d", "ssm_glu_w", "ret_decay_logit",
             "ret_w_proj", "mix_w_out"]
    outs = [loss, grad_x]
    for kind in range(4):
        outs += [upd[nm][kind] for nm in order]
    return tuple(outs)
```

```python
import functools
import math

import jax
import jax.numpy as jnp
import numpy as np
from jax import lax
from jax.experimental import pallas as pl
from jax.experimental.pallas import tpu as pltpu
from jax.experimental.pallas import tpu_sc as plsc

F32 = jnp.float32
BF16 = jnp.bfloat16
MXU_DTYPE = jnp.bfloat16
MESH_AXES = ("x", "y", "c")
N_DEV = 8
V7X_VMEM_LIMIT_BYTES = 56 * 1024 * 1024
LANE = 128
SUBLANE = 8

GRID_W = 64
RET_CHUNK = 128
ROPE_BASE = 10000.0
NORM_EPS = 1e-6
ADAM_LR = 0.001
ADAM_B1 = 0.9
ADAM_B2 = 0.999
ADAM_EPS = 1e-08
ADAM_WD = 0.01
ADAM_STEP = 10
SSM_TILE_GROUPS = 8
SSM_HALF_GROUPS = 4
N_SEG = 16


def _params(sem=None):
    return pltpu.CompilerParams(dimension_semantics=sem, vmem_limit_bytes=V7X_VMEM_LIMIT_BYTES)


def _tile(n, target, mult):
    best = None
    t = mult
    while t <= min(n, target):
        if n % t == 0:
            best = t
        t += mult
    return n if best is None else best


def _sds(shape, dtype):
    return jax.ShapeDtypeStruct(tuple(shape), dtype)


def _mm(a, b, dims, out_dtype, name, tm=512, tn=1408, tk=2048):
    if dims == "nn":
        (m, k), (k2, n) = a.shape, b.shape
    elif dims == "nt":
        (m, k), (n, k2) = a.shape, b.shape
    else:
        (k, m), (k2, n) = a.shape, b.shape
    assert k == k2, (a.shape, b.shape, dims)
    tm = _tile(m, tm, 16)
    tn = _tile(n, tn, LANE)
    tk = _tile(k, tk, LANE if dims != "tn" else 16)
    nk = k // tk
    dn = {"nn": (((1,), (0,)), ((), ())), "nt": (((1,), (1,)), ((), ())), "tn": (((0,), (0,)), ((), ()))}[dims]

    def product(a_ref, b_ref):
        return lax.dot_general(a_ref[...].astype(MXU_DTYPE), b_ref[...].astype(MXU_DTYPE), dn,
                               preferred_element_type=F32)

    def body_single(a_ref, b_ref, o_ref):
        o_ref[...] = product(a_ref, b_ref).astype(o_ref.dtype)

    def body(a_ref, b_ref, o_ref, acc_ref):
        kk = pl.program_id(2)

        @pl.when(kk == 0)
        def _():
            acc_ref[...] = product(a_ref, b_ref)

        @pl.when((kk > 0) & (kk < nk - 1))
        def _():
            acc_ref[...] += product(a_ref, b_ref)

        @pl.when(kk == nk - 1)
        def _():
            o_ref[...] = (acc_ref[...] + product(a_ref, b_ref)).astype(o_ref.dtype)

    if dims == "nn":
        a_spec = pl.BlockSpec((tm, tk), lambda j, i, kk: (i, kk))
        b_spec = pl.BlockSpec((tk, tn), lambda j, i, kk: (kk, j))
    elif dims == "nt":
        a_spec = pl.BlockSpec((tm, tk), lambda j, i, kk: (i, kk))
        b_spec = pl.BlockSpec((tn, tk), lambda j, i, kk: (j, kk))
    else:
        a_spec = pl.BlockSpec((tk, tm), lambda j, i, kk: (kk, i))
        b_spec = pl.BlockSpec((tk, tn), lambda j, i, kk: (kk, j))
    return pl.pallas_call(
        body_single if nk == 1 else body, name=name, grid=(n // tn, m // tm, nk), in_specs=[a_spec, b_spec],
        out_specs=pl.BlockSpec((tm, tn), lambda j, i, kk: (i, j)), out_shape=_sds((m, n), out_dtype),
        scratch_shapes=[] if nk == 1 else [pltpu.VMEM((tm, tn), F32)],
        compiler_params=_params(("parallel", "parallel", "arbitrary")))(a, b)


def _mm_swiglu(a, b, name, tm=512, tn=512):
    m, k = a.shape
    k2, f2 = b.shape
    f = f2 // 2
    assert k == k2
    tm = _tile(m, tm, 16)
    tn = _tile(f, tn, 2 * LANE)
    nj = f // tn

    def body(a_ref, bg_ref, bu_ref, g_ref, u_ref, act_ref):
        av = a_ref[...].astype(MXU_DTYPE)
        gate = jnp.dot(av, bg_ref[...].astype(MXU_DTYPE), preferred_element_type=F32)
        up = jnp.dot(av, bu_ref[...].astype(MXU_DTYPE), preferred_element_type=F32)
        g_ref[...] = gate.astype(g_ref.dtype)
        u_ref[...] = up.astype(u_ref.dtype)
        act_ref[...] = (gate * _sigmoid(gate) * up).astype(act_ref.dtype)

    tile = pl.BlockSpec((tm, tn), lambda j, i: (i, j))
    out = _sds((m, f), BF16)
    return pl.pallas_call(
        body, name=name, grid=(nj, m // tm),
        in_specs=[pl.BlockSpec((tm, k), lambda j, i: (i, 0)), pl.BlockSpec((k, tn), lambda j, i: (0, j)),
                  pl.BlockSpec((k, tn), lambda j, i: (0, j + nj))],
        out_specs=[tile, tile, tile], out_shape=[out, out, out],
        compiler_params=_params(("parallel", "parallel")))(a, b, b)


def _rows(name, body, n_tiles, ins, outs):
    in_specs = [pl.BlockSpec(blk, imap) for (_, blk, imap) in ins]
    out_specs = [pl.BlockSpec(blk, imap) for (_, _, blk, imap) in outs]
    out_shape = [_sds(shape, dt) for (shape, dt, _, _) in outs]
    res = pl.pallas_call(body, name=name, grid=(n_tiles,), in_specs=in_specs, out_specs=out_specs,
                         out_shape=out_shape, compiler_params=_params(("arbitrary",)))(*[a for (a, _, _) in ins])
    return res


def _row_in(arr, tile, width=None, col=0, x_only_offset=None):
    width = arr.shape[1] if width is None else width
    if x_only_offset is None:
        return (arr, (tile, width), lambda i: (i, col))
    return (arr, (tile, width), lambda i: (jnp.maximum(i - x_only_offset, 0), col))


def _vec_in(arr, idx_fn):
    return (arr, (1, 1, arr.shape[2]), lambda i: (idx_fn(i), 0, 0))


def _rms(h):
    return lax.rsqrt(jnp.mean(h * h, axis=-1, keepdims=True) + NORM_EPS)


def _sigmoid(z):
    return 1.0 / (1.0 + jnp.exp(-z))


def _ada_pre_fwd(h, g6, mods, gi, mi, nct, tile, name):
    r, d = h.shape
    sel = lambda i: jnp.where(i >= nct, 1, 0)

    def body(h_ref, g_ref, sh_ref, sc_ref, u_ref):
        hh = h_ref[...]
        n = hh * _rms(hh) * g_ref[0]
        u_ref[...] = (n * (1.0 + sc_ref[0]) + sh_ref[0]).astype(u_ref.dtype)

    (u,) = _rows(name, body, r // tile,
                 [_row_in(h, tile), _vec_in(g6, lambda i: gi), _vec_in(mods, lambda i: sel(i) * 9 + 3 * mi),
                  _vec_in(mods, lambda i: sel(i) * 9 + 3 * mi + 1)],
                 [((r, d), BF16, (tile, d), lambda i: (i, 0))])
    return u


def _ada_pre_bwd(h, du, dres, g6, mods, gi, mi, nct, nsel, tile, name, dres_x_only=False, latent_dh_only=False):
    r, d = h.shape
    dh_rows = r - nct * tile if latent_dh_only else r
    dh_map = (lambda i: (jnp.maximum(i - nct, 0), 0)) if latent_dh_only else (lambda i: (i, 0))
    sel = lambda i: jnp.where(i >= nct, 1, 0) if nsel == 2 else 0
    msel = lambda i: jnp.where(i >= nct, 1, 0)
    off = nct if dres_x_only else None

    def body(h_ref, du_ref, dr_ref, g_ref, sc_ref, dh_ref, dg_ref, dsh_ref, dsc_ref):
        i = pl.program_id(0)
        hh = h_ref[...]
        rr = _rms(hh)
        g = g_ref[0]
        hn = hh * rr
        n = hn * g
        du_ = du_ref[...].astype(F32)
        dn = du_ * (1.0 + sc_ref[0])

        @pl.when(i == 0)
        def _():
            dg_ref[...] = jnp.zeros_like(dg_ref)

        @pl.when((i == 0) | (i == nct))
        def _():
            dsh_ref[...] = jnp.zeros_like(dsh_ref)
            dsc_ref[...] = jnp.zeros_like(dsc_ref)

        dg_ref[0] += jnp.sum(dn * hn, axis=0, keepdims=True)
        dsh_ref[0] += jnp.sum(du_, axis=0, keepdims=True)
        dsc_ref[0] += jnp.sum(du_ * n, axis=0, keepdims=True)
        t = dn * g
        dh = rr * t - hn * (rr * jnp.mean(t * hn, axis=-1, keepdims=True))
        if dres_x_only:
            dh_ref[...] = dh + jnp.where(i >= nct, dr_ref[...], 0.0)
        else:
            dh_ref[...] = dh + dr_ref[...]

    dh, dg, dsh, dsc = _rows(
        name, body, r // tile,
        [_row_in(h, tile), _row_in(du, tile), _row_in(dres, tile, x_only_offset=off), _vec_in(g6, lambda i: gi),
         _vec_in(mods, lambda i: msel(i) * 9 + 3 * mi + 1)],
        [((dh_rows, d), F32, (tile, d), dh_map), ((1, 1, d), F32, (1, 1, d), lambda i: (0, 0, 0)),
         ((nsel, 1, d), F32, (1, 1, d), lambda i: (sel(i), 0, 0)),
         ((nsel, 1, d), F32, (1, 1, d), lambda i: (sel(i), 0, 0))])
    return dh, dg, dsh, dsc


def _ada_post_fwd(h, o, g6, mods, gi, mi, res_w, nct, tile, name, h_tile_offset=0):
    r, d = o.shape
    sel = lambda i: jnp.where(i >= nct, 1, 0)

    def body(h_ref, o_ref, g_ref, gt_ref, y_ref):
        oo = o_ref[...]
        n = oo * _rms(oo) * g_ref[0]
        y_ref[...] = h_ref[...] + res_w * gt_ref[0] * n

    (y,) = _rows(name, body, r // tile,
                 [(h, (tile, d), lambda i: (i + h_tile_offset, 0)), _row_in(o, tile), _vec_in(g6, lambda i: gi),
                  _vec_in(mods, lambda i: sel(i) * 9 + 3 * mi + 2)],
                 [((r, d), F32, (tile, d), lambda i: (i, 0))])
    return y


def _ada_post_bwd(dy, o, g6, mods, gi, mi, res_w, nct, nsel, tile, name):
    r, d = o.shape
    sel = lambda i: jnp.where(i >= nct, 1, 0) if nsel == 2 else 0
    msel = lambda i: jnp.where(i >= nct, 1, 0)

    def body(dy_ref, o_ref, g_ref, gt_ref, do_ref, dg_ref, dgt_ref):
        i = pl.program_id(0)
        oo = o_ref[...]
        rr = _rms(oo)
        g = g_ref[0]
        on = oo * rr
        dy_ = dy_ref[...] * res_w

        @pl.when(i == 0)
        def _():
            dg_ref[...] = jnp.zeros_like(dg_ref)

        @pl.when((i == 0) | (i == nct))
        def _():
            dgt_ref[...] = jnp.zeros_like(dgt_ref)

        dgt_ref[0] += jnp.sum(dy_ * (on * g), axis=0, keepdims=True)
        dn = dy_ * gt_ref[0]
        dg_ref[0] += jnp.sum(dn * on, axis=0, keepdims=True)
        t = dn * g
        do_ref[...] = (rr * t - on * (rr * jnp.mean(t * on, axis=-1, keepdims=True))).astype(do_ref.dtype)

    do, dg, dgt = _rows(
        name, body, r // tile,
        [_row_in(dy, tile), _row_in(o, tile), _vec_in(g6, lambda i: gi),
         _vec_in(mods, lambda i: msel(i) * 9 + 3 * mi + 2)],
        [((r, d), BF16, (tile, d), lambda i: (i, 0)), ((1, 1, d), F32, (1, 1, d), lambda i: (0, 0, 0)),
         ((nsel, 1, d), F32, (1, 1, d), lambda i: (sel(i), 0, 0))])
    return do, dg, dgt


def _swiglu_bwd(gate, up, da, tile, name):
    r, f = gate.shape

    def body(g_ref, u_ref, da_ref, dh_ref):
        gt = g_ref[...].astype(F32)
        d = da_ref[...].astype(F32)
        sg = _sigmoid(gt)
        dh_ref[:, :f] = (d * u_ref[...].astype(F32) * (sg * (1.0 + gt * (1.0 - sg)))).astype(dh_ref.dtype)
        dh_ref[:, f:] = (d * gt * sg).astype(dh_ref.dtype)

    (dh,) = _rows(name, body, r // tile, [_row_in(gate, tile), _row_in(up, tile), _row_in(da, tile)],
                  [((r, 2 * f), BF16, (tile, 2 * f), lambda i: (i, 0))])
    return dh


def _gelu_parts(y):
    c0 = math.sqrt(2.0 / math.pi)
    inner = c0 * (y + 0.044715 * y * y * y)
    th = jnp.tanh(inner)
    return th, c0 * (1.0 + 3 * 0.044715 * y * y)


def _ssm_out_fwd(y0, y1, hm, dskip, nct, tile, name):
    t_rows, s = y0.shape

    def body(y0_ref, y1_ref, u_ref, d_ref, a_ref):
        y = y0_ref[...] + y1_ref[...] + d_ref[0] * u_ref[...]
        th, _ = _gelu_parts(y)
        a_ref[...] = (0.5 * y * (1.0 + th)).astype(a_ref.dtype)

    (a,) = _rows(name, body, t_rows // tile,
                 [_row_in(y0, tile), _row_in(y1, tile), (hm, (tile, s), lambda i: (i + nct, 0)),
                  _vec_in(dskip, lambda i: 0)],
                 [((t_rows, s), BF16, (tile, s), lambda i: (i, 0))])
    return a


def _ssm_out_bwd(y0, y1, hm, dskip, da, nct, tile, name):
    t_rows, s = y0.shape

    def body(y0_ref, y1_ref, u_ref, d_ref, da_ref, dy_ref, du_ref, dd_ref):
        i = pl.program_id(0)
        u = u_ref[...]
        y = y0_ref[...] + y1_ref[...] + d_ref[0] * u
        th, dinner = _gelu_parts(y)
        dy = da_ref[...] * (0.5 * (1.0 + th) + 0.5 * y * (1.0 - th * th) * dinner)
        dy_ref[...] = dy
        du_ref[...] = dy * d_ref[0]

        @pl.when(i == 0)
        def _():
            dd_ref[...] = jnp.zeros_like(dd_ref)

        dd_ref[0] += jnp.sum(dy * u, axis=0, keepdims=True)

    dy, du, dd = _rows(name, body, t_rows // tile,
                       [_row_in(y0, tile), _row_in(y1, tile), (hm, (tile, s), lambda i: (i + nct, 0)),
                        _vec_in(dskip, lambda i: 0), _row_in(da, tile)],
                       [((t_rows, s), F32, (tile, s), lambda i: (i, 0)), ((t_rows, s), F32, (tile, s), lambda i: (i, 0)),
                        ((1, 1, s), F32, (1, 1, s), lambda i: (0, 0, 0))])
    return dy, du, dd


def _col_pieces(arr, off, width, tile, nct, unit=None):
    pw = math.gcd(off, width if unit is None else unit)
    specs = [(arr, (tile, pw), functools.partial(lambda i, cb: (i + nct, cb), cb=off // pw + p))
             for p in range(width // pw)]
    return specs, pw


def _ret_gate_fwd(o0, o1, hm, g_off, heads, dv, nct, tile, name):
    t_rows, w = o0.shape
    g_specs, pw = _col_pieces(hm, g_off, w, tile, nct)
    ng = len(g_specs)

    def body(o0_ref, o1_ref, *refs):
        g_refs, r_ref = refs[:ng], refs[ng]
        for hd in range(heads):
            cs = slice(hd * dv, (hd + 1) * dv)
            o = o0_ref[:, cs] + o1_ref[:, cs]
            lo = (hd * dv) % pw
            g = g_refs[(hd * dv) // pw][:, lo:lo + dv]
            r_ref[:, cs] = (g * _sigmoid(g) * (o * _rms(o))).astype(r_ref.dtype)

    (ri,) = _rows(name, body, t_rows // tile, [_row_in(o0, tile), _row_in(o1, tile)] + g_specs,
                  [((t_rows, w), BF16, (tile, w), lambda i: (i, 0))])
    return ri


def _ret_gate_bwd(o0, o1, hm, g_off, dri, heads, dv, nct, tile, name):
    t_rows, w = o0.shape
    g_specs, pw = _col_pieces(hm, g_off, w, tile, nct)
    ng = len(g_specs)

    def body(o0_ref, o1_ref, d_ref, *refs):
        g_refs, do_ref, dg_ref = refs[:ng], refs[ng], refs[ng + 1]
        for hd in range(heads):
            cs = slice(hd * dv, (hd + 1) * dv)
            o = o0_ref[:, cs] + o1_ref[:, cs]
            lo = (hd * dv) % pw
            g = g_refs[(hd * dv) // pw][:, lo:lo + dv]
            d = d_ref[:, cs]
            rr = _rms(o)
            on = o * rr
            sg = _sigmoid(g)
            dg_ref[:, cs] = (d * on * (sg * (1.0 + g * (1.0 - sg)))).astype(dg_ref.dtype)
            t = d * (g * sg)
            do_ref[:, cs] = rr * t - on * (rr * jnp.mean(t * on, axis=-1, keepdims=True))

    do, dg = _rows(name, body, t_rows // tile, [_row_in(o0, tile), _row_in(o1, tile), _row_in(dri, tile)] + g_specs,
                   [((t_rows, w), F32, (tile, w), lambda i: (i, 0)), ((t_rows, w), BF16, (tile, w), lambda i: (i, 0))])
    return do, dg


def _merge_fwd(gab, rb, hm, gs_off, nct, tile, name):
    t_rows, d = rb.shape
    specs, pw = _col_pieces(hm, gs_off, 2 * d, tile, nct, unit=d)
    npc = d // pw

    def body(gab_ref, rb_ref, *refs):
        gs_refs, gr_refs, m_ref = refs[:npc], refs[npc:2 * npc], refs[2 * npc]
        for p in range(npc):
            cs = slice(p * pw, (p + 1) * pw)
            ga = gab_ref[:, cs].astype(F32)
            gb = gab_ref[:, d + p * pw:d + (p + 1) * pw].astype(F32)
            m_ref[:, cs] = (_sigmoid(gs_refs[p][...]) * (ga * _sigmoid(gb))
                            + _sigmoid(gr_refs[p][...]) * rb_ref[:, cs].astype(F32)).astype(m_ref.dtype)

    (mg,) = _rows(name, body, t_rows // tile, [_row_in(gab, tile), _row_in(rb, tile)] + specs,
                  [((t_rows, d), BF16, (tile, d), lambda i: (i, 0))])
    return mg


def _merge_bwd(gab, rb, hm, gs_off, dm, nct, tile, name):
    t_rows, d = rb.shape
    specs, pw = _col_pieces(hm, gs_off, 2 * d, tile, nct, unit=d)
    npc = d // pw

    def body(gab_ref, rb_ref, dm_ref, *refs):
        gs_refs, gr_refs = refs[:npc], refs[npc:2 * npc]
        dgab_ref, drb_ref, dgs_ref, dgr_ref = refs[2 * npc:]
        for p in range(npc):
            cs = slice(p * pw, (p + 1) * pw)
            cs2 = slice(d + p * pw, d + (p + 1) * pw)
            ga = gab_ref[:, cs].astype(F32)
            gb = gab_ref[:, cs2].astype(F32)
            dmm = dm_ref[:, cs].astype(F32)
            ss = _sigmoid(gs_refs[p][...])
            sr = _sigmoid(gr_refs[p][...])
            sb = _sigmoid(gb)
            dbr = dmm * ss
            dgab_ref[:, cs] = (dbr * sb).astype(dgab_ref.dtype)
            dgab_ref[:, cs2] = (dbr * ga * sb * (1.0 - sb)).astype(dgab_ref.dtype)
            drb_ref[:, cs] = (dmm * sr).astype(drb_ref.dtype)
            dgs_ref[:, cs] = (dmm * (ga * sb) * ss * (1.0 - ss)).astype(dgs_ref.dtype)
            dgr_ref[:, cs] = (dmm * rb_ref[:, cs].astype(F32) * sr * (1.0 - sr)).astype(dgr_ref.dtype)

    return _rows(name, body, t_rows // tile, [_row_in(gab, tile), _row_in(rb, tile), _row_in(dm, tile)] + specs,
                 [((t_rows, 2 * d), BF16, (tile, 2 * d), lambda i: (i, 0)), ((t_rows, d), BF16, (tile, d), lambda i: (i, 0)),
                  ((t_rows, d), BF16, (tile, d), lambda i: (i, 0)), ((t_rows, d), BF16, (tile, d), lambda i: (i, 0))])


def _assemble_dhm(dus, dq0, dq1, dk0, dk1, dv0, dv1, dg, dgs, dgr, nct, tile, name):
    r, s = dus.shape
    qk = dq0.shape[1]
    vw = dv0.shape[1]
    d = dgs.shape[1]
    mi = s + 2 * qk + 2 * vw + 2 * d
    c_q, c_k, c_v, c_g, c_gs, c_gr = s, s + qk, s + 2 * qk, s + 2 * qk + vw, s + 2 * qk + 2 * vw, s + 2 * qk + 2 * vw + d

    def body(dus_ref, dq0_ref, dq1_ref, dk0_ref, dk1_ref, dv0_ref, dv1_ref, dg_ref, dgs_ref, dgr_ref, o_ref):
        i = pl.program_id(0)
        lat = i >= nct
        o_ref[:, :s] = dus_ref[...].astype(o_ref.dtype)
        o_ref[:, c_q:c_k] = (dq0_ref[...] + dq1_ref[...]).astype(o_ref.dtype)
        o_ref[:, c_k:c_v] = (dk0_ref[...] + dk1_ref[...]).astype(o_ref.dtype)
        o_ref[:, c_v:c_g] = (dv0_ref[...] + dv1_ref[...]).astype(o_ref.dtype)
        o_ref[:, c_g:c_gs] = jnp.where(lat, dg_ref[...], 0.0).astype(o_ref.dtype)
        o_ref[:, c_gs:c_gr] = jnp.where(lat, dgs_ref[...], 0.0).astype(o_ref.dtype)
        o_ref[:, c_gr:] = jnp.where(lat, dgr_ref[...], 0.0).astype(o_ref.dtype)

    (out,) = _rows(name, body, r // tile,
                   [_row_in(dus, tile), _row_in(dq0, tile), _row_in(dq1, tile), _row_in(dk0, tile), _row_in(dk1, tile),
                    _row_in(dv0, tile), _row_in(dv1, tile), _row_in(dg, tile, x_only_offset=nct),
                    _row_in(dgs, tile, x_only_offset=nct), _row_in(dgr, tile, x_only_offset=nct)],
                   [((r, mi), BF16, (tile, mi), lambda i: (i, 0))])
    return out


def _loss_grad(y, target, tile, name):
    t_rows, d = y.shape

    def body(y_ref, t_ref, dy_ref, l_ref):
        i = pl.program_id(0)
        e = y_ref[...] - t_ref[...]
        dy_ref[...] = e * (1.0 / d)

        @pl.when(i == 0)
        def _():
            l_ref[...] = jnp.zeros_like(l_ref)

        l_ref[0] += jnp.sum(e * e, axis=0, keepdims=True)

    return _rows(name, body, t_rows // tile, [_row_in(y, tile), _row_in(target, tile)],
                 [((t_rows, d), F32, (tile, d), lambda i: (i, 0)), ((1, 1, d), F32, (1, 1, d), lambda i: (0, 0, 0))])


def _silu_rows(v, name):
    def body(v_ref, o_ref):
        z = v_ref[...]
        o_ref[...] = z * _sigmoid(z)

    (o,) = _rows(name, body, 1, [_row_in(v, v.shape[0])], [(v.shape, F32, v.shape, lambda i: (0, 0))])
    return o


def _silu_grad_rows(v, dv, name):
    def body(v_ref, d_ref, o_ref):
        z = v_ref[...]
        sg = _sigmoid(z)
        o_ref[...] = d_ref[...] * (sg * (1.0 + z * (1.0 - sg)))

    (o,) = _rows(name, body, 1, [_row_in(v, v.shape[0]), _row_in(dv, v.shape[0])],
                 [(v.shape, F32, v.shape, lambda i: (0, 0))])
    return o


def _sum_leading(g8, name):
    n, r, c = g8.shape
    tile = _tile(r, 256, SUBLANE)

    def body(g_ref, o_ref):
        acc = g_ref[0]
        for j in range(1, n):
            acc = acc + g_ref[j]
        o_ref[...] = acc

    (o,) = _rows(name, body, r // tile, [(g8, (n, tile, c), lambda i: (0, i, 0))],
                 [((r, c), F32, (tile, c), lambda i: (i, 0))])
    return o


def _pair_sum(g, recv, axis, name):
    n, br, bc = recv.shape
    tile = _tile(br, 256, 16)
    nrt = br // tile
    core = lax.axis_index("c").astype(jnp.int32).reshape(1)

    def body(c_ref, g_ref, r_ref, o_ref):
        o_ref[0] = (g_ref[...].astype(F32) + r_ref[0].astype(F32)).astype(o_ref.dtype)

    if axis == 1:
        g_spec = pl.BlockSpec((tile, bc), lambda q, i, c_ref: (i, 2 * q + c_ref[0]))
    else:
        g_spec = pl.BlockSpec((tile, bc), lambda q, i, c_ref: ((2 * q + c_ref[0]) * nrt + i, 0))
    slot = pl.BlockSpec((1, tile, bc), lambda q, i, c_ref: (q, i, 0))
    return pl.pallas_call(
        body, name=name, out_shape=_sds((n, br, bc), recv.dtype),
        grid_spec=pltpu.PrefetchScalarGridSpec(num_scalar_prefetch=1, grid=(n, nrt), in_specs=[g_spec, slot],
                                               out_specs=slot),
        compiler_params=_params(("arbitrary", "arbitrary")))(core, g, recv)


def _adam_math(w, m, v, g):
    c1 = 1.0 / (1.0 - ADAM_B1 ** ADAM_STEP)
    c2 = 1.0 / (1.0 - ADAM_B2 ** ADAM_STEP)
    mm = ADAM_B1 * m + (1.0 - ADAM_B1) * g
    vv = ADAM_B2 * v + (1.0 - ADAM_B2) * (g * g)
    return -ADAM_LR * ((mm * c1) / (jnp.sqrt(vv * c2) + ADAM_EPS) + ADAM_WD * w), mm, vv


def _adamw(w, m, v, gparts, name):
    r, c = w.shape
    n = gparts.shape[0]
    tile = _tile(r, 256, 16)

    def body(w_ref, m_ref, v_ref, g_ref, go_ref, d_ref, mo_ref, vo_ref):
        g = g_ref[0].astype(F32)
        for j in range(1, n):
            g = g + g_ref[j].astype(F32)
        go_ref[...] = g
        d_ref[...], mo_ref[...], vo_ref[...] = _adam_math(w_ref[...], m_ref[...], v_ref[...], g)

    rs = lambda arr: _row_in(arr, tile)
    out = ((r, c), F32, (tile, c), lambda i: (i, 0))
    return _rows(name, body, r // tile, [rs(w), rs(m), rs(v), (gparts, (n, tile, c), lambda i: (0, i, 0))],
                 [out, out, out, out])


def _adamw_scattered(w, m, v, layer, p, recv, name, filled=None):
    nl, r, c = w.shape
    n = recv.shape[0]
    tile = _tile(r, 256, 16)
    chip = (2 * lax.axis_index("x") + lax.axis_index("y")).astype(jnp.int32).reshape(1)
    n_prev = 0 if filled is None else len(filled)

    def body(q_ref, w_ref, m_ref, v_ref, p_ref, g_ref, *rest):
        go_ref, d_ref, mo_ref, vo_ref = rest[n_prev:]
        g = p_ref[0].astype(F32)
        for j in range(n):
            g = g + g_ref[j].astype(F32)
        go_ref[0] = g
        d_ref[0], mo_ref[0], vo_ref[0] = _adam_math(w_ref[0], m_ref[0], v_ref[0], g)

    slab = pl.BlockSpec((1, tile, c), lambda i, q_ref: (layer, i, 0))
    anywhere = pl.BlockSpec(memory_space=pl.ANY)
    out = _sds((nl, r, c), F32)
    prev = [] if filled is None else list(filled)
    return pl.pallas_call(
        body, name=name, out_shape=[out, out, out, out],
        grid_spec=pltpu.PrefetchScalarGridSpec(
            num_scalar_prefetch=1, grid=(r // tile,),
            in_specs=[slab, slab, slab, pl.BlockSpec((1, tile, c), lambda i, q_ref: (q_ref[0], i, 0)),
                      pl.BlockSpec((n, tile, c), lambda i, q_ref: (0, i, 0))] + [anywhere] * n_prev,
            out_specs=[slab, slab, slab, slab]),
        input_output_aliases={6 + j: j for j in range(n_prev)},
        compiler_params=_params(("arbitrary",)))(chip, w, m, v, p, recv, *prev)


def _cmul(ar, ai, br, bi):
    return ar * br - ai * bi, ar * bi + ai * br


def _cpow(ar, ai, n):
    pr, pi = jnp.ones_like(ar), jnp.zeros_like(ar)
    br, bi = ar, ai
    while n:
        if n & 1:
            pr, pi = _cmul(pr, pi, br, bi)
        n >>= 1
        if n:
            br, bi = _cmul(br, bi, br, bi)
    return pr, pi


def _s5_scan_into(x_ref, ar1, ai1, ns, fin_ref, hin_ref, reverse, paired=None):
    st = ar1.shape[1]
    ar = jnp.broadcast_to(ar1, (N_SEG, st))
    ai = jnp.broadcast_to(ai1, (N_SEG, st))
    zero = jnp.zeros((N_SEG, st), F32)

    def slab(k):
        if isinstance(k, int):
            return pl.ds(k * N_SEG, N_SEG)
        return pl.ds(pl.multiple_of(k * N_SEG, N_SEG), N_SEG)

    def pass1(j, carry):
        hr, hi = carry
        k = ns - 1 - j if reverse else j
        nr, ni = _cmul(ar, ai, hr, hi)
        return nr + x_ref[slab(k), :st], ni + x_ref[slab(k), st:]

    fr, fi = lax.fori_loop(0, ns, pass1, (zero, zero))
    fin_ref[:, :st] = fr
    fin_ref[:, st:] = fi
    pr, pi = _cpow(ar1, ai1, ns)
    order = list(range(N_SEG - 1, -1, -1)) if reverse else list(range(N_SEG))
    hin_ref[order[0]:order[0] + 1, :] = jnp.zeros((1, 2 * st), F32)
    for a_, b_ in zip(order[:-1], order[1:]):
        cr, ci = _cmul(pr, pi, hin_ref[a_:a_ + 1, :st], hin_ref[a_:a_ + 1, st:])
        hin_ref[b_:b_ + 1, :st] = cr + fin_ref[a_:a_ + 1, :st]
        hin_ref[b_:b_ + 1, st:] = ci + fin_ref[a_:a_ + 1, st:]

    def step2(k, hr, hi):
        nr, ni = _cmul(ar, ai, hr, hi)
        nr = nr + x_ref[slab(k), :st]
        ni = ni + x_ref[slab(k), st:]
        x_ref[slab(k), :st] = nr
        x_ref[slab(k), st:] = ni
        return nr, ni

    if paired is None:
        def pass2(j, carry):
            return step2(ns - 1 - j if reverse else j, *carry)

        lax.fori_loop(0, ns, pass2, (hin_ref[:, :st], hin_ref[:, st:]))
        return None
    p_ref, p_edge_ref, shift = paired

    def pass2_paired(j, carry):
        hr, hi, acr, aci = carry
        k = ns - 1 - j if reverse else j
        nr, ni = step2(k, hr, hi)
        p_r, p_i = p_ref[slab(k + shift), :st], p_ref[slab(k + shift), st:]
        return nr, ni, acr + nr * p_r + ni * p_i, aci + ni * p_r - nr * p_i

    hr, hi, acr, aci = lax.fori_loop(0, ns - 1, pass2_paired, (hin_ref[:, :st], hin_ref[:, st:], zero, zero))
    nr, ni = step2(0 if reverse else ns - 1, hr, hi)
    p_r, p_i = p_edge_ref[:, :st], p_edge_ref[:, st:]
    return acr + nr * p_r + ni * p_i, aci + ni * p_r - nr * p_i


def _s5_specs(r, ch, st):
    u_spec = pl.BlockSpec((r, ch), lambda j: (0, j // 2))
    w_spec = pl.BlockSpec((1, ch, 2 * st), lambda j: (j, 0, 0))
    c_spec = pl.BlockSpec((1, 2 * st, ch), lambda j: (j, 0, 0))
    a_spec = pl.BlockSpec((1, 2, st), lambda j: (j, 0, 0))
    return u_spec, w_spec, c_spec, a_spec


def _s5_fwd(up, w, c, a, rev, name):
    r, s = up.shape
    nh, ch, st2 = w.shape
    st = st2 // 2
    ns = r // N_SEG
    nb = r // N_DEV
    u_spec, w_spec, c_spec, a_spec = _s5_specs(r, ch, st)

    def body(u_ref, w_ref, c_ref, a_ref, y_ref, x, fin, hin):
        j = pl.program_id(0)
        w_b = w_ref[0].astype(MXU_DTYPE)
        c_b = c_ref[0].astype(MXU_DTYPE)
        for rb in range(N_DEV):
            rows = slice(rb * nb, (rb + 1) * nb)
            x[rows, :] = jnp.dot(u_ref[rows, :].astype(MXU_DTYPE), w_b, preferred_element_type=F32)
        _s5_scan_into(x, a_ref[0, 0:1, :], a_ref[0, 1:2, :], ns, fin, hin, rev)
        for rb in range(N_DEV):
            rows = slice(rb * nb, (rb + 1) * nb)
            yb = jnp.dot(x[rows, :].astype(MXU_DTYPE), c_b, preferred_element_type=F32)

            @pl.when(j % 2 == 0)
            def _():
                y_ref[rows, :] = yb

            @pl.when(j % 2 == 1)
            def _():
                y_ref[rows, :] += yb

    small = pltpu.VMEM((N_SEG, st2), F32)
    return pl.pallas_call(
        body, name=name, grid=(nh,), in_specs=[u_spec, w_spec, c_spec, a_spec],
        out_specs=pl.BlockSpec((r, ch), lambda j: (0, j // 2)), out_shape=_sds((r, s), F32),
        scratch_shapes=[pltpu.VMEM((r, st2), F32), small, small],
        compiler_params=_params(("arbitrary",)))(up, w, c, a)


def _s5_bwd(up, dyp, w, c, a, rev, name):
    r, s = up.shape
    nh, ch, st2 = w.shape
    st = st2 // 2
    ns = r // N_SEG
    nb = r // N_DEV
    u_spec, w_spec, c_spec, a_spec = _s5_specs(r, ch, st)
    nt = (((1,), (1,)), ((), ()))
    tn = (((0,), (0,)), ((), ()))

    def body(u_ref, dy_ref, w_ref, c_ref, a_ref, du_ref, dw_ref, dc_ref, da_ref, h, g, fin, sin_, ein):
        j = pl.program_id(0)
        w_b = w_ref[0].astype(MXU_DTYPE)
        c_b = c_ref[0].astype(MXU_DTYPE)
        for rb in range(N_DEV):
            rows = slice(rb * nb, (rb + 1) * nb)
            h[rows, :] = jnp.dot(u_ref[rows, :].astype(MXU_DTYPE), w_b, preferred_element_type=F32)
        ar1, ai1 = a_ref[0, 0:1, :], a_ref[0, 1:2, :]
        _s5_scan_into(h, ar1, ai1, ns, fin, sin_, rev)
        dc = jnp.zeros((st2, ch), F32)
        for rb in range(N_DEV):
            rows = slice(rb * nb, (rb + 1) * nb)
            dyb = dy_ref[rows, :].astype(MXU_DTYPE)
            g[rows, :] = lax.dot_general(dyb, c_b, nt, preferred_element_type=F32)
            dc += lax.dot_general(h[rows, :].astype(MXU_DTYPE), dyb, tn, preferred_element_type=F32)
        dc_ref[0] = dc
        acr, aci = _s5_scan_into(g, ar1, -ai1, ns, fin, ein, not rev, paired=(h, sin_, 1 if rev else -1))
        da_ref[0, 0:1, :] = jnp.sum(acr, axis=0, keepdims=True)
        da_ref[0, 1:2, :] = jnp.sum(aci, axis=0, keepdims=True)
        dw = jnp.zeros((ch, st2), F32)
        for rb in range(N_DEV):
            rows = slice(rb * nb, (rb + 1) * nb)
            gb = g[rows, :].astype(MXU_DTYPE)
            dub = lax.dot_general(gb, w_b, nt, preferred_element_type=F32)
            dw += lax.dot_general(u_ref[rows, :].astype(MXU_DTYPE), gb, tn, preferred_element_type=F32)

            @pl.when(j % 2 == 0)
            def _():
                du_ref[rows, :] = dub

            @pl.when(j % 2 == 1)
            def _():
                du_ref[rows, :] += dub

        dw_ref[0] = dw

    small = pltpu.VMEM((N_SEG, st2), F32)
    big = pltpu.VMEM((r, st2), F32)
    return pl.pallas_call(
        body, name=name, grid=(nh,), in_specs=[u_spec, u_spec, w_spec, c_spec, a_spec],
        out_specs=[pl.BlockSpec((r, ch), lambda j: (0, j // 2)), w_spec, c_spec, a_spec],
        out_shape=[_sds((r, s), F32), _sds(w.shape, F32), _sds(c.shape, F32), _sds(a.shape, F32)],
        scratch_shapes=[big, big, small, small, small],
        compiler_params=_params(("arbitrary",)))(up, dyp, w, c, a)


def _rope(t, cos, sin):
    quarter = t.shape[1] // 4
    lane = lax.broadcasted_iota(jnp.int32, t.shape, 1)
    first = (lane // quarter) % 2 == 0
    partner = jnp.where(first, pltpu.roll(t, t.shape[1] - quarter, 1), pltpu.roll(t, quarter, 1))
    return t * cos + partner * sin


def _rope_t(d, cos, sin):
    quarter = d.shape[1] // 4
    ds_ = d * sin
    lane = lax.broadcasted_iota(jnp.int32, d.shape, 1)
    first = (lane // quarter) % 2 == 0
    partner = jnp.where(first, pltpu.roll(ds_, d.shape[1] - quarter, 1), pltpu.roll(ds_, quarter, 1))
    return d * cos + partner


def _chunk_of_step(s, nch, ncc, rev):
    if not rev:
        return s
    return jnp.where(s < ncc, ncc - 1 - s, nch + ncc - 1 - s)


def _heads_per_step(heads, dk, dv, q_off):
    v_off = q_off + 2 * heads * dk
    for hpg in range(heads, 0, -1):
        if heads % hpg == 0 and q_off % (hpg * dk) == 0:
            piece = math.gcd(v_off, hpg * dv)
            if piece % dv == 0:
                return hpg, piece
    return 1, dv


def _v_specs(hpg, dv, piece, v_off, ch, chunk_of):
    n_pieces = hpg * dv // piece
    return [pl.BlockSpec((ch, piece), functools.partial(
        lambda h, s, p: (chunk_of(s), v_off // piece + h * n_pieces + p), p=p)) for p in range(n_pieces)]


def _v_of_head(v_refs, hl, dv, piece):
    lo = (hl * dv) % piece
    return v_refs[(hl * dv) // piece][:, lo:lo + dv]


def _ret_fwd(hm, cos, sin, decay, wend, win, gch, heads, dk, dv, q_off, ncc, rev, name):
    r = hm.shape[0]
    ch = RET_CHUNK
    nch = r // ch
    t_rows = r - ncc * ch
    hpg, piece = _heads_per_step(heads, dk, dv, q_off)
    qb, kb = q_off // (hpg * dk), (q_off + heads * dk) // (hpg * dk)
    q_scale = dk ** -0.5
    nt = (((1,), (1,)), ((), ()))
    tn = (((0,), (0,)), ((), ()))
    cof = lambda s: _chunk_of_step(s, nch, ncc, rev)
    v_specs = _v_specs(hpg, dv, piece, q_off + 2 * heads * dk, ch, cof)
    nv = len(v_specs)

    def body(q_ref, k_ref, *refs):
        v_refs = refs[:nv]
        cos_ref, sin_ref, dec_ref, we_ref, wi_ref, g_ref, o_ref, sin_out, st = refs[nv:]
        s = pl.program_id(1)

        @pl.when(s == 0)
        def _():
            st[...] = jnp.zeros_like(st)

        cos_, sin_ = cos_ref[...], sin_ref[...]
        for hl in range(hpg):
            ks, vs = slice(hl * dk, (hl + 1) * dk), slice(hl * dv, (hl + 1) * dv)
            q = _rope(q_ref[:, ks], cos_, sin_) * q_scale
            k = _rope(k_ref[:, ks], cos_, sin_)
            v = _v_of_head(v_refs, hl, dv, piece).astype(MXU_DTYPE)
            s_cur = st[hl]
            sin_out[hl, 0] = s_cur
            kw = (k * we_ref[hl]).astype(MXU_DTYPE)
            qw = (q * wi_ref[hl]).astype(MXU_DTYPE)
            scores = lax.dot_general(q.astype(MXU_DTYPE), k.astype(MXU_DTYPE), nt,
                                     preferred_element_type=F32) * dec_ref[hl]
            o_ref[:, vs] = (jnp.dot(scores.astype(MXU_DTYPE), v, preferred_element_type=F32)
                            + jnp.dot(qw, s_cur.astype(MXU_DTYPE), preferred_element_type=F32))
            st[hl] = g_ref[hl] * s_cur + lax.dot_general(kw, v, tn, preferred_element_type=F32)

    tab = lambda w: pl.BlockSpec((hpg, ch, w), lambda h, s: (h, 0, 0))
    return pl.pallas_call(
        body, name=name, grid=(heads // hpg, nch),
        in_specs=[pl.BlockSpec((ch, hpg * dk), lambda h, s: (cof(s), qb + h)),
                  pl.BlockSpec((ch, hpg * dk), lambda h, s: (cof(s), kb + h))] + v_specs +
                 [pl.BlockSpec((ch, dk), lambda h, s: (cof(s), 0)),
                  pl.BlockSpec((ch, dk), lambda h, s: (cof(s), 0)),
                  tab(ch), tab(dk), tab(dk), tab(dv)],
        out_specs=[pl.BlockSpec((ch, hpg * dv), lambda h, s: (jnp.maximum(cof(s) - ncc, 0) if not rev
                                                               else jnp.where(s < ncc, nch - ncc - 1, cof(s) - ncc), h)),
                   pl.BlockSpec((hpg, 1, dk, dv), lambda h, s: (h, s, 0, 0))],
        out_shape=[_sds((t_rows, heads * dv), F32), _sds((heads, nch, dk, dv), F32)],
        scratch_shapes=[pltpu.VMEM((hpg, dk, dv), F32)],
        compiler_params=_params(("parallel", "arbitrary")))(hm, hm, *([hm] * nv), cos, sin, decay, wend, win, gch)


def _ret_bwd(hm, cos, sin, decay, wend, win, gch, s_in, do, heads, dk, dv, q_off, ncc, rev, name):
    r = hm.shape[0]
    ch = RET_CHUNK
    nch = r // ch
    hpg, piece = _heads_per_step(heads, dk, dv, q_off)
    qb, kb = q_off // (hpg * dk), (q_off + heads * dk) // (hpg * dk)
    q_scale = dk ** -0.5
    nt = (((1,), (1,)), ((), ()))
    tn = (((0,), (0,)), ((), ()))
    cof = lambda rr: _chunk_of_step(nch - 1 - rr, nch, ncc, rev)
    v_specs = _v_specs(hpg, dv, piece, q_off + 2 * heads * dk, ch, cof)
    nv = len(v_specs)

    def body(q_ref, k_ref, *refs):
        v_refs = refs[:nv]
        (cos_ref, sin_ref, dec_ref, we_ref, wi_ref, g_ref, sin_ref2, do_ref,
         dq_ref, dk_ref, dv_ref, ddec_ref, dwe_ref, dwi_ref, dg_ref, dst) = refs[nv:]
        rr = pl.program_id(1)
        n = cof(rr)

        @pl.when(rr == 0)
        def _():
            dst[...] = jnp.zeros_like(dst)
            ddec_ref[...] = jnp.zeros_like(ddec_ref)
            dwe_ref[...] = jnp.zeros_like(dwe_ref)
            dwi_ref[...] = jnp.zeros_like(dwi_ref)
            dg_ref[...] = jnp.zeros_like(dg_ref)

        cos_, sin_ = cos_ref[...], sin_ref[...]
        for hl in range(hpg):
            ks, vs = slice(hl * dk, (hl + 1) * dk), slice(hl * dv, (hl + 1) * dv)
            q = _rope(q_ref[:, ks], cos_, sin_) * q_scale
            k = _rope(k_ref[:, ks], cos_, sin_)
            v = _v_of_head(v_refs, hl, dv, piece).astype(MXU_DTYPE)
            qb_, kb_ = q.astype(MXU_DTYPE), k.astype(MXU_DTYPE)
            kw = (k * we_ref[hl]).astype(MXU_DTYPE)
            qw = (q * wi_ref[hl]).astype(MXU_DTYPE)
            sraw = lax.dot_general(qb_, kb_, nt, preferred_element_type=F32)
            scores = (sraw * dec_ref[hl]).astype(MXU_DTYPE)
            d_o = jnp.where(n >= ncc, do_ref[:, vs], 0.0).astype(MXU_DTYPE)
            s_n = sin_ref2[hl, 0]
            s_nb = s_n.astype(MXU_DTYPE)
            ds1 = dst[hl]
            ds1b = ds1.astype(MXU_DTYPE)
            dsc = lax.dot_general(d_o, v, nt, preferred_element_type=F32)
            dsr = (dsc * dec_ref[hl]).astype(MXU_DTYPE)
            ddec_ref[hl] += dsc * sraw
            t1 = lax.dot_general(d_o, s_nb, nt, preferred_element_type=F32)
            dq_r = jnp.dot(dsr, kb_, preferred_element_type=F32) + t1 * wi_ref[hl]
            dwi_ref[hl] += t1 * q
            t2 = lax.dot_general(v, ds1b, nt, preferred_element_type=F32)
            dk_r = lax.dot_general(dsr, qb_, tn, preferred_element_type=F32) + t2 * we_ref[hl]
            dwe_ref[hl] += t2 * k
            dv_ref[:, vs] = (lax.dot_general(scores, d_o, tn, preferred_element_type=F32)
                             + jnp.dot(kw, ds1b, preferred_element_type=F32))
            dg_ref[hl] += ds1 * s_n
            dst[hl] = g_ref[hl] * ds1 + lax.dot_general(qw, d_o, tn, preferred_element_type=F32)
            dq_ref[:, ks] = _rope_t(dq_r, cos_, sin_) * q_scale
            dk_ref[:, ks] = _rope_t(dk_r, cos_, sin_)

    tab = lambda w: pl.BlockSpec((hpg, ch, w), lambda h, rr: (h, 0, 0))
    return pl.pallas_call(
        body, name=name, grid=(heads // hpg, nch),
        in_specs=[pl.BlockSpec((ch, hpg * dk), lambda h, rr: (cof(rr), qb + h)),
                  pl.BlockSpec((ch, hpg * dk), lambda h, rr: (cof(rr), kb + h))] + v_specs +
                 [pl.BlockSpec((ch, dk), lambda h, rr: (cof(rr), 0)),
                  pl.BlockSpec((ch, dk), lambda h, rr: (cof(rr), 0)),
                  tab(ch), tab(dk), tab(dk), tab(dv),
                  pl.BlockSpec((hpg, 1, dk, dv), lambda h, rr: (h, nch - 1 - rr, 0, 0)),
                  pl.BlockSpec((ch, hpg * dv), lambda h, rr: (jnp.maximum(cof(rr) - ncc, 0), h))],
        out_specs=[pl.BlockSpec((ch, hpg * dk), lambda h, rr: (cof(rr), h)),
                   pl.BlockSpec((ch, hpg * dk), lambda h, rr: (cof(rr), h)),
                   pl.BlockSpec((ch, hpg * dv), lambda h, rr: (cof(rr), h)),
                   tab(ch), tab(dk), tab(dk), tab(dv)],
        out_shape=[_sds((r, heads * dk), F32), _sds((r, heads * dk), F32), _sds((r, heads * dv), F32),
                   _sds(decay.shape, F32), _sds(wend.shape, F32), _sds(win.shape, F32), _sds(gch.shape, F32)],
        scratch_shapes=[pltpu.VMEM((hpg, dk, dv), F32)],
        compiler_params=_params(("parallel", "arbitrary")))(hm, hm, *([hm] * nv), cos, sin, decay, wend, win, gch, s_in, do)


_HBM = pl.BlockSpec(memory_space=pltpu.HBM)
_MESH = pl.DeviceIdType.MESH
ALL_GATHER_COLLECTIVE_ID = 1
SIBLING_COLLECTIVE_ID = 2
CHIPS_COLLECTIVE_ID = 3


def _axis_slice(ref, axis, start, size):
    idx = [slice(None)] * len(ref.shape)
    idx[axis] = pl.ds(start, size)
    return ref.at[tuple(idx)]


def _sibling_and_chip_peers():
    x, y, c = lax.axis_index("x"), lax.axis_index("y"), lax.axis_index("c")
    return [(x, y, 1 - c), (1 - x, y, c), (x, 1 - y, c), (1 - x, 1 - y, c)]


def _launch_exchange(body, name, operand, out_shape, sems, peers_fn, collective_id, on_sequencer):
    if not on_sequencer:
        return pl.pallas_call(body, name=name, out_shape=out_shape, in_specs=[_HBM], out_specs=_HBM,
                              scratch_shapes=sems)(operand)

    def sequencer_body(in_ref, out_ref, *sem_refs):
        peers = peers_fn()
        barrier = pltpu.get_barrier_semaphore()
        for peer in peers:
            pl.semaphore_signal(barrier, inc=1, device_id=peer, device_id_type=_MESH)
        pl.semaphore_wait(barrier, len(peers))
        body(in_ref, out_ref, *sem_refs)

    return pl.kernel(sequencer_body, out_type=out_shape, name=name,
                     mesh=plsc.ScalarSubcoreMesh(axis_name="sequencer", num_cores=1), scratch_types=sems,
                     compiler_params=pltpu.CompilerParams(collective_id=collective_id))(operand)


def _all_gather(shard, axis, name, on_sequencer=False):
    m = shard.shape[axis]
    out_shape = list(shard.shape)
    out_shape[axis] = N_DEV * m

    def body(x_ref, out_ref, send_sems, recv_sems, local_sem):
        x, y, c = lax.axis_index("x"), lax.axis_index("y"), lax.axis_index("c")
        me, sibling = (x, y, c), (x, y, 1 - c)
        chips = [(1 - x, y), (x, 1 - y), (1 - x, 1 - y)]

        def block(px, py, pc):
            return _axis_slice(out_ref, axis, (4 * px + 2 * py + pc) * m, m)

        def copy(k, blk, to, src=None):
            return pltpu.make_async_remote_copy(
                src_ref=block(*blk) if src is None else src, dst_ref=block(*blk), send_sem=send_sems.at[k],
                recv_sem=recv_sems.at[k], device_id=to, device_id_type=_MESH)

        mine = pltpu.make_async_copy(x_ref, block(*me), local_sem)
        mine.start()
        first = [copy(0, me, sibling, src=x_ref)]
        first += [copy(1 + j, me, (*chip, c), src=x_ref) for j, chip in enumerate(chips)]
        for cp in first:
            cp.start()
        passed = [copy(4 + j, (*chip, c), sibling) for j, chip in enumerate(chips)]
        for j, chip in enumerate(chips):
            copy(1 + j, (*chip, c), me).wait_recv()
            passed[j].start()
        copy(0, sibling, me).wait_recv()
        for j, chip in enumerate(chips):
            copy(4 + j, (*chip, 1 - c), me).wait_recv()
        for cp in first + passed:
            cp.wait_send()
        mine.wait()

    return _launch_exchange(
        body, name, shard, _sds(out_shape, shard.dtype),
        [pltpu.SemaphoreType.DMA((7,)), pltpu.SemaphoreType.DMA((7,)), pltpu.SemaphoreType.DMA(())],
        _sibling_and_chip_peers, ALL_GATHER_COLLECTIVE_ID, on_sequencer)


def _rs_sibling(g, axis, name, on_sequencer=False):
    m = g.shape[axis] // N_DEV
    blk_shape = list(g.shape)
    blk_shape[axis] = m
    n_chips = N_DEV // 2

    def body(g_ref, recv_ref, send_sems, recv_sems):
        x, y, c = lax.axis_index("x"), lax.axis_index("y"), lax.axis_index("c")
        sibling = (x, y, 1 - c)
        send = [pltpu.make_async_remote_copy(
            src_ref=_axis_slice(g_ref, axis, (2 * q + 1 - c) * m, m), dst_ref=recv_ref.at[q],
            send_sem=send_sems.at[q], recv_sem=recv_sems.at[q], device_id=sibling, device_id_type=_MESH)
            for q in range(n_chips)]
        for cp in send:
            cp.start()
        for cp in send:
            cp.wait_recv()
        for cp in send:
            cp.wait_send()

    return _launch_exchange(
        body, name, g, _sds([n_chips] + blk_shape, g.dtype),
        [pltpu.SemaphoreType.DMA((n_chips,)), pltpu.SemaphoreType.DMA((n_chips,))],
        lambda: _sibling_and_chip_peers()[:1], SIBLING_COLLECTIVE_ID, on_sequencer)


def _rs_chips(p, name, on_sequencer=False):
    n_peers = p.shape[0] - 1

    def body(p_ref, out_ref, send_sems, recv_sems):
        x, y, c = lax.axis_index("x"), lax.axis_index("y"), lax.axis_index("c")
        chips = [(1 - x, y), (x, 1 - y), (1 - x, 1 - y)]
        send = [pltpu.make_async_remote_copy(
            src_ref=p_ref.at[2 * cx + cy], dst_ref=out_ref.at[j], send_sem=send_sems.at[j],
            recv_sem=recv_sems.at[j], device_id=(cx, cy, c), device_id_type=_MESH)
            for j, (cx, cy) in enumerate(chips)]
        for cp in send:
            cp.start()
        for cp in send:
            cp.wait_recv()
        for cp in send:
            cp.wait_send()

    return _launch_exchange(
        body, name, p, _sds((n_peers,) + p.shape[1:], p.dtype),
        [pltpu.SemaphoreType.DMA((n_peers,)), pltpu.SemaphoreType.DMA((n_peers,))],
        lambda: _sibling_and_chip_peers()[1:], CHIPS_COLLECTIVE_ID, on_sequencer)


def _reduce_scatter(g, axis, name):
    sib = _rs_sibling(g, axis, name + "_d2d", on_sequencer=True)
    p = _pair_sum(g, sib, axis, name + "_pair")
    return p, _rs_chips(p, name + "_ici", on_sequencer=True)


def _s5_tables(lam_re, lam_im, log_step, b_re, b_im, c_re, c_im):
    nd, g, p, cg = b_re.shape
    step = jnp.exp(log_step)[..., None]
    mag = jnp.exp(lam_re * step)
    a_re, a_im = mag * jnp.cos(lam_im * step), mag * jnp.sin(lam_im * step)
    den = lam_re * lam_re + lam_im * lam_im
    num_re, num_im = a_re - 1.0, a_im
    k_re = (num_re * lam_re + num_im * lam_im) / den
    k_im = (num_im * lam_re - num_re * lam_im) / den
    bb_re = k_re[..., None] * b_re - k_im[..., None] * b_im
    bb_im = k_re[..., None] * b_im + k_im[..., None] * b_re
    gt = g // SSM_TILE_GROUPS
    hg = SSM_HALF_GROUPS
    eye = jnp.eye(SSM_TILE_GROUPS, dtype=F32).reshape(SSM_TILE_GROUPS, 2, hg)

    def pack_b(bb):
        w = jnp.einsum("djhqpc,ghq->djhgcqp", bb.reshape(nd, gt, 2, hg, p, cg), eye)
        return w.reshape(nd, gt * 2, SSM_TILE_GROUPS * cg, hg * p)

    def pack_c(cc):
        w = jnp.einsum("djhqcp,ghq->djhqpgc", cc.reshape(nd, gt, 2, hg, cg, p), eye)
        return w.reshape(nd, gt * 2, hg * p, SSM_TILE_GROUPS * cg)

    a = jnp.stack([a_re.reshape(nd, gt * 2, hg * p), a_im.reshape(nd, gt * 2, hg * p)], axis=2)
    w = jnp.concatenate([pack_b(bb_re), pack_b(bb_im)], axis=-1)
    c = jnp.concatenate([pack_c(c_re), -pack_c(c_im)], axis=-2)
    return w, c, a


def _ret_tables(decay_logit, dk, dv):
    ch = RET_CHUNK
    nd, h = decay_logit.shape
    lg = jax.nn.log_sigmoid(decay_logit)[:, :, None]
    pos = jnp.arange(ch, dtype=F32)
    fwd_diff = pos[:, None] - pos[None, :]
    diff = jnp.stack([fwd_diff, -fwd_diff])[:, None]
    mask = jnp.stack([fwd_diff >= 0, -fwd_diff > 0])[:, None]
    end_pos = jnp.stack([ch - 1.0 - pos, pos])[:, None]
    in_pos = jnp.stack([pos + 1.0, ch - pos])[:, None]
    w_end = jnp.exp(lg * end_pos)
    w_in = jnp.exp(lg * in_pos)
    decay = jnp.where(mask, jnp.exp(lg[..., None] * jnp.where(mask, diff, 0.0)), 0.0)
    g_chunk = jnp.exp(lg[..., 0] * ch)
    return (decay, jnp.broadcast_to(w_end[..., None], (nd, h, ch, dk)), jnp.broadcast_to(w_in[..., None], (nd, h, ch, dk)),
            jnp.broadcast_to(g_chunk[..., None, None], (nd, h, dk, dv)))


def _rope_tables(t_rows, ncc, dk):
    quarter = dk // 4
    idx = np.arange(t_rows)
    row, col = idx // GRID_W, idx % GRID_W
    inv = ROPE_BASE ** (-np.arange(quarter, dtype=np.float32) / quarter)
    ang_r = row.astype(np.float32)[:, None] * inv
    ang_c = col.astype(np.float32)[:, None] * inv
    ang_r, ang_c = jnp.asarray(ang_r, F32), jnp.asarray(ang_c, F32)
    cos = jnp.concatenate([jnp.cos(ang_r), jnp.cos(ang_r), jnp.cos(ang_c), jnp.cos(ang_c)], axis=1)
    sin = jnp.concatenate([-jnp.sin(ang_r), jnp.sin(ang_r), -jnp.sin(ang_c), jnp.sin(ang_c)], axis=1)
    n_ctx = ncc * RET_CHUNK
    cos = jnp.concatenate([jnp.ones((n_ctx, dk), F32), cos], axis=0)
    sin = jnp.concatenate([jnp.zeros((n_ctx, dk), F32), sin], axis=0)
    return cos, sin


def _to_scan_layout(ctx_rows, lat_rows, rev):
    u = jnp.concatenate([lat_rows, ctx_rows] if rev else [ctx_rows, lat_rows], axis=0)
    r, w = u.shape
    return u.reshape(N_SEG, r // N_SEG, w).transpose(1, 0, 2).reshape(r, w)


def _from_scan_layout(yp, n_ctx, rev):
    r, w = yp.shape
    y = yp.reshape(r // N_SEG, N_SEG, w).transpose(1, 0, 2).reshape(r, w)
    return (y[r - n_ctx:], y[:r - n_ctx]) if rev else (y[:n_ctx], y[n_ctx:])


def _pack(parts, width):
    rows = []
    for p in parts:
        flat = p.reshape(-1).astype(F32)
        n = flat.shape[0]
        rows.append(jnp.pad(flat, (0, -n % (SUBLANE * width))).reshape(-1, width))
    return jnp.concatenate(rows, axis=0)


def _packed_rows(n, width):
    return -(-n // (SUBLANE * width)) * SUBLANE


def _unpack(flat2d, shapes):
    width = flat2d.shape[1]
    out, row = [], 0
    for shp in shapes:
        n = int(np.prod(shp))
        nr = _packed_rows(n, width)
        out.append(flat2d[row:row + nr].reshape(-1)[:n].reshape(shp))
        row += nr
    return out


def kernel(x, c, ctx, c_ctx, ada_w, ada_b, norm_g, ffn_w_in, ffn_w_out, mix_w_in, ssm_lam_re, ssm_lam_im, ssm_log_step, ssm_b_re, ssm_b_im, ssm_c_re, ssm_c_im, ssm_d, ssm_glu_w, ret_decay_logit, ret_w_proj, mix_w_out, loss_target, m_c_ctx, m_ada_w, m_ada_b, m_norm_g, m_ffn_w_in, m_ffn_w_out, m_mix_w_in, m_ssm_lam_re, m_ssm_lam_im, m_ssm_log_step, m_ssm_b_re, m_ssm_b_im, m_ssm_c_re, m_ssm_c_im, m_ssm_d, m_ssm_glu_w, m_ret_decay_logit, m_ret_w_proj, m_mix_w_out, v_c_ctx, v_ada_w, v_ada_b, v_norm_g, v_ffn_w_in, v_ffn_w_out, v_mix_w_in, v_ssm_lam_re, v_ssm_lam_im, v_ssm_log_step, v_ssm_b_re, v_ssm_b_im, v_ssm_c_re, v_ssm_c_im, v_ssm_d, v_ssm_glu_w, v_ret_decay_logit, v_ret_w_proj, v_mix_w_out):
    t_rows, d = x.shape[1], x.shape[2]
    n_ctx = ctx.shape[1]
    r = n_ctx + t_rows
    ssm_w = ssm_d.shape[1]
    heads = ret_decay_logit.shape[2]
    mi = mix_w_in.shape[2] * N_DEV
    dk = (mi - ssm_w - 2 * d) // (6 * heads)
    dv = 2 * dk
    qk_w, v_w = heads * dk, heads * dv
    q_off = ssm_w
    ncc = n_ctx // RET_CHUNK
    tile = n_ctx
    nct = 1
    wide_tile = _tile(n_ctx, 128, 16)
    assert r % (N_SEG * SUBLANE) == 0 and n_ctx % RET_CHUNK == 0 and t_rows % tile == 0
    me = 4 * lax.axis_index("x") + 2 * lax.axis_index("y") + lax.axis_index("c")
    g_off = ssm_w + 2 * qk_w + v_w
    gs_off = g_off + v_w

    ng_cols = norm_g.shape[2]
    small0 = _pack([c[0], norm_g[0]], d)
    small0_all = _all_gather(small0, 0, "ag_cond")

    bf = lambda w: w.astype(BF16)
    small0_all, sh_in1 = lax.optimization_barrier((small0_all, bf(ffn_w_in[0, 0])))
    small0_all = small0_all.reshape(N_DEV, -1)
    w_in1 = _all_gather(sh_in1, 1, "ag_ffn1_in", on_sequencer=True)
    w_glu = _all_gather(bf(ssm_glu_w[0]), 1, "ag_glu", on_sequencer=True)
    w_rp = _all_gather(bf(ret_w_proj[0]), 0, "ag_ret_proj", on_sequencer=True)
    w_mo = _all_gather(bf(mix_w_out[0]), 0, "ag_mix_out", on_sequencer=True)
    w_in2 = _all_gather(bf(ffn_w_in[0, 1]), 1, "ag_ffn2_in", on_sequencer=True)
    w_out2 = _all_gather(bf(ffn_w_out[0, 1]), 0, "ag_ffn2_out", on_sequencer=True)

    ng_at = _packed_rows(d, d) * d
    c_all = small0_all[:, :d]
    g_full = small0_all[:, ng_at:ng_at + 6 * ng_cols].reshape(N_DEV, 6, ng_cols).transpose(1, 0, 2).reshape(6, d)
    g6 = g_full.reshape(6, 1, d)
    cc = jnp.concatenate([c_all, c_ctx[None, :], jnp.zeros((2 * SUBLANE - N_DEV - 1, d), F32)], axis=0)
    sc = _silu_rows(cc, "ada_silu")
    na = ada_w.shape[2]
    a_loc = _mm(sc, ada_w[0], "nn", F32, "ada_fwd", tm=16, tn=na, tk=512)
    a_all = _all_gather(a_loc, 0, "ag_ada")
    a_all, sh_out1, sh_mix = lax.optimization_barrier((a_all, bf(ffn_w_out[0, 0]), bf(mix_w_in[0])))
    a_all = a_all.reshape(N_DEV, 2 * SUBLANE, na)
    w_out1 = _all_gather(sh_out1, 0, "ag_ffn1_out", on_sequencer=True)
    w_mix = _all_gather(sh_mix, 1, "ag_mix_in", on_sequencer=True)
    ada_x = lax.dynamic_index_in_dim(a_all, me, axis=1, keepdims=False).reshape(9 * d) + ada_b[0]
    ada_c = a_all[:, N_DEV, :].reshape(9 * d) + ada_b[0]
    mods = jnp.stack([ada_c.reshape(9, d), ada_x.reshape(9, d)]).reshape(18, 1, d)

    xin = jnp.concatenate([ctx[0], x[0]], axis=0)
    u1 = _ada_pre_fwd(xin, g6, mods, 0, 0, nct, tile, "pre1")
    g1, up1, a1 = _mm_swiglu(u1, w_in1, "ffn1_in", tm=544)
    o1 = _mm(a1, w_out1, "nn", F32, "ffn1_out", tm=544, tn=1024, tk=2816)
    x1 = _ada_post_fwd(xin, o1, g6, mods, 1, 0, 0.5, nct, tile, "post1")
    u2 = _ada_pre_fwd(x1, g6, mods, 2, 1, nct, tile, "pre2")
    hm = _mm(u2, w_mix, "nn", F32, "mix_in", tm=544, tn=1024)

    us_ctx, us_lat = hm[:n_ctx, :ssm_w], hm[n_ctx:, :ssm_w]
    dskip = ssm_d.reshape(1, 1, ssm_w)
    s5_prm = (ssm_lam_re[0], ssm_lam_im[0], ssm_log_step[0], ssm_b_re[0], ssm_b_im[0], ssm_c_re[0], ssm_c_im[0])
    s5_tabs_both, s5_vjp = jax.vjp(_s5_tables, *s5_prm)
    s5_tabs, ups, y_dirs = [], [], []
    for dr in range(2):
        tabs = tuple(t[dr] for t in s5_tabs_both)
        up = _to_scan_layout(us_ctx, us_lat, dr == 1)
        yp = _s5_fwd(up, *tabs, dr == 1, "s5_fwd%d" % dr)
        s5_tabs.append(tabs)
        ups.append(up)
        y_dirs.append(_from_scan_layout(yp, n_ctx, dr == 1)[1])
    a_ssm = _ssm_out_fwd(y_dirs[0], y_dirs[1], hm, dskip, nct, tile, "ssm_out")
    gab = _mm(a_ssm, w_glu, "nn", BF16, "glu", tm=512, tn=2048, tk=ssm_w)

    cos, sin = _rope_tables(t_rows, ncc, dk)
    ret_tabs_both, ret_vjp = jax.vjp(functools.partial(_ret_tables, dk=dk, dv=dv), ret_decay_logit[0])
    ret_tabs, o_dirs, s_ins = [], [], []
    for dr in range(2):
        tabs = tuple(t[dr] for t in ret_tabs_both)
        o_d, s_in = _ret_fwd(hm, cos, sin, *tabs, heads, dk, dv, q_off, ncc, dr == 1, "ret_fwd%d" % dr)
        ret_tabs.append(tabs)
        o_dirs.append(o_d)
        s_ins.append(s_in)
    ret_in = _ret_gate_fwd(o_dirs[0], o_dirs[1], hm, g_off, heads, dv, nct, tile, "ret_gate")
    rb = _mm(ret_in, w_rp, "nn", BF16, "ret_proj", tm=512, tn=d, tk=v_w)
    merged = _merge_fwd(gab, rb, hm, gs_off, nct, tile, "merge")
    mix = _mm(merged, w_mo, "nn", F32, "mix_out", tm=512, tn=d, tk=d)
    x2 = _ada_post_fwd(x1, mix, g6, mods, 3, 1, 1.0, 0, tile, "post2", h_tile_offset=nct)
    u3 = _ada_pre_fwd(x2, g6, mods, 4, 2, 0, tile, "pre3")
    g3, up3, a3 = _mm_swiglu(u3, w_in2, "ffn2_in", tm=512)
    o3 = _mm(a3, w_out2, "nn", F32, "ffn2_out", tm=512, tn=1024, tk=2816)
    x3 = _ada_post_fwd(x2, o3, g6, mods, 5, 2, 0.5, 0, tile, "post3")
    dy, lcols = _loss_grad(x3, loss_target[0], tile, "loss")
    loss_part = (0.5 * jnp.sum(lcols) / d).reshape(1)

    dg6 = [None] * 6
    dmod = {}

    def add_mod(sel_rows, k, val):
        for sel, row in sel_rows:
            dmod[(sel, k)] = dmod.get((sel, k), 0.0) + val[row, 0]

    both, lat = [(0, 0), (1, 1)], [(1, 0)]

    def tie(*vals):
        return lax.optimization_barrier(vals)

    def big_update(w3d, m3d, v3d, layer, gfull, axis, name, filled=None):
        p, recv = _reduce_scatter(gfull, axis, "rs_" + name)
        return _adamw_scattered(w3d, m3d, v3d, layer, p, recv, "adamw_" + name, filled)

    do3, dg6[5], dgt = _ada_post_bwd(dy, o3, g6, mods, 5, 2, 0.5, 0, 1, tile, "post3_bwd")
    add_mod(lat, 8, dgt)
    gw_out2 = _mm(a3, do3, "tn", BF16, "ffn2_out_dw", tm=1408, tn=1024, tk=2176)
    do3, gw_out2 = tie(do3, gw_out2)
    up_out2 = big_update(ffn_w_out[0], m_ffn_w_out[0], v_ffn_w_out[0], 1, gw_out2, 0, "ffn2_out")
    da3 = _mm(do3, w_out2, "nt", BF16, "ffn2_out_dx", tm=512, tn=2816, tk=d)
    dh3 = _swiglu_bwd(g3, up3, da3, wide_tile, "swiglu2_bwd")
    gw_in2 = _mm(u3, dh3, "tn", BF16, "ffn2_in_dw", tm=1024, tn=1024, tk=2176)
    dh3, gw_in2 = tie(dh3, gw_in2)
    up_in2 = big_update(ffn_w_in[0], m_ffn_w_in[0], v_ffn_w_in[0], 1, gw_in2, 1, "ffn2_in")
    du3 = _mm(dh3, w_in2, "nt", F32, "ffn2_in_dx", tm=512, tn=d, tk=1024)
    dx2, dg6[4], dsh, dsc = _ada_pre_bwd(x2, du3, dy, g6, mods, 4, 2, 0, 1, tile, "pre3_bwd")
    add_mod(lat, 6, dsh)
    add_mod(lat, 7, dsc)
    dmix, dg6[3], dgt = _ada_post_bwd(dx2, mix, g6, mods, 3, 1, 1.0, 0, 1, tile, "post2_bwd")
    add_mod(lat, 5, dgt)
    gw_mo = _mm(merged, dmix, "tn", BF16, "mix_out_dw", tm=1024, tn=1024, tk=2176)
    dmix, gw_mo = tie(dmix, gw_mo)
    up_mo = big_update(mix_w_out, m_mix_w_out, v_mix_w_out, 0, gw_mo, 0, "mix_out")
    dmerged = _mm(dmix, w_mo, "nt", BF16, "mix_out_dx", tm=512, tn=d, tk=d)
    dgab, drb, dgs, dgr = _merge_bwd(gab, rb, hm, gs_off, dmerged, nct, tile, "merge_bwd")
    gw_glu = _mm(a_ssm, dgab, "tn", BF16, "glu_dw", tm=1024, tn=1024, tk=2176)
    gw_rp = _mm(ret_in, drb, "tn", BF16, "ret_proj_dw", tm=1024, tn=1024, tk=2176)
    dgab, drb, gw_glu, gw_rp = tie(dgab, drb, gw_glu, gw_rp)
    up_glu = big_update(ssm_glu_w, m_ssm_glu_w, v_ssm_glu_w, 0, gw_glu, 1, "glu")
    up_rp = big_update(ret_w_proj, m_ret_w_proj, v_ret_w_proj, 0, gw_rp, 0, "ret_proj")
    da_ssm = _mm(dgab, w_glu, "nt", F32, "glu_dx", tm=512, tn=ssm_w, tk=2 * d)
    dret_in = _mm(drb, w_rp, "nt", F32, "ret_proj_dx", tm=512, tn=v_w, tk=d)
    d_o, dg_gate = _ret_gate_bwd(o_dirs[0], o_dirs[1], hm, g_off, dret_in, heads, dv, nct, tile, "ret_gate_bwd")
    dy_ssm, dus_direct, d_dskip = _ssm_out_bwd(y_dirs[0], y_dirs[1], hm, dskip, da_ssm, nct, tile, "ssm_out_bwd")
    s5_table_grads, du_ctx, du_lat = [], [], [dus_direct]
    for dr in range(2):
        dyp = _to_scan_layout(jnp.zeros((n_ctx, ssm_w), F32), dy_ssm, dr == 1)
        if dr == 1:
            dyp, up_out2, up_in2 = tie(dyp, up_out2, up_in2)
        outs = _s5_bwd(ups[dr], dyp, *s5_tabs[dr], dr == 1, "s5_bwd%d" % dr)
        part_ctx, part_lat = _from_scan_layout(outs[0], n_ctx, dr == 1)
        du_ctx.append(part_ctx)
        du_lat.append(part_lat)
        s5_table_grads.append(outs[1:])
    dqkv, ret_table_grads = [], []
    for dr in range(2):
        if dr == 1:
            d_o, up_mo, up_glu, up_rp = tie(d_o, up_mo, up_glu, up_rp)
        outs = _ret_bwd(hm, cos, sin, *ret_tabs[dr], s_ins[dr], d_o, heads, dk, dv, q_off, ncc, dr == 1,
                        "ret_bwd%d" % dr)
        dqkv.append(outs[:3])
        ret_table_grads.append(outs[3:])
    both_dirs = lambda grads: tuple(jnp.stack([g0, g1]) for g0, g1 in zip(*grads))
    early_parts = list(s5_vjp(both_dirs(s5_table_grads))) + list(ret_vjp(both_dirs(ret_table_grads)))
    s5_names = 7
    early_shapes = [p.shape for p in early_parts]
    early_all = _all_gather(_pack(early_parts, 1024), 0, "ag_s5_grads", on_sequencer=True)
    early_sum = _sum_leading(early_all.reshape(N_DEV, -1, 1024), "sum_s5_grads")
    dus = jnp.concatenate([du_ctx[0] + du_ctx[1], du_lat[0] + du_lat[1] + du_lat[2]], axis=0)
    dhm = _assemble_dhm(dus, dqkv[0][0], dqkv[1][0], dqkv[0][1], dqkv[1][1], dqkv[0][2], dqkv[1][2],
                        dg_gate, dgs, dgr, n_ctx // wide_tile, wide_tile, "assemble_dhm")
    gw_mix = _mm(u2, dhm, "tn", BF16, "mix_in_dw", tm=1024, tn=1024, tk=2176)
    dhm, gw_mix = tie(dhm, gw_mix)
    up_mix = big_update(mix_w_in, m_mix_w_in, v_mix_w_in, 0, gw_mix, 1, "mix_in")
    du2 = _mm(dhm, w_mix, "nt", F32, "mix_in_dx", tm=544, tn=d, tk=1024)
    dx1, dg6[2], dsh, dsc = _ada_pre_bwd(x1, du2, dx2, g6, mods, 2, 1, nct, 2, tile, "pre2_bwd", dres_x_only=True)
    add_mod(both, 3, dsh)
    add_mod(both, 4, dsc)
    do1, dg6[1], dgt = _ada_post_bwd(dx1, o1, g6, mods, 1, 0, 0.5, nct, 2, tile, "post1_bwd")
    add_mod(both, 2, dgt)
    gw_out1 = _mm(a1, do1, "tn", BF16, "ffn1_out_dw", tm=1408, tn=1024, tk=2176)
    do1, gw_out1 = tie(do1, gw_out1)
    up_out1 = big_update(ffn_w_out[0], m_ffn_w_out[0], v_ffn_w_out[0], 0, gw_out1, 0, "ffn1_out", filled=up_out2)
    da1 = _mm(do1, w_out1, "nt", BF16, "ffn1_out_dx", tm=544, tn=2816, tk=d)
    dh1 = _swiglu_bwd(g1, up1, da1, wide_tile, "swiglu1_bwd")
    dh1, up_mix, early_sum = tie(dh1, up_mix, early_sum)
    early_sums = _unpack(early_sum, early_shapes)
    gw_in1 = _mm(u1, dh1, "tn", BF16, "ffn1_in_dw", tm=1024, tn=1024, tk=2176)
    dh1, gw_in1 = tie(dh1, gw_in1)
    up_in1 = big_update(ffn_w_in[0], m_ffn_w_in[0], v_ffn_w_in[0], 0, gw_in1, 1, "ffn1_in", filled=up_in2)
    du1 = _mm(dh1, w_in1, "nt", F32, "ffn1_in_dx", tm=544, tn=d, tk=1024)
    dx_lat, dg6[0], dsh, dsc = _ada_pre_bwd(xin, du1, dx1, g6, mods, 0, 0, nct, 2, tile, "pre1_bwd",
                                            latent_dh_only=True)
    add_mod(both, 0, dsh)
    add_mod(both, 1, dsc)
    grad_x = dx_lat[None]

    zero_d = jnp.zeros((d,), F32)
    d_ada_x = jnp.stack([dmod.get((1, k), zero_d) for k in range(9)]).reshape(9 * d)
    d_ada_c = jnp.stack([dmod.get((0, k), zero_d) for k in range(9)]).reshape(9 * d)
    dg_full = jnp.stack([g[0, 0] for g in dg6])
    small_parts = [d_ada_x, d_ada_c, dg_full, d_dskip, loss_part]
    small_shapes = [p.shape for p in small_parts]
    packed = _pack(small_parts, 1024)
    gathered = _all_gather(packed, 0, "ag_small_grads").reshape(N_DEV, -1, 1024)
    summed = _sum_leading(gathered, "sum_small_grads")
    sums = _unpack(summed, small_shapes)
    sum_dx, sum_dc, sum_dg = sums[0], sums[1], sums[2]
    loss = sums[4][0]
    grad_ada_b = (sum_dx + sum_dc)[None]
    dx_rows = gathered.reshape(N_DEV, -1)[:, :9 * d]
    col0 = me * na
    da_rows = jnp.concatenate([lax.dynamic_slice_in_dim(dx_rows, col0, na, axis=1),
                               lax.dynamic_slice_in_dim(sum_dc[None], col0, na, axis=1),
                               jnp.zeros((2 * SUBLANE - N_DEV - 1, na), F32)], axis=0)
    grad_ada_w = _mm(sc, da_rows, "tn", F32, "ada_dw", tm=512, tn=na, tk=16)
    d_sc = _mm(da_rows, ada_w[0], "nt", F32, "ada_dx", tm=16, tn=512, tk=na)
    d_sc_all = _all_gather(jnp.broadcast_to(d_sc[N_DEV:N_DEV + 1], (SUBLANE, d)), 0, "ag_dctx")
    d_sc_sum = _sum_leading(d_sc_all.reshape(N_DEV, SUBLANE, d), "sum_dctx")
    grad_c_ctx = _silu_grad_rows(jnp.broadcast_to(c_ctx[None], (SUBLANE, d)), d_sc_sum, "ctx_silu_bwd")[0]
    grad_norm_g = lax.dynamic_slice_in_dim(sum_dg, me * ng_cols, ng_cols, axis=1)[None]

    upd = {}
    upd["ffn_w_in"] = [o[None] for o in up_in1]
    upd["ffn_w_out"] = [o[None] for o in up_out1]
    upd["mix_w_in"] = list(up_mix)
    upd["ssm_glu_w"] = list(up_glu)
    upd["ret_w_proj"] = list(up_rp)
    upd["mix_w_out"] = list(up_mo)
    upd["ada_w"] = [o[None] for o in _adamw(ada_w[0], m_ada_w[0], v_ada_w[0], grad_ada_w[None], "adamw_ada_w")]

    small_names = ["c_ctx", "ada_b", "norm_g", "ssm_lam_re", "ssm_lam_im", "ssm_log_step", "ssm_b_re", "ssm_b_im",
                   "ssm_c_re", "ssm_c_im", "ssm_d", "ret_decay_logit"]
    small_w = [c_ctx, ada_b, norm_g, ssm_lam_re, ssm_lam_im, ssm_log_step, ssm_b_re, ssm_b_im, ssm_c_re, ssm_c_im,
               ssm_d, ret_decay_logit]
    small_m = [m_c_ctx, m_ada_b, m_norm_g, m_ssm_lam_re, m_ssm_lam_im, m_ssm_log_step, m_ssm_b_re, m_ssm_b_im,
               m_ssm_c_re, m_ssm_c_im, m_ssm_d, m_ret_decay_logit]
    small_v = [v_c_ctx, v_ada_b, v_norm_g, v_ssm_lam_re, v_ssm_lam_im, v_ssm_log_step, v_ssm_b_re, v_ssm_b_im,
               v_ssm_c_re, v_ssm_c_im, v_ssm_d, v_ret_decay_logit]
    small_g = [grad_c_ctx, grad_ada_b, grad_norm_g] + [s[None] for s in early_sums[:s5_names]] + \
              [sums[3].reshape(ssm_d.shape), early_sums[s5_names][None]]
    shapes = [w.shape for w in small_w]
    res = _adamw(_pack(small_w, 1024), _pack(small_m, 1024), _pack(small_v, 1024), _pack(small_g, 1024)[None],
                 "adamw_small")
    small_out = [_unpack(o, shapes) for o in res]
    for i, nm in enumerate(small_names):
        upd[nm] = [small_out[kind][i] for kind in range(4)]

    order = ["c_ctx", "ada_w", "ada_b", "norm_g", "ffn_w_in", "ffn_w_out", "mix_w_in", "ssm_lam_re", "ssm_lam_im",
             "ssm_log_step", "ssm_b_re", "ssm_b_im", "ssm_c_re", "ssm_c_im", "ssm_d", "ssm_glu_w", "ret_decay_logit",
             "ret_w_proj", "mix_w_out"]
    outs = [loss, grad_x]
    for kind in range(4):
        outs += [upd[nm][kind] for nm in order]
    return tuple(outs)
```

```python
import functools
import math

import jax
import jax.numpy as jnp
import numpy as np
from jax import lax
from jax.experimental import pallas as pl
from jax.experimental.pallas import tpu as pltpu
from jax.experimental.pallas import tpu_sc as plsc

F32 = jnp.float32
BF16 = jnp.bfloat16
MXU_DTYPE = jnp.bfloat16
MESH_AXES = ("x", "y", "c")
N_DEV = 8
V7X_VMEM_LIMIT_BYTES = 56 * 1024 * 1024
LANE = 128
SUBLANE = 8

GRID_W = 64
RET_CHUNK = 128
ROPE_BASE = 10000.0
NORM_EPS = 1e-6
ADAM_LR = 0.001
ADAM_B1 = 0.9
ADAM_B2 = 0.999
ADAM_EPS = 1e-08
ADAM_WD = 0.01
ADAM_STEP = 10
SSM_TILE_GROUPS = 8
SSM_HALF_GROUPS = 4
N_SEG = 16


def _params(sem=None):
    return pltpu.CompilerParams(dimension_semantics=sem, vmem_limit_bytes=V7X_VMEM_LIMIT_BYTES)


def _tile(n, target, mult):
    best = None
    t = mult
    while t <= min(n, target):
        if n % t == 0:
            best = t
        t += mult
    return n if best is None else best


def _sds(shape, dtype):
    return jax.ShapeDtypeStruct(tuple(shape), dtype)


def _mm(a, b, dims, out_dtype, name, tm=512, tn=1408, tk=2048):
    if dims == "nn":
        (m, k), (k2, n) = a.shape, b.shape
    elif dims == "nt":
        (m, k), (n, k2) = a.shape, b.shape
    else:
        (k, m), (k2, n) = a.shape, b.shape
    assert k == k2, (a.shape, b.shape, dims)
    tm = _tile(m, tm, 16)
    tn = _tile(n, tn, LANE)
    tk = _tile(k, tk, LANE if dims != "tn" else 16)
    nk = k // tk
    dn = {"nn": (((1,), (0,)), ((), ())), "nt": (((1,), (1,)), ((), ())), "tn": (((0,), (0,)), ((), ()))}[dims]

    def product(a_ref, b_ref):
        return lax.dot_general(a_ref[...].astype(MXU_DTYPE), b_ref[...].astype(MXU_DTYPE), dn,
                               preferred_element_type=F32)

    def body_single(a_ref, b_ref, o_ref):
        o_ref[...] = product(a_ref, b_ref).astype(o_ref.dtype)

    def body(a_ref, b_ref, o_ref, acc_ref):
        kk = pl.program_id(2)

        @pl.when(kk == 0)
        def _():
            acc_ref[...] = product(a_ref, b_ref)

        @pl.when((kk > 0) & (kk < nk - 1))
        def _():
            acc_ref[...] += product(a_ref, b_ref)

        @pl.when(kk == nk - 1)
        def _():
            o_ref[...] = (acc_ref[...] + product(a_ref, b_ref)).astype(o_ref.dtype)

    if dims == "nn":
        a_spec = pl.BlockSpec((tm, tk), lambda j, i, kk: (i, kk))
        b_spec = pl.BlockSpec((tk, tn), lambda j, i, kk: (kk, j))
    elif dims == "nt":
        a_spec = pl.BlockSpec((tm, tk), lambda j, i, kk: (i, kk))
        b_spec = pl.BlockSpec((tn, tk), lambda j, i, kk: (j, kk))
    else:
        a_spec = pl.BlockSpec((tk, tm), lambda j, i, kk: (kk, i))
        b_spec = pl.BlockSpec((tk, tn), lambda j, i, kk: (kk, j))
    return pl.pallas_call(
        body_single if nk == 1 else body, name=name, grid=(n // tn, m // tm, nk), in_specs=[a_spec, b_spec],
        out_specs=pl.BlockSpec((tm, tn), lambda j, i, kk: (i, j)), out_shape=_sds((m, n), out_dtype),
        scratch_shapes=[] if nk == 1 else [pltpu.VMEM((tm, tn), F32)],
        compiler_params=_params(("parallel", "parallel", "arbitrary")))(a, b)


def _mm_swiglu(a, b, name, tm=512, tn=512):
    m, k = a.shape
    k2, f2 = b.shape
    f = f2 // 2
    assert k == k2
    tm = _tile(m, tm, 16)
    tn = _tile(f, tn, 2 * LANE)
    nj = f // tn

    def body(a_ref, bg_ref, bu_ref, g_ref, u_ref, act_ref):
        av = a_ref[...].astype(MXU_DTYPE)
        gate = jnp.dot(av, bg_ref[...].astype(MXU_DTYPE), preferred_element_type=F32)
        up = jnp.dot(av, bu_ref[...].astype(MXU_DTYPE), preferred_element_type=F32)
        g_ref[...] = gate.astype(g_ref.dtype)
        u_ref[...] = up.astype(u_ref.dtype)
        act_ref[...] = (gate * _sigmoid(gate) * up).astype(act_ref.dtype)

    tile = pl.BlockSpec((tm, tn), lambda j, i: (i, j))
    out = _sds((m, f), BF16)
    return pl.pallas_call(
        body, name=name, grid=(nj, m // tm),
        in_specs=[pl.BlockSpec((tm, k), lambda j, i: (i, 0)), pl.BlockSpec((k, tn), lambda j, i: (0, j)),
                  pl.BlockSpec((k, tn), lambda j, i: (0, j + nj))],
        out_specs=[tile, tile, tile], out_shape=[out, out, out],
        compiler_params=_params(("parallel", "parallel")))(a, b, b)


def _rows(name, body, n_tiles, ins, outs):
    in_specs = [pl.BlockSpec(blk, imap) for (_, blk, imap) in ins]
    out_specs = [pl.BlockSpec(blk, imap) for (_, _, blk, imap) in outs]
    out_shape = [_sds(shape, dt) for (shape, dt, _, _) in outs]
    res = pl.pallas_call(body, name=name, grid=(n_tiles,), in_specs=in_specs, out_specs=out_specs,
                         out_shape=out_shape, compiler_params=_params(("arbitrary",)))(*[a for (a, _, _) in ins])
    return res


def _row_in(arr, tile, width=None, col=0, x_only_offset=None):
    width = arr.shape[1] if width is None else width
    if x_only_offset is None:
        return (arr, (tile, width), lambda i: (i, col))
    return (arr, (tile, width), lambda i: (jnp.maximum(i - x_only_offset, 0), col))


def _vec_in(arr, idx_fn):
    return (arr, (1, 1, arr.shape[2]), lambda i: (idx_fn(i), 0, 0))


def _rms(h):
    return lax.rsqrt(jnp.mean(h * h, axis=-1, keepdims=True) + NORM_EPS)


def _sigmoid(z):
    return 1.0 / (1.0 + jnp.exp(-z))


def _ada_pre_fwd(h, g6, mods, gi, mi, nct, tile, name):
    r, d = h.shape
    sel = lambda i: jnp.where(i >= nct, 1, 0)

    def body(h_ref, g_ref, sh_ref, sc_ref, u_ref):
        hh = h_ref[...]
        n = hh * _rms(hh) * g_ref[0]
        u_ref[...] = (n * (1.0 + sc_ref[0]) + sh_ref[0]).astype(u_ref.dtype)

    (u,) = _rows(name, body, r // tile,
                 [_row_in(h, tile), _vec_in(g6, lambda i: gi), _vec_in(mods, lambda i: sel(i) * 9 + 3 * mi),
                  _vec_in(mods, lambda i: sel(i) * 9 + 3 * mi + 1)],
                 [((r, d), BF16, (tile, d), lambda i: (i, 0))])
    return u


def _ada_pre_bwd(h, du, dres, g6, mods, gi, mi, nct, nsel, tile, name, dres_x_only=False, latent_dh_only=False):
    r, d = h.shape
    dh_rows = r - nct * tile if latent_dh_only else r
    dh_map = (lambda i: (jnp.maximum(i - nct, 0), 0)) if latent_dh_only else (lambda i: (i, 0))
    sel = lambda i: jnp.where(i >= nct, 1, 0) if nsel == 2 else 0
    msel = lambda i: jnp.where(i >= nct, 1, 0)
    off = nct if dres_x_only else None

    def body(h_ref, du_ref, dr_ref, g_ref, sc_ref, dh_ref, dg_ref, dsh_ref, dsc_ref):
        i = pl.program_id(0)
        hh = h_ref[...]
        rr = _rms(hh)
        g = g_ref[0]
        hn = hh * rr
        n = hn * g
        du_ = du_ref[...].astype(F32)
        dn = du_ * (1.0 + sc_ref[0])

        @pl.when(i == 0)
        def _():
            dg_ref[...] = jnp.zeros_like(dg_ref)

        @pl.when((i == 0) | (i == nct))
        def _():
            dsh_ref[...] = jnp.zeros_like(dsh_ref)
            dsc_ref[...] = jnp.zeros_like(dsc_ref)

        dg_ref[0] += jnp.sum(dn * hn, axis=0, keepdims=True)
        dsh_ref[0] += jnp.sum(du_, axis=0, keepdims=True)
        dsc_ref[0] += jnp.sum(du_ * n, axis=0, keepdims=True)
        t = dn * g
        dh = rr * t - hn * (rr * jnp.mean(t * hn, axis=-1, keepdims=True))
        if dres_x_only:
            dh_ref[...] = dh + jnp.where(i >= nct, dr_ref[...], 0.0)
        else:
            dh_ref[...] = dh + dr_ref[...]

    dh, dg, dsh, dsc = _rows(
        name, body, r // tile,
        [_row_in(h, tile), _row_in(du, tile), _row_in(dres, tile, x_only_offset=off), _vec_in(g6, lambda i: gi),
         _vec_in(mods, lambda i: msel(i) * 9 + 3 * mi + 1)],
        [((dh_rows, d), F32, (tile, d), dh_map), ((1, 1, d), F32, (1, 1, d), lambda i: (0, 0, 0)),
         ((nsel, 1, d), F32, (1, 1, d), lambda i: (sel(i), 0, 0)),
         ((nsel, 1, d), F32, (1, 1, d), lambda i: (sel(i), 0, 0))])
    return dh, dg, dsh, dsc


def _ada_post_fwd(h, o, g6, mods, gi, mi, res_w, nct, tile, name, h_tile_offset=0):
    r, d = o.shape
    sel = lambda i: jnp.where(i >= nct, 1, 0)

    def body(h_ref, o_ref, g_ref, gt_ref, y_ref):
        oo = o_ref[...]
        n = oo * _rms(oo) * g_ref[0]
        y_ref[...] = h_ref[...] + res_w * gt_ref[0] * n

    (y,) = _rows(name, body, r // tile,
                 [(h, (tile, d), lambda i: (i + h_tile_offset, 0)), _row_in(o, tile), _vec_in(g6, lambda i: gi),
                  _vec_in(mods, lambda i: sel(i) * 9 + 3 * mi + 2)],
                 [((r, d), F32, (tile, d), lambda i: (i, 0))])
    return y


def _ada_post_bwd(dy, o, g6, mods, gi, mi, res_w, nct, nsel, tile, name):
    r, d = o.shape
    sel = lambda i: jnp.where(i >= nct, 1, 0) if nsel == 2 else 0
    msel = lambda i: jnp.where(i >= nct, 1, 0)

    def body(dy_ref, o_ref, g_ref, gt_ref, do_ref, dg_ref, dgt_ref):
        i = pl.program_id(0)
        oo = o_ref[...]
        rr = _rms(oo)
        g = g_ref[0]
        on = oo * rr
        dy_ = dy_ref[...] * res_w

        @pl.when(i == 0)
        def _():
            dg_ref[...] = jnp.zeros_like(dg_ref)

        @pl.when((i == 0) | (i == nct))
        def _():
            dgt_ref[...] = jnp.zeros_like(dgt_ref)

        dgt_ref[0] += jnp.sum(dy_ * (on * g), axis=0, keepdims=True)
        dn = dy_ * gt_ref[0]
        dg_ref[0] += jnp.sum(dn * on, axis=0, keepdims=True)
        t = dn * g
        do_ref[...] = (rr * t - on * (rr * jnp.mean(t * on, axis=-1, keepdims=True))).astype(do_ref.dtype)

    do, dg, dgt = _rows(
        name, body, r // tile,
        [_row_in(dy, tile), _row_in(o, tile), _vec_in(g6, lambda i: gi),
         _vec_in(mods, lambda i: msel(i) * 9 + 3 * mi + 2)],
        [((r, d), BF16, (tile, d), lambda i: (i, 0)), ((1, 1, d), F32, (1, 1, d), lambda i: (0, 0, 0)),
         ((nsel, 1, d), F32, (1, 1, d), lambda i: (sel(i), 0, 0))])
    return do, dg, dgt


def _swiglu_bwd(gate, up, da, tile, name):
    r, f = gate.shape

    def body(g_ref, u_ref, da_ref, dh_ref):
        gt = g_ref[...].astype(F32)
        d = da_ref[...].astype(F32)
        sg = _sigmoid(gt)
        dh_ref[:, :f] = (d * u_ref[...].astype(F32) * (sg * (1.0 + gt * (1.0 - sg)))).astype(dh_ref.dtype)
        dh_ref[:, f:] = (d * gt * sg).astype(dh_ref.dtype)

    (dh,) = _rows(name, body, r // tile, [_row_in(gate, tile), _row_in(up, tile), _row_in(da, tile)],
                  [((r, 2 * f), BF16, (tile, 2 * f), lambda i: (i, 0))])
    return dh


def _gelu_parts(y):
    c0 = math.sqrt(2.0 / math.pi)
    inner = c0 * (y + 0.044715 * y * y * y)
    th = jnp.tanh(inner)
    return th, c0 * (1.0 + 3 * 0.044715 * y * y)


def _ssm_out_fwd(y0, y1, hm, dskip, nct, tile, name):
    t_rows, s = y0.shape

    def body(y0_ref, y1_ref, u_ref, d_ref, a_ref):
        y = y0_ref[...] + y1_ref[...] + d_ref[0] * u_ref[...].astype(F32)
        th, _ = _gelu_parts(y)
        a_ref[...] = (0.5 * y * (1.0 + th)).astype(a_ref.dtype)

    (a,) = _rows(name, body, t_rows // tile,
                 [_row_in(y0, tile), _row_in(y1, tile), (hm, (tile, s), lambda i: (i + nct, 0)),
                  _vec_in(dskip, lambda i: 0)],
                 [((t_rows, s), BF16, (tile, s), lambda i: (i, 0))])
    return a


def _ssm_out_bwd(y0, y1, hm, dskip, da, nct, tile, name):
    t_rows, s = y0.shape

    def body(y0_ref, y1_ref, u_ref, d_ref, da_ref, dy_ref, du_ref, dd_ref):
        i = pl.program_id(0)
        u = u_ref[...].astype(F32)
        y = y0_ref[...] + y1_ref[...] + d_ref[0] * u
        th, dinner = _gelu_parts(y)
        dy = da_ref[...] * (0.5 * (1.0 + th) + 0.5 * y * (1.0 - th * th) * dinner)
        dy_ref[...] = dy
        du_ref[...] = dy * d_ref[0]

        @pl.when(i == 0)
        def _():
            dd_ref[...] = jnp.zeros_like(dd_ref)

        dd_ref[0] += jnp.sum(dy * u, axis=0, keepdims=True)

    dy, du, dd = _rows(name, body, t_rows // tile,
                       [_row_in(y0, tile), _row_in(y1, tile), (hm, (tile, s), lambda i: (i + nct, 0)),
                        _vec_in(dskip, lambda i: 0), _row_in(da, tile)],
                       [((t_rows, s), F32, (tile, s), lambda i: (i, 0)), ((t_rows, s), F32, (tile, s), lambda i: (i, 0)),
                        ((1, 1, s), F32, (1, 1, s), lambda i: (0, 0, 0))])
    return dy, du, dd


def _col_pieces(arr, off, width, tile, nct, unit=None):
    pw = math.gcd(off, width if unit is None else unit)
    specs = [(arr, (tile, pw), functools.partial(lambda i, cb: (i + nct, cb), cb=off // pw + p))
             for p in range(width // pw)]
    return specs, pw


def _ret_gate_fwd(o0, o1, hm, g_off, heads, dv, nct, tile, name):
    t_rows, w = o0.shape
    g_specs, pw = _col_pieces(hm, g_off, w, tile, nct)
    ng = len(g_specs)

    def body(o0_ref, o1_ref, *refs):
        g_refs, r_ref = refs[:ng], refs[ng]
        for hd in range(heads):
            cs = slice(hd * dv, (hd + 1) * dv)
            o = o0_ref[:, cs] + o1_ref[:, cs]
            lo = (hd * dv) % pw
            g = g_refs[(hd * dv) // pw][:, lo:lo + dv].astype(F32)
            r_ref[:, cs] = (g * _sigmoid(g) * (o * _rms(o))).astype(r_ref.dtype)

    (ri,) = _rows(name, body, t_rows // tile, [_row_in(o0, tile), _row_in(o1, tile)] + g_specs,
                  [((t_rows, w), BF16, (tile, w), lambda i: (i, 0))])
    return ri


def _ret_gate_bwd(o0, o1, hm, g_off, dri, heads, dv, nct, tile, name):
    t_rows, w = o0.shape
    g_specs, pw = _col_pieces(hm, g_off, w, tile, nct)
    ng = len(g_specs)

    def body(o0_ref, o1_ref, d_ref, *refs):
        g_refs, do_ref, dg_ref = refs[:ng], refs[ng], refs[ng + 1]
        for hd in range(heads):
            cs = slice(hd * dv, (hd + 1) * dv)
            o = o0_ref[:, cs] + o1_ref[:, cs]
            lo = (hd * dv) % pw
            g = g_refs[(hd * dv) // pw][:, lo:lo + dv].astype(F32)
            d = d_ref[:, cs]
            rr = _rms(o)
            on = o * rr
            sg = _sigmoid(g)
            dg_ref[:, cs] = (d * on * (sg * (1.0 + g * (1.0 - sg)))).astype(dg_ref.dtype)
            t = d * (g * sg)
            do_ref[:, cs] = rr * t - on * (rr * jnp.mean(t * on, axis=-1, keepdims=True))

    do, dg = _rows(name, body, t_rows // tile, [_row_in(o0, tile), _row_in(o1, tile), _row_in(dri, tile)] + g_specs,
                   [((t_rows, w), F32, (tile, w), lambda i: (i, 0)), ((t_rows, w), BF16, (tile, w), lambda i: (i, 0))])
    return do, dg


def _merge_fwd(gab, rb, hm, gs_off, nct, tile, name):
    t_rows, d = rb.shape
    specs, pw = _col_pieces(hm, gs_off, 2 * d, tile, nct, unit=d)
    npc = d // pw

    def body(gab_ref, rb_ref, *refs):
        gs_refs, gr_refs, m_ref = refs[:npc], refs[npc:2 * npc], refs[2 * npc]
        for p in range(npc):
            cs = slice(p * pw, (p + 1) * pw)
            ga = gab_ref[:, cs].astype(F32)
            gb = gab_ref[:, d + p * pw:d + (p + 1) * pw].astype(F32)
            m_ref[:, cs] = (_sigmoid(gs_refs[p][...].astype(F32)) * (ga * _sigmoid(gb))
                            + _sigmoid(gr_refs[p][...].astype(F32)) * rb_ref[:, cs].astype(F32)).astype(m_ref.dtype)

    (mg,) = _rows(name, body, t_rows // tile, [_row_in(gab, tile), _row_in(rb, tile)] + specs,
                  [((t_rows, d), BF16, (tile, d), lambda i: (i, 0))])
    return mg


def _merge_bwd(gab, rb, hm, gs_off, dm, nct, tile, name):
    t_rows, d = rb.shape
    specs, pw = _col_pieces(hm, gs_off, 2 * d, tile, nct, unit=d)
    npc = d // pw

    def body(gab_ref, rb_ref, dm_ref, *refs):
        gs_refs, gr_refs = refs[:npc], refs[npc:2 * npc]
        dgab_ref, drb_ref, dgs_ref, dgr_ref = refs[2 * npc:]
        for p in range(npc):
            cs = slice(p * pw, (p + 1) * pw)
            cs2 = slice(d + p * pw, d + (p + 1) * pw)
            ga = gab_ref[:, cs].astype(F32)
            gb = gab_ref[:, cs2].astype(F32)
            dmm = dm_ref[:, cs].astype(F32)
            ss = _sigmoid(gs_refs[p][...].astype(F32))
            sr = _sigmoid(gr_refs[p][...].astype(F32))
            sb = _sigmoid(gb)
            dbr = dmm * ss
            dgab_ref[:, cs] = (dbr * sb).astype(dgab_ref.dtype)
            dgab_ref[:, cs2] = (dbr * ga * sb * (1.0 - sb)).astype(dgab_ref.dtype)
            drb_ref[:, cs] = (dmm * sr).astype(drb_ref.dtype)
            dgs_ref[:, cs] = (dmm * (ga * sb) * ss * (1.0 - ss)).astype(dgs_ref.dtype)
            dgr_ref[:, cs] = (dmm * rb_ref[:, cs].astype(F32) * sr * (1.0 - sr)).astype(dgr_ref.dtype)

    return _rows(name, body, t_rows // tile, [_row_in(gab, tile), _row_in(rb, tile), _row_in(dm, tile)] + specs,
                 [((t_rows, 2 * d), BF16, (tile, 2 * d), lambda i: (i, 0)), ((t_rows, d), BF16, (tile, d), lambda i: (i, 0)),
                  ((t_rows, d), BF16, (tile, d), lambda i: (i, 0)), ((t_rows, d), BF16, (tile, d), lambda i: (i, 0))])


def _assemble_dhm(dus, dq0, dq1, dk0, dk1, dv0, dv1, dg, dgs, dgr, nct, tile, name):
    r, s = dus.shape
    qk = dq0.shape[1]
    vw = dv0.shape[1]
    d = dgs.shape[1]
    mi = s + 2 * qk + 2 * vw + 2 * d
    c_q, c_k, c_v, c_g, c_gs, c_gr = s, s + qk, s + 2 * qk, s + 2 * qk + vw, s + 2 * qk + 2 * vw, s + 2 * qk + 2 * vw + d

    def body(dus_ref, dq0_ref, dq1_ref, dk0_ref, dk1_ref, dv0_ref, dv1_ref, dg_ref, dgs_ref, dgr_ref, o_ref):
        i = pl.program_id(0)
        lat = i >= nct
        o_ref[:, :s] = dus_ref[...].astype(o_ref.dtype)
        o_ref[:, c_q:c_k] = (dq0_ref[...] + dq1_ref[...]).astype(o_ref.dtype)
        o_ref[:, c_k:c_v] = (dk0_ref[...] + dk1_ref[...]).astype(o_ref.dtype)
        o_ref[:, c_v:c_g] = (dv0_ref[...] + dv1_ref[...]).astype(o_ref.dtype)
        o_ref[:, c_g:c_gs] = jnp.where(lat, dg_ref[...], 0.0).astype(o_ref.dtype)
        o_ref[:, c_gs:c_gr] = jnp.where(lat, dgs_ref[...], 0.0).astype(o_ref.dtype)
        o_ref[:, c_gr:] = jnp.where(lat, dgr_ref[...], 0.0).astype(o_ref.dtype)

    (out,) = _rows(name, body, r // tile,
                   [_row_in(dus, tile), _row_in(dq0, tile), _row_in(dq1, tile), _row_in(dk0, tile), _row_in(dk1, tile),
                    _row_in(dv0, tile), _row_in(dv1, tile), _row_in(dg, tile, x_only_offset=nct),
                    _row_in(dgs, tile, x_only_offset=nct), _row_in(dgr, tile, x_only_offset=nct)],
                   [((r, mi), BF16, (tile, mi), lambda i: (i, 0))])
    return out


def _loss_grad(y, target, tile, name):
    t_rows, d = y.shape

    def body(y_ref, t_ref, dy_ref, l_ref):
        i = pl.program_id(0)
        e = y_ref[...] - t_ref[...]
        dy_ref[...] = e * (1.0 / d)

        @pl.when(i == 0)
        def _():
            l_ref[...] = jnp.zeros_like(l_ref)

        l_ref[0] += jnp.sum(e * e, axis=0, keepdims=True)

    return _rows(name, body, t_rows // tile, [_row_in(y, tile), _row_in(target, tile)],
                 [((t_rows, d), F32, (tile, d), lambda i: (i, 0)), ((1, 1, d), F32, (1, 1, d), lambda i: (0, 0, 0))])


def _silu_rows(v, name):
    def body(v_ref, o_ref):
        z = v_ref[...]
        o_ref[...] = z * _sigmoid(z)

    (o,) = _rows(name, body, 1, [_row_in(v, v.shape[0])], [(v.shape, F32, v.shape, lambda i: (0, 0))])
    return o


def _silu_grad_rows(v, dv, name):
    def body(v_ref, d_ref, o_ref):
        z = v_ref[...]
        sg = _sigmoid(z)
        o_ref[...] = d_ref[...] * (sg * (1.0 + z * (1.0 - sg)))

    (o,) = _rows(name, body, 1, [_row_in(v, v.shape[0]), _row_in(dv, v.shape[0])],
                 [(v.shape, F32, v.shape, lambda i: (0, 0))])
    return o


def _sum_leading(g8, name):
    n, r, c = g8.shape
    tile = _tile(r, 256, SUBLANE)

    def body(g_ref, o_ref):
        acc = g_ref[0]
        for j in range(1, n):
            acc = acc + g_ref[j]
        o_ref[...] = acc

    (o,) = _rows(name, body, r // tile, [(g8, (n, tile, c), lambda i: (0, i, 0))],
                 [((r, c), F32, (tile, c), lambda i: (i, 0))])
    return o


def _pair_sum(g, recv, axis, name):
    n, br, bc = recv.shape
    tile = _tile(br, 256, 16)
    nrt = br // tile
    core = lax.axis_index("c").astype(jnp.int32).reshape(1)

    def body(c_ref, g_ref, r_ref, o_ref):
        o_ref[0] = (g_ref[...].astype(F32) + r_ref[0].astype(F32)).astype(o_ref.dtype)

    if axis == 1:
        g_spec = pl.BlockSpec((tile, bc), lambda q, i, c_ref: (i, 2 * q + c_ref[0]))
    else:
        g_spec = pl.BlockSpec((tile, bc), lambda q, i, c_ref: ((2 * q + c_ref[0]) * nrt + i, 0))
    slot = pl.BlockSpec((1, tile, bc), lambda q, i, c_ref: (q, i, 0))
    return pl.pallas_call(
        body, name=name, out_shape=_sds((n, br, bc), recv.dtype),
        grid_spec=pltpu.PrefetchScalarGridSpec(num_scalar_prefetch=1, grid=(n, nrt), in_specs=[g_spec, slot],
                                               out_specs=slot),
        compiler_params=_params(("arbitrary", "arbitrary")))(core, g, recv)


def _adam_math(w, m, v, g):
    c1 = 1.0 / (1.0 - ADAM_B1 ** ADAM_STEP)
    c2 = 1.0 / (1.0 - ADAM_B2 ** ADAM_STEP)
    mm = ADAM_B1 * m + (1.0 - ADAM_B1) * g
    vv = ADAM_B2 * v + (1.0 - ADAM_B2) * (g * g)
    return -ADAM_LR * ((mm * c1) / (jnp.sqrt(vv * c2) + ADAM_EPS) + ADAM_WD * w), mm, vv


def _adamw(w, m, v, gparts, name):
    r, c = w.shape
    n = gparts.shape[0]
    tile = _tile(r, 256, 16)

    def body(w_ref, m_ref, v_ref, g_ref, go_ref, d_ref, mo_ref, vo_ref):
        g = g_ref[0].astype(F32)
        for j in range(1, n):
            g = g + g_ref[j].astype(F32)
        go_ref[...] = g
        d_ref[...], mo_ref[...], vo_ref[...] = _adam_math(w_ref[...], m_ref[...], v_ref[...], g)

    rs = lambda arr: _row_in(arr, tile)
    out = ((r, c), F32, (tile, c), lambda i: (i, 0))
    return _rows(name, body, r // tile, [rs(w), rs(m), rs(v), (gparts, (n, tile, c), lambda i: (0, i, 0))],
                 [out, out, out, out])


def _adamw_scattered(w, m, v, layer, p, recv, name, filled=None):
    nl, r, c = w.shape
    n = recv.shape[0]
    tile = _tile(r, 256, 16)
    chip = (2 * lax.axis_index("x") + lax.axis_index("y")).astype(jnp.int32).reshape(1)
    n_prev = 0 if filled is None else len(filled)

    def body(q_ref, w_ref, m_ref, v_ref, p_ref, g_ref, *rest):
        go_ref, d_ref, mo_ref, vo_ref = rest[n_prev:]
        g = p_ref[0].astype(F32)
        for j in range(n):
            g = g + g_ref[j].astype(F32)
        go_ref[0] = g
        d_ref[0], mo_ref[0], vo_ref[0] = _adam_math(w_ref[0], m_ref[0], v_ref[0], g)

    slab = pl.BlockSpec((1, tile, c), lambda i, q_ref: (layer, i, 0))
    anywhere = pl.BlockSpec(memory_space=pl.ANY)
    out = _sds((nl, r, c), F32)
    prev = [] if filled is None else list(filled)
    return pl.pallas_call(
        body, name=name, out_shape=[out, out, out, out],
        grid_spec=pltpu.PrefetchScalarGridSpec(
            num_scalar_prefetch=1, grid=(r // tile,),
            in_specs=[slab, slab, slab, pl.BlockSpec((1, tile, c), lambda i, q_ref: (q_ref[0], i, 0)),
                      pl.BlockSpec((n, tile, c), lambda i, q_ref: (0, i, 0))] + [anywhere] * n_prev,
            out_specs=[slab, slab, slab, slab]),
        input_output_aliases={6 + j: j for j in range(n_prev)},
        compiler_params=_params(("arbitrary",)))(chip, w, m, v, p, recv, *prev)


def _cmul(ar, ai, br, bi):
    return ar * br - ai * bi, ar * bi + ai * br


def _cpow(ar, ai, n):
    pr, pi = jnp.ones_like(ar), jnp.zeros_like(ar)
    br, bi = ar, ai
    while n:
        if n & 1:
            pr, pi = _cmul(pr, pi, br, bi)
        n >>= 1
        if n:
            br, bi = _cmul(br, bi, br, bi)
    return pr, pi


def _s5_scan_into(x_ref, ar1, ai1, ns, fin_ref, hin_ref, reverse, paired=None):
    st = ar1.shape[1]
    ar = jnp.broadcast_to(ar1, (N_SEG, st))
    ai = jnp.broadcast_to(ai1, (N_SEG, st))
    zero = jnp.zeros((N_SEG, st), F32)

    def slab(k):
        if isinstance(k, int):
            return pl.ds(k * N_SEG, N_SEG)
        return pl.ds(pl.multiple_of(k * N_SEG, N_SEG), N_SEG)

    def pass1(j, carry):
        hr, hi = carry
        k = ns - 1 - j if reverse else j
        nr, ni = _cmul(ar, ai, hr, hi)
        return nr + x_ref[slab(k), :st], ni + x_ref[slab(k), st:]

    fr, fi = lax.fori_loop(0, ns, pass1, (zero, zero))
    fin_ref[:, :st] = fr
    fin_ref[:, st:] = fi
    pr, pi = _cpow(ar1, ai1, ns)
    order = list(range(N_SEG - 1, -1, -1)) if reverse else list(range(N_SEG))
    hin_ref[order[0]:order[0] + 1, :] = jnp.zeros((1, 2 * st), F32)
    for a_, b_ in zip(order[:-1], order[1:]):
        cr, ci = _cmul(pr, pi, hin_ref[a_:a_ + 1, :st], hin_ref[a_:a_ + 1, st:])
        hin_ref[b_:b_ + 1, :st] = cr + fin_ref[a_:a_ + 1, :st]
        hin_ref[b_:b_ + 1, st:] = ci + fin_ref[a_:a_ + 1, st:]

    def step2(k, hr, hi):
        nr, ni = _cmul(ar, ai, hr, hi)
        nr = nr + x_ref[slab(k), :st]
        ni = ni + x_ref[slab(k), st:]
        x_ref[slab(k), :st] = nr
        x_ref[slab(k), st:] = ni
        return nr, ni

    if paired is None:
        def pass2(j, carry):
            return step2(ns - 1 - j if reverse else j, *carry)

        lax.fori_loop(0, ns, pass2, (hin_ref[:, :st], hin_ref[:, st:]))
        return None
    p_ref, p_edge_ref, shift = paired

    def pass2_paired(j, carry):
        hr, hi, acr, aci = carry
        k = ns - 1 - j if reverse else j
        nr, ni = step2(k, hr, hi)
        p_r, p_i = p_ref[slab(k + shift), :st], p_ref[slab(k + shift), st:]
        return nr, ni, acr + nr * p_r + ni * p_i, aci + ni * p_r - nr * p_i

    hr, hi, acr, aci = lax.fori_loop(0, ns - 1, pass2_paired, (hin_ref[:, :st], hin_ref[:, st:], zero, zero))
    nr, ni = step2(0 if reverse else ns - 1, hr, hi)
    p_r, p_i = p_edge_ref[:, :st], p_edge_ref[:, st:]
    return acr + nr * p_r + ni * p_i, aci + ni * p_r - nr * p_i


def _s5_specs(r, ch, st):
    u_spec = pl.BlockSpec((r, ch), lambda j: (0, j // 2))
    w_spec = pl.BlockSpec((1, ch, 2 * st), lambda j: (j, 0, 0))
    c_spec = pl.BlockSpec((1, 2 * st, ch), lambda j: (j, 0, 0))
    a_spec = pl.BlockSpec((1, 2, st), lambda j: (j, 0, 0))
    return u_spec, w_spec, c_spec, a_spec


def _s5_fwd(up, w, c, a, rev, name):
    r, s = up.shape
    nh, ch, st2 = w.shape
    st = st2 // 2
    ns = r // N_SEG
    nb = r // N_DEV
    u_spec, w_spec, c_spec, a_spec = _s5_specs(r, ch, st)

    def body(u_ref, w_ref, c_ref, a_ref, y_ref, x, fin, hin):
        j = pl.program_id(0)
        w_b = w_ref[0].astype(MXU_DTYPE)
        c_b = c_ref[0].astype(MXU_DTYPE)
        for rb in range(N_DEV):
            rows = slice(rb * nb, (rb + 1) * nb)
            x[rows, :] = jnp.dot(u_ref[rows, :].astype(MXU_DTYPE), w_b, preferred_element_type=F32)
        _s5_scan_into(x, a_ref[0, 0:1, :], a_ref[0, 1:2, :], ns, fin, hin, rev)
        for rb in range(N_DEV):
            rows = slice(rb * nb, (rb + 1) * nb)
            yb = jnp.dot(x[rows, :].astype(MXU_DTYPE), c_b, preferred_element_type=F32)

            @pl.when(j % 2 == 0)
            def _():
                y_ref[rows, :] = yb

            @pl.when(j % 2 == 1)
            def _():
                y_ref[rows, :] += yb

    small = pltpu.VMEM((N_SEG, st2), F32)
    return pl.pallas_call(
        body, name=name, grid=(nh,), in_specs=[u_spec, w_spec, c_spec, a_spec],
        out_specs=pl.BlockSpec((r, ch), lambda j: (0, j // 2)), out_shape=_sds((r, s), F32),
        scratch_shapes=[pltpu.VMEM((r, st2), F32), small, small],
        compiler_params=_params(("arbitrary",)))(up, w, c, a)


def _s5_bwd(up, dyp, w, c, a, rev, name):
    r, s = up.shape
    nh, ch, st2 = w.shape
    st = st2 // 2
    ns = r // N_SEG
    nb = r // N_DEV
    u_spec, w_spec, c_spec, a_spec = _s5_specs(r, ch, st)
    nt = (((1,), (1,)), ((), ()))
    tn = (((0,), (0,)), ((), ()))

    def body(u_ref, dy_ref, w_ref, c_ref, a_ref, du_ref, dw_ref, dc_ref, da_ref, h, g, fin, sin_, ein):
        j = pl.program_id(0)
        w_b = w_ref[0].astype(MXU_DTYPE)
        c_b = c_ref[0].astype(MXU_DTYPE)
        for rb in range(N_DEV):
            rows = slice(rb * nb, (rb + 1) * nb)
            h[rows, :] = jnp.dot(u_ref[rows, :].astype(MXU_DTYPE), w_b, preferred_element_type=F32)
        ar1, ai1 = a_ref[0, 0:1, :], a_ref[0, 1:2, :]
        _s5_scan_into(h, ar1, ai1, ns, fin, sin_, rev)
        dc = jnp.zeros((st2, ch), F32)
        for rb in range(N_DEV):
            rows = slice(rb * nb, (rb + 1) * nb)
            dyb = dy_ref[rows, :].astype(MXU_DTYPE)
            g[rows, :] = lax.dot_general(dyb, c_b, nt, preferred_element_type=F32)
            dc += lax.dot_general(h[rows, :].astype(MXU_DTYPE), dyb, tn, preferred_element_type=F32)
        dc_ref[0] = dc
        acr, aci = _s5_scan_into(g, ar1, -ai1, ns, fin, ein, not rev, paired=(h, sin_, 1 if rev else -1))
        da_ref[0, 0:1, :] = jnp.sum(acr, axis=0, keepdims=True)
        da_ref[0, 1:2, :] = jnp.sum(aci, axis=0, keepdims=True)
        dw = jnp.zeros((ch, st2), F32)
        for rb in range(N_DEV):
            rows = slice(rb * nb, (rb + 1) * nb)
            gb = g[rows, :].astype(MXU_DTYPE)
            dub = lax.dot_general(gb, w_b, nt, preferred_element_type=F32)
            dw += lax.dot_general(u_ref[rows, :].astype(MXU_DTYPE), gb, tn, preferred_element_type=F32)

            @pl.when(j % 2 == 0)
            def _():
                du_ref[rows, :] = dub

            @pl.when(j % 2 == 1)
            def _():
                du_ref[rows, :] += dub

        dw_ref[0] = dw

    small = pltpu.VMEM((N_SEG, st2), F32)
    big = pltpu.VMEM((r, st2), F32)
    return pl.pallas_call(
        body, name=name, grid=(nh,), in_specs=[u_spec, u_spec, w_spec, c_spec, a_spec],
        out_specs=[pl.BlockSpec((r, ch), lambda j: (0, j // 2)), w_spec, c_spec, a_spec],
        out_shape=[_sds((r, s), F32), _sds(w.shape, F32), _sds(c.shape, F32), _sds(a.shape, F32)],
        scratch_shapes=[big, big, small, small, small],
        compiler_params=_params(("arbitrary",)))(up, dyp, w, c, a)


def _rope(t, cos, sin):
    quarter = t.shape[1] // 4
    lane = lax.broadcasted_iota(jnp.int32, t.shape, 1)
    first = (lane // quarter) % 2 == 0
    partner = jnp.where(first, pltpu.roll(t, t.shape[1] - quarter, 1), pltpu.roll(t, quarter, 1))
    return t * cos + partner * sin


def _rope_t(d, cos, sin):
    quarter = d.shape[1] // 4
    ds_ = d * sin
    lane = lax.broadcasted_iota(jnp.int32, d.shape, 1)
    first = (lane // quarter) % 2 == 0
    partner = jnp.where(first, pltpu.roll(ds_, d.shape[1] - quarter, 1), pltpu.roll(ds_, quarter, 1))
    return d * cos + partner


def _chunk_of_step(s, nch, ncc, rev):
    if not rev:
        return s
    return jnp.where(s < ncc, ncc - 1 - s, nch + ncc - 1 - s)


def _heads_per_step(heads, dk, dv, q_off):
    v_off = q_off + 2 * heads * dk
    for hpg in range(heads, 0, -1):
        if heads % hpg == 0 and q_off % (hpg * dk) == 0:
            piece = math.gcd(v_off, hpg * dv)
            if piece % dv == 0:
                return hpg, piece
    return 1, dv


def _v_specs(hpg, dv, piece, v_off, ch, chunk_of):
    n_pieces = hpg * dv // piece
    return [pl.BlockSpec((ch, piece), functools.partial(
        lambda h, s, p: (chunk_of(s), v_off // piece + h * n_pieces + p), p=p)) for p in range(n_pieces)]


def _v_of_head(v_refs, hl, dv, piece):
    lo = (hl * dv) % piece
    return v_refs[(hl * dv) // piece][:, lo:lo + dv]


def _ret_fwd(hm, cos, sin, decay, wend, win, gch, heads, dk, dv, q_off, ncc, rev, name):
    r = hm.shape[0]
    ch = RET_CHUNK
    nch = r // ch
    t_rows = r - ncc * ch
    hpg, piece = _heads_per_step(heads, dk, dv, q_off)
    qb, kb = q_off // (hpg * dk), (q_off + heads * dk) // (hpg * dk)
    q_scale = dk ** -0.5
    nt = (((1,), (1,)), ((), ()))
    tn = (((0,), (0,)), ((), ()))
    cof = lambda s: _chunk_of_step(s, nch, ncc, rev)
    v_specs = _v_specs(hpg, dv, piece, q_off + 2 * heads * dk, ch, cof)
    nv = len(v_specs)

    def body(q_ref, k_ref, *refs):
        v_refs = refs[:nv]
        cos_ref, sin_ref, dec_ref, we_ref, wi_ref, g_ref, o_ref, sin_out, st = refs[nv:]
        s = pl.program_id(1)

        @pl.when(s == 0)
        def _():
            st[...] = jnp.zeros_like(st)

        cos_, sin_ = cos_ref[...], sin_ref[...]
        for hl in range(hpg):
            ks, vs = slice(hl * dk, (hl + 1) * dk), slice(hl * dv, (hl + 1) * dv)
            q = _rope(q_ref[:, ks].astype(F32), cos_, sin_) * q_scale
            k = _rope(k_ref[:, ks].astype(F32), cos_, sin_)
            v = _v_of_head(v_refs, hl, dv, piece).astype(MXU_DTYPE)
            s_cur = st[hl]
            sin_out[hl, 0] = s_cur
            kw = (k * we_ref[hl]).astype(MXU_DTYPE)
            qw = (q * wi_ref[hl]).astype(MXU_DTYPE)
            scores = lax.dot_general(q.astype(MXU_DTYPE), k.astype(MXU_DTYPE), nt,
                                     preferred_element_type=F32) * dec_ref[hl]
            o_ref[:, vs] = (jnp.dot(scores.astype(MXU_DTYPE), v, preferred_element_type=F32)
                            + jnp.dot(qw, s_cur.astype(MXU_DTYPE), preferred_element_type=F32))
            st[hl] = g_ref[hl] * s_cur + lax.dot_general(kw, v, tn, preferred_element_type=F32)

    tab = lambda w: pl.BlockSpec((hpg, ch, w), lambda h, s: (h, 0, 0))
    return pl.pallas_call(
        body, name=name, grid=(heads // hpg, nch),
        in_specs=[pl.BlockSpec((ch, hpg * dk), lambda h, s: (cof(s), qb + h)),
                  pl.BlockSpec((ch, hpg * dk), lambda h, s: (cof(s), kb + h))] + v_specs +
                 [pl.BlockSpec((ch, dk), lambda h, s: (cof(s), 0)),
                  pl.BlockSpec((ch, dk), lambda h, s: (cof(s), 0)),
                  tab(ch), tab(dk), tab(dk), tab(dv)],
        out_specs=[pl.BlockSpec((ch, hpg * dv), lambda h, s: (jnp.maximum(cof(s) - ncc, 0) if not rev
                                                               else jnp.where(s < ncc, nch - ncc - 1, cof(s) - ncc), h)),
                   pl.BlockSpec((hpg, 1, dk, dv), lambda h, s: (h, s, 0, 0))],
        out_shape=[_sds((t_rows, heads * dv), F32), _sds((heads, nch, dk, dv), F32)],
        scratch_shapes=[pltpu.VMEM((hpg, dk, dv), F32)],
        compiler_params=_params(("parallel", "arbitrary")))(hm, hm, *([hm] * nv), cos, sin, decay, wend, win, gch)


def _ret_bwd(hm, cos, sin, decay, wend, win, gch, s_in, do, heads, dk, dv, q_off, ncc, rev, name):
    r = hm.shape[0]
    ch = RET_CHUNK
    nch = r // ch
    hpg, piece = _heads_per_step(heads, dk, dv, q_off)
    qb, kb = q_off // (hpg * dk), (q_off + heads * dk) // (hpg * dk)
    q_scale = dk ** -0.5
    nt = (((1,), (1,)), ((), ()))
    tn = (((0,), (0,)), ((), ()))
    cof = lambda rr: _chunk_of_step(nch - 1 - rr, nch, ncc, rev)
    v_specs = _v_specs(hpg, dv, piece, q_off + 2 * heads * dk, ch, cof)
    nv = len(v_specs)

    def body(q_ref, k_ref, *refs):
        v_refs = refs[:nv]
        (cos_ref, sin_ref, dec_ref, we_ref, wi_ref, g_ref, sin_ref2, do_ref,
         dq_ref, dk_ref, dv_ref, ddec_ref, dwe_ref, dwi_ref, dg_ref, dst) = refs[nv:]
        rr = pl.program_id(1)
        n = cof(rr)

        @pl.when(rr == 0)
        def _():
            dst[...] = jnp.zeros_like(dst)
            ddec_ref[...] = jnp.zeros_like(ddec_ref)
            dwe_ref[...] = jnp.zeros_like(dwe_ref)
            dwi_ref[...] = jnp.zeros_like(dwi_ref)
            dg_ref[...] = jnp.zeros_like(dg_ref)

        cos_, sin_ = cos_ref[...], sin_ref[...]
        for hl in range(hpg):
            ks, vs = slice(hl * dk, (hl + 1) * dk), slice(hl * dv, (hl + 1) * dv)
            q = _rope(q_ref[:, ks].astype(F32), cos_, sin_) * q_scale
            k = _rope(k_ref[:, ks].astype(F32), cos_, sin_)
            v = _v_of_head(v_refs, hl, dv, piece).astype(MXU_DTYPE)
            qb_, kb_ = q.astype(MXU_DTYPE), k.astype(MXU_DTYPE)
            kw = (k * we_ref[hl]).astype(MXU_DTYPE)
            qw = (q * wi_ref[hl]).astype(MXU_DTYPE)
            sraw = lax.dot_general(qb_, kb_, nt, preferred_element_type=F32)
            scores = (sraw * dec_ref[hl]).astype(MXU_DTYPE)
            d_o = jnp.where(n >= ncc, do_ref[:, vs], 0.0).astype(MXU_DTYPE)
            s_n = sin_ref2[hl, 0]
            s_nb = s_n.astype(MXU_DTYPE)
            ds1 = dst[hl]
            ds1b = ds1.astype(MXU_DTYPE)
            dsc = lax.dot_general(d_o, v, nt, preferred_element_type=F32)
            dsr = (dsc * dec_ref[hl]).astype(MXU_DTYPE)
            ddec_ref[hl] += dsc * sraw
            t1 = lax.dot_general(d_o, s_nb, nt, preferred_element_type=F32)
            dq_r = jnp.dot(dsr, kb_, preferred_element_type=F32) + t1 * wi_ref[hl]
            dwi_ref[hl] += t1 * q
            t2 = lax.dot_general(v, ds1b, nt, preferred_element_type=F32)
            dk_r = lax.dot_general(dsr, qb_, tn, preferred_element_type=F32) + t2 * we_ref[hl]
            dwe_ref[hl] += t2 * k
            dv_ref[:, vs] = (lax.dot_general(scores, d_o, tn, preferred_element_type=F32)
                             + jnp.dot(kw, ds1b, preferred_element_type=F32))
            dg_ref[hl] += ds1 * s_n
            dst[hl] = g_ref[hl] * ds1 + lax.dot_general(qw, d_o, tn, preferred_element_type=F32)
            dq_ref[:, ks] = _rope_t(dq_r, cos_, sin_) * q_scale
            dk_ref[:, ks] = _rope_t(dk_r, cos_, sin_)

    tab = lambda w: pl.BlockSpec((hpg, ch, w), lambda h, rr: (h, 0, 0))
    return pl.pallas_call(
        body, name=name, grid=(heads // hpg, nch),
        in_specs=[pl.BlockSpec((ch, hpg * dk), lambda h, rr: (cof(rr), qb + h)),
                  pl.BlockSpec((ch, hpg * dk), lambda h, rr: (cof(rr), kb + h))] + v_specs +
                 [pl.BlockSpec((ch, dk), lambda h, rr: (cof(rr), 0)),
                  pl.BlockSpec((ch, dk), lambda h, rr: (cof(rr), 0)),
                  tab(ch), tab(dk), tab(dk), tab(dv),
                  pl.BlockSpec((hpg, 1, dk, dv), lambda h, rr: (h, nch - 1 - rr, 0, 0)),
                  pl.BlockSpec((ch, hpg * dv), lambda h, rr: (jnp.maximum(cof(rr) - ncc, 0), h))],
        out_specs=[pl.BlockSpec((ch, hpg * dk), lambda h, rr: (cof(rr), h)),
                   pl.BlockSpec((ch, hpg * dk), lambda h, rr: (cof(rr), h)),
                   pl.BlockSpec((ch, hpg * dv), lambda h, rr: (cof(rr), h)),
                   tab(ch), tab(dk), tab(dk), tab(dv)],
        out_shape=[_sds((r, heads * dk), F32), _sds((r, heads * dk), F32), _sds((r, heads * dv), F32),
                   _sds(decay.shape, F32), _sds(wend.shape, F32), _sds(win.shape, F32), _sds(gch.shape, F32)],
        scratch_shapes=[pltpu.VMEM((hpg, dk, dv), F32)],
        compiler_params=_params(("parallel", "arbitrary")))(hm, hm, *([hm] * nv), cos, sin, decay, wend, win, gch, s_in, do)


_HBM = pl.BlockSpec(memory_space=pltpu.HBM)
_MESH = pl.DeviceIdType.MESH
ALL_GATHER_COLLECTIVE_ID = 1
SIBLING_COLLECTIVE_ID = 2
CHIPS_COLLECTIVE_ID = 3


def _axis_slice(ref, axis, start, size):
    idx = [slice(None)] * len(ref.shape)
    idx[axis] = pl.ds(start, size)
    return ref.at[tuple(idx)]


def _sibling_and_chip_peers():
    x, y, c = lax.axis_index("x"), lax.axis_index("y"), lax.axis_index("c")
    return [(x, y, 1 - c), (1 - x, y, c), (x, 1 - y, c), (1 - x, 1 - y, c)]


def _launch_exchange(body, name, operand, out_shape, sems, peers_fn, collective_id, on_sequencer):
    if not on_sequencer:
        return pl.pallas_call(body, name=name, out_shape=out_shape, in_specs=[_HBM], out_specs=_HBM,
                              scratch_shapes=sems)(operand)

    def sequencer_body(in_ref, out_ref, *sem_refs):
        peers = peers_fn()
        barrier = pltpu.get_barrier_semaphore()
        for peer in peers:
            pl.semaphore_signal(barrier, inc=1, device_id=peer, device_id_type=_MESH)
        pl.semaphore_wait(barrier, len(peers))
        body(in_ref, out_ref, *sem_refs)

    return pl.kernel(sequencer_body, out_type=out_shape, name=name,
                     mesh=plsc.ScalarSubcoreMesh(axis_name="sequencer", num_cores=1), scratch_types=sems,
                     compiler_params=pltpu.CompilerParams(collective_id=collective_id))(operand)


def _all_gather(shard, axis, name, on_sequencer=False):
    m = shard.shape[axis]
    out_shape = list(shard.shape)
    out_shape[axis] = N_DEV * m

    def body(x_ref, out_ref, send_sems, recv_sems, local_sem):
        x, y, c = lax.axis_index("x"), lax.axis_index("y"), lax.axis_index("c")
        me, sibling = (x, y, c), (x, y, 1 - c)
        chips = [(1 - x, y), (x, 1 - y), (1 - x, 1 - y)]

        def block(px, py, pc):
            return _axis_slice(out_ref, axis, (4 * px + 2 * py + pc) * m, m)

        def copy(k, blk, to, src=None):
            return pltpu.make_async_remote_copy(
                src_ref=block(*blk) if src is None else src, dst_ref=block(*blk), send_sem=send_sems.at[k],
                recv_sem=recv_sems.at[k], device_id=to, device_id_type=_MESH)

        mine = pltpu.make_async_copy(x_ref, block(*me), local_sem)
        mine.start()
        first = [copy(0, me, sibling, src=x_ref)]
        first += [copy(1 + j, me, (*chip, c), src=x_ref) for j, chip in enumerate(chips)]
        for cp in first:
            cp.start()
        passed = [copy(4 + j, (*chip, c), sibling) for j, chip in enumerate(chips)]
        for j, chip in enumerate(chips):
            copy(1 + j, (*chip, c), me).wait_recv()
            passed[j].start()
        copy(0, sibling, me).wait_recv()
        for j, chip in enumerate(chips):
            copy(4 + j, (*chip, 1 - c), me).wait_recv()
        for cp in first + passed:
            cp.wait_send()
        mine.wait()

    return _launch_exchange(
        body, name, shard, _sds(out_shape, shard.dtype),
        [pltpu.SemaphoreType.DMA((7,)), pltpu.SemaphoreType.DMA((7,)), pltpu.SemaphoreType.DMA(())],
        _sibling_and_chip_peers, ALL_GATHER_COLLECTIVE_ID, on_sequencer)


def _rs_sibling(g, axis, name, on_sequencer=False):
    m = g.shape[axis] // N_DEV
    blk_shape = list(g.shape)
    blk_shape[axis] = m
    n_chips = N_DEV // 2

    def body(g_ref, recv_ref, send_sems, recv_sems):
        x, y, c = lax.axis_index("x"), lax.axis_index("y"), lax.axis_index("c")
        sibling = (x, y, 1 - c)
        send = [pltpu.make_async_remote_copy(
            src_ref=_axis_slice(g_ref, axis, (2 * q + 1 - c) * m, m), dst_ref=recv_ref.at[q],
            send_sem=send_sems.at[q], recv_sem=recv_sems.at[q], device_id=sibling, device_id_type=_MESH)
            for q in range(n_chips)]
        for cp in send:
            cp.start()
        for cp in send:
            cp.wait_recv()
        for cp in send:
            cp.wait_send()

    return _launch_exchange(
        body, name, g, _sds([n_chips] + blk_shape, g.dtype),
        [pltpu.SemaphoreType.DMA((n_chips,)), pltpu.SemaphoreType.DMA((n_chips,))],
        lambda: _sibling_and_chip_peers()[:1], SIBLING_COLLECTIVE_ID, on_sequencer)


def _rs_chips(p, name, on_sequencer=False):
    n_peers = p.shape[0] - 1

    def body(p_ref, out_ref, send_sems, recv_sems):
        x, y, c = lax.axis_index("x"), lax.axis_index("y"), lax.axis_index("c")
        chips = [(1 - x, y), (x, 1 - y), (1 - x, 1 - y)]
        send = [pltpu.make_async_remote_copy(
            src_ref=p_ref.at[2 * cx + cy], dst_ref=out_ref.at[j], send_sem=send_sems.at[j],
            recv_sem=recv_sems.at[j], device_id=(cx, cy, c), device_id_type=_MESH)
            for j, (cx, cy) in enumerate(chips)]
        for cp in send:
            cp.start()
        for cp in send:
            cp.wait_recv()
        for cp in send:
            cp.wait_send()

    return _launch_exchange(
        body, name, p, _sds((n_peers,) + p.shape[1:], p.dtype),
        [pltpu.SemaphoreType.DMA((n_peers,)), pltpu.SemaphoreType.DMA((n_peers,))],
        lambda: _sibling_and_chip_peers()[1:], CHIPS_COLLECTIVE_ID, on_sequencer)


def _reduce_scatter(g, axis, name):
    sib = _rs_sibling(g, axis, name + "_d2d", on_sequencer=True)
    p = _pair_sum(g, sib, axis, name + "_pair")
    return p, _rs_chips(p, name + "_ici", on_sequencer=True)


def _s5_tables(lam_re, lam_im, log_step, b_re, b_im, c_re, c_im):
    nd, g, p, cg = b_re.shape
    step = jnp.exp(log_step)[..., None]
    mag = jnp.exp(lam_re * step)
    a_re, a_im = mag * jnp.cos(lam_im * step), mag * jnp.sin(lam_im * step)
    den = lam_re * lam_re + lam_im * lam_im
    num_re, num_im = a_re - 1.0, a_im
    k_re = (num_re * lam_re + num_im * lam_im) / den
    k_im = (num_im * lam_re - num_re * lam_im) / den
    bb_re = k_re[..., None] * b_re - k_im[..., None] * b_im
    bb_im = k_re[..., None] * b_im + k_im[..., None] * b_re
    gt = g // SSM_TILE_GROUPS
    hg = SSM_HALF_GROUPS
    eye = jnp.eye(SSM_TILE_GROUPS, dtype=F32).reshape(SSM_TILE_GROUPS, 2, hg)

    def pack_b(bb):
        w = jnp.einsum("djhqpc,ghq->djhgcqp", bb.reshape(nd, gt, 2, hg, p, cg), eye)
        return w.reshape(nd, gt * 2, SSM_TILE_GROUPS * cg, hg * p)

    def pack_c(cc):
        w = jnp.einsum("djhqcp,ghq->djhqpgc", cc.reshape(nd, gt, 2, hg, cg, p), eye)
        return w.reshape(nd, gt * 2, hg * p, SSM_TILE_GROUPS * cg)

    a = jnp.stack([a_re.reshape(nd, gt * 2, hg * p), a_im.reshape(nd, gt * 2, hg * p)], axis=2)
    w = jnp.concatenate([pack_b(bb_re), pack_b(bb_im)], axis=-1)
    c = jnp.concatenate([pack_c(c_re), -pack_c(c_im)], axis=-2)
    return w, c, a


def _ret_tables(decay_logit, dk, dv):
    ch = RET_CHUNK
    nd, h = decay_logit.shape
    lg = jax.nn.log_sigmoid(decay_logit)[:, :, None]
    pos = jnp.arange(ch, dtype=F32)
    fwd_diff = pos[:, None] - pos[None, :]
    diff = jnp.stack([fwd_diff, -fwd_diff])[:, None]
    mask = jnp.stack([fwd_diff >= 0, -fwd_diff > 0])[:, None]
    end_pos = jnp.stack([ch - 1.0 - pos, pos])[:, None]
    in_pos = jnp.stack([pos + 1.0, ch - pos])[:, None]
    w_end = jnp.exp(lg * end_pos)
    w_in = jnp.exp(lg * in_pos)
    decay = jnp.where(mask, jnp.exp(lg[..., None] * jnp.where(mask, diff, 0.0)), 0.0)
    g_chunk = jnp.exp(lg[..., 0] * ch)
    return (decay, jnp.broadcast_to(w_end[..., None], (nd, h, ch, dk)), jnp.broadcast_to(w_in[..., None], (nd, h, ch, dk)),
            jnp.broadcast_to(g_chunk[..., None, None], (nd, h, dk, dv)))


def _rope_tables(t_rows, ncc, dk):
    quarter = dk // 4
    idx = np.arange(t_rows)
    row, col = idx // GRID_W, idx % GRID_W
    inv = ROPE_BASE ** (-np.arange(quarter, dtype=np.float32) / quarter)
    ang_r = row.astype(np.float32)[:, None] * inv
    ang_c = col.astype(np.float32)[:, None] * inv
    ang_r, ang_c = jnp.asarray(ang_r, F32), jnp.asarray(ang_c, F32)
    cos = jnp.concatenate([jnp.cos(ang_r), jnp.cos(ang_r), jnp.cos(ang_c), jnp.cos(ang_c)], axis=1)
    sin = jnp.concatenate([-jnp.sin(ang_r), jnp.sin(ang_r), -jnp.sin(ang_c), jnp.sin(ang_c)], axis=1)
    n_ctx = ncc * RET_CHUNK
    cos = jnp.concatenate([jnp.ones((n_ctx, dk), F32), cos], axis=0)
    sin = jnp.concatenate([jnp.zeros((n_ctx, dk), F32), sin], axis=0)
    return cos, sin


def _to_scan_layout(ctx_rows, lat_rows, rev):
    u = jnp.concatenate([lat_rows, ctx_rows] if rev else [ctx_rows, lat_rows], axis=0)
    r, w = u.shape
    return u.reshape(N_SEG, r // N_SEG, w).transpose(1, 0, 2).reshape(r, w)


def _from_scan_layout(yp, n_ctx, rev):
    r, w = yp.shape
    y = yp.reshape(r // N_SEG, N_SEG, w).transpose(1, 0, 2).reshape(r, w)
    return (y[r - n_ctx:], y[:r - n_ctx]) if rev else (y[:n_ctx], y[n_ctx:])


def _pack(parts, width):
    rows = []
    for p in parts:
        flat = p.reshape(-1).astype(F32)
        n = flat.shape[0]
        rows.append(jnp.pad(flat, (0, -n % (SUBLANE * width))).reshape(-1, width))
    return jnp.concatenate(rows, axis=0)


def _packed_rows(n, width):
    return -(-n // (SUBLANE * width)) * SUBLANE


def _unpack(flat2d, shapes):
    width = flat2d.shape[1]
    out, row = [], 0
    for shp in shapes:
        n = int(np.prod(shp))
        nr = _packed_rows(n, width)
        out.append(flat2d[row:row + nr].reshape(-1)[:n].reshape(shp))
        row += nr
    return out


def kernel(x, c, ctx, c_ctx, ada_w, ada_b, norm_g, ffn_w_in, ffn_w_out, mix_w_in, ssm_lam_re, ssm_lam_im, ssm_log_step, ssm_b_re, ssm_b_im, ssm_c_re, ssm_c_im, ssm_d, ssm_glu_w, ret_decay_logit, ret_w_proj, mix_w_out, loss_target, m_c_ctx, m_ada_w, m_ada_b, m_norm_g, m_ffn_w_in, m_ffn_w_out, m_mix_w_in, m_ssm_lam_re, m_ssm_lam_im, m_ssm_log_step, m_ssm_b_re, m_ssm_b_im, m_ssm_c_re, m_ssm_c_im, m_ssm_d, m_ssm_glu_w, m_ret_decay_logit, m_ret_w_proj, m_mix_w_out, v_c_ctx, v_ada_w, v_ada_b, v_norm_g, v_ffn_w_in, v_ffn_w_out, v_mix_w_in, v_ssm_lam_re, v_ssm_lam_im, v_ssm_log_step, v_ssm_b_re, v_ssm_b_im, v_ssm_c_re, v_ssm_c_im, v_ssm_d, v_ssm_glu_w, v_ret_decay_logit, v_ret_w_proj, v_mix_w_out):
    t_rows, d = x.shape[1], x.shape[2]
    n_ctx = ctx.shape[1]
    r = n_ctx + t_rows
    ssm_w = ssm_d.shape[1]
    heads = ret_decay_logit.shape[2]
    mi = mix_w_in.shape[2] * N_DEV
    dk = (mi - ssm_w - 2 * d) // (6 * heads)
    dv = 2 * dk
    qk_w, v_w = heads * dk, heads * dv
    q_off = ssm_w
    ncc = n_ctx // RET_CHUNK
    tile = n_ctx
    nct = 1
    wide_tile = _tile(n_ctx, 128, 16)
    assert r % (N_SEG * SUBLANE) == 0 and n_ctx % RET_CHUNK == 0 and t_rows % tile == 0
    me = 4 * lax.axis_index("x") + 2 * lax.axis_index("y") + lax.axis_index("c")
    g_off = ssm_w + 2 * qk_w + v_w
    gs_off = g_off + v_w

    ng_cols = norm_g.shape[2]
    small0 = _pack([c[0], norm_g[0]], d)
    small0_all = _all_gather(small0, 0, "ag_cond")

    bf = lambda w: w.astype(BF16)
    small0_all, sh_in1 = lax.optimization_barrier((small0_all, bf(ffn_w_in[0, 0])))
    small0_all = small0_all.reshape(N_DEV, -1)
    w_in1 = _all_gather(sh_in1, 1, "ag_ffn1_in", on_sequencer=True)
    w_glu = _all_gather(bf(ssm_glu_w[0]), 1, "ag_glu", on_sequencer=True)
    w_rp = _all_gather(bf(ret_w_proj[0]), 0, "ag_ret_proj", on_sequencer=True)
    w_mo = _all_gather(bf(mix_w_out[0]), 0, "ag_mix_out", on_sequencer=True)
    w_in2 = _all_gather(bf(ffn_w_in[0, 1]), 1, "ag_ffn2_in", on_sequencer=True)
    w_out2 = _all_gather(bf(ffn_w_out[0, 1]), 0, "ag_ffn2_out", on_sequencer=True)

    ng_at = _packed_rows(d, d) * d
    c_all = small0_all[:, :d]
    g_full = small0_all[:, ng_at:ng_at + 6 * ng_cols].reshape(N_DEV, 6, ng_cols).transpose(1, 0, 2).reshape(6, d)
    g6 = g_full.reshape(6, 1, d)
    cc = jnp.concatenate([c_all, c_ctx[None, :], jnp.zeros((2 * SUBLANE - N_DEV - 1, d), F32)], axis=0)
    sc = _silu_rows(cc, "ada_silu")
    na = ada_w.shape[2]
    a_loc = _mm(sc, ada_w[0], "nn", F32, "ada_fwd", tm=16, tn=na, tk=512)
    a_all = _all_gather(a_loc, 0, "ag_ada")
    a_all, sh_out1, sh_mix = lax.optimization_barrier((a_all, bf(ffn_w_out[0, 0]), bf(mix_w_in[0])))
    a_all = a_all.reshape(N_DEV, 2 * SUBLANE, na)
    w_out1 = _all_gather(sh_out1, 0, "ag_ffn1_out", on_sequencer=True)
    w_mix = _all_gather(sh_mix, 1, "ag_mix_in", on_sequencer=True)
    ada_x = lax.dynamic_index_in_dim(a_all, me, axis=1, keepdims=False).reshape(9 * d) + ada_b[0]
    ada_c = a_all[:, N_DEV, :].reshape(9 * d) + ada_b[0]
    mods = jnp.stack([ada_c.reshape(9, d), ada_x.reshape(9, d)]).reshape(18, 1, d)

    xin = jnp.concatenate([ctx[0], x[0]], axis=0)
    u1 = _ada_pre_fwd(xin, g6, mods, 0, 0, nct, tile, "pre1")
    g1, up1, a1 = _mm_swiglu(u1, w_in1, "ffn1_in", tm=544)
    o1 = _mm(a1, w_out1, "nn", F32, "ffn1_out", tm=544, tn=1024, tk=2816)
    x1 = _ada_post_fwd(xin, o1, g6, mods, 1, 0, 0.5, nct, tile, "post1")
    u2 = _ada_pre_fwd(x1, g6, mods, 2, 1, nct, tile, "pre2")
    hm = _mm(u2, w_mix, "nn", BF16, "mix_in", tm=544, tn=1024)

    us_ctx, us_lat = hm[:n_ctx, :ssm_w], hm[n_ctx:, :ssm_w]
    dskip = ssm_d.reshape(1, 1, ssm_w)
    s5_prm = (ssm_lam_re[0], ssm_lam_im[0], ssm_log_step[0], ssm_b_re[0], ssm_b_im[0], ssm_c_re[0], ssm_c_im[0])
    s5_tabs_both, s5_vjp = jax.vjp(_s5_tables, *s5_prm)
    s5_tabs, ups, y_dirs = [], [], []
    for dr in range(2):
        tabs = tuple(t[dr] for t in s5_tabs_both)
        up = _to_scan_layout(us_ctx, us_lat, dr == 1)
        yp = _s5_fwd(up, *tabs, dr == 1, "s5_fwd%d" % dr)
        s5_tabs.append(tabs)
        ups.append(up)
        y_dirs.append(_from_scan_layout(yp, n_ctx, dr == 1)[1])
    a_ssm = _ssm_out_fwd(y_dirs[0], y_dirs[1], hm, dskip, nct, tile, "ssm_out")
    gab = _mm(a_ssm, w_glu, "nn", BF16, "glu", tm=512, tn=2048, tk=ssm_w)

    cos, sin = _rope_tables(t_rows, ncc, dk)
    ret_tabs_both, ret_vjp = jax.vjp(functools.partial(_ret_tables, dk=dk, dv=dv), ret_decay_logit[0])
    ret_tabs, o_dirs, s_ins = [], [], []
    for dr in range(2):
        tabs = tuple(t[dr] for t in ret_tabs_both)
        o_d, s_in = _ret_fwd(hm, cos, sin, *tabs, heads, dk, dv, q_off, ncc, dr == 1, "ret_fwd%d" % dr)
        ret_tabs.append(tabs)
        o_dirs.append(o_d)
        s_ins.append(s_in)
    ret_in = _ret_gate_fwd(o_dirs[0], o_dirs[1], hm, g_off, heads, dv, nct, tile, "ret_gate")
    rb = _mm(ret_in, w_rp, "nn", BF16, "ret_proj", tm=512, tn=d, tk=v_w)
    merged = _merge_fwd(gab, rb, hm, gs_off, nct, tile, "merge")
    mix = _mm(merged, w_mo, "nn", F32, "mix_out", tm=512, tn=d, tk=d)
    x2 = _ada_post_fwd(x1, mix, g6, mods, 3, 1, 1.0, 0, tile, "post2", h_tile_offset=nct)
    u3 = _ada_pre_fwd(x2, g6, mods, 4, 2, 0, tile, "pre3")
    g3, up3, a3 = _mm_swiglu(u3, w_in2, "ffn2_in", tm=512)
    o3 = _mm(a3, w_out2, "nn", F32, "ffn2_out", tm=512, tn=1024, tk=2816)
    x3 = _ada_post_fwd(x2, o3, g6, mods, 5, 2, 0.5, 0, tile, "post3")
    dy, lcols = _loss_grad(x3, loss_target[0], tile, "loss")
    loss_part = (0.5 * jnp.sum(lcols) / d).reshape(1)

    dg6 = [None] * 6
    dmod = {}

    def add_mod(sel_rows, k, val):
        for sel, row in sel_rows:
            dmod[(sel, k)] = dmod.get((sel, k), 0.0) + val[row, 0]

    both, lat = [(0, 0), (1, 1)], [(1, 0)]

    def tie(*vals):
        return lax.optimization_barrier(vals)

    def big_update(w3d, m3d, v3d, layer, gfull, axis, name, filled=None):
        p, recv = _reduce_scatter(gfull, axis, "rs_" + name)
        return _adamw_scattered(w3d, m3d, v3d, layer, p, recv, "adamw_" + name, filled)

    do3, dg6[5], dgt = _ada_post_bwd(dy, o3, g6, mods, 5, 2, 0.5, 0, 1, tile, "post3_bwd")
    add_mod(lat, 8, dgt)
    gw_out2 = _mm(a3, do3, "tn", BF16, "ffn2_out_dw", tm=1408, tn=1024, tk=2176)
    do3, gw_out2 = tie(do3, gw_out2)
    up_out2 = big_update(ffn_w_out[0], m_ffn_w_out[0], v_ffn_w_out[0], 1, gw_out2, 0, "ffn2_out")
    da3 = _mm(do3, w_out2, "nt", BF16, "ffn2_out_dx", tm=512, tn=2816, tk=d)
    dh3 = _swiglu_bwd(g3, up3, da3, wide_tile, "swiglu2_bwd")
    gw_in2 = _mm(u3, dh3, "tn", BF16, "ffn2_in_dw", tm=1024, tn=1024, tk=2176)
    dh3, gw_in2 = tie(dh3, gw_in2)
    up_in2 = big_update(ffn_w_in[0], m_ffn_w_in[0], v_ffn_w_in[0], 1, gw_in2, 1, "ffn2_in")
    du3 = _mm(dh3, w_in2, "nt", F32, "ffn2_in_dx", tm=512, tn=d, tk=1024)
    dx2, dg6[4], dsh, dsc = _ada_pre_bwd(x2, du3, dy, g6, mods, 4, 2, 0, 1, tile, "pre3_bwd")
    add_mod(lat, 6, dsh)
    add_mod(lat, 7, dsc)
    dmix, dg6[3], dgt = _ada_post_bwd(dx2, mix, g6, mods, 3, 1, 1.0, 0, 1, tile, "post2_bwd")
    add_mod(lat, 5, dgt)
    gw_mo = _mm(merged, dmix, "tn", BF16, "mix_out_dw", tm=1024, tn=1024, tk=2176)
    dmix, gw_mo = tie(dmix, gw_mo)
    up_mo = big_update(mix_w_out, m_mix_w_out, v_mix_w_out, 0, gw_mo, 0, "mix_out")
    dmerged = _mm(dmix, w_mo, "nt", BF16, "mix_out_dx", tm=512, tn=d, tk=d)
    dgab, drb, dgs, dgr = _merge_bwd(gab, rb, hm, gs_off, dmerged, nct, tile, "merge_bwd")
    gw_glu = _mm(a_ssm, dgab, "tn", BF16, "glu_dw", tm=1024, tn=1024, tk=2176)
    gw_rp = _mm(ret_in, drb, "tn", BF16, "ret_proj_dw", tm=1024, tn=1024, tk=2176)
    dgab, drb, gw_glu, gw_rp = tie(dgab, drb, gw_glu, gw_rp)
    up_glu = big_update(ssm_glu_w, m_ssm_glu_w, v_ssm_glu_w, 0, gw_glu, 1, "glu")
    up_rp = big_update(ret_w_proj, m_ret_w_proj, v_ret_w_proj, 0, gw_rp, 0, "ret_proj")
    da_ssm = _mm(dgab, w_glu, "nt", F32, "glu_dx", tm=512, tn=ssm_w, tk=2 * d)
    dret_in = _mm(drb, w_rp, "nt", F32, "ret_proj_dx", tm=512, tn=v_w, tk=d)
    d_o, dg_gate = _ret_gate_bwd(o_dirs[0], o_dirs[1], hm, g_off, dret_in, heads, dv, nct, tile, "ret_gate_bwd")
    dy_ssm, dus_direct, d_dskip = _ssm_out_bwd(y_dirs[0], y_dirs[1], hm, dskip, da_ssm, nct, tile, "ssm_out_bwd")
    s5_table_grads, du_ctx, du_lat = [], [], [dus_direct]
    for dr in range(2):
        dyp = _to_scan_layout(jnp.zeros((n_ctx, ssm_w), F32), dy_ssm, dr == 1)
        if dr == 1:
            dyp, up_out2, up_in2 = tie(dyp, up_out2, up_in2)
        outs = _s5_bwd(ups[dr], dyp, *s5_tabs[dr], dr == 1, "s5_bwd%d" % dr)
        part_ctx, part_lat = _from_scan_layout(outs[0], n_ctx, dr == 1)
        du_ctx.append(part_ctx)
        du_lat.append(part_lat)
        s5_table_grads.append(outs[1:])
    dqkv, ret_table_grads = [], []
    for dr in range(2):
        if dr == 1:
            d_o, up_mo, up_glu, up_rp = tie(d_o, up_mo, up_glu, up_rp)
        outs = _ret_bwd(hm, cos, sin, *ret_tabs[dr], s_ins[dr], d_o, heads, dk, dv, q_off, ncc, dr == 1,
                        "ret_bwd%d" % dr)
        dqkv.append(outs[:3])
        ret_table_grads.append(outs[3:])
    both_dirs = lambda grads: tuple(jnp.stack([g0, g1]) for g0, g1 in zip(*grads))
    early_parts = list(s5_vjp(both_dirs(s5_table_grads))) + list(ret_vjp(both_dirs(ret_table_grads)))
    s5_names = 7
    early_shapes = [p.shape for p in early_parts]
    early_all = _all_gather(_pack(early_parts, 1024), 0, "ag_s5_grads", on_sequencer=True)
    early_sum = _sum_leading(early_all.reshape(N_DEV, -1, 1024), "sum_s5_grads")
    dus = jnp.concatenate([du_ctx[0] + du_ctx[1], du_lat[0] + du_lat[1] + du_lat[2]], axis=0)
    dhm = _assemble_dhm(dus, dqkv[0][0], dqkv[1][0], dqkv[0][1], dqkv[1][1], dqkv[0][2], dqkv[1][2],
                        dg_gate, dgs, dgr, n_ctx // wide_tile, wide_tile, "assemble_dhm")
    gw_mix = _mm(u2, dhm, "tn", BF16, "mix_in_dw", tm=1024, tn=1024, tk=2176)
    dhm, gw_mix = tie(dhm, gw_mix)
    up_mix = big_update(mix_w_in, m_mix_w_in, v_mix_w_in, 0, gw_mix, 1, "mix_in")
    du2 = _mm(dhm, w_mix, "nt", F32, "mix_in_dx", tm=544, tn=d, tk=1024)
    dx1, dg6[2], dsh, dsc = _ada_pre_bwd(x1, du2, dx2, g6, mods, 2, 1, nct, 2, tile, "pre2_bwd", dres_x_only=True)
    add_mod(both, 3, dsh)
    add_mod(both, 4, dsc)
    do1, dg6[1], dgt = _ada_post_bwd(dx1, o1, g6, mods, 1, 0, 0.5, nct, 2, tile, "post1_bwd")
    add_mod(both, 2, dgt)
    gw_out1 = _mm(a1, do1, "tn", BF16, "ffn1_out_dw", tm=1408, tn=1024, tk=2176)
    do1, gw_out1 = tie(do1, gw_out1)
    up_out1 = big_update(ffn_w_out[0], m_ffn_w_out[0], v_ffn_w_out[0], 0, gw_out1, 0, "ffn1_out", filled=up_out2)
    da1 = _mm(do1, w_out1, "nt", BF16, "ffn1_out_dx", tm=544, tn=2816, tk=d)
    dh1 = _swiglu_bwd(g1, up1, da1, wide_tile, "swiglu1_bwd")
    dh1, up_mix, early_sum = tie(dh1, up_mix, early_sum)
    early_sums = _unpack(early_sum, early_shapes)
    gw_in1 = _mm(u1, dh1, "tn", BF16, "ffn1_in_dw", tm=1024, tn=1024, tk=2176)
    dh1, gw_in1 = tie(dh1, gw_in1)
    up_in1 = big_update(ffn_w_in[0], m_ffn_w_in[0], v_ffn_w_in[0], 0, gw_in1, 1, "ffn1_in", filled=up_in2)
    du1 = _mm(dh1, w_in1, "nt", F32, "ffn1_in_dx", tm=544, tn=d, tk=1024)
    dx_lat, dg6[0], dsh, dsc = _ada_pre_bwd(xin, du1, dx1, g6, mods, 0, 0, nct, 2, tile, "pre1_bwd",
                                            latent_dh_only=True)
    add_mod(both, 0, dsh)
    add_mod(both, 1, dsc)
    grad_x = dx_lat[None]

    zero_d = jnp.zeros((d,), F32)
    d_ada_x = jnp.stack([dmod.get((1, k), zero_d) for k in range(9)]).reshape(9 * d)
    d_ada_c = jnp.stack([dmod.get((0, k), zero_d) for k in range(9)]).reshape(9 * d)
    dg_full = jnp.stack([g[0, 0] for g in dg6])
    small_parts = [d_ada_x, d_ada_c, dg_full, d_dskip, loss_part]
    small_shapes = [p.shape for p in small_parts]
    packed = _pack(small_parts, 1024)
    gathered = _all_gather(packed, 0, "ag_small_grads").reshape(N_DEV, -1, 1024)
    summed = _sum_leading(gathered, "sum_small_grads")
    sums = _unpack(summed, small_shapes)
    sum_dx, sum_dc, sum_dg = sums[0], sums[1], sums[2]
    loss = sums[4][0]
    grad_ada_b = (sum_dx + sum_dc)[None]
    dx_rows = gathered.reshape(N_DEV, -1)[:, :9 * d]
    col0 = me * na
    da_rows = jnp.concatenate([lax.dynamic_slice_in_dim(dx_rows, col0, na, axis=1),
                               lax.dynamic_slice_in_dim(sum_dc[None], col0, na, axis=1),
                               jnp.zeros((2 * SUBLANE - N_DEV - 1, na), F32)], axis=0)
    grad_ada_w = _mm(sc, da_rows, "tn", F32, "ada_dw", tm=512, tn=na, tk=16)
    d_sc = _mm(da_rows, ada_w[0], "nt", F32, "ada_dx", tm=16, tn=512, tk=na)
    d_sc_all = _all_gather(jnp.broadcast_to(d_sc[N_DEV:N_DEV + 1], (SUBLANE, d)), 0, "ag_dctx")
    d_sc_sum = _sum_leading(d_sc_all.reshape(N_DEV, SUBLANE, d), "sum_dctx")
    grad_c_ctx = _silu_grad_rows(jnp.broadcast_to(c_ctx[None], (SUBLANE, d)), d_sc_sum, "ctx_silu_bwd")[0]
    grad_norm_g = lax.dynamic_slice_in_dim(sum_dg, me * ng_cols, ng_cols, axis=1)[None]

    upd = {}
    upd["ffn_w_in"] = [o[None] for o in up_in1]
    upd["ffn_w_out"] = [o[None] for o in up_out1]
    upd["mix_w_in"] = list(up_mix)
    upd["ssm_glu_w"] = list(up_glu)
    upd["ret_w_proj"] = list(up_rp)
    upd["mix_w_out"] = list(up_mo)
    upd["ada_w"] = [o[None] for o in _adamw(ada_w[0], m_ada_w[0], v_ada_w[0], grad_ada_w[None], "adamw_ada_w")]

    small_names = ["c_ctx", "ada_b", "norm_g", "ssm_lam_re", "ssm_lam_im", "ssm_log_step", "ssm_b_re", "ssm_b_im",
                   "ssm_c_re", "ssm_c_im", "ssm_d", "ret_decay_logit"]
    small_w = [c_ctx, ada_b, norm_g, ssm_lam_re, ssm_lam_im, ssm_log_step, ssm_b_re, ssm_b_im, ssm_c_re, ssm_c_im,
               ssm_d, ret_decay_logit]
    small_m = [m_c_ctx, m_ada_b, m_norm_g, m_ssm_lam_re, m_ssm_lam_im, m_ssm_log_step, m_ssm_b_re, m_ssm_b_im,
               m_ssm_c_re, m_ssm_c_im, m_ssm_d, m_ret_decay_logit]
    small_v = [v_c_ctx, v_ada_b, v_norm_g, v_ssm_lam_re, v_ssm_lam_im, v_ssm_log_step, v_ssm_b_re, v_ssm_b_im,
               v_ssm_c_re, v_ssm_c_im, v_ssm_d, v_ret_decay_logit]
    small_g = [grad_c_ctx, grad_ada_b, grad_norm_g] + [s[None] for s in early_sums[:s5_names]] + \
              [sums[3].reshape(ssm_d.shape), early_sums[s5_names][None]]
    shapes = [w.shape for w in small_w]
    res = _adamw(_pack(small_w, 1024), _pack(small_m, 1024), _pack(small_v, 1024), _pack(small_g, 1024)[None],
                 "adamw_small")
    small_out = [_unpack(o, shapes) for o in res]
    for i, nm in enumerate(small_names):
        upd[nm] = [small_out[kind][i] for kind in range(4)]

    order = ["c_ctx", "ada_w", "ada_b", "norm_g", "ffn_w_in", "ffn_w_out", "mix_w_in", "ssm_lam_re", "ssm_lam_im",
             "ssm_log_step", "ssm_b_re", "ssm_b_im", "ssm_c_re", "ssm_c_im", "ssm_d", "ssm_glu_w", "ret_decay_logit",
             "ret_w_proj", "mix_w_out"]
    outs = [loss, grad_x]
    for kind in range(4):
        outs += [upd[nm][kind] for nm in order]
    return tuple(outs)
```

```python
import functools
import math

import jax
import jax.numpy as jnp
import numpy as np
from jax import lax
from jax.experimental import pallas as pl
from jax.experimental.pallas import tpu as pltpu
from jax.experimental.pallas import tpu_sc as plsc

F32 = jnp.float32
BF16 = jnp.bfloat16
MXU_DTYPE = jnp.bfloat16
MESH_AXES = ("x", "y", "c")
N_DEV = 8
V7X_VMEM_LIMIT_BYTES = 56 * 1024 * 1024
LANE = 128
SUBLANE = 8

GRID_W = 64
RET_CHUNK = 128
ROPE_BASE = 10000.0
NORM_EPS = 1e-6
ADAM_LR = 0.001
ADAM_B1 = 0.9
ADAM_B2 = 0.999
ADAM_EPS = 1e-08
ADAM_WD = 0.01
ADAM_STEP = 10
SSM_TILE_GROUPS = 8
SSM_HALF_GROUPS = 4
N_SEG = 16


def _params(sem=None):
    return pltpu.CompilerParams(dimension_semantics=sem, vmem_limit_bytes=V7X_VMEM_LIMIT_BYTES)


def _tile(n, target, mult):
    best = None
    t = mult
    while t <= min(n, target):
        if n % t == 0:
            best = t
        t += mult
    return n if best is None else best


def _sds(shape, dtype):
    return jax.ShapeDtypeStruct(tuple(shape), dtype)


def _mm(a, b, dims, out_dtype, name, tm=512, tn=1408, tk=2048):
    if dims == "nn":
        (m, k), (k2, n) = a.shape, b.shape
    elif dims == "nt":
        (m, k), (n, k2) = a.shape, b.shape
    else:
        (k, m), (k2, n) = a.shape, b.shape
    assert k == k2, (a.shape, b.shape, dims)
    tm = _tile(m, tm, 16)
    tn = _tile(n, tn, LANE)
    tk = _tile(k, tk, LANE if dims != "tn" else 16)
    nk = k // tk
    dn = {"nn": (((1,), (0,)), ((), ())), "nt": (((1,), (1,)), ((), ())), "tn": (((0,), (0,)), ((), ()))}[dims]

    def product(a_ref, b_ref):
        return lax.dot_general(a_ref[...].astype(MXU_DTYPE), b_ref[...].astype(MXU_DTYPE), dn,
                               preferred_element_type=F32)

    def body_single(a_ref, b_ref, o_ref):
        o_ref[...] = product(a_ref, b_ref).astype(o_ref.dtype)

    def body(a_ref, b_ref, o_ref, acc_ref):
        kk = pl.program_id(2)

        @pl.when(kk == 0)
        def _():
            acc_ref[...] = product(a_ref, b_ref)

        @pl.when((kk > 0) & (kk < nk - 1))
        def _():
            acc_ref[...] += product(a_ref, b_ref)

        @pl.when(kk == nk - 1)
        def _():
            o_ref[...] = (acc_ref[...] + product(a_ref, b_ref)).astype(o_ref.dtype)

    if dims == "nn":
        a_spec = pl.BlockSpec((tm, tk), lambda j, i, kk: (i, kk))
        b_spec = pl.BlockSpec((tk, tn), lambda j, i, kk: (kk, j))
    elif dims == "nt":
        a_spec = pl.BlockSpec((tm, tk), lambda j, i, kk: (i, kk))
        b_spec = pl.BlockSpec((tn, tk), lambda j, i, kk: (j, kk))
    else:
        a_spec = pl.BlockSpec((tk, tm), lambda j, i, kk: (kk, i))
        b_spec = pl.BlockSpec((tk, tn), lambda j, i, kk: (kk, j))
    return pl.pallas_call(
        body_single if nk == 1 else body, name=name, grid=(n // tn, m // tm, nk), in_specs=[a_spec, b_spec],
        out_specs=pl.BlockSpec((tm, tn), lambda j, i, kk: (i, j)), out_shape=_sds((m, n), out_dtype),
        scratch_shapes=[] if nk == 1 else [pltpu.VMEM((tm, tn), F32)],
        compiler_params=_params(("parallel", "parallel", "arbitrary")))(a, b)


def _mm_swiglu(a, b, name, tm=512, tn=512):
    m, k = a.shape
    k2, f2 = b.shape
    f = f2 // 2
    assert k == k2
    tm = _tile(m, tm, 16)
    tn = _tile(f, tn, 2 * LANE)
    nj = f // tn

    def body(a_ref, bg_ref, bu_ref, g_ref, u_ref, act_ref):
        av = a_ref[...].astype(MXU_DTYPE)
        gate = jnp.dot(av, bg_ref[...].astype(MXU_DTYPE), preferred_element_type=F32)
        up = jnp.dot(av, bu_ref[...].astype(MXU_DTYPE), preferred_element_type=F32)
        g_ref[...] = gate.astype(g_ref.dtype)
        u_ref[...] = up.astype(u_ref.dtype)
        act_ref[...] = (gate * _sigmoid(gate) * up).astype(act_ref.dtype)

    tile = pl.BlockSpec((tm, tn), lambda j, i: (i, j))
    out = _sds((m, f), BF16)
    return pl.pallas_call(
        body, name=name, grid=(nj, m // tm),
        in_specs=[pl.BlockSpec((tm, k), lambda j, i: (i, 0)), pl.BlockSpec((k, tn), lambda j, i: (0, j)),
                  pl.BlockSpec((k, tn), lambda j, i: (0, j + nj))],
        out_specs=[tile, tile, tile], out_shape=[out, out, out],
        compiler_params=_params(("parallel", "parallel")))(a, b, b)


def _rows(name, body, n_tiles, ins, outs):
    in_specs = [pl.BlockSpec(blk, imap) for (_, blk, imap) in ins]
    out_specs = [pl.BlockSpec(blk, imap) for (_, _, blk, imap) in outs]
    out_shape = [_sds(shape, dt) for (shape, dt, _, _) in outs]
    res = pl.pallas_call(body, name=name, grid=(n_tiles,), in_specs=in_specs, out_specs=out_specs,
                         out_shape=out_shape, compiler_params=_params(("arbitrary",)))(*[a for (a, _, _) in ins])
    return res


def _row_in(arr, tile, width=None, col=0, x_only_offset=None):
    width = arr.shape[1] if width is None else width
    if x_only_offset is None:
        return (arr, (tile, width), lambda i: (i, col))
    return (arr, (tile, width), lambda i: (jnp.maximum(i - x_only_offset, 0), col))


def _vec_in(arr, idx_fn):
    return (arr, (1, 1, arr.shape[2]), lambda i: (idx_fn(i), 0, 0))


def _rms(h):
    return lax.rsqrt(jnp.mean(h * h, axis=-1, keepdims=True) + NORM_EPS)


def _sigmoid(z):
    return 1.0 / (1.0 + jnp.exp(-z))


def _ada_pre_fwd(h, g6, mods, gi, mi, nct, tile, name):
    r, d = h.shape
    sel = lambda i: jnp.where(i >= nct, 1, 0)

    def body(h_ref, g_ref, sh_ref, sc_ref, u_ref):
        hh = h_ref[...]
        n = hh * _rms(hh) * g_ref[0]
        u_ref[...] = (n * (1.0 + sc_ref[0]) + sh_ref[0]).astype(u_ref.dtype)

    (u,) = _rows(name, body, r // tile,
                 [_row_in(h, tile), _vec_in(g6, lambda i: gi), _vec_in(mods, lambda i: sel(i) * 9 + 3 * mi),
                  _vec_in(mods, lambda i: sel(i) * 9 + 3 * mi + 1)],
                 [((r, d), BF16, (tile, d), lambda i: (i, 0))])
    return u


def _ada_pre_bwd(h, du, dres, g6, mods, gi, mi, nct, nsel, tile, name, dres_x_only=False, latent_dh_only=False):
    r, d = h.shape
    dh_rows = r - nct * tile if latent_dh_only else r
    dh_map = (lambda i: (jnp.maximum(i - nct, 0), 0)) if latent_dh_only else (lambda i: (i, 0))
    sel = lambda i: jnp.where(i >= nct, 1, 0) if nsel == 2 else 0
    msel = lambda i: jnp.where(i >= nct, 1, 0)
    off = nct if dres_x_only else None

    def body(h_ref, du_ref, dr_ref, g_ref, sc_ref, dh_ref, dg_ref, dsh_ref, dsc_ref):
        i = pl.program_id(0)
        hh = h_ref[...]
        rr = _rms(hh)
        g = g_ref[0]
        hn = hh * rr
        n = hn * g
        du_ = du_ref[...].astype(F32)
        dn = du_ * (1.0 + sc_ref[0])

        @pl.when(i == 0)
        def _():
            dg_ref[...] = jnp.zeros_like(dg_ref)

        @pl.when((i == 0) | (i == nct))
        def _():
            dsh_ref[...] = jnp.zeros_like(dsh_ref)
            dsc_ref[...] = jnp.zeros_like(dsc_ref)

        dg_ref[0] += jnp.sum(dn * hn, axis=0, keepdims=True)
        dsh_ref[0] += jnp.sum(du_, axis=0, keepdims=True)
        dsc_ref[0] += jnp.sum(du_ * n, axis=0, keepdims=True)
        t = dn * g
        dh = rr * t - hn * (rr * jnp.mean(t * hn, axis=-1, keepdims=True))
        if dres_x_only:
            dh_ref[...] = dh + jnp.where(i >= nct, dr_ref[...], 0.0)
        else:
            dh_ref[...] = dh + dr_ref[...]

    dh, dg, dsh, dsc = _rows(
        name, body, r // tile,
        [_row_in(h, tile), _row_in(du, tile), _row_in(dres, tile, x_only_offset=off), _vec_in(g6, lambda i: gi),
         _vec_in(mods, lambda i: msel(i) * 9 + 3 * mi + 1)],
        [((dh_rows, d), F32, (tile, d), dh_map), ((1, 1, d), F32, (1, 1, d), lambda i: (0, 0, 0)),
         ((nsel, 1, d), F32, (1, 1, d), lambda i: (sel(i), 0, 0)),
         ((nsel, 1, d), F32, (1, 1, d), lambda i: (sel(i), 0, 0))])
    return dh, dg, dsh, dsc


def _ada_post_fwd(h, o, g6, mods, gi, mi, res_w, nct, tile, name, h_tile_offset=0):
    r, d = o.shape
    sel = lambda i: jnp.where(i >= nct, 1, 0)

    def body(h_ref, o_ref, g_ref, gt_ref, y_ref):
        oo = o_ref[...].astype(F32)
        n = oo * _rms(oo) * g_ref[0]
        y_ref[...] = h_ref[...] + res_w * gt_ref[0] * n

    (y,) = _rows(name, body, r // tile,
                 [(h, (tile, d), lambda i: (i + h_tile_offset, 0)), _row_in(o, tile), _vec_in(g6, lambda i: gi),
                  _vec_in(mods, lambda i: sel(i) * 9 + 3 * mi + 2)],
                 [((r, d), F32, (tile, d), lambda i: (i, 0))])
    return y


def _ada_post_bwd(dy, o, g6, mods, gi, mi, res_w, nct, nsel, tile, name):
    r, d = o.shape
    sel = lambda i: jnp.where(i >= nct, 1, 0) if nsel == 2 else 0
    msel = lambda i: jnp.where(i >= nct, 1, 0)

    def body(dy_ref, o_ref, g_ref, gt_ref, do_ref, dg_ref, dgt_ref):
        i = pl.program_id(0)
        oo = o_ref[...].astype(F32)
        rr = _rms(oo)
        g = g_ref[0]
        on = oo * rr
        dy_ = dy_ref[...] * res_w

        @pl.when(i == 0)
        def _():
            dg_ref[...] = jnp.zeros_like(dg_ref)

        @pl.when((i == 0) | (i == nct))
        def _():
            dgt_ref[...] = jnp.zeros_like(dgt_ref)

        dgt_ref[0] += jnp.sum(dy_ * (on * g), axis=0, keepdims=True)
        dn = dy_ * gt_ref[0]
        dg_ref[0] += jnp.sum(dn * on, axis=0, keepdims=True)
        t = dn * g
        do_ref[...] = (rr * t - on * (rr * jnp.mean(t * on, axis=-1, keepdims=True))).astype(do_ref.dtype)

    do, dg, dgt = _rows(
        name, body, r // tile,
        [_row_in(dy, tile), _row_in(o, tile), _vec_in(g6, lambda i: gi),
         _vec_in(mods, lambda i: msel(i) * 9 + 3 * mi + 2)],
        [((r, d), BF16, (tile, d), lambda i: (i, 0)), ((1, 1, d), F32, (1, 1, d), lambda i: (0, 0, 0)),
         ((nsel, 1, d), F32, (1, 1, d), lambda i: (sel(i), 0, 0))])
    return do, dg, dgt


def _swiglu_bwd(gate, up, da, tile, name):
    r, f = gate.shape

    def body(g_ref, u_ref, da_ref, dh_ref):
        gt = g_ref[...].astype(F32)
        d = da_ref[...].astype(F32)
        sg = _sigmoid(gt)
        dh_ref[:, :f] = (d * u_ref[...].astype(F32) * (sg * (1.0 + gt * (1.0 - sg)))).astype(dh_ref.dtype)
        dh_ref[:, f:] = (d * gt * sg).astype(dh_ref.dtype)

    (dh,) = _rows(name, body, r // tile, [_row_in(gate, tile), _row_in(up, tile), _row_in(da, tile)],
                  [((r, 2 * f), BF16, (tile, 2 * f), lambda i: (i, 0))])
    return dh


def _gelu_parts(y):
    c0 = math.sqrt(2.0 / math.pi)
    inner = c0 * (y + 0.044715 * y * y * y)
    th = jnp.tanh(inner)
    return th, c0 * (1.0 + 3 * 0.044715 * y * y)


def _ssm_out_fwd(y0, y1, hm, dskip, nct, tile, name):
    t_rows, s = y0.shape

    def body(y0_ref, y1_ref, u_ref, d_ref, a_ref):
        y = y0_ref[...] + y1_ref[...] + d_ref[0] * u_ref[...].astype(F32)
        th, _ = _gelu_parts(y)
        a_ref[...] = (0.5 * y * (1.0 + th)).astype(a_ref.dtype)

    (a,) = _rows(name, body, t_rows // tile,
                 [_row_in(y0, tile), _row_in(y1, tile), (hm, (tile, s), lambda i: (i + nct, 0)),
                  _vec_in(dskip, lambda i: 0)],
                 [((t_rows, s), BF16, (tile, s), lambda i: (i, 0))])
    return a


def _ssm_out_bwd(y0, y1, hm, dskip, da, nct, tile, name):
    t_rows, s = y0.shape

    def body(y0_ref, y1_ref, u_ref, d_ref, da_ref, dy_ref, du_ref, dd_ref):
        i = pl.program_id(0)
        u = u_ref[...].astype(F32)
        y = y0_ref[...] + y1_ref[...] + d_ref[0] * u
        th, dinner = _gelu_parts(y)
        dy = da_ref[...].astype(F32) * (0.5 * (1.0 + th) + 0.5 * y * (1.0 - th * th) * dinner)
        dy_ref[...] = dy
        du_ref[...] = dy * d_ref[0]

        @pl.when(i == 0)
        def _():
            dd_ref[...] = jnp.zeros_like(dd_ref)

        dd_ref[0] += jnp.sum(dy * u, axis=0, keepdims=True)

    dy, du, dd = _rows(name, body, t_rows // tile,
                       [_row_in(y0, tile), _row_in(y1, tile), (hm, (tile, s), lambda i: (i + nct, 0)),
                        _vec_in(dskip, lambda i: 0), _row_in(da, tile)],
                       [((t_rows, s), F32, (tile, s), lambda i: (i, 0)), ((t_rows, s), F32, (tile, s), lambda i: (i, 0)),
                        ((1, 1, s), F32, (1, 1, s), lambda i: (0, 0, 0))])
    return dy, du, dd


def _col_pieces(arr, off, width, tile, nct, unit=None):
    pw = math.gcd(off, width if unit is None else unit)
    specs = [(arr, (tile, pw), functools.partial(lambda i, cb: (i + nct, cb), cb=off // pw + p))
             for p in range(width // pw)]
    return specs, pw


def _ret_gate_fwd(o0, o1, hm, g_off, heads, dv, nct, tile, name):
    t_rows, w = o0.shape
    g_specs, pw = _col_pieces(hm, g_off, w, tile, nct)
    ng = len(g_specs)

    def body(o0_ref, o1_ref, *refs):
        g_refs, r_ref = refs[:ng], refs[ng]
        for hd in range(heads):
            cs = slice(hd * dv, (hd + 1) * dv)
            o = o0_ref[:, cs] + o1_ref[:, cs]
            lo = (hd * dv) % pw
            g = g_refs[(hd * dv) // pw][:, lo:lo + dv].astype(F32)
            r_ref[:, cs] = (g * _sigmoid(g) * (o * _rms(o))).astype(r_ref.dtype)

    (ri,) = _rows(name, body, t_rows // tile, [_row_in(o0, tile), _row_in(o1, tile)] + g_specs,
                  [((t_rows, w), BF16, (tile, w), lambda i: (i, 0))])
    return ri


def _ret_gate_bwd(o0, o1, hm, g_off, dri, heads, dv, nct, tile, name):
    t_rows, w = o0.shape
    g_specs, pw = _col_pieces(hm, g_off, w, tile, nct)
    ng = len(g_specs)

    def body(o0_ref, o1_ref, d_ref, *refs):
        g_refs, do_ref, dg_ref = refs[:ng], refs[ng], refs[ng + 1]
        for hd in range(heads):
            cs = slice(hd * dv, (hd + 1) * dv)
            o = o0_ref[:, cs] + o1_ref[:, cs]
            lo = (hd * dv) % pw
            g = g_refs[(hd * dv) // pw][:, lo:lo + dv].astype(F32)
            d = d_ref[:, cs].astype(F32)
            rr = _rms(o)
            on = o * rr
            sg = _sigmoid(g)
            dg_ref[:, cs] = (d * on * (sg * (1.0 + g * (1.0 - sg)))).astype(dg_ref.dtype)
            t = d * (g * sg)
            do_ref[:, cs] = rr * t - on * (rr * jnp.mean(t * on, axis=-1, keepdims=True))

    do, dg = _rows(name, body, t_rows // tile, [_row_in(o0, tile), _row_in(o1, tile), _row_in(dri, tile)] + g_specs,
                   [((t_rows, w), F32, (tile, w), lambda i: (i, 0)), ((t_rows, w), BF16, (tile, w), lambda i: (i, 0))])
    return do, dg


def _merge_fwd(gab, rb, hm, gs_off, nct, tile, name):
    t_rows, d = rb.shape
    specs, pw = _col_pieces(hm, gs_off, 2 * d, tile, nct, unit=d)
    npc = d // pw

    def body(gab_ref, rb_ref, *refs):
        gs_refs, gr_refs, m_ref = refs[:npc], refs[npc:2 * npc], refs[2 * npc]
        for p in range(npc):
            cs = slice(p * pw, (p + 1) * pw)
            ga = gab_ref[:, cs].astype(F32)
            gb = gab_ref[:, d + p * pw:d + (p + 1) * pw].astype(F32)
            m_ref[:, cs] = (_sigmoid(gs_refs[p][...].astype(F32)) * (ga * _sigmoid(gb))
                            + _sigmoid(gr_refs[p][...].astype(F32)) * rb_ref[:, cs].astype(F32)).astype(m_ref.dtype)

    (mg,) = _rows(name, body, t_rows // tile, [_row_in(gab, tile), _row_in(rb, tile)] + specs,
                  [((t_rows, d), BF16, (tile, d), lambda i: (i, 0))])
    return mg


def _merge_bwd(gab, rb, hm, gs_off, dm, nct, tile, name):
    t_rows, d = rb.shape
    specs, pw = _col_pieces(hm, gs_off, 2 * d, tile, nct, unit=d)
    npc = d // pw

    def body(gab_ref, rb_ref, dm_ref, *refs):
        gs_refs, gr_refs = refs[:npc], refs[npc:2 * npc]
        dgab_ref, drb_ref, dgs_ref, dgr_ref = refs[2 * npc:]
        for p in range(npc):
            cs = slice(p * pw, (p + 1) * pw)
            cs2 = slice(d + p * pw, d + (p + 1) * pw)
            ga = gab_ref[:, cs].astype(F32)
            gb = gab_ref[:, cs2].astype(F32)
            dmm = dm_ref[:, cs].astype(F32)
            ss = _sigmoid(gs_refs[p][...].astype(F32))
            sr = _sigmoid(gr_refs[p][...].astype(F32))
            sb = _sigmoid(gb)
            dbr = dmm * ss
            dgab_ref[:, cs] = (dbr * sb).astype(dgab_ref.dtype)
            dgab_ref[:, cs2] = (dbr * ga * sb * (1.0 - sb)).astype(dgab_ref.dtype)
            drb_ref[:, cs] = (dmm * sr).astype(drb_ref.dtype)
            dgs_ref[:, cs] = (dmm * (ga * sb) * ss * (1.0 - ss)).astype(dgs_ref.dtype)
            dgr_ref[:, cs] = (dmm * rb_ref[:, cs].astype(F32) * sr * (1.0 - sr)).astype(dgr_ref.dtype)

    return _rows(name, body, t_rows // tile, [_row_in(gab, tile), _row_in(rb, tile), _row_in(dm, tile)] + specs,
                 [((t_rows, 2 * d), BF16, (tile, 2 * d), lambda i: (i, 0)), ((t_rows, d), BF16, (tile, d), lambda i: (i, 0)),
                  ((t_rows, d), BF16, (tile, d), lambda i: (i, 0)), ((t_rows, d), BF16, (tile, d), lambda i: (i, 0))])


def _assemble_dhm(dus, dq0, dq1, dk0, dk1, dv0, dv1, dg, dgs, dgr, nct, tile, name):
    r, s = dus.shape
    qk = dq0.shape[1]
    vw = dv0.shape[1]
    d = dgs.shape[1]
    mi = s + 2 * qk + 2 * vw + 2 * d
    c_q, c_k, c_v, c_g, c_gs, c_gr = s, s + qk, s + 2 * qk, s + 2 * qk + vw, s + 2 * qk + 2 * vw, s + 2 * qk + 2 * vw + d

    def body(dus_ref, dq0_ref, dq1_ref, dk0_ref, dk1_ref, dv0_ref, dv1_ref, dg_ref, dgs_ref, dgr_ref, o_ref):
        i = pl.program_id(0)
        lat = i >= nct
        o_ref[:, :s] = dus_ref[...].astype(o_ref.dtype)
        o_ref[:, c_q:c_k] = (dq0_ref[...].astype(F32) + dq1_ref[...].astype(F32)).astype(o_ref.dtype)
        o_ref[:, c_k:c_v] = (dk0_ref[...].astype(F32) + dk1_ref[...].astype(F32)).astype(o_ref.dtype)
        o_ref[:, c_v:c_g] = (dv0_ref[...].astype(F32) + dv1_ref[...].astype(F32)).astype(o_ref.dtype)
        o_ref[:, c_g:c_gs] = jnp.where(lat, dg_ref[...], 0.0).astype(o_ref.dtype)
        o_ref[:, c_gs:c_gr] = jnp.where(lat, dgs_ref[...], 0.0).astype(o_ref.dtype)
        o_ref[:, c_gr:] = jnp.where(lat, dgr_ref[...], 0.0).astype(o_ref.dtype)

    (out,) = _rows(name, body, r // tile,
                   [_row_in(dus, tile), _row_in(dq0, tile), _row_in(dq1, tile), _row_in(dk0, tile), _row_in(dk1, tile),
                    _row_in(dv0, tile), _row_in(dv1, tile), _row_in(dg, tile, x_only_offset=nct),
                    _row_in(dgs, tile, x_only_offset=nct), _row_in(dgr, tile, x_only_offset=nct)],
                   [((r, mi), BF16, (tile, mi), lambda i: (i, 0))])
    return out


def _loss_grad(y, target, tile, name):
    t_rows, d = y.shape

    def body(y_ref, t_ref, dy_ref, l_ref):
        i = pl.program_id(0)
        e = y_ref[...] - t_ref[...]
        dy_ref[...] = e * (1.0 / d)

        @pl.when(i == 0)
        def _():
            l_ref[...] = jnp.zeros_like(l_ref)

        l_ref[0] += jnp.sum(e * e, axis=0, keepdims=True)

    return _rows(name, body, t_rows // tile, [_row_in(y, tile), _row_in(target, tile)],
                 [((t_rows, d), F32, (tile, d), lambda i: (i, 0)), ((1, 1, d), F32, (1, 1, d), lambda i: (0, 0, 0))])


def _silu_rows(v, name):
    def body(v_ref, o_ref):
        z = v_ref[...]
        o_ref[...] = z * _sigmoid(z)

    (o,) = _rows(name, body, 1, [_row_in(v, v.shape[0])], [(v.shape, F32, v.shape, lambda i: (0, 0))])
    return o


def _silu_grad_rows(v, dv, name):
    def body(v_ref, d_ref, o_ref):
        z = v_ref[...]
        sg = _sigmoid(z)
        o_ref[...] = d_ref[...] * (sg * (1.0 + z * (1.0 - sg)))

    (o,) = _rows(name, body, 1, [_row_in(v, v.shape[0]), _row_in(dv, v.shape[0])],
                 [(v.shape, F32, v.shape, lambda i: (0, 0))])
    return o


def _sum_leading(g8, name):
    n, r, c = g8.shape
    tile = _tile(r, 256, SUBLANE)

    def body(g_ref, o_ref):
        acc = g_ref[0]
        for j in range(1, n):
            acc = acc + g_ref[j]
        o_ref[...] = acc

    (o,) = _rows(name, body, r // tile, [(g8, (n, tile, c), lambda i: (0, i, 0))],
                 [((r, c), F32, (tile, c), lambda i: (i, 0))])
    return o


def _pair_sum(g, recv, axis, name):
    n, br, bc = recv.shape
    tile = _tile(br, 256, 16)
    nrt = br // tile
    core = lax.axis_index("c").astype(jnp.int32).reshape(1)

    def body(c_ref, g_ref, r_ref, o_ref):
        o_ref[0] = (g_ref[...].astype(F32) + r_ref[0].astype(F32)).astype(o_ref.dtype)

    if axis == 1:
        g_spec = pl.BlockSpec((tile, bc), lambda q, i, c_ref: (i, 2 * q + c_ref[0]))
    else:
        g_spec = pl.BlockSpec((tile, bc), lambda q, i, c_ref: ((2 * q + c_ref[0]) * nrt + i, 0))
    slot = pl.BlockSpec((1, tile, bc), lambda q, i, c_ref: (q, i, 0))
    return pl.pallas_call(
        body, name=name, out_shape=_sds((n, br, bc), recv.dtype),
        grid_spec=pltpu.PrefetchScalarGridSpec(num_scalar_prefetch=1, grid=(n, nrt), in_specs=[g_spec, slot],
                                               out_specs=slot),
        compiler_params=_params(("arbitrary", "arbitrary")))(core, g, recv)


def _adam_math(w, m, v, g):
    c1 = 1.0 / (1.0 - ADAM_B1 ** ADAM_STEP)
    c2 = 1.0 / (1.0 - ADAM_B2 ** ADAM_STEP)
    mm = ADAM_B1 * m + (1.0 - ADAM_B1) * g
    vv = ADAM_B2 * v + (1.0 - ADAM_B2) * (g * g)
    return -ADAM_LR * ((mm * c1) / (jnp.sqrt(vv * c2) + ADAM_EPS) + ADAM_WD * w), mm, vv


def _adamw(w, m, v, gparts, name):
    r, c = w.shape
    n = gparts.shape[0]
    tile = _tile(r, 256, 16)

    def body(w_ref, m_ref, v_ref, g_ref, go_ref, d_ref, mo_ref, vo_ref):
        g = g_ref[0].astype(F32)
        for j in range(1, n):
            g = g + g_ref[j].astype(F32)
        go_ref[...] = g
        d_ref[...], mo_ref[...], vo_ref[...] = _adam_math(w_ref[...], m_ref[...], v_ref[...], g)

    rs = lambda arr: _row_in(arr, tile)
    out = ((r, c), F32, (tile, c), lambda i: (i, 0))
    return _rows(name, body, r // tile, [rs(w), rs(m), rs(v), (gparts, (n, tile, c), lambda i: (0, i, 0))],
                 [out, out, out, out])


def _adamw_scattered(w, m, v, layer, p, recv, name, filled=None):
    nl, r, c = w.shape
    n = recv.shape[0]
    tile = _tile(r, 256, 16)
    chip = (2 * lax.axis_index("x") + lax.axis_index("y")).astype(jnp.int32).reshape(1)
    n_prev = 0 if filled is None else len(filled)

    def body(q_ref, w_ref, m_ref, v_ref, p_ref, g_ref, *rest):
        go_ref, d_ref, mo_ref, vo_ref = rest[n_prev:]
        g = p_ref[0].astype(F32)
        for j in range(n):
            g = g + g_ref[j].astype(F32)
        go_ref[0] = g
        d_ref[0], mo_ref[0], vo_ref[0] = _adam_math(w_ref[0], m_ref[0], v_ref[0], g)

    slab = pl.BlockSpec((1, tile, c), lambda i, q_ref: (layer, i, 0))
    anywhere = pl.BlockSpec(memory_space=pl.ANY)
    out = _sds((nl, r, c), F32)
    prev = [] if filled is None else list(filled)
    return pl.pallas_call(
        body, name=name, out_shape=[out, out, out, out],
        grid_spec=pltpu.PrefetchScalarGridSpec(
            num_scalar_prefetch=1, grid=(r // tile,),
            in_specs=[slab, slab, slab, pl.BlockSpec((1, tile, c), lambda i, q_ref: (q_ref[0], i, 0)),
                      pl.BlockSpec((n, tile, c), lambda i, q_ref: (0, i, 0))] + [anywhere] * n_prev,
            out_specs=[slab, slab, slab, slab]),
        input_output_aliases={6 + j: j for j in range(n_prev)},
        compiler_params=_params(("arbitrary",)))(chip, w, m, v, p, recv, *prev)


def _cmul(ar, ai, br, bi):
    return ar * br - ai * bi, ar * bi + ai * br


def _cpow(ar, ai, n):
    pr, pi = jnp.ones_like(ar), jnp.zeros_like(ar)
    br, bi = ar, ai
    while n:
        if n & 1:
            pr, pi = _cmul(pr, pi, br, bi)
        n >>= 1
        if n:
            br, bi = _cmul(br, bi, br, bi)
    return pr, pi


def _s5_scan_into(x_ref, ar1, ai1, ns, fin_ref, hin_ref, reverse, paired=None):
    st = ar1.shape[1]
    ar = jnp.broadcast_to(ar1, (N_SEG, st))
    ai = jnp.broadcast_to(ai1, (N_SEG, st))
    zero = jnp.zeros((N_SEG, st), F32)

    def slab(k):
        if isinstance(k, int):
            return pl.ds(k * N_SEG, N_SEG)
        return pl.ds(pl.multiple_of(k * N_SEG, N_SEG), N_SEG)

    def pass1(j, carry):
        hr, hi = carry
        k = ns - 1 - j if reverse else j
        nr, ni = _cmul(ar, ai, hr, hi)
        return nr + x_ref[slab(k), :st], ni + x_ref[slab(k), st:]

    fr, fi = lax.fori_loop(0, ns, pass1, (zero, zero))
    fin_ref[:, :st] = fr
    fin_ref[:, st:] = fi
    pr, pi = _cpow(ar1, ai1, ns)
    order = list(range(N_SEG - 1, -1, -1)) if reverse else list(range(N_SEG))
    hin_ref[order[0]:order[0] + 1, :] = jnp.zeros((1, 2 * st), F32)
    for a_, b_ in zip(order[:-1], order[1:]):
        cr, ci = _cmul(pr, pi, hin_ref[a_:a_ + 1, :st], hin_ref[a_:a_ + 1, st:])
        hin_ref[b_:b_ + 1, :st] = cr + fin_ref[a_:a_ + 1, :st]
        hin_ref[b_:b_ + 1, st:] = ci + fin_ref[a_:a_ + 1, st:]

    def step2(k, hr, hi):
        nr, ni = _cmul(ar, ai, hr, hi)
        nr = nr + x_ref[slab(k), :st]
        ni = ni + x_ref[slab(k), st:]
        x_ref[slab(k), :st] = nr
        x_ref[slab(k), st:] = ni
        return nr, ni

    if paired is None:
        def pass2(j, carry):
            return step2(ns - 1 - j if reverse else j, *carry)

        lax.fori_loop(0, ns, pass2, (hin_ref[:, :st], hin_ref[:, st:]))
        return None
    p_ref, p_edge_ref, shift = paired

    def pass2_paired(j, carry):
        hr, hi, acr, aci = carry
        k = ns - 1 - j if reverse else j
        nr, ni = step2(k, hr, hi)
        p_r, p_i = p_ref[slab(k + shift), :st], p_ref[slab(k + shift), st:]
        return nr, ni, acr + nr * p_r + ni * p_i, aci + ni * p_r - nr * p_i

    hr, hi, acr, aci = lax.fori_loop(0, ns - 1, pass2_paired, (hin_ref[:, :st], hin_ref[:, st:], zero, zero))
    nr, ni = step2(0 if reverse else ns - 1, hr, hi)
    p_r, p_i = p_edge_ref[:, :st], p_edge_ref[:, st:]
    return acr + nr * p_r + ni * p_i, aci + ni * p_r - nr * p_i


def _s5_specs(r, ch, st):
    u_spec = pl.BlockSpec((r, ch), lambda j: (0, j // 2))
    w_spec = pl.BlockSpec((1, ch, 2 * st), lambda j: (j, 0, 0))
    c_spec = pl.BlockSpec((1, 2 * st, ch), lambda j: (j, 0, 0))
    a_spec = pl.BlockSpec((1, 2, st), lambda j: (j, 0, 0))
    return u_spec, w_spec, c_spec, a_spec


def _s5_fwd(up, w, c, a, rev, name):
    r, s = up.shape
    nh, ch, st2 = w.shape
    st = st2 // 2
    ns = r // N_SEG
    nb = r // N_DEV
    u_spec, w_spec, c_spec, a_spec = _s5_specs(r, ch, st)

    def body(u_ref, w_ref, c_ref, a_ref, y_ref, x, fin, hin):
        j = pl.program_id(0)
        w_b = w_ref[0].astype(MXU_DTYPE)
        c_b = c_ref[0].astype(MXU_DTYPE)
        for rb in range(N_DEV):
            rows = slice(rb * nb, (rb + 1) * nb)
            x[rows, :] = jnp.dot(u_ref[rows, :].astype(MXU_DTYPE), w_b, preferred_element_type=F32)
        _s5_scan_into(x, a_ref[0, 0:1, :], a_ref[0, 1:2, :], ns, fin, hin, rev)
        for rb in range(N_DEV):
            rows = slice(rb * nb, (rb + 1) * nb)
            yb = jnp.dot(x[rows, :].astype(MXU_DTYPE), c_b, preferred_element_type=F32)

            @pl.when(j % 2 == 0)
            def _():
                y_ref[rows, :] = yb

            @pl.when(j % 2 == 1)
            def _():
                y_ref[rows, :] += yb

    small = pltpu.VMEM((N_SEG, st2), F32)
    return pl.pallas_call(
        body, name=name, grid=(nh,), in_specs=[u_spec, w_spec, c_spec, a_spec],
        out_specs=pl.BlockSpec((r, ch), lambda j: (0, j // 2)), out_shape=_sds((r, s), F32),
        scratch_shapes=[pltpu.VMEM((r, st2), F32), small, small],
        compiler_params=_params(("arbitrary",)))(up, w, c, a)


def _s5_bwd(up, dyp, w, c, a, rev, name):
    r, s = up.shape
    nh, ch, st2 = w.shape
    st = st2 // 2
    ns = r // N_SEG
    nb = r // N_DEV
    u_spec, w_spec, c_spec, a_spec = _s5_specs(r, ch, st)
    nt = (((1,), (1,)), ((), ()))
    tn = (((0,), (0,)), ((), ()))

    def body(u_ref, dy_ref, w_ref, c_ref, a_ref, du_ref, dw_ref, dc_ref, da_ref, h, g, fin, sin_, ein):
        j = pl.program_id(0)
        w_b = w_ref[0].astype(MXU_DTYPE)
        c_b = c_ref[0].astype(MXU_DTYPE)
        for rb in range(N_DEV):
            rows = slice(rb * nb, (rb + 1) * nb)
            h[rows, :] = jnp.dot(u_ref[rows, :].astype(MXU_DTYPE), w_b, preferred_element_type=F32)
        ar1, ai1 = a_ref[0, 0:1, :], a_ref[0, 1:2, :]
        _s5_scan_into(h, ar1, ai1, ns, fin, sin_, rev)
        dc = jnp.zeros((st2, ch), F32)
        for rb in range(N_DEV):
            rows = slice(rb * nb, (rb + 1) * nb)
            dyb = dy_ref[rows, :].astype(MXU_DTYPE)
            g[rows, :] = lax.dot_general(dyb, c_b, nt, preferred_element_type=F32)
            dc += lax.dot_general(h[rows, :].astype(MXU_DTYPE), dyb, tn, preferred_element_type=F32)
        dc_ref[0] = dc
        acr, aci = _s5_scan_into(g, ar1, -ai1, ns, fin, ein, not rev, paired=(h, sin_, 1 if rev else -1))
        da_ref[0, 0:1, :] = jnp.sum(acr, axis=0, keepdims=True)
        da_ref[0, 1:2, :] = jnp.sum(aci, axis=0, keepdims=True)
        dw = jnp.zeros((ch, st2), F32)
        for rb in range(N_DEV):
            rows = slice(rb * nb, (rb + 1) * nb)
            gb = g[rows, :].astype(MXU_DTYPE)
            dub = lax.dot_general(gb, w_b, nt, preferred_element_type=F32)
            dw += lax.dot_general(u_ref[rows, :].astype(MXU_DTYPE), gb, tn, preferred_element_type=F32)

            @pl.when(j % 2 == 0)
            def _():
                du_ref[rows, :] = dub

            @pl.when(j % 2 == 1)
            def _():
                du_ref[rows, :] += dub

        dw_ref[0] = dw

    small = pltpu.VMEM((N_SEG, st2), F32)
    big = pltpu.VMEM((r, st2), F32)
    return pl.pallas_call(
        body, name=name, grid=(nh,), in_specs=[u_spec, u_spec, w_spec, c_spec, a_spec],
        out_specs=[pl.BlockSpec((r, ch), lambda j: (0, j // 2)), w_spec, c_spec, a_spec],
        out_shape=[_sds((r, s), F32), _sds(w.shape, F32), _sds(c.shape, F32), _sds(a.shape, F32)],
        scratch_shapes=[big, big, small, small, small],
        compiler_params=_params(("arbitrary",)))(up, dyp, w, c, a)


def _rope(t, cos, sin):
    quarter = t.shape[1] // 4
    lane = lax.broadcasted_iota(jnp.int32, t.shape, 1)
    first = (lane // quarter) % 2 == 0
    partner = jnp.where(first, pltpu.roll(t, t.shape[1] - quarter, 1), pltpu.roll(t, quarter, 1))
    return t * cos + partner * sin


def _rope_t(d, cos, sin):
    quarter = d.shape[1] // 4
    ds_ = d * sin
    lane = lax.broadcasted_iota(jnp.int32, d.shape, 1)
    first = (lane // quarter) % 2 == 0
    partner = jnp.where(first, pltpu.roll(ds_, d.shape[1] - quarter, 1), pltpu.roll(ds_, quarter, 1))
    return d * cos + partner


def _chunk_of_step(s, nch, ncc, rev):
    if not rev:
        return s
    return jnp.where(s < ncc, ncc - 1 - s, nch + ncc - 1 - s)


def _heads_per_step(heads, dk, dv, q_off):
    v_off = q_off + 2 * heads * dk
    for hpg in range(heads, 0, -1):
        if heads % hpg == 0 and q_off % (hpg * dk) == 0:
            piece = math.gcd(v_off, hpg * dv)
            if piece % dv == 0:
                return hpg, piece
    return 1, dv


def _v_specs(hpg, dv, piece, v_off, ch, chunk_of):
    n_pieces = hpg * dv // piece
    return [pl.BlockSpec((ch, piece), functools.partial(
        lambda h, s, p: (chunk_of(s), v_off // piece + h * n_pieces + p), p=p)) for p in range(n_pieces)]


def _v_of_head(v_refs, hl, dv, piece):
    lo = (hl * dv) % piece
    return v_refs[(hl * dv) // piece][:, lo:lo + dv]


def _ret_fwd(hm, cos, sin, decay, wend, win, gch, heads, dk, dv, q_off, ncc, rev, name):
    r = hm.shape[0]
    ch = RET_CHUNK
    nch = r // ch
    t_rows = r - ncc * ch
    hpg, piece = _heads_per_step(heads, dk, dv, q_off)
    qb, kb = q_off // (hpg * dk), (q_off + heads * dk) // (hpg * dk)
    q_scale = dk ** -0.5
    nt = (((1,), (1,)), ((), ()))
    tn = (((0,), (0,)), ((), ()))
    cof = lambda s: _chunk_of_step(s, nch, ncc, rev)
    v_specs = _v_specs(hpg, dv, piece, q_off + 2 * heads * dk, ch, cof)
    nv = len(v_specs)

    def body(q_ref, k_ref, *refs):
        v_refs = refs[:nv]
        cos_ref, sin_ref, dec_ref, we_ref, wi_ref, g_ref, o_ref, sin_out, st = refs[nv:]
        s = pl.program_id(1)

        @pl.when(s == 0)
        def _():
            st[...] = jnp.zeros_like(st)

        cos_, sin_ = cos_ref[...], sin_ref[...]
        for hl in range(hpg):
            ks, vs = slice(hl * dk, (hl + 1) * dk), slice(hl * dv, (hl + 1) * dv)
            q = _rope(q_ref[:, ks].astype(F32), cos_, sin_) * q_scale
            k = _rope(k_ref[:, ks].astype(F32), cos_, sin_)
            v = _v_of_head(v_refs, hl, dv, piece).astype(MXU_DTYPE)
            s_cur = st[hl]
            sin_out[hl, 0] = s_cur
            kw = (k * we_ref[hl]).astype(MXU_DTYPE)
            qw = (q * wi_ref[hl]).astype(MXU_DTYPE)
            scores = lax.dot_general(q.astype(MXU_DTYPE), k.astype(MXU_DTYPE), nt,
                                     preferred_element_type=F32) * dec_ref[hl]
            o_ref[:, vs] = (jnp.dot(scores.astype(MXU_DTYPE), v, preferred_element_type=F32)
                            + jnp.dot(qw, s_cur.astype(MXU_DTYPE), preferred_element_type=F32))
            st[hl] = g_ref[hl] * s_cur + lax.dot_general(kw, v, tn, preferred_element_type=F32)

    tab = lambda w: pl.BlockSpec((hpg, ch, w), lambda h, s: (h, 0, 0))
    return pl.pallas_call(
        body, name=name, grid=(heads // hpg, nch),
        in_specs=[pl.BlockSpec((ch, hpg * dk), lambda h, s: (cof(s), qb + h)),
                  pl.BlockSpec((ch, hpg * dk), lambda h, s: (cof(s), kb + h))] + v_specs +
                 [pl.BlockSpec((ch, dk), lambda h, s: (cof(s), 0)),
                  pl.BlockSpec((ch, dk), lambda h, s: (cof(s), 0)),
                  tab(ch), tab(dk), tab(dk), tab(dv)],
        out_specs=[pl.BlockSpec((ch, hpg * dv), lambda h, s: (jnp.maximum(cof(s) - ncc, 0) if not rev
                                                               else jnp.where(s < ncc, nch - ncc - 1, cof(s) - ncc), h)),
                   pl.BlockSpec((hpg, 1, dk, dv), lambda h, s: (h, s, 0, 0))],
        out_shape=[_sds((t_rows, heads * dv), F32), _sds((heads, nch, dk, dv), F32)],
        scratch_shapes=[pltpu.VMEM((hpg, dk, dv), F32)],
        compiler_params=_params(("parallel", "arbitrary")))(hm, hm, *([hm] * nv), cos, sin, decay, wend, win, gch)


def _ret_bwd(hm, cos, sin, decay, wend, win, gch, s_in, do, heads, dk, dv, q_off, ncc, rev, name):
    r = hm.shape[0]
    ch = RET_CHUNK
    nch = r // ch
    hpg, piece = _heads_per_step(heads, dk, dv, q_off)
    qb, kb = q_off // (hpg * dk), (q_off + heads * dk) // (hpg * dk)
    q_scale = dk ** -0.5
    nt = (((1,), (1,)), ((), ()))
    tn = (((0,), (0,)), ((), ()))
    cof = lambda rr: _chunk_of_step(nch - 1 - rr, nch, ncc, rev)
    v_specs = _v_specs(hpg, dv, piece, q_off + 2 * heads * dk, ch, cof)
    nv = len(v_specs)

    def body(q_ref, k_ref, *refs):
        v_refs = refs[:nv]
        (cos_ref, sin_ref, dec_ref, we_ref, wi_ref, g_ref, sin_ref2, do_ref,
         dq_ref, dk_ref, dv_ref, ddec_ref, dwe_ref, dwi_ref, dg_ref, dst) = refs[nv:]
        rr = pl.program_id(1)
        n = cof(rr)

        @pl.when(rr == 0)
        def _():
            dst[...] = jnp.zeros_like(dst)
            ddec_ref[...] = jnp.zeros_like(ddec_ref)
            dwe_ref[...] = jnp.zeros_like(dwe_ref)
            dwi_ref[...] = jnp.zeros_like(dwi_ref)
            dg_ref[...] = jnp.zeros_like(dg_ref)

        cos_, sin_ = cos_ref[...], sin_ref[...]
        for hl in range(hpg):
            ks, vs = slice(hl * dk, (hl + 1) * dk), slice(hl * dv, (hl + 1) * dv)
            q = _rope(q_ref[:, ks].astype(F32), cos_, sin_) * q_scale
            k = _rope(k_ref[:, ks].astype(F32), cos_, sin_)
            v = _v_of_head(v_refs, hl, dv, piece).astype(MXU_DTYPE)
            qb_, kb_ = q.astype(MXU_DTYPE), k.astype(MXU_DTYPE)
            kw = (k * we_ref[hl]).astype(MXU_DTYPE)
            qw = (q * wi_ref[hl]).astype(MXU_DTYPE)
            sraw = lax.dot_general(qb_, kb_, nt, preferred_element_type=F32)
            scores = (sraw * dec_ref[hl]).astype(MXU_DTYPE)
            d_o = jnp.where(n >= ncc, do_ref[:, vs], 0.0).astype(MXU_DTYPE)
            s_n = sin_ref2[hl, 0]
            s_nb = s_n.astype(MXU_DTYPE)
            ds1 = dst[hl]
            ds1b = ds1.astype(MXU_DTYPE)
            dsc = lax.dot_general(d_o, v, nt, preferred_element_type=F32)
            dsr = (dsc * dec_ref[hl]).astype(MXU_DTYPE)
            ddec_ref[hl] += dsc * sraw
            t1 = lax.dot_general(d_o, s_nb, nt, preferred_element_type=F32)
            dq_r = jnp.dot(dsr, kb_, preferred_element_type=F32) + t1 * wi_ref[hl]
            dwi_ref[hl] += t1 * q
            t2 = lax.dot_general(v, ds1b, nt, preferred_element_type=F32)
            dk_r = lax.dot_general(dsr, qb_, tn, preferred_element_type=F32) + t2 * we_ref[hl]
            dwe_ref[hl] += t2 * k
            dv_ref[:, vs] = (lax.dot_general(scores, d_o, tn, preferred_element_type=F32)
                             + jnp.dot(kw, ds1b, preferred_element_type=F32)).astype(dv_ref.dtype)
            dg_ref[hl] += ds1 * s_n
            dst[hl] = g_ref[hl] * ds1 + lax.dot_general(qw, d_o, tn, preferred_element_type=F32)
            dq_ref[:, ks] = (_rope_t(dq_r, cos_, sin_) * q_scale).astype(dq_ref.dtype)
            dk_ref[:, ks] = _rope_t(dk_r, cos_, sin_).astype(dk_ref.dtype)

    tab = lambda w: pl.BlockSpec((hpg, ch, w), lambda h, rr: (h, 0, 0))
    return pl.pallas_call(
        body, name=name, grid=(heads // hpg, nch),
        in_specs=[pl.BlockSpec((ch, hpg * dk), lambda h, rr: (cof(rr), qb + h)),
                  pl.BlockSpec((ch, hpg * dk), lambda h, rr: (cof(rr), kb + h))] + v_specs +
                 [pl.BlockSpec((ch, dk), lambda h, rr: (cof(rr), 0)),
                  pl.BlockSpec((ch, dk), lambda h, rr: (cof(rr), 0)),
                  tab(ch), tab(dk), tab(dk), tab(dv),
                  pl.BlockSpec((hpg, 1, dk, dv), lambda h, rr: (h, nch - 1 - rr, 0, 0)),
                  pl.BlockSpec((ch, hpg * dv), lambda h, rr: (jnp.maximum(cof(rr) - ncc, 0), h))],
        out_specs=[pl.BlockSpec((ch, hpg * dk), lambda h, rr: (cof(rr), h)),
                   pl.BlockSpec((ch, hpg * dk), lambda h, rr: (cof(rr), h)),
                   pl.BlockSpec((ch, hpg * dv), lambda h, rr: (cof(rr), h)),
                   tab(ch), tab(dk), tab(dk), tab(dv)],
        out_shape=[_sds((r, heads * dk), BF16), _sds((r, heads * dk), BF16), _sds((r, heads * dv), BF16),
                   _sds(decay.shape, F32), _sds(wend.shape, F32), _sds(win.shape, F32), _sds(gch.shape, F32)],
        scratch_shapes=[pltpu.VMEM((hpg, dk, dv), F32)],
        compiler_params=_params(("parallel", "arbitrary")))(hm, hm, *([hm] * nv), cos, sin, decay, wend, win, gch, s_in, do)


_HBM = pl.BlockSpec(memory_space=pltpu.HBM)
_MESH = pl.DeviceIdType.MESH
ALL_GATHER_COLLECTIVE_ID = 1
SIBLING_COLLECTIVE_ID = 2
CHIPS_COLLECTIVE_ID = 3


def _axis_slice(ref, axis, start, size):
    idx = [slice(None)] * len(ref.shape)
    idx[axis] = pl.ds(start, size)
    return ref.at[tuple(idx)]


def _sibling_and_chip_peers():
    x, y, c = lax.axis_index("x"), lax.axis_index("y"), lax.axis_index("c")
    return [(x, y, 1 - c), (1 - x, y, c), (x, 1 - y, c), (1 - x, 1 - y, c)]


def _launch_exchange(body, name, operand, out_shape, sems, peers_fn, collective_id, on_sequencer):
    if not on_sequencer:
        return pl.pallas_call(body, name=name, out_shape=out_shape, in_specs=[_HBM], out_specs=_HBM,
                              scratch_shapes=sems)(operand)

    def sequencer_body(in_ref, out_ref, *sem_refs):
        peers = peers_fn()
        barrier = pltpu.get_barrier_semaphore()
        for peer in peers:
            pl.semaphore_signal(barrier, inc=1, device_id=peer, device_id_type=_MESH)
        pl.semaphore_wait(barrier, len(peers))
        body(in_ref, out_ref, *sem_refs)

    return pl.kernel(sequencer_body, out_type=out_shape, name=name,
                     mesh=plsc.ScalarSubcoreMesh(axis_name="sequencer", num_cores=1), scratch_types=sems,
                     compiler_params=pltpu.CompilerParams(collective_id=collective_id))(operand)


def _all_gather(shard, axis, name, on_sequencer=False):
    m = shard.shape[axis]
    out_shape = list(shard.shape)
    out_shape[axis] = N_DEV * m

    def body(x_ref, out_ref, send_sems, recv_sems, local_sem):
        x, y, c = lax.axis_index("x"), lax.axis_index("y"), lax.axis_index("c")
        me, sibling = (x, y, c), (x, y, 1 - c)
        chips = [(1 - x, y), (x, 1 - y), (1 - x, 1 - y)]

        def block(px, py, pc):
            return _axis_slice(out_ref, axis, (4 * px + 2 * py + pc) * m, m)

        def copy(k, blk, to, src=None):
            return pltpu.make_async_remote_copy(
                src_ref=block(*blk) if src is None else src, dst_ref=block(*blk), send_sem=send_sems.at[k],
                recv_sem=recv_sems.at[k], device_id=to, device_id_type=_MESH)

        mine = pltpu.make_async_copy(x_ref, block(*me), local_sem)
        mine.start()
        first = [copy(0, me, sibling, src=x_ref)]
        first += [copy(1 + j, me, (*chip, c), src=x_ref) for j, chip in enumerate(chips)]
        for cp in first:
            cp.start()
        passed = [copy(4 + j, (*chip, c), sibling) for j, chip in enumerate(chips)]
        for j, chip in enumerate(chips):
            copy(1 + j, (*chip, c), me).wait_recv()
            passed[j].start()
        copy(0, sibling, me).wait_recv()
        for j, chip in enumerate(chips):
            copy(4 + j, (*chip, 1 - c), me).wait_recv()
        for cp in first + passed:
            cp.wait_send()
        mine.wait()

    return _launch_exchange(
        body, name, shard, _sds(out_shape, shard.dtype),
        [pltpu.SemaphoreType.DMA((7,)), pltpu.SemaphoreType.DMA((7,)), pltpu.SemaphoreType.DMA(())],
        _sibling_and_chip_peers, ALL_GATHER_COLLECTIVE_ID, on_sequencer)


def _rs_sibling(g, axis, name, on_sequencer=False):
    m = g.shape[axis] // N_DEV
    blk_shape = list(g.shape)
    blk_shape[axis] = m
    n_chips = N_DEV // 2

    def body(g_ref, recv_ref, send_sems, recv_sems):
        x, y, c = lax.axis_index("x"), lax.axis_index("y"), lax.axis_index("c")
        sibling = (x, y, 1 - c)
        send = [pltpu.make_async_remote_copy(
            src_ref=_axis_slice(g_ref, axis, (2 * q + 1 - c) * m, m), dst_ref=recv_ref.at[q],
            send_sem=send_sems.at[q], recv_sem=recv_sems.at[q], device_id=sibling, device_id_type=_MESH)
            for q in range(n_chips)]
        for cp in send:
            cp.start()
        for cp in send:
            cp.wait_recv()
        for cp in send:
            cp.wait_send()

    return _launch_exchange(
        body, name, g, _sds([n_chips] + blk_shape, g.dtype),
        [pltpu.SemaphoreType.DMA((n_chips,)), pltpu.SemaphoreType.DMA((n_chips,))],
        lambda: _sibling_and_chip_peers()[:1], SIBLING_COLLECTIVE_ID, on_sequencer)


def _rs_chips(p, name, on_sequencer=False):
    n_peers = p.shape[0] - 1

    def body(p_ref, out_ref, send_sems, recv_sems):
        x, y, c = lax.axis_index("x"), lax.axis_index("y"), lax.axis_index("c")
        chips = [(1 - x, y), (x, 1 - y), (1 - x, 1 - y)]
        send = [pltpu.make_async_remote_copy(
            src_ref=p_ref.at[2 * cx + cy], dst_ref=out_ref.at[j], send_sem=send_sems.at[j],
            recv_sem=recv_sems.at[j], device_id=(cx, cy, c), device_id_type=_MESH)
            for j, (cx, cy) in enumerate(chips)]
        for cp in send:
            cp.start()
        for cp in send:
            cp.wait_recv()
        for cp in send:
            cp.wait_send()

    return _launch_exchange(
        body, name, p, _sds((n_peers,) + p.shape[1:], p.dtype),
        [pltpu.SemaphoreType.DMA((n_peers,)), pltpu.SemaphoreType.DMA((n_peers,))],
        lambda: _sibling_and_chip_peers()[1:], CHIPS_COLLECTIVE_ID, on_sequencer)


def _reduce_scatter(g, axis, name):
    sib = _rs_sibling(g, axis, name + "_d2d", on_sequencer=True)
    p = _pair_sum(g, sib, axis, name + "_pair")
    return p, _rs_chips(p, name + "_ici", on_sequencer=True)


def _s5_tables(lam_re, lam_im, log_step, b_re, b_im, c_re, c_im):
    nd, g, p, cg = b_re.shape
    step = jnp.exp(log_step)[..., None]
    mag = jnp.exp(lam_re * step)
    a_re, a_im = mag * jnp.cos(lam_im * step), mag * jnp.sin(lam_im * step)
    den = lam_re * lam_re + lam_im * lam_im
    num_re, num_im = a_re - 1.0, a_im
    k_re = (num_re * lam_re + num_im * lam_im) / den
    k_im = (num_im * lam_re - num_re * lam_im) / den
    bb_re = k_re[..., None] * b_re - k_im[..., None] * b_im
    bb_im = k_re[..., None] * b_im + k_im[..., None] * b_re
    gt = g // SSM_TILE_GROUPS
    hg = SSM_HALF_GROUPS
    eye = jnp.eye(SSM_TILE_GROUPS, dtype=F32).reshape(SSM_TILE_GROUPS, 2, hg)

    def pack_b(bb):
        w = jnp.einsum("djhqpc,ghq->djhgcqp", bb.reshape(nd, gt, 2, hg, p, cg), eye)
        return w.reshape(nd, gt * 2, SSM_TILE_GROUPS * cg, hg * p)

    def pack_c(cc):
        w = jnp.einsum("djhqcp,ghq->djhqpgc", cc.reshape(nd, gt, 2, hg, cg, p), eye)
        return w.reshape(nd, gt * 2, hg * p, SSM_TILE_GROUPS * cg)

    a = jnp.stack([a_re.reshape(nd, gt * 2, hg * p), a_im.reshape(nd, gt * 2, hg * p)], axis=2)
    w = jnp.concatenate([pack_b(bb_re), pack_b(bb_im)], axis=-1)
    c = jnp.concatenate([pack_c(c_re), -pack_c(c_im)], axis=-2)
    return w, c, a


def _ret_tables(decay_logit, dk, dv):
    ch = RET_CHUNK
    nd, h = decay_logit.shape
    lg = jax.nn.log_sigmoid(decay_logit)[:, :, None]
    pos = jnp.arange(ch, dtype=F32)
    fwd_diff = pos[:, None] - pos[None, :]
    diff = jnp.stack([fwd_diff, -fwd_diff])[:, None]
    mask = jnp.stack([fwd_diff >= 0, -fwd_diff > 0])[:, None]
    end_pos = jnp.stack([ch - 1.0 - pos, pos])[:, None]
    in_pos = jnp.stack([pos + 1.0, ch - pos])[:, None]
    w_end = jnp.exp(lg * end_pos)
    w_in = jnp.exp(lg * in_pos)
    decay = jnp.where(mask, jnp.exp(lg[..., None] * jnp.where(mask, diff, 0.0)), 0.0)
    g_chunk = jnp.exp(lg[..., 0] * ch)
    return (decay, jnp.broadcast_to(w_end[..., None], (nd, h, ch, dk)), jnp.broadcast_to(w_in[..., None], (nd, h, ch, dk)),
            jnp.broadcast_to(g_chunk[..., None, None], (nd, h, dk, dv)))


def _rope_tables(t_rows, ncc, dk):
    quarter = dk // 4
    idx = np.arange(t_rows)
    row, col = idx // GRID_W, idx % GRID_W
    inv = ROPE_BASE ** (-np.arange(quarter, dtype=np.float32) / quarter)
    ang_r = row.astype(np.float32)[:, None] * inv
    ang_c = col.astype(np.float32)[:, None] * inv
    ang_r, ang_c = jnp.asarray(ang_r, F32), jnp.asarray(ang_c, F32)
    cos = jnp.concatenate([jnp.cos(ang_r), jnp.cos(ang_r), jnp.cos(ang_c), jnp.cos(ang_c)], axis=1)
    sin = jnp.concatenate([-jnp.sin(ang_r), jnp.sin(ang_r), -jnp.sin(ang_c), jnp.sin(ang_c)], axis=1)
    n_ctx = ncc * RET_CHUNK
    cos = jnp.concatenate([jnp.ones((n_ctx, dk), F32), cos], axis=0)
    sin = jnp.concatenate([jnp.zeros((n_ctx, dk), F32), sin], axis=0)
    return cos, sin


def _to_scan_layout(ctx_rows, lat_rows, rev):
    u = jnp.concatenate([lat_rows, ctx_rows] if rev else [ctx_rows, lat_rows], axis=0)
    r, w = u.shape
    return u.reshape(N_SEG, r // N_SEG, w).transpose(1, 0, 2).reshape(r, w)


def _from_scan_layout(yp, n_ctx, rev):
    r, w = yp.shape
    y = yp.reshape(r // N_SEG, N_SEG, w).transpose(1, 0, 2).reshape(r, w)
    return (y[r - n_ctx:], y[:r - n_ctx]) if rev else (y[:n_ctx], y[n_ctx:])


def _pack(parts, width):
    rows = []
    for p in parts:
        flat = p.reshape(-1).astype(F32)
        n = flat.shape[0]
        rows.append(jnp.pad(flat, (0, -n % (SUBLANE * width))).reshape(-1, width))
    return jnp.concatenate(rows, axis=0)


def _packed_rows(n, width):
    return -(-n // (SUBLANE * width)) * SUBLANE


def _unpack(flat2d, shapes):
    width = flat2d.shape[1]
    out, row = [], 0
    for shp in shapes:
        n = int(np.prod(shp))
        nr = _packed_rows(n, width)
        out.append(flat2d[row:row + nr].reshape(-1)[:n].reshape(shp))
        row += nr
    return out


def kernel(x, c, ctx, c_ctx, ada_w, ada_b, norm_g, ffn_w_in, ffn_w_out, mix_w_in, ssm_lam_re, ssm_lam_im, ssm_log_step, ssm_b_re, ssm_b_im, ssm_c_re, ssm_c_im, ssm_d, ssm_glu_w, ret_decay_logit, ret_w_proj, mix_w_out, loss_target, m_c_ctx, m_ada_w, m_ada_b, m_norm_g, m_ffn_w_in, m_ffn_w_out, m_mix_w_in, m_ssm_lam_re, m_ssm_lam_im, m_ssm_log_step, m_ssm_b_re, m_ssm_b_im, m_ssm_c_re, m_ssm_c_im, m_ssm_d, m_ssm_glu_w, m_ret_decay_logit, m_ret_w_proj, m_mix_w_out, v_c_ctx, v_ada_w, v_ada_b, v_norm_g, v_ffn_w_in, v_ffn_w_out, v_mix_w_in, v_ssm_lam_re, v_ssm_lam_im, v_ssm_log_step, v_ssm_b_re, v_ssm_b_im, v_ssm_c_re, v_ssm_c_im, v_ssm_d, v_ssm_glu_w, v_ret_decay_logit, v_ret_w_proj, v_mix_w_out):
    t_rows, d = x.shape[1], x.shape[2]
    n_ctx = ctx.shape[1]
    r = n_ctx + t_rows
    ssm_w = ssm_d.shape[1]
    heads = ret_decay_logit.shape[2]
    mi = mix_w_in.shape[2] * N_DEV
    dk = (mi - ssm_w - 2 * d) // (6 * heads)
    dv = 2 * dk
    qk_w, v_w = heads * dk, heads * dv
    q_off = ssm_w
    ncc = n_ctx // RET_CHUNK
    tile = n_ctx
    nct = 1
    wide_tile = _tile(n_ctx, 128, 16)
    assert r % (N_SEG * SUBLANE) == 0 and n_ctx % RET_CHUNK == 0 and t_rows % tile == 0
    me = 4 * lax.axis_index("x") + 2 * lax.axis_index("y") + lax.axis_index("c")
    g_off = ssm_w + 2 * qk_w + v_w
    gs_off = g_off + v_w

    ng_cols = norm_g.shape[2]
    small0 = _pack([c[0], norm_g[0]], d)
    small0_all = _all_gather(small0, 0, "ag_cond")

    bf = lambda w: w.astype(BF16)
    small0_all, sh_in1 = lax.optimization_barrier((small0_all, bf(ffn_w_in[0, 0])))
    small0_all = small0_all.reshape(N_DEV, -1)
    w_in1 = _all_gather(sh_in1, 1, "ag_ffn1_in", on_sequencer=True)
    w_glu = _all_gather(bf(ssm_glu_w[0]), 1, "ag_glu", on_sequencer=True)
    w_rp = _all_gather(bf(ret_w_proj[0]), 0, "ag_ret_proj", on_sequencer=True)
    w_mo = _all_gather(bf(mix_w_out[0]), 0, "ag_mix_out", on_sequencer=True)
    w_in2 = _all_gather(bf(ffn_w_in[0, 1]), 1, "ag_ffn2_in", on_sequencer=True)
    w_out2 = _all_gather(bf(ffn_w_out[0, 1]), 0, "ag_ffn2_out", on_sequencer=True)

    ng_at = _packed_rows(d, d) * d
    c_all = small0_all[:, :d]
    g_full = small0_all[:, ng_at:ng_at + 6 * ng_cols].reshape(N_DEV, 6, ng_cols).transpose(1, 0, 2).reshape(6, d)
    g6 = g_full.reshape(6, 1, d)
    cc = jnp.concatenate([c_all, c_ctx[None, :], jnp.zeros((2 * SUBLANE - N_DEV - 1, d), F32)], axis=0)
    sc = _silu_rows(cc, "ada_silu")
    na = ada_w.shape[2]
    a_loc = _mm(sc, ada_w[0], "nn", F32, "ada_fwd", tm=16, tn=na, tk=512)
    a_all = _all_gather(a_loc, 0, "ag_ada")
    a_all, sh_out1, sh_mix = lax.optimization_barrier((a_all, bf(ffn_w_out[0, 0]), bf(mix_w_in[0])))
    a_all = a_all.reshape(N_DEV, 2 * SUBLANE, na)
    w_out1 = _all_gather(sh_out1, 0, "ag_ffn1_out", on_sequencer=True)
    w_mix = _all_gather(sh_mix, 1, "ag_mix_in", on_sequencer=True)
    ada_x = lax.dynamic_index_in_dim(a_all, me, axis=1, keepdims=False).reshape(9 * d) + ada_b[0]
    ada_c = a_all[:, N_DEV, :].reshape(9 * d) + ada_b[0]
    mods = jnp.stack([ada_c.reshape(9, d), ada_x.reshape(9, d)]).reshape(18, 1, d)

    xin = jnp.concatenate([ctx[0], x[0]], axis=0)
    u1 = _ada_pre_fwd(xin, g6, mods, 0, 0, nct, tile, "pre1")
    g1, up1, a1 = _mm_swiglu(u1, w_in1, "ffn1_in", tm=544)
    o1 = _mm(a1, w_out1, "nn", BF16, "ffn1_out", tm=544, tn=1024, tk=2816)
    x1 = _ada_post_fwd(xin, o1, g6, mods, 1, 0, 0.5, nct, tile, "post1")
    u2 = _ada_pre_fwd(x1, g6, mods, 2, 1, nct, tile, "pre2")
    hm = _mm(u2, w_mix, "nn", BF16, "mix_in", tm=544, tn=1024)

    us_ctx, us_lat = hm[:n_ctx, :ssm_w], hm[n_ctx:, :ssm_w]
    dskip = ssm_d.reshape(1, 1, ssm_w)
    s5_prm = (ssm_lam_re[0], ssm_lam_im[0], ssm_log_step[0], ssm_b_re[0], ssm_b_im[0], ssm_c_re[0], ssm_c_im[0])
    s5_tabs_both, s5_vjp = jax.vjp(_s5_tables, *s5_prm)
    s5_tabs, ups, y_dirs = [], [], []
    for dr in range(2):
        tabs = tuple(t[dr] for t in s5_tabs_both)
        up = _to_scan_layout(us_ctx, us_lat, dr == 1)
        yp = _s5_fwd(up, *tabs, dr == 1, "s5_fwd%d" % dr)
        s5_tabs.append(tabs)
        ups.append(up)
        y_dirs.append(_from_scan_layout(yp, n_ctx, dr == 1)[1])
    a_ssm = _ssm_out_fwd(y_dirs[0], y_dirs[1], hm, dskip, nct, tile, "ssm_out")
    gab = _mm(a_ssm, w_glu, "nn", BF16, "glu", tm=512, tn=2048, tk=ssm_w)

    cos, sin = _rope_tables(t_rows, ncc, dk)
    ret_tabs_both, ret_vjp = jax.vjp(functools.partial(_ret_tables, dk=dk, dv=dv), ret_decay_logit[0])
    ret_tabs, o_dirs, s_ins = [], [], []
    for dr in range(2):
        tabs = tuple(t[dr] for t in ret_tabs_both)
        o_d, s_in = _ret_fwd(hm, cos, sin, *tabs, heads, dk, dv, q_off, ncc, dr == 1, "ret_fwd%d" % dr)
        ret_tabs.append(tabs)
        o_dirs.append(o_d)
        s_ins.append(s_in)
    ret_in = _ret_gate_fwd(o_dirs[0], o_dirs[1], hm, g_off, heads, dv, nct, tile, "ret_gate")
    rb = _mm(ret_in, w_rp, "nn", BF16, "ret_proj", tm=512, tn=d, tk=v_w)
    merged = _merge_fwd(gab, rb, hm, gs_off, nct, tile, "merge")
    mix = _mm(merged, w_mo, "nn", BF16, "mix_out", tm=512, tn=d, tk=d)
    x2 = _ada_post_fwd(x1, mix, g6, mods, 3, 1, 1.0, 0, tile, "post2", h_tile_offset=nct)
    u3 = _ada_pre_fwd(x2, g6, mods, 4, 2, 0, tile, "pre3")
    g3, up3, a3 = _mm_swiglu(u3, w_in2, "ffn2_in", tm=512)
    o3 = _mm(a3, w_out2, "nn", BF16, "ffn2_out", tm=512, tn=1024, tk=2816)
    x3 = _ada_post_fwd(x2, o3, g6, mods, 5, 2, 0.5, 0, tile, "post3")
    dy, lcols = _loss_grad(x3, loss_target[0], tile, "loss")
    loss_part = (0.5 * jnp.sum(lcols) / d).reshape(1)

    dg6 = [None] * 6
    dmod = {}

    def add_mod(sel_rows, k, val):
        for sel, row in sel_rows:
            dmod[(sel, k)] = dmod.get((sel, k), 0.0) + val[row, 0]

    both, lat = [(0, 0), (1, 1)], [(1, 0)]

    def tie(*vals):
        return lax.optimization_barrier(vals)

    def big_update(w3d, m3d, v3d, layer, gfull, axis, name, filled=None):
        p, recv = _reduce_scatter(gfull, axis, "rs_" + name)
        return _adamw_scattered(w3d, m3d, v3d, layer, p, recv, "adamw_" + name, filled)

    do3, dg6[5], dgt = _ada_post_bwd(dy, o3, g6, mods, 5, 2, 0.5, 0, 1, tile, "post3_bwd")
    add_mod(lat, 8, dgt)
    gw_out2 = _mm(a3, do3, "tn", BF16, "ffn2_out_dw", tm=1408, tn=1024, tk=2176)
    do3, gw_out2 = tie(do3, gw_out2)
    up_out2 = big_update(ffn_w_out[0], m_ffn_w_out[0], v_ffn_w_out[0], 1, gw_out2, 0, "ffn2_out")
    da3 = _mm(do3, w_out2, "nt", BF16, "ffn2_out_dx", tm=512, tn=2816, tk=d)
    dh3 = _swiglu_bwd(g3, up3, da3, wide_tile, "swiglu2_bwd")
    gw_in2 = _mm(u3, dh3, "tn", BF16, "ffn2_in_dw", tm=1024, tn=1024, tk=2176)
    dh3, gw_in2 = tie(dh3, gw_in2)
    up_in2 = big_update(ffn_w_in[0], m_ffn_w_in[0], v_ffn_w_in[0], 1, gw_in2, 1, "ffn2_in")
    du3 = _mm(dh3, w_in2, "nt", F32, "ffn2_in_dx", tm=512, tn=d, tk=1024)
    dx2, dg6[4], dsh, dsc = _ada_pre_bwd(x2, du3, dy, g6, mods, 4, 2, 0, 1, tile, "pre3_bwd")
    add_mod(lat, 6, dsh)
    add_mod(lat, 7, dsc)
    dmix, dg6[3], dgt = _ada_post_bwd(dx2, mix, g6, mods, 3, 1, 1.0, 0, 1, tile, "post2_bwd")
    add_mod(lat, 5, dgt)
    gw_mo = _mm(merged, dmix, "tn", BF16, "mix_out_dw", tm=1024, tn=1024, tk=2176)
    dmix, gw_mo = tie(dmix, gw_mo)
    up_mo = big_update(mix_w_out, m_mix_w_out, v_mix_w_out, 0, gw_mo, 0, "mix_out")
    dmerged = _mm(dmix, w_mo, "nt", BF16, "mix_out_dx", tm=512, tn=d, tk=d)
    dgab, drb, dgs, dgr = _merge_bwd(gab, rb, hm, gs_off, dmerged, nct, tile, "merge_bwd")
    gw_glu = _mm(a_ssm, dgab, "tn", BF16, "glu_dw", tm=1024, tn=1024, tk=2176)
    gw_rp = _mm(ret_in, drb, "tn", BF16, "ret_proj_dw", tm=1024, tn=1024, tk=2176)
    dgab, drb, gw_glu, gw_rp = tie(dgab, drb, gw_glu, gw_rp)
    up_glu = big_update(ssm_glu_w, m_ssm_glu_w, v_ssm_glu_w, 0, gw_glu, 1, "glu")
    up_rp = big_update(ret_w_proj, m_ret_w_proj, v_ret_w_proj, 0, gw_rp, 0, "ret_proj")
    da_ssm = _mm(dgab, w_glu, "nt", BF16, "glu_dx", tm=512, tn=ssm_w, tk=2 * d)
    dret_in = _mm(drb, w_rp, "nt", BF16, "ret_proj_dx", tm=512, tn=v_w, tk=d)
    d_o, dg_gate = _ret_gate_bwd(o_dirs[0], o_dirs[1], hm, g_off, dret_in, heads, dv, nct, tile, "ret_gate_bwd")
    dy_ssm, dus_direct, d_dskip = _ssm_out_bwd(y_dirs[0], y_dirs[1], hm, dskip, da_ssm, nct, tile, "ssm_out_bwd")
    s5_table_grads, du_ctx, du_lat = [], [], [dus_direct]
    for dr in range(2):
        dyp = _to_scan_layout(jnp.zeros((n_ctx, ssm_w), F32), dy_ssm, dr == 1)
        if dr == 1:
            dyp, up_out2, up_in2 = tie(dyp, up_out2, up_in2)
        outs = _s5_bwd(ups[dr], dyp, *s5_tabs[dr], dr == 1, "s5_bwd%d" % dr)
        part_ctx, part_lat = _from_scan_layout(outs[0], n_ctx, dr == 1)
        du_ctx.append(part_ctx)
        du_lat.append(part_lat)
        s5_table_grads.append(outs[1:])
    dqkv, ret_table_grads = [], []
    for dr in range(2):
        if dr == 1:
            d_o, up_mo, up_glu, up_rp = tie(d_o, up_mo, up_glu, up_rp)
        outs = _ret_bwd(hm, cos, sin, *ret_tabs[dr], s_ins[dr], d_o, heads, dk, dv, q_off, ncc, dr == 1,
                        "ret_bwd%d" % dr)
        dqkv.append(outs[:3])
        ret_table_grads.append(outs[3:])
    both_dirs = lambda grads: tuple(jnp.stack([g0, g1]) for g0, g1 in zip(*grads))
    early_parts = list(s5_vjp(both_dirs(s5_table_grads))) + list(ret_vjp(both_dirs(ret_table_grads)))
    s5_names = 7
    early_shapes = [p.shape for p in early_parts]
    early_all = _all_gather(_pack(early_parts, 1024), 0, "ag_s5_grads", on_sequencer=True)
    early_sum = _sum_leading(early_all.reshape(N_DEV, -1, 1024), "sum_s5_grads")
    dus = jnp.concatenate([du_ctx[0] + du_ctx[1], du_lat[0] + du_lat[1] + du_lat[2]], axis=0)
    dhm = _assemble_dhm(dus, dqkv[0][0], dqkv[1][0], dqkv[0][1], dqkv[1][1], dqkv[0][2], dqkv[1][2],
                        dg_gate, dgs, dgr, n_ctx // wide_tile, wide_tile, "assemble_dhm")
    gw_mix = _mm(u2, dhm, "tn", BF16, "mix_in_dw", tm=1024, tn=1024, tk=2176)
    dhm, gw_mix = tie(dhm, gw_mix)
    up_mix = big_update(mix_w_in, m_mix_w_in, v_mix_w_in, 0, gw_mix, 1, "mix_in")
    du2 = _mm(dhm, w_mix, "nt", F32, "mix_in_dx", tm=544, tn=d, tk=1024)
    dx1, dg6[2], dsh, dsc = _ada_pre_bwd(x1, du2, dx2, g6, mods, 2, 1, nct, 2, tile, "pre2_bwd", dres_x_only=True)
    add_mod(both, 3, dsh)
    add_mod(both, 4, dsc)
    do1, dg6[1], dgt = _ada_post_bwd(dx1, o1, g6, mods, 1, 0, 0.5, nct, 2, tile, "post1_bwd")
    add_mod(both, 2, dgt)
    gw_out1 = _mm(a1, do1, "tn", BF16, "ffn1_out_dw", tm=1408, tn=1024, tk=2176)
    do1, gw_out1 = tie(do1, gw_out1)
    up_out1 = big_update(ffn_w_out[0], m_ffn_w_out[0], v_ffn_w_out[0], 0, gw_out1, 0, "ffn1_out", filled=up_out2)
    da1 = _mm(do1, w_out1, "nt", BF16, "ffn1_out_dx", tm=544, tn=2816, tk=d)
    dh1 = _swiglu_bwd(g1, up1, da1, wide_tile, "swiglu1_bwd")
    dh1, up_mix, early_sum = tie(dh1, up_mix, early_sum)
    early_sums = _unpack(early_sum, early_shapes)
    gw_in1 = _mm(u1, dh1, "tn", BF16, "ffn1_in_dw", tm=1024, tn=1024, tk=2176)
    dh1, gw_in1 = tie(dh1, gw_in1)
    up_in1 = big_update(ffn_w_in[0], m_ffn_w_in[0], v_ffn_w_in[0], 0, gw_in1, 1, "ffn1_in", filled=up_in2)
    du1 = _mm(dh1, w_in1, "nt", F32, "ffn1_in_dx", tm=544, tn=d, tk=1024)
    dx_lat, dg6[0], dsh, dsc = _ada_pre_bwd(xin, du1, dx1, g6, mods, 0, 0, nct, 2, tile, "pre1_bwd",
                                            latent_dh_only=True)
    add_mod(both, 0, dsh)
    add_mod(both, 1, dsc)
    grad_x = dx_lat[None]

    zero_d = jnp.zeros((d,), F32)
    d_ada_x = jnp.stack([dmod.get((1, k), zero_d) for k in range(9)]).reshape(9 * d)
    d_ada_c = jnp.stack([dmod.get((0, k), zero_d) for k in range(9)]).reshape(9 * d)
    dg_full = jnp.stack([g[0, 0] for g in dg6])
    small_parts = [d_ada_x, d_ada_c, dg_full, d_dskip, loss_part]
    small_shapes = [p.shape for p in small_parts]
    packed = _pack(small_parts, 1024)
    gathered = _all_gather(packed, 0, "ag_small_grads").reshape(N_DEV, -1, 1024)
    summed = _sum_leading(gathered, "sum_small_grads")
    sums = _unpack(summed, small_shapes)
    sum_dx, sum_dc, sum_dg = sums[0], sums[1], sums[2]
    loss = sums[4][0]
    grad_ada_b = (sum_dx + sum_dc)[None]
    dx_rows = gathered.reshape(N_DEV, -1)[:, :9 * d]
    col0 = me * na
    da_rows = jnp.concatenate([lax.dynamic_slice_in_dim(dx_rows, col0, na, axis=1),
                               lax.dynamic_slice_in_dim(sum_dc[None], col0, na, axis=1),
                               jnp.zeros((2 * SUBLANE - N_DEV - 1, na), F32)], axis=0)
    grad_ada_w = _mm(sc, da_rows, "tn", F32, "ada_dw", tm=512, tn=na, tk=16)
    d_sc = _mm(da_rows, ada_w[0], "nt", F32, "ada_dx", tm=16, tn=512, tk=na)
    d_sc_all = _all_gather(jnp.broadcast_to(d_sc[N_DEV:N_DEV + 1], (SUBLANE, d)), 0, "ag_dctx")
    d_sc_sum = _sum_leading(d_sc_all.reshape(N_DEV, SUBLANE, d), "sum_dctx")
    grad_c_ctx = _silu_grad_rows(jnp.broadcast_to(c_ctx[None], (SUBLANE, d)), d_sc_sum, "ctx_silu_bwd")[0]
    grad_norm_g = lax.dynamic_slice_in_dim(sum_dg, me * ng_cols, ng_cols, axis=1)[None]

    upd = {}
    upd["ffn_w_in"] = [o[None] for o in up_in1]
    upd["ffn_w_out"] = [o[None] for o in up_out1]
    upd["mix_w_in"] = list(up_mix)
    upd["ssm_glu_w"] = list(up_glu)
    upd["ret_w_proj"] = list(up_rp)
    upd["mix_w_out"] = list(up_mo)
    upd["ada_w"] = [o[None] for o in _adamw(ada_w[0], m_ada_w[0], v_ada_w[0], grad_ada_w[None], "adamw_ada_w")]

    small_names = ["c_ctx", "ada_b", "norm_g", "ssm_lam_re", "ssm_lam_im", "ssm_log_step", "ssm_b_re", "ssm_b_im",
                   "ssm_c_re", "ssm_c_im", "ssm_d", "ret_decay_logit"]
    small_w = [c_ctx, ada_b, norm_g, ssm_lam_re, ssm_lam_im, ssm_log_step, ssm_b_re, ssm_b_im, ssm_c_re, ssm_c_im,
               ssm_d, ret_decay_logit]
    small_m = [m_c_ctx, m_ada_b, m_norm_g, m_ssm_lam_re, m_ssm_lam_im, m_ssm_log_step, m_ssm_b_re, m_ssm_b_im,
               m_ssm_c_re, m_ssm_c_im, m_ssm_d, m_ret_decay_logit]
    small_v = [v_c_ctx, v_ada_b, v_norm_g, v_ssm_lam_re, v_ssm_lam_im, v_ssm_log_step, v_ssm_b_re, v_ssm_b_im,
               v_ssm_c_re, v_ssm_c_im, v_ssm_d, v_ret_decay_logit]
    small_g = [grad_c_ctx, grad_ada_b, grad_norm_g] + [s[None] for s in early_sums[:s5_names]] + \
              [sums[3].reshape(ssm_d.shape), early_sums[s5_names][None]]
    shapes = [w.shape for w in small_w]
    res = _adamw(_pack(small_w, 1024), _pack(small_m, 1024), _pack(small_v, 1024), _pack(small_g, 1024)[None],
                 "adamw_small")
    small_out = [_unpack(o, shapes) for o in res]
    for i, nm in enumerate(small_names):
        upd[nm] = [small_out[kind][i] for kind in range(4)]

    order = ["c_ctx", "ada_w", "ada_b", "norm_g", "ffn_w_in", "ffn_w_out", "mix_w_in", "ssm_lam_re", "ssm_lam_im",
             "ssm_log_step", "ssm_b_re", "ssm_b_im", "ssm_c_re", "ssm_c_im", "ssm_d", "ssm_glu_w", "ret_decay_logit",
             "ret_w_proj", "mix_w_out"]
    outs = [loss, grad_x]
    for kind in range(4):
        outs += [upd[nm][kind] for nm in order]
    return tuple(outs)
```

```python
import functools
import math

import jax
import jax.numpy as jnp
import numpy as np
from jax import lax
from jax.experimental import pallas as pl
from jax.experimental.pallas import tpu as pltpu
from jax.experimental.pallas import tpu_sc as plsc

F32 = jnp.float32
BF16 = jnp.bfloat16
MXU_DTYPE = jnp.bfloat16
MESH_AXES = ("x", "y", "c")
N_DEV = 8
V7X_VMEM_LIMIT_BYTES = 56 * 1024 * 1024
LANE = 128
SUBLANE = 8

GRID_W = 64
RET_CHUNK = 128
ROPE_BASE = 10000.0
NORM_EPS = 1e-6
ADAM_LR = 0.001
ADAM_B1 = 0.9
ADAM_B2 = 0.999
ADAM_EPS = 1e-08
ADAM_WD = 0.01
ADAM_STEP = 10
SSM_TILE_GROUPS = 8
SSM_HALF_GROUPS = 4
N_SEG = 16


def _params(sem=None):
    return pltpu.CompilerParams(dimension_semantics=sem, vmem_limit_bytes=V7X_VMEM_LIMIT_BYTES)


def _tile(n, target, mult):
    best = None
    t = mult
    while t <= min(n, target):
        if n % t == 0:
            best = t
        t += mult
    return n if best is None else best


def _sds(shape, dtype):
    return jax.ShapeDtypeStruct(tuple(shape), dtype)


def _mm(a, b, dims, out_dtype, name, tm=512, tn=1408, tk=2048):
    if dims == "nn":
        (m, k), (k2, n) = a.shape, b.shape
    elif dims == "nt":
        (m, k), (n, k2) = a.shape, b.shape
    else:
        (k, m), (k2, n) = a.shape, b.shape
    assert k == k2, (a.shape, b.shape, dims)
    tm = _tile(m, tm, 16)
    tn = _tile(n, tn, LANE)
    tk = _tile(k, tk, LANE if dims != "tn" else 16)
    nk = k // tk
    dn = {"nn": (((1,), (0,)), ((), ())), "nt": (((1,), (1,)), ((), ())), "tn": (((0,), (0,)), ((), ()))}[dims]

    def product(a_ref, b_ref):
        return lax.dot_general(a_ref[...].astype(MXU_DTYPE), b_ref[...].astype(MXU_DTYPE), dn,
                               preferred_element_type=F32)

    def body_single(a_ref, b_ref, o_ref):
        o_ref[...] = product(a_ref, b_ref).astype(o_ref.dtype)

    def body(a_ref, b_ref, o_ref, acc_ref):
        kk = pl.program_id(2)

        @pl.when(kk == 0)
        def _():
            acc_ref[...] = product(a_ref, b_ref)

        @pl.when((kk > 0) & (kk < nk - 1))
        def _():
            acc_ref[...] += product(a_ref, b_ref)

        @pl.when(kk == nk - 1)
        def _():
            o_ref[...] = (acc_ref[...] + product(a_ref, b_ref)).astype(o_ref.dtype)

    if dims == "nn":
        a_spec = pl.BlockSpec((tm, tk), lambda j, i, kk: (i, kk))
        b_spec = pl.BlockSpec((tk, tn), lambda j, i, kk: (kk, j))
    elif dims == "nt":
        a_spec = pl.BlockSpec((tm, tk), lambda j, i, kk: (i, kk))
        b_spec = pl.BlockSpec((tn, tk), lambda j, i, kk: (j, kk))
    else:
        a_spec = pl.BlockSpec((tk, tm), lambda j, i, kk: (kk, i))
        b_spec = pl.BlockSpec((tk, tn), lambda j, i, kk: (kk, j))
    return pl.pallas_call(
        body_single if nk == 1 else body, name=name, grid=(n // tn, m // tm, nk), in_specs=[a_spec, b_spec],
        out_specs=pl.BlockSpec((tm, tn), lambda j, i, kk: (i, j)), out_shape=_sds((m, n), out_dtype),
        scratch_shapes=[] if nk == 1 else [pltpu.VMEM((tm, tn), F32)],
        compiler_params=_params(("parallel", "parallel", "arbitrary")))(a, b)


def _mm_swiglu(a, b, name, tm=512, tn=512):
    m, k = a.shape
    k2, f2 = b.shape
    f = f2 // 2
    assert k == k2
    tm = _tile(m, tm, 16)
    tn = _tile(f, tn, 2 * LANE)
    nj = f // tn

    def body(a_ref, bg_ref, bu_ref, g_ref, u_ref, act_ref):
        av = a_ref[...].astype(MXU_DTYPE)
        gate = jnp.dot(av, bg_ref[...].astype(MXU_DTYPE), preferred_element_type=F32)
        up = jnp.dot(av, bu_ref[...].astype(MXU_DTYPE), preferred_element_type=F32)
        g_ref[...] = gate.astype(g_ref.dtype)
        u_ref[...] = up.astype(u_ref.dtype)
        act_ref[...] = (gate * _sigmoid(gate) * up).astype(act_ref.dtype)

    tile = pl.BlockSpec((tm, tn), lambda j, i: (i, j))
    out = _sds((m, f), BF16)
    return pl.pallas_call(
        body, name=name, grid=(nj, m // tm),
        in_specs=[pl.BlockSpec((tm, k), lambda j, i: (i, 0)), pl.BlockSpec((k, tn), lambda j, i: (0, j)),
                  pl.BlockSpec((k, tn), lambda j, i: (0, j + nj))],
        out_specs=[tile, tile, tile], out_shape=[out, out, out],
        compiler_params=_params(("parallel", "parallel")))(a, b, b)


def _rows(name, body, n_tiles, ins, outs):
    in_specs = [pl.BlockSpec(blk, imap) for (_, blk, imap) in ins]
    out_specs = [pl.BlockSpec(blk, imap) for (_, _, blk, imap) in outs]
    out_shape = [_sds(shape, dt) for (shape, dt, _, _) in outs]
    res = pl.pallas_call(body, name=name, grid=(n_tiles,), in_specs=in_specs, out_specs=out_specs,
                         out_shape=out_shape, compiler_params=_params(("arbitrary",)))(*[a for (a, _, _) in ins])
    return res


def _row_in(arr, tile, width=None, col=0, x_only_offset=None):
    width = arr.shape[1] if width is None else width
    if x_only_offset is None:
        return (arr, (tile, width), lambda i: (i, col))
    return (arr, (tile, width), lambda i: (jnp.maximum(i - x_only_offset, 0), col))


def _vec_in(arr, idx_fn):
    return (arr, (1, 1, arr.shape[2]), lambda i: (idx_fn(i), 0, 0))


def _rms(h):
    return lax.rsqrt(jnp.mean(h * h, axis=-1, keepdims=True) + NORM_EPS)


def _sigmoid(z):
    return 1.0 / (1.0 + jnp.exp(-z))


def _ada_pre_fwd(h, g6, mods, gi, mi, nct, tile, name):
    r, d = h.shape
    sel = lambda i: jnp.where(i >= nct, 1, 0)

    def body(h_ref, g_ref, sh_ref, sc_ref, u_ref):
        hh = h_ref[...]
        n = hh * _rms(hh) * g_ref[0]
        u_ref[...] = (n * (1.0 + sc_ref[0]) + sh_ref[0]).astype(u_ref.dtype)

    (u,) = _rows(name, body, r // tile,
                 [_row_in(h, tile), _vec_in(g6, lambda i: gi), _vec_in(mods, lambda i: sel(i) * 9 + 3 * mi),
                  _vec_in(mods, lambda i: sel(i) * 9 + 3 * mi + 1)],
                 [((r, d), BF16, (tile, d), lambda i: (i, 0))])
    return u


def _ada_pre_bwd(h, du, dres, g6, mods, gi, mi, nct, nsel, tile, name, dres_x_only=False, latent_dh_only=False):
    r, d = h.shape
    dh_rows = r - nct * tile if latent_dh_only else r
    dh_map = (lambda i: (jnp.maximum(i - nct, 0), 0)) if latent_dh_only else (lambda i: (i, 0))
    sel = lambda i: jnp.where(i >= nct, 1, 0) if nsel == 2 else 0
    msel = lambda i: jnp.where(i >= nct, 1, 0)
    off = nct if dres_x_only else None

    def body(h_ref, du_ref, dr_ref, g_ref, sc_ref, dh_ref, dg_ref, dsh_ref, dsc_ref):
        i = pl.program_id(0)
        hh = h_ref[...]
        rr = _rms(hh)
        g = g_ref[0]
        hn = hh * rr
        n = hn * g
        du_ = du_ref[...].astype(F32)
        dn = du_ * (1.0 + sc_ref[0])

        @pl.when(i == 0)
        def _():
            dg_ref[...] = jnp.zeros_like(dg_ref)

        @pl.when((i == 0) | (i == nct))
        def _():
            dsh_ref[...] = jnp.zeros_like(dsh_ref)
            dsc_ref[...] = jnp.zeros_like(dsc_ref)

        dg_ref[0] += jnp.sum(dn * hn, axis=0, keepdims=True)
        dsh_ref[0] += jnp.sum(du_, axis=0, keepdims=True)
        dsc_ref[0] += jnp.sum(du_ * n, axis=0, keepdims=True)
        t = dn * g
        dh = rr * t - hn * (rr * jnp.mean(t * hn, axis=-1, keepdims=True))
        if dres_x_only:
            dh_ref[...] = dh + jnp.where(i >= nct, dr_ref[...], 0.0)
        else:
            dh_ref[...] = dh + dr_ref[...]

    dh, dg, dsh, dsc = _rows(
        name, body, r // tile,
        [_row_in(h, tile), _row_in(du, tile), _row_in(dres, tile, x_only_offset=off), _vec_in(g6, lambda i: gi),
         _vec_in(mods, lambda i: msel(i) * 9 + 3 * mi + 1)],
        [((dh_rows, d), F32, (tile, d), dh_map), ((1, 1, d), F32, (1, 1, d), lambda i: (0, 0, 0)),
         ((nsel, 1, d), F32, (1, 1, d), lambda i: (sel(i), 0, 0)),
         ((nsel, 1, d), F32, (1, 1, d), lambda i: (sel(i), 0, 0))])
    return dh, dg, dsh, dsc


def _ada_post_fwd(h, o, g6, mods, gi, mi, res_w, nct, tile, name, h_tile_offset=0, then_pre=None, target=None):
    r, d = o.shape
    sel = lambda i: jnp.where(i >= nct, 1, 0)

    def body(h_ref, o_ref, g_ref, gt_ref, *refs):
        oo = o_ref[...].astype(F32)
        n = oo * _rms(oo) * g_ref[0]
        y = h_ref[...] + res_w * gt_ref[0] * n
        if target is not None:
            t_ref, dy_ref, l_ref = refs
            e = y - t_ref[...]
            dy_ref[...] = e * (1.0 / d)

            @pl.when(pl.program_id(0) == 0)
            def _():
                l_ref[...] = jnp.zeros_like(l_ref)

            l_ref[0] += jnp.sum(e * e, axis=0, keepdims=True)
        elif then_pre is not None:
            g2_ref, sh_ref, sc_ref, y_ref, u_ref = refs
            y_ref[...] = y
            u_ref[...] = (y * _rms(y) * g2_ref[0] * (1.0 + sc_ref[0]) + sh_ref[0]).astype(u_ref.dtype)
        else:
            refs[0][...] = y

    ins = [(h, (tile, d), lambda i: (i + h_tile_offset, 0)), _row_in(o, tile), _vec_in(g6, lambda i: gi),
           _vec_in(mods, lambda i: sel(i) * 9 + 3 * mi + 2)]
    row_out = lambda dt: ((r, d), dt, (tile, d), lambda i: (i, 0))
    if target is not None:
        ins.append(_row_in(target, tile))
        outs = [row_out(F32), ((1, 1, d), F32, (1, 1, d), lambda i: (0, 0, 0))]
    elif then_pre is not None:
        gi2, mi2 = then_pre
        ins += [_vec_in(g6, lambda i: gi2), _vec_in(mods, lambda i: sel(i) * 9 + 3 * mi2),
                _vec_in(mods, lambda i: sel(i) * 9 + 3 * mi2 + 1)]
        outs = [row_out(F32), row_out(BF16)]
    else:
        outs = [row_out(F32)]
    res = _rows(name, body, r // tile, ins, outs)
    return res[0] if len(res) == 1 else res


def _ada_post_bwd(dy, o, g6, mods, gi, mi, res_w, nct, nsel, tile, name):
    r, d = o.shape
    sel = lambda i: jnp.where(i >= nct, 1, 0) if nsel == 2 else 0
    msel = lambda i: jnp.where(i >= nct, 1, 0)

    def body(dy_ref, o_ref, g_ref, gt_ref, do_ref, dg_ref, dgt_ref):
        i = pl.program_id(0)
        oo = o_ref[...].astype(F32)
        rr = _rms(oo)
        g = g_ref[0]
        on = oo * rr
        dy_ = dy_ref[...] * res_w

        @pl.when(i == 0)
        def _():
            dg_ref[...] = jnp.zeros_like(dg_ref)

        @pl.when((i == 0) | (i == nct))
        def _():
            dgt_ref[...] = jnp.zeros_like(dgt_ref)

        dgt_ref[0] += jnp.sum(dy_ * (on * g), axis=0, keepdims=True)
        dn = dy_ * gt_ref[0]
        dg_ref[0] += jnp.sum(dn * on, axis=0, keepdims=True)
        t = dn * g
        do_ref[...] = (rr * t - on * (rr * jnp.mean(t * on, axis=-1, keepdims=True))).astype(do_ref.dtype)

    do, dg, dgt = _rows(
        name, body, r // tile,
        [_row_in(dy, tile), _row_in(o, tile), _vec_in(g6, lambda i: gi),
         _vec_in(mods, lambda i: msel(i) * 9 + 3 * mi + 2)],
        [((r, d), BF16, (tile, d), lambda i: (i, 0)), ((1, 1, d), F32, (1, 1, d), lambda i: (0, 0, 0)),
         ((nsel, 1, d), F32, (1, 1, d), lambda i: (sel(i), 0, 0))])
    return do, dg, dgt


def _swiglu_bwd(gate, up, da, tile, name):
    r, f = gate.shape

    def body(g_ref, u_ref, da_ref, dh_ref):
        gt = g_ref[...].astype(F32)
        d = da_ref[...].astype(F32)
        sg = _sigmoid(gt)
        dh_ref[:, :f] = (d * u_ref[...].astype(F32) * (sg * (1.0 + gt * (1.0 - sg)))).astype(dh_ref.dtype)
        dh_ref[:, f:] = (d * gt * sg).astype(dh_ref.dtype)

    (dh,) = _rows(name, body, r // tile, [_row_in(gate, tile), _row_in(up, tile), _row_in(da, tile)],
                  [((r, 2 * f), BF16, (tile, 2 * f), lambda i: (i, 0))])
    return dh


def _gelu_parts(y):
    c0 = math.sqrt(2.0 / math.pi)
    inner = c0 * (y + 0.044715 * y * y * y)
    th = jnp.tanh(inner)
    return th, c0 * (1.0 + 3 * 0.044715 * y * y)


def _ssm_out_fwd(y0, y1, hm, dskip, nct, tile, name):
    t_rows, s = y0.shape

    def body(y0_ref, y1_ref, u_ref, d_ref, a_ref):
        y = y0_ref[...] + y1_ref[...] + d_ref[0] * u_ref[...].astype(F32)
        th, _ = _gelu_parts(y)
        a_ref[...] = (0.5 * y * (1.0 + th)).astype(a_ref.dtype)

    (a,) = _rows(name, body, t_rows // tile,
                 [_row_in(y0, tile), _row_in(y1, tile), (hm, (tile, s), lambda i: (i + nct, 0)),
                  _vec_in(dskip, lambda i: 0)],
                 [((t_rows, s), BF16, (tile, s), lambda i: (i, 0))])
    return a


def _ssm_out_bwd(y0, y1, hm, dskip, da, nct, tile, name):
    t_rows, s = y0.shape

    def body(y0_ref, y1_ref, u_ref, d_ref, da_ref, dy_ref, du_ref, dd_ref):
        i = pl.program_id(0)
        u = u_ref[...].astype(F32)
        y = y0_ref[...] + y1_ref[...] + d_ref[0] * u
        th, dinner = _gelu_parts(y)
        dy = da_ref[...].astype(F32) * (0.5 * (1.0 + th) + 0.5 * y * (1.0 - th * th) * dinner)
        dy_ref[...] = dy
        du_ref[...] = dy * d_ref[0]

        @pl.when(i == 0)
        def _():
            dd_ref[...] = jnp.zeros_like(dd_ref)

        dd_ref[0] += jnp.sum(dy * u, axis=0, keepdims=True)

    dy, du, dd = _rows(name, body, t_rows // tile,
                       [_row_in(y0, tile), _row_in(y1, tile), (hm, (tile, s), lambda i: (i + nct, 0)),
                        _vec_in(dskip, lambda i: 0), _row_in(da, tile)],
                       [((t_rows, s), F32, (tile, s), lambda i: (i, 0)), ((t_rows, s), F32, (tile, s), lambda i: (i, 0)),
                        ((1, 1, s), F32, (1, 1, s), lambda i: (0, 0, 0))])
    return dy, du, dd


def _col_pieces(arr, off, width, tile, nct, unit=None):
    pw = math.gcd(off, width if unit is None else unit)
    specs = [(arr, (tile, pw), functools.partial(lambda i, cb: (i + nct, cb), cb=off // pw + p))
             for p in range(width // pw)]
    return specs, pw


def _ret_gate_fwd(o0, o1, hm, g_off, heads, dv, nct, tile, name):
    t_rows, w = o0.shape
    g_specs, pw = _col_pieces(hm, g_off, w, tile, nct)
    ng = len(g_specs)

    def body(o0_ref, o1_ref, *refs):
        g_refs, r_ref = refs[:ng], refs[ng]
        for hd in range(heads):
            cs = slice(hd * dv, (hd + 1) * dv)
            o = o0_ref[:, cs] + o1_ref[:, cs]
            lo = (hd * dv) % pw
            g = g_refs[(hd * dv) // pw][:, lo:lo + dv].astype(F32)
            r_ref[:, cs] = (g * _sigmoid(g) * (o * _rms(o))).astype(r_ref.dtype)

    (ri,) = _rows(name, body, t_rows // tile, [_row_in(o0, tile), _row_in(o1, tile)] + g_specs,
                  [((t_rows, w), BF16, (tile, w), lambda i: (i, 0))])
    return ri


def _ret_gate_bwd(o0, o1, hm, g_off, dri, heads, dv, nct, tile, name):
    t_rows, w = o0.shape
    g_specs, pw = _col_pieces(hm, g_off, w, tile, nct)
    ng = len(g_specs)

    def body(o0_ref, o1_ref, d_ref, *refs):
        g_refs, do_ref, dg_ref = refs[:ng], refs[ng], refs[ng + 1]
        for hd in range(heads):
            cs = slice(hd * dv, (hd + 1) * dv)
            o = o0_ref[:, cs] + o1_ref[:, cs]
            lo = (hd * dv) % pw
            g = g_refs[(hd * dv) // pw][:, lo:lo + dv].astype(F32)
            d = d_ref[:, cs].astype(F32)
            rr = _rms(o)
            on = o * rr
            sg = _sigmoid(g)
            dg_ref[:, cs] = (d * on * (sg * (1.0 + g * (1.0 - sg)))).astype(dg_ref.dtype)
            t = d * (g * sg)
            do_ref[:, cs] = rr * t - on * (rr * jnp.mean(t * on, axis=-1, keepdims=True))

    do, dg = _rows(name, body, t_rows // tile, [_row_in(o0, tile), _row_in(o1, tile), _row_in(dri, tile)] + g_specs,
                   [((t_rows, w), F32, (tile, w), lambda i: (i, 0)), ((t_rows, w), BF16, (tile, w), lambda i: (i, 0))])
    return do, dg


def _merge_fwd(gab, rb, hm, gs_off, nct, tile, name):
    t_rows, d = rb.shape
    specs, pw = _col_pieces(hm, gs_off, 2 * d, tile, nct, unit=d)
    npc = d // pw

    def body(gab_ref, rb_ref, *refs):
        gs_refs, gr_refs, m_ref = refs[:npc], refs[npc:2 * npc], refs[2 * npc]
        for p in range(npc):
            cs = slice(p * pw, (p + 1) * pw)
            ga = gab_ref[:, cs].astype(F32)
            gb = gab_ref[:, d + p * pw:d + (p + 1) * pw].astype(F32)
            m_ref[:, cs] = (_sigmoid(gs_refs[p][...].astype(F32)) * (ga * _sigmoid(gb))
                            + _sigmoid(gr_refs[p][...].astype(F32)) * rb_ref[:, cs].astype(F32)).astype(m_ref.dtype)

    (mg,) = _rows(name, body, t_rows // tile, [_row_in(gab, tile), _row_in(rb, tile)] + specs,
                  [((t_rows, d), BF16, (tile, d), lambda i: (i, 0))])
    return mg


def _merge_bwd(gab, rb, hm, gs_off, dm, nct, tile, name):
    t_rows, d = rb.shape
    specs, pw = _col_pieces(hm, gs_off, 2 * d, tile, nct, unit=d)
    npc = d // pw

    def body(gab_ref, rb_ref, dm_ref, *refs):
        gs_refs, gr_refs = refs[:npc], refs[npc:2 * npc]
        dgab_ref, drb_ref, dgs_ref, dgr_ref = refs[2 * npc:]
        for p in range(npc):
            cs = slice(p * pw, (p + 1) * pw)
            cs2 = slice(d + p * pw, d + (p + 1) * pw)
            ga = gab_ref[:, cs].astype(F32)
            gb = gab_ref[:, cs2].astype(F32)
            dmm = dm_ref[:, cs].astype(F32)
            ss = _sigmoid(gs_refs[p][...].astype(F32))
            sr = _sigmoid(gr_refs[p][...].astype(F32))
            sb = _sigmoid(gb)
            dbr = dmm * ss
            dgab_ref[:, cs] = (dbr * sb).astype(dgab_ref.dtype)
            dgab_ref[:, cs2] = (dbr * ga * sb * (1.0 - sb)).astype(dgab_ref.dtype)
            drb_ref[:, cs] = (dmm * sr).astype(drb_ref.dtype)
            dgs_ref[:, cs] = (dmm * (ga * sb) * ss * (1.0 - ss)).astype(dgs_ref.dtype)
            dgr_ref[:, cs] = (dmm * rb_ref[:, cs].astype(F32) * sr * (1.0 - sr)).astype(dgr_ref.dtype)

    return _rows(name, body, t_rows // tile, [_row_in(gab, tile), _row_in(rb, tile), _row_in(dm, tile)] + specs,
                 [((t_rows, 2 * d), BF16, (tile, 2 * d), lambda i: (i, 0)), ((t_rows, d), BF16, (tile, d), lambda i: (i, 0)),
                  ((t_rows, d), BF16, (tile, d), lambda i: (i, 0)), ((t_rows, d), BF16, (tile, d), lambda i: (i, 0))])


def _assemble_dhm(dus, dq0, dq1, dk0, dk1, dv0, dv1, dg, dgs, dgr, nct, tile, name):
    r, s = dus.shape
    qk = dq0.shape[1]
    vw = dv0.shape[1]
    d = dgs.shape[1]
    mi = s + 2 * qk + 2 * vw + 2 * d
    c_q, c_k, c_v, c_g, c_gs, c_gr = s, s + qk, s + 2 * qk, s + 2 * qk + vw, s + 2 * qk + 2 * vw, s + 2 * qk + 2 * vw + d

    def body(dus_ref, dq0_ref, dq1_ref, dk0_ref, dk1_ref, dv0_ref, dv1_ref, dg_ref, dgs_ref, dgr_ref, o_ref):
        i = pl.program_id(0)
        lat = i >= nct
        o_ref[:, :s] = dus_ref[...].astype(o_ref.dtype)
        o_ref[:, c_q:c_k] = (dq0_ref[...].astype(F32) + dq1_ref[...].astype(F32)).astype(o_ref.dtype)
        o_ref[:, c_k:c_v] = (dk0_ref[...].astype(F32) + dk1_ref[...].astype(F32)).astype(o_ref.dtype)
        o_ref[:, c_v:c_g] = (dv0_ref[...].astype(F32) + dv1_ref[...].astype(F32)).astype(o_ref.dtype)
        o_ref[:, c_g:c_gs] = jnp.where(lat, dg_ref[...], 0.0).astype(o_ref.dtype)
        o_ref[:, c_gs:c_gr] = jnp.where(lat, dgs_ref[...], 0.0).astype(o_ref.dtype)
        o_ref[:, c_gr:] = jnp.where(lat, dgr_ref[...], 0.0).astype(o_ref.dtype)

    (out,) = _rows(name, body, r // tile,
                   [_row_in(dus, tile), _row_in(dq0, tile), _row_in(dq1, tile), _row_in(dk0, tile), _row_in(dk1, tile),
                    _row_in(dv0, tile), _row_in(dv1, tile), _row_in(dg, tile, x_only_offset=nct),
                    _row_in(dgs, tile, x_only_offset=nct), _row_in(dgr, tile, x_only_offset=nct)],
                   [((r, mi), BF16, (tile, mi), lambda i: (i, 0))])
    return out


def _silu_rows(v, name):
    def body(v_ref, o_ref):
        z = v_ref[...]
        o_ref[...] = z * _sigmoid(z)

    (o,) = _rows(name, body, 1, [_row_in(v, v.shape[0])], [(v.shape, F32, v.shape, lambda i: (0, 0))])
    return o


def _silu_grad_rows(v, dv, name):
    def body(v_ref, d_ref, o_ref):
        z = v_ref[...]
        sg = _sigmoid(z)
        o_ref[...] = d_ref[...] * (sg * (1.0 + z * (1.0 - sg)))

    (o,) = _rows(name, body, 1, [_row_in(v, v.shape[0]), _row_in(dv, v.shape[0])],
                 [(v.shape, F32, v.shape, lambda i: (0, 0))])
    return o


def _sum_leading(g8, name):
    n, r, c = g8.shape
    tile = _tile(r, 256, SUBLANE)

    def body(g_ref, o_ref):
        acc = g_ref[0]
        for j in range(1, n):
            acc = acc + g_ref[j]
        o_ref[...] = acc

    (o,) = _rows(name, body, r // tile, [(g8, (n, tile, c), lambda i: (0, i, 0))],
                 [((r, c), F32, (tile, c), lambda i: (i, 0))])
    return o


def _pair_sum(g, recv, axis, name):
    n, br, bc = recv.shape
    tile = _tile(br, 256, 16)
    nrt = br // tile
    core = lax.axis_index("c").astype(jnp.int32).reshape(1)

    def body(c_ref, g_ref, r_ref, o_ref):
        o_ref[0] = (g_ref[...].astype(F32) + r_ref[0].astype(F32)).astype(o_ref.dtype)

    if axis == 1:
        g_spec = pl.BlockSpec((tile, bc), lambda q, i, c_ref: (i, 2 * q + c_ref[0]))
    else:
        g_spec = pl.BlockSpec((tile, bc), lambda q, i, c_ref: ((2 * q + c_ref[0]) * nrt + i, 0))
    slot = pl.BlockSpec((1, tile, bc), lambda q, i, c_ref: (q, i, 0))
    return pl.pallas_call(
        body, name=name, out_shape=_sds((n, br, bc), recv.dtype),
        grid_spec=pltpu.PrefetchScalarGridSpec(num_scalar_prefetch=1, grid=(n, nrt), in_specs=[g_spec, slot],
                                               out_specs=slot),
        compiler_params=_params(("arbitrary", "arbitrary")))(core, g, recv)


def _adam_math(w, m, v, g):
    c1 = 1.0 / (1.0 - ADAM_B1 ** ADAM_STEP)
    c2 = 1.0 / (1.0 - ADAM_B2 ** ADAM_STEP)
    mm = ADAM_B1 * m + (1.0 - ADAM_B1) * g
    vv = ADAM_B2 * v + (1.0 - ADAM_B2) * (g * g)
    return -ADAM_LR * ((mm * c1) / (jnp.sqrt(vv * c2) + ADAM_EPS) + ADAM_WD * w), mm, vv


def _adamw(w, m, v, gparts, name):
    r, c = w.shape
    n = gparts.shape[0]
    tile = _tile(r, 256, 16)

    def body(w_ref, m_ref, v_ref, g_ref, go_ref, d_ref, mo_ref, vo_ref):
        g = g_ref[0].astype(F32)
        for j in range(1, n):
            g = g + g_ref[j].astype(F32)
        go_ref[...] = g
        d_ref[...], mo_ref[...], vo_ref[...] = _adam_math(w_ref[...], m_ref[...], v_ref[...], g)

    rs = lambda arr: _row_in(arr, tile)
    out = ((r, c), F32, (tile, c), lambda i: (i, 0))
    return _rows(name, body, r // tile, [rs(w), rs(m), rs(v), (gparts, (n, tile, c), lambda i: (0, i, 0))],
                 [out, out, out, out])


def _adamw_scattered(w, m, v, layer, p, recv, name, filled=None):
    nl, r, c = w.shape
    n = recv.shape[0]
    tile = _tile(r, 256, 16)
    chip = (2 * lax.axis_index("x") + lax.axis_index("y")).astype(jnp.int32).reshape(1)
    n_prev = 0 if filled is None else len(filled)

    def body(q_ref, w_ref, m_ref, v_ref, p_ref, g_ref, *rest):
        go_ref, d_ref, mo_ref, vo_ref = rest[n_prev:]
        g = p_ref[0].astype(F32)
        for j in range(n):
            g = g + g_ref[j].astype(F32)
        go_ref[0] = g
        d_ref[0], mo_ref[0], vo_ref[0] = _adam_math(w_ref[0], m_ref[0], v_ref[0], g)

    slab = pl.BlockSpec((1, tile, c), lambda i, q_ref: (layer, i, 0))
    anywhere = pl.BlockSpec(memory_space=pl.ANY)
    out = _sds((nl, r, c), F32)
    prev = [] if filled is None else list(filled)
    return pl.pallas_call(
        body, name=name, out_shape=[out, out, out, out],
        grid_spec=pltpu.PrefetchScalarGridSpec(
            num_scalar_prefetch=1, grid=(r // tile,),
            in_specs=[slab, slab, slab, pl.BlockSpec((1, tile, c), lambda i, q_ref: (q_ref[0], i, 0)),
                      pl.BlockSpec((n, tile, c), lambda i, q_ref: (0, i, 0))] + [anywhere] * n_prev,
            out_specs=[slab, slab, slab, slab]),
        input_output_aliases={6 + j: j for j in range(n_prev)},
        compiler_params=_params(("arbitrary",)))(chip, w, m, v, p, recv, *prev)


def _cmul(ar, ai, br, bi):
    return ar * br - ai * bi, ar * bi + ai * br


def _cpow(ar, ai, n):
    pr, pi = jnp.ones_like(ar), jnp.zeros_like(ar)
    br, bi = ar, ai
    while n:
        if n & 1:
            pr, pi = _cmul(pr, pi, br, bi)
        n >>= 1
        if n:
            br, bi = _cmul(br, bi, br, bi)
    return pr, pi


def _s5_scan_into(x_ref, ar1, ai1, ns, fin_ref, hin_ref, reverse, paired=None):
    st = ar1.shape[1]
    ar = jnp.broadcast_to(ar1, (N_SEG, st))
    ai = jnp.broadcast_to(ai1, (N_SEG, st))
    zero = jnp.zeros((N_SEG, st), F32)

    def slab(k):
        if isinstance(k, int):
            return pl.ds(k * N_SEG, N_SEG)
        return pl.ds(pl.multiple_of(k * N_SEG, N_SEG), N_SEG)

    def pass1(j, carry):
        hr, hi = carry
        k = ns - 1 - j if reverse else j
        nr, ni = _cmul(ar, ai, hr, hi)
        return nr + x_ref[slab(k), :st], ni + x_ref[slab(k), st:]

    fr, fi = lax.fori_loop(0, ns, pass1, (zero, zero))
    fin_ref[:, :st] = fr
    fin_ref[:, st:] = fi
    pr, pi = _cpow(ar1, ai1, ns)
    order = list(range(N_SEG - 1, -1, -1)) if reverse else list(range(N_SEG))
    hin_ref[order[0]:order[0] + 1, :] = jnp.zeros((1, 2 * st), F32)
    for a_, b_ in zip(order[:-1], order[1:]):
        cr, ci = _cmul(pr, pi, hin_ref[a_:a_ + 1, :st], hin_ref[a_:a_ + 1, st:])
        hin_ref[b_:b_ + 1, :st] = cr + fin_ref[a_:a_ + 1, :st]
        hin_ref[b_:b_ + 1, st:] = ci + fin_ref[a_:a_ + 1, st:]

    def step2(k, hr, hi):
        nr, ni = _cmul(ar, ai, hr, hi)
        nr = nr + x_ref[slab(k), :st]
        ni = ni + x_ref[slab(k), st:]
        x_ref[slab(k), :st] = nr
        x_ref[slab(k), st:] = ni
        return nr, ni

    if paired is None:
        def pass2(j, carry):
            return step2(ns - 1 - j if reverse else j, *carry)

        lax.fori_loop(0, ns, pass2, (hin_ref[:, :st], hin_ref[:, st:]))
        return None
    p_ref, p_edge_ref, shift = paired

    def pass2_paired(j, carry):
        hr, hi, acr, aci = carry
        k = ns - 1 - j if reverse else j
        nr, ni = step2(k, hr, hi)
        p_r, p_i = p_ref[slab(k + shift), :st], p_ref[slab(k + shift), st:]
        return nr, ni, acr + nr * p_r + ni * p_i, aci + ni * p_r - nr * p_i

    hr, hi, acr, aci = lax.fori_loop(0, ns - 1, pass2_paired, (hin_ref[:, :st], hin_ref[:, st:], zero, zero))
    nr, ni = step2(0 if reverse else ns - 1, hr, hi)
    p_r, p_i = p_edge_ref[:, :st], p_edge_ref[:, st:]
    return acr + nr * p_r + ni * p_i, aci + ni * p_r - nr * p_i


def _s5_specs(r, ch, st):
    u_spec = pl.BlockSpec((r, ch), lambda j: (0, j // 2))
    w_spec = pl.BlockSpec((1, ch, 2 * st), lambda j: (j, 0, 0))
    c_spec = pl.BlockSpec((1, 2 * st, ch), lambda j: (j, 0, 0))
    a_spec = pl.BlockSpec((1, 2, st), lambda j: (j, 0, 0))
    return u_spec, w_spec, c_spec, a_spec


def _s5_fwd(up, w, c, a, rev, name):
    r, s = up.shape
    nh, ch, st2 = w.shape
    st = st2 // 2
    ns = r // N_SEG
    nb = r // N_DEV
    u_spec, w_spec, c_spec, a_spec = _s5_specs(r, ch, st)

    def body(u_ref, w_ref, c_ref, a_ref, y_ref, x, fin, hin):
        j = pl.program_id(0)
        w_b = w_ref[0].astype(MXU_DTYPE)
        c_b = c_ref[0].astype(MXU_DTYPE)
        for rb in range(N_DEV):
            rows = slice(rb * nb, (rb + 1) * nb)
            x[rows, :] = jnp.dot(u_ref[rows, :].astype(MXU_DTYPE), w_b, preferred_element_type=F32)
        _s5_scan_into(x, a_ref[0, 0:1, :], a_ref[0, 1:2, :], ns, fin, hin, rev)
        for rb in range(N_DEV):
            rows = slice(rb * nb, (rb + 1) * nb)
            yb = jnp.dot(x[rows, :].astype(MXU_DTYPE), c_b, preferred_element_type=F32)

            @pl.when(j % 2 == 0)
            def _():
                y_ref[rows, :] = yb

            @pl.when(j % 2 == 1)
            def _():
                y_ref[rows, :] += yb

    small = pltpu.VMEM((N_SEG, st2), F32)
    return pl.pallas_call(
        body, name=name, grid=(nh,), in_specs=[u_spec, w_spec, c_spec, a_spec],
        out_specs=pl.BlockSpec((r, ch), lambda j: (0, j // 2)), out_shape=_sds((r, s), F32),
        scratch_shapes=[pltpu.VMEM((r, st2), F32), small, small],
        compiler_params=_params(("arbitrary",)))(up, w, c, a)


def _s5_bwd(up, dyp, w, c, a, rev, name):
    r, s = up.shape
    nh, ch, st2 = w.shape
    st = st2 // 2
    ns = r // N_SEG
    nb = r // N_DEV
    u_spec, w_spec, c_spec, a_spec = _s5_specs(r, ch, st)
    nt = (((1,), (1,)), ((), ()))
    tn = (((0,), (0,)), ((), ()))

    def body(u_ref, dy_ref, w_ref, c_ref, a_ref, du_ref, dw_ref, dc_ref, da_ref, h, g, fin, sin_, ein):
        j = pl.program_id(0)
        w_b = w_ref[0].astype(MXU_DTYPE)
        c_b = c_ref[0].astype(MXU_DTYPE)
        for rb in range(N_DEV):
            rows = slice(rb * nb, (rb + 1) * nb)
            h[rows, :] = jnp.dot(u_ref[rows, :].astype(MXU_DTYPE), w_b, preferred_element_type=F32)
        ar1, ai1 = a_ref[0, 0:1, :], a_ref[0, 1:2, :]
        _s5_scan_into(h, ar1, ai1, ns, fin, sin_, rev)
        dc = jnp.zeros((st2, ch), F32)
        for rb in range(N_DEV):
            rows = slice(rb * nb, (rb + 1) * nb)
            dyb = dy_ref[rows, :].astype(MXU_DTYPE)
            g[rows, :] = lax.dot_general(dyb, c_b, nt, preferred_element_type=F32)
            dc += lax.dot_general(h[rows, :].astype(MXU_DTYPE), dyb, tn, preferred_element_type=F32)
        dc_ref[0] = dc
        acr, aci = _s5_scan_into(g, ar1, -ai1, ns, fin, ein, not rev, paired=(h, sin_, 1 if rev else -1))
        da_ref[0, 0:1, :] = jnp.sum(acr, axis=0, keepdims=True)
        da_ref[0, 1:2, :] = jnp.sum(aci, axis=0, keepdims=True)
        dw = jnp.zeros((ch, st2), F32)
        for rb in range(N_DEV):
            rows = slice(rb * nb, (rb + 1) * nb)
            gb = g[rows, :].astype(MXU_DTYPE)
            dub = lax.dot_general(gb, w_b, nt, preferred_element_type=F32)
            dw += lax.dot_general(u_ref[rows, :].astype(MXU_DTYPE), gb, tn, preferred_element_type=F32)

            @pl.when(j % 2 == 0)
            def _():
                du_ref[rows, :] = dub

            @pl.when(j % 2 == 1)
            def _():
                du_ref[rows, :] += dub

        dw_ref[0] = dw

    small = pltpu.VMEM((N_SEG, st2), F32)
    big = pltpu.VMEM((r, st2), F32)
    return pl.pallas_call(
        body, name=name, grid=(nh,), in_specs=[u_spec, u_spec, w_spec, c_spec, a_spec],
        out_specs=[pl.BlockSpec((r, ch), lambda j: (0, j // 2)), w_spec, c_spec, a_spec],
        out_shape=[_sds((r, s), F32), _sds(w.shape, F32), _sds(c.shape, F32), _sds(a.shape, F32)],
        scratch_shapes=[big, big, small, small, small],
        compiler_params=_params(("arbitrary",)))(up, dyp, w, c, a)


def _rope(t, cos, sin):
    quarter = t.shape[1] // 4
    lane = lax.broadcasted_iota(jnp.int32, t.shape, 1)
    first = (lane // quarter) % 2 == 0
    partner = jnp.where(first, pltpu.roll(t, t.shape[1] - quarter, 1), pltpu.roll(t, quarter, 1))
    return t * cos + partner * sin


def _rope_t(d, cos, sin):
    quarter = d.shape[1] // 4
    ds_ = d * sin
    lane = lax.broadcasted_iota(jnp.int32, d.shape, 1)
    first = (lane // quarter) % 2 == 0
    partner = jnp.where(first, pltpu.roll(ds_, d.shape[1] - quarter, 1), pltpu.roll(ds_, quarter, 1))
    return d * cos + partner


def _chunk_of_step(s, nch, ncc, rev):
    if not rev:
        return s
    return jnp.where(s < ncc, ncc - 1 - s, nch + ncc - 1 - s)


def _heads_per_step(heads, dk, dv, q_off):
    v_off = q_off + 2 * heads * dk
    for hpg in range(heads, 0, -1):
        if heads % hpg == 0 and q_off % (hpg * dk) == 0:
            piece = math.gcd(v_off, hpg * dv)
            if piece % dv == 0:
                return hpg, piece
    return 1, dv


def _v_specs(hpg, dv, piece, v_off, ch, chunk_of):
    n_pieces = hpg * dv // piece
    return [pl.BlockSpec((ch, piece), functools.partial(
        lambda h, s, p: (chunk_of(s), v_off // piece + h * n_pieces + p), p=p)) for p in range(n_pieces)]


def _v_of_head(v_refs, hl, dv, piece):
    lo = (hl * dv) % piece
    return v_refs[(hl * dv) // piece][:, lo:lo + dv]


def _ret_fwd(hm, cos, sin, decay, wend, win, gch, heads, dk, dv, q_off, ncc, rev, name):
    r = hm.shape[0]
    ch = RET_CHUNK
    nch = r // ch
    t_rows = r - ncc * ch
    hpg, piece = _heads_per_step(heads, dk, dv, q_off)
    qb, kb = q_off // (hpg * dk), (q_off + heads * dk) // (hpg * dk)
    q_scale = dk ** -0.5
    nt = (((1,), (1,)), ((), ()))
    tn = (((0,), (0,)), ((), ()))
    cof = lambda s: _chunk_of_step(s, nch, ncc, rev)
    v_specs = _v_specs(hpg, dv, piece, q_off + 2 * heads * dk, ch, cof)
    nv = len(v_specs)

    def body(q_ref, k_ref, *refs):
        v_refs = refs[:nv]
        cos_ref, sin_ref, dec_ref, we_ref, wi_ref, g_ref, o_ref, sin_out, st = refs[nv:]
        s = pl.program_id(1)

        @pl.when(s == 0)
        def _():
            st[...] = jnp.zeros_like(st)

        cos_, sin_ = cos_ref[...], sin_ref[...]
        for hl in range(hpg):
            ks, vs = slice(hl * dk, (hl + 1) * dk), slice(hl * dv, (hl + 1) * dv)
            q = _rope(q_ref[:, ks].astype(F32), cos_, sin_) * q_scale
            k = _rope(k_ref[:, ks].astype(F32), cos_, sin_)
            v = _v_of_head(v_refs, hl, dv, piece).astype(MXU_DTYPE)
            s_cur = st[hl]
            sin_out[hl, 0] = s_cur
            kw = (k * we_ref[hl]).astype(MXU_DTYPE)
            qw = (q * wi_ref[hl]).astype(MXU_DTYPE)
            scores = lax.dot_general(q.astype(MXU_DTYPE), k.astype(MXU_DTYPE), nt,
                                     preferred_element_type=F32) * dec_ref[hl]
            o_ref[:, vs] = (jnp.dot(scores.astype(MXU_DTYPE), v, preferred_element_type=F32)
                            + jnp.dot(qw, s_cur.astype(MXU_DTYPE), preferred_element_type=F32))
            st[hl] = g_ref[hl] * s_cur + lax.dot_general(kw, v, tn, preferred_element_type=F32)

    tab = lambda w: pl.BlockSpec((hpg, ch, w), lambda h, s: (h, 0, 0))
    return pl.pallas_call(
        body, name=name, grid=(heads // hpg, nch),
        in_specs=[pl.BlockSpec((ch, hpg * dk), lambda h, s: (cof(s), qb + h)),
                  pl.BlockSpec((ch, hpg * dk), lambda h, s: (cof(s), kb + h))] + v_specs +
                 [pl.BlockSpec((ch, dk), lambda h, s: (cof(s), 0)),
                  pl.BlockSpec((ch, dk), lambda h, s: (cof(s), 0)),
                  tab(ch), tab(dk), tab(dk), tab(dv)],
        out_specs=[pl.BlockSpec((ch, hpg * dv), lambda h, s: (jnp.maximum(cof(s) - ncc, 0) if not rev
                                                               else jnp.where(s < ncc, nch - ncc - 1, cof(s) - ncc), h)),
                   pl.BlockSpec((hpg, 1, dk, dv), lambda h, s: (h, s, 0, 0))],
        out_shape=[_sds((t_rows, heads * dv), F32), _sds((heads, nch, dk, dv), F32)],
        scratch_shapes=[pltpu.VMEM((hpg, dk, dv), F32)],
        compiler_params=_params(("parallel", "arbitrary")))(hm, hm, *([hm] * nv), cos, sin, decay, wend, win, gch)


def _ret_bwd(hm, cos, sin, decay, wend, win, gch, s_in, do, heads, dk, dv, q_off, ncc, rev, name):
    r = hm.shape[0]
    ch = RET_CHUNK
    nch = r // ch
    hpg, piece = _heads_per_step(heads, dk, dv, q_off)
    qb, kb = q_off // (hpg * dk), (q_off + heads * dk) // (hpg * dk)
    q_scale = dk ** -0.5
    nt = (((1,), (1,)), ((), ()))
    tn = (((0,), (0,)), ((), ()))
    cof = lambda rr: _chunk_of_step(nch - 1 - rr, nch, ncc, rev)
    v_specs = _v_specs(hpg, dv, piece, q_off + 2 * heads * dk, ch, cof)
    nv = len(v_specs)

    def body(q_ref, k_ref, *refs):
        v_refs = refs[:nv]
        (cos_ref, sin_ref, dec_ref, we_ref, wi_ref, g_ref, sin_ref2, do_ref,
         dq_ref, dk_ref, dv_ref, ddec_ref, dwe_ref, dwi_ref, dg_ref, dst) = refs[nv:]
        rr = pl.program_id(1)
        n = cof(rr)

        @pl.when(rr == 0)
        def _():
            dst[...] = jnp.zeros_like(dst)
            ddec_ref[...] = jnp.zeros_like(ddec_ref)
            dwe_ref[...] = jnp.zeros_like(dwe_ref)
            dwi_ref[...] = jnp.zeros_like(dwi_ref)
            dg_ref[...] = jnp.zeros_like(dg_ref)

        cos_, sin_ = cos_ref[...], sin_ref[...]
        for hl in range(hpg):
            ks, vs = slice(hl * dk, (hl + 1) * dk), slice(hl * dv, (hl + 1) * dv)
            q = _rope(q_ref[:, ks].astype(F32), cos_, sin_) * q_scale
            k = _rope(k_ref[:, ks].astype(F32), cos_, sin_)
            v = _v_of_head(v_refs, hl, dv, piece).astype(MXU_DTYPE)
            qb_, kb_ = q.astype(MXU_DTYPE), k.astype(MXU_DTYPE)
            kw = (k * we_ref[hl]).astype(MXU_DTYPE)
            qw = (q * wi_ref[hl]).astype(MXU_DTYPE)
            sraw = lax.dot_general(qb_, kb_, nt, preferred_element_type=F32)
            scores = (sraw * dec_ref[hl]).astype(MXU_DTYPE)
            d_o = jnp.where(n >= ncc, do_ref[:, vs], 0.0).astype(MXU_DTYPE)
            s_n = sin_ref2[hl, 0]
            s_nb = s_n.astype(MXU_DTYPE)
            ds1 = dst[hl]
            ds1b = ds1.astype(MXU_DTYPE)
            dsc = lax.dot_general(d_o, v, nt, preferred_element_type=F32)
            dsr = (dsc * dec_ref[hl]).astype(MXU_DTYPE)
            ddec_ref[hl] += dsc * sraw
            t1 = lax.dot_general(d_o, s_nb, nt, preferred_element_type=F32)
            dq_r = jnp.dot(dsr, kb_, preferred_element_type=F32) + t1 * wi_ref[hl]
            dwi_ref[hl] += t1 * q
            t2 = lax.dot_general(v, ds1b, nt, preferred_element_type=F32)
            dk_r = lax.dot_general(dsr, qb_, tn, preferred_element_type=F32) + t2 * we_ref[hl]
            dwe_ref[hl] += t2 * k
            dv_ref[:, vs] = (lax.dot_general(scores, d_o, tn, preferred_element_type=F32)
                             + jnp.dot(kw, ds1b, preferred_element_type=F32)).astype(dv_ref.dtype)
            dg_ref[hl] += ds1 * s_n
            dst[hl] = g_ref[hl] * ds1 + lax.dot_general(qw, d_o, tn, preferred_element_type=F32)
            dq_ref[:, ks] = (_rope_t(dq_r, cos_, sin_) * q_scale).astype(dq_ref.dtype)
            dk_ref[:, ks] = _rope_t(dk_r, cos_, sin_).astype(dk_ref.dtype)

    tab = lambda w: pl.BlockSpec((hpg, ch, w), lambda h, rr: (h, 0, 0))
    return pl.pallas_call(
        body, name=name, grid=(heads // hpg, nch),
        in_specs=[pl.BlockSpec((ch, hpg * dk), lambda h, rr: (cof(rr), qb + h)),
                  pl.BlockSpec((ch, hpg * dk), lambda h, rr: (cof(rr), kb + h))] + v_specs +
                 [pl.BlockSpec((ch, dk), lambda h, rr: (cof(rr), 0)),
                  pl.BlockSpec((ch, dk), lambda h, rr: (cof(rr), 0)),
                  tab(ch), tab(dk), tab(dk), tab(dv),
                  pl.BlockSpec((hpg, 1, dk, dv), lambda h, rr: (h, nch - 1 - rr, 0, 0)),
                  pl.BlockSpec((ch, hpg * dv), lambda h, rr: (jnp.maximum(cof(rr) - ncc, 0), h))],
        out_specs=[pl.BlockSpec((ch, hpg * dk), lambda h, rr: (cof(rr), h)),
                   pl.BlockSpec((ch, hpg * dk), lambda h, rr: (cof(rr), h)),
                   pl.BlockSpec((ch, hpg * dv), lambda h, rr: (cof(rr), h)),
                   tab(ch), tab(dk), tab(dk), tab(dv)],
        out_shape=[_sds((r, heads * dk), BF16), _sds((r, heads * dk), BF16), _sds((r, heads * dv), BF16),
                   _sds(decay.shape, F32), _sds(wend.shape, F32), _sds(win.shape, F32), _sds(gch.shape, F32)],
        scratch_shapes=[pltpu.VMEM((hpg, dk, dv), F32)],
        compiler_params=_params(("parallel", "arbitrary")))(hm, hm, *([hm] * nv), cos, sin, decay, wend, win, gch, s_in, do)


_HBM = pl.BlockSpec(memory_space=pltpu.HBM)
_MESH = pl.DeviceIdType.MESH
ALL_GATHER_COLLECTIVE_ID = 1
SIBLING_COLLECTIVE_ID = 2
CHIPS_COLLECTIVE_ID = 3


def _axis_slice(ref, axis, start, size):
    idx = [slice(None)] * len(ref.shape)
    idx[axis] = pl.ds(start, size)
    return ref.at[tuple(idx)]


def _sibling_and_chip_peers():
    x, y, c = lax.axis_index("x"), lax.axis_index("y"), lax.axis_index("c")
    return [(x, y, 1 - c), (1 - x, y, c), (x, 1 - y, c), (1 - x, 1 - y, c)]


def _launch_exchange(body, name, operand, out_shape, sems, peers_fn, collective_id, on_sequencer):
    if not on_sequencer:
        return pl.pallas_call(body, name=name, out_shape=out_shape, in_specs=[_HBM], out_specs=_HBM,
                              scratch_shapes=sems)(operand)

    def sequencer_body(in_ref, out_ref, *sem_refs):
        peers = peers_fn()
        barrier = pltpu.get_barrier_semaphore()
        for peer in peers:
            pl.semaphore_signal(barrier, inc=1, device_id=peer, device_id_type=_MESH)
        pl.semaphore_wait(barrier, len(peers))
        body(in_ref, out_ref, *sem_refs)

    return pl.kernel(sequencer_body, out_type=out_shape, name=name,
                     mesh=plsc.ScalarSubcoreMesh(axis_name="sequencer", num_cores=1), scratch_types=sems,
                     compiler_params=pltpu.CompilerParams(collective_id=collective_id))(operand)


def _all_gather(shard, axis, name, on_sequencer=False):
    m = shard.shape[axis]
    out_shape = list(shard.shape)
    out_shape[axis] = N_DEV * m

    def body(x_ref, out_ref, send_sems, recv_sems, local_sem):
        x, y, c = lax.axis_index("x"), lax.axis_index("y"), lax.axis_index("c")
        me, sibling = (x, y, c), (x, y, 1 - c)
        chips = [(1 - x, y), (x, 1 - y), (1 - x, 1 - y)]

        def block(px, py, pc):
            return _axis_slice(out_ref, axis, (4 * px + 2 * py + pc) * m, m)

        def copy(k, blk, to, src=None):
            return pltpu.make_async_remote_copy(
                src_ref=block(*blk) if src is None else src, dst_ref=block(*blk), send_sem=send_sems.at[k],
                recv_sem=recv_sems.at[k], device_id=to, device_id_type=_MESH)

        mine = pltpu.make_async_copy(x_ref, block(*me), local_sem)
        mine.start()
        first = [copy(0, me, sibling, src=x_ref)]
        first += [copy(1 + j, me, (*chip, c), src=x_ref) for j, chip in enumerate(chips)]
        for cp in first:
            cp.start()
        passed = [copy(4 + j, (*chip, c), sibling) for j, chip in enumerate(chips)]
        for j, chip in enumerate(chips):
            copy(1 + j, (*chip, c), me).wait_recv()
            passed[j].start()
        copy(0, sibling, me).wait_recv()
        for j, chip in enumerate(chips):
            copy(4 + j, (*chip, 1 - c), me).wait_recv()
        for cp in first + passed:
            cp.wait_send()
        mine.wait()

    return _launch_exchange(
        body, name, shard, _sds(out_shape, shard.dtype),
        [pltpu.SemaphoreType.DMA((7,)), pltpu.SemaphoreType.DMA((7,)), pltpu.SemaphoreType.DMA(())],
        _sibling_and_chip_peers, ALL_GATHER_COLLECTIVE_ID, on_sequencer)


def _rs_sibling(g, axis, name, on_sequencer=False):
    m = g.shape[axis] // N_DEV
    blk_shape = list(g.shape)
    blk_shape[axis] = m
    n_chips = N_DEV // 2

    def body(g_ref, recv_ref, send_sems, recv_sems):
        x, y, c = lax.axis_index("x"), lax.axis_index("y"), lax.axis_index("c")
        sibling = (x, y, 1 - c)
        send = [pltpu.make_async_remote_copy(
            src_ref=_axis_slice(g_ref, axis, (2 * q + 1 - c) * m, m), dst_ref=recv_ref.at[q],
            send_sem=send_sems.at[q], recv_sem=recv_sems.at[q], device_id=sibling, device_id_type=_MESH)
            for q in range(n_chips)]
        for cp in send:
            cp.start()
        for cp in send:
            cp.wait_recv()
        for cp in send:
            cp.wait_send()

    return _launch_exchange(
        body, name, g, _sds([n_chips] + blk_shape, g.dtype),
        [pltpu.SemaphoreType.DMA((n_chips,)), pltpu.SemaphoreType.DMA((n_chips,))],
        lambda: _sibling_and_chip_peers()[:1], SIBLING_COLLECTIVE_ID, on_sequencer)


def _rs_chips(p, name, on_sequencer=False):
    n_peers = p.shape[0] - 1

    def body(p_ref, out_ref, send_sems, recv_sems):
        x, y, c = lax.axis_index("x"), lax.axis_index("y"), lax.axis_index("c")
        chips = [(1 - x, y), (x, 1 - y), (1 - x, 1 - y)]
        send = [pltpu.make_async_remote_copy(
            src_ref=p_ref.at[2 * cx + cy], dst_ref=out_ref.at[j], send_sem=send_sems.at[j],
            recv_sem=recv_sems.at[j], device_id=(cx, cy, c), device_id_type=_MESH)
            for j, (cx, cy) in enumerate(chips)]
        for cp in send:
            cp.start()
        for cp in send:
            cp.wait_recv()
        for cp in send:
            cp.wait_send()

    return _launch_exchange(
        body, name, p, _sds((n_peers,) + p.shape[1:], p.dtype),
        [pltpu.SemaphoreType.DMA((n_peers,)), pltpu.SemaphoreType.DMA((n_peers,))],
        lambda: _sibling_and_chip_peers()[1:], CHIPS_COLLECTIVE_ID, on_sequencer)


def _reduce_scatter(g, axis, name):
    sib = _rs_sibling(g, axis, name + "_d2d", on_sequencer=True)
    p = _pair_sum(g, sib, axis, name + "_pair")
    return p, _rs_chips(p, name + "_ici", on_sequencer=True)


def _s5_tables(lam_re, lam_im, log_step, b_re, b_im, c_re, c_im):
    nd, g, p, cg = b_re.shape
    step = jnp.exp(log_step)[..., None]
    mag = jnp.exp(lam_re * step)
    a_re, a_im = mag * jnp.cos(lam_im * step), mag * jnp.sin(lam_im * step)
    den = lam_re * lam_re + lam_im * lam_im
    num_re, num_im = a_re - 1.0, a_im
    k_re = (num_re * lam_re + num_im * lam_im) / den
    k_im = (num_im * lam_re - num_re * lam_im) / den
    bb_re = k_re[..., None] * b_re - k_im[..., None] * b_im
    bb_im = k_re[..., None] * b_im + k_im[..., None] * b_re
    gt = g // SSM_TILE_GROUPS
    hg = SSM_HALF_GROUPS
    eye = jnp.eye(SSM_TILE_GROUPS, dtype=F32).reshape(SSM_TILE_GROUPS, 2, hg)

    def pack_b(bb):
        w = jnp.einsum("djhqpc,ghq->djhgcqp", bb.reshape(nd, gt, 2, hg, p, cg), eye)
        return w.reshape(nd, gt * 2, SSM_TILE_GROUPS * cg, hg * p)

    def pack_c(cc):
        w = jnp.einsum("djhqcp,ghq->djhqpgc", cc.reshape(nd, gt, 2, hg, cg, p), eye)
        return w.reshape(nd, gt * 2, hg * p, SSM_TILE_GROUPS * cg)

    a = jnp.stack([a_re.reshape(nd, gt * 2, hg * p), a_im.reshape(nd, gt * 2, hg * p)], axis=2)
    w = jnp.concatenate([pack_b(bb_re), pack_b(bb_im)], axis=-1)
    c = jnp.concatenate([pack_c(c_re), -pack_c(c_im)], axis=-2)
    return w, c, a


def _ret_tables(decay_logit, dk, dv):
    ch = RET_CHUNK
    nd, h = decay_logit.shape
    lg = jax.nn.log_sigmoid(decay_logit)[:, :, None]
    pos = jnp.arange(ch, dtype=F32)
    fwd_diff = pos[:, None] - pos[None, :]
    diff = jnp.stack([fwd_diff, -fwd_diff])[:, None]
    mask = jnp.stack([fwd_diff >= 0, -fwd_diff > 0])[:, None]
    end_pos = jnp.stack([ch - 1.0 - pos, pos])[:, None]
    in_pos = jnp.stack([pos + 1.0, ch - pos])[:, None]
    w_end = jnp.exp(lg * end_pos)
    w_in = jnp.exp(lg * in_pos)
    decay = jnp.where(mask, jnp.exp(lg[..., None] * jnp.where(mask, diff, 0.0)), 0.0)
    g_chunk = jnp.exp(lg[..., 0] * ch)
    return (decay, jnp.broadcast_to(w_end[..., None], (nd, h, ch, dk)), jnp.broadcast_to(w_in[..., None], (nd, h, ch, dk)),
            jnp.broadcast_to(g_chunk[..., None, None], (nd, h, dk, dv)))


def _rope_tables(t_rows, ncc, dk):
    quarter = dk // 4
    idx = np.arange(t_rows)
    row, col = idx // GRID_W, idx % GRID_W
    inv = ROPE_BASE ** (-np.arange(quarter, dtype=np.float32) / quarter)
    ang_r = row.astype(np.float32)[:, None] * inv
    ang_c = col.astype(np.float32)[:, None] * inv
    ang_r, ang_c = jnp.asarray(ang_r, F32), jnp.asarray(ang_c, F32)
    cos = jnp.concatenate([jnp.cos(ang_r), jnp.cos(ang_r), jnp.cos(ang_c), jnp.cos(ang_c)], axis=1)
    sin = jnp.concatenate([-jnp.sin(ang_r), jnp.sin(ang_r), -jnp.sin(ang_c), jnp.sin(ang_c)], axis=1)
    n_ctx = ncc * RET_CHUNK
    cos = jnp.concatenate([jnp.ones((n_ctx, dk), F32), cos], axis=0)
    sin = jnp.concatenate([jnp.zeros((n_ctx, dk), F32), sin], axis=0)
    return cos, sin


def _to_scan_layout(ctx_rows, lat_rows, rev):
    u = jnp.concatenate([lat_rows, ctx_rows] if rev else [ctx_rows, lat_rows], axis=0)
    r, w = u.shape
    return u.reshape(N_SEG, r // N_SEG, w).transpose(1, 0, 2).reshape(r, w)


def _from_scan_layout(yp, n_ctx, rev):
    r, w = yp.shape
    y = yp.reshape(r // N_SEG, N_SEG, w).transpose(1, 0, 2).reshape(r, w)
    return (y[r - n_ctx:], y[:r - n_ctx]) if rev else (y[:n_ctx], y[n_ctx:])


def _pack(parts, width):
    rows = []
    for p in parts:
        flat = p.reshape(-1).astype(F32)
        n = flat.shape[0]
        rows.append(jnp.pad(flat, (0, -n % (SUBLANE * width))).reshape(-1, width))
    return jnp.concatenate(rows, axis=0)


def _packed_rows(n, width):
    return -(-n // (SUBLANE * width)) * SUBLANE


def _unpack(flat2d, shapes):
    width = flat2d.shape[1]
    out, row = [], 0
    for shp in shapes:
        n = int(np.prod(shp))
        nr = _packed_rows(n, width)
        out.append(flat2d[row:row + nr].reshape(-1)[:n].reshape(shp))
        row += nr
    return out


def kernel(x, c, ctx, c_ctx, ada_w, ada_b, norm_g, ffn_w_in, ffn_w_out, mix_w_in, ssm_lam_re, ssm_lam_im, ssm_log_step, ssm_b_re, ssm_b_im, ssm_c_re, ssm_c_im, ssm_d, ssm_glu_w, ret_decay_logit, ret_w_proj, mix_w_out, loss_target, m_c_ctx, m_ada_w, m_ada_b, m_norm_g, m_ffn_w_in, m_ffn_w_out, m_mix_w_in, m_ssm_lam_re, m_ssm_lam_im, m_ssm_log_step, m_ssm_b_re, m_ssm_b_im, m_ssm_c_re, m_ssm_c_im, m_ssm_d, m_ssm_glu_w, m_ret_decay_logit, m_ret_w_proj, m_mix_w_out, v_c_ctx, v_ada_w, v_ada_b, v_norm_g, v_ffn_w_in, v_ffn_w_out, v_mix_w_in, v_ssm_lam_re, v_ssm_lam_im, v_ssm_log_step, v_ssm_b_re, v_ssm_b_im, v_ssm_c_re, v_ssm_c_im, v_ssm_d, v_ssm_glu_w, v_ret_decay_logit, v_ret_w_proj, v_mix_w_out):
    t_rows, d = x.shape[1], x.shape[2]
    n_ctx = ctx.shape[1]
    r = n_ctx + t_rows
    ssm_w = ssm_d.shape[1]
    heads = ret_decay_logit.shape[2]
    mi = mix_w_in.shape[2] * N_DEV
    dk = (mi - ssm_w - 2 * d) // (6 * heads)
    dv = 2 * dk
    qk_w, v_w = heads * dk, heads * dv
    q_off = ssm_w
    ncc = n_ctx // RET_CHUNK
    tile = n_ctx
    nct = 1
    wide_tile = _tile(n_ctx, 128, 16)
    assert r % (N_SEG * SUBLANE) == 0 and n_ctx % RET_CHUNK == 0 and t_rows % tile == 0
    me = 4 * lax.axis_index("x") + 2 * lax.axis_index("y") + lax.axis_index("c")
    g_off = ssm_w + 2 * qk_w + v_w
    gs_off = g_off + v_w

    ng_cols = norm_g.shape[2]
    small0 = _pack([c[0], norm_g[0]], d)
    small0_all = _all_gather(small0, 0, "ag_cond")

    bf = lambda w: w.astype(BF16)
    small0_all, sh_in1 = lax.optimization_barrier((small0_all, bf(ffn_w_in[0, 0])))
    small0_all = small0_all.reshape(N_DEV, -1)
    w_in1 = _all_gather(sh_in1, 1, "ag_ffn1_in", on_sequencer=True)
    w_glu = _all_gather(bf(ssm_glu_w[0]), 1, "ag_glu", on_sequencer=True)
    w_rp = _all_gather(bf(ret_w_proj[0]), 0, "ag_ret_proj", on_sequencer=True)
    w_mo = _all_gather(bf(mix_w_out[0]), 0, "ag_mix_out", on_sequencer=True)
    w_in2 = _all_gather(bf(ffn_w_in[0, 1]), 1, "ag_ffn2_in", on_sequencer=True)
    w_out2 = _all_gather(bf(ffn_w_out[0, 1]), 0, "ag_ffn2_out", on_sequencer=True)

    ng_at = _packed_rows(d, d) * d
    c_all = small0_all[:, :d]
    g_full = small0_all[:, ng_at:ng_at + 6 * ng_cols].reshape(N_DEV, 6, ng_cols).transpose(1, 0, 2).reshape(6, d)
    g6 = g_full.reshape(6, 1, d)
    cc = jnp.concatenate([c_all, c_ctx[None, :], jnp.zeros((2 * SUBLANE - N_DEV - 1, d), F32)], axis=0)
    sc = _silu_rows(cc, "ada_silu")
    na = ada_w.shape[2]
    a_loc = _mm(sc, ada_w[0], "nn", F32, "ada_fwd", tm=16, tn=na, tk=512)
    a_all = _all_gather(a_loc, 0, "ag_ada")
    a_all, sh_out1, sh_mix = lax.optimization_barrier((a_all, bf(ffn_w_out[0, 0]), bf(mix_w_in[0])))
    a_all = a_all.reshape(N_DEV, 2 * SUBLANE, na)
    w_out1 = _all_gather(sh_out1, 0, "ag_ffn1_out", on_sequencer=True)
    w_mix = _all_gather(sh_mix, 1, "ag_mix_in", on_sequencer=True)
    ada_x = lax.dynamic_index_in_dim(a_all, me, axis=1, keepdims=False).reshape(9 * d) + ada_b[0]
    ada_c = a_all[:, N_DEV, :].reshape(9 * d) + ada_b[0]
    mods = jnp.stack([ada_c.reshape(9, d), ada_x.reshape(9, d)]).reshape(18, 1, d)

    xin = jnp.concatenate([ctx[0], x[0]], axis=0)
    u1 = _ada_pre_fwd(xin, g6, mods, 0, 0, nct, tile, "pre1")
    g1, up1, a1 = _mm_swiglu(u1, w_in1, "ffn1_in", tm=544)
    o1 = _mm(a1, w_out1, "nn", BF16, "ffn1_out", tm=544, tn=1024, tk=2816)
    x1, u2 = _ada_post_fwd(xin, o1, g6, mods, 1, 0, 0.5, nct, tile, "post1_pre2", then_pre=(2, 1))
    hm = _mm(u2, w_mix, "nn", BF16, "mix_in", tm=544, tn=1024)

    us_ctx, us_lat = hm[:n_ctx, :ssm_w], hm[n_ctx:, :ssm_w]
    dskip = ssm_d.reshape(1, 1, ssm_w)
    s5_prm = (ssm_lam_re[0], ssm_lam_im[0], ssm_log_step[0], ssm_b_re[0], ssm_b_im[0], ssm_c_re[0], ssm_c_im[0])
    s5_tabs_both, s5_vjp = jax.vjp(_s5_tables, *s5_prm)
    s5_tabs, ups, y_dirs = [], [], []
    for dr in range(2):
        tabs = tuple(t[dr] for t in s5_tabs_both)
        up = _to_scan_layout(us_ctx, us_lat, dr == 1)
        yp = _s5_fwd(up, *tabs, dr == 1, "s5_fwd%d" % dr)
        s5_tabs.append(tabs)
        ups.append(up)
        y_dirs.append(_from_scan_layout(yp, n_ctx, dr == 1)[1])
    a_ssm = _ssm_out_fwd(y_dirs[0], y_dirs[1], hm, dskip, nct, tile, "ssm_out")
    gab = _mm(a_ssm, w_glu, "nn", BF16, "glu", tm=512, tn=2048, tk=ssm_w)

    cos, sin = _rope_tables(t_rows, ncc, dk)
    ret_tabs_both, ret_vjp = jax.vjp(functools.partial(_ret_tables, dk=dk, dv=dv), ret_decay_logit[0])
    ret_tabs, o_dirs, s_ins = [], [], []
    for dr in range(2):
        tabs = tuple(t[dr] for t in ret_tabs_both)
        o_d, s_in = _ret_fwd(hm, cos, sin, *tabs, heads, dk, dv, q_off, ncc, dr == 1, "ret_fwd%d" % dr)
        ret_tabs.append(tabs)
        o_dirs.append(o_d)
        s_ins.append(s_in)
    ret_in = _ret_gate_fwd(o_dirs[0], o_dirs[1], hm, g_off, heads, dv, nct, tile, "ret_gate")
    rb = _mm(ret_in, w_rp, "nn", BF16, "ret_proj", tm=512, tn=d, tk=v_w)
    merged = _merge_fwd(gab, rb, hm, gs_off, nct, tile, "merge")
    mix = _mm(merged, w_mo, "nn", BF16, "mix_out", tm=512, tn=d, tk=d)
    x2, u3 = _ada_post_fwd(x1, mix, g6, mods, 3, 1, 1.0, 0, tile, "post2_pre3", h_tile_offset=nct, then_pre=(4, 2))
    g3, up3, a3 = _mm_swiglu(u3, w_in2, "ffn2_in", tm=512)
    o3 = _mm(a3, w_out2, "nn", BF16, "ffn2_out", tm=512, tn=1024, tk=2816)
    dy, lcols = _ada_post_fwd(x2, o3, g6, mods, 5, 2, 0.5, 0, tile, "post3_loss", target=loss_target[0])
    loss_part = (0.5 * jnp.sum(lcols) / d).reshape(1)

    dg6 = [None] * 6
    dmod = {}

    def add_mod(sel_rows, k, val):
        for sel, row in sel_rows:
            dmod[(sel, k)] = dmod.get((sel, k), 0.0) + val[row, 0]

    both, lat = [(0, 0), (1, 1)], [(1, 0)]

    def tie(*vals):
        return lax.optimization_barrier(vals)

    def big_update(w3d, m3d, v3d, layer, gfull, axis, name, filled=None):
        p, recv = _reduce_scatter(gfull, axis, "rs_" + name)
        return _adamw_scattered(w3d, m3d, v3d, layer, p, recv, "adamw_" + name, filled)

    do3, dg6[5], dgt = _ada_post_bwd(dy, o3, g6, mods, 5, 2, 0.5, 0, 1, tile, "post3_bwd")
    add_mod(lat, 8, dgt)
    gw_out2 = _mm(a3, do3, "tn", BF16, "ffn2_out_dw", tm=1408, tn=1024, tk=2176)
    do3, gw_out2 = tie(do3, gw_out2)
    up_out2 = big_update(ffn_w_out[0], m_ffn_w_out[0], v_ffn_w_out[0], 1, gw_out2, 0, "ffn2_out")
    da3 = _mm(do3, w_out2, "nt", BF16, "ffn2_out_dx", tm=512, tn=2816, tk=d)
    dh3 = _swiglu_bwd(g3, up3, da3, wide_tile, "swiglu2_bwd")
    gw_in2 = _mm(u3, dh3, "tn", BF16, "ffn2_in_dw", tm=1024, tn=1024, tk=2176)
    dh3, gw_in2 = tie(dh3, gw_in2)
    up_in2 = big_update(ffn_w_in[0], m_ffn_w_in[0], v_ffn_w_in[0], 1, gw_in2, 1, "ffn2_in")
    du3 = _mm(dh3, w_in2, "nt", F32, "ffn2_in_dx", tm=512, tn=d, tk=1024)
    dx2, dg6[4], dsh, dsc = _ada_pre_bwd(x2, du3, dy, g6, mods, 4, 2, 0, 1, tile, "pre3_bwd")
    add_mod(lat, 6, dsh)
    add_mod(lat, 7, dsc)
    dmix, dg6[3], dgt = _ada_post_bwd(dx2, mix, g6, mods, 3, 1, 1.0, 0, 1, tile, "post2_bwd")
    add_mod(lat, 5, dgt)
    gw_mo = _mm(merged, dmix, "tn", BF16, "mix_out_dw", tm=1024, tn=1024, tk=2176)
    dmix, gw_mo = tie(dmix, gw_mo)
    up_mo = big_update(mix_w_out, m_mix_w_out, v_mix_w_out, 0, gw_mo, 0, "mix_out")
    dmerged = _mm(dmix, w_mo, "nt", BF16, "mix_out_dx", tm=512, tn=d, tk=d)
    dgab, drb, dgs, dgr = _merge_bwd(gab, rb, hm, gs_off, dmerged, nct, tile, "merge_bwd")
    gw_glu = _mm(a_ssm, dgab, "tn", BF16, "glu_dw", tm=1024, tn=1024, tk=2176)
    gw_rp = _mm(ret_in, drb, "tn", BF16, "ret_proj_dw", tm=1024, tn=1024, tk=2176)
    dgab, drb, gw_glu, gw_rp = tie(dgab, drb, gw_glu, gw_rp)
    up_glu = big_update(ssm_glu_w, m_ssm_glu_w, v_ssm_glu_w, 0, gw_glu, 1, "glu")
    up_rp = big_update(ret_w_proj, m_ret_w_proj, v_ret_w_proj, 0, gw_rp, 0, "ret_proj")
    da_ssm = _mm(dgab, w_glu, "nt", BF16, "glu_dx", tm=512, tn=ssm_w, tk=2 * d)
    dret_in = _mm(drb, w_rp, "nt", BF16, "ret_proj_dx", tm=512, tn=v_w, tk=d)
    d_o, dg_gate = _ret_gate_bwd(o_dirs[0], o_dirs[1], hm, g_off, dret_in, heads, dv, nct, tile, "ret_gate_bwd")
    dy_ssm, dus_direct, d_dskip = _ssm_out_bwd(y_dirs[0], y_dirs[1], hm, dskip, da_ssm, nct, tile, "ssm_out_bwd")
    s5_table_grads, du_ctx, du_lat = [], [], [dus_direct]
    for dr in range(2):
        dyp = _to_scan_layout(jnp.zeros((n_ctx, ssm_w), F32), dy_ssm, dr == 1)
        if dr == 1:
            dyp, up_out2, up_in2 = tie(dyp, up_out2, up_in2)
        outs = _s5_bwd(ups[dr], dyp, *s5_tabs[dr], dr == 1, "s5_bwd%d" % dr)
        part_ctx, part_lat = _from_scan_layout(outs[0], n_ctx, dr == 1)
        du_ctx.append(part_ctx)
        du_lat.append(part_lat)
        s5_table_grads.append(outs[1:])
    dqkv, ret_table_grads = [], []
    for dr in range(2):
        if dr == 1:
            d_o, up_mo, up_glu, up_rp = tie(d_o, up_mo, up_glu, up_rp)
        outs = _ret_bwd(hm, cos, sin, *ret_tabs[dr], s_ins[dr], d_o, heads, dk, dv, q_off, ncc, dr == 1,
                        "ret_bwd%d" % dr)
        dqkv.append(outs[:3])
        ret_table_grads.append(outs[3:])
    both_dirs = lambda grads: tuple(jnp.stack([g0, g1]) for g0, g1 in zip(*grads))
    early_parts = list(s5_vjp(both_dirs(s5_table_grads))) + list(ret_vjp(both_dirs(ret_table_grads)))
    s5_names = 7
    early_shapes = [p.shape for p in early_parts]
    early_all = _all_gather(_pack(early_parts, 1024), 0, "ag_s5_grads", on_sequencer=True)
    early_sum = _sum_leading(early_all.reshape(N_DEV, -1, 1024), "sum_s5_grads")
    dus = jnp.concatenate([du_ctx[0] + du_ctx[1], du_lat[0] + du_lat[1] + du_lat[2]], axis=0)
    dhm = _assemble_dhm(dus, dqkv[0][0], dqkv[1][0], dqkv[0][1], dqkv[1][1], dqkv[0][2], dqkv[1][2],
                        dg_gate, dgs, dgr, n_ctx // wide_tile, wide_tile, "assemble_dhm")
    gw_mix = _mm(u2, dhm, "tn", BF16, "mix_in_dw", tm=1024, tn=1024, tk=2176)
    dhm, gw_mix = tie(dhm, gw_mix)
    up_mix = big_update(mix_w_in, m_mix_w_in, v_mix_w_in, 0, gw_mix, 1, "mix_in")
    du2 = _mm(dhm, w_mix, "nt", F32, "mix_in_dx", tm=544, tn=d, tk=1024)
    dx1, dg6[2], dsh, dsc = _ada_pre_bwd(x1, du2, dx2, g6, mods, 2, 1, nct, 2, tile, "pre2_bwd", dres_x_only=True)
    add_mod(both, 3, dsh)
    add_mod(both, 4, dsc)
    do1, dg6[1], dgt = _ada_post_bwd(dx1, o1, g6, mods, 1, 0, 0.5, nct, 2, tile, "post1_bwd")
    add_mod(both, 2, dgt)
    gw_out1 = _mm(a1, do1, "tn", BF16, "ffn1_out_dw", tm=1408, tn=1024, tk=2176)
    do1, gw_out1 = tie(do1, gw_out1)
    up_out1 = big_update(ffn_w_out[0], m_ffn_w_out[0], v_ffn_w_out[0], 0, gw_out1, 0, "ffn1_out", filled=up_out2)
    da1 = _mm(do1, w_out1, "nt", BF16, "ffn1_out_dx", tm=544, tn=2816, tk=d)
    dh1 = _swiglu_bwd(g1, up1, da1, wide_tile, "swiglu1_bwd")
    dh1, up_mix, early_sum = tie(dh1, up_mix, early_sum)
    early_sums = _unpack(early_sum, early_shapes)
    gw_in1 = _mm(u1, dh1, "tn", BF16, "ffn1_in_dw", tm=1024, tn=1024, tk=2176)
    dh1, gw_in1 = tie(dh1, gw_in1)
    up_in1 = big_update(ffn_w_in[0], m_ffn_w_in[0], v_ffn_w_in[0], 0, gw_in1, 1, "ffn1_in", filled=up_in2)
    du1 = _mm(dh1, w_in1, "nt", F32, "ffn1_in_dx", tm=544, tn=d, tk=1024)
    dx_lat, dg6[0], dsh, dsc = _ada_pre_bwd(xin, du1, dx1, g6, mods, 0, 0, nct, 2, tile, "pre1_bwd",
                                            latent_dh_only=True)
    add_mod(both, 0, dsh)
    add_mod(both, 1, dsc)
    grad_x = dx_lat[None]

    zero_d = jnp.zeros((d,), F32)
    d_ada_x = jnp.stack([dmod.get((1, k), zero_d) for k in range(9)]).reshape(9 * d)
    d_ada_c = jnp.stack([dmod.get((0, k), zero_d) for k in range(9)]).reshape(9 * d)
    dg_full = jnp.stack([g[0, 0] for g in dg6])
    small_parts = [d_ada_x, d_ada_c, dg_full, d_dskip, loss_part]
    small_shapes = [p.shape for p in small_parts]
    packed = _pack(small_parts, 1024)
    gathered = _all_gather(packed, 0, "ag_small_grads").reshape(N_DEV, -1, 1024)
    summed = _sum_leading(gathered, "sum_small_grads")
    sums = _unpack(summed, small_shapes)
    sum_dx, sum_dc, sum_dg = sums[0], sums[1], sums[2]
    loss = sums[4][0]
    grad_ada_b = (sum_dx + sum_dc)[None]
    dx_rows = gathered.reshape(N_DEV, -1)[:, :9 * d]
    col0 = me * na
    da_rows = jnp.concatenate([lax.dynamic_slice_in_dim(dx_rows, col0, na, axis=1),
                               lax.dynamic_slice_in_dim(sum_dc[None], col0, na, axis=1),
                               jnp.zeros((2 * SUBLANE - N_DEV - 1, na), F32)], axis=0)
    grad_ada_w = _mm(sc, da_rows, "tn", F32, "ada_dw", tm=512, tn=na, tk=16)
    d_sc = _mm(da_rows, ada_w[0], "nt", F32, "ada_dx", tm=16, tn=512, tk=na)
    d_sc_all = _all_gather(jnp.broadcast_to(d_sc[N_DEV:N_DEV + 1], (SUBLANE, d)), 0, "ag_dctx")
    d_sc_sum = _sum_leading(d_sc_all.reshape(N_DEV, SUBLANE, d), "sum_dctx")
    grad_c_ctx = _silu_grad_rows(jnp.broadcast_to(c_ctx[None], (SUBLANE, d)), d_sc_sum, "ctx_silu_bwd")[0]
    grad_norm_g = lax.dynamic_slice_in_dim(sum_dg, me * ng_cols, ng_cols, axis=1)[None]

    upd = {}
    upd["ffn_w_in"] = [o[None] for o in up_in1]
    upd["ffn_w_out"] = [o[None] for o in up_out1]
    upd["mix_w_in"] = list(up_mix)
    upd["ssm_glu_w"] = list(up_glu)
    upd["ret_w_proj"] = list(up_rp)
    upd["mix_w_out"] = list(up_mo)
    upd["ada_w"] = [o[None] for o in _adamw(ada_w[0], m_ada_w[0], v_ada_w[0], grad_ada_w[None], "adamw_ada_w")]

    small_names = ["c_ctx", "ada_b", "norm_g", "ssm_lam_re", "ssm_lam_im", "ssm_log_step", "ssm_b_re", "ssm_b_im",
                   "ssm_c_re", "ssm_c_im", "ssm_d", "ret_decay_logit"]
    small_w = [c_ctx, ada_b, norm_g, ssm_lam_re, ssm_lam_im, ssm_log_step, ssm_b_re, ssm_b_im, ssm_c_re, ssm_c_im,
               ssm_d, ret_decay_logit]
    small_m = [m_c_ctx, m_ada_b, m_norm_g, m_ssm_lam_re, m_ssm_lam_im, m_ssm_log_step, m_ssm_b_re, m_ssm_b_im,
               m_ssm_c_re, m_ssm_c_im, m_ssm_d, m_ret_decay_logit]
    small_v = [v_c_ctx, v_ada_b, v_norm_g, v_ssm_lam_re, v_ssm_lam_im, v_ssm_log_step, v_ssm_b_re, v_ssm_b_im,
               v_ssm_c_re, v_ssm_c_im, v_ssm_d, v_ret_decay_logit]
    small_g = [grad_c_ctx, grad_ada_b, grad_norm_g] + [s[None] for s in early_sums[:s5_names]] + \
              [sums[3].reshape(ssm_d.shape), early_sums[s5_names][None]]
    shapes = [w.shape for w in small_w]
    res = _adamw(_pack(small_w, 1024), _pack(small_m, 1024), _pack(small_v, 1024), _pack(small_g, 1024)[None],
                 "adamw_small")
    small_out = [_unpack(o, shapes) for o in res]
    for i, nm in enumerate(small_names):
        upd[nm] = [small_out[kind][i] for kind in range(4)]

    order = ["c_ctx", "ada_w", "ada_b", "norm_g", "ffn_w_in", "ffn_w_out", "mix_w_in", "ssm_lam_re", "ssm_lam_im",
             "ssm_log_step", "ssm_b_re", "ssm_b_im", "ssm_c_re", "ssm_c_im", "ssm_d", "ssm_glu_w", "ret_decay_logit",
             "ret_w_proj", "mix_w_out"]
    outs = [loss, grad_x]
    for kind in range(4):
        outs += [upd[nm][kind] for nm in order]
    return tuple(outs)
```

```python
import functools
import math

import jax
import jax.numpy as jnp
import numpy as np
from jax import lax
from jax.experimental import pallas as pl
from jax.experimental.pallas import tpu as pltpu
from jax.experimental.pallas import tpu_sc as plsc

F32 = jnp.float32
BF16 = jnp.bfloat16
MXU_DTYPE = jnp.bfloat16
MESH_AXES = ("x", "y", "c")
N_DEV = 8
V7X_VMEM_LIMIT_BYTES = 56 * 1024 * 1024
LANE = 128
SUBLANE = 8

GRID_W = 64
RET_CHUNK = 128
ROPE_BASE = 10000.0
NORM_EPS = 1e-6
ADAM_LR = 0.001
ADAM_B1 = 0.9
ADAM_B2 = 0.999
ADAM_EPS = 1e-08
ADAM_WD = 0.01
ADAM_STEP = 10
SSM_TILE_GROUPS = 8
SSM_HALF_GROUPS = 4
N_SEG = 16


def _params(sem=None):
    return pltpu.CompilerParams(dimension_semantics=sem, vmem_limit_bytes=V7X_VMEM_LIMIT_BYTES)


def _tile(n, target, mult):
    best = None
    t = mult
    while t <= min(n, target):
        if n % t == 0:
            best = t
        t += mult
    return n if best is None else best


def _sds(shape, dtype):
    return jax.ShapeDtypeStruct(tuple(shape), dtype)


def _mm(a, b, dims, out_dtype, name, tm=512, tn=1408, tk=2048):
    if dims == "nn":
        (m, k), (k2, n) = a.shape, b.shape
    elif dims == "nt":
        (m, k), (n, k2) = a.shape, b.shape
    else:
        (k, m), (k2, n) = a.shape, b.shape
    assert k == k2, (a.shape, b.shape, dims)
    tm = _tile(m, tm, 16)
    tn = _tile(n, tn, LANE)
    tk = _tile(k, tk, LANE if dims != "tn" else 16)
    nk = k // tk
    dn = {"nn": (((1,), (0,)), ((), ())), "nt": (((1,), (1,)), ((), ())), "tn": (((0,), (0,)), ((), ()))}[dims]

    def product(a_ref, b_ref):
        return lax.dot_general(a_ref[...].astype(MXU_DTYPE), b_ref[...].astype(MXU_DTYPE), dn,
                               preferred_element_type=F32)

    def body_single(a_ref, b_ref, o_ref):
        o_ref[...] = product(a_ref, b_ref).astype(o_ref.dtype)

    def body(a_ref, b_ref, o_ref, acc_ref):
        kk = pl.program_id(2)

        @pl.when(kk == 0)
        def _():
            acc_ref[...] = product(a_ref, b_ref)

        @pl.when((kk > 0) & (kk < nk - 1))
        def _():
            acc_ref[...] += product(a_ref, b_ref)

        @pl.when(kk == nk - 1)
        def _():
            o_ref[...] = (acc_ref[...] + product(a_ref, b_ref)).astype(o_ref.dtype)

    if dims == "nn":
        a_spec = pl.BlockSpec((tm, tk), lambda j, i, kk: (i, kk))
        b_spec = pl.BlockSpec((tk, tn), lambda j, i, kk: (kk, j))
    elif dims == "nt":
        a_spec = pl.BlockSpec((tm, tk), lambda j, i, kk: (i, kk))
        b_spec = pl.BlockSpec((tn, tk), lambda j, i, kk: (j, kk))
    else:
        a_spec = pl.BlockSpec((tk, tm), lambda j, i, kk: (kk, i))
        b_spec = pl.BlockSpec((tk, tn), lambda j, i, kk: (kk, j))
    return pl.pallas_call(
        body_single if nk == 1 else body, name=name, grid=(n // tn, m // tm, nk), in_specs=[a_spec, b_spec],
        out_specs=pl.BlockSpec((tm, tn), lambda j, i, kk: (i, j)), out_shape=_sds((m, n), out_dtype),
        scratch_shapes=[] if nk == 1 else [pltpu.VMEM((tm, tn), F32)],
        compiler_params=_params(("parallel", "parallel", "arbitrary")))(a, b)


def _mm_swiglu(a, b, name, tm=512, tn=512):
    m, k = a.shape
    k2, f2 = b.shape
    f = f2 // 2
    assert k == k2
    tm = _tile(m, tm, 16)
    tn = _tile(f, tn, 2 * LANE)
    nj = f // tn

    def body(a_ref, bg_ref, bu_ref, g_ref, u_ref, act_ref):
        av = a_ref[...].astype(MXU_DTYPE)
        gate = jnp.dot(av, bg_ref[...].astype(MXU_DTYPE), preferred_element_type=F32)
        up = jnp.dot(av, bu_ref[...].astype(MXU_DTYPE), preferred_element_type=F32)
        g_ref[...] = gate.astype(g_ref.dtype)
        u_ref[...] = up.astype(u_ref.dtype)
        act_ref[...] = (gate * _sigmoid(gate) * up).astype(act_ref.dtype)

    tile = pl.BlockSpec((tm, tn), lambda j, i: (i, j))
    out = _sds((m, f), BF16)
    return pl.pallas_call(
        body, name=name, grid=(nj, m // tm),
        in_specs=[pl.BlockSpec((tm, k), lambda j, i: (i, 0)), pl.BlockSpec((k, tn), lambda j, i: (0, j)),
                  pl.BlockSpec((k, tn), lambda j, i: (0, j + nj))],
        out_specs=[tile, tile, tile], out_shape=[out, out, out],
        compiler_params=_params(("parallel", "parallel")))(a, b, b)


def _rows(name, body, n_tiles, ins, outs):
    in_specs = [pl.BlockSpec(blk, imap) for (_, blk, imap) in ins]
    out_specs = [pl.BlockSpec(blk, imap) for (_, _, blk, imap) in outs]
    out_shape = [_sds(shape, dt) for (shape, dt, _, _) in outs]
    res = pl.pallas_call(body, name=name, grid=(n_tiles,), in_specs=in_specs, out_specs=out_specs,
                         out_shape=out_shape, compiler_params=_params(("arbitrary",)))(*[a for (a, _, _) in ins])
    return res


def _row_in(arr, tile, width=None, col=0, x_only_offset=None):
    width = arr.shape[1] if width is None else width
    if x_only_offset is None:
        return (arr, (tile, width), lambda i: (i, col))
    return (arr, (tile, width), lambda i: (jnp.maximum(i - x_only_offset, 0), col))


def _vec_in(arr, idx_fn):
    return (arr, (1, 1, arr.shape[2]), lambda i: (idx_fn(i), 0, 0))


def _rms(h):
    return lax.rsqrt(jnp.mean(h * h, axis=-1, keepdims=True) + NORM_EPS)


def _sigmoid(z):
    return 1.0 / (1.0 + jnp.exp(-z))


def _ada_pre_fwd(h, g6, mods, gi, mi, nct, tile, name):
    r, d = h.shape
    sel = lambda i: jnp.where(i >= nct, 1, 0)

    def body(h_ref, g_ref, sh_ref, sc_ref, u_ref):
        hh = h_ref[...]
        n = hh * _rms(hh) * g_ref[0]
        u_ref[...] = (n * (1.0 + sc_ref[0]) + sh_ref[0]).astype(u_ref.dtype)

    (u,) = _rows(name, body, r // tile,
                 [_row_in(h, tile), _vec_in(g6, lambda i: gi), _vec_in(mods, lambda i: sel(i) * 9 + 3 * mi),
                  _vec_in(mods, lambda i: sel(i) * 9 + 3 * mi + 1)],
                 [((r, d), BF16, (tile, d), lambda i: (i, 0))])
    return u


def _ada_pre_bwd(h, du, dres, g6, mods, gi, mi, nct, nsel, tile, name, dres_x_only=False, latent_dh_only=False):
    r, d = h.shape
    dh_rows = r - nct * tile if latent_dh_only else r
    dh_map = (lambda i: (jnp.maximum(i - nct, 0), 0)) if latent_dh_only else (lambda i: (i, 0))
    sel = lambda i: jnp.where(i >= nct, 1, 0) if nsel == 2 else 0
    msel = lambda i: jnp.where(i >= nct, 1, 0)
    off = nct if dres_x_only else None

    def body(h_ref, du_ref, dr_ref, g_ref, sc_ref, dh_ref, dg_ref, dsh_ref, dsc_ref):
        i = pl.program_id(0)
        hh = h_ref[...]
        rr = _rms(hh)
        g = g_ref[0]
        hn = hh * rr
        n = hn * g
        du_ = du_ref[...].astype(F32)
        dn = du_ * (1.0 + sc_ref[0])

        @pl.when(i == 0)
        def _():
            dg_ref[...] = jnp.zeros_like(dg_ref)

        @pl.when((i == 0) | (i == nct))
        def _():
            dsh_ref[...] = jnp.zeros_like(dsh_ref)
            dsc_ref[...] = jnp.zeros_like(dsc_ref)

        dg_ref[0] += jnp.sum(dn * hn, axis=0, keepdims=True)
        dsh_ref[0] += jnp.sum(du_, axis=0, keepdims=True)
        dsc_ref[0] += jnp.sum(du_ * n, axis=0, keepdims=True)
        t = dn * g
        dh = rr * t - hn * (rr * jnp.mean(t * hn, axis=-1, keepdims=True))
        if dres_x_only:
            dh_ref[...] = dh + jnp.where(i >= nct, dr_ref[...], 0.0)
        else:
            dh_ref[...] = dh + dr_ref[...]

    dh, dg, dsh, dsc = _rows(
        name, body, r // tile,
        [_row_in(h, tile), _row_in(du, tile), _row_in(dres, tile, x_only_offset=off), _vec_in(g6, lambda i: gi),
         _vec_in(mods, lambda i: msel(i) * 9 + 3 * mi + 1)],
        [((dh_rows, d), F32, (tile, d), dh_map), ((1, 1, d), F32, (1, 1, d), lambda i: (0, 0, 0)),
         ((nsel, 1, d), F32, (1, 1, d), lambda i: (sel(i), 0, 0)),
         ((nsel, 1, d), F32, (1, 1, d), lambda i: (sel(i), 0, 0))])
    return dh, dg, dsh, dsc


def _ada_post_fwd(h, o, g6, mods, gi, mi, res_w, nct, tile, name, h_tile_offset=0, then_pre=None, target=None):
    r, d = o.shape
    sel = lambda i: jnp.where(i >= nct, 1, 0)

    def body(h_ref, o_ref, g_ref, gt_ref, *refs):
        oo = o_ref[...].astype(F32)
        n = oo * _rms(oo) * g_ref[0]
        y = h_ref[...] + res_w * gt_ref[0] * n
        if target is not None:
            t_ref, dy_ref, l_ref = refs
            e = y - t_ref[...]
            dy_ref[...] = e * (1.0 / d)

            @pl.when(pl.program_id(0) == 0)
            def _():
                l_ref[...] = jnp.zeros_like(l_ref)

            l_ref[0] += jnp.sum(e * e, axis=0, keepdims=True)
        elif then_pre is not None:
            g2_ref, sh_ref, sc_ref, y_ref, u_ref = refs
            y_ref[...] = y
            u_ref[...] = (y * _rms(y) * g2_ref[0] * (1.0 + sc_ref[0]) + sh_ref[0]).astype(u_ref.dtype)
        else:
            refs[0][...] = y

    ins = [(h, (tile, d), lambda i: (i + h_tile_offset, 0)), _row_in(o, tile), _vec_in(g6, lambda i: gi),
           _vec_in(mods, lambda i: sel(i) * 9 + 3 * mi + 2)]
    row_out = lambda dt: ((r, d), dt, (tile, d), lambda i: (i, 0))
    if target is not None:
        ins.append(_row_in(target, tile))
        outs = [row_out(F32), ((1, 1, d), F32, (1, 1, d), lambda i: (0, 0, 0))]
    elif then_pre is not None:
        gi2, mi2 = then_pre
        ins += [_vec_in(g6, lambda i: gi2), _vec_in(mods, lambda i: sel(i) * 9 + 3 * mi2),
                _vec_in(mods, lambda i: sel(i) * 9 + 3 * mi2 + 1)]
        outs = [row_out(F32), row_out(BF16)]
    else:
        outs = [row_out(F32)]
    res = _rows(name, body, r // tile, ins, outs)
    return res[0] if len(res) == 1 else res


def _ada_post_bwd(dy, o, g6, mods, gi, mi, res_w, nct, nsel, tile, name):
    r, d = o.shape
    sel = lambda i: jnp.where(i >= nct, 1, 0) if nsel == 2 else 0
    msel = lambda i: jnp.where(i >= nct, 1, 0)

    def body(dy_ref, o_ref, g_ref, gt_ref, do_ref, dg_ref, dgt_ref):
        i = pl.program_id(0)
        oo = o_ref[...].astype(F32)
        rr = _rms(oo)
        g = g_ref[0]
        on = oo * rr
        dy_ = dy_ref[...] * res_w

        @pl.when(i == 0)
        def _():
            dg_ref[...] = jnp.zeros_like(dg_ref)

        @pl.when((i == 0) | (i == nct))
        def _():
            dgt_ref[...] = jnp.zeros_like(dgt_ref)

        dgt_ref[0] += jnp.sum(dy_ * (on * g), axis=0, keepdims=True)
        dn = dy_ * gt_ref[0]
        dg_ref[0] += jnp.sum(dn * on, axis=0, keepdims=True)
        t = dn * g
        do_ref[...] = (rr * t - on * (rr * jnp.mean(t * on, axis=-1, keepdims=True))).astype(do_ref.dtype)

    do, dg, dgt = _rows(
        name, body, r // tile,
        [_row_in(dy, tile), _row_in(o, tile), _vec_in(g6, lambda i: gi),
         _vec_in(mods, lambda i: msel(i) * 9 + 3 * mi + 2)],
        [((r, d), BF16, (tile, d), lambda i: (i, 0)), ((1, 1, d), F32, (1, 1, d), lambda i: (0, 0, 0)),
         ((nsel, 1, d), F32, (1, 1, d), lambda i: (sel(i), 0, 0))])
    return do, dg, dgt


def _swiglu_bwd(gate, up, da, tile, name):
    r, f = gate.shape

    def body(g_ref, u_ref, da_ref, dh_ref):
        gt = g_ref[...].astype(F32)
        d = da_ref[...].astype(F32)
        sg = _sigmoid(gt)
        dh_ref[:, :f] = (d * u_ref[...].astype(F32) * (sg * (1.0 + gt * (1.0 - sg)))).astype(dh_ref.dtype)
        dh_ref[:, f:] = (d * gt * sg).astype(dh_ref.dtype)

    (dh,) = _rows(name, body, r // tile, [_row_in(gate, tile), _row_in(up, tile), _row_in(da, tile)],
                  [((r, 2 * f), BF16, (tile, 2 * f), lambda i: (i, 0))])
    return dh


def _gelu_parts(y):
    c0 = math.sqrt(2.0 / math.pi)
    inner = c0 * (y + 0.044715 * y * y * y)
    th = jnp.tanh(inner)
    return th, c0 * (1.0 + 3 * 0.044715 * y * y)


def _ssm_out_fwd(y0, y1, hm, dskip, nct, tile, name):
    t_rows, s = y0.shape

    def body(y0_ref, y1_ref, u_ref, d_ref, a_ref):
        y = y0_ref[...] + y1_ref[...] + d_ref[0] * u_ref[...].astype(F32)
        th, _ = _gelu_parts(y)
        a_ref[...] = (0.5 * y * (1.0 + th)).astype(a_ref.dtype)

    (a,) = _rows(name, body, t_rows // tile,
                 [_row_in(y0, tile), _row_in(y1, tile), (hm, (tile, s), lambda i: (i + nct, 0)),
                  _vec_in(dskip, lambda i: 0)],
                 [((t_rows, s), BF16, (tile, s), lambda i: (i, 0))])
    return a


def _ssm_out_bwd(y0, y1, hm, dskip, da, nct, tile, name):
    t_rows, s = y0.shape

    def body(y0_ref, y1_ref, u_ref, d_ref, da_ref, dy_ref, du_ref, dd_ref):
        i = pl.program_id(0)
        u = u_ref[...].astype(F32)
        y = y0_ref[...] + y1_ref[...] + d_ref[0] * u
        th, dinner = _gelu_parts(y)
        dy = da_ref[...].astype(F32) * (0.5 * (1.0 + th) + 0.5 * y * (1.0 - th * th) * dinner)
        dy_ref[...] = dy
        du_ref[...] = dy * d_ref[0]

        @pl.when(i == 0)
        def _():
            dd_ref[...] = jnp.zeros_like(dd_ref)

        dd_ref[0] += jnp.sum(dy * u, axis=0, keepdims=True)

    dy, du, dd = _rows(name, body, t_rows // tile,
                       [_row_in(y0, tile), _row_in(y1, tile), (hm, (tile, s), lambda i: (i + nct, 0)),
                        _vec_in(dskip, lambda i: 0), _row_in(da, tile)],
                       [((t_rows, s), F32, (tile, s), lambda i: (i, 0)), ((t_rows, s), F32, (tile, s), lambda i: (i, 0)),
                        ((1, 1, s), F32, (1, 1, s), lambda i: (0, 0, 0))])
    return dy, du, dd


def _col_pieces(arr, off, width, tile, nct, unit=None):
    pw = math.gcd(off, width if unit is None else unit)
    specs = [(arr, (tile, pw), functools.partial(lambda i, cb: (i + nct, cb), cb=off // pw + p))
             for p in range(width // pw)]
    return specs, pw


def _ret_gate_fwd(o0, o1, hm, g_off, heads, dv, nct, tile, name):
    t_rows, w = o0.shape
    g_specs, pw = _col_pieces(hm, g_off, w, tile, nct)
    ng = len(g_specs)

    def body(o0_ref, o1_ref, *refs):
        g_refs, r_ref = refs[:ng], refs[ng]
        for hd in range(heads):
            cs = slice(hd * dv, (hd + 1) * dv)
            o = o0_ref[:, cs] + o1_ref[:, cs]
            lo = (hd * dv) % pw
            g = g_refs[(hd * dv) // pw][:, lo:lo + dv].astype(F32)
            r_ref[:, cs] = (g * _sigmoid(g) * (o * _rms(o))).astype(r_ref.dtype)

    (ri,) = _rows(name, body, t_rows // tile, [_row_in(o0, tile), _row_in(o1, tile)] + g_specs,
                  [((t_rows, w), BF16, (tile, w), lambda i: (i, 0))])
    return ri


def _ret_gate_bwd(o0, o1, hm, g_off, dri, heads, dv, nct, tile, name):
    t_rows, w = o0.shape
    g_specs, pw = _col_pieces(hm, g_off, w, tile, nct)
    ng = len(g_specs)

    def body(o0_ref, o1_ref, d_ref, *refs):
        g_refs, do_ref, dg_ref = refs[:ng], refs[ng], refs[ng + 1]
        for hd in range(heads):
            cs = slice(hd * dv, (hd + 1) * dv)
            o = o0_ref[:, cs] + o1_ref[:, cs]
            lo = (hd * dv) % pw
            g = g_refs[(hd * dv) // pw][:, lo:lo + dv].astype(F32)
            d = d_ref[:, cs].astype(F32)
            rr = _rms(o)
            on = o * rr
            sg = _sigmoid(g)
            dg_ref[:, cs] = (d * on * (sg * (1.0 + g * (1.0 - sg)))).astype(dg_ref.dtype)
            t = d * (g * sg)
            do_ref[:, cs] = rr * t - on * (rr * jnp.mean(t * on, axis=-1, keepdims=True))

    do, dg = _rows(name, body, t_rows // tile, [_row_in(o0, tile), _row_in(o1, tile), _row_in(dri, tile)] + g_specs,
                   [((t_rows, w), F32, (tile, w), lambda i: (i, 0)), ((t_rows, w), BF16, (tile, w), lambda i: (i, 0))])
    return do, dg


def _merge_fwd(gab, rb, hm, gs_off, nct, tile, name):
    t_rows, d = rb.shape
    specs, pw = _col_pieces(hm, gs_off, 2 * d, tile, nct, unit=d)
    npc = d // pw

    def body(gab_ref, rb_ref, *refs):
        gs_refs, gr_refs, m_ref = refs[:npc], refs[npc:2 * npc], refs[2 * npc]
        for p in range(npc):
            cs = slice(p * pw, (p + 1) * pw)
            ga = gab_ref[:, cs].astype(F32)
            gb = gab_ref[:, d + p * pw:d + (p + 1) * pw].astype(F32)
            m_ref[:, cs] = (_sigmoid(gs_refs[p][...].astype(F32)) * (ga * _sigmoid(gb))
                            + _sigmoid(gr_refs[p][...].astype(F32)) * rb_ref[:, cs].astype(F32)).astype(m_ref.dtype)

    (mg,) = _rows(name, body, t_rows // tile, [_row_in(gab, tile), _row_in(rb, tile)] + specs,
                  [((t_rows, d), BF16, (tile, d), lambda i: (i, 0))])
    return mg


def _merge_bwd(gab, rb, hm, gs_off, dm, nct, tile, name):
    t_rows, d = rb.shape
    specs, pw = _col_pieces(hm, gs_off, 2 * d, tile, nct, unit=d)
    npc = d // pw

    def body(gab_ref, rb_ref, dm_ref, *refs):
        gs_refs, gr_refs = refs[:npc], refs[npc:2 * npc]
        dgab_ref, drb_ref, dgs_ref, dgr_ref = refs[2 * npc:]
        for p in range(npc):
            cs = slice(p * pw, (p + 1) * pw)
            cs2 = slice(d + p * pw, d + (p + 1) * pw)
            ga = gab_ref[:, cs].astype(F32)
            gb = gab_ref[:, cs2].astype(F32)
            dmm = dm_ref[:, cs].astype(F32)
            ss = _sigmoid(gs_refs[p][...].astype(F32))
            sr = _sigmoid(gr_refs[p][...].astype(F32))
            sb = _sigmoid(gb)
            dbr = dmm * ss
            dgab_ref[:, cs] = (dbr * sb).astype(dgab_ref.dtype)
            dgab_ref[:, cs2] = (dbr * ga * sb * (1.0 - sb)).astype(dgab_ref.dtype)
            drb_ref[:, cs] = (dmm * sr).astype(drb_ref.dtype)
            dgs_ref[:, cs] = (dmm * (ga * sb) * ss * (1.0 - ss)).astype(dgs_ref.dtype)
            dgr_ref[:, cs] = (dmm * rb_ref[:, cs].astype(F32) * sr * (1.0 - sr)).astype(dgr_ref.dtype)

    return _rows(name, body, t_rows // tile, [_row_in(gab, tile), _row_in(rb, tile), _row_in(dm, tile)] + specs,
                 [((t_rows, 2 * d), BF16, (tile, 2 * d), lambda i: (i, 0)), ((t_rows, d), BF16, (tile, d), lambda i: (i, 0)),
                  ((t_rows, d), BF16, (tile, d), lambda i: (i, 0)), ((t_rows, d), BF16, (tile, d), lambda i: (i, 0))])


def _assemble_dhm(dus, dq0, dq1, dk0, dk1, dv0, dv1, dg, dgs, dgr, nct, tile, name):
    r, s = dus.shape
    qk = dq0.shape[1]
    vw = dv0.shape[1]
    d = dgs.shape[1]
    mi = s + 2 * qk + 2 * vw + 2 * d
    c_q, c_k, c_v, c_g, c_gs, c_gr = s, s + qk, s + 2 * qk, s + 2 * qk + vw, s + 2 * qk + 2 * vw, s + 2 * qk + 2 * vw + d

    def body(dus_ref, dq0_ref, dq1_ref, dk0_ref, dk1_ref, dv0_ref, dv1_ref, dg_ref, dgs_ref, dgr_ref, o_ref):
        i = pl.program_id(0)
        lat = i >= nct
        o_ref[:, :s] = dus_ref[...].astype(o_ref.dtype)
        o_ref[:, c_q:c_k] = (dq0_ref[...].astype(F32) + dq1_ref[...].astype(F32)).astype(o_ref.dtype)
        o_ref[:, c_k:c_v] = (dk0_ref[...].astype(F32) + dk1_ref[...].astype(F32)).astype(o_ref.dtype)
        o_ref[:, c_v:c_g] = (dv0_ref[...].astype(F32) + dv1_ref[...].astype(F32)).astype(o_ref.dtype)
        o_ref[:, c_g:c_gs] = jnp.where(lat, dg_ref[...], 0.0).astype(o_ref.dtype)
        o_ref[:, c_gs:c_gr] = jnp.where(lat, dgs_ref[...], 0.0).astype(o_ref.dtype)
        o_ref[:, c_gr:] = jnp.where(lat, dgr_ref[...], 0.0).astype(o_ref.dtype)

    (out,) = _rows(name, body, r // tile,
                   [_row_in(dus, tile), _row_in(dq0, tile), _row_in(dq1, tile), _row_in(dk0, tile), _row_in(dk1, tile),
                    _row_in(dv0, tile), _row_in(dv1, tile), _row_in(dg, tile, x_only_offset=nct),
                    _row_in(dgs, tile, x_only_offset=nct), _row_in(dgr, tile, x_only_offset=nct)],
                   [((r, mi), BF16, (tile, mi), lambda i: (i, 0))])
    return out


def _silu_rows(v, name):
    def body(v_ref, o_ref):
        z = v_ref[...]
        o_ref[...] = z * _sigmoid(z)

    (o,) = _rows(name, body, 1, [_row_in(v, v.shape[0])], [(v.shape, F32, v.shape, lambda i: (0, 0))])
    return o


def _silu_grad_rows(v, dv, name):
    def body(v_ref, d_ref, o_ref):
        z = v_ref[...]
        sg = _sigmoid(z)
        o_ref[...] = d_ref[...] * (sg * (1.0 + z * (1.0 - sg)))

    (o,) = _rows(name, body, 1, [_row_in(v, v.shape[0]), _row_in(dv, v.shape[0])],
                 [(v.shape, F32, v.shape, lambda i: (0, 0))])
    return o


def _sum_leading(g8, name):
    n, r, c = g8.shape
    tile = _tile(r, 256, SUBLANE)

    def body(g_ref, o_ref):
        acc = g_ref[0]
        for j in range(1, n):
            acc = acc + g_ref[j]
        o_ref[...] = acc

    (o,) = _rows(name, body, r // tile, [(g8, (n, tile, c), lambda i: (0, i, 0))],
                 [((r, c), F32, (tile, c), lambda i: (i, 0))])
    return o


def _pair_sum(g, recv, axis, name):
    n, br, bc = recv.shape
    tile = _tile(br, 256, 16)
    nrt = br // tile
    core = lax.axis_index("c").astype(jnp.int32).reshape(1)

    def body(c_ref, g_ref, r_ref, o_ref):
        o_ref[0] = (g_ref[...].astype(F32) + r_ref[0].astype(F32)).astype(o_ref.dtype)

    if axis == 1:
        g_spec = pl.BlockSpec((tile, bc), lambda q, i, c_ref: (i, 2 * q + c_ref[0]))
    else:
        g_spec = pl.BlockSpec((tile, bc), lambda q, i, c_ref: ((2 * q + c_ref[0]) * nrt + i, 0))
    slot = pl.BlockSpec((1, tile, bc), lambda q, i, c_ref: (q, i, 0))
    return pl.pallas_call(
        body, name=name, out_shape=_sds((n, br, bc), recv.dtype),
        grid_spec=pltpu.PrefetchScalarGridSpec(num_scalar_prefetch=1, grid=(n, nrt), in_specs=[g_spec, slot],
                                               out_specs=slot),
        compiler_params=_params(("arbitrary", "arbitrary")))(core, g, recv)


def _adam_math(w, m, v, g):
    c1 = 1.0 / (1.0 - ADAM_B1 ** ADAM_STEP)
    c2 = 1.0 / (1.0 - ADAM_B2 ** ADAM_STEP)
    mm = ADAM_B1 * m + (1.0 - ADAM_B1) * g
    vv = ADAM_B2 * v + (1.0 - ADAM_B2) * (g * g)
    return -ADAM_LR * ((mm * c1) / (jnp.sqrt(vv * c2) + ADAM_EPS) + ADAM_WD * w), mm, vv


def _adamw(w, m, v, gparts, name):
    r, c = w.shape
    n = gparts.shape[0]
    tile = _tile(r, 256, 16)

    def body(w_ref, m_ref, v_ref, g_ref, go_ref, d_ref, mo_ref, vo_ref):
        g = g_ref[0].astype(F32)
        for j in range(1, n):
            g = g + g_ref[j].astype(F32)
        go_ref[...] = g
        d_ref[...], mo_ref[...], vo_ref[...] = _adam_math(w_ref[...], m_ref[...], v_ref[...], g)

    rs = lambda arr: _row_in(arr, tile)
    out = ((r, c), F32, (tile, c), lambda i: (i, 0))
    return _rows(name, body, r // tile, [rs(w), rs(m), rs(v), (gparts, (n, tile, c), lambda i: (0, i, 0))],
                 [out, out, out, out])


def _adamw_scattered(w, m, v, layer, p, recv, name, filled=None):
    nl, r, c = w.shape
    n = recv.shape[0]
    tile = _tile(r, 256, 16)
    chip = (2 * lax.axis_index("x") + lax.axis_index("y")).astype(jnp.int32).reshape(1)
    n_prev = 0 if filled is None else len(filled)

    def body(q_ref, w_ref, m_ref, v_ref, p_ref, g_ref, *rest):
        go_ref, d_ref, mo_ref, vo_ref = rest[n_prev:]
        g = p_ref[0].astype(F32)
        for j in range(n):
            g = g + g_ref[j].astype(F32)
        go_ref[0] = g
        d_ref[0], mo_ref[0], vo_ref[0] = _adam_math(w_ref[0], m_ref[0], v_ref[0], g)

    slab = pl.BlockSpec((1, tile, c), lambda i, q_ref: (layer, i, 0))
    anywhere = pl.BlockSpec(memory_space=pl.ANY)
    out = _sds((nl, r, c), F32)
    prev = [] if filled is None else list(filled)
    return pl.pallas_call(
        body, name=name, out_shape=[out, out, out, out],
        grid_spec=pltpu.PrefetchScalarGridSpec(
            num_scalar_prefetch=1, grid=(r // tile,),
            in_specs=[slab, slab, slab, pl.BlockSpec((1, tile, c), lambda i, q_ref: (q_ref[0], i, 0)),
                      pl.BlockSpec((n, tile, c), lambda i, q_ref: (0, i, 0))] + [anywhere] * n_prev,
            out_specs=[slab, slab, slab, slab]),
        input_output_aliases={6 + j: j for j in range(n_prev)},
        compiler_params=_params(("arbitrary",)))(chip, w, m, v, p, recv, *prev)


def _cmul(ar, ai, br, bi):
    return ar * br - ai * bi, ar * bi + ai * br


def _cpow(ar, ai, n):
    pr, pi = jnp.ones_like(ar), jnp.zeros_like(ar)
    br, bi = ar, ai
    while n:
        if n & 1:
            pr, pi = _cmul(pr, pi, br, bi)
        n >>= 1
        if n:
            br, bi = _cmul(br, bi, br, bi)
    return pr, pi


def _s5_scan_into(x_ref, ar1, ai1, ns, fin_ref, hin_ref, reverse, paired=None):
    st = ar1.shape[1]
    ar = jnp.broadcast_to(ar1, (N_SEG, st))
    ai = jnp.broadcast_to(ai1, (N_SEG, st))
    zero = jnp.zeros((N_SEG, st), F32)

    def slab(k):
        if isinstance(k, int):
            return pl.ds(k * N_SEG, N_SEG)
        return pl.ds(pl.multiple_of(k * N_SEG, N_SEG), N_SEG)

    def pass1(j, carry):
        hr, hi = carry
        k = ns - 1 - j if reverse else j
        nr, ni = _cmul(ar, ai, hr, hi)
        return nr + x_ref[slab(k), :st], ni + x_ref[slab(k), st:]

    fr, fi = lax.fori_loop(0, ns, pass1, (zero, zero))
    fin_ref[:, :st] = fr
    fin_ref[:, st:] = fi
    pr, pi = _cpow(ar1, ai1, ns)
    order = list(range(N_SEG - 1, -1, -1)) if reverse else list(range(N_SEG))
    hin_ref[order[0]:order[0] + 1, :] = jnp.zeros((1, 2 * st), F32)
    for a_, b_ in zip(order[:-1], order[1:]):
        cr, ci = _cmul(pr, pi, hin_ref[a_:a_ + 1, :st], hin_ref[a_:a_ + 1, st:])
        hin_ref[b_:b_ + 1, :st] = cr + fin_ref[a_:a_ + 1, :st]
        hin_ref[b_:b_ + 1, st:] = ci + fin_ref[a_:a_ + 1, st:]

    def step2(k, hr, hi):
        nr, ni = _cmul(ar, ai, hr, hi)
        nr = nr + x_ref[slab(k), :st]
        ni = ni + x_ref[slab(k), st:]
        x_ref[slab(k), :st] = nr
        x_ref[slab(k), st:] = ni
        return nr, ni

    if paired is None:
        def pass2(j, carry):
            return step2(ns - 1 - j if reverse else j, *carry)

        lax.fori_loop(0, ns, pass2, (hin_ref[:, :st], hin_ref[:, st:]))
        return None
    p_ref, p_edge_ref, shift = paired

    def pass2_paired(j, carry):
        hr, hi, acr, aci = carry
        k = ns - 1 - j if reverse else j
        nr, ni = step2(k, hr, hi)
        p_r, p_i = p_ref[slab(k + shift), :st], p_ref[slab(k + shift), st:]
        return nr, ni, acr + nr * p_r + ni * p_i, aci + ni * p_r - nr * p_i

    hr, hi, acr, aci = lax.fori_loop(0, ns - 1, pass2_paired, (hin_ref[:, :st], hin_ref[:, st:], zero, zero))
    nr, ni = step2(0 if reverse else ns - 1, hr, hi)
    p_r, p_i = p_edge_ref[:, :st], p_edge_ref[:, st:]
    return acr + nr * p_r + ni * p_i, aci + ni * p_r - nr * p_i


def _s5_specs(r, ch, st):
    u_spec = pl.BlockSpec((r, ch), lambda j: (0, j // 2))
    w_spec = pl.BlockSpec((1, ch, 2 * st), lambda j: (j, 0, 0))
    c_spec = pl.BlockSpec((1, 2 * st, ch), lambda j: (j, 0, 0))
    a_spec = pl.BlockSpec((1, 2, st), lambda j: (j, 0, 0))
    return u_spec, w_spec, c_spec, a_spec


def _s5_fwd(up, w, c, a, rev, name):
    r, s = up.shape
    nh, ch, st2 = w.shape
    st = st2 // 2
    ns = r // N_SEG
    nb = r // N_DEV
    u_spec, w_spec, c_spec, a_spec = _s5_specs(r, ch, st)

    def body(u_ref, w_ref, c_ref, a_ref, y_ref, x, fin, hin):
        j = pl.program_id(0)
        w_b = w_ref[0].astype(MXU_DTYPE)
        c_b = c_ref[0].astype(MXU_DTYPE)
        for rb in range(N_DEV):
            rows = slice(rb * nb, (rb + 1) * nb)
            x[rows, :] = jnp.dot(u_ref[rows, :].astype(MXU_DTYPE), w_b, preferred_element_type=F32)
        _s5_scan_into(x, a_ref[0, 0:1, :], a_ref[0, 1:2, :], ns, fin, hin, rev)
        for rb in range(N_DEV):
            rows = slice(rb * nb, (rb + 1) * nb)
            yb = jnp.dot(x[rows, :].astype(MXU_DTYPE), c_b, preferred_element_type=F32)

            @pl.when(j % 2 == 0)
            def _():
                y_ref[rows, :] = yb

            @pl.when(j % 2 == 1)
            def _():
                y_ref[rows, :] += yb

    small = pltpu.VMEM((N_SEG, st2), F32)
    return pl.pallas_call(
        body, name=name, grid=(nh,), in_specs=[u_spec, w_spec, c_spec, a_spec],
        out_specs=pl.BlockSpec((r, ch), lambda j: (0, j // 2)), out_shape=_sds((r, s), F32),
        scratch_shapes=[pltpu.VMEM((r, st2), F32), small, small],
        compiler_params=_params(("arbitrary",)))(up, w, c, a)


def _s5_bwd(up, dyp, w, c, a, rev, name):
    r, s = up.shape
    nh, ch, st2 = w.shape
    st = st2 // 2
    ns = r // N_SEG
    nb = r // N_DEV
    u_spec, w_spec, c_spec, a_spec = _s5_specs(r, ch, st)
    nt = (((1,), (1,)), ((), ()))
    tn = (((0,), (0,)), ((), ()))

    def body(u_ref, dy_ref, w_ref, c_ref, a_ref, du_ref, dw_ref, dc_ref, da_ref, h, g, fin, sin_, ein):
        j = pl.program_id(0)
        w_b = w_ref[0].astype(MXU_DTYPE)
        c_b = c_ref[0].astype(MXU_DTYPE)
        for rb in range(N_DEV):
            rows = slice(rb * nb, (rb + 1) * nb)
            h[rows, :] = jnp.dot(u_ref[rows, :].astype(MXU_DTYPE), w_b, preferred_element_type=F32)
        ar1, ai1 = a_ref[0, 0:1, :], a_ref[0, 1:2, :]
        _s5_scan_into(h, ar1, ai1, ns, fin, sin_, rev)
        dc = jnp.zeros((st2, ch), F32)
        for rb in range(N_DEV):
            rows = slice(rb * nb, (rb + 1) * nb)
            dyb = dy_ref[rows, :].astype(MXU_DTYPE)
            g[rows, :] = lax.dot_general(dyb, c_b, nt, preferred_element_type=F32)
            dc += lax.dot_general(h[rows, :].astype(MXU_DTYPE), dyb, tn, preferred_element_type=F32)
        dc_ref[0] = dc
        acr, aci = _s5_scan_into(g, ar1, -ai1, ns, fin, ein, not rev, paired=(h, sin_, 1 if rev else -1))
        da_ref[0, 0:1, :] = jnp.sum(acr, axis=0, keepdims=True)
        da_ref[0, 1:2, :] = jnp.sum(aci, axis=0, keepdims=True)
        dw = jnp.zeros((ch, st2), F32)
        for rb in range(N_DEV):
            rows = slice(rb * nb, (rb + 1) * nb)
            gb = g[rows, :].astype(MXU_DTYPE)
            dub = lax.dot_general(gb, w_b, nt, preferred_element_type=F32)
            dw += lax.dot_general(u_ref[rows, :].astype(MXU_DTYPE), gb, tn, preferred_element_type=F32)

            @pl.when(j % 2 == 0)
            def _():
                du_ref[rows, :] = dub

            @pl.when(j % 2 == 1)
            def _():
                du_ref[rows, :] += dub

        dw_ref[0] = dw

    small = pltpu.VMEM((N_SEG, st2), F32)
    big = pltpu.VMEM((r, st2), F32)
    return pl.pallas_call(
        body, name=name, grid=(nh,), in_specs=[u_spec, u_spec, w_spec, c_spec, a_spec],
        out_specs=[pl.BlockSpec((r, ch), lambda j: (0, j // 2)), w_spec, c_spec, a_spec],
        out_shape=[_sds((r, s), F32), _sds(w.shape, F32), _sds(c.shape, F32), _sds(a.shape, F32)],
        scratch_shapes=[big, big, small, small, small],
        compiler_params=_params(("arbitrary",)))(up, dyp, w, c, a)


def _rope(t, cos, sin):
    quarter = t.shape[1] // 4
    lane = lax.broadcasted_iota(jnp.int32, t.shape, 1)
    first = (lane // quarter) % 2 == 0
    partner = jnp.where(first, pltpu.roll(t, t.shape[1] - quarter, 1), pltpu.roll(t, quarter, 1))
    return t * cos + partner * sin


def _rope_t(d, cos, sin):
    quarter = d.shape[1] // 4
    ds_ = d * sin
    lane = lax.broadcasted_iota(jnp.int32, d.shape, 1)
    first = (lane // quarter) % 2 == 0
    partner = jnp.where(first, pltpu.roll(ds_, d.shape[1] - quarter, 1), pltpu.roll(ds_, quarter, 1))
    return d * cos + partner


def _chunk_of_step(s, nch, ncc, rev):
    if not rev:
        return s
    return jnp.where(s < ncc, ncc - 1 - s, nch + ncc - 1 - s)


def _heads_per_step(heads, dk, dv, q_off):
    v_off = q_off + 2 * heads * dk
    for hpg in range(heads, 0, -1):
        if heads % hpg == 0 and q_off % (hpg * dk) == 0:
            piece = math.gcd(v_off, hpg * dv)
            if piece % dv == 0:
                return hpg, piece
    return 1, dv


def _v_specs(hpg, dv, piece, v_off, ch, chunk_of):
    n_pieces = hpg * dv // piece
    return [pl.BlockSpec((ch, piece), functools.partial(
        lambda h, s, p: (chunk_of(s), v_off // piece + h * n_pieces + p), p=p)) for p in range(n_pieces)]


def _v_of_head(v_refs, hl, dv, piece):
    lo = (hl * dv) % piece
    return v_refs[(hl * dv) // piece][:, lo:lo + dv]


def _ret_fwd(hm, cos, sin, decay, wend, win, gch, heads, dk, dv, q_off, ncc, rev, name):
    r = hm.shape[0]
    ch = RET_CHUNK
    nch = r // ch
    t_rows = r - ncc * ch
    hpg, piece = _heads_per_step(heads, dk, dv, q_off)
    qb, kb = q_off // (hpg * dk), (q_off + heads * dk) // (hpg * dk)
    q_scale = dk ** -0.5
    nt = (((1,), (1,)), ((), ()))
    tn = (((0,), (0,)), ((), ()))
    cof = lambda s: _chunk_of_step(s, nch, ncc, rev)
    v_specs = _v_specs(hpg, dv, piece, q_off + 2 * heads * dk, ch, cof)
    nv = len(v_specs)

    def body(q_ref, k_ref, *refs):
        v_refs = refs[:nv]
        cos_ref, sin_ref, dec_ref, we_ref, wi_ref, g_ref, o_ref, sin_out, st = refs[nv:]
        s = pl.program_id(1)

        @pl.when(s == 0)
        def _():
            st[...] = jnp.zeros_like(st)

        cos_, sin_ = cos_ref[...], sin_ref[...]
        for hl in range(hpg):
            ks, vs = slice(hl * dk, (hl + 1) * dk), slice(hl * dv, (hl + 1) * dv)
            q = _rope(q_ref[:, ks].astype(F32), cos_, sin_) * q_scale
            k = _rope(k_ref[:, ks].astype(F32), cos_, sin_)
            v = _v_of_head(v_refs, hl, dv, piece).astype(MXU_DTYPE)
            s_cur = st[hl]
            sin_out[hl, 0] = s_cur
            kw = (k * we_ref[hl]).astype(MXU_DTYPE)
            qw = (q * wi_ref[hl]).astype(MXU_DTYPE)
            scores = lax.dot_general(q.astype(MXU_DTYPE), k.astype(MXU_DTYPE), nt,
                                     preferred_element_type=F32) * dec_ref[hl]
            o_ref[:, vs] = (jnp.dot(scores.astype(MXU_DTYPE), v, preferred_element_type=F32)
                            + jnp.dot(qw, s_cur.astype(MXU_DTYPE), preferred_element_type=F32))
            st[hl] = g_ref[hl] * s_cur + lax.dot_general(kw, v, tn, preferred_element_type=F32)

    tab = lambda w: pl.BlockSpec((hpg, ch, w), lambda h, s: (h, 0, 0))
    return pl.pallas_call(
        body, name=name, grid=(heads // hpg, nch),
        in_specs=[pl.BlockSpec((ch, hpg * dk), lambda h, s: (cof(s), qb + h)),
                  pl.BlockSpec((ch, hpg * dk), lambda h, s: (cof(s), kb + h))] + v_specs +
                 [pl.BlockSpec((ch, dk), lambda h, s: (cof(s), 0)),
                  pl.BlockSpec((ch, dk), lambda h, s: (cof(s), 0)),
                  tab(ch), tab(dk), tab(dk), tab(dv)],
        out_specs=[pl.BlockSpec((ch, hpg * dv), lambda h, s: (jnp.maximum(cof(s) - ncc, 0) if not rev
                                                               else jnp.where(s < ncc, nch - ncc - 1, cof(s) - ncc), h)),
                   pl.BlockSpec((hpg, 1, dk, dv), lambda h, s: (h, s, 0, 0))],
        out_shape=[_sds((t_rows, heads * dv), F32), _sds((heads, nch, dk, dv), F32)],
        scratch_shapes=[pltpu.VMEM((hpg, dk, dv), F32)],
        compiler_params=_params(("parallel", "arbitrary")))(hm, hm, *([hm] * nv), cos, sin, decay, wend, win, gch)


def _ret_bwd(hm, cos, sin, decay, wend, win, gch, s_in, do, heads, dk, dv, q_off, ncc, rev, name):
    r = hm.shape[0]
    ch = RET_CHUNK
    nch = r // ch
    hpg, piece = _heads_per_step(heads, dk, dv, q_off)
    qb, kb = q_off // (hpg * dk), (q_off + heads * dk) // (hpg * dk)
    q_scale = dk ** -0.5
    nt = (((1,), (1,)), ((), ()))
    tn = (((0,), (0,)), ((), ()))
    cof = lambda rr: _chunk_of_step(nch - 1 - rr, nch, ncc, rev)
    v_specs = _v_specs(hpg, dv, piece, q_off + 2 * heads * dk, ch, cof)
    nv = len(v_specs)

    def body(q_ref, k_ref, *refs):
        v_refs = refs[:nv]
        (cos_ref, sin_ref, dec_ref, we_ref, wi_ref, g_ref, sin_ref2, do_ref,
         dq_ref, dk_ref, dv_ref, ddec_ref, dwe_ref, dwi_ref, dg_ref, dst) = refs[nv:]
        rr = pl.program_id(1)
        n = cof(rr)

        @pl.when(rr == 0)
        def _():
            dst[...] = jnp.zeros_like(dst)
            ddec_ref[...] = jnp.zeros_like(ddec_ref)
            dwe_ref[...] = jnp.zeros_like(dwe_ref)
            dwi_ref[...] = jnp.zeros_like(dwi_ref)
            dg_ref[...] = jnp.zeros_like(dg_ref)

        cos_, sin_ = cos_ref[...], sin_ref[...]
        for hl in range(hpg):
            ks, vs = slice(hl * dk, (hl + 1) * dk), slice(hl * dv, (hl + 1) * dv)
            q = _rope(q_ref[:, ks].astype(F32), cos_, sin_) * q_scale
            k = _rope(k_ref[:, ks].astype(F32), cos_, sin_)
            v = _v_of_head(v_refs, hl, dv, piece).astype(MXU_DTYPE)
            qb_, kb_ = q.astype(MXU_DTYPE), k.astype(MXU_DTYPE)
            kw = (k * we_ref[hl]).astype(MXU_DTYPE)
            qw = (q * wi_ref[hl]).astype(MXU_DTYPE)
            sraw = lax.dot_general(qb_, kb_, nt, preferred_element_type=F32)
            scores = (sraw * dec_ref[hl]).astype(MXU_DTYPE)
            d_o = jnp.where(n >= ncc, do_ref[:, vs], 0.0).astype(MXU_DTYPE)
            s_n = sin_ref2[hl, 0]
            s_nb = s_n.astype(MXU_DTYPE)
            ds1 = dst[hl]
            ds1b = ds1.astype(MXU_DTYPE)
            dsc = lax.dot_general(d_o, v, nt, preferred_element_type=F32)
            dsr = (dsc * dec_ref[hl]).astype(MXU_DTYPE)
            ddec_ref[hl] += dsc * sraw
            t1 = lax.dot_general(d_o, s_nb, nt, preferred_element_type=F32)
            dq_r = jnp.dot(dsr, kb_, preferred_element_type=F32) + t1 * wi_ref[hl]
            dwi_ref[hl] += t1 * q
            t2 = lax.dot_general(v, ds1b, nt, preferred_element_type=F32)
            dk_r = lax.dot_general(dsr, qb_, tn, preferred_element_type=F32) + t2 * we_ref[hl]
            dwe_ref[hl] += t2 * k
            dv_ref[:, vs] = (lax.dot_general(scores, d_o, tn, preferred_element_type=F32)
                             + jnp.dot(kw, ds1b, preferred_element_type=F32)).astype(dv_ref.dtype)
            dg_ref[hl] += ds1 * s_n
            dst[hl] = g_ref[hl] * ds1 + lax.dot_general(qw, d_o, tn, preferred_element_type=F32)
            dq_ref[:, ks] = (_rope_t(dq_r, cos_, sin_) * q_scale).astype(dq_ref.dtype)
            dk_ref[:, ks] = _rope_t(dk_r, cos_, sin_).astype(dk_ref.dtype)

    tab = lambda w: pl.BlockSpec((hpg, ch, w), lambda h, rr: (h, 0, 0))
    return pl.pallas_call(
        body, name=name, grid=(heads // hpg, nch),
        in_specs=[pl.BlockSpec((ch, hpg * dk), lambda h, rr: (cof(rr), qb + h)),
                  pl.BlockSpec((ch, hpg * dk), lambda h, rr: (cof(rr), kb + h))] + v_specs +
                 [pl.BlockSpec((ch, dk), lambda h, rr: (cof(rr), 0)),
                  pl.BlockSpec((ch, dk), lambda h, rr: (cof(rr), 0)),
                  tab(ch), tab(dk), tab(dk), tab(dv),
                  pl.BlockSpec((hpg, 1, dk, dv), lambda h, rr: (h, nch - 1 - rr, 0, 0)),
                  pl.BlockSpec((ch, hpg * dv), lambda h, rr: (jnp.maximum(cof(rr) - ncc, 0), h))],
        out_specs=[pl.BlockSpec((ch, hpg * dk), lambda h, rr: (cof(rr), h)),
                   pl.BlockSpec((ch, hpg * dk), lambda h, rr: (cof(rr), h)),
                   pl.BlockSpec((ch, hpg * dv), lambda h, rr: (cof(rr), h)),
                   tab(ch), tab(dk), tab(dk), tab(dv)],
        out_shape=[_sds((r, heads * dk), BF16), _sds((r, heads * dk), BF16), _sds((r, heads * dv), BF16),
                   _sds(decay.shape, F32), _sds(wend.shape, F32), _sds(win.shape, F32), _sds(gch.shape, F32)],
        scratch_shapes=[pltpu.VMEM((hpg, dk, dv), F32)],
        compiler_params=_params(("parallel", "arbitrary")))(hm, hm, *([hm] * nv), cos, sin, decay, wend, win, gch, s_in, do)


_HBM = pl.BlockSpec(memory_space=pltpu.HBM)
_MESH = pl.DeviceIdType.MESH
ALL_GATHER_COLLECTIVE_ID = 1
SIBLING_COLLECTIVE_ID = 2
CHIPS_COLLECTIVE_ID = 3


def _axis_slice(ref, axis, start, size):
    idx = [slice(None)] * len(ref.shape)
    idx[axis] = pl.ds(start, size)
    return ref.at[tuple(idx)]


def _sibling_and_chip_peers():
    x, y, c = lax.axis_index("x"), lax.axis_index("y"), lax.axis_index("c")
    return [(x, y, 1 - c), (1 - x, y, c), (x, 1 - y, c), (1 - x, 1 - y, c)]


def _launch_exchange(body, name, operand, out_shape, sems, peers_fn, collective_id, on_sequencer):
    if not on_sequencer:
        return pl.pallas_call(body, name=name, out_shape=out_shape, in_specs=[_HBM], out_specs=_HBM,
                              scratch_shapes=sems)(operand)

    def sequencer_body(in_ref, out_ref, *sem_refs):
        peers = peers_fn()
        barrier = pltpu.get_barrier_semaphore()
        for peer in peers:
            pl.semaphore_signal(barrier, inc=1, device_id=peer, device_id_type=_MESH)
        pl.semaphore_wait(barrier, len(peers))
        body(in_ref, out_ref, *sem_refs)

    return pl.kernel(sequencer_body, out_type=out_shape, name=name,
                     mesh=plsc.ScalarSubcoreMesh(axis_name="sequencer", num_cores=1), scratch_types=sems,
                     compiler_params=pltpu.CompilerParams(collective_id=collective_id))(operand)


def _all_gather(shard, axis, name, on_sequencer=False):
    m = shard.shape[axis]
    out_shape = list(shard.shape)
    out_shape[axis] = N_DEV * m

    def body(x_ref, out_ref, send_sems, recv_sems, local_sem):
        x, y, c = lax.axis_index("x"), lax.axis_index("y"), lax.axis_index("c")
        me, sibling = (x, y, c), (x, y, 1 - c)
        chips = [(1 - x, y), (x, 1 - y), (1 - x, 1 - y)]

        def block(px, py, pc):
            return _axis_slice(out_ref, axis, (4 * px + 2 * py + pc) * m, m)

        def copy(k, blk, to, src=None):
            return pltpu.make_async_remote_copy(
                src_ref=block(*blk) if src is None else src, dst_ref=block(*blk), send_sem=send_sems.at[k],
                recv_sem=recv_sems.at[k], device_id=to, device_id_type=_MESH)

        mine = pltpu.make_async_copy(x_ref, block(*me), local_sem)
        mine.start()
        first = [copy(0, me, sibling, src=x_ref)]
        first += [copy(1 + j, me, (*chip, c), src=x_ref) for j, chip in enumerate(chips)]
        for cp in first:
            cp.start()
        passed = [copy(4 + j, (*chip, c), sibling) for j, chip in enumerate(chips)]
        for j, chip in enumerate(chips):
            copy(1 + j, (*chip, c), me).wait_recv()
            passed[j].start()
        copy(0, sibling, me).wait_recv()
        for j, chip in enumerate(chips):
            copy(4 + j, (*chip, 1 - c), me).wait_recv()
        for cp in first + passed:
            cp.wait_send()
        mine.wait()

    return _launch_exchange(
        body, name, shard, _sds(out_shape, shard.dtype),
        [pltpu.SemaphoreType.DMA((7,)), pltpu.SemaphoreType.DMA((7,)), pltpu.SemaphoreType.DMA(())],
        _sibling_and_chip_peers, ALL_GATHER_COLLECTIVE_ID, on_sequencer)


def _rs_sibling(g, axis, name, on_sequencer=False):
    m = g.shape[axis] // N_DEV
    blk_shape = list(g.shape)
    blk_shape[axis] = m
    n_chips = N_DEV // 2

    def body(g_ref, recv_ref, send_sems, recv_sems):
        x, y, c = lax.axis_index("x"), lax.axis_index("y"), lax.axis_index("c")
        sibling = (x, y, 1 - c)
        send = [pltpu.make_async_remote_copy(
            src_ref=_axis_slice(g_ref, axis, (2 * q + 1 - c) * m, m), dst_ref=recv_ref.at[q],
            send_sem=send_sems.at[q], recv_sem=recv_sems.at[q], device_id=sibling, device_id_type=_MESH)
            for q in range(n_chips)]
        for cp in send:
            cp.start()
        for cp in send:
            cp.wait_recv()
        for cp in send:
            cp.wait_send()

    return _launch_exchange(
        body, name, g, _sds([n_chips] + blk_shape, g.dtype),
        [pltpu.SemaphoreType.DMA((n_chips,)), pltpu.SemaphoreType.DMA((n_chips,))],
        lambda: _sibling_and_chip_peers()[:1], SIBLING_COLLECTIVE_ID, on_sequencer)


def _rs_chips(p, name, on_sequencer=False):
    n_peers = p.shape[0] - 1

    def body(p_ref, out_ref, send_sems, recv_sems):
        x, y, c = lax.axis_index("x"), lax.axis_index("y"), lax.axis_index("c")
        chips = [(1 - x, y), (x, 1 - y), (1 - x, 1 - y)]
        send = [pltpu.make_async_remote_copy(
            src_ref=p_ref.at[2 * cx + cy], dst_ref=out_ref.at[j], send_sem=send_sems.at[j],
            recv_sem=recv_sems.at[j], device_id=(cx, cy, c), device_id_type=_MESH)
            for j, (cx, cy) in enumerate(chips)]
        for cp in send:
            cp.start()
        for cp in send:
            cp.wait_recv()
        for cp in send:
            cp.wait_send()

    return _launch_exchange(
        body, name, p, _sds((n_peers,) + p.shape[1:], p.dtype),
        [pltpu.SemaphoreType.DMA((n_peers,)), pltpu.SemaphoreType.DMA((n_peers,))],
        lambda: _sibling_and_chip_peers()[1:], CHIPS_COLLECTIVE_ID, on_sequencer)


def _reduce_scatter(g, axis, name):
    sib = _rs_sibling(g, axis, name + "_d2d", on_sequencer=True)
    p = _pair_sum(g, sib, axis, name + "_pair")
    return p, _rs_chips(p, name + "_ici", on_sequencer=True)


def _s5_tables(lam_re, lam_im, log_step, b_re, b_im, c_re, c_im):
    nd, g, p, cg = b_re.shape
    step = jnp.exp(log_step)[..., None]
    mag = jnp.exp(lam_re * step)
    a_re, a_im = mag * jnp.cos(lam_im * step), mag * jnp.sin(lam_im * step)
    den = lam_re * lam_re + lam_im * lam_im
    num_re, num_im = a_re - 1.0, a_im
    k_re = (num_re * lam_re + num_im * lam_im) / den
    k_im = (num_im * lam_re - num_re * lam_im) / den
    bb_re = k_re[..., None] * b_re - k_im[..., None] * b_im
    bb_im = k_re[..., None] * b_im + k_im[..., None] * b_re
    gt = g // SSM_TILE_GROUPS
    hg = SSM_HALF_GROUPS
    eye = jnp.eye(SSM_TILE_GROUPS, dtype=F32).reshape(SSM_TILE_GROUPS, 2, hg)

    def pack_b(bb):
        w = jnp.einsum("djhqpc,ghq->djhgcqp", bb.reshape(nd, gt, 2, hg, p, cg), eye)
        return w.reshape(nd, gt * 2, SSM_TILE_GROUPS * cg, hg * p)

    def pack_c(cc):
        w = jnp.einsum("djhqcp,ghq->djhqpgc", cc.reshape(nd, gt, 2, hg, cg, p), eye)
        return w.reshape(nd, gt * 2, hg * p, SSM_TILE_GROUPS * cg)

    a = jnp.stack([a_re.reshape(nd, gt * 2, hg * p), a_im.reshape(nd, gt * 2, hg * p)], axis=2)
    w = jnp.concatenate([pack_b(bb_re), pack_b(bb_im)], axis=-1)
    c = jnp.concatenate([pack_c(c_re), -pack_c(c_im)], axis=-2)
    return w, c, a


def _ret_tables(decay_logit, dk, dv):
    ch = RET_CHUNK
    nd, h = decay_logit.shape
    lg = jax.nn.log_sigmoid(decay_logit)[:, :, None]
    pos = jnp.arange(ch, dtype=F32)
    fwd_diff = pos[:, None] - pos[None, :]
    diff = jnp.stack([fwd_diff, -fwd_diff])[:, None]
    mask = jnp.stack([fwd_diff >= 0, -fwd_diff > 0])[:, None]
    end_pos = jnp.stack([ch - 1.0 - pos, pos])[:, None]
    in_pos = jnp.stack([pos + 1.0, ch - pos])[:, None]
    w_end = jnp.exp(lg * end_pos)
    w_in = jnp.exp(lg * in_pos)
    decay = jnp.where(mask, jnp.exp(lg[..., None] * jnp.where(mask, diff, 0.0)), 0.0)
    g_chunk = jnp.exp(lg[..., 0] * ch)
    return (decay, jnp.broadcast_to(w_end[..., None], (nd, h, ch, dk)), jnp.broadcast_to(w_in[..., None], (nd, h, ch, dk)),
            jnp.broadcast_to(g_chunk[..., None, None], (nd, h, dk, dv)))


def _rope_tables(t_rows, ncc, dk):
    quarter = dk // 4
    idx = np.arange(t_rows)
    row, col = idx // GRID_W, idx % GRID_W
    inv = ROPE_BASE ** (-np.arange(quarter, dtype=np.float32) / quarter)
    ang_r = row.astype(np.float32)[:, None] * inv
    ang_c = col.astype(np.float32)[:, None] * inv
    ang_r, ang_c = jnp.asarray(ang_r, F32), jnp.asarray(ang_c, F32)
    cos = jnp.concatenate([jnp.cos(ang_r), jnp.cos(ang_r), jnp.cos(ang_c), jnp.cos(ang_c)], axis=1)
    sin = jnp.concatenate([-jnp.sin(ang_r), jnp.sin(ang_r), -jnp.sin(ang_c), jnp.sin(ang_c)], axis=1)
    n_ctx = ncc * RET_CHUNK
    cos = jnp.concatenate([jnp.ones((n_ctx, dk), F32), cos], axis=0)
    sin = jnp.concatenate([jnp.zeros((n_ctx, dk), F32), sin], axis=0)
    return cos, sin


def _to_scan_layout(ctx_rows, lat_rows, rev):
    u = jnp.concatenate([lat_rows, ctx_rows] if rev else [ctx_rows, lat_rows], axis=0)
    r, w = u.shape
    return u.reshape(N_SEG, r // N_SEG, w).transpose(1, 0, 2).reshape(r, w)


def _from_scan_layout(yp, n_ctx, rev):
    r, w = yp.shape
    y = yp.reshape(r // N_SEG, N_SEG, w).transpose(1, 0, 2).reshape(r, w)
    return (y[r - n_ctx:], y[:r - n_ctx]) if rev else (y[:n_ctx], y[n_ctx:])


def _pack(parts, width):
    rows = []
    for p in parts:
        flat = p.reshape(-1).astype(F32)
        n = flat.shape[0]
        rows.append(jnp.pad(flat, (0, -n % (SUBLANE * width))).reshape(-1, width))
    return jnp.concatenate(rows, axis=0)


def _packed_rows(n, width):
    return -(-n // (SUBLANE * width)) * SUBLANE


def _unpack(flat2d, shapes):
    width = flat2d.shape[1]
    out, row = [], 0
    for shp in shapes:
        n = int(np.prod(shp))
        nr = _packed_rows(n, width)
        out.append(flat2d[row:row + nr].reshape(-1)[:n].reshape(shp))
        row += nr
    return out


def kernel(x, c, ctx, c_ctx, ada_w, ada_b, norm_g, ffn_w_in, ffn_w_out, mix_w_in, ssm_lam_re, ssm_lam_im, ssm_log_step, ssm_b_re, ssm_b_im, ssm_c_re, ssm_c_im, ssm_d, ssm_glu_w, ret_decay_logit, ret_w_proj, mix_w_out, loss_target, m_c_ctx, m_ada_w, m_ada_b, m_norm_g, m_ffn_w_in, m_ffn_w_out, m_mix_w_in, m_ssm_lam_re, m_ssm_lam_im, m_ssm_log_step, m_ssm_b_re, m_ssm_b_im, m_ssm_c_re, m_ssm_c_im, m_ssm_d, m_ssm_glu_w, m_ret_decay_logit, m_ret_w_proj, m_mix_w_out, v_c_ctx, v_ada_w, v_ada_b, v_norm_g, v_ffn_w_in, v_ffn_w_out, v_mix_w_in, v_ssm_lam_re, v_ssm_lam_im, v_ssm_log_step, v_ssm_b_re, v_ssm_b_im, v_ssm_c_re, v_ssm_c_im, v_ssm_d, v_ssm_glu_w, v_ret_decay_logit, v_ret_w_proj, v_mix_w_out):
    t_rows, d = x.shape[1], x.shape[2]
    n_ctx = ctx.shape[1]
    r = n_ctx + t_rows
    ssm_w = ssm_d.shape[1]
    heads = ret_decay_logit.shape[2]
    mi = mix_w_in.shape[2] * N_DEV
    dk = (mi - ssm_w - 2 * d) // (6 * heads)
    dv = 2 * dk
    qk_w, v_w = heads * dk, heads * dv
    q_off = ssm_w
    ncc = n_ctx // RET_CHUNK
    tile = n_ctx
    nct = 1
    wide_tile = _tile(n_ctx, 128, 16)
    assert r % (N_SEG * SUBLANE) == 0 and n_ctx % RET_CHUNK == 0 and t_rows % tile == 0
    me = 4 * lax.axis_index("x") + 2 * lax.axis_index("y") + lax.axis_index("c")
    g_off = ssm_w + 2 * qk_w + v_w
    gs_off = g_off + v_w

    ng_cols = norm_g.shape[2]
    small0 = _pack([c[0], norm_g[0]], d)
    small0_all = _all_gather(small0, 0, "ag_cond")

    bf = lambda w: w.astype(BF16)
    small0_all, sh_in1 = lax.optimization_barrier((small0_all, bf(ffn_w_in[0, 0])))
    small0_all = small0_all.reshape(N_DEV, -1)
    w_in1 = _all_gather(sh_in1, 1, "ag_ffn1_in", on_sequencer=True)
    w_glu = _all_gather(bf(ssm_glu_w[0]), 1, "ag_glu", on_sequencer=True)
    w_rp = _all_gather(bf(ret_w_proj[0]), 0, "ag_ret_proj", on_sequencer=True)
    w_mo = _all_gather(bf(mix_w_out[0]), 0, "ag_mix_out", on_sequencer=True)
    w_in2 = _all_gather(bf(ffn_w_in[0, 1]), 1, "ag_ffn2_in", on_sequencer=True)
    w_out2 = _all_gather(bf(ffn_w_out[0, 1]), 0, "ag_ffn2_out", on_sequencer=True)

    ng_at = _packed_rows(d, d) * d
    c_all = small0_all[:, :d]
    g_full = small0_all[:, ng_at:ng_at + 6 * ng_cols].reshape(N_DEV, 6, ng_cols).transpose(1, 0, 2).reshape(6, d)
    g6 = g_full.reshape(6, 1, d)
    cc = jnp.concatenate([c_all, c_ctx[None, :], jnp.zeros((2 * SUBLANE - N_DEV - 1, d), F32)], axis=0)
    sc = _silu_rows(cc, "ada_silu")
    na = ada_w.shape[2]
    a_loc = _mm(sc, ada_w[0], "nn", F32, "ada_fwd", tm=16, tn=na, tk=512)
    a_all = _all_gather(a_loc, 0, "ag_ada")
    a_all, sh_out1, sh_mix = lax.optimization_barrier((a_all, bf(ffn_w_out[0, 0]), bf(mix_w_in[0])))
    a_all = a_all.reshape(N_DEV, 2 * SUBLANE, na)
    w_out1 = _all_gather(sh_out1, 0, "ag_ffn1_out", on_sequencer=True)
    w_mix = _all_gather(sh_mix, 1, "ag_mix_in", on_sequencer=True)
    ada_x = lax.dynamic_index_in_dim(a_all, me, axis=1, keepdims=False).reshape(9 * d) + ada_b[0]
    ada_c = a_all[:, N_DEV, :].reshape(9 * d) + ada_b[0]
    mods = jnp.stack([ada_c.reshape(9, d), ada_x.reshape(9, d)]).reshape(18, 1, d)

    xin = jnp.concatenate([ctx[0], x[0]], axis=0)
    u1 = _ada_pre_fwd(xin, g6, mods, 0, 0, nct, tile, "pre1")
    g1, up1, a1 = _mm_swiglu(u1, w_in1, "ffn1_in", tm=1088)
    o1 = _mm(a1, w_out1, "nn", BF16, "ffn1_out", tm=544, tn=1024, tk=2816)
    x1, u2 = _ada_post_fwd(xin, o1, g6, mods, 1, 0, 0.5, nct, tile, "post1_pre2", then_pre=(2, 1))
    hm = _mm(u2, w_mix, "nn", BF16, "mix_in", tm=1088, tn=1024)

    us_ctx, us_lat = hm[:n_ctx, :ssm_w], hm[n_ctx:, :ssm_w]
    dskip = ssm_d.reshape(1, 1, ssm_w)
    s5_prm = (ssm_lam_re[0], ssm_lam_im[0], ssm_log_step[0], ssm_b_re[0], ssm_b_im[0], ssm_c_re[0], ssm_c_im[0])
    s5_tabs_both, s5_vjp = jax.vjp(_s5_tables, *s5_prm)
    s5_tabs, ups, y_dirs = [], [], []
    for dr in range(2):
        tabs = tuple(t[dr] for t in s5_tabs_both)
        up = _to_scan_layout(us_ctx, us_lat, dr == 1)
        yp = _s5_fwd(up, *tabs, dr == 1, "s5_fwd%d" % dr)
        s5_tabs.append(tabs)
        ups.append(up)
        y_dirs.append(_from_scan_layout(yp, n_ctx, dr == 1)[1])
    a_ssm = _ssm_out_fwd(y_dirs[0], y_dirs[1], hm, dskip, nct, tile, "ssm_out")
    gab = _mm(a_ssm, w_glu, "nn", BF16, "glu", tm=512, tn=2048, tk=ssm_w)

    cos, sin = _rope_tables(t_rows, ncc, dk)
    ret_tabs_both, ret_vjp = jax.vjp(functools.partial(_ret_tables, dk=dk, dv=dv), ret_decay_logit[0])
    ret_tabs, o_dirs, s_ins = [], [], []
    for dr in range(2):
        tabs = tuple(t[dr] for t in ret_tabs_both)
        o_d, s_in = _ret_fwd(hm, cos, sin, *tabs, heads, dk, dv, q_off, ncc, dr == 1, "ret_fwd%d" % dr)
        ret_tabs.append(tabs)
        o_dirs.append(o_d)
        s_ins.append(s_in)
    ret_in = _ret_gate_fwd(o_dirs[0], o_dirs[1], hm, g_off, heads, dv, nct, tile, "ret_gate")
    rb = _mm(ret_in, w_rp, "nn", BF16, "ret_proj", tm=512, tn=d, tk=v_w)
    merged = _merge_fwd(gab, rb, hm, gs_off, nct, tile, "merge")
    mix = _mm(merged, w_mo, "nn", BF16, "mix_out", tm=512, tn=d, tk=d)
    x2, u3 = _ada_post_fwd(x1, mix, g6, mods, 3, 1, 1.0, 0, tile, "post2_pre3", h_tile_offset=nct, then_pre=(4, 2))
    g3, up3, a3 = _mm_swiglu(u3, w_in2, "ffn2_in", tm=1024)
    o3 = _mm(a3, w_out2, "nn", BF16, "ffn2_out", tm=512, tn=1024, tk=2816)
    dy, lcols = _ada_post_fwd(x2, o3, g6, mods, 5, 2, 0.5, 0, tile, "post3_loss", target=loss_target[0])
    loss_part = (0.5 * jnp.sum(lcols) / d).reshape(1)

    dg6 = [None] * 6
    dmod = {}

    def add_mod(sel_rows, k, val):
        for sel, row in sel_rows:
            dmod[(sel, k)] = dmod.get((sel, k), 0.0) + val[row, 0]

    both, lat = [(0, 0), (1, 1)], [(1, 0)]

    def tie(*vals):
        return lax.optimization_barrier(vals)

    def big_update(w3d, m3d, v3d, layer, gfull, axis, name, filled=None):
        p, recv = _reduce_scatter(gfull, axis, "rs_" + name)
        return _adamw_scattered(w3d, m3d, v3d, layer, p, recv, "adamw_" + name, filled)

    do3, dg6[5], dgt = _ada_post_bwd(dy, o3, g6, mods, 5, 2, 0.5, 0, 1, tile, "post3_bwd")
    add_mod(lat, 8, dgt)
    gw_out2 = _mm(a3, do3, "tn", BF16, "ffn2_out_dw", tm=1408, tn=1024, tk=2176)
    do3, gw_out2 = tie(do3, gw_out2)
    up_out2 = big_update(ffn_w_out[0], m_ffn_w_out[0], v_ffn_w_out[0], 1, gw_out2, 0, "ffn2_out")
    da3 = _mm(do3, w_out2, "nt", BF16, "ffn2_out_dx", tm=512, tn=2816, tk=d)
    dh3 = _swiglu_bwd(g3, up3, da3, wide_tile, "swiglu2_bwd")
    gw_in2 = _mm(u3, dh3, "tn", BF16, "ffn2_in_dw", tm=1024, tn=1024, tk=2176)
    dh3, gw_in2 = tie(dh3, gw_in2)
    up_in2 = big_update(ffn_w_in[0], m_ffn_w_in[0], v_ffn_w_in[0], 1, gw_in2, 1, "ffn2_in")
    du3 = _mm(dh3, w_in2, "nt", F32, "ffn2_in_dx", tm=512, tn=d, tk=1024)
    dx2, dg6[4], dsh, dsc = _ada_pre_bwd(x2, du3, dy, g6, mods, 4, 2, 0, 1, tile, "pre3_bwd")
    add_mod(lat, 6, dsh)
    add_mod(lat, 7, dsc)
    dmix, dg6[3], dgt = _ada_post_bwd(dx2, mix, g6, mods, 3, 1, 1.0, 0, 1, tile, "post2_bwd")
    add_mod(lat, 5, dgt)
    gw_mo = _mm(merged, dmix, "tn", BF16, "mix_out_dw", tm=1024, tn=1024, tk=2176)
    dmix, gw_mo = tie(dmix, gw_mo)
    up_mo = big_update(mix_w_out, m_mix_w_out, v_mix_w_out, 0, gw_mo, 0, "mix_out")
    dmerged = _mm(dmix, w_mo, "nt", BF16, "mix_out_dx", tm=512, tn=d, tk=d)
    dgab, drb, dgs, dgr = _merge_bwd(gab, rb, hm, gs_off, dmerged, nct, tile, "merge_bwd")
    gw_glu = _mm(a_ssm, dgab, "tn", BF16, "glu_dw", tm=1024, tn=1024, tk=2176)
    gw_rp = _mm(ret_in, drb, "tn", BF16, "ret_proj_dw", tm=1024, tn=1024, tk=2176)
    dgab, drb, gw_glu, gw_rp = tie(dgab, drb, gw_glu, gw_rp)
    up_glu = big_update(ssm_glu_w, m_ssm_glu_w, v_ssm_glu_w, 0, gw_glu, 1, "glu")
    up_rp = big_update(ret_w_proj, m_ret_w_proj, v_ret_w_proj, 0, gw_rp, 0, "ret_proj")
    da_ssm = _mm(dgab, w_glu, "nt", BF16, "glu_dx", tm=512, tn=ssm_w, tk=2 * d)
    dret_in = _mm(drb, w_rp, "nt", BF16, "ret_proj_dx", tm=512, tn=v_w, tk=d)
    d_o, dg_gate = _ret_gate_bwd(o_dirs[0], o_dirs[1], hm, g_off, dret_in, heads, dv, nct, tile, "ret_gate_bwd")
    dy_ssm, dus_direct, d_dskip = _ssm_out_bwd(y_dirs[0], y_dirs[1], hm, dskip, da_ssm, nct, tile, "ssm_out_bwd")
    s5_table_grads, du_ctx, du_lat = [], [], [dus_direct]
    for dr in range(2):
        dyp = _to_scan_layout(jnp.zeros((n_ctx, ssm_w), F32), dy_ssm, dr == 1)
        if dr == 1:
            dyp, up_out2, up_in2 = tie(dyp, up_out2, up_in2)
        outs = _s5_bwd(ups[dr], dyp, *s5_tabs[dr], dr == 1, "s5_bwd%d" % dr)
        part_ctx, part_lat = _from_scan_layout(outs[0], n_ctx, dr == 1)
        du_ctx.append(part_ctx)
        du_lat.append(part_lat)
        s5_table_grads.append(outs[1:])
    dqkv, ret_table_grads = [], []
    for dr in range(2):
        if dr == 1:
            d_o, up_mo, up_glu, up_rp = tie(d_o, up_mo, up_glu, up_rp)
        outs = _ret_bwd(hm, cos, sin, *ret_tabs[dr], s_ins[dr], d_o, heads, dk, dv, q_off, ncc, dr == 1,
                        "ret_bwd%d" % dr)
        dqkv.append(outs[:3])
        ret_table_grads.append(outs[3:])
    both_dirs = lambda grads: tuple(jnp.stack([g0, g1]) for g0, g1 in zip(*grads))
    early_parts = list(s5_vjp(both_dirs(s5_table_grads))) + list(ret_vjp(both_dirs(ret_table_grads)))
    s5_names = 7
    early_shapes = [p.shape for p in early_parts]
    early_all = _all_gather(_pack(early_parts, 1024), 0, "ag_s5_grads", on_sequencer=True)
    early_sum = _sum_leading(early_all.reshape(N_DEV, -1, 1024), "sum_s5_grads")
    dus = jnp.concatenate([du_ctx[0] + du_ctx[1], du_lat[0] + du_lat[1] + du_lat[2]], axis=0)
    dhm = _assemble_dhm(dus, dqkv[0][0], dqkv[1][0], dqkv[0][1], dqkv[1][1], dqkv[0][2], dqkv[1][2],
                        dg_gate, dgs, dgr, n_ctx // wide_tile, wide_tile, "assemble_dhm")
    gw_mix = _mm(u2, dhm, "tn", BF16, "mix_in_dw", tm=1024, tn=1024, tk=2176)
    dhm, gw_mix = tie(dhm, gw_mix)
    up_mix = big_update(mix_w_in, m_mix_w_in, v_mix_w_in, 0, gw_mix, 1, "mix_in")
    du2 = _mm(dhm, w_mix, "nt", F32, "mix_in_dx", tm=544, tn=d, tk=1024)
    dx1, dg6[2], dsh, dsc = _ada_pre_bwd(x1, du2, dx2, g6, mods, 2, 1, nct, 2, tile, "pre2_bwd", dres_x_only=True)
    add_mod(both, 3, dsh)
    add_mod(both, 4, dsc)
    do1, dg6[1], dgt = _ada_post_bwd(dx1, o1, g6, mods, 1, 0, 0.5, nct, 2, tile, "post1_bwd")
    add_mod(both, 2, dgt)
    gw_out1 = _mm(a1, do1, "tn", BF16, "ffn1_out_dw", tm=1408, tn=1024, tk=2176)
    do1, gw_out1 = tie(do1, gw_out1)
    up_out1 = big_update(ffn_w_out[0], m_ffn_w_out[0], v_ffn_w_out[0], 0, gw_out1, 0, "ffn1_out", filled=up_out2)
    da1 = _mm(do1, w_out1, "nt", BF16, "ffn1_out_dx", tm=544, tn=2816, tk=d)
    dh1 = _swiglu_bwd(g1, up1, da1, wide_tile, "swiglu1_bwd")
    dh1, up_mix, early_sum = tie(dh1, up_mix, early_sum)
    early_sums = _unpack(early_sum, early_shapes)
    gw_in1 = _mm(u1, dh1, "tn", BF16, "ffn1_in_dw", tm=1024, tn=1024, tk=2176)
    dh1, gw_in1 = tie(dh1, gw_in1)
    up_in1 = big_update(ffn_w_in[0], m_ffn_w_in[0], v_ffn_w_in[0], 0, gw_in1, 1, "ffn1_in", filled=up_in2)
    du1 = _mm(dh1, w_in1, "nt", F32, "ffn1_in_dx", tm=544, tn=d, tk=1024)
    dx_lat, dg6[0], dsh, dsc = _ada_pre_bwd(xin, du1, dx1, g6, mods, 0, 0, nct, 2, tile, "pre1_bwd",
                                            latent_dh_only=True)
    add_mod(both, 0, dsh)
    add_mod(both, 1, dsc)
    grad_x = dx_lat[None]

    zero_d = jnp.zeros((d,), F32)
    d_ada_x = jnp.stack([dmod.get((1, k), zero_d) for k in range(9)]).reshape(9 * d)
    d_ada_c = jnp.stack([dmod.get((0, k), zero_d) for k in range(9)]).reshape(9 * d)
    dg_full = jnp.stack([g[0, 0] for g in dg6])
    small_parts = [d_ada_x, d_ada_c, dg_full, d_dskip, loss_part]
    small_shapes = [p.shape for p in small_parts]
    packed = _pack(small_parts, 1024)
    gathered = _all_gather(packed, 0, "ag_small_grads").reshape(N_DEV, -1, 1024)
    summed = _sum_leading(gathered, "sum_small_grads")
    sums = _unpack(summed, small_shapes)
    sum_dx, sum_dc, sum_dg = sums[0], sums[1], sums[2]
    loss = sums[4][0]
    grad_ada_b = (sum_dx + sum_dc)[None]
    dx_rows = gathered.reshape(N_DEV, -1)[:, :9 * d]
    col0 = me * na
    da_rows = jnp.concatenate([lax.dynamic_slice_in_dim(dx_rows, col0, na, axis=1),
                               lax.dynamic_slice_in_dim(sum_dc[None], col0, na, axis=1),
                               jnp.zeros((2 * SUBLANE - N_DEV - 1, na), F32)], axis=0)
    grad_ada_w = _mm(sc, da_rows, "tn", F32, "ada_dw", tm=512, tn=na, tk=16)
    d_sc = _mm(da_rows, ada_w[0], "nt", F32, "ada_dx", tm=16, tn=512, tk=na)
    d_sc_all = _all_gather(jnp.broadcast_to(d_sc[N_DEV:N_DEV + 1], (SUBLANE, d)), 0, "ag_dctx")
    d_sc_sum = _sum_leading(d_sc_all.reshape(N_DEV, SUBLANE, d), "sum_dctx")
    grad_c_ctx = _silu_grad_rows(jnp.broadcast_to(c_ctx[None], (SUBLANE, d)), d_sc_sum, "ctx_silu_bwd")[0]
    grad_norm_g = lax.dynamic_slice_in_dim(sum_dg, me * ng_cols, ng_cols, axis=1)[None]

    upd = {}
    upd["ffn_w_in"] = [o[None] for o in up_in1]
    upd["ffn_w_out"] = [o[None] for o in up_out1]
    upd["mix_w_in"] = list(up_mix)
    upd["ssm_glu_w"] = list(up_glu)
    upd["ret_w_proj"] = list(up_rp)
    upd["mix_w_out"] = list(up_mo)
    upd["ada_w"] = [o[None] for o in _adamw(ada_w[0], m_ada_w[0], v_ada_w[0], grad_ada_w[None], "adamw_ada_w")]

    small_names = ["c_ctx", "ada_b", "norm_g", "ssm_lam_re", "ssm_lam_im", "ssm_log_step", "ssm_b_re", "ssm_b_im",
                   "ssm_c_re", "ssm_c_im", "ssm_d", "ret_decay_logit"]
    small_w = [c_ctx, ada_b, norm_g, ssm_lam_re, ssm_lam_im, ssm_log_step, ssm_b_re, ssm_b_im, ssm_c_re, ssm_c_im,
               ssm_d, ret_decay_logit]
    small_m = [m_c_ctx, m_ada_b, m_norm_g, m_ssm_lam_re, m_ssm_lam_im, m_ssm_log_step, m_ssm_b_re, m_ssm_b_im,
               m_ssm_c_re, m_ssm_c_im, m_ssm_d, m_ret_decay_logit]
    small_v = [v_c_ctx, v_ada_b, v_norm_g, v_ssm_lam_re, v_ssm_lam_im, v_ssm_log_step, v_ssm_b_re, v_ssm_b_im,
               v_ssm_c_re, v_ssm_c_im, v_ssm_d, v_ret_decay_logit]
    small_g = [grad_c_ctx, grad_ada_b, grad_norm_g] + [s[None] for s in early_sums[:s5_names]] + \
              [sums[3].reshape(ssm_d.shape), early_sums[s5_names][None]]
    shapes = [w.shape for w in small_w]
    res = _adamw(_pack(small_w, 1024), _pack(small_m, 1024), _pack(small_v, 1024), _pack(small_g, 1024)[None],
                 "adamw_small")
    small_out = [_unpack(o, shapes) for o in res]
    for i, nm in enumerate(small_names):
        upd[nm] = [small_out[kind][i] for kind in range(4)]

    order = ["c_ctx", "ada_w", "ada_b", "norm_g", "ffn_w_in", "ffn_w_out", "mix_w_in", "ssm_lam_re", "ssm_lam_im",
             "ssm_log_step", "ssm_b_re", "ssm_b_im", "ssm_c_re", "ssm_c_im", "ssm_d", "ssm_glu_w", "ret_decay_logit",
             "ret_w_proj", "mix_w_out"]
    outs = [loss, grad_x]
    for kind in range(4):
        outs += [upd[nm][kind] for nm in order]
    return tuple(outs)
```

```python
import functools
import math

import jax
import jax.numpy as jnp
import numpy as np
from jax import lax
from jax.experimental import pallas as pl
from jax.experimental.pallas import tpu as pltpu
from jax.experimental.pallas import tpu_sc as plsc

F32 = jnp.float32
BF16 = jnp.bfloat16
MXU_DTYPE = jnp.bfloat16
MESH_AXES = ("x", "y", "c")
N_DEV = 8
V7X_VMEM_LIMIT_BYTES = 56 * 1024 * 1024
LANE = 128
SUBLANE = 8

GRID_W = 64
RET_CHUNK = 128
ROPE_BASE = 10000.0
NORM_EPS = 1e-6
ADAM_LR = 0.001
ADAM_B1 = 0.9
ADAM_B2 = 0.999
ADAM_EPS = 1e-08
ADAM_WD = 0.01
ADAM_STEP = 10
SSM_TILE_GROUPS = 8
SSM_HALF_GROUPS = 4
N_SEG = 16


def _params(sem=None):
    return pltpu.CompilerParams(dimension_semantics=sem, vmem_limit_bytes=V7X_VMEM_LIMIT_BYTES)


def _tile(n, target, mult):
    best = None
    t = mult
    while t <= min(n, target):
        if n % t == 0:
            best = t
        t += mult
    return n if best is None else best


def _sds(shape, dtype):
    return jax.ShapeDtypeStruct(tuple(shape), dtype)


def _mm(a, b, dims, out_dtype, name, tm=512, tn=1408, tk=2048):
    if dims == "nn":
        (m, k), (k2, n) = a.shape, b.shape
    elif dims == "nt":
        (m, k), (n, k2) = a.shape, b.shape
    else:
        (k, m), (k2, n) = a.shape, b.shape
    assert k == k2, (a.shape, b.shape, dims)
    tm = _tile(m, tm, 16)
    tn = _tile(n, tn, LANE)
    tk = _tile(k, tk, LANE if dims != "tn" else 16)
    nk = k // tk
    dn = {"nn": (((1,), (0,)), ((), ())), "nt": (((1,), (1,)), ((), ())), "tn": (((0,), (0,)), ((), ()))}[dims]

    def product(a_ref, b_ref):
        return lax.dot_general(a_ref[...].astype(MXU_DTYPE), b_ref[...].astype(MXU_DTYPE), dn,
                               preferred_element_type=F32)

    def body_single(a_ref, b_ref, o_ref):
        o_ref[...] = product(a_ref, b_ref).astype(o_ref.dtype)

    def body(a_ref, b_ref, o_ref, acc_ref):
        kk = pl.program_id(2)

        @pl.when(kk == 0)
        def _():
            acc_ref[...] = product(a_ref, b_ref)

        @pl.when((kk > 0) & (kk < nk - 1))
        def _():
            acc_ref[...] += product(a_ref, b_ref)

        @pl.when(kk == nk - 1)
        def _():
            o_ref[...] = (acc_ref[...] + product(a_ref, b_ref)).astype(o_ref.dtype)

    if dims == "nn":
        a_spec = pl.BlockSpec((tm, tk), lambda j, i, kk: (i, kk))
        b_spec = pl.BlockSpec((tk, tn), lambda j, i, kk: (kk, j))
    elif dims == "nt":
        a_spec = pl.BlockSpec((tm, tk), lambda j, i, kk: (i, kk))
        b_spec = pl.BlockSpec((tn, tk), lambda j, i, kk: (j, kk))
    else:
        a_spec = pl.BlockSpec((tk, tm), lambda j, i, kk: (kk, i))
        b_spec = pl.BlockSpec((tk, tn), lambda j, i, kk: (kk, j))
    return pl.pallas_call(
        body_single if nk == 1 else body, name=name, grid=(n // tn, m // tm, nk), in_specs=[a_spec, b_spec],
        out_specs=pl.BlockSpec((tm, tn), lambda j, i, kk: (i, j)), out_shape=_sds((m, n), out_dtype),
        scratch_shapes=[] if nk == 1 else [pltpu.VMEM((tm, tn), F32)],
        compiler_params=_params(("parallel", "parallel", "arbitrary")))(a, b)


def _mm_swiglu(a, b, name, tm=512, tn=512):
    m, k = a.shape
    k2, f2 = b.shape
    f = f2 // 2
    assert k == k2
    tm = _tile(m, tm, 16)
    tn = _tile(f, tn, 2 * LANE)
    nj = f // tn

    def body(a_ref, bg_ref, bu_ref, g_ref, u_ref, act_ref):
        av = a_ref[...].astype(MXU_DTYPE)
        gate = jnp.dot(av, bg_ref[...].astype(MXU_DTYPE), preferred_element_type=F32)
        up = jnp.dot(av, bu_ref[...].astype(MXU_DTYPE), preferred_element_type=F32)
        g_ref[...] = gate.astype(g_ref.dtype)
        u_ref[...] = up.astype(u_ref.dtype)
        act_ref[...] = (gate * _sigmoid(gate) * up).astype(act_ref.dtype)

    tile = pl.BlockSpec((tm, tn), lambda j, i: (i, j))
    out = _sds((m, f), BF16)
    return pl.pallas_call(
        body, name=name, grid=(nj, m // tm),
        in_specs=[pl.BlockSpec((tm, k), lambda j, i: (i, 0)), pl.BlockSpec((k, tn), lambda j, i: (0, j)),
                  pl.BlockSpec((k, tn), lambda j, i: (0, j + nj))],
        out_specs=[tile, tile, tile], out_shape=[out, out, out],
        compiler_params=_params(("parallel", "parallel")))(a, b, b)


def _rows(name, body, n_tiles, ins, outs):
    in_specs = [pl.BlockSpec(blk, imap) for (_, blk, imap) in ins]
    out_specs = [pl.BlockSpec(blk, imap) for (_, _, blk, imap) in outs]
    out_shape = [_sds(shape, dt) for (shape, dt, _, _) in outs]
    res = pl.pallas_call(body, name=name, grid=(n_tiles,), in_specs=in_specs, out_specs=out_specs,
                         out_shape=out_shape, compiler_params=_params(("arbitrary",)))(*[a for (a, _, _) in ins])
    return res


def _row_in(arr, tile, width=None, col=0, x_only_offset=None):
    width = arr.shape[1] if width is None else width
    if x_only_offset is None:
        return (arr, (tile, width), lambda i: (i, col))
    return (arr, (tile, width), lambda i: (jnp.maximum(i - x_only_offset, 0), col))


def _vec_in(arr, idx_fn):
    return (arr, (1, 1, arr.shape[2]), lambda i: (idx_fn(i), 0, 0))


def _rms(h):
    return lax.rsqrt(jnp.mean(h * h, axis=-1, keepdims=True) + NORM_EPS)


def _sigmoid(z):
    return 1.0 / (1.0 + jnp.exp(-z))


def _ada_pre_fwd(h, g6, mods, gi, mi, nct, tile, name):
    r, d = h.shape
    sel = lambda i: jnp.where(i >= nct, 1, 0)

    def body(h_ref, g_ref, sh_ref, sc_ref, u_ref):
        hh = h_ref[...]
        n = hh * _rms(hh) * g_ref[0]
        u_ref[...] = (n * (1.0 + sc_ref[0]) + sh_ref[0]).astype(u_ref.dtype)

    (u,) = _rows(name, body, r // tile,
                 [_row_in(h, tile), _vec_in(g6, lambda i: gi), _vec_in(mods, lambda i: sel(i) * 9 + 3 * mi),
                  _vec_in(mods, lambda i: sel(i) * 9 + 3 * mi + 1)],
                 [((r, d), BF16, (tile, d), lambda i: (i, 0))])
    return u


def _ada_pre_bwd(h, du, dres, g6, mods, gi, mi, nct, nsel, tile, name, dres_x_only=False, latent_dh_only=False):
    r, d = h.shape
    dh_rows = r - nct * tile if latent_dh_only else r
    dh_map = (lambda i: (jnp.maximum(i - nct, 0), 0)) if latent_dh_only else (lambda i: (i, 0))
    sel = lambda i: jnp.where(i >= nct, 1, 0) if nsel == 2 else 0
    msel = lambda i: jnp.where(i >= nct, 1, 0)
    off = nct if dres_x_only else None

    def body(h_ref, du_ref, dr_ref, g_ref, sc_ref, dh_ref, dg_ref, dsh_ref, dsc_ref):
        i = pl.program_id(0)
        hh = h_ref[...]
        rr = _rms(hh)
        g = g_ref[0]
        hn = hh * rr
        n = hn * g
        du_ = du_ref[...].astype(F32)
        dn = du_ * (1.0 + sc_ref[0])

        @pl.when(i == 0)
        def _():
            dg_ref[...] = jnp.zeros_like(dg_ref)

        @pl.when((i == 0) | (i == nct))
        def _():
            dsh_ref[...] = jnp.zeros_like(dsh_ref)
            dsc_ref[...] = jnp.zeros_like(dsc_ref)

        dg_ref[0] += jnp.sum(dn * hn, axis=0, keepdims=True)
        dsh_ref[0] += jnp.sum(du_, axis=0, keepdims=True)
        dsc_ref[0] += jnp.sum(du_ * n, axis=0, keepdims=True)
        t = dn * g
        dh = rr * t - hn * (rr * jnp.mean(t * hn, axis=-1, keepdims=True))
        if dres_x_only:
            dh_ref[...] = dh + jnp.where(i >= nct, dr_ref[...], 0.0)
        else:
            dh_ref[...] = dh + dr_ref[...]

    dh, dg, dsh, dsc = _rows(
        name, body, r // tile,
        [_row_in(h, tile), _row_in(du, tile), _row_in(dres, tile, x_only_offset=off), _vec_in(g6, lambda i: gi),
         _vec_in(mods, lambda i: msel(i) * 9 + 3 * mi + 1)],
        [((dh_rows, d), F32, (tile, d), dh_map), ((1, 1, d), F32, (1, 1, d), lambda i: (0, 0, 0)),
         ((nsel, 1, d), F32, (1, 1, d), lambda i: (sel(i), 0, 0)),
         ((nsel, 1, d), F32, (1, 1, d), lambda i: (sel(i), 0, 0))])
    return dh, dg, dsh, dsc


def _ada_post_fwd(h, o, g6, mods, gi, mi, res_w, nct, tile, name, h_tile_offset=0, then_pre=None, target=None):
    r, d = o.shape
    sel = lambda i: jnp.where(i >= nct, 1, 0)

    def body(h_ref, o_ref, g_ref, gt_ref, *refs):
        oo = o_ref[...].astype(F32)
        n = oo * _rms(oo) * g_ref[0]
        y = h_ref[...] + res_w * gt_ref[0] * n
        if target is not None:
            t_ref, dy_ref, l_ref = refs
            e = y - t_ref[...]
            dy_ref[...] = e * (1.0 / d)

            @pl.when(pl.program_id(0) == 0)
            def _():
                l_ref[...] = jnp.zeros_like(l_ref)

            l_ref[0] += jnp.sum(e * e, axis=0, keepdims=True)
        elif then_pre is not None:
            g2_ref, sh_ref, sc_ref, y_ref, u_ref = refs
            y_ref[...] = y
            u_ref[...] = (y * _rms(y) * g2_ref[0] * (1.0 + sc_ref[0]) + sh_ref[0]).astype(u_ref.dtype)
        else:
            refs[0][...] = y

    ins = [(h, (tile, d), lambda i: (i + h_tile_offset, 0)), _row_in(o, tile), _vec_in(g6, lambda i: gi),
           _vec_in(mods, lambda i: sel(i) * 9 + 3 * mi + 2)]
    row_out = lambda dt: ((r, d), dt, (tile, d), lambda i: (i, 0))
    if target is not None:
        ins.append(_row_in(target, tile))
        outs = [row_out(F32), ((1, 1, d), F32, (1, 1, d), lambda i: (0, 0, 0))]
    elif then_pre is not None:
        gi2, mi2 = then_pre
        ins += [_vec_in(g6, lambda i: gi2), _vec_in(mods, lambda i: sel(i) * 9 + 3 * mi2),
                _vec_in(mods, lambda i: sel(i) * 9 + 3 * mi2 + 1)]
        outs = [row_out(F32), row_out(BF16)]
    else:
        outs = [row_out(F32)]
    res = _rows(name, body, r // tile, ins, outs)
    return res[0] if len(res) == 1 else res


def _ada_post_bwd(dy, o, g6, mods, gi, mi, res_w, nct, nsel, tile, name):
    r, d = o.shape
    sel = lambda i: jnp.where(i >= nct, 1, 0) if nsel == 2 else 0
    msel = lambda i: jnp.where(i >= nct, 1, 0)

    def body(dy_ref, o_ref, g_ref, gt_ref, do_ref, dg_ref, dgt_ref):
        i = pl.program_id(0)
        oo = o_ref[...].astype(F32)
        rr = _rms(oo)
        g = g_ref[0]
        on = oo * rr
        dy_ = dy_ref[...] * res_w

        @pl.when(i == 0)
        def _():
            dg_ref[...] = jnp.zeros_like(dg_ref)

        @pl.when((i == 0) | (i == nct))
        def _():
            dgt_ref[...] = jnp.zeros_like(dgt_ref)

        dgt_ref[0] += jnp.sum(dy_ * (on * g), axis=0, keepdims=True)
        dn = dy_ * gt_ref[0]
        dg_ref[0] += jnp.sum(dn * on, axis=0, keepdims=True)
        t = dn * g
        do_ref[...] = (rr * t - on * (rr * jnp.mean(t * on, axis=-1, keepdims=True))).astype(do_ref.dtype)

    do, dg, dgt = _rows(
        name, body, r // tile,
        [_row_in(dy, tile), _row_in(o, tile), _vec_in(g6, lambda i: gi),
         _vec_in(mods, lambda i: msel(i) * 9 + 3 * mi + 2)],
        [((r, d), BF16, (tile, d), lambda i: (i, 0)), ((1, 1, d), F32, (1, 1, d), lambda i: (0, 0, 0)),
         ((nsel, 1, d), F32, (1, 1, d), lambda i: (sel(i), 0, 0))])
    return do, dg, dgt


def _swiglu_bwd(gate, up, da, tile, name):
    r, f = gate.shape

    def body(g_ref, u_ref, da_ref, dh_ref):
        gt = g_ref[...].astype(F32)
        d = da_ref[...].astype(F32)
        sg = _sigmoid(gt)
        dh_ref[:, :f] = (d * u_ref[...].astype(F32) * (sg * (1.0 + gt * (1.0 - sg)))).astype(dh_ref.dtype)
        dh_ref[:, f:] = (d * gt * sg).astype(dh_ref.dtype)

    (dh,) = _rows(name, body, r // tile, [_row_in(gate, tile), _row_in(up, tile), _row_in(da, tile)],
                  [((r, 2 * f), BF16, (tile, 2 * f), lambda i: (i, 0))])
    return dh


def _gelu_parts(y):
    c0 = math.sqrt(2.0 / math.pi)
    inner = c0 * (y + 0.044715 * y * y * y)
    th = jnp.tanh(inner)
    return th, c0 * (1.0 + 3 * 0.044715 * y * y)


def _ssm_out_fwd(y0, y1, hm, dskip, nct, tile, name):
    t_rows, s = y0.shape

    def body(y0_ref, y1_ref, u_ref, d_ref, a_ref):
        y = y0_ref[...] + y1_ref[...] + d_ref[0] * u_ref[...].astype(F32)
        th, _ = _gelu_parts(y)
        a_ref[...] = (0.5 * y * (1.0 + th)).astype(a_ref.dtype)

    (a,) = _rows(name, body, t_rows // tile,
                 [_row_in(y0, tile), _row_in(y1, tile), (hm, (tile, s), lambda i: (i + nct, 0)),
                  _vec_in(dskip, lambda i: 0)],
                 [((t_rows, s), BF16, (tile, s), lambda i: (i, 0))])
    return a


def _ssm_out_bwd(y0, y1, hm, dskip, da, nct, tile, name):
    t_rows, s = y0.shape

    def body(y0_ref, y1_ref, u_ref, d_ref, da_ref, dy_ref, du_ref, dd_ref):
        i = pl.program_id(0)
        u = u_ref[...].astype(F32)
        y = y0_ref[...] + y1_ref[...] + d_ref[0] * u
        th, dinner = _gelu_parts(y)
        dy = da_ref[...].astype(F32) * (0.5 * (1.0 + th) + 0.5 * y * (1.0 - th * th) * dinner)
        dy_ref[...] = dy
        du_ref[...] = dy * d_ref[0]

        @pl.when(i == 0)
        def _():
            dd_ref[...] = jnp.zeros_like(dd_ref)

        dd_ref[0] += jnp.sum(dy * u, axis=0, keepdims=True)

    dy, du, dd = _rows(name, body, t_rows // tile,
                       [_row_in(y0, tile), _row_in(y1, tile), (hm, (tile, s), lambda i: (i + nct, 0)),
                        _vec_in(dskip, lambda i: 0), _row_in(da, tile)],
                       [((t_rows, s), F32, (tile, s), lambda i: (i, 0)), ((t_rows, s), F32, (tile, s), lambda i: (i, 0)),
                        ((1, 1, s), F32, (1, 1, s), lambda i: (0, 0, 0))])
    return dy, du, dd


def _col_pieces(arr, off, width, tile, nct, unit=None):
    pw = math.gcd(off, width if unit is None else unit)
    specs = [(arr, (tile, pw), functools.partial(lambda i, cb: (i + nct, cb), cb=off // pw + p))
             for p in range(width // pw)]
    return specs, pw


def _ret_gate_fwd(o0, o1, hm, g_off, heads, dv, nct, tile, name):
    t_rows, w = o0.shape
    g_specs, pw = _col_pieces(hm, g_off, w, tile, nct)
    ng = len(g_specs)

    def body(o0_ref, o1_ref, *refs):
        g_refs, r_ref = refs[:ng], refs[ng]
        for hd in range(heads):
            cs = slice(hd * dv, (hd + 1) * dv)
            o = o0_ref[:, cs] + o1_ref[:, cs]
            lo = (hd * dv) % pw
            g = g_refs[(hd * dv) // pw][:, lo:lo + dv].astype(F32)
            r_ref[:, cs] = (g * _sigmoid(g) * (o * _rms(o))).astype(r_ref.dtype)

    (ri,) = _rows(name, body, t_rows // tile, [_row_in(o0, tile), _row_in(o1, tile)] + g_specs,
                  [((t_rows, w), BF16, (tile, w), lambda i: (i, 0))])
    return ri


def _ret_gate_bwd(o0, o1, hm, g_off, dri, heads, dv, nct, tile, name):
    t_rows, w = o0.shape
    g_specs, pw = _col_pieces(hm, g_off, w, tile, nct)
    ng = len(g_specs)

    def body(o0_ref, o1_ref, d_ref, *refs):
        g_refs, do_ref, dg_ref = refs[:ng], refs[ng], refs[ng + 1]
        for hd in range(heads):
            cs = slice(hd * dv, (hd + 1) * dv)
            o = o0_ref[:, cs] + o1_ref[:, cs]
            lo = (hd * dv) % pw
            g = g_refs[(hd * dv) // pw][:, lo:lo + dv].astype(F32)
            d = d_ref[:, cs].astype(F32)
            rr = _rms(o)
            on = o * rr
            sg = _sigmoid(g)
            dg_ref[:, cs] = (d * on * (sg * (1.0 + g * (1.0 - sg)))).astype(dg_ref.dtype)
            t = d * (g * sg)
            do_ref[:, cs] = rr * t - on * (rr * jnp.mean(t * on, axis=-1, keepdims=True))

    do, dg = _rows(name, body, t_rows // tile, [_row_in(o0, tile), _row_in(o1, tile), _row_in(dri, tile)] + g_specs,
                   [((t_rows, w), F32, (tile, w), lambda i: (i, 0)), ((t_rows, w), BF16, (tile, w), lambda i: (i, 0))])
    return do, dg


def _merge_fwd(gab, rb, hm, gs_off, nct, tile, name):
    t_rows, d = rb.shape
    specs, pw = _col_pieces(hm, gs_off, 2 * d, tile, nct, unit=d)
    npc = d // pw

    def body(gab_ref, rb_ref, *refs):
        gs_refs, gr_refs, m_ref = refs[:npc], refs[npc:2 * npc], refs[2 * npc]
        for p in range(npc):
            cs = slice(p * pw, (p + 1) * pw)
            ga = gab_ref[:, cs].astype(F32)
            gb = gab_ref[:, d + p * pw:d + (p + 1) * pw].astype(F32)
            m_ref[:, cs] = (_sigmoid(gs_refs[p][...].astype(F32)) * (ga * _sigmoid(gb))
                            + _sigmoid(gr_refs[p][...].astype(F32)) * rb_ref[:, cs].astype(F32)).astype(m_ref.dtype)

    (mg,) = _rows(name, body, t_rows // tile, [_row_in(gab, tile), _row_in(rb, tile)] + specs,
                  [((t_rows, d), BF16, (tile, d), lambda i: (i, 0))])
    return mg


def _merge_bwd(gab, rb, hm, gs_off, dm, nct, tile, name):
    t_rows, d = rb.shape
    specs, pw = _col_pieces(hm, gs_off, 2 * d, tile, nct, unit=d)
    npc = d // pw

    def body(gab_ref, rb_ref, dm_ref, *refs):
        gs_refs, gr_refs = refs[:npc], refs[npc:2 * npc]
        dgab_ref, drb_ref, dgs_ref, dgr_ref = refs[2 * npc:]
        for p in range(npc):
            cs = slice(p * pw, (p + 1) * pw)
            cs2 = slice(d + p * pw, d + (p + 1) * pw)
            ga = gab_ref[:, cs].astype(F32)
            gb = gab_ref[:, cs2].astype(F32)
            dmm = dm_ref[:, cs].astype(F32)
            ss = _sigmoid(gs_refs[p][...].astype(F32))
            sr = _sigmoid(gr_refs[p][...].astype(F32))
            sb = _sigmoid(gb)
            dbr = dmm * ss
            dgab_ref[:, cs] = (dbr * sb).astype(dgab_ref.dtype)
            dgab_ref[:, cs2] = (dbr * ga * sb * (1.0 - sb)).astype(dgab_ref.dtype)
            drb_ref[:, cs] = (dmm * sr).astype(drb_ref.dtype)
            dgs_ref[:, cs] = (dmm * (ga * sb) * ss * (1.0 - ss)).astype(dgs_ref.dtype)
            dgr_ref[:, cs] = (dmm * rb_ref[:, cs].astype(F32) * sr * (1.0 - sr)).astype(dgr_ref.dtype)

    return _rows(name, body, t_rows // tile, [_row_in(gab, tile), _row_in(rb, tile), _row_in(dm, tile)] + specs,
                 [((t_rows, 2 * d), BF16, (tile, 2 * d), lambda i: (i, 0)), ((t_rows, d), BF16, (tile, d), lambda i: (i, 0)),
                  ((t_rows, d), BF16, (tile, d), lambda i: (i, 0)), ((t_rows, d), BF16, (tile, d), lambda i: (i, 0))])


def _assemble_dhm(dus, dq0, dq1, dk0, dk1, dv0, dv1, dg, dgs, dgr, nct, tile, name):
    r, s = dus.shape
    qk = dq0.shape[1]
    vw = dv0.shape[1]
    d = dgs.shape[1]
    mi = s + 2 * qk + 2 * vw + 2 * d
    c_q, c_k, c_v, c_g, c_gs, c_gr = s, s + qk, s + 2 * qk, s + 2 * qk + vw, s + 2 * qk + 2 * vw, s + 2 * qk + 2 * vw + d

    def body(dus_ref, dq0_ref, dq1_ref, dk0_ref, dk1_ref, dv0_ref, dv1_ref, dg_ref, dgs_ref, dgr_ref, o_ref):
        i = pl.program_id(0)
        lat = i >= nct
        o_ref[:, :s] = dus_ref[...].astype(o_ref.dtype)
        o_ref[:, c_q:c_k] = (dq0_ref[...].astype(F32) + dq1_ref[...].astype(F32)).astype(o_ref.dtype)
        o_ref[:, c_k:c_v] = (dk0_ref[...].astype(F32) + dk1_ref[...].astype(F32)).astype(o_ref.dtype)
        o_ref[:, c_v:c_g] = (dv0_ref[...].astype(F32) + dv1_ref[...].astype(F32)).astype(o_ref.dtype)
        o_ref[:, c_g:c_gs] = jnp.where(lat, dg_ref[...], 0.0).astype(o_ref.dtype)
        o_ref[:, c_gs:c_gr] = jnp.where(lat, dgs_ref[...], 0.0).astype(o_ref.dtype)
        o_ref[:, c_gr:] = jnp.where(lat, dgr_ref[...], 0.0).astype(o_ref.dtype)

    (out,) = _rows(name, body, r // tile,
                   [_row_in(dus, tile), _row_in(dq0, tile), _row_in(dq1, tile), _row_in(dk0, tile), _row_in(dk1, tile),
                    _row_in(dv0, tile), _row_in(dv1, tile), _row_in(dg, tile, x_only_offset=nct),
                    _row_in(dgs, tile, x_only_offset=nct), _row_in(dgr, tile, x_only_offset=nct)],
                   [((r, mi), BF16, (tile, mi), lambda i: (i, 0))])
    return out


def _silu_rows(v, name):
    def body(v_ref, o_ref):
        z = v_ref[...]
        o_ref[...] = z * _sigmoid(z)

    (o,) = _rows(name, body, 1, [_row_in(v, v.shape[0])], [(v.shape, F32, v.shape, lambda i: (0, 0))])
    return o


def _silu_grad_rows(v, dv, name):
    def body(v_ref, d_ref, o_ref):
        z = v_ref[...]
        sg = _sigmoid(z)
        o_ref[...] = d_ref[...] * (sg * (1.0 + z * (1.0 - sg)))

    (o,) = _rows(name, body, 1, [_row_in(v, v.shape[0]), _row_in(dv, v.shape[0])],
                 [(v.shape, F32, v.shape, lambda i: (0, 0))])
    return o


def _sum_leading(g8, name):
    n, r, c = g8.shape
    tile = _tile(r, 256, SUBLANE)

    def body(g_ref, o_ref):
        acc = g_ref[0]
        for j in range(1, n):
            acc = acc + g_ref[j]
        o_ref[...] = acc

    (o,) = _rows(name, body, r // tile, [(g8, (n, tile, c), lambda i: (0, i, 0))],
                 [((r, c), F32, (tile, c), lambda i: (i, 0))])
    return o


def _pair_sum(g, recv, axis, name):
    n, br, bc = recv.shape
    tile = _tile(br, 256, 16)
    nrt = br // tile
    core = lax.axis_index("c").astype(jnp.int32).reshape(1)

    def body(c_ref, g_ref, r_ref, o_ref):
        o_ref[0] = (g_ref[...].astype(F32) + r_ref[0].astype(F32)).astype(o_ref.dtype)

    if axis == 1:
        g_spec = pl.BlockSpec((tile, bc), lambda q, i, c_ref: (i, 2 * q + c_ref[0]))
    else:
        g_spec = pl.BlockSpec((tile, bc), lambda q, i, c_ref: ((2 * q + c_ref[0]) * nrt + i, 0))
    slot = pl.BlockSpec((1, tile, bc), lambda q, i, c_ref: (q, i, 0))
    return pl.pallas_call(
        body, name=name, out_shape=_sds((n, br, bc), recv.dtype),
        grid_spec=pltpu.PrefetchScalarGridSpec(num_scalar_prefetch=1, grid=(n, nrt), in_specs=[g_spec, slot],
                                               out_specs=slot),
        compiler_params=_params(("arbitrary", "arbitrary")))(core, g, recv)


def _adam_math(w, m, v, g):
    c1 = 1.0 / (1.0 - ADAM_B1 ** ADAM_STEP)
    c2 = 1.0 / (1.0 - ADAM_B2 ** ADAM_STEP)
    mm = ADAM_B1 * m + (1.0 - ADAM_B1) * g
    vv = ADAM_B2 * v + (1.0 - ADAM_B2) * (g * g)
    return -ADAM_LR * ((mm * c1) / (jnp.sqrt(vv * c2) + ADAM_EPS) + ADAM_WD * w), mm, vv


def _adamw(w, m, v, gparts, name):
    r, c = w.shape
    n = gparts.shape[0]
    tile = _tile(r, 256, 16)

    def body(w_ref, m_ref, v_ref, g_ref, go_ref, d_ref, mo_ref, vo_ref):
        g = g_ref[0].astype(F32)
        for j in range(1, n):
            g = g + g_ref[j].astype(F32)
        go_ref[...] = g
        d_ref[...], mo_ref[...], vo_ref[...] = _adam_math(w_ref[...], m_ref[...], v_ref[...], g)

    rs = lambda arr: _row_in(arr, tile)
    out = ((r, c), F32, (tile, c), lambda i: (i, 0))
    return _rows(name, body, r // tile, [rs(w), rs(m), rs(v), (gparts, (n, tile, c), lambda i: (0, i, 0))],
                 [out, out, out, out])


def _adamw_scattered(w, m, v, layer, p, recv, name, filled=None):
    nl, r, c = w.shape
    n = recv.shape[0]
    tile = _tile(r, 256, 16)
    chip = (2 * lax.axis_index("x") + lax.axis_index("y")).astype(jnp.int32).reshape(1)
    n_prev = 0 if filled is None else len(filled)

    def body(q_ref, w_ref, m_ref, v_ref, p_ref, g_ref, *rest):
        go_ref, d_ref, mo_ref, vo_ref = rest[n_prev:]
        g = p_ref[0].astype(F32)
        for j in range(n):
            g = g + g_ref[j].astype(F32)
        go_ref[0] = g
        d_ref[0], mo_ref[0], vo_ref[0] = _adam_math(w_ref[0], m_ref[0], v_ref[0], g)

    slab = pl.BlockSpec((1, tile, c), lambda i, q_ref: (layer, i, 0))
    anywhere = pl.BlockSpec(memory_space=pl.ANY)
    out = _sds((nl, r, c), F32)
    prev = [] if filled is None else list(filled)
    return pl.pallas_call(
        body, name=name, out_shape=[out, out, out, out],
        grid_spec=pltpu.PrefetchScalarGridSpec(
            num_scalar_prefetch=1, grid=(r // tile,),
            in_specs=[slab, slab, slab, pl.BlockSpec((1, tile, c), lambda i, q_ref: (q_ref[0], i, 0)),
                      pl.BlockSpec((n, tile, c), lambda i, q_ref: (0, i, 0))] + [anywhere] * n_prev,
            out_specs=[slab, slab, slab, slab]),
        input_output_aliases={6 + j: j for j in range(n_prev)},
        compiler_params=_params(("arbitrary",)))(chip, w, m, v, p, recv, *prev)


def _cmul(ar, ai, br, bi):
    return ar * br - ai * bi, ar * bi + ai * br


def _cpow(ar, ai, n):
    pr, pi = jnp.ones_like(ar), jnp.zeros_like(ar)
    br, bi = ar, ai
    while n:
        if n & 1:
            pr, pi = _cmul(pr, pi, br, bi)
        n >>= 1
        if n:
            br, bi = _cmul(br, bi, br, bi)
    return pr, pi


def _s5_scan_into(x_ref, ar1, ai1, ns, fin_ref, hin_ref, reverse, paired=None):
    st = ar1.shape[1]
    ar = jnp.broadcast_to(ar1, (N_SEG, st))
    ai = jnp.broadcast_to(ai1, (N_SEG, st))
    zero = jnp.zeros((N_SEG, st), F32)

    def slab(k):
        if isinstance(k, int):
            return pl.ds(k * N_SEG, N_SEG)
        return pl.ds(pl.multiple_of(k * N_SEG, N_SEG), N_SEG)

    def pass1(j, carry):
        hr, hi = carry
        k = ns - 1 - j if reverse else j
        nr, ni = _cmul(ar, ai, hr, hi)
        return nr + x_ref[slab(k), :st], ni + x_ref[slab(k), st:]

    fr, fi = lax.fori_loop(0, ns, pass1, (zero, zero))
    fin_ref[:, :st] = fr
    fin_ref[:, st:] = fi
    pr, pi = _cpow(ar1, ai1, ns)
    order = list(range(N_SEG - 1, -1, -1)) if reverse else list(range(N_SEG))
    hin_ref[order[0]:order[0] + 1, :] = jnp.zeros((1, 2 * st), F32)
    for a_, b_ in zip(order[:-1], order[1:]):
        cr, ci = _cmul(pr, pi, hin_ref[a_:a_ + 1, :st], hin_ref[a_:a_ + 1, st:])
        hin_ref[b_:b_ + 1, :st] = cr + fin_ref[a_:a_ + 1, :st]
        hin_ref[b_:b_ + 1, st:] = ci + fin_ref[a_:a_ + 1, st:]

    def step2(k, hr, hi):
        nr, ni = _cmul(ar, ai, hr, hi)
        nr = nr + x_ref[slab(k), :st]
        ni = ni + x_ref[slab(k), st:]
        x_ref[slab(k), :st] = nr
        x_ref[slab(k), st:] = ni
        return nr, ni

    if paired is None:
        def pass2(j, carry):
            return step2(ns - 1 - j if reverse else j, *carry)

        lax.fori_loop(0, ns, pass2, (hin_ref[:, :st], hin_ref[:, st:]))
        return None
    p_ref, p_edge_ref, shift = paired

    def pass2_paired(j, carry):
        hr, hi, acr, aci = carry
        k = ns - 1 - j if reverse else j
        nr, ni = step2(k, hr, hi)
        p_r, p_i = p_ref[slab(k + shift), :st], p_ref[slab(k + shift), st:]
        return nr, ni, acr + nr * p_r + ni * p_i, aci + ni * p_r - nr * p_i

    hr, hi, acr, aci = lax.fori_loop(0, ns - 1, pass2_paired, (hin_ref[:, :st], hin_ref[:, st:], zero, zero))
    nr, ni = step2(0 if reverse else ns - 1, hr, hi)
    p_r, p_i = p_edge_ref[:, :st], p_edge_ref[:, st:]
    return acr + nr * p_r + ni * p_i, aci + ni * p_r - nr * p_i


def _s5_specs(r, ch, st):
    u_spec = pl.BlockSpec((r, ch), lambda j: (0, j // 2))
    w_spec = pl.BlockSpec((1, ch, 2 * st), lambda j: (j, 0, 0))
    c_spec = pl.BlockSpec((1, 2 * st, ch), lambda j: (j, 0, 0))
    a_spec = pl.BlockSpec((1, 2, st), lambda j: (j, 0, 0))
    return u_spec, w_spec, c_spec, a_spec


def _s5_fwd(up, w, c, a, rev, name):
    r, s = up.shape
    nh, ch, st2 = w.shape
    st = st2 // 2
    ns = r // N_SEG
    nb = r // N_DEV
    u_spec, w_spec, c_spec, a_spec = _s5_specs(r, ch, st)

    def body(u_ref, w_ref, c_ref, a_ref, y_ref, x, fin, hin):
        j = pl.program_id(0)
        w_b = w_ref[0].astype(MXU_DTYPE)
        c_b = c_ref[0].astype(MXU_DTYPE)
        for rb in range(N_DEV):
            rows = slice(rb * nb, (rb + 1) * nb)
            x[rows, :] = jnp.dot(u_ref[rows, :].astype(MXU_DTYPE), w_b, preferred_element_type=F32)
        _s5_scan_into(x, a_ref[0, 0:1, :], a_ref[0, 1:2, :], ns, fin, hin, rev)
        for rb in range(N_DEV):
            rows = slice(rb * nb, (rb + 1) * nb)
            yb = jnp.dot(x[rows, :].astype(MXU_DTYPE), c_b, preferred_element_type=F32)

            @pl.when(j % 2 == 0)
            def _():
                y_ref[rows, :] = yb

            @pl.when(j % 2 == 1)
            def _():
                y_ref[rows, :] += yb

    small = pltpu.VMEM((N_SEG, st2), F32)
    return pl.pallas_call(
        body, name=name, grid=(nh,), in_specs=[u_spec, w_spec, c_spec, a_spec],
        out_specs=pl.BlockSpec((r, ch), lambda j: (0, j // 2)), out_shape=_sds((r, s), F32),
        scratch_shapes=[pltpu.VMEM((r, st2), F32), small, small],
        compiler_params=_params(("arbitrary",)))(up, w, c, a)


def _s5_bwd(up, dyp, w, c, a, rev, name):
    r, s = up.shape
    nh, ch, st2 = w.shape
    st = st2 // 2
    ns = r // N_SEG
    nb = r // N_DEV
    u_spec, w_spec, c_spec, a_spec = _s5_specs(r, ch, st)
    nt = (((1,), (1,)), ((), ()))
    tn = (((0,), (0,)), ((), ()))

    def body(u_ref, dy_ref, w_ref, c_ref, a_ref, du_ref, dw_ref, dc_ref, da_ref, h, g, fin, sin_, ein):
        j = pl.program_id(0)
        w_b = w_ref[0].astype(MXU_DTYPE)
        c_b = c_ref[0].astype(MXU_DTYPE)
        for rb in range(N_DEV):
            rows = slice(rb * nb, (rb + 1) * nb)
            h[rows, :] = jnp.dot(u_ref[rows, :].astype(MXU_DTYPE), w_b, preferred_element_type=F32)
        ar1, ai1 = a_ref[0, 0:1, :], a_ref[0, 1:2, :]
        _s5_scan_into(h, ar1, ai1, ns, fin, sin_, rev)
        dc = jnp.zeros((st2, ch), F32)
        for rb in range(N_DEV):
            rows = slice(rb * nb, (rb + 1) * nb)
            dyb = dy_ref[rows, :].astype(MXU_DTYPE)
            g[rows, :] = lax.dot_general(dyb, c_b, nt, preferred_element_type=F32)
            dc += lax.dot_general(h[rows, :].astype(MXU_DTYPE), dyb, tn, preferred_element_type=F32)
        dc_ref[0] = dc
        acr, aci = _s5_scan_into(g, ar1, -ai1, ns, fin, ein, not rev, paired=(h, sin_, 1 if rev else -1))
        da_ref[0, 0:1, :] = jnp.sum(acr, axis=0, keepdims=True)
        da_ref[0, 1:2, :] = jnp.sum(aci, axis=0, keepdims=True)
        dw = jnp.zeros((ch, st2), F32)
        for rb in range(N_DEV):
            rows = slice(rb * nb, (rb + 1) * nb)
            gb = g[rows, :].astype(MXU_DTYPE)
            dub = lax.dot_general(gb, w_b, nt, preferred_element_type=F32)
            dw += lax.dot_general(u_ref[rows, :].astype(MXU_DTYPE), gb, tn, preferred_element_type=F32)

            @pl.when(j % 2 == 0)
            def _():
                du_ref[rows, :] = dub

            @pl.when(j % 2 == 1)
            def _():
                du_ref[rows, :] += dub

        dw_ref[0] = dw

    small = pltpu.VMEM((N_SEG, st2), F32)
    big = pltpu.VMEM((r, st2), F32)
    return pl.pallas_call(
        body, name=name, grid=(nh,), in_specs=[u_spec, u_spec, w_spec, c_spec, a_spec],
        out_specs=[pl.BlockSpec((r, ch), lambda j: (0, j // 2)), w_spec, c_spec, a_spec],
        out_shape=[_sds((r, s), F32), _sds(w.shape, F32), _sds(c.shape, F32), _sds(a.shape, F32)],
        scratch_shapes=[big, big, small, small, small],
        compiler_params=_params(("arbitrary",)))(up, dyp, w, c, a)


def _rope(t, cos, sin):
    quarter = t.shape[1] // 4
    lane = lax.broadcasted_iota(jnp.int32, t.shape, 1)
    first = (lane // quarter) % 2 == 0
    partner = jnp.where(first, pltpu.roll(t, t.shape[1] - quarter, 1), pltpu.roll(t, quarter, 1))
    return t * cos + partner * sin


def _rope_t(d, cos, sin):
    quarter = d.shape[1] // 4
    ds_ = d * sin
    lane = lax.broadcasted_iota(jnp.int32, d.shape, 1)
    first = (lane // quarter) % 2 == 0
    partner = jnp.where(first, pltpu.roll(ds_, d.shape[1] - quarter, 1), pltpu.roll(ds_, quarter, 1))
    return d * cos + partner


def _chunk_of_step(s, nch, ncc, rev):
    if not rev:
        return s
    return jnp.where(s < ncc, ncc - 1 - s, nch + ncc - 1 - s)


def _heads_per_step(heads, dk, dv, q_off):
    v_off = q_off + 2 * heads * dk
    for hpg in range(heads, 0, -1):
        if heads % hpg == 0 and q_off % (hpg * dk) == 0:
            piece = math.gcd(v_off, hpg * dv)
            if piece % dv == 0:
                return hpg, piece
    return 1, dv


def _v_specs(hpg, dv, piece, v_off, ch, chunk_of):
    n_pieces = hpg * dv // piece
    return [pl.BlockSpec((ch, piece), functools.partial(
        lambda h, s, p: (chunk_of(s), v_off // piece + h * n_pieces + p), p=p)) for p in range(n_pieces)]


def _v_of_head(v_refs, hl, dv, piece):
    lo = (hl * dv) % piece
    return v_refs[(hl * dv) // piece][:, lo:lo + dv]


def _ret_fwd(hm, cos, sin, decay, wend, win, gch, heads, dk, dv, q_off, ncc, rev, name):
    r = hm.shape[0]
    ch = RET_CHUNK
    nch = r // ch
    t_rows = r - ncc * ch
    hpg, piece = _heads_per_step(heads, dk, dv, q_off)
    qb, kb = q_off // (hpg * dk), (q_off + heads * dk) // (hpg * dk)
    q_scale = dk ** -0.5
    nt = (((1,), (1,)), ((), ()))
    tn = (((0,), (0,)), ((), ()))
    cof = lambda s: _chunk_of_step(s, nch, ncc, rev)
    v_specs = _v_specs(hpg, dv, piece, q_off + 2 * heads * dk, ch, cof)
    nv = len(v_specs)

    def body(q_ref, k_ref, *refs):
        v_refs = refs[:nv]
        cos_ref, sin_ref, dec_ref, we_ref, wi_ref, g_ref, o_ref, sin_out, st = refs[nv:]
        s = pl.program_id(1)

        @pl.when(s == 0)
        def _():
            st[...] = jnp.zeros_like(st)

        cos_, sin_ = cos_ref[...], sin_ref[...]
        for hl in range(hpg):
            ks, vs = slice(hl * dk, (hl + 1) * dk), slice(hl * dv, (hl + 1) * dv)
            q = _rope(q_ref[:, ks].astype(F32), cos_, sin_) * q_scale
            k = _rope(k_ref[:, ks].astype(F32), cos_, sin_)
            v = _v_of_head(v_refs, hl, dv, piece).astype(MXU_DTYPE)
            s_cur = st[hl]
            sin_out[hl, 0] = s_cur
            kw = (k * we_ref[hl]).astype(MXU_DTYPE)
            qw = (q * wi_ref[hl]).astype(MXU_DTYPE)
            scores = lax.dot_general(q.astype(MXU_DTYPE), k.astype(MXU_DTYPE), nt,
                                     preferred_element_type=F32) * dec_ref[hl]
            o_ref[:, vs] = (jnp.dot(scores.astype(MXU_DTYPE), v, preferred_element_type=F32)
                            + jnp.dot(qw, s_cur.astype(MXU_DTYPE), preferred_element_type=F32))
            st[hl] = g_ref[hl] * s_cur + lax.dot_general(kw, v, tn, preferred_element_type=F32)

    tab = lambda w: pl.BlockSpec((hpg, ch, w), lambda h, s: (h, 0, 0))
    return pl.pallas_call(
        body, name=name, grid=(heads // hpg, nch),
        in_specs=[pl.BlockSpec((ch, hpg * dk), lambda h, s: (cof(s), qb + h)),
                  pl.BlockSpec((ch, hpg * dk), lambda h, s: (cof(s), kb + h))] + v_specs +
                 [pl.BlockSpec((ch, dk), lambda h, s: (cof(s), 0)),
                  pl.BlockSpec((ch, dk), lambda h, s: (cof(s), 0)),
                  tab(ch), tab(dk), tab(dk), tab(dv)],
        out_specs=[pl.BlockSpec((ch, hpg * dv), lambda h, s: (jnp.maximum(cof(s) - ncc, 0) if not rev
                                                               else jnp.where(s < ncc, nch - ncc - 1, cof(s) - ncc), h)),
                   pl.BlockSpec((hpg, 1, dk, dv), lambda h, s: (h, s, 0, 0))],
        out_shape=[_sds((t_rows, heads * dv), F32), _sds((heads, nch, dk, dv), F32)],
        scratch_shapes=[pltpu.VMEM((hpg, dk, dv), F32)],
        compiler_params=_params(("parallel", "arbitrary")))(hm, hm, *([hm] * nv), cos, sin, decay, wend, win, gch)


def _ret_bwd(hm, cos, sin, decay, wend, win, gch, s_in, do, heads, dk, dv, q_off, ncc, rev, name):
    r = hm.shape[0]
    ch = RET_CHUNK
    nch = r // ch
    hpg, piece = _heads_per_step(heads, dk, dv, q_off)
    qb, kb = q_off // (hpg * dk), (q_off + heads * dk) // (hpg * dk)
    q_scale = dk ** -0.5
    nt = (((1,), (1,)), ((), ()))
    tn = (((0,), (0,)), ((), ()))
    cof = lambda rr: _chunk_of_step(nch - 1 - rr, nch, ncc, rev)
    v_specs = _v_specs(hpg, dv, piece, q_off + 2 * heads * dk, ch, cof)
    nv = len(v_specs)

    def body(q_ref, k_ref, *refs):
        v_refs = refs[:nv]
        (cos_ref, sin_ref, dec_ref, we_ref, wi_ref, g_ref, sin_ref2, do_ref,
         dq_ref, dk_ref, dv_ref, ddec_ref, dwe_ref, dwi_ref, dg_ref, dst) = refs[nv:]
        rr = pl.program_id(1)
        n = cof(rr)

        @pl.when(rr == 0)
        def _():
            dst[...] = jnp.zeros_like(dst)
            ddec_ref[...] = jnp.zeros_like(ddec_ref)
            dwe_ref[...] = jnp.zeros_like(dwe_ref)
            dwi_ref[...] = jnp.zeros_like(dwi_ref)
            dg_ref[...] = jnp.zeros_like(dg_ref)

        cos_, sin_ = cos_ref[...], sin_ref[...]
        for hl in range(hpg):
            ks, vs = slice(hl * dk, (hl + 1) * dk), slice(hl * dv, (hl + 1) * dv)
            q = _rope(q_ref[:, ks].astype(F32), cos_, sin_) * q_scale
            k = _rope(k_ref[:, ks].astype(F32), cos_, sin_)
            v = _v_of_head(v_refs, hl, dv, piece).astype(MXU_DTYPE)
            qb_, kb_ = q.astype(MXU_DTYPE), k.astype(MXU_DTYPE)
            kw = (k * we_ref[hl]).astype(MXU_DTYPE)
            qw = (q * wi_ref[hl]).astype(MXU_DTYPE)
            sraw = lax.dot_general(qb_, kb_, nt, preferred_element_type=F32)
            scores = (sraw * dec_ref[hl]).astype(MXU_DTYPE)
            d_o = jnp.where(n >= ncc, do_ref[:, vs], 0.0).astype(MXU_DTYPE)
            s_n = sin_ref2[hl, 0]
            s_nb = s_n.astype(MXU_DTYPE)
            ds1 = dst[hl]
            ds1b = ds1.astype(MXU_DTYPE)
            dsc = lax.dot_general(d_o, v, nt, preferred_element_type=F32)
            dsr = (dsc * dec_ref[hl]).astype(MXU_DTYPE)
            ddec_ref[hl] += dsc * sraw
            t1 = lax.dot_general(d_o, s_nb, nt, preferred_element_type=F32)
            dq_r = jnp.dot(dsr, kb_, preferred_element_type=F32) + t1 * wi_ref[hl]
            dwi_ref[hl] += t1 * q
            t2 = lax.dot_general(v, ds1b, nt, preferred_element_type=F32)
            dk_r = lax.dot_general(dsr, qb_, tn, preferred_element_type=F32) + t2 * we_ref[hl]
            dwe_ref[hl] += t2 * k
            dv_ref[:, vs] = (lax.dot_general(scores, d_o, tn, preferred_element_type=F32)
                             + jnp.dot(kw, ds1b, preferred_element_type=F32)).astype(dv_ref.dtype)
            dg_ref[hl] += ds1 * s_n
            dst[hl] = g_ref[hl] * ds1 + lax.dot_general(qw, d_o, tn, preferred_element_type=F32)
            dq_ref[:, ks] = (_rope_t(dq_r, cos_, sin_) * q_scale).astype(dq_ref.dtype)
            dk_ref[:, ks] = _rope_t(dk_r, cos_, sin_).astype(dk_ref.dtype)

    tab = lambda w: pl.BlockSpec((hpg, ch, w), lambda h, rr: (h, 0, 0))
    return pl.pallas_call(
        body, name=name, grid=(heads // hpg, nch),
        in_specs=[pl.BlockSpec((ch, hpg * dk), lambda h, rr: (cof(rr), qb + h)),
                  pl.BlockSpec((ch, hpg * dk), lambda h, rr: (cof(rr), kb + h))] + v_specs +
                 [pl.BlockSpec((ch, dk), lambda h, rr: (cof(rr), 0)),
                  pl.BlockSpec((ch, dk), lambda h, rr: (cof(rr), 0)),
                  tab(ch), tab(dk), tab(dk), tab(dv),
                  pl.BlockSpec((hpg, 1, dk, dv), lambda h, rr: (h, nch - 1 - rr, 0, 0)),
                  pl.BlockSpec((ch, hpg * dv), lambda h, rr: (jnp.maximum(cof(rr) - ncc, 0), h))],
        out_specs=[pl.BlockSpec((ch, hpg * dk), lambda h, rr: (cof(rr), h)),
                   pl.BlockSpec((ch, hpg * dk), lambda h, rr: (cof(rr), h)),
                   pl.BlockSpec((ch, hpg * dv), lambda h, rr: (cof(rr), h)),
                   tab(ch), tab(dk), tab(dk), tab(dv)],
        out_shape=[_sds((r, heads * dk), BF16), _sds((r, heads * dk), BF16), _sds((r, heads * dv), BF16),
                   _sds(decay.shape, F32), _sds(wend.shape, F32), _sds(win.shape, F32), _sds(gch.shape, F32)],
        scratch_shapes=[pltpu.VMEM((hpg, dk, dv), F32)],
        compiler_params=_params(("parallel", "arbitrary")))(hm, hm, *([hm] * nv), cos, sin, decay, wend, win, gch, s_in, do)


_HBM = pl.BlockSpec(memory_space=pltpu.HBM)
_MESH = pl.DeviceIdType.MESH
ALL_GATHER_COLLECTIVE_ID = 1
SIBLING_COLLECTIVE_ID = 2
CHIPS_COLLECTIVE_ID = 3


def _axis_slice(ref, axis, start, size):
    idx = [slice(None)] * len(ref.shape)
    idx[axis] = pl.ds(start, size)
    return ref.at[tuple(idx)]


def _sibling_and_chip_peers():
    x, y, c = lax.axis_index("x"), lax.axis_index("y"), lax.axis_index("c")
    return [(x, y, 1 - c), (1 - x, y, c), (x, 1 - y, c), (1 - x, 1 - y, c)]


def _launch_exchange(body, name, operand, out_shape, sems, peers_fn, collective_id, on_sequencer):
    if not on_sequencer:
        return pl.pallas_call(body, name=name, out_shape=out_shape, in_specs=[_HBM], out_specs=_HBM,
                              scratch_shapes=sems)(operand)

    def sequencer_body(in_ref, out_ref, *sem_refs):
        peers = peers_fn()
        barrier = pltpu.get_barrier_semaphore()
        for peer in peers:
            pl.semaphore_signal(barrier, inc=1, device_id=peer, device_id_type=_MESH)
        pl.semaphore_wait(barrier, len(peers))
        body(in_ref, out_ref, *sem_refs)

    return pl.kernel(sequencer_body, out_type=out_shape, name=name,
                     mesh=plsc.ScalarSubcoreMesh(axis_name="sequencer", num_cores=1), scratch_types=sems,
                     compiler_params=pltpu.CompilerParams(collective_id=collective_id))(operand)


def _all_gather(shard, axis, name, on_sequencer=False):
    m = shard.shape[axis]
    out_shape = list(shard.shape)
    out_shape[axis] = N_DEV * m

    def body(x_ref, out_ref, send_sems, recv_sems, local_sem):
        x, y, c = lax.axis_index("x"), lax.axis_index("y"), lax.axis_index("c")
        me, sibling = (x, y, c), (x, y, 1 - c)
        chips = [(1 - x, y), (x, 1 - y), (1 - x, 1 - y)]

        def block(px, py, pc):
            return _axis_slice(out_ref, axis, (4 * px + 2 * py + pc) * m, m)

        def copy(k, blk, to, src=None):
            return pltpu.make_async_remote_copy(
                src_ref=block(*blk) if src is None else src, dst_ref=block(*blk), send_sem=send_sems.at[k],
                recv_sem=recv_sems.at[k], device_id=to, device_id_type=_MESH)

        mine = pltpu.make_async_copy(x_ref, block(*me), local_sem)
        mine.start()
        first = [copy(0, me, sibling, src=x_ref)]
        first += [copy(1 + j, me, (*chip, c), src=x_ref) for j, chip in enumerate(chips)]
        for cp in first:
            cp.start()
        passed = [copy(4 + j, (*chip, c), sibling) for j, chip in enumerate(chips)]
        for j, chip in enumerate(chips):
            copy(1 + j, (*chip, c), me).wait_recv()
            passed[j].start()
        copy(0, sibling, me).wait_recv()
        for j, chip in enumerate(chips):
            copy(4 + j, (*chip, 1 - c), me).wait_recv()
        for cp in first + passed:
            cp.wait_send()
        mine.wait()

    return _launch_exchange(
        body, name, shard, _sds(out_shape, shard.dtype),
        [pltpu.SemaphoreType.DMA((7,)), pltpu.SemaphoreType.DMA((7,)), pltpu.SemaphoreType.DMA(())],
        _sibling_and_chip_peers, ALL_GATHER_COLLECTIVE_ID, on_sequencer)


def _rs_sibling(g, axis, name, on_sequencer=False):
    m = g.shape[axis] // N_DEV
    blk_shape = list(g.shape)
    blk_shape[axis] = m
    n_chips = N_DEV // 2

    def body(g_ref, recv_ref, send_sems, recv_sems):
        x, y, c = lax.axis_index("x"), lax.axis_index("y"), lax.axis_index("c")
        sibling = (x, y, 1 - c)
        send = [pltpu.make_async_remote_copy(
            src_ref=_axis_slice(g_ref, axis, (2 * q + 1 - c) * m, m), dst_ref=recv_ref.at[q],
            send_sem=send_sems.at[q], recv_sem=recv_sems.at[q], device_id=sibling, device_id_type=_MESH)
            for q in range(n_chips)]
        for cp in send:
            cp.start()
        for cp in send:
            cp.wait_recv()
        for cp in send:
            cp.wait_send()

    return _launch_exchange(
        body, name, g, _sds([n_chips] + blk_shape, g.dtype),
        [pltpu.SemaphoreType.DMA((n_chips,)), pltpu.SemaphoreType.DMA((n_chips,))],
        lambda: _sibling_and_chip_peers()[:1], SIBLING_COLLECTIVE_ID, on_sequencer)


def _rs_chips(p, name, on_sequencer=False):
    n_peers = p.shape[0] - 1

    def body(p_ref, out_ref, send_sems, recv_sems):
        x, y, c = lax.axis_index("x"), lax.axis_index("y"), lax.axis_index("c")
        chips = [(1 - x, y), (x, 1 - y), (1 - x, 1 - y)]
        send = [pltpu.make_async_remote_copy(
            src_ref=p_ref.at[2 * cx + cy], dst_ref=out_ref.at[j], send_sem=send_sems.at[j],
            recv_sem=recv_sems.at[j], device_id=(cx, cy, c), device_id_type=_MESH)
            for j, (cx, cy) in enumerate(chips)]
        for cp in send:
            cp.start()
        for cp in send:
            cp.wait_recv()
        for cp in send:
            cp.wait_send()

    return _launch_exchange(
        body, name, p, _sds((n_peers,) + p.shape[1:], p.dtype),
        [pltpu.SemaphoreType.DMA((n_peers,)), pltpu.SemaphoreType.DMA((n_peers,))],
        lambda: _sibling_and_chip_peers()[1:], CHIPS_COLLECTIVE_ID, on_sequencer)


def _reduce_scatter(g, axis, name):
    sib = _rs_sibling(g, axis, name + "_d2d", on_sequencer=True)
    p = _pair_sum(g, sib, axis, name + "_pair")
    return p, _rs_chips(p, name + "_ici", on_sequencer=True)


def _s5_tables(lam_re, lam_im, log_step, b_re, b_im, c_re, c_im):
    nd, g, p, cg = b_re.shape
    step = jnp.exp(log_step)[..., None]
    mag = jnp.exp(lam_re * step)
    a_re, a_im = mag * jnp.cos(lam_im * step), mag * jnp.sin(lam_im * step)
    den = lam_re * lam_re + lam_im * lam_im
    num_re, num_im = a_re - 1.0, a_im
    k_re = (num_re * lam_re + num_im * lam_im) / den
    k_im = (num_im * lam_re - num_re * lam_im) / den
    bb_re = k_re[..., None] * b_re - k_im[..., None] * b_im
    bb_im = k_re[..., None] * b_im + k_im[..., None] * b_re
    gt = g // SSM_TILE_GROUPS
    hg = SSM_HALF_GROUPS
    eye = jnp.eye(SSM_TILE_GROUPS, dtype=F32).reshape(SSM_TILE_GROUPS, 2, hg)

    def pack_b(bb):
        w = jnp.einsum("djhqpc,ghq->djhgcqp", bb.reshape(nd, gt, 2, hg, p, cg), eye)
        return w.reshape(nd, gt * 2, SSM_TILE_GROUPS * cg, hg * p)

    def pack_c(cc):
        w = jnp.einsum("djhqcp,ghq->djhqpgc", cc.reshape(nd, gt, 2, hg, cg, p), eye)
        return w.reshape(nd, gt * 2, hg * p, SSM_TILE_GROUPS * cg)

    a = jnp.stack([a_re.reshape(nd, gt * 2, hg * p), a_im.reshape(nd, gt * 2, hg * p)], axis=2)
    w = jnp.concatenate([pack_b(bb_re), pack_b(bb_im)], axis=-1)
    c = jnp.concatenate([pack_c(c_re), -pack_c(c_im)], axis=-2)
    return w, c, a


def _ret_tables(decay_logit, dk, dv):
    ch = RET_CHUNK
    nd, h = decay_logit.shape
    lg = jax.nn.log_sigmoid(decay_logit)[:, :, None]
    pos = jnp.arange(ch, dtype=F32)
    fwd_diff = pos[:, None] - pos[None, :]
    diff = jnp.stack([fwd_diff, -fwd_diff])[:, None]
    mask = jnp.stack([fwd_diff >= 0, -fwd_diff > 0])[:, None]
    end_pos = jnp.stack([ch - 1.0 - pos, pos])[:, None]
    in_pos = jnp.stack([pos + 1.0, ch - pos])[:, None]
    w_end = jnp.exp(lg * end_pos)
    w_in = jnp.exp(lg * in_pos)
    decay = jnp.where(mask, jnp.exp(lg[..., None] * jnp.where(mask, diff, 0.0)), 0.0)
    g_chunk = jnp.exp(lg[..., 0] * ch)
    return (decay, jnp.broadcast_to(w_end[..., None], (nd, h, ch, dk)), jnp.broadcast_to(w_in[..., None], (nd, h, ch, dk)),
            jnp.broadcast_to(g_chunk[..., None, None], (nd, h, dk, dv)))


def _rope_tables(t_rows, ncc, dk):
    quarter = dk // 4
    idx = np.arange(t_rows)
    row, col = idx // GRID_W, idx % GRID_W
    inv = ROPE_BASE ** (-np.arange(quarter, dtype=np.float32) / quarter)
    ang_r = row.astype(np.float32)[:, None] * inv
    ang_c = col.astype(np.float32)[:, None] * inv
    ang_r, ang_c = jnp.asarray(ang_r, F32), jnp.asarray(ang_c, F32)
    cos = jnp.concatenate([jnp.cos(ang_r), jnp.cos(ang_r), jnp.cos(ang_c), jnp.cos(ang_c)], axis=1)
    sin = jnp.concatenate([-jnp.sin(ang_r), jnp.sin(ang_r), -jnp.sin(ang_c), jnp.sin(ang_c)], axis=1)
    n_ctx = ncc * RET_CHUNK
    cos = jnp.concatenate([jnp.ones((n_ctx, dk), F32), cos], axis=0)
    sin = jnp.concatenate([jnp.zeros((n_ctx, dk), F32), sin], axis=0)
    return cos, sin


def _to_scan_layout(ctx_rows, lat_rows, rev):
    u = jnp.concatenate([lat_rows, ctx_rows] if rev else [ctx_rows, lat_rows], axis=0)
    r, w = u.shape
    return u.reshape(N_SEG, r // N_SEG, w).transpose(1, 0, 2).reshape(r, w)


def _from_scan_layout(yp, n_ctx, rev):
    r, w = yp.shape
    y = yp.reshape(r // N_SEG, N_SEG, w).transpose(1, 0, 2).reshape(r, w)
    return (y[r - n_ctx:], y[:r - n_ctx]) if rev else (y[:n_ctx], y[n_ctx:])


def _pack(parts, width):
    rows = []
    for p in parts:
        flat = p.reshape(-1).astype(F32)
        n = flat.shape[0]
        rows.append(jnp.pad(flat, (0, -n % (SUBLANE * width))).reshape(-1, width))
    return jnp.concatenate(rows, axis=0)


def _packed_rows(n, width):
    return -(-n // (SUBLANE * width)) * SUBLANE


def _unpack(flat2d, shapes):
    width = flat2d.shape[1]
    out, row = [], 0
    for shp in shapes:
        n = int(np.prod(shp))
        nr = _packed_rows(n, width)
        out.append(flat2d[row:row + nr].reshape(-1)[:n].reshape(shp))
        row += nr
    return out


def kernel(x, c, ctx, c_ctx, ada_w, ada_b, norm_g, ffn_w_in, ffn_w_out, mix_w_in, ssm_lam_re, ssm_lam_im, ssm_log_step, ssm_b_re, ssm_b_im, ssm_c_re, ssm_c_im, ssm_d, ssm_glu_w, ret_decay_logit, ret_w_proj, mix_w_out, loss_target, m_c_ctx, m_ada_w, m_ada_b, m_norm_g, m_ffn_w_in, m_ffn_w_out, m_mix_w_in, m_ssm_lam_re, m_ssm_lam_im, m_ssm_log_step, m_ssm_b_re, m_ssm_b_im, m_ssm_c_re, m_ssm_c_im, m_ssm_d, m_ssm_glu_w, m_ret_decay_logit, m_ret_w_proj, m_mix_w_out, v_c_ctx, v_ada_w, v_ada_b, v_norm_g, v_ffn_w_in, v_ffn_w_out, v_mix_w_in, v_ssm_lam_re, v_ssm_lam_im, v_ssm_log_step, v_ssm_b_re, v_ssm_b_im, v_ssm_c_re, v_ssm_c_im, v_ssm_d, v_ssm_glu_w, v_ret_decay_logit, v_ret_w_proj, v_mix_w_out):
    t_rows, d = x.shape[1], x.shape[2]
    n_ctx = ctx.shape[1]
    r = n_ctx + t_rows
    ssm_w = ssm_d.shape[1]
    heads = ret_decay_logit.shape[2]
    mi = mix_w_in.shape[2] * N_DEV
    dk = (mi - ssm_w - 2 * d) // (6 * heads)
    dv = 2 * dk
    qk_w, v_w = heads * dk, heads * dv
    q_off = ssm_w
    ncc = n_ctx // RET_CHUNK
    tile = n_ctx
    nct = 1
    wide_tile = _tile(n_ctx, 128, 16)
    assert r % (N_SEG * SUBLANE) == 0 and n_ctx % RET_CHUNK == 0 and t_rows % tile == 0
    me = 4 * lax.axis_index("x") + 2 * lax.axis_index("y") + lax.axis_index("c")
    g_off = ssm_w + 2 * qk_w + v_w
    gs_off = g_off + v_w

    ng_cols = norm_g.shape[2]
    small0 = _pack([c[0], norm_g[0]], d)
    small0_all = _all_gather(small0, 0, "ag_cond")

    bf = lambda w: w.astype(BF16)
    small0_all, sh_in1 = lax.optimization_barrier((small0_all, bf(ffn_w_in[0, 0])))
    small0_all = small0_all.reshape(N_DEV, -1)
    w_in1 = _all_gather(sh_in1, 1, "ag_ffn1_in", on_sequencer=True)
    w_glu = _all_gather(bf(ssm_glu_w[0]), 1, "ag_glu", on_sequencer=True)
    w_rp = _all_gather(bf(ret_w_proj[0]), 0, "ag_ret_proj", on_sequencer=True)
    w_mo = _all_gather(bf(mix_w_out[0]), 0, "ag_mix_out", on_sequencer=True)
    w_in2 = _all_gather(bf(ffn_w_in[0, 1]), 1, "ag_ffn2_in", on_sequencer=True)
    w_out2 = _all_gather(bf(ffn_w_out[0, 1]), 0, "ag_ffn2_out", on_sequencer=True)

    ng_at = _packed_rows(d, d) * d
    c_all = small0_all[:, :d]
    g_full = small0_all[:, ng_at:ng_at + 6 * ng_cols].reshape(N_DEV, 6, ng_cols).transpose(1, 0, 2).reshape(6, d)
    g6 = g_full.reshape(6, 1, d)
    cc = jnp.concatenate([c_all, c_ctx[None, :], jnp.zeros((2 * SUBLANE - N_DEV - 1, d), F32)], axis=0)
    sc = _silu_rows(cc, "ada_silu")
    na = ada_w.shape[2]
    a_loc = _mm(sc, ada_w[0], "nn", F32, "ada_fwd", tm=16, tn=na, tk=512)
    a_all = _all_gather(a_loc, 0, "ag_ada")
    a_all, sh_out1, sh_mix = lax.optimization_barrier((a_all, bf(ffn_w_out[0, 0]), bf(mix_w_in[0])))
    a_all = a_all.reshape(N_DEV, 2 * SUBLANE, na)
    w_out1 = _all_gather(sh_out1, 0, "ag_ffn1_out", on_sequencer=True)
    w_mix = _all_gather(sh_mix, 1, "ag_mix_in", on_sequencer=True)
    ada_x = lax.dynamic_index_in_dim(a_all, me, axis=1, keepdims=False).reshape(9 * d) + ada_b[0]
    ada_c = a_all[:, N_DEV, :].reshape(9 * d) + ada_b[0]
    mods = jnp.stack([ada_c.reshape(9, d), ada_x.reshape(9, d)]).reshape(18, 1, d)

    xin = jnp.concatenate([ctx[0], x[0]], axis=0)
    u1 = _ada_pre_fwd(xin, g6, mods, 0, 0, nct, tile, "pre1")
    g1, up1, a1 = _mm_swiglu(u1, w_in1, "ffn1_in", tm=1088)
    o1 = _mm(a1, w_out1, "nn", BF16, "ffn1_out", tm=1088, tn=1024, tk=2816)
    x1, u2 = _ada_post_fwd(xin, o1, g6, mods, 1, 0, 0.5, nct, tile, "post1_pre2", then_pre=(2, 1))
    hm = _mm(u2, w_mix, "nn", BF16, "mix_in", tm=1088, tn=1024)

    us_ctx, us_lat = hm[:n_ctx, :ssm_w], hm[n_ctx:, :ssm_w]
    dskip = ssm_d.reshape(1, 1, ssm_w)
    s5_prm = (ssm_lam_re[0], ssm_lam_im[0], ssm_log_step[0], ssm_b_re[0], ssm_b_im[0], ssm_c_re[0], ssm_c_im[0])
    s5_tabs_both, s5_vjp = jax.vjp(_s5_tables, *s5_prm)
    s5_tabs, ups, y_dirs = [], [], []
    for dr in range(2):
        tabs = tuple(t[dr] for t in s5_tabs_both)
        up = _to_scan_layout(us_ctx, us_lat, dr == 1)
        yp = _s5_fwd(up, *tabs, dr == 1, "s5_fwd%d" % dr)
        s5_tabs.append(tabs)
        ups.append(up)
        y_dirs.append(_from_scan_layout(yp, n_ctx, dr == 1)[1])
    a_ssm = _ssm_out_fwd(y_dirs[0], y_dirs[1], hm, dskip, nct, tile, "ssm_out")
    gab = _mm(a_ssm, w_glu, "nn", BF16, "glu", tm=512, tn=2048, tk=ssm_w)

    cos, sin = _rope_tables(t_rows, ncc, dk)
    ret_tabs_both, ret_vjp = jax.vjp(functools.partial(_ret_tables, dk=dk, dv=dv), ret_decay_logit[0])
    ret_tabs, o_dirs, s_ins = [], [], []
    for dr in range(2):
        tabs = tuple(t[dr] for t in ret_tabs_both)
        o_d, s_in = _ret_fwd(hm, cos, sin, *tabs, heads, dk, dv, q_off, ncc, dr == 1, "ret_fwd%d" % dr)
        ret_tabs.append(tabs)
        o_dirs.append(o_d)
        s_ins.append(s_in)
    ret_in = _ret_gate_fwd(o_dirs[0], o_dirs[1], hm, g_off, heads, dv, nct, tile, "ret_gate")
    rb = _mm(ret_in, w_rp, "nn", BF16, "ret_proj", tm=512, tn=d, tk=v_w)
    merged = _merge_fwd(gab, rb, hm, gs_off, nct, tile, "merge")
    mix = _mm(merged, w_mo, "nn", BF16, "mix_out", tm=512, tn=d, tk=d)
    x2, u3 = _ada_post_fwd(x1, mix, g6, mods, 3, 1, 1.0, 0, tile, "post2_pre3", h_tile_offset=nct, then_pre=(4, 2))
    g3, up3, a3 = _mm_swiglu(u3, w_in2, "ffn2_in", tm=1024)
    o3 = _mm(a3, w_out2, "nn", BF16, "ffn2_out", tm=1024, tn=1024, tk=2816)
    dy, lcols = _ada_post_fwd(x2, o3, g6, mods, 5, 2, 0.5, 0, tile, "post3_loss", target=loss_target[0])
    loss_part = (0.5 * jnp.sum(lcols) / d).reshape(1)

    dg6 = [None] * 6
    dmod = {}

    def add_mod(sel_rows, k, val):
        for sel, row in sel_rows:
            dmod[(sel, k)] = dmod.get((sel, k), 0.0) + val[row, 0]

    both, lat = [(0, 0), (1, 1)], [(1, 0)]

    def tie(*vals):
        return lax.optimization_barrier(vals)

    def big_update(w3d, m3d, v3d, layer, gfull, axis, name, filled=None):
        p, recv = _reduce_scatter(gfull, axis, "rs_" + name)
        return _adamw_scattered(w3d, m3d, v3d, layer, p, recv, "adamw_" + name, filled)

    do3, dg6[5], dgt = _ada_post_bwd(dy, o3, g6, mods, 5, 2, 0.5, 0, 1, tile, "post3_bwd")
    add_mod(lat, 8, dgt)
    gw_out2 = _mm(a3, do3, "tn", BF16, "ffn2_out_dw", tm=1408, tn=1024, tk=2176)
    do3, gw_out2 = tie(do3, gw_out2)
    up_out2 = big_update(ffn_w_out[0], m_ffn_w_out[0], v_ffn_w_out[0], 1, gw_out2, 0, "ffn2_out")
    da3 = _mm(do3, w_out2, "nt", BF16, "ffn2_out_dx", tm=512, tn=2816, tk=d)
    dh3 = _swiglu_bwd(g3, up3, da3, wide_tile, "swiglu2_bwd")
    gw_in2 = _mm(u3, dh3, "tn", BF16, "ffn2_in_dw", tm=1024, tn=1024, tk=2176)
    dh3, gw_in2 = tie(dh3, gw_in2)
    up_in2 = big_update(ffn_w_in[0], m_ffn_w_in[0], v_ffn_w_in[0], 1, gw_in2, 1, "ffn2_in")
    du3 = _mm(dh3, w_in2, "nt", F32, "ffn2_in_dx", tm=1024, tn=1024, tk=1024)
    dx2, dg6[4], dsh, dsc = _ada_pre_bwd(x2, du3, dy, g6, mods, 4, 2, 0, 1, tile, "pre3_bwd")
    add_mod(lat, 6, dsh)
    add_mod(lat, 7, dsc)
    dmix, dg6[3], dgt = _ada_post_bwd(dx2, mix, g6, mods, 3, 1, 1.0, 0, 1, tile, "post2_bwd")
    add_mod(lat, 5, dgt)
    gw_mo = _mm(merged, dmix, "tn", BF16, "mix_out_dw", tm=1024, tn=1024, tk=2176)
    dmix, gw_mo = tie(dmix, gw_mo)
    up_mo = big_update(mix_w_out, m_mix_w_out, v_mix_w_out, 0, gw_mo, 0, "mix_out")
    dmerged = _mm(dmix, w_mo, "nt", BF16, "mix_out_dx", tm=512, tn=d, tk=d)
    dgab, drb, dgs, dgr = _merge_bwd(gab, rb, hm, gs_off, dmerged, nct, tile, "merge_bwd")
    gw_glu = _mm(a_ssm, dgab, "tn", BF16, "glu_dw", tm=1024, tn=1024, tk=2176)
    gw_rp = _mm(ret_in, drb, "tn", BF16, "ret_proj_dw", tm=1024, tn=1024, tk=2176)
    dgab, drb, gw_glu, gw_rp = tie(dgab, drb, gw_glu, gw_rp)
    up_glu = big_update(ssm_glu_w, m_ssm_glu_w, v_ssm_glu_w, 0, gw_glu, 1, "glu")
    up_rp = big_update(ret_w_proj, m_ret_w_proj, v_ret_w_proj, 0, gw_rp, 0, "ret_proj")
    da_ssm = _mm(dgab, w_glu, "nt", BF16, "glu_dx", tm=512, tn=ssm_w, tk=2 * d)
    dret_in = _mm(drb, w_rp, "nt", BF16, "ret_proj_dx", tm=512, tn=v_w, tk=d)
    d_o, dg_gate = _ret_gate_bwd(o_dirs[0], o_dirs[1], hm, g_off, dret_in, heads, dv, nct, tile, "ret_gate_bwd")
    dy_ssm, dus_direct, d_dskip = _ssm_out_bwd(y_dirs[0], y_dirs[1], hm, dskip, da_ssm, nct, tile, "ssm_out_bwd")
    s5_table_grads, du_ctx, du_lat = [], [], [dus_direct]
    for dr in range(2):
        dyp = _to_scan_layout(jnp.zeros((n_ctx, ssm_w), F32), dy_ssm, dr == 1)
        if dr == 1:
            dyp, up_out2, up_in2 = tie(dyp, up_out2, up_in2)
        outs = _s5_bwd(ups[dr], dyp, *s5_tabs[dr], dr == 1, "s5_bwd%d" % dr)
        part_ctx, part_lat = _from_scan_layout(outs[0], n_ctx, dr == 1)
        du_ctx.append(part_ctx)
        du_lat.append(part_lat)
        s5_table_grads.append(outs[1:])
    dqkv, ret_table_grads = [], []
    for dr in range(2):
        if dr == 1:
            d_o, up_mo, up_glu, up_rp = tie(d_o, up_mo, up_glu, up_rp)
        outs = _ret_bwd(hm, cos, sin, *ret_tabs[dr], s_ins[dr], d_o, heads, dk, dv, q_off, ncc, dr == 1,
                        "ret_bwd%d" % dr)
        dqkv.append(outs[:3])
        ret_table_grads.append(outs[3:])
    both_dirs = lambda grads: tuple(jnp.stack([g0, g1]) for g0, g1 in zip(*grads))
    early_parts = list(s5_vjp(both_dirs(s5_table_grads))) + list(ret_vjp(both_dirs(ret_table_grads)))
    s5_names = 7
    early_shapes = [p.shape for p in early_parts]
    early_all = _all_gather(_pack(early_parts, 1024), 0, "ag_s5_grads", on_sequencer=True)
    early_sum = _sum_leading(early_all.reshape(N_DEV, -1, 1024), "sum_s5_grads")
    dus = jnp.concatenate([du_ctx[0] + du_ctx[1], du_lat[0] + du_lat[1] + du_lat[2]], axis=0)
    dhm = _assemble_dhm(dus, dqkv[0][0], dqkv[1][0], dqkv[0][1], dqkv[1][1], dqkv[0][2], dqkv[1][2],
                        dg_gate, dgs, dgr, n_ctx // wide_tile, wide_tile, "assemble_dhm")
    gw_mix = _mm(u2, dhm, "tn", BF16, "mix_in_dw", tm=1024, tn=1024, tk=2176)
    dhm, gw_mix = tie(dhm, gw_mix)
    up_mix = big_update(mix_w_in, m_mix_w_in, v_mix_w_in, 0, gw_mix, 1, "mix_in")
    du2 = _mm(dhm, w_mix, "nt", F32, "mix_in_dx", tm=1088, tn=1024, tk=1024)
    dx1, dg6[2], dsh, dsc = _ada_pre_bwd(x1, du2, dx2, g6, mods, 2, 1, nct, 2, tile, "pre2_bwd", dres_x_only=True)
    add_mod(both, 3, dsh)
    add_mod(both, 4, dsc)
    do1, dg6[1], dgt = _ada_post_bwd(dx1, o1, g6, mods, 1, 0, 0.5, nct, 2, tile, "post1_bwd")
    add_mod(both, 2, dgt)
    gw_out1 = _mm(a1, do1, "tn", BF16, "ffn1_out_dw", tm=1408, tn=1024, tk=2176)
    do1, gw_out1 = tie(do1, gw_out1)
    up_out1 = big_update(ffn_w_out[0], m_ffn_w_out[0], v_ffn_w_out[0], 0, gw_out1, 0, "ffn1_out", filled=up_out2)
    da1 = _mm(do1, w_out1, "nt", BF16, "ffn1_out_dx", tm=544, tn=2816, tk=d)
    dh1 = _swiglu_bwd(g1, up1, da1, wide_tile, "swiglu1_bwd")
    dh1, up_mix, early_sum = tie(dh1, up_mix, early_sum)
    early_sums = _unpack(early_sum, early_shapes)
    gw_in1 = _mm(u1, dh1, "tn", BF16, "ffn1_in_dw", tm=1024, tn=1024, tk=2176)
    dh1, gw_in1 = tie(dh1, gw_in1)
    up_in1 = big_update(ffn_w_in[0], m_ffn_w_in[0], v_ffn_w_in[0], 0, gw_in1, 1, "ffn1_in", filled=up_in2)
    du1 = _mm(dh1, w_in1, "nt", F32, "ffn1_in_dx", tm=1088, tn=1024, tk=1024)
    dx_lat, dg6[0], dsh, dsc = _ada_pre_bwd(xin, du1, dx1, g6, mods, 0, 0, nct, 2, tile, "pre1_bwd",
                                            latent_dh_only=True)
    add_mod(both, 0, dsh)
    add_mod(both, 1, dsc)
    grad_x = dx_lat[None]

    zero_d = jnp.zeros((d,), F32)
    d_ada_x = jnp.stack([dmod.get((1, k), zero_d) for k in range(9)]).reshape(9 * d)
    d_ada_c = jnp.stack([dmod.get((0, k), zero_d) for k in range(9)]).reshape(9 * d)
    dg_full = jnp.stack([g[0, 0] for g in dg6])
    small_parts = [d_ada_x, d_ada_c, dg_full, d_dskip, loss_part]
    small_shapes = [p.shape for p in small_parts]
    packed = _pack(small_parts, 1024)
    gathered = _all_gather(packed, 0, "ag_small_grads").reshape(N_DEV, -1, 1024)
    summed = _sum_leading(gathered, "sum_small_grads")
    sums = _unpack(summed, small_shapes)
    sum_dx, sum_dc, sum_dg = sums[0], sums[1], sums[2]
    loss = sums[4][0]
    grad_ada_b = (sum_dx + sum_dc)[None]
    dx_rows = gathered.reshape(N_DEV, -1)[:, :9 * d]
    col0 = me * na
    da_rows = jnp.concatenate([lax.dynamic_slice_in_dim(dx_rows, col0, na, axis=1),
                               lax.dynamic_slice_in_dim(sum_dc[None], col0, na, axis=1),
                               jnp.zeros((2 * SUBLANE - N_DEV - 1, na), F32)], axis=0)
    grad_ada_w = _mm(sc, da_rows, "tn", F32, "ada_dw", tm=512, tn=na, tk=16)
    d_sc = _mm(da_rows, ada_w[0], "nt", F32, "ada_dx", tm=16, tn=512, tk=na)
    d_sc_all = _all_gather(jnp.broadcast_to(d_sc[N_DEV:N_DEV + 1], (SUBLANE, d)), 0, "ag_dctx")
    d_sc_sum = _sum_leading(d_sc_all.reshape(N_DEV, SUBLANE, d), "sum_dctx")
    grad_c_ctx = _silu_grad_rows(jnp.broadcast_to(c_ctx[None], (SUBLANE, d)), d_sc_sum, "ctx_silu_bwd")[0]
    grad_norm_g = lax.dynamic_slice_in_dim(sum_dg, me * ng_cols, ng_cols, axis=1)[None]

    upd = {}
    upd["ffn_w_in"] = [o[None] for o in up_in1]
    upd["ffn_w_out"] = [o[None] for o in up_out1]
    upd["mix_w_in"] = list(up_mix)
    upd["ssm_glu_w"] = list(up_glu)
    upd["ret_w_proj"] = list(up_rp)
    upd["mix_w_out"] = list(up_mo)
    upd["ada_w"] = [o[None] for o in _adamw(ada_w[0], m_ada_w[0], v_ada_w[0], grad_ada_w[None], "adamw_ada_w")]

    small_names = ["c_ctx", "ada_b", "norm_g", "ssm_lam_re", "ssm_lam_im", "ssm_log_step", "ssm_b_re", "ssm_b_im",
                   "ssm_c_re", "ssm_c_im", "ssm_d", "ret_decay_logit"]
    small_w = [c_ctx, ada_b, norm_g, ssm_lam_re, ssm_lam_im, ssm_log_step, ssm_b_re, ssm_b_im, ssm_c_re, ssm_c_im,
               ssm_d, ret_decay_logit]
    small_m = [m_c_ctx, m_ada_b, m_norm_g, m_ssm_lam_re, m_ssm_lam_im, m_ssm_log_step, m_ssm_b_re, m_ssm_b_im,
               m_ssm_c_re, m_ssm_c_im, m_ssm_d, m_ret_decay_logit]
    small_v = [v_c_ctx, v_ada_b, v_norm_g, v_ssm_lam_re, v_ssm_lam_im, v_ssm_log_step, v_ssm_b_re, v_ssm_b_im,
               v_ssm_c_re, v_ssm_c_im, v_ssm_d, v_ret_decay_logit]
    small_g = [grad_c_ctx, grad_ada_b, grad_norm_g] + [s[None] for s in early_sums[:s5_names]] + \
              [sums[3].reshape(ssm_d.shape), early_sums[s5_names][None]]
    shapes = [w.shape for w in small_w]
    res = _adamw(_pack(small_w, 1024), _pack(small_m, 1024), _pack(small_v, 1024), _pack(small_g, 1024)[None],
                 "adamw_small")
    small_out = [_unpack(o, shapes) for o in res]
    for i, nm in enumerate(small_names):
        upd[nm] = [small_out[kind][i] for kind in range(4)]

    order = ["c_ctx", "ada_w", "ada_b", "norm_g", "ffn_w_in", "ffn_w_out", "mix_w_in", "ssm_lam_re", "ssm_lam_im",
             "ssm_log_step", "ssm_b_re", "ssm_b_im", "ssm_c_re", "ssm_c_im", "ssm_d", "ssm_glu_w", "ret_decay_logit",
             "ret_w_proj", "mix_w_out"]
    outs = [loss, grad_x]
    for kind in range(4):
        outs += [upd[nm][kind] for nm in order]
    return tuple(outs)
```
